```python
import jax, jax.numpy as jnp
from jax import lax
import numpy as np

D_MODEL = 1024
BATCH = 8
SEQ = 2048
DEPTH = 1
DEC_BATCH = 128
DEC_SEQ = 1
PAST_LEN = 16384
PAGE_SIZE = 128

MIX_WIDTH = D_MODEL
GDN_HEADS = 4
GDN_DV = (MIX_WIDTH // 2) // GDN_HEADS
GDN_DK = GDN_DV
GDN_WIDTH = GDN_HEADS * GDN_DV
ML_HEADS = 4
ML_WIDTH = MIX_WIDTH - GDN_WIDTH
ML_DV = ML_WIDTH // ML_HEADS
ML_DK = ML_DV // 2
CONV_W = 4
CONV_CH = 2 * GDN_HEADS * GDN_DK + GDN_WIDTH
CHUNK = 64
D_FF = 4 * D_MODEL
EPS = 1e-6
IN_SPLITS = (GDN_HEADS * GDN_DK, GDN_HEADS * GDN_DK, GDN_WIDTH, GDN_WIDTH, GDN_HEADS, GDN_HEADS,
             ML_HEADS * ML_DK, ML_HEADS * ML_DK, ML_WIDTH, ML_WIDTH, ML_HEADS, ML_HEADS)
IN_COLS = sum(IN_SPLITS)

kernel_name = 'hybrid_gdn_mlstm_decoder_step'


def _rmsnorm(x, g):
    xf = x.astype(jnp.float32)
    y = xf * lax.rsqrt(jnp.mean(xf * xf, axis=-1, keepdims=True) + EPS)
    return (y * g.astype(jnp.float32)).astype(x.dtype)


def _l2norm(x):
    return x * lax.rsqrt(jnp.sum(x * x, axis=-1, keepdims=True) + EPS)


def _split(x, sizes):
    offs = np.cumsum(sizes)[:-1].tolist()
    return jnp.split(x, offs, axis=-1)


def _causal_conv(x, buf, w):
    T = x.shape[1]
    xp = jnp.concatenate([buf, x], axis=1)
    y = xp[:, 0:T] * w[0]
    for j in range(1, CONV_W):
        y = y + xp[:, j:j + T] * w[j]
    return jax.nn.silu(y), xp[:, -(CONV_W - 1):]


def _to_chunks(x, chunk):
    B, T = x.shape[:2]
    x = x.reshape((B, T // chunk, chunk) + x.shape[2:])
    x = jnp.moveaxis(x, 2, 3)
    return jnp.moveaxis(x, 1, 0)


def _from_chunks(o):
    N, B, H, C, E = o.shape
    return o.transpose(1, 0, 3, 2, 4).reshape(B, N * C, H, E)


def _gated_delta_rule(q, k, v, g, beta, S0, chunk):
    qc, kc, vc, gc, bc = (_to_chunks(t, chunk) for t in (q, k, v, g, beta))
    G = jnp.cumsum(gc, axis=-1)
    causal = jnp.tril(jnp.ones((chunk, chunk), dtype=bool))
    strict = jnp.tril(jnp.ones((chunk, chunk), dtype=bool), -1)
    decay = jnp.exp(jnp.where(causal, G[..., :, None] - G[..., None, :], -jnp.inf))
    kk = jnp.einsum('nbhcd,nbhsd->nbhcs', kc, kc)
    A = jnp.where(strict, bc[..., None] * kk * decay, 0.0) + jnp.eye(chunk, dtype=jnp.float32)
    w_v = lax.linalg.triangular_solve(A, bc[..., None] * vc, left_side=True, lower=True, unit_diagonal=True)
    w_k = lax.linalg.triangular_solve(A, (bc * jnp.exp(G))[..., None] * kc, left_side=True, lower=True,
                                      unit_diagonal=True)
    qk = jnp.einsum('nbhcd,nbhsd->nbhcs', qc, kc) * decay
    k_end = kc * jnp.exp(G[..., -1:] - G)[..., None]
    g_end = jnp.exp(G[..., -1])
    eG = jnp.exp(G)

    def step(S, xs):
        q_, wv_, wk_, qk_, ke_, ge_, eg_ = xs
        U = wv_ - jnp.einsum('bhcd,bhde->bhce', wk_, S)
        o = eg_[..., None] * jnp.einsum('bhcd,bhde->bhce', q_, S) + jnp.einsum('bhcs,bhse->bhce', qk_, U)
        S = ge_[..., None, None] * S + jnp.einsum('bhcd,bhce->bhde', ke_, U)
        return S, o

    S, o = lax.scan(step, S0, (qc, w_v, w_k, qk, k_end, g_end, eG))
    return _from_chunks(o), S


def _mlstm(q, k, v, ig, lf, C0, n0, m0, chunk):
    qc, kc, vc, ic, fc = (_to_chunks(t, chunk) for t in (q, k, v, ig, lf))
    F = jnp.cumsum(fc, axis=-1)
    causal = jnp.tril(jnp.ones((chunk, chunk), dtype=bool))
    D = jnp.where(causal, F[..., :, None] - F[..., None, :] + ic[..., None, :], -jnp.inf)
    D_max = jnp.max(D, axis=-1)
    qk = jnp.einsum('nbhcd,nbhsd->nbhcs', qc, kc)

    def step(carry, xs):
        C, n, m = carry
        q_, k_, v_, F_, D_, Dm_, qk_ = xs
        b = F_ + m[..., None]
        mt = jnp.maximum(b, Dm_)
        w_prev = jnp.exp(b - mt)
        P = jnp.exp(D_ - mt[..., None])
        Pqk = P * qk_
        num = w_prev[..., None] * jnp.einsum('bhcd,bhde->bhce', q_, C) + jnp.einsum('bhcs,bhse->bhce', Pqk, v_)
        den = w_prev * jnp.einsum('bhcd,bhd->bhc', q_, n) + jnp.sum(Pqk, axis=-1)
        h = num / jnp.maximum(jnp.abs(den), jnp.exp(-mt))[..., None]
        p_end = P[..., -1, :]
        C = w_prev[..., -1, None, None] * C + jnp.einsum('bhs,bhsd,bhse->bhde', p_end, k_, v_)
        n = w_prev[..., -1, None] * n + jnp.einsum('bhs,bhsd->bhd', p_end, k_)
        return (C, n, mt[..., -1]), h

    (C, n, m), h = lax.scan(step, (C0, n0, m0), (qc, kc, vc, F, D, D_max, qk))
    return _from_chunks(h), C, n, m


def _mixer(h, conv_buf, S0, C0, n0, m0, w_in, conv_w, a_log, dt_bias, gdn_norm_g, b_igate, b_fgate,
           mlstm_norm_g, w_out):
    f32 = jnp.float32
    Bsz, T, _ = h.shape
    chunk = CHUNK if T % CHUNK == 0 else T
    (gq, gk, gv, gz, gb, ga, mq, mk, mv, mo, mi, mf) = _split(h @ w_in, IN_SPLITS)
    qkv, conv_new = _causal_conv(jnp.concatenate([gq, gk, gv], axis=-1), conv_buf.astype(h.dtype), conv_w)
    gq, gk, gv = _split(qkv.astype(f32), (GDN_HEADS * GDN_DK, GDN_HEADS * GDN_DK, GDN_WIDTH))
    gq = _l2norm(gq.reshape(Bsz, T, GDN_HEADS, GDN_DK)) * (GDN_DK ** -0.5)
    gk = _l2norm(gk.reshape(Bsz, T, GDN_HEADS, GDN_DK))
    gv = gv.reshape(Bsz, T, GDN_HEADS, GDN_DV)
    beta = jax.nn.sigmoid(gb.astype(f32))
    g = -jnp.exp(a_log.astype(f32)) * jax.nn.softplus(ga.astype(f32) + dt_bias.astype(f32))
    o_gdn, S_new = _gated_delta_rule(gq, gk, gv, g, beta, S0.astype(f32), chunk)
    o_gdn = _rmsnorm(o_gdn, gdn_norm_g) * jax.nn.silu(gz.astype(f32).reshape(Bsz, T, GDN_HEADS, GDN_DV))
    mq = mq.astype(f32).reshape(Bsz, T, ML_HEADS, ML_DK)
    mk = mk.astype(f32).reshape(Bsz, T, ML_HEADS, ML_DK) * (ML_DK ** -0.5)
    mv = mv.astype(f32).reshape(Bsz, T, ML_HEADS, ML_DV)
    ig = mi.astype(f32) + b_igate.astype(f32)
    lf = jax.nn.log_sigmoid(mf.astype(f32) + b_fgate.astype(f32))
    h_ml, C_new, n_new, m_new = _mlstm(mq, mk, mv, ig, lf, C0.astype(f32), n0.astype(f32), m0.astype(f32), chunk)
    h_ml = jax.nn.sigmoid(mo.astype(f32)).reshape(Bsz, T, ML_HEADS, ML_DV) * _rmsnorm(h_ml, mlstm_norm_g)
    o = jnp.concatenate([o_gdn.reshape(Bsz, T, GDN_WIDTH), h_ml.reshape(Bsz, T, ML_WIDTH)], axis=-1)
    return o.astype(h.dtype) @ w_out, (conv_new, S_new, C_new, n_new, m_new)


def _layer(x, states, params):
    (norm_pre_mix, w_in, conv_w, a_log, dt_bias, gdn_norm_g, b_igate, b_fgate, mlstm_norm_g, w_out,
     norm_post_mix, norm_pre_mlp, w_up, w_down, norm_post_mlp) = params
    conv_buf, S0, C0, n0, m0 = states
    h = _rmsnorm(x, norm_pre_mix)
    mix, new_states = _mixer(h, conv_buf, S0, C0, n0, m0, w_in, conv_w, a_log, dt_bias, gdn_norm_g,
                             b_igate, b_fgate, mlstm_norm_g, w_out)
    x = x + _rmsnorm(mix, norm_post_mix)
    u = jnp.square(jax.nn.relu(_rmsnorm(x, norm_pre_mlp) @ w_up))
    x = x + _rmsnorm(u @ w_down, norm_post_mlp)
    return x, new_states


def _trunk(x, states, params):
    new = [[] for _ in states]
    for l in range(DEPTH):
        x, st = _layer(x, tuple(s[l] for s in states), tuple(p[l] for p in params))
        for lst, s in zip(new, st):
            lst.append(s)
    return x, tuple(jnp.stack(lst) for lst in new)


def setup_inputs(seed: int = 0) -> dict:
    key = jax.random.key(seed)
    ks = jax.random.split(key, 24)
    nrm = jax.random.normal
    f32 = jnp.float32
    dt = jnp.exp(jax.random.uniform(ks[10], (DEPTH, GDN_HEADS), f32) * (np.log(0.1) - np.log(0.001)) + np.log(0.001))
    return {
        'x_prompt': nrm(ks[0], (BATCH, SEQ, D_MODEL), f32),
        'x_sample': nrm(ks[1], (DEC_BATCH, DEC_SEQ, D_MODEL), f32),
        'state_gdn_conv': nrm(ks[2], (DEPTH, DEC_BATCH, CONV_W - 1, CONV_CH), f32),
        'state_gdn_S': 0.05 * nrm(ks[3], (DEPTH, DEC_BATCH, GDN_HEADS, GDN_DK, GDN_DV), f32),
        'state_mlstm_C': 0.05 * nrm(ks[4], (DEPTH, DEC_BATCH, ML_HEADS, ML_DK, ML_DV), f32),
        'state_mlstm_n': 0.05 * nrm(ks[5], (DEPTH, DEC_BATCH, ML_HEADS, ML_DK), f32),
        'state_mlstm_m': nrm(ks[6], (DEPTH, DEC_BATCH, ML_HEADS), f32),
        'norm_pre_mix': 1.0 + 0.02 * nrm(ks[7], (DEPTH, D_MODEL), f32),
        'w_in': nrm(ks[8], (DEPTH, D_MODEL, IN_COLS), f32) * D_MODEL ** -0.5,
        'conv_w': 0.5 * nrm(ks[9], (DEPTH, CONV_W, CONV_CH), f32),
        'a_log': jnp.log(jax.random.uniform(ks[11], (DEPTH, GDN_HEADS), f32, 1.0, 16.0)),
        'dt_bias': dt + jnp.log(-jnp.expm1(-dt)),
        'gdn_norm_g': 1.0 + 0.02 * nrm(ks[12], (DEPTH, GDN_DV), f32),
        'b_igate': 0.1 * nrm(ks[13], (DEPTH, ML_HEADS), f32),
        'b_fgate': jnp.linspace(3.0, 6.0, ML_HEADS, dtype=f32)[None, :] + 0.1 * nrm(ks[14], (DEPTH, ML_HEADS), f32),
        'mlstm_norm_g': 1.0 + 0.02 * nrm(ks[15], (DEPTH, ML_DV), f32),
        'w_out': nrm(ks[16], (DEPTH, MIX_WIDTH, D_MODEL), f32) * MIX_WIDTH ** -0.5,
        'norm_post_mix': 1.0 + 0.02 * nrm(ks[17], (DEPTH, D_MODEL), f32),
        'norm_pre_mlp': 1.0 + 0.02 * nrm(ks[18], (DEPTH, D_MODEL), f32),
        'w_up': nrm(ks[19], (DEPTH, D_MODEL, D_FF), f32) * D_MODEL ** -0.5,
        'w_down': nrm(ks[20], (DEPTH, D_FF, D_MODEL), f32) * D_FF ** -0.5,
        'norm_post_mlp': 1.0 + 0.02 * nrm(ks[21], (DEPTH, D_MODEL), f32),
    }


def reference(x_prompt, x_sample, state_gdn_conv, state_gdn_S, state_mlstm_C, state_mlstm_n, state_mlstm_m,
              norm_pre_mix, w_in, conv_w, a_log, dt_bias, gdn_norm_g, b_igate, b_fgate, mlstm_norm_g, w_out,
              norm_post_mix, norm_pre_mlp, w_up, w_down, norm_post_mlp):
    params = (norm_pre_mix, w_in, conv_w, a_log, dt_bias, gdn_norm_g, b_igate, b_fgate, mlstm_norm_g, w_out,
              norm_post_mix, norm_pre_mlp, w_up, w_down, norm_post_mlp)
    B = x_prompt.shape[0]
    f32 = jnp.float32
    zero_states = (jnp.zeros((DEPTH, B, CONV_W - 1, CONV_CH), x_prompt.dtype),
                   jnp.zeros((DEPTH, B, GDN_HEADS, GDN_DK, GDN_DV), f32),
                   jnp.zeros((DEPTH, B, ML_HEADS, ML_DK, ML_DV), f32),
                   jnp.zeros((DEPTH, B, ML_HEADS, ML_DK), f32),
                   jnp.zeros((DEPTH, B, ML_HEADS), f32))
    y_prompt, (p_conv, p_S, p_C, p_n, p_m) = _trunk(x_prompt, zero_states, params)
    sample_states = (state_gdn_conv, state_gdn_S, state_mlstm_C, state_mlstm_n, state_mlstm_m)
    y_sample, (s_conv, s_S, s_C, s_n, s_m) = _trunk(x_sample, sample_states, params)
    return (y_prompt, y_sample, p_conv, p_S, p_C, p_n, p_m, s_conv, s_S, s_C, s_n, s_m)
```

```python
import functools

import jax
import jax.numpy as jnp
from jax import lax
from jax.experimental import pallas as pl
from jax.experimental.pallas import tpu as pltpu

F32 = jnp.float32
BF16 = jnp.bfloat16
EPS = 1e-6

D_MODEL = 1024
HEADS = 4
GDN_D = 128
ML_DK = 64
ML_DV = 128
CONV_W = 4
CONV_CH = 3 * HEADS * GDN_D
D_FF = 4 * D_MODEL
CHUNK = 64

COL_QKV = 0
COL_Z = 1536
COL_MV = 2048
COL_MO = 2560
COL_MQ = 3072
COL_MK = 3328
COL_GATE = 3584
N_PROJ = COL_GATE + 128
LANE_GB, LANE_GA, LANE_MI, LANE_MF = 0, 4, 8, 12

VMEM_LIMIT = 56 * 1024 * 1024


def _rms(x, g):
    return x * lax.rsqrt(jnp.mean(x * x, axis=-1, keepdims=True) + EPS) * g


def _softplus(x):
    return jnp.maximum(x, 0.0) + jnp.log1p(jnp.exp(-jnp.abs(x)))


def _sigmoid(x):
    return 1.0 / (1.0 + jnp.exp(-x))


def _mm(a, b):
    return jnp.dot(a.astype(BF16), b.astype(BF16), preferred_element_type=F32)


def _mm_nt(a, b):
    return lax.dot_general(a.astype(BF16), b.astype(BF16), (((1,), (1,)), ((), ())),
                           preferred_element_type=F32)


def _in_proj_kernel(x_ref, g_ref, w_ref, o_ref, h_ref, *, n_chunk):
    h_ref[...] = _rms(x_ref[...], g_ref[...]).astype(BF16)
    n = o_ref.shape[1]
    for c0 in range(0, n, n_chunk):
        c1 = min(c0 + n_chunk, n)
        o_ref[:, c0:c1] = jnp.dot(h_ref[...], w_ref[:, c0:c1], preferred_element_type=F32)


def _in_proj(x2d, g, w, tm):
    m, k = x2d.shape
    n = w.shape[1]
    return pl.pallas_call(
        functools.partial(_in_proj_kernel, n_chunk=512),
        grid=(m // tm,),
        in_specs=[pl.BlockSpec((tm, k), lambda i: (i, 0)),
                  pl.BlockSpec((1, k), lambda i: (0, 0)),
                  pl.BlockSpec((k, n), lambda i: (0, 0))],
        out_specs=pl.BlockSpec((tm, n), lambda i: (i, 0)),
        out_shape=jax.ShapeDtypeStruct((m, n), F32),
        scratch_shapes=[pltpu.VMEM((tm, k), BF16)],
        compiler_params=pltpu.CompilerParams(dimension_semantics=("arbitrary",),
                                             vmem_limit_bytes=VMEM_LIMIT),
        name="in_proj",
    )(x2d, g, w)


def _out_mlp_kernel(og_ref, om_ref, x_ref, wo_ref, g1_ref, g2_ref, wu_ref, wd_ref, g3_ref,
                    y_ref, hn_ref, acc_ref, *, ff_chunk):
    half = og_ref.shape[1]
    mix = (jnp.dot(og_ref[...], wo_ref[0:half, :], preferred_element_type=F32)
           + jnp.dot(om_ref[...], wo_ref[half:2 * half, :], preferred_element_type=F32))
    x1 = x_ref[...] + _rms(mix, g1_ref[...])
    y_ref[...] = x1
    hn_ref[...] = _rms(x1, g2_ref[...]).astype(BF16)
    d_ff = wu_ref.shape[1]
    for c0 in range(0, d_ff, ff_chunk):
        u = jnp.dot(hn_ref[...], wu_ref[:, c0:c0 + ff_chunk], preferred_element_type=F32)
        u = jnp.square(jnp.maximum(u, 0.0)).astype(BF16)
        d = jnp.dot(u, wd_ref[c0:c0 + ff_chunk, :], preferred_element_type=F32)
        if c0 == 0:
            acc_ref[...] = d
        else:
            acc_ref[...] += d
    y_ref[...] = y_ref[...] + _rms(acc_ref[...], g3_ref[...])


def _out_mlp(og, om, x2d, wo, g1, g2, wu, wd, g3, tm):
    m, d = x2d.shape
    half = og.shape[1]
    d_ff = wu.shape[1]
    const = lambda i: (0, 0)
    return pl.pallas_call(
        functools.partial(_out_mlp_kernel, ff_chunk=1024),
        grid=(m // tm,),
        in_specs=[pl.BlockSpec((tm, half), lambda i: (i, 0)),
                  pl.BlockSpec((tm, half), lambda i: (i, 0)),
                  pl.BlockSpec((tm, d), lambda i: (i, 0)),
                  pl.BlockSpec((d, d), const, pipeline_mode=pl.Buffered(1)),
                  pl.BlockSpec((1, d), const),
                  pl.BlockSpec((1, d), const),
                  pl.BlockSpec((d, d_ff), const, pipeline_mode=pl.Buffered(1)),
                  pl.BlockSpec((d_ff, d), const, pipeline_mode=pl.Buffered(1)),
                  pl.BlockSpec((1, d), const)],
        out_specs=pl.BlockSpec((tm, d), lambda i: (i, 0)),
        out_shape=jax.ShapeDtypeStruct((m, d), F32),
        scratch_shapes=[pltpu.VMEM((tm, d), BF16), pltpu.VMEM((tm, d), F32)],
        compiler_params=pltpu.CompilerParams(dimension_semantics=("arbitrary",),
                                             vmem_limit_bytes=VMEM_LIMIT),
        name="out_mlp",
    )(og, om, x2d, wo, g1, g2, wu, wd, g3)


def _chunk_masks():
    ii = lax.broadcasted_iota(jnp.int32, (CHUNK, CHUNK), 0)
    jj = lax.broadcasted_iota(jnp.int32, (CHUNK, CHUNK), 1)
    return ii, jj


def _cumsum_col_row(x_col, x_row, ii, jj):
    c_col = jnp.sum(jnp.where(jj <= ii, x_row, 0.0), axis=1, keepdims=True)
    c_row = jnp.sum(jnp.where(ii <= jj, x_col, 0.0), axis=0, keepdims=True)
    return c_col, c_row


def _gdn_kernel(qkvz_ref, gcol_ref, grow_ref, cs_ref, s0_ref, cw_ref, plane_ref, psub_ref, gn_ref,
                o_ref, sout_ref, xp_ref, yc_ref, s_ref, *, tb):
    t = pl.program_id(1)
    nt = pl.num_programs(1)

    @pl.when(t == 0)
    def _():
        xp_ref[0:8, :] = cs_ref[...]
        s_ref[...] = s0_ref[...]

    @pl.when(t > 0)
    def _():
        xp_ref[0:8, :] = xp_ref[tb:tb + 8, :]

    xp_ref[8:tb + 8, :] = qkvz_ref[:, COL_QKV:COL_QKV + CONV_CH]

    def conv_tile(ct, carry):
        c0 = pl.multiple_of(ct * 128, 128)
        w = cw_ref[:, pl.ds(c0, 128)]
        for r0 in range(0, tb, 128):
            acc = xp_ref[5 + r0:5 + r0 + 128, pl.ds(c0, 128)] * w[0:1, :]
            for j in range(1, CONV_W):
                acc = acc + xp_ref[5 + j + r0:5 + j + r0 + 128, pl.ds(c0, 128)] * w[j:j + 1, :]
            yc_ref[r0:r0 + 128, pl.ds(c0, 128)] = acc * _sigmoid(acc)
        return carry

    lax.fori_loop(0, CONV_CH // 128, conv_tile, 0)

    ii, jj = _chunk_masks()
    incl = jj <= ii
    strict = jj < ii
    eye = (ii == jj).astype(F32)
    neg_a_lane = -jnp.exp(plane_ref[0:1, :])
    dtb_lane = plane_ref[1:2, :]
    neg_a_sub = -jnp.exp(psub_ref[:, 0:1])
    dtb_sub = psub_ref[:, 1:2]
    gn = gn_ref[...]

    def chunk_body(c, carry):
        r0 = pl.multiple_of(c * CHUNK, CHUNK)
        gc = gcol_ref[pl.ds(r0, CHUNK), :]
        gr = grow_ref[c]
        beta_t = _sigmoid(gc)
        g_t = neg_a_lane * _softplus(gc + dtb_lane)
        g_r = neg_a_sub * _softplus(gr + dtb_sub)
        for h in range(HEADS):
            lo = h * GDN_D
            q = yc_ref[pl.ds(r0, CHUNK), lo:lo + GDN_D]
            k = yc_ref[pl.ds(r0, CHUNK), 512 + lo:512 + lo + GDN_D]
            v = yc_ref[pl.ds(r0, CHUNK), 1024 + lo:1024 + lo + GDN_D]
            q = q * lax.rsqrt(jnp.sum(q * q, axis=-1, keepdims=True) + EPS) * (GDN_D ** -0.5)
            k = k * lax.rsqrt(jnp.sum(k * k, axis=-1, keepdims=True) + EPS)
            beta = beta_t[:, LANE_GB + h:LANE_GB + h + 1]
            g_col, g_row = _cumsum_col_row(g_t[:, LANE_GA + h:LANE_GA + h + 1],
                                           g_r[LANE_GA + h:LANE_GA + h + 1, :], ii, jj)
            decay = jnp.where(incl, jnp.exp(jnp.where(incl, g_col - g_row, 0.0)), 0.0)
            kk = _mm_nt(k, k)
            n_mat = jnp.where(strict, beta * kk * decay, 0.0)
            x_inv = eye - n_mat
            p_pow = -n_mat
            for _ in range(5):
                p_pow = _mm(p_pow, p_pow)
                x_inv = x_inv + _mm(x_inv, p_pow)
            e_g = jnp.exp(g_col)
            rhs = jnp.concatenate([beta * v, (beta * e_g) * k], axis=1)
            w_all = _mm(x_inv, rhs)
            w_v = w_all[:, 0:GDN_D]
            w_k = w_all[:, GDN_D:2 * GDN_D]
            qk = _mm_nt(q, k) * decay
            g_end = g_col[CHUNK - 1:CHUNK, :]
            k_end = k * jnp.exp(g_end - g_col)
            s_h = s_ref[h]
            r = _mm(jnp.concatenate([w_k, q], axis=0), s_h)
            u = w_v - r[0:CHUNK]
            o = e_g * r[CHUNK:2 * CHUNK] + _mm(qk, u)
            s_ref[h] = jnp.exp(g_end) * s_h + _mm(k_end.T, u)
            z = qkvz_ref[pl.ds(r0, CHUNK), COL_Z + lo:COL_Z + lo + GDN_D]
            out = _rms(o, gn) * (z * _sigmoid(z))
            o_ref[pl.ds(r0, CHUNK), lo:lo + GDN_D] = out.astype(o_ref.dtype)
        return carry

    lax.fori_loop(0, tb // CHUNK, chunk_body, 0)

    @pl.when(t == nt - 1)
    def _():
        sout_ref[...] = s_ref[...]


def _gdn_prompt(proj, grow, cs8, s0, conv_w, plane, psub, gn, tb):
    b, t, _ = proj.shape
    ncb = tb // CHUNK
    return pl.pallas_call(
        functools.partial(_gdn_kernel, tb=tb),
        grid=(b, t // tb),
        in_specs=[pl.BlockSpec((None, tb, 2048), lambda i, j: (i, j, 0)),
                  pl.BlockSpec((None, tb, 128), lambda i, j: (i, j, COL_GATE // 128)),
                  pl.BlockSpec((None, ncb, 16, CHUNK), lambda i, j: (i, j, 0, 0)),
                  pl.BlockSpec((None, 8, CONV_CH), lambda i, j: (i, 0, 0)),
                  pl.BlockSpec((None, HEADS, GDN_D, GDN_D), lambda i, j: (i, 0, 0, 0)),
                  pl.BlockSpec((CONV_W, CONV_CH), lambda i, j: (0, 0)),
                  pl.BlockSpec((2, 128), lambda i, j: (0, 0)),
                  pl.BlockSpec((16, 2), lambda i, j: (0, 0)),
                  pl.BlockSpec((1, GDN_D), lambda i, j: (0, 0))],
        out_specs=[pl.BlockSpec((None, tb, HEADS * GDN_D), lambda i, j: (i, j, 0)),
                   pl.BlockSpec((None, HEADS, GDN_D, GDN_D), lambda i, j: (i, 0, 0, 0))],
        out_shape=[jax.ShapeDtypeStruct((b, t, HEADS * GDN_D), BF16),
                   jax.ShapeDtypeStruct((b, HEADS, GDN_D, GDN_D), F32)],
        scratch_shapes=[pltpu.VMEM((tb + 8, CONV_CH), F32),
                        pltpu.VMEM((tb, CONV_CH), F32),
                        pltpu.VMEM((HEADS, GDN_D, GDN_D), F32)],
        compiler_params=pltpu.CompilerParams(dimension_semantics=("arbitrary", "arbitrary"),
                                             vmem_limit_bytes=VMEM_LIMIT),
        name="gdn_prompt",
    )(proj, proj, grow, cs8, s0, conv_w, plane, psub, gn)


def _mlstm_kernel(mvo_ref, mqk_ref, gcol_ref, grow_ref, c0_ref, n0_ref, m0_ref, blane_ref, bsub_ref,
                  gn_ref, h_ref, cout_ref, nout_ref, mout_ref, c_ref, n_ref, m_ref, *, tb):
    t = pl.program_id(1)
    nt = pl.num_programs(1)

    @pl.when(t == 0)
    def _():
        c_ref[...] = c0_ref[...]
        n_ref[...] = n0_ref[...]
        m_ref[...] = m0_ref[...]

    ii, jj = _chunk_masks()
    incl = jj <= ii
    lane = lax.broadcasted_iota(jnp.int32, (1, 128), 1)
    row128 = lax.broadcasted_iota(jnp.int32, (128, 1), 0)
    gn = gn_ref[...]
    blane = blane_ref[...]
    bsub = bsub_ref[...]

    def chunk_body(c, carry):
        r0 = pl.multiple_of(c * CHUNK, CHUNK)
        gc = gcol_ref[pl.ds(r0, CHUNK), :] + blane
        gr = grow_ref[c] + bsub
        lf_c = -_softplus(-gc)
        lf_r = -_softplus(-gr)
        for p in range(HEADS // 2):
            qb = mqk_ref[pl.ds(r0, CHUNK), p * 128:(p + 1) * 128]
            kb = mqk_ref[pl.ds(r0, CHUNK), 256 + p * 128:256 + (p + 1) * 128] * (ML_DK ** -0.5)
            c_p = c_ref[p]
            n_p = n_ref[p]
            c_add = jnp.zeros((128, 128), F32)
            n_add = jnp.zeros((1, 128), F32)
            w_ends = []
            for e in range(2):
                h = 2 * p + e
                lm = (lane >= e * ML_DK) & (lane < (e + 1) * ML_DK)
                qh = jnp.where(lm, qb, 0.0)
                kh = jnp.where(lm, kb, 0.0)
                f_col, f_row = _cumsum_col_row(lf_c[:, LANE_MF + h:LANE_MF + h + 1],
                                               lf_r[LANE_MF + h:LANE_MF + h + 1, :], ii, jj)
                ig_row = gr[LANE_MI + h:LANE_MI + h + 1, :]
                ig_col = gc[:, LANE_MI + h:LANE_MI + h + 1]
                d_mat = jnp.where(incl, f_col - f_row + ig_row, -jnp.inf)
                d_max = jnp.max(d_mat, axis=1, keepdims=True)
                qk = _mm_nt(qh, kb)
                m_h = m_ref[:, h:h + 1]
                bcol = f_col + m_h
                mt = jnp.maximum(bcol, d_max)
                w_prev = jnp.exp(bcol - mt)
                p_mat = jnp.where(incl, jnp.exp(jnp.where(incl, d_mat - mt, 0.0)), 0.0)
                pqk = p_mat * qk
                v = mvo_ref[pl.ds(r0, CHUNK), h * ML_DV:(h + 1) * ML_DV]
                num = w_prev * _mm(qh, c_p) + _mm(pqk, v)
                den = (w_prev * jnp.sum(qh * n_p, axis=-1, keepdims=True)
                       + jnp.sum(pqk, axis=-1, keepdims=True))
                hh = num / jnp.maximum(jnp.abs(den), jnp.exp(-mt))
                mt_end = mt[CHUNK - 1:CHUNK, :]
                w_ends.append(w_prev[CHUNK - 1:CHUNK, :])
                p_end = jnp.exp(f_col[CHUNK - 1:CHUNK, :] - f_col + ig_col - mt_end)
                kp = kh * p_end
                c_add = c_add + _mm(kp.T, v)
                n_add = n_add + jnp.sum(kp, axis=0, keepdims=True)
                m_ref[:, h:h + 1] = mt_end
                mo = mvo_ref[pl.ds(r0, CHUNK), 512 + h * ML_DV:512 + (h + 1) * ML_DV]
                out = _sigmoid(mo) * _rms(hh, gn)
                h_ref[pl.ds(r0, CHUNK), h * ML_DV:(h + 1) * ML_DV] = out.astype(h_ref.dtype)
            c_ref[p] = jnp.where(row128 < ML_DK, w_ends[0], w_ends[1]) * c_p + c_add
            n_ref[p] = jnp.where(lane < ML_DK, w_ends[0], w_ends[1]) * n_p + n_add
        return carry

    lax.fori_loop(0, tb // CHUNK, chunk_body, 0)

    @pl.when(t == nt - 1)
    def _():
        cout_ref[...] = c_ref[...]
        nout_ref[...] = n_ref[...]
        mout_ref[...] = m_ref[...]


def _mlstm_prompt(proj, grow, c0, n0, m0, blane, bsub, gn, tb):
    b, t, _ = proj.shape
    ncb = tb // CHUNK
    hp = HEADS // 2
    return pl.pallas_call(
        functools.partial(_mlstm_kernel, tb=tb),
        grid=(b, t // tb),
        in_specs=[pl.BlockSpec((None, tb, 1024), lambda i, j: (i, j, COL_MV // 1024)),
                  pl.BlockSpec((None, tb, 512), lambda i, j: (i, j, COL_MQ // 512)),
                  pl.BlockSpec((None, tb, 128), lambda i, j: (i, j, COL_GATE // 128)),
                  pl.BlockSpec((None, ncb, 16, CHUNK), lambda i, j: (i, j, 0, 0)),
                  pl.BlockSpec((None, hp, 128, 128), lambda i, j: (i, 0, 0, 0)),
                  pl.BlockSpec((None, hp, 1, 128), lambda i, j: (i, 0, 0, 0)),
                  pl.BlockSpec((None, 1, 128), lambda i, j: (i, 0, 0)),
                  pl.BlockSpec((1, 128), lambda i, j: (0, 0)),
                  pl.BlockSpec((16, 1), lambda i, j: (0, 0)),
                  pl.BlockSpec((1, ML_DV), lambda i, j: (0, 0))],
        out_specs=[pl.BlockSpec((None, tb, HEADS * ML_DV), lambda i, j: (i, j, 0)),
                   pl.BlockSpec((None, hp, 128, 128), lambda i, j: (i, 0, 0, 0)),
                   pl.BlockSpec((None, hp, 1, 128), lambda i, j: (i, 0, 0, 0)),
                   pl.BlockSpec((None, 1, 128), lambda i, j: (i, 0, 0))],
        out_shape=[jax.ShapeDtypeStruct((b, t, HEADS * ML_DV), BF16),
                   jax.ShapeDtypeStruct((b, hp, 128, 128), F32),
                   jax.ShapeDtypeStruct((b, hp, 1, 128), F32),
                   jax.ShapeDtypeStruct((b, 1, 128), F32)],
        scratch_shapes=[pltpu.VMEM((hp, 128, 128), F32),
                        pltpu.VMEM((hp, 1, 128), F32),
                        pltpu.VMEM((1, 128), F32)],
        compiler_params=pltpu.CompilerParams(dimension_semantics=("arbitrary", "arbitrary"),
                                             vmem_limit_bytes=VMEM_LIMIT),
        name="mlstm_prompt",
    )(proj, proj, proj, grow, c0, n0, m0, blane, bsub, gn)


def _decode_kernel(proj_ref, cs_ref, s0_ref, c0_ref, n0_ref, m0_ref, cw_ref, plane_ref, gng_ref, gnm_ref,
                   og_ref, om_ref, sout_ref, cout_ref, nout_ref, mout_ref, row_ref, *, bb):
    lane = lax.broadcasted_iota(jnp.int32, (1, 128), 1)
    gt = proj_ref[:, COL_GATE:COL_GATE + 128]
    beta_t = _sigmoid(gt)
    g_t = -jnp.exp(plane_ref[0:1, :]) * _softplus(gt + plane_ref[1:2, :])
    eg_t = jnp.exp(g_t)
    gb_t = gt + plane_ref[2:3, :]
    lf_t = -_softplus(-gb_t)

    for h in range(HEADS):
        cols = []
        for part in range(3):
            c0 = part * 512 + h * GDN_D
            w = cw_ref[:, c0:c0 + GDN_D]
            acc = cs_ref[:, 0, c0:c0 + GDN_D] * w[0:1, :]
            acc = acc + cs_ref[:, 1, c0:c0 + GDN_D] * w[1:2, :]
            acc = acc + cs_ref[:, 2, c0:c0 + GDN_D] * w[2:3, :]
            acc = acc + proj_ref[:, COL_QKV + c0:COL_QKV + c0 + GDN_D] * w[3:4, :]
            cols.append(acc * _sigmoid(acc))
        q, k, v = cols
        q = q * lax.rsqrt(jnp.sum(q * q, axis=-1, keepdims=True) + EPS) * (GDN_D ** -0.5)
        k = k * lax.rsqrt(jnp.sum(k * k, axis=-1, keepdims=True) + EPS)
        qk = jnp.sum(q * k, axis=-1, keepdims=True)
        q_t = q.T
        k_t = k.T
        beta = beta_t[:, LANE_GB + h:LANE_GB + h + 1]
        eg = eg_t[:, LANE_GA + h:LANE_GA + h + 1]
        for b in range(bb):
            s_bh = s0_ref[b, h]
            k_col = k_t[:, b:b + 1]
            q_col = q_t[:, b:b + 1]
            ks = jnp.sum(k_col * s_bh, axis=0, keepdims=True)
            qs = jnp.sum(q_col * s_bh, axis=0, keepdims=True)
            eg_b = eg[b:b + 1, :]
            u = beta[b:b + 1, :] * (v[b:b + 1, :] - eg_b * ks)
            row_ref[b:b + 1, :] = eg_b * qs + qk[b:b + 1, :] * u
            sout_ref[b, h] = eg_b * s_bh + k_col * u
        z = proj_ref[:, COL_Z + h * GDN_D:COL_Z + (h + 1) * GDN_D]
        out = _rms(row_ref[...], gng_ref[...]) * (z * _sigmoid(z))
        og_ref[:, h * GDN_D:(h + 1) * GDN_D] = out.astype(og_ref.dtype)

    for p in range(HEADS // 2):
        qb = proj_ref[:, COL_MQ + p * 128:COL_MQ + (p + 1) * 128]
        kb = proj_ref[:, COL_MK + p * 128:COL_MK + (p + 1) * 128] * (ML_DK ** -0.5)
        n_p = n0_ref[:, p * 128:(p + 1) * 128]
        q_t = qb.T
        k_t = kb.T
        w_prev, p_in, qkp, floor_ = [], [], [], []
        for e in range(2):
            h = 2 * p + e
            lm = (lane >= e * ML_DK) & (lane < (e + 1) * ML_DK)
            ig = gb_t[:, LANE_MI + h:LANE_MI + h + 1]
            lf = lf_t[:, LANE_MF + h:LANE_MF + h + 1]
            m_old = m0_ref[:, h:h + 1]
            m_new = jnp.maximum(lf + m_old, ig)
            w_prev.append(jnp.exp(lf + m_old - m_new))
            p_in.append(jnp.exp(ig - m_new))
            floor_.append(jnp.exp(-m_new))
            qk = jnp.sum(jnp.where(lm, qb * kb, 0.0), axis=-1, keepdims=True)
            qn = jnp.sum(jnp.where(lm, qb * n_p, 0.0), axis=-1, keepdims=True)
            qkp.append((p_in[e] * qk, w_prev[e] * qn))
            mout_ref[:, h:h + 1] = m_new
        lo_lane = lane < ML_DK
        nout_ref[:, p * 128:(p + 1) * 128] = (jnp.where(lo_lane, w_prev[0], w_prev[1]) * n_p
                                              + jnp.where(lo_lane, p_in[0], p_in[1]) * kb)
        for e in range(2):
            h = 2 * p + e
            v = proj_ref[:, COL_MV + h * ML_DV:COL_MV + (h + 1) * ML_DV]
            rs = slice(e * ML_DK, (e + 1) * ML_DK)
            for b in range(bb):
                c_bh = c0_ref[b, p, rs, :]
                q_col = q_t[rs, b:b + 1]
                k_col = k_t[rs, b:b + 1]
                qc = jnp.sum(q_col * c_bh, axis=0, keepdims=True)
                wp = w_prev[e][b:b + 1, :]
                pi = p_in[e][b:b + 1, :]
                pqk, wqn = qkp[e][0][b:b + 1, :], qkp[e][1][b:b + 1, :]
                v_row = v[b:b + 1, :]
                num = wp * qc + pqk * v_row
                den = wqn + pqk
                row_ref[b:b + 1, :] = num / jnp.maximum(jnp.abs(den), floor_[e][b:b + 1, :])
                cout_ref[b, p, rs, :] = wp * c_bh + (pi * k_col) * v_row
            mo = proj_ref[:, COL_MO + h * ML_DV:COL_MO + (h + 1) * ML_DV]
            out = _sigmoid(mo) * _rms(row_ref[...], gnm_ref[...])
            om_ref[:, h * ML_DV:(h + 1) * ML_DV] = out.astype(om_ref.dtype)
    mout_ref[:, HEADS:128] = m0_ref[:, HEADS:128]


def _decode(proj, cs, s0, c0, n0, m0, conv_w, plane, gng, gnm, bb):
    b = proj.shape[0]
    hp = HEADS // 2
    return pl.pallas_call(
        functools.partial(_decode_kernel, bb=bb),
        grid=(b // bb,),
        in_specs=[pl.BlockSpec((bb, N_PROJ), lambda i: (i, 0)),
                  pl.BlockSpec((bb, CONV_W - 1, CONV_CH), lambda i: (i, 0, 0)),
                  pl.BlockSpec((bb, HEADS, GDN_D, GDN_D), lambda i: (i, 0, 0, 0)),
                  pl.BlockSpec((bb, hp, 128, 128), lambda i: (i, 0, 0, 0)),
                  pl.BlockSpec((bb, HEADS * ML_DK), lambda i: (i, 0)),
                  pl.BlockSpec((bb, 128), lambda i: (i, 0)),
                  pl.BlockSpec((CONV_W, CONV_CH), lambda i: (0, 0)),
                  pl.BlockSpec((3, 128), lambda i: (0, 0)),
                  pl.BlockSpec((1, GDN_D), lambda i: (0, 0)),
                  pl.BlockSpec((1, ML_DV), lambda i: (0, 0))],
        out_specs=[pl.BlockSpec((bb, HEADS * GDN_D), lambda i: (i, 0)),
                   pl.BlockSpec((bb, HEADS * ML_DV), lambda i: (i, 0)),
                   pl.BlockSpec((bb, HEADS, GDN_D, GDN_D), lambda i: (i, 0, 0, 0)),
                   pl.BlockSpec((bb, hp, 128, 128), lambda i: (i, 0, 0, 0)),
                   pl.BlockSpec((bb, HEADS * ML_DK), lambda i: (i, 0)),
                   pl.BlockSpec((bb, 128), lambda i: (i, 0))],
        out_shape=[jax.ShapeDtypeStruct((b, HEADS * GDN_D), BF16),
                   jax.ShapeDtypeStruct((b, HEADS * ML_DV), BF16),
                   jax.ShapeDtypeStruct((b, HEADS, GDN_D, GDN_D), F32),
                   jax.ShapeDtypeStruct((b, hp, 128, 128), F32),
                   jax.ShapeDtypeStruct((b, HEADS * ML_DK), F32),
                   jax.ShapeDtypeStruct((b, 128), F32)],
        scratch_shapes=[pltpu.VMEM((bb, 128), F32)],
        compiler_params=pltpu.CompilerParams(dimension_semantics=("arbitrary",),
                                             vmem_limit_bytes=VMEM_LIMIT),
        name="decode_step",
    )(proj, cs, s0, c0, n0, m0, conv_w, plane, gng, gnm)


def _regroup_w_in(w_in):
    sizes = (512, 512, 512, 512, 4, 4, 256, 256, 512, 512, 4, 4)
    offs = [0]
    for s in sizes:
        offs.append(offs[-1] + s)
    gq, gk, gv, gz, gb, ga, mq, mk, mv, mo, mi, mf = (w_in[:, offs[i]:offs[i + 1]] for i in range(12))
    pad = jnp.zeros((w_in.shape[0], 128 - 16), w_in.dtype)
    return jnp.concatenate([gq, gk, gv, gz, mv, mo, mq, mk, gb, ga, mi, mf, pad], axis=1).astype(BF16)


def _lane_vec(pairs):
    v = jnp.zeros((128,), F32)
    for off, val in pairs:
        v = v.at[off:off + HEADS].set(val.astype(F32))
    return v


def _prep_params(norm_pre_mix, w_in, conv_w, a_log, dt_bias, gdn_norm_g, b_igate, b_fgate, mlstm_norm_g,
                 w_out, norm_post_mix, norm_pre_mlp, w_up, w_down, norm_post_mlp):
    alog_v = _lane_vec([(LANE_GA, a_log[0])])
    dtb_v = _lane_vec([(LANE_GA, dt_bias[0])])
    bias_v = _lane_vec([(LANE_MI, b_igate[0]), (LANE_MF, b_fgate[0])])
    return dict(
        w_in=_regroup_w_in(w_in[0]), wo=w_out[0].astype(BF16), wu=w_up[0].astype(BF16),
        wd=w_down[0].astype(BF16),
        g_pre=norm_pre_mix[0][None, :], g1=norm_post_mix[0][None, :], g2=norm_pre_mlp[0][None, :],
        g3=norm_post_mlp[0][None, :], cw=conv_w[0], gng=gdn_norm_g[0][None, :], gnm=mlstm_norm_g[0][None, :],
        gdn_plane=jnp.stack([alog_v, dtb_v]),
        gdn_psub=jnp.stack([alog_v[:16], dtb_v[:16]], axis=1),
        ml_blane=bias_v[None, :],
        ml_bsub=bias_v[:16, None],
        dec_plane=jnp.stack([alog_v, dtb_v, bias_v]),
    )


def _prompt_path(x, conv0, s0, c0, n0, m0, prm, tb, tm):
    bsz, seq, d = x.shape
    hp = HEADS // 2
    x2d = x.reshape(bsz * seq, d)
    proj = _in_proj(x2d, prm["g_pre"], prm["w_in"], tm=tm).reshape(bsz, seq, N_PROJ)
    gates = proj[:, :, COL_GATE:COL_GATE + 16]
    grow = gates.reshape(bsz, seq // CHUNK, CHUNK, 16).transpose(0, 1, 3, 2)
    cs8 = jnp.pad(conv0, ((0, 0), (8 - (CONV_W - 1), 0), (0, 0)))
    og, s_new = _gdn_prompt(proj, grow, cs8, s0, prm["cw"], prm["gdn_plane"], prm["gdn_psub"], prm["gng"], tb=tb)
    om, c_new, n_new, m_new = _mlstm_prompt(
        proj, grow, c0.reshape(bsz, hp, 128, 128), n0.reshape(bsz, hp, 1, 128),
        jnp.pad(m0, ((0, 0), (0, 128 - HEADS)))[:, None, :], prm["ml_blane"], prm["ml_bsub"], prm["gnm"], tb=tb)
    y = _out_mlp(og.reshape(bsz * seq, -1), om.reshape(bsz * seq, -1), x2d, prm["wo"], prm["g1"], prm["g2"],
                 prm["wu"], prm["wd"], prm["g3"], tm=tm).reshape(bsz, seq, d)
    xp = jnp.concatenate([conv0, proj[:, seq - (CONV_W - 1):, COL_QKV:COL_QKV + CONV_CH]], axis=1)
    conv_new = xp[:, -(CONV_W - 1):]
    return y, (conv_new, s_new, c_new.reshape(bsz, HEADS, ML_DK, ML_DV), n_new.reshape(bsz, HEADS, ML_DK),
               m_new[:, 0, :HEADS])


def _sample_path(x, conv0, s0, c0, n0, m0, prm):
    dec, _, d = x.shape
    hp = HEADS // 2
    xs = x.reshape(dec, d)
    proj = _in_proj(xs, prm["g_pre"], prm["w_in"], tm=dec)
    og, om, s_new, c_new, n_new, m_new = _decode(
        proj, conv0, s0, c0.reshape(dec, hp, 128, 128), n0.reshape(dec, HEADS * ML_DK),
        jnp.pad(m0, ((0, 0), (0, 128 - HEADS))), prm["cw"], prm["dec_plane"], prm["gng"], prm["gnm"], bb=8)
    y = _out_mlp(og, om, xs, prm["wo"], prm["g1"], prm["g2"], prm["wu"], prm["wd"], prm["g3"],
                 tm=dec).reshape(dec, 1, d)
    conv_new = jnp.concatenate([conv0[:, 1:, :], proj[:, None, COL_QKV:COL_QKV + CONV_CH]], axis=1)
    return y, (conv_new, s_new, c_new.reshape(dec, HEADS, ML_DK, ML_DV), n_new.reshape(dec, HEADS, ML_DK),
               m_new[:, :HEADS])


def kernel(x_prompt, x_sample, state_gdn_conv, state_gdn_S, state_mlstm_C, state_mlstm_n, state_mlstm_m,
           norm_pre_mix, w_in, conv_w, a_log, dt_bias, gdn_norm_g, b_igate, b_fgate, mlstm_norm_g, w_out,
           norm_post_mix, norm_pre_mlp, w_up, w_down, norm_post_mlp):
    bsz = x_prompt.shape[0]
    prm = _prep_params(norm_pre_mix, w_in, conv_w, a_log, dt_bias, gdn_norm_g, b_igate, b_fgate, mlstm_norm_g,
                       w_out, norm_post_mix, norm_pre_mlp, w_up, w_down, norm_post_mlp)
    y_p, p_st = _prompt_path(
        x_prompt, jnp.zeros((bsz, CONV_W - 1, CONV_CH), F32), jnp.zeros((bsz, HEADS, GDN_D, GDN_D), F32),
        jnp.zeros((bsz, HEADS, ML_DK, ML_DV), F32), jnp.zeros((bsz, HEADS, ML_DK), F32),
        jnp.zeros((bsz, HEADS), F32), prm, tb=512, tm=512)
    y_s, s_st = _sample_path(x_sample, state_gdn_conv[0], state_gdn_S[0], state_mlstm_C[0], state_mlstm_n[0],
                             state_mlstm_m[0], prm)
    return (y_p, y_s) + tuple(a[None] for a in p_st) + tuple(a[None] for a in s_st)
```

```python
import functools

import jax
import jax.numpy as jnp
from jax import lax
from jax.experimental import pallas as pl
from jax.experimental.pallas import tpu as pltpu

F32 = jnp.float32
BF16 = jnp.bfloat16
EPS = 1e-6

D_MODEL = 1024
HEADS = 4
GDN_D = 128
ML_DK = 64
ML_DV = 128
CONV_W = 4
CONV_CH = 3 * HEADS * GDN_D
D_FF = 4 * D_MODEL
CHUNK = 64

COL_QKV = 0
COL_Z = 1536
COL_MV = 2048
COL_MO = 2560
COL_MQ = 3072
COL_MK = 3328
COL_GATE = 3584
N_PROJ = COL_GATE + 128
LANE_GB, LANE_GA, LANE_MI, LANE_MF = 0, 4, 8, 12

VMEM_LIMIT = 56 * 1024 * 1024


def _rms(x, g):
    return x * lax.rsqrt(jnp.mean(x * x, axis=-1, keepdims=True) + EPS) * g


def _softplus(x):
    return jnp.maximum(x, 0.0) + jnp.log1p(jnp.exp(-jnp.abs(x)))


def _sigmoid(x):
    return 1.0 / (1.0 + jnp.exp(-x))


def _mm(a, b):
    return jnp.dot(a.astype(BF16), b.astype(BF16), preferred_element_type=F32)


def _mm_nt(a, b):
    return lax.dot_general(a.astype(BF16), b.astype(BF16), (((1,), (1,)), ((), ())),
                           preferred_element_type=F32)


def _in_proj_kernel(x_ref, g_ref, w_ref, o_ref, h_ref, *, n_chunk):
    h_ref[...] = _rms(x_ref[...], g_ref[...]).astype(BF16)
    n = o_ref.shape[1]
    for c0 in range(0, n, n_chunk):
        c1 = min(c0 + n_chunk, n)
        o_ref[:, c0:c1] = jnp.dot(h_ref[...], w_ref[:, c0:c1], preferred_element_type=F32)


def _in_proj(x2d, g, w, tm):
    m, k = x2d.shape
    n = w.shape[1]
    return pl.pallas_call(
        functools.partial(_in_proj_kernel, n_chunk=512),
        grid=(m // tm,),
        in_specs=[pl.BlockSpec((tm, k), lambda i: (i, 0)),
                  pl.BlockSpec((1, k), lambda i: (0, 0)),
                  pl.BlockSpec((k, n), lambda i: (0, 0))],
        out_specs=pl.BlockSpec((tm, n), lambda i: (i, 0)),
        out_shape=jax.ShapeDtypeStruct((m, n), F32),
        scratch_shapes=[pltpu.VMEM((tm, k), BF16)],
        compiler_params=pltpu.CompilerParams(dimension_semantics=("arbitrary",),
                                             vmem_limit_bytes=VMEM_LIMIT),
        name="in_proj",
    )(x2d, g, w)


def _out_mlp_kernel(og_ref, om_ref, x_ref, wo_ref, g1_ref, g2_ref, wu_ref, wd_ref, g3_ref,
                    y_ref, hn_ref, acc_ref, *, ff_chunk):
    half = og_ref.shape[1]
    mix = (jnp.dot(og_ref[...], wo_ref[0:half, :], preferred_element_type=F32)
           + jnp.dot(om_ref[...], wo_ref[half:2 * half, :], preferred_element_type=F32))
    x1 = x_ref[...] + _rms(mix, g1_ref[...])
    y_ref[...] = x1
    hn_ref[...] = _rms(x1, g2_ref[...]).astype(BF16)
    d_ff = wu_ref.shape[1]
    for c0 in range(0, d_ff, ff_chunk):
        u = jnp.dot(hn_ref[...], wu_ref[:, c0:c0 + ff_chunk], preferred_element_type=F32)
        u = jnp.square(jnp.maximum(u, 0.0)).astype(BF16)
        d = jnp.dot(u, wd_ref[c0:c0 + ff_chunk, :], preferred_element_type=F32)
        if c0 == 0:
            acc_ref[...] = d
        else:
            acc_ref[...] += d
    y_ref[...] = y_ref[...] + _rms(acc_ref[...], g3_ref[...])


def _out_mlp(og, om, x2d, wo, g1, g2, wu, wd, g3, tm):
    m, d = x2d.shape
    half = og.shape[1]
    d_ff = wu.shape[1]
    const = lambda i: (0, 0)
    return pl.pallas_call(
        functools.partial(_out_mlp_kernel, ff_chunk=1024),
        grid=(m // tm,),
        in_specs=[pl.BlockSpec((tm, half), lambda i: (i, 0)),
                  pl.BlockSpec((tm, half), lambda i: (i, 0)),
                  pl.BlockSpec((tm, d), lambda i: (i, 0)),
                  pl.BlockSpec((d, d), const, pipeline_mode=pl.Buffered(1)),
                  pl.BlockSpec((1, d), const),
                  pl.BlockSpec((1, d), const),
                  pl.BlockSpec((d, d_ff), const, pipeline_mode=pl.Buffered(1)),
                  pl.BlockSpec((d_ff, d), const, pipeline_mode=pl.Buffered(1)),
                  pl.BlockSpec((1, d), const)],
        out_specs=pl.BlockSpec((tm, d), lambda i: (i, 0)),
        out_shape=jax.ShapeDtypeStruct((m, d), F32),
        scratch_shapes=[pltpu.VMEM((tm, d), BF16), pltpu.VMEM((tm, d), F32)],
        compiler_params=pltpu.CompilerParams(dimension_semantics=("arbitrary",),
                                             vmem_limit_bytes=VMEM_LIMIT),
        name="out_mlp",
    )(og, om, x2d, wo, g1, g2, wu, wd, g3)


def _chunk_masks():
    ii = lax.broadcasted_iota(jnp.int32, (CHUNK, CHUNK), 0)
    jj = lax.broadcasted_iota(jnp.int32, (CHUNK, CHUNK), 1)
    return ii, jj


def _cumsum_col_row(x_col, x_row, ii, jj):
    c_col = jnp.sum(jnp.where(jj <= ii, x_row, 0.0), axis=1, keepdims=True)
    c_row = jnp.sum(jnp.where(ii <= jj, x_col, 0.0), axis=0, keepdims=True)
    return c_col, c_row


def _gdn_kernel(qkvz_ref, gcol_ref, grow_ref, cs_ref, s0_ref, cw_ref, plane_ref, psub_ref, gn_ref,
                o_ref, sout_ref, xp_ref, yc_ref, s_ref, wv_ref, lhs_ref, qk_ref, ket_ref, ge_ref,
                *, tb, unroll_a):
    t = pl.program_id(1)
    nt = pl.num_programs(1)

    @pl.when(t == 0)
    def _():
        xp_ref[0:8, :] = cs_ref[...]
        s_ref[...] = s0_ref[...]

    @pl.when(t > 0)
    def _():
        xp_ref[0:8, :] = xp_ref[tb:tb + 8, :]

    xp_ref[8:tb + 8, :] = qkvz_ref[:, COL_QKV:COL_QKV + CONV_CH]

    def conv_tile(ct, carry):
        c0 = pl.multiple_of(ct * 128, 128)
        w = cw_ref[:, pl.ds(c0, 128)]
        for r0 in range(0, tb, 128):
            acc = xp_ref[5 + r0:5 + r0 + 128, pl.ds(c0, 128)] * w[0:1, :]
            for j in range(1, CONV_W):
                acc = acc + xp_ref[5 + j + r0:5 + j + r0 + 128, pl.ds(c0, 128)] * w[j:j + 1, :]
            yc_ref[r0:r0 + 128, pl.ds(c0, 128)] = acc * _sigmoid(acc)
        return carry

    lax.fori_loop(0, CONV_CH // 128, conv_tile, 0)

    ii, jj = _chunk_masks()
    incl = jj <= ii
    strict = jj < ii
    eye = (ii == jj).astype(F32)
    neg_a_lane = -jnp.exp(plane_ref[0:1, :])
    dtb_lane = plane_ref[1:2, :]
    neg_a_sub = -jnp.exp(psub_ref[:, 0:1])
    dtb_sub = psub_ref[:, 1:2]
    gn = gn_ref[...]

    def phase_a(ci, carry):
        items = []
        for cu in range(unroll_a):
            c = ci * unroll_a + cu
            r0 = pl.multiple_of(c * CHUNK, CHUNK)
            gc = gcol_ref[pl.ds(r0, CHUNK), :]
            gr = grow_ref[c]
            beta_t = _sigmoid(gc)
            g_t = neg_a_lane * _softplus(gc + dtb_lane)
            g_r = neg_a_sub * _softplus(gr + dtb_sub)
            for h in range(HEADS):
                lo = h * GDN_D
                items.append(dict(c=c, h=h,
                                  q=yc_ref[pl.ds(r0, CHUNK), lo:lo + GDN_D],
                                  k=yc_ref[pl.ds(r0, CHUNK), 512 + lo:512 + lo + GDN_D],
                                  v=yc_ref[pl.ds(r0, CHUNK), 1024 + lo:1024 + lo + GDN_D],
                                  beta=beta_t[:, LANE_GB + h:LANE_GB + h + 1],
                                  gg_col=g_t[:, LANE_GA + h:LANE_GA + h + 1],
                                  gg_row=g_r[LANE_GA + h:LANE_GA + h + 1, :]))
        for it in items:
            it["k"] = it["k"] * lax.rsqrt(jnp.sum(it["k"] * it["k"], axis=-1, keepdims=True) + EPS)
        for it in items:
            it["kk"] = _mm_nt(it["k"], it["k"])
        for it in items:
            it["q"] = (it["q"] * lax.rsqrt(jnp.sum(it["q"] * it["q"], axis=-1, keepdims=True) + EPS)
                       * (GDN_D ** -0.5))
        for it in items:
            it["g_col"], g_row = _cumsum_col_row(it["gg_col"], it["gg_row"], ii, jj)
            it["decay"] = jnp.where(incl, jnp.exp(jnp.where(incl, it["g_col"] - g_row, 0.0)), 0.0)
        for it in items:
            n_mat = jnp.where(strict, it["beta"] * it["kk"] * it["decay"], 0.0)
            it["x"] = eye - n_mat
            it["p"] = -n_mat
        for _ in range(5):
            for it in items:
                it["p"] = _mm(it["p"], it["p"])
            for it in items:
                it["x"] = it["x"] + _mm(it["x"], it["p"])
        for it in items:
            e_g = jnp.exp(it["g_col"])
            it["e_g"] = e_g
            rhs = jnp.concatenate([it["beta"] * it["v"], (it["beta"] * e_g) * it["k"]], axis=1)
            it["w"] = _mm(it["x"], rhs)
        for it in items:
            it["qk"] = _mm_nt(it["q"], it["k"]) * it["decay"]
        for it in items:
            c, h = it["c"], it["h"]
            g_end = it["g_col"][CHUNK - 1:CHUNK, :]
            k_end = it["k"] * jnp.exp(g_end - it["g_col"])
            wv_ref[c, h] = it["w"][:, 0:GDN_D]
            lhs_ref[c, h] = jnp.concatenate([it["w"][:, GDN_D:2 * GDN_D], it["e_g"] * it["q"]],
                                            axis=0).astype(BF16)
            qk_ref[c, h] = it["qk"].astype(BF16)
            ket_ref[c, h] = k_end.T.astype(BF16)
            ge_ref[c, h] = jnp.broadcast_to(jnp.exp(g_end), (1, GDN_D))
        return carry

    lax.fori_loop(0, tb // CHUNK // unroll_a, phase_a, 0)

    def phase_b(c, carry):
        r0 = pl.multiple_of(c * CHUNK, CHUNK)
        s = [s_ref[h] for h in range(HEADS)]
        r = [jnp.dot(lhs_ref[c, h], s[h].astype(BF16), preferred_element_type=F32) for h in range(HEADS)]
        ub = [(wv_ref[c, h] - r[h][0:CHUNK]).astype(BF16) for h in range(HEADS)]
        o = [r[h][CHUNK:2 * CHUNK] + jnp.dot(qk_ref[c, h], ub[h], preferred_element_type=F32)
             for h in range(HEADS)]
        for h in range(HEADS):
            s_ref[h] = ge_ref[c, h] * s[h] + jnp.dot(ket_ref[c, h], ub[h], preferred_element_type=F32)
        for h in range(HEADS):
            lo = h * GDN_D
            z = qkvz_ref[pl.ds(r0, CHUNK), COL_Z + lo:COL_Z + lo + GDN_D]
            out = _rms(o[h], gn) * (z * _sigmoid(z))
            o_ref[pl.ds(r0, CHUNK), lo:lo + GDN_D] = out.astype(o_ref.dtype)
        return carry

    lax.fori_loop(0, tb // CHUNK, phase_b, 0)

    @pl.when(t == nt - 1)
    def _():
        sout_ref[...] = s_ref[...]


def _gdn_prompt(proj, grow, cs8, s0, conv_w, plane, psub, gn, tb, unroll_a=2):
    b, t, _ = proj.shape
    ncb = tb // CHUNK
    return pl.pallas_call(
        functools.partial(_gdn_kernel, tb=tb, unroll_a=unroll_a),
        grid=(b, t // tb),
        in_specs=[pl.BlockSpec((None, tb, 2048), lambda i, j: (i, j, 0)),
                  pl.BlockSpec((None, tb, 128), lambda i, j: (i, j, COL_GATE // 128)),
                  pl.BlockSpec((None, ncb, 16, CHUNK), lambda i, j: (i, j, 0, 0)),
                  pl.BlockSpec((None, 8, CONV_CH), lambda i, j: (i, 0, 0)),
                  pl.BlockSpec((None, HEADS, GDN_D, GDN_D), lambda i, j: (i, 0, 0, 0)),
                  pl.BlockSpec((CONV_W, CONV_CH), lambda i, j: (0, 0)),
                  pl.BlockSpec((2, 128), lambda i, j: (0, 0)),
                  pl.BlockSpec((16, 2), lambda i, j: (0, 0)),
                  pl.BlockSpec((1, GDN_D), lambda i, j: (0, 0))],
        out_specs=[pl.BlockSpec((None, tb, HEADS * GDN_D), lambda i, j: (i, j, 0)),
                   pl.BlockSpec((None, HEADS, GDN_D, GDN_D), lambda i, j: (i, 0, 0, 0))],
        out_shape=[jax.ShapeDtypeStruct((b, t, HEADS * GDN_D), BF16),
                   jax.ShapeDtypeStruct((b, HEADS, GDN_D, GDN_D), F32)],
        scratch_shapes=[pltpu.VMEM((tb + 8, CONV_CH), F32),
                        pltpu.VMEM((tb, CONV_CH), F32),
                        pltpu.VMEM((HEADS, GDN_D, GDN_D), F32),
                        pltpu.VMEM((ncb, HEADS, CHUNK, GDN_D), F32),
                        pltpu.VMEM((ncb, HEADS, 2 * CHUNK, GDN_D), BF16),
                        pltpu.VMEM((ncb, HEADS, CHUNK, CHUNK), BF16),
                        pltpu.VMEM((ncb, HEADS, GDN_D, CHUNK), BF16),
                        pltpu.VMEM((ncb, HEADS, 1, GDN_D), F32)],
        compiler_params=pltpu.CompilerParams(dimension_semantics=("arbitrary", "arbitrary"),
                                             vmem_limit_bytes=VMEM_LIMIT),
        name="gdn_prompt",
    )(proj, proj, grow, cs8, s0, conv_w, plane, psub, gn)


def _mlstm_kernel(mvo_ref, mqk_ref, gcol_ref, grow_ref, c0_ref, n0_ref, m0_ref, blane_ref, bsub_ref,
                  gn_ref, h_ref, cout_ref, nout_ref, mout_ref, c_ref, n_ref, m_ref, *, tb, unroll):
    t = pl.program_id(1)
    nt = pl.num_programs(1)

    @pl.when(t == 0)
    def _():
        c_ref[...] = c0_ref[...]
        n_ref[...] = n0_ref[...]
        m_ref[...] = m0_ref[...]

    ii, jj = _chunk_masks()
    incl = jj <= ii
    lane = lax.broadcasted_iota(jnp.int32, (1, 128), 1)
    row128 = lax.broadcasted_iota(jnp.int32, (128, 1), 0)
    gn = gn_ref[...]
    blane = blane_ref[...]
    bsub = bsub_ref[...]

    def chunk_body(ci, carry):
        items = []
        for cu in range(unroll):
            c = ci * unroll + cu
            r0 = pl.multiple_of(c * CHUNK, CHUNK)
            gc = gcol_ref[pl.ds(r0, CHUNK), :] + blane
            gr = grow_ref[c] + bsub
            lf_c = -_softplus(-gc)
            lf_r = -_softplus(-gr)
            for p in range(HEADS // 2):
                qb = mqk_ref[pl.ds(r0, CHUNK), p * 128:(p + 1) * 128]
                kb = mqk_ref[pl.ds(r0, CHUNK), 256 + p * 128:256 + (p + 1) * 128] * (ML_DK ** -0.5)
                for e in range(2):
                    h = 2 * p + e
                    lm = (lane >= e * ML_DK) & (lane < (e + 1) * ML_DK)
                    items.append(dict(cu=cu, r0=r0, p=p, e=e, h=h, qh=jnp.where(lm, qb, 0.0), kb=kb,
                                      kh=jnp.where(lm, kb, 0.0),
                                      lf_col=lf_c[:, LANE_MF + h:LANE_MF + h + 1],
                                      lf_row=lf_r[LANE_MF + h:LANE_MF + h + 1, :],
                                      ig_row=gr[LANE_MI + h:LANE_MI + h + 1, :],
                                      ig_col=gc[:, LANE_MI + h:LANE_MI + h + 1],
                                      v=mvo_ref[pl.ds(r0, CHUNK), h * ML_DV:(h + 1) * ML_DV]))
        for it in items:
            it["qk"] = _mm_nt(it["qh"], it["kb"])
        for it in items:
            it["f_col"], it["f_row"] = _cumsum_col_row(it["lf_col"], it["lf_row"], ii, jj)
        for it in items:
            it["d_mat"] = jnp.where(incl, it["f_col"] - it["f_row"] + it["ig_row"], -jnp.inf)
            it["d_max"] = jnp.max(it["d_mat"], axis=1, keepdims=True)
        for it in items:
            p0 = jnp.where(incl, jnp.exp(jnp.where(incl, it["d_mat"] - it["d_max"], 0.0)), 0.0)
            pend0 = jnp.exp(it["f_col"][CHUNK - 1:CHUNK, :] - it["f_col"] + it["ig_col"]
                            - it["d_max"][CHUNK - 1:CHUNK, :])
            it["kp0"] = it["kh"] * pend0
            it["pqk0"] = p0 * it["qk"]
        for it in items:
            it["pv0"] = _mm(it["pqk0"], it["v"])
            it["rs0"] = jnp.sum(it["pqk0"], axis=-1, keepdims=True)
        for it in items:
            it["cadd0"] = _mm(it["kp0"].T, it["v"])
            it["nadd0"] = jnp.sum(it["kp0"], axis=0, keepdims=True)

        m_cur = [m_ref[:, h:h + 1] for h in range(HEADS)]
        for it in items:
            h = it["h"]
            it["m_prev"] = m_cur[h]
            m_cur[h] = jnp.maximum(it["f_col"][CHUNK - 1:CHUNK, :] + m_cur[h], it["d_max"][CHUNK - 1:CHUNK, :])
        for it in items:
            bcol = it["f_col"] + it["m_prev"]
            mt = jnp.maximum(bcol, it["d_max"])
            it["mt"] = mt
            it["w_prev"] = jnp.exp(bcol - mt)
            it["sc"] = jnp.exp(it["d_max"] - mt)
        c_cur = [c_ref[p] for p in range(HEADS // 2)]
        n_cur = [n_ref[p] for p in range(HEADS // 2)]
        for cu in range(unroll):
            for p in range(HEADS // 2):
                pair = [it for it in items if it["cu"] == cu and it["p"] == p]
                w_end = [it["w_prev"][CHUNK - 1:CHUNK, :] for it in pair]
                s_end = [it["sc"][CHUNK - 1:CHUNK, :] for it in pair]
                for it in pair:
                    it["c_prev"] = c_cur[p]
                    it["n_prev"] = n_cur[p]
                c_cur[p] = (jnp.where(row128 < ML_DK, w_end[0], w_end[1]) * c_cur[p]
                            + s_end[0] * pair[0]["cadd0"] + s_end[1] * pair[1]["cadd0"])
                n_cur[p] = (jnp.where(lane < ML_DK, w_end[0], w_end[1]) * n_cur[p]
                            + s_end[0] * pair[0]["nadd0"] + s_end[1] * pair[1]["nadd0"])
        for it in items:
            it["qc"] = _mm(it["qh"], it["c_prev"])
        for it in items:
            num = it["w_prev"] * it["qc"] + it["sc"] * it["pv0"]
            den = (it["w_prev"] * jnp.sum(it["qh"] * it["n_prev"], axis=-1, keepdims=True)
                   + it["sc"] * it["rs0"])
            it["hh"] = num / jnp.maximum(jnp.abs(den), jnp.exp(-it["mt"]))
        for it in items:
            h = it["h"]
            mo = mvo_ref[pl.ds(it["r0"], CHUNK), 512 + h * ML_DV:512 + (h + 1) * ML_DV]
            out = _sigmoid(mo) * _rms(it["hh"], gn)
            h_ref[pl.ds(it["r0"], CHUNK), h * ML_DV:(h + 1) * ML_DV] = out.astype(h_ref.dtype)
        for p in range(HEADS // 2):
            c_ref[p] = c_cur[p]
            n_ref[p] = n_cur[p]
        for h in range(HEADS):
            m_ref[:, h:h + 1] = m_cur[h]
        return carry

    lax.fori_loop(0, tb // CHUNK // unroll, chunk_body, 0)

    @pl.when(t == nt - 1)
    def _():
        cout_ref[...] = c_ref[...]
        nout_ref[...] = n_ref[...]
        mout_ref[...] = m_ref[...]


def _mlstm_prompt(proj, grow, c0, n0, m0, blane, bsub, gn, tb, unroll=2):
    b, t, _ = proj.shape
    ncb = tb // CHUNK
    hp = HEADS // 2
    return pl.pallas_call(
        functools.partial(_mlstm_kernel, tb=tb, unroll=unroll),
        grid=(b, t // tb),
        in_specs=[pl.BlockSpec((None, tb, 1024), lambda i, j: (i, j, COL_MV // 1024)),
                  pl.BlockSpec((None, tb, 512), lambda i, j: (i, j, COL_MQ // 512)),
                  pl.BlockSpec((None, tb, 128), lambda i, j: (i, j, COL_GATE // 128)),
                  pl.BlockSpec((None, ncb, 16, CHUNK), lambda i, j: (i, j, 0, 0)),
                  pl.BlockSpec((None, hp, 128, 128), lambda i, j: (i, 0, 0, 0)),
                  pl.BlockSpec((None, hp, 1, 128), lambda i, j: (i, 0, 0, 0)),
                  pl.BlockSpec((None, 1, 128), lambda i, j: (i, 0, 0)),
                  pl.BlockSpec((1, 128), lambda i, j: (0, 0)),
                  pl.BlockSpec((16, 1), lambda i, j: (0, 0)),
                  pl.BlockSpec((1, ML_DV), lambda i, j: (0, 0))],
        out_specs=[pl.BlockSpec((None, tb, HEADS * ML_DV), lambda i, j: (i, j, 0)),
                   pl.BlockSpec((None, hp, 128, 128), lambda i, j: (i, 0, 0, 0)),
                   pl.BlockSpec((None, hp, 1, 128), lambda i, j: (i, 0, 0, 0)),
                   pl.BlockSpec((None, 1, 128), lambda i, j: (i, 0, 0))],
        out_shape=[jax.ShapeDtypeStruct((b, t, HEADS * ML_DV), BF16),
                   jax.ShapeDtypeStruct((b, hp, 128, 128), F32),
                   jax.ShapeDtypeStruct((b, hp, 1, 128), F32),
                   jax.ShapeDtypeStruct((b, 1, 128), F32)],
        scratch_shapes=[pltpu.VMEM((hp, 128, 128), F32),
                        pltpu.VMEM((hp, 1, 128), F32),
                        pltpu.VMEM((1, 128), F32)],
        compiler_params=pltpu.CompilerParams(dimension_semantics=("arbitrary", "arbitrary"),
                                             vmem_limit_bytes=VMEM_LIMIT),
        name="mlstm_prompt",
    )(proj, proj, proj, grow, c0, n0, m0, blane, bsub, gn)


def _decode_kernel(proj_ref, cs_ref, s0_ref, c0_ref, n0_ref, m0_ref, cw_ref, plane_ref, gng_ref, gnm_ref,
                   og_ref, om_ref, sout_ref, cout_ref, nout_ref, mout_ref, row_ref, *, bb):
    lane = lax.broadcasted_iota(jnp.int32, (1, 128), 1)
    gt = proj_ref[:, COL_GATE:COL_GATE + 128]
    beta_t = _sigmoid(gt)
    g_t = -jnp.exp(plane_ref[0:1, :]) * _softplus(gt + plane_ref[1:2, :])
    eg_t = jnp.exp(g_t)
    gb_t = gt + plane_ref[2:3, :]
    lf_t = -_softplus(-gb_t)

    for h in range(HEADS):
        cols = []
        for part in range(3):
            c0 = part * 512 + h * GDN_D
            w = cw_ref[:, c0:c0 + GDN_D]
            acc = cs_ref[:, 0, c0:c0 + GDN_D] * w[0:1, :]
            acc = acc + cs_ref[:, 1, c0:c0 + GDN_D] * w[1:2, :]
            acc = acc + cs_ref[:, 2, c0:c0 + GDN_D] * w[2:3, :]
            acc = acc + proj_ref[:, COL_QKV + c0:COL_QKV + c0 + GDN_D] * w[3:4, :]
            cols.append(acc * _sigmoid(acc))
        q, k, v = cols
        q = q * lax.rsqrt(jnp.sum(q * q, axis=-1, keepdims=True) + EPS) * (GDN_D ** -0.5)
        k = k * lax.rsqrt(jnp.sum(k * k, axis=-1, keepdims=True) + EPS)
        qk = jnp.sum(q * k, axis=-1, keepdims=True)
        q_t = q.T
        k_t = k.T
        beta = beta_t[:, LANE_GB + h:LANE_GB + h + 1]
        eg = eg_t[:, LANE_GA + h:LANE_GA + h + 1]
        for b in range(bb):
            s_bh = s0_ref[b, h]
            k_col = k_t[:, b:b + 1]
            q_col = q_t[:, b:b + 1]
            ks = jnp.sum(k_col * s_bh, axis=0, keepdims=True)
            qs = jnp.sum(q_col * s_bh, axis=0, keepdims=True)
            eg_b = eg[b:b + 1, :]
            u = beta[b:b + 1, :] * (v[b:b + 1, :] - eg_b * ks)
            row_ref[b:b + 1, :] = eg_b * qs + qk[b:b + 1, :] * u
            sout_ref[b, h] = eg_b * s_bh + k_col * u
        z = proj_ref[:, COL_Z + h * GDN_D:COL_Z + (h + 1) * GDN_D]
        out = _rms(row_ref[...], gng_ref[...]) * (z * _sigmoid(z))
        og_ref[:, h * GDN_D:(h + 1) * GDN_D] = out.astype(og_ref.dtype)

    for p in range(HEADS // 2):
        qb = proj_ref[:, COL_MQ + p * 128:COL_MQ + (p + 1) * 128]
        kb = proj_ref[:, COL_MK + p * 128:COL_MK + (p + 1) * 128] * (ML_DK ** -0.5)
        n_p = n0_ref[:, p * 128:(p + 1) * 128]
        q_t = qb.T
        k_t = kb.T
        w_prev, p_in, qkp, floor_ = [], [], [], []
        for e in range(2):
            h = 2 * p + e
            lm = (lane >= e * ML_DK) & (lane < (e + 1) * ML_DK)
            ig = gb_t[:, LANE_MI + h:LANE_MI + h + 1]
            lf = lf_t[:, LANE_MF + h:LANE_MF + h + 1]
            m_old = m0_ref[:, h:h + 1]
            m_new = jnp.maximum(lf + m_old, ig)
            w_prev.append(jnp.exp(lf + m_old - m_new))
            p_in.append(jnp.exp(ig - m_new))
            floor_.append(jnp.exp(-m_new))
            qk = jnp.sum(jnp.where(lm, qb * kb, 0.0), axis=-1, keepdims=True)
            qn = jnp.sum(jnp.where(lm, qb * n_p, 0.0), axis=-1, keepdims=True)
            qkp.append((p_in[e] * qk, w_prev[e] * qn))
            mout_ref[:, h:h + 1] = m_new
        lo_lane = lane < ML_DK
        nout_ref[:, p * 128:(p + 1) * 128] = (jnp.where(lo_lane, w_prev[0], w_prev[1]) * n_p
                                              + jnp.where(lo_lane, p_in[0], p_in[1]) * kb)
        for e in range(2):
            h = 2 * p + e
            v = proj_ref[:, COL_MV + h * ML_DV:COL_MV + (h + 1) * ML_DV]
            rs = slice(e * ML_DK, (e + 1) * ML_DK)
            for b in range(bb):
                c_bh = c0_ref[b, p, rs, :]
                q_col = q_t[rs, b:b + 1]
                k_col = k_t[rs, b:b + 1]
                qc = jnp.sum(q_col * c_bh, axis=0, keepdims=True)
                wp = w_prev[e][b:b + 1, :]
                pi = p_in[e][b:b + 1, :]
                pqk, wqn = qkp[e][0][b:b + 1, :], qkp[e][1][b:b + 1, :]
                v_row = v[b:b + 1, :]
                num = wp * qc + pqk * v_row
                den = wqn + pqk
                row_ref[b:b + 1, :] = num / jnp.maximum(jnp.abs(den), floor_[e][b:b + 1, :])
                cout_ref[b, p, rs, :] = wp * c_bh + (pi * k_col) * v_row
            mo = proj_ref[:, COL_MO + h * ML_DV:COL_MO + (h + 1) * ML_DV]
            out = _sigmoid(mo) * _rms(row_ref[...], gnm_ref[...])
            om_ref[:, h * ML_DV:(h + 1) * ML_DV] = out.astype(om_ref.dtype)
    mout_ref[:, HEADS:128] = m0_ref[:, HEADS:128]


def _decode(proj, cs, s0, c0, n0, m0, conv_w, plane, gng, gnm, bb):
    b = proj.shape[0]
    hp = HEADS // 2
    return pl.pallas_call(
        functools.partial(_decode_kernel, bb=bb),
        grid=(b // bb,),
        in_specs=[pl.BlockSpec((bb, N_PROJ), lambda i: (i, 0)),
                  pl.BlockSpec((bb, CONV_W - 1, CONV_CH), lambda i: (i, 0, 0)),
                  pl.BlockSpec((bb, HEADS, GDN_D, GDN_D), lambda i: (i, 0, 0, 0)),
                  pl.BlockSpec((bb, hp, 128, 128), lambda i: (i, 0, 0, 0)),
                  pl.BlockSpec((bb, HEADS * ML_DK), lambda i: (i, 0)),
                  pl.BlockSpec((bb, 128), lambda i: (i, 0)),
                  pl.BlockSpec((CONV_W, CONV_CH), lambda i: (0, 0)),
                  pl.BlockSpec((3, 128), lambda i: (0, 0)),
                  pl.BlockSpec((1, GDN_D), lambda i: (0, 0)),
                  pl.BlockSpec((1, ML_DV), lambda i: (0, 0))],
        out_specs=[pl.BlockSpec((bb, HEADS * GDN_D), lambda i: (i, 0)),
                   pl.BlockSpec((bb, HEADS * ML_DV), lambda i: (i, 0)),
                   pl.BlockSpec((bb, HEADS, GDN_D, GDN_D), lambda i: (i, 0, 0, 0)),
                   pl.BlockSpec((bb, hp, 128, 128), lambda i: (i, 0, 0, 0)),
                   pl.BlockSpec((bb, HEADS * ML_DK), lambda i: (i, 0)),
                   pl.BlockSpec((bb, 128), lambda i: (i, 0))],
        out_shape=[jax.ShapeDtypeStruct((b, HEADS * GDN_D), BF16),
                   jax.ShapeDtypeStruct((b, HEADS * ML_DV), BF16),
                   jax.ShapeDtypeStruct((b, HEADS, GDN_D, GDN_D), F32),
                   jax.ShapeDtypeStruct((b, hp, 128, 128), F32),
                   jax.ShapeDtypeStruct((b, HEADS * ML_DK), F32),
                   jax.ShapeDtypeStruct((b, 128), F32)],
        scratch_shapes=[pltpu.VMEM((bb, 128), F32)],
        compiler_params=pltpu.CompilerParams(dimension_semantics=("arbitrary",),
                                             vmem_limit_bytes=VMEM_LIMIT),
        name="decode_step",
    )(proj, cs, s0, c0, n0, m0, conv_w, plane, gng, gnm)


def _regroup_w_in(w_in):
    sizes = (512, 512, 512, 512, 4, 4, 256, 256, 512, 512, 4, 4)
    offs = [0]
    for s in sizes:
        offs.append(offs[-1] + s)
    gq, gk, gv, gz, gb, ga, mq, mk, mv, mo, mi, mf = (w_in[:, offs[i]:offs[i + 1]] for i in range(12))
    pad = jnp.zeros((w_in.shape[0], 128 - 16), w_in.dtype)
    return jnp.concatenate([gq, gk, gv, gz, mv, mo, mq, mk, gb, ga, mi, mf, pad], axis=1).astype(BF16)


def _lane_vec(pairs):
    v = jnp.zeros((128,), F32)
    for off, val in pairs:
        v = v.at[off:off + HEADS].set(val.astype(F32))
    return v


def _prep_params(norm_pre_mix, w_in, conv_w, a_log, dt_bias, gdn_norm_g, b_igate, b_fgate, mlstm_norm_g,
                 w_out, norm_post_mix, norm_pre_mlp, w_up, w_down, norm_post_mlp):
    alog_v = _lane_vec([(LANE_GA, a_log[0])])
    dtb_v = _lane_vec([(LANE_GA, dt_bias[0])])
    bias_v = _lane_vec([(LANE_MI, b_igate[0]), (LANE_MF, b_fgate[0])])
    return dict(
        w_in=_regroup_w_in(w_in[0]), wo=w_out[0].astype(BF16), wu=w_up[0].astype(BF16),
        wd=w_down[0].astype(BF16),
        g_pre=norm_pre_mix[0][None, :], g1=norm_post_mix[0][None, :], g2=norm_pre_mlp[0][None, :],
        g3=norm_post_mlp[0][None, :], cw=conv_w[0], gng=gdn_norm_g[0][None, :], gnm=mlstm_norm_g[0][None, :],
        gdn_plane=jnp.stack([alog_v, dtb_v]),
        gdn_psub=jnp.stack([alog_v[:16], dtb_v[:16]], axis=1),
        ml_blane=bias_v[None, :],
        ml_bsub=bias_v[:16, None],
        dec_plane=jnp.stack([alog_v, dtb_v, bias_v]),
    )


def _prompt_path(x, conv0, s0, c0, n0, m0, prm, tb, tm):
    bsz, seq, d = x.shape
    hp = HEADS // 2
    x2d = x.reshape(bsz * seq, d)
    proj = _in_proj(x2d, prm["g_pre"], prm["w_in"], tm=tm).reshape(bsz, seq, N_PROJ)
    gates = proj[:, :, COL_GATE:COL_GATE + 16]
    grow = gates.reshape(bsz, seq // CHUNK, CHUNK, 16).transpose(0, 1, 3, 2)
    cs8 = jnp.pad(conv0, ((0, 0), (8 - (CONV_W - 1), 0), (0, 0)))
    og, s_new = _gdn_prompt(proj, grow, cs8, s0, prm["cw"], prm["gdn_plane"], prm["gdn_psub"], prm["gng"], tb=tb)
    om, c_new, n_new, m_new = _mlstm_prompt(
        proj, grow, c0.reshape(bsz, hp, 128, 128), n0.reshape(bsz, hp, 1, 128),
        jnp.pad(m0, ((0, 0), (0, 128 - HEADS)))[:, None, :], prm["ml_blane"], prm["ml_bsub"], prm["gnm"], tb=tb)
    y = _out_mlp(og.reshape(bsz * seq, -1), om.reshape(bsz * seq, -1), x2d, prm["wo"], prm["g1"], prm["g2"],
                 prm["wu"], prm["wd"], prm["g3"], tm=tm).reshape(bsz, seq, d)
    xp = jnp.concatenate([conv0, proj[:, seq - (CONV_W - 1):, COL_QKV:COL_QKV + CONV_CH]], axis=1)
    conv_new = xp[:, -(CONV_W - 1):]
    return y, (conv_new, s_new, c_new.reshape(bsz, HEADS, ML_DK, ML_DV), n_new.reshape(bsz, HEADS, ML_DK),
               m_new[:, 0, :HEADS])


def _sample_path(x, conv0, s0, c0, n0, m0, prm):
    dec, _, d = x.shape
    hp = HEADS // 2
    xs = x.reshape(dec, d)
    proj = _in_proj(xs, prm["g_pre"], prm["w_in"], tm=dec)
    og, om, s_new, c_new, n_new, m_new = _decode(
        proj, conv0, s0, c0.reshape(dec, hp, 128, 128), n0.reshape(dec, HEADS * ML_DK),
        jnp.pad(m0, ((0, 0), (0, 128 - HEADS))), prm["cw"], prm["dec_plane"], prm["gng"], prm["gnm"], bb=8)
    y = _out_mlp(og, om, xs, prm["wo"], prm["g1"], prm["g2"], prm["wu"], prm["wd"], prm["g3"],
                 tm=dec).reshape(dec, 1, d)
    conv_new = jnp.concatenate([conv0[:, 1:, :], proj[:, None, COL_QKV:COL_QKV + CONV_CH]], axis=1)
    return y, (conv_new, s_new, c_new.reshape(dec, HEADS, ML_DK, ML_DV), n_new.reshape(dec, HEADS, ML_DK),
               m_new[:, :HEADS])


def kernel(x_prompt, x_sample, state_gdn_conv, state_gdn_S, state_mlstm_C, state_mlstm_n, state_mlstm_m,
           norm_pre_mix, w_in, conv_w, a_log, dt_bias, gdn_norm_g, b_igate, b_fgate, mlstm_norm_g, w_out,
           norm_post_mix, norm_pre_mlp, w_up, w_down, norm_post_mlp):
    bsz = x_prompt.shape[0]
    prm = _prep_params(norm_pre_mix, w_in, conv_w, a_log, dt_bias, gdn_norm_g, b_igate, b_fgate, mlstm_norm_g,
                       w_out, norm_post_mix, norm_pre_mlp, w_up, w_down, norm_post_mlp)
    y_p, p_st = _prompt_path(
        x_prompt, jnp.zeros((bsz, CONV_W - 1, CONV_CH), F32), jnp.zeros((bsz, HEADS, GDN_D, GDN_D), F32),
        jnp.zeros((bsz, HEADS, ML_DK, ML_DV), F32), jnp.zeros((bsz, HEADS, ML_DK), F32),
        jnp.zeros((bsz, HEADS), F32), prm, tb=512, tm=512)
    y_s, s_st = _sample_path(x_sample, state_gdn_conv[0], state_gdn_S[0], state_mlstm_C[0], state_mlstm_n[0],
                             state_mlstm_m[0], prm)
    return (y_p, y_s) + tuple(a[None] for a in p_st) + tuple(a[None] for a in s_st)
```

```python
import functools

import jax
import jax.numpy as jnp
from jax import lax
from jax.experimental import pallas as pl
from jax.experimental.pallas import tpu as pltpu

F32 = jnp.float32
BF16 = jnp.bfloat16
EPS = 1e-6

D_MODEL = 1024
HEADS = 4
GDN_D = 128
ML_DK = 64
ML_DV = 128
CONV_W = 4
CONV_CH = 3 * HEADS * GDN_D
D_FF = 4 * D_MODEL
CHUNK = 64

COL_QKV = 0
COL_Z = 1536
COL_MV = 2048
COL_MO = 2560
COL_MQ = 3072
COL_MK = 3328
COL_GATE = 3584
N_PROJ = COL_GATE + 128
LANE_GB, LANE_GA, LANE_MI, LANE_MF = 0, 4, 8, 12

VMEM_LIMIT = 56 * 1024 * 1024


def _rms(x, g):
    return x * lax.rsqrt(jnp.mean(x * x, axis=-1, keepdims=True) + EPS) * g


def _softplus(x):
    return jnp.maximum(x, 0.0) + jnp.log1p(jnp.exp(-jnp.abs(x)))


def _sigmoid(x):
    return 1.0 / (1.0 + jnp.exp(-x))


def _mm(a, b):
    return jnp.dot(a.astype(BF16), b.astype(BF16), preferred_element_type=F32)


def _mm_nt(a, b):
    return lax.dot_general(a.astype(BF16), b.astype(BF16), (((1,), (1,)), ((), ())),
                           preferred_element_type=F32)


def _in_proj_kernel(x_ref, g_ref, w_ref, o_ref, h_ref, *, n_chunk):
    h_ref[...] = _rms(x_ref[...], g_ref[...]).astype(BF16)
    n = o_ref.shape[1]
    for c0 in range(0, n, n_chunk):
        c1 = min(c0 + n_chunk, n)
        o_ref[:, c0:c1] = jnp.dot(h_ref[...], w_ref[:, c0:c1], preferred_element_type=F32)


def _in_proj(x2d, g, w, tm):
    m, k = x2d.shape
    n = w.shape[1]
    return pl.pallas_call(
        functools.partial(_in_proj_kernel, n_chunk=512),
        grid=(m // tm,),
        in_specs=[pl.BlockSpec((tm, k), lambda i: (i, 0)),
                  pl.BlockSpec((1, k), lambda i: (0, 0)),
                  pl.BlockSpec((k, n), lambda i: (0, 0))],
        out_specs=pl.BlockSpec((tm, n), lambda i: (i, 0)),
        out_shape=jax.ShapeDtypeStruct((m, n), F32),
        scratch_shapes=[pltpu.VMEM((tm, k), BF16)],
        compiler_params=pltpu.CompilerParams(dimension_semantics=("arbitrary",),
                                             vmem_limit_bytes=VMEM_LIMIT),
        name="in_proj",
    )(x2d, g, w)


def _out_mlp_kernel(og_ref, om_ref, x_ref, wo_ref, g1_ref, g2_ref, wu_ref, wd_ref, g3_ref,
                    y_ref, hn_ref, acc_ref, *, ff_chunk):
    half = og_ref.shape[1]
    mix = (jnp.dot(og_ref[...], wo_ref[0:half, :], preferred_element_type=F32)
           + jnp.dot(om_ref[...], wo_ref[half:2 * half, :], preferred_element_type=F32))
    x1 = x_ref[...] + _rms(mix, g1_ref[...])
    y_ref[...] = x1
    hn_ref[...] = _rms(x1, g2_ref[...]).astype(BF16)
    d_ff = wu_ref.shape[1]
    for c0 in range(0, d_ff, ff_chunk):
        u = jnp.dot(hn_ref[...], wu_ref[:, c0:c0 + ff_chunk], preferred_element_type=F32)
        u = jnp.square(jnp.maximum(u, 0.0)).astype(BF16)
        d = jnp.dot(u, wd_ref[c0:c0 + ff_chunk, :], preferred_element_type=F32)
        if c0 == 0:
            acc_ref[...] = d
        else:
            acc_ref[...] += d
    y_ref[...] = y_ref[...] + _rms(acc_ref[...], g3_ref[...])


def _out_mlp(og, om, x2d, wo, g1, g2, wu, wd, g3, tm):
    m, d = x2d.shape
    half = og.shape[1]
    d_ff = wu.shape[1]
    const = lambda i: (0, 0)
    return pl.pallas_call(
        functools.partial(_out_mlp_kernel, ff_chunk=1024),
        grid=(m // tm,),
        in_specs=[pl.BlockSpec((tm, half), lambda i: (i, 0)),
                  pl.BlockSpec((tm, half), lambda i: (i, 0)),
                  pl.BlockSpec((tm, d), lambda i: (i, 0)),
                  pl.BlockSpec((d, d), const, pipeline_mode=pl.Buffered(1)),
                  pl.BlockSpec((1, d), const),
                  pl.BlockSpec((1, d), const),
                  pl.BlockSpec((d, d_ff), const, pipeline_mode=pl.Buffered(1)),
                  pl.BlockSpec((d_ff, d), const, pipeline_mode=pl.Buffered(1)),
                  pl.BlockSpec((1, d), const)],
        out_specs=pl.BlockSpec((tm, d), lambda i: (i, 0)),
        out_shape=jax.ShapeDtypeStruct((m, d), F32),
        scratch_shapes=[pltpu.VMEM((tm, d), BF16), pltpu.VMEM((tm, d), F32)],
        compiler_params=pltpu.CompilerParams(dimension_semantics=("arbitrary",),
                                             vmem_limit_bytes=VMEM_LIMIT),
        name="out_mlp",
    )(og, om, x2d, wo, g1, g2, wu, wd, g3)


def _chunk_masks():
    ii = lax.broadcasted_iota(jnp.int32, (CHUNK, CHUNK), 0)
    jj = lax.broadcasted_iota(jnp.int32, (CHUNK, CHUNK), 1)
    return ii, jj


def _cumsum_col_row(x_col, x_row, ii, jj):
    c_col = jnp.sum(jnp.where(jj <= ii, x_row, 0.0), axis=1, keepdims=True)
    c_row = jnp.sum(jnp.where(ii <= jj, x_col, 0.0), axis=0, keepdims=True)
    return c_col, c_row


def _gdn_kernel(qkvz_ref, gcol_ref, grow_ref, cs_ref, s0_ref, cw_ref, plane_ref, psub_ref, gn_ref,
                o_ref, sout_ref, xp_ref, yc_ref, s_ref, wv_ref, lhs_ref, qk_ref, ket_ref, ge_ref,
                *, tb, unroll_a):
    t = pl.program_id(1)
    nt = pl.num_programs(1)

    @pl.when(t == 0)
    def _():
        xp_ref[0:8, :] = cs_ref[...]
        s_ref[...] = s0_ref[...]

    @pl.when(t > 0)
    def _():
        xp_ref[0:8, :] = xp_ref[tb:tb + 8, :]

    xp_ref[8:tb + 8, :] = qkvz_ref[:, COL_QKV:COL_QKV + CONV_CH]

    def conv_tile(ct, carry):
        c0 = pl.multiple_of(ct * 128, 128)
        w = cw_ref[:, pl.ds(c0, 128)]
        for r0 in range(0, tb, 128):
            acc = xp_ref[5 + r0:5 + r0 + 128, pl.ds(c0, 128)] * w[0:1, :]
            for j in range(1, CONV_W):
                acc = acc + xp_ref[5 + j + r0:5 + j + r0 + 128, pl.ds(c0, 128)] * w[j:j + 1, :]
            yc_ref[r0:r0 + 128, pl.ds(c0, 128)] = acc * _sigmoid(acc)
        return carry

    lax.fori_loop(0, CONV_CH // 128, conv_tile, 0)

    ii, jj = _chunk_masks()
    incl = jj <= ii
    strict = jj < ii
    eye = (ii == jj).astype(F32)
    neg_a_lane = -jnp.exp(plane_ref[0:1, :])
    dtb_lane = plane_ref[1:2, :]
    neg_a_sub = -jnp.exp(psub_ref[:, 0:1])
    dtb_sub = psub_ref[:, 1:2]
    gn = gn_ref[...]

    def phase_a(ci, carry):
        items = []
        for cu in range(unroll_a):
            c = ci * unroll_a + cu
            r0 = pl.multiple_of(c * CHUNK, CHUNK)
            gc = gcol_ref[pl.ds(r0, CHUNK), :]
            gr = grow_ref[c]
            beta_t = _sigmoid(gc)
            g_t = neg_a_lane * _softplus(gc + dtb_lane)
            g_r = neg_a_sub * _softplus(gr + dtb_sub)
            for h in range(HEADS):
                lo = h * GDN_D
                items.append(dict(c=c, h=h,
                                  q=yc_ref[pl.ds(r0, CHUNK), lo:lo + GDN_D],
                                  k=yc_ref[pl.ds(r0, CHUNK), 512 + lo:512 + lo + GDN_D],
                                  v=yc_ref[pl.ds(r0, CHUNK), 1024 + lo:1024 + lo + GDN_D],
                                  beta=beta_t[:, LANE_GB + h:LANE_GB + h + 1],
                                  gg_col=g_t[:, LANE_GA + h:LANE_GA + h + 1],
                                  gg_row=g_r[LANE_GA + h:LANE_GA + h + 1, :]))
        for it in items:
            it["k"] = it["k"] * lax.rsqrt(jnp.sum(it["k"] * it["k"], axis=-1, keepdims=True) + EPS)
        for it in items:
            it["kk"] = _mm_nt(it["k"], it["k"])
        for it in items:
            it["q"] = (it["q"] * lax.rsqrt(jnp.sum(it["q"] * it["q"], axis=-1, keepdims=True) + EPS)
                       * (GDN_D ** -0.5))
        for it in items:
            it["g_col"], g_row = _cumsum_col_row(it["gg_col"], it["gg_row"], ii, jj)
            it["decay"] = jnp.where(incl, jnp.exp(jnp.where(incl, it["g_col"] - g_row, 0.0)), 0.0)
        for it in items:
            n_mat = jnp.where(strict, it["beta"] * it["kk"] * it["decay"], 0.0)
            it["x"] = eye - n_mat
            it["p"] = -n_mat
        for _ in range(5):
            for it in items:
                it["p"] = _mm(it["p"], it["p"])
            for it in items:
                it["x"] = it["x"] + _mm(it["x"], it["p"])
        for it in items:
            e_g = jnp.exp(it["g_col"])
            it["e_g"] = e_g
            rhs = jnp.concatenate([it["beta"] * it["v"], (it["beta"] * e_g) * it["k"]], axis=1)
            it["w"] = _mm(it["x"], rhs)
        for it in items:
            it["qk"] = _mm_nt(it["q"], it["k"]) * it["decay"]
        for it in items:
            c, h = it["c"], it["h"]
            g_end = it["g_col"][CHUNK - 1:CHUNK, :]
            k_end = it["k"] * jnp.exp(g_end - it["g_col"])
            wv_ref[c, h] = it["w"][:, 0:GDN_D]
            lhs_ref[c, h] = jnp.concatenate([it["w"][:, GDN_D:2 * GDN_D], it["e_g"] * it["q"]],
                                            axis=0).astype(BF16)
            qk_ref[c, h] = it["qk"].astype(BF16)
            ket_ref[c, h] = k_end.T.astype(BF16)
            ge_ref[c, h] = jnp.broadcast_to(jnp.exp(g_end), (1, GDN_D))
        return carry

    lax.fori_loop(0, tb // CHUNK // unroll_a, phase_a, 0)

    def phase_b(c, carry):
        r0 = pl.multiple_of(c * CHUNK, CHUNK)
        s = [s_ref[h] for h in range(HEADS)]
        r = [jnp.dot(lhs_ref[c, h], s[h].astype(BF16), preferred_element_type=F32) for h in range(HEADS)]
        ub = [(wv_ref[c, h] - r[h][0:CHUNK]).astype(BF16) for h in range(HEADS)]
        o = [r[h][CHUNK:2 * CHUNK] + jnp.dot(qk_ref[c, h], ub[h], preferred_element_type=F32)
             for h in range(HEADS)]
        for h in range(HEADS):
            s_ref[h] = ge_ref[c, h] * s[h] + jnp.dot(ket_ref[c, h], ub[h], preferred_element_type=F32)
        for h in range(HEADS):
            lo = h * GDN_D
            z = qkvz_ref[pl.ds(r0, CHUNK), COL_Z + lo:COL_Z + lo + GDN_D]
            out = _rms(o[h], gn) * (z * _sigmoid(z))
            o_ref[pl.ds(r0, CHUNK), lo:lo + GDN_D] = out.astype(o_ref.dtype)
        return carry

    lax.fori_loop(0, tb // CHUNK, phase_b, 0)

    @pl.when(t == nt - 1)
    def _():
        sout_ref[...] = s_ref[...]


def _gdn_prompt(proj, grow, cs8, s0, conv_w, plane, psub, gn, tb, unroll_a=8):
    b, t, _ = proj.shape
    ncb = tb // CHUNK
    return pl.pallas_call(
        functools.partial(_gdn_kernel, tb=tb, unroll_a=unroll_a),
        grid=(b, t // tb),
        in_specs=[pl.BlockSpec((None, tb, 2048), lambda i, j: (i, j, 0)),
                  pl.BlockSpec((None, tb, 128), lambda i, j: (i, j, COL_GATE // 128)),
                  pl.BlockSpec((None, ncb, 16, CHUNK), lambda i, j: (i, j, 0, 0)),
                  pl.BlockSpec((None, 8, CONV_CH), lambda i, j: (i, 0, 0)),
                  pl.BlockSpec((None, HEADS, GDN_D, GDN_D), lambda i, j: (i, 0, 0, 0)),
                  pl.BlockSpec((CONV_W, CONV_CH), lambda i, j: (0, 0)),
                  pl.BlockSpec((2, 128), lambda i, j: (0, 0)),
                  pl.BlockSpec((16, 2), lambda i, j: (0, 0)),
                  pl.BlockSpec((1, GDN_D), lambda i, j: (0, 0))],
        out_specs=[pl.BlockSpec((None, tb, HEADS * GDN_D), lambda i, j: (i, j, 0)),
                   pl.BlockSpec((None, HEADS, GDN_D, GDN_D), lambda i, j: (i, 0, 0, 0))],
        out_shape=[jax.ShapeDtypeStruct((b, t, HEADS * GDN_D), BF16),
                   jax.ShapeDtypeStruct((b, HEADS, GDN_D, GDN_D), F32)],
        scratch_shapes=[pltpu.VMEM((tb + 8, CONV_CH), F32),
                        pltpu.VMEM((tb, CONV_CH), F32),
                        pltpu.VMEM((HEADS, GDN_D, GDN_D), F32),
                        pltpu.VMEM((ncb, HEADS, CHUNK, GDN_D), F32),
                        pltpu.VMEM((ncb, HEADS, 2 * CHUNK, GDN_D), BF16),
                        pltpu.VMEM((ncb, HEADS, CHUNK, CHUNK), BF16),
                        pltpu.VMEM((ncb, HEADS, GDN_D, CHUNK), BF16),
                        pltpu.VMEM((ncb, HEADS, 1, GDN_D), F32)],
        compiler_params=pltpu.CompilerParams(dimension_semantics=("arbitrary", "arbitrary"),
                                             vmem_limit_bytes=VMEM_LIMIT),
        name="gdn_prompt",
    )(proj, proj, grow, cs8, s0, conv_w, plane, psub, gn)


def _mlstm_kernel(mvo_ref, mqk_ref, gcol_ref, grow_ref, c0_ref, n0_ref, m0_ref, blane_ref, bsub_ref,
                  gn_ref, h_ref, cout_ref, nout_ref, mout_ref, c_ref, n_ref, m_ref, *, tb, unroll):
    t = pl.program_id(1)
    nt = pl.num_programs(1)

    @pl.when(t == 0)
    def _():
        c_ref[...] = c0_ref[...]
        n_ref[...] = n0_ref[...]
        m_ref[...] = m0_ref[...]

    ii, jj = _chunk_masks()
    incl = jj <= ii
    lane = lax.broadcasted_iota(jnp.int32, (1, 128), 1)
    row128 = lax.broadcasted_iota(jnp.int32, (128, 1), 0)
    gn = gn_ref[...]
    blane = blane_ref[...]
    bsub = bsub_ref[...]

    def chunk_body(ci, carry):
        items = []
        for cu in range(unroll):
            c = ci * unroll + cu
            r0 = pl.multiple_of(c * CHUNK, CHUNK)
            gc = gcol_ref[pl.ds(r0, CHUNK), :] + blane
            gr = grow_ref[c] + bsub
            lf_c = -_softplus(-gc)
            lf_r = -_softplus(-gr)
            for p in range(HEADS // 2):
                qb = mqk_ref[pl.ds(r0, CHUNK), p * 128:(p + 1) * 128]
                kb = mqk_ref[pl.ds(r0, CHUNK), 256 + p * 128:256 + (p + 1) * 128] * (ML_DK ** -0.5)
                for e in range(2):
                    h = 2 * p + e
                    lm = (lane >= e * ML_DK) & (lane < (e + 1) * ML_DK)
                    items.append(dict(cu=cu, r0=r0, p=p, e=e, h=h, qh=jnp.where(lm, qb, 0.0), kb=kb,
                                      kh=jnp.where(lm, kb, 0.0),
                                      lf_col=lf_c[:, LANE_MF + h:LANE_MF + h + 1],
                                      lf_row=lf_r[LANE_MF + h:LANE_MF + h + 1, :],
                                      ig_row=gr[LANE_MI + h:LANE_MI + h + 1, :],
                                      ig_col=gc[:, LANE_MI + h:LANE_MI + h + 1],
                                      v=mvo_ref[pl.ds(r0, CHUNK), h * ML_DV:(h + 1) * ML_DV]))
        for it in items:
            it["qk"] = _mm_nt(it["qh"], it["kb"])
        for it in items:
            it["f_col"], it["f_row"] = _cumsum_col_row(it["lf_col"], it["lf_row"], ii, jj)
        for it in items:
            it["d_mat"] = jnp.where(incl, it["f_col"] - it["f_row"] + it["ig_row"], -jnp.inf)
            it["d_max"] = jnp.max(it["d_mat"], axis=1, keepdims=True)
        for it in items:
            p0 = jnp.where(incl, jnp.exp(jnp.where(incl, it["d_mat"] - it["d_max"], 0.0)), 0.0)
            pend0 = jnp.exp(it["f_col"][CHUNK - 1:CHUNK, :] - it["f_col"] + it["ig_col"]
                            - it["d_max"][CHUNK - 1:CHUNK, :])
            it["kp0"] = it["kh"] * pend0
            it["pqk0"] = p0 * it["qk"]
        for it in items:
            it["pv0"] = _mm(it["pqk0"], it["v"])
            it["rs0"] = jnp.sum(it["pqk0"], axis=-1, keepdims=True)
        for it in items:
            it["cadd0"] = _mm(it["kp0"].T, it["v"])
            it["nadd0"] = jnp.sum(it["kp0"], axis=0, keepdims=True)

        m_cur = [m_ref[:, h:h + 1] for h in range(HEADS)]
        for it in items:
            h = it["h"]
            it["m_prev"] = m_cur[h]
            m_cur[h] = jnp.maximum(it["f_col"][CHUNK - 1:CHUNK, :] + m_cur[h], it["d_max"][CHUNK - 1:CHUNK, :])
        for it in items:
            bcol = it["f_col"] + it["m_prev"]
            mt = jnp.maximum(bcol, it["d_max"])
            it["mt"] = mt
            it["w_prev"] = jnp.exp(bcol - mt)
            it["sc"] = jnp.exp(it["d_max"] - mt)
        c_cur = [c_ref[p] for p in range(HEADS // 2)]
        n_cur = [n_ref[p] for p in range(HEADS // 2)]
        for cu in range(unroll):
            for p in range(HEADS // 2):
                pair = [it for it in items if it["cu"] == cu and it["p"] == p]
                w_end = [it["w_prev"][CHUNK - 1:CHUNK, :] for it in pair]
                s_end = [it["sc"][CHUNK - 1:CHUNK, :] for it in pair]
                for it in pair:
                    it["c_prev"] = c_cur[p]
                    it["n_prev"] = n_cur[p]
                c_cur[p] = (jnp.where(row128 < ML_DK, w_end[0], w_end[1]) * c_cur[p]
                            + s_end[0] * pair[0]["cadd0"] + s_end[1] * pair[1]["cadd0"])
                n_cur[p] = (jnp.where(lane < ML_DK, w_end[0], w_end[1]) * n_cur[p]
                            + s_end[0] * pair[0]["nadd0"] + s_end[1] * pair[1]["nadd0"])
        for it in items:
            it["qc"] = _mm(it["qh"], it["c_prev"])
        for it in items:
            num = it["w_prev"] * it["qc"] + it["sc"] * it["pv0"]
            den = (it["w_prev"] * jnp.sum(it["qh"] * it["n_prev"], axis=-1, keepdims=True)
                   + it["sc"] * it["rs0"])
            it["hh"] = num / jnp.maximum(jnp.abs(den), jnp.exp(-it["mt"]))
        for it in items:
            h = it["h"]
            mo = mvo_ref[pl.ds(it["r0"], CHUNK), 512 + h * ML_DV:512 + (h + 1) * ML_DV]
            out = _sigmoid(mo) * _rms(it["hh"], gn)
            h_ref[pl.ds(it["r0"], CHUNK), h * ML_DV:(h + 1) * ML_DV] = out.astype(h_ref.dtype)
        for p in range(HEADS // 2):
            c_ref[p] = c_cur[p]
            n_ref[p] = n_cur[p]
        for h in range(HEADS):
            m_ref[:, h:h + 1] = m_cur[h]
        return carry

    lax.fori_loop(0, tb // CHUNK // unroll, chunk_body, 0)

    @pl.when(t == nt - 1)
    def _():
        cout_ref[...] = c_ref[...]
        nout_ref[...] = n_ref[...]
        mout_ref[...] = m_ref[...]


def _mlstm_prompt(proj, grow, c0, n0, m0, blane, bsub, gn, tb, unroll=8):
    b, t, _ = proj.shape
    ncb = tb // CHUNK
    hp = HEADS // 2
    return pl.pallas_call(
        functools.partial(_mlstm_kernel, tb=tb, unroll=unroll),
        grid=(b, t // tb),
        in_specs=[pl.BlockSpec((None, tb, 1024), lambda i, j: (i, j, COL_MV // 1024)),
                  pl.BlockSpec((None, tb, 512), lambda i, j: (i, j, COL_MQ // 512)),
                  pl.BlockSpec((None, tb, 128), lambda i, j: (i, j, COL_GATE // 128)),
                  pl.BlockSpec((None, ncb, 16, CHUNK), lambda i, j: (i, j, 0, 0)),
                  pl.BlockSpec((None, hp, 128, 128), lambda i, j: (i, 0, 0, 0)),
                  pl.BlockSpec((None, hp, 1, 128), lambda i, j: (i, 0, 0, 0)),
                  pl.BlockSpec((None, 1, 128), lambda i, j: (i, 0, 0)),
                  pl.BlockSpec((1, 128), lambda i, j: (0, 0)),
                  pl.BlockSpec((16, 1), lambda i, j: (0, 0)),
                  pl.BlockSpec((1, ML_DV), lambda i, j: (0, 0))],
        out_specs=[pl.BlockSpec((None, tb, HEADS * ML_DV), lambda i, j: (i, j, 0)),
                   pl.BlockSpec((None, hp, 128, 128), lambda i, j: (i, 0, 0, 0)),
                   pl.BlockSpec((None, hp, 1, 128), lambda i, j: (i, 0, 0, 0)),
                   pl.BlockSpec((None, 1, 128), lambda i, j: (i, 0, 0))],
        out_shape=[jax.ShapeDtypeStruct((b, t, HEADS * ML_DV), BF16),
                   jax.ShapeDtypeStruct((b, hp, 128, 128), F32),
                   jax.ShapeDtypeStruct((b, hp, 1, 128), F32),
                   jax.ShapeDtypeStruct((b, 1, 128), F32)],
        scratch_shapes=[pltpu.VMEM((hp, 128, 128), F32),
                        pltpu.VMEM((hp, 1, 128), F32),
                        pltpu.VMEM((1, 128), F32)],
        compiler_params=pltpu.CompilerParams(dimension_semantics=("arbitrary", "arbitrary"),
                                             vmem_limit_bytes=VMEM_LIMIT),
        name="mlstm_prompt",
    )(proj, proj, proj, grow, c0, n0, m0, blane, bsub, gn)


def _decode_kernel(proj_ref, cs_ref, s0_ref, c0_ref, n0_ref, m0_ref, cw_ref, plane_ref, gng_ref, gnm_ref,
                   og_ref, om_ref, sout_ref, cout_ref, nout_ref, mout_ref, row_ref, *, bb):
    lane = lax.broadcasted_iota(jnp.int32, (1, 128), 1)
    gt = proj_ref[:, COL_GATE:COL_GATE + 128]
    beta_t = _sigmoid(gt)
    g_t = -jnp.exp(plane_ref[0:1, :]) * _softplus(gt + plane_ref[1:2, :])
    eg_t = jnp.exp(g_t)
    gb_t = gt + plane_ref[2:3, :]
    lf_t = -_softplus(-gb_t)

    for h in range(HEADS):
        cols = []
        for part in range(3):
            c0 = part * 512 + h * GDN_D
            w = cw_ref[:, c0:c0 + GDN_D]
            acc = cs_ref[:, 0, c0:c0 + GDN_D] * w[0:1, :]
            acc = acc + cs_ref[:, 1, c0:c0 + GDN_D] * w[1:2, :]
            acc = acc + cs_ref[:, 2, c0:c0 + GDN_D] * w[2:3, :]
            acc = acc + proj_ref[:, COL_QKV + c0:COL_QKV + c0 + GDN_D] * w[3:4, :]
            cols.append(acc * _sigmoid(acc))
        q, k, v = cols
        q = q * lax.rsqrt(jnp.sum(q * q, axis=-1, keepdims=True) + EPS) * (GDN_D ** -0.5)
        k = k * lax.rsqrt(jnp.sum(k * k, axis=-1, keepdims=True) + EPS)
        qk = jnp.sum(q * k, axis=-1, keepdims=True)
        q_t = q.T
        k_t = k.T
        beta = beta_t[:, LANE_GB + h:LANE_GB + h + 1]
        eg = eg_t[:, LANE_GA + h:LANE_GA + h + 1]
        for b in range(bb):
            s_bh = s0_ref[b, h]
            k_col = k_t[:, b:b + 1]
            q_col = q_t[:, b:b + 1]
            ks = jnp.sum(k_col * s_bh, axis=0, keepdims=True)
            qs = jnp.sum(q_col * s_bh, axis=0, keepdims=True)
            eg_b = eg[b:b + 1, :]
            u = beta[b:b + 1, :] * (v[b:b + 1, :] - eg_b * ks)
            row_ref[b:b + 1, :] = eg_b * qs + qk[b:b + 1, :] * u
            sout_ref[b, h] = eg_b * s_bh + k_col * u
        z = proj_ref[:, COL_Z + h * GDN_D:COL_Z + (h + 1) * GDN_D]
        out = _rms(row_ref[...], gng_ref[...]) * (z * _sigmoid(z))
        og_ref[:, h * GDN_D:(h + 1) * GDN_D] = out.astype(og_ref.dtype)

    for p in range(HEADS // 2):
        qb = proj_ref[:, COL_MQ + p * 128:COL_MQ + (p + 1) * 128]
        kb = proj_ref[:, COL_MK + p * 128:COL_MK + (p + 1) * 128] * (ML_DK ** -0.5)
        n_p = n0_ref[:, p * 128:(p + 1) * 128]
        q_t = qb.T
        k_t = kb.T
        w_prev, p_in, qkp, floor_ = [], [], [], []
        for e in range(2):
            h = 2 * p + e
            lm = (lane >= e * ML_DK) & (lane < (e + 1) * ML_DK)
            ig = gb_t[:, LANE_MI + h:LANE_MI + h + 1]
            lf = lf_t[:, LANE_MF + h:LANE_MF + h + 1]
            m_old = m0_ref[:, h:h + 1]
            m_new = jnp.maximum(lf + m_old, ig)
            w_prev.append(jnp.exp(lf + m_old - m_new))
            p_in.append(jnp.exp(ig - m_new))
            floor_.append(jnp.exp(-m_new))
            qk = jnp.sum(jnp.where(lm, qb * kb, 0.0), axis=-1, keepdims=True)
            qn = jnp.sum(jnp.where(lm, qb * n_p, 0.0), axis=-1, keepdims=True)
            qkp.append((p_in[e] * qk, w_prev[e] * qn))
            mout_ref[:, h:h + 1] = m_new
        lo_lane = lane < ML_DK
        nout_ref[:, p * 128:(p + 1) * 128] = (jnp.where(lo_lane, w_prev[0], w_prev[1]) * n_p
                                              + jnp.where(lo_lane, p_in[0], p_in[1]) * kb)
        for e in range(2):
            h = 2 * p + e
            v = proj_ref[:, COL_MV + h * ML_DV:COL_MV + (h + 1) * ML_DV]
            rs = slice(e * ML_DK, (e + 1) * ML_DK)
            for b in range(bb):
                c_bh = c0_ref[b, p, rs, :]
                q_col = q_t[rs, b:b + 1]
                k_col = k_t[rs, b:b + 1]
                qc = jnp.sum(q_col * c_bh, axis=0, keepdims=True)
                wp = w_prev[e][b:b + 1, :]
                pi = p_in[e][b:b + 1, :]
                pqk, wqn = qkp[e][0][b:b + 1, :], qkp[e][1][b:b + 1, :]
                v_row = v[b:b + 1, :]
                num = wp * qc + pqk * v_row
                den = wqn + pqk
                row_ref[b:b + 1, :] = num / jnp.maximum(jnp.abs(den), floor_[e][b:b + 1, :])
                cout_ref[b, p, rs, :] = wp * c_bh + (pi * k_col) * v_row
            mo = proj_ref[:, COL_MO + h * ML_DV:COL_MO + (h + 1) * ML_DV]
            out = _sigmoid(mo) * _rms(row_ref[...], gnm_ref[...])
            om_ref[:, h * ML_DV:(h + 1) * ML_DV] = out.astype(om_ref.dtype)
    mout_ref[:, HEADS:128] = m0_ref[:, HEADS:128]


def _decode(proj, cs, s0, c0, n0, m0, conv_w, plane, gng, gnm, bb):
    b = proj.shape[0]
    hp = HEADS // 2
    return pl.pallas_call(
        functools.partial(_decode_kernel, bb=bb),
        grid=(b // bb,),
        in_specs=[pl.BlockSpec((bb, N_PROJ), lambda i: (i, 0)),
                  pl.BlockSpec((bb, CONV_W - 1, CONV_CH), lambda i: (i, 0, 0)),
                  pl.BlockSpec((bb, HEADS, GDN_D, GDN_D), lambda i: (i, 0, 0, 0)),
                  pl.BlockSpec((bb, hp, 128, 128), lambda i: (i, 0, 0, 0)),
                  pl.BlockSpec((bb, HEADS * ML_DK), lambda i: (i, 0)),
                  pl.BlockSpec((bb, 128), lambda i: (i, 0)),
                  pl.BlockSpec((CONV_W, CONV_CH), lambda i: (0, 0)),
                  pl.BlockSpec((3, 128), lambda i: (0, 0)),
                  pl.BlockSpec((1, GDN_D), lambda i: (0, 0)),
                  pl.BlockSpec((1, ML_DV), lambda i: (0, 0))],
        out_specs=[pl.BlockSpec((bb, HEADS * GDN_D), lambda i: (i, 0)),
                   pl.BlockSpec((bb, HEADS * ML_DV), lambda i: (i, 0)),
                   pl.BlockSpec((bb, HEADS, GDN_D, GDN_D), lambda i: (i, 0, 0, 0)),
                   pl.BlockSpec((bb, hp, 128, 128), lambda i: (i, 0, 0, 0)),
                   pl.BlockSpec((bb, HEADS * ML_DK), lambda i: (i, 0)),
                   pl.BlockSpec((bb, 128), lambda i: (i, 0))],
        out_shape=[jax.ShapeDtypeStruct((b, HEADS * GDN_D), BF16),
                   jax.ShapeDtypeStruct((b, HEADS * ML_DV), BF16),
                   jax.ShapeDtypeStruct((b, HEADS, GDN_D, GDN_D), F32),
                   jax.ShapeDtypeStruct((b, hp, 128, 128), F32),
                   jax.ShapeDtypeStruct((b, HEADS * ML_DK), F32),
                   jax.ShapeDtypeStruct((b, 128), F32)],
        scratch_shapes=[pltpu.VMEM((bb, 128), F32)],
        compiler_params=pltpu.CompilerParams(dimension_semantics=("arbitrary",),
                                             vmem_limit_bytes=VMEM_LIMIT),
        name="decode_step",
    )(proj, cs, s0, c0, n0, m0, conv_w, plane, gng, gnm)


def _regroup_w_in(w_in):
    sizes = (512, 512, 512, 512, 4, 4, 256, 256, 512, 512, 4, 4)
    offs = [0]
    for s in sizes:
        offs.append(offs[-1] + s)
    gq, gk, gv, gz, gb, ga, mq, mk, mv, mo, mi, mf = (w_in[:, offs[i]:offs[i + 1]] for i in range(12))
    pad = jnp.zeros((w_in.shape[0], 128 - 16), w_in.dtype)
    return jnp.concatenate([gq, gk, gv, gz, mv, mo, mq, mk, gb, ga, mi, mf, pad], axis=1).astype(BF16)


def _lane_vec(pairs):
    v = jnp.zeros((128,), F32)
    for off, val in pairs:
        v = v.at[off:off + HEADS].set(val.astype(F32))
    return v


def _prep_params(norm_pre_mix, w_in, conv_w, a_log, dt_bias, gdn_norm_g, b_igate, b_fgate, mlstm_norm_g,
                 w_out, norm_post_mix, norm_pre_mlp, w_up, w_down, norm_post_mlp):
    alog_v = _lane_vec([(LANE_GA, a_log[0])])
    dtb_v = _lane_vec([(LANE_GA, dt_bias[0])])
    bias_v = _lane_vec([(LANE_MI, b_igate[0]), (LANE_MF, b_fgate[0])])
    return dict(
        w_in=_regroup_w_in(w_in[0]), wo=w_out[0].astype(BF16), wu=w_up[0].astype(BF16),
        wd=w_down[0].astype(BF16),
        g_pre=norm_pre_mix[0][None, :], g1=norm_post_mix[0][None, :], g2=norm_pre_mlp[0][None, :],
        g3=norm_post_mlp[0][None, :], cw=conv_w[0], gng=gdn_norm_g[0][None, :], gnm=mlstm_norm_g[0][None, :],
        gdn_plane=jnp.stack([alog_v, dtb_v]),
        gdn_psub=jnp.stack([alog_v[:16], dtb_v[:16]], axis=1),
        ml_blane=bias_v[None, :],
        ml_bsub=bias_v[:16, None],
        dec_plane=jnp.stack([alog_v, dtb_v, bias_v]),
    )


def _prompt_path(x, conv0, s0, c0, n0, m0, prm, tb, tm):
    bsz, seq, d = x.shape
    hp = HEADS // 2
    x2d = x.reshape(bsz * seq, d)
    proj = _in_proj(x2d, prm["g_pre"], prm["w_in"], tm=tm).reshape(bsz, seq, N_PROJ)
    gates = proj[:, :, COL_GATE:COL_GATE + 16]
    grow = gates.reshape(bsz, seq // CHUNK, CHUNK, 16).transpose(0, 1, 3, 2)
    cs8 = jnp.pad(conv0, ((0, 0), (8 - (CONV_W - 1), 0), (0, 0)))
    og, s_new = _gdn_prompt(proj, grow, cs8, s0, prm["cw"], prm["gdn_plane"], prm["gdn_psub"], prm["gng"], tb=tb)
    om, c_new, n_new, m_new = _mlstm_prompt(
        proj, grow, c0.reshape(bsz, hp, 128, 128), n0.reshape(bsz, hp, 1, 128),
        jnp.pad(m0, ((0, 0), (0, 128 - HEADS)))[:, None, :], prm["ml_blane"], prm["ml_bsub"], prm["gnm"], tb=tb)
    y = _out_mlp(og.reshape(bsz * seq, -1), om.reshape(bsz * seq, -1), x2d, prm["wo"], prm["g1"], prm["g2"],
                 prm["wu"], prm["wd"], prm["g3"], tm=tm).reshape(bsz, seq, d)
    xp = jnp.concatenate([conv0, proj[:, seq - (CONV_W - 1):, COL_QKV:COL_QKV + CONV_CH]], axis=1)
    conv_new = xp[:, -(CONV_W - 1):]
    return y, (conv_new, s_new, c_new.reshape(bsz, HEADS, ML_DK, ML_DV), n_new.reshape(bsz, HEADS, ML_DK),
               m_new[:, 0, :HEADS])


def _sample_path(x, conv0, s0, c0, n0, m0, prm):
    dec, _, d = x.shape
    hp = HEADS // 2
    xs = x.reshape(dec, d)
    proj = _in_proj(xs, prm["g_pre"], prm["w_in"], tm=dec)
    og, om, s_new, c_new, n_new, m_new = _decode(
        proj, conv0, s0, c0.reshape(dec, hp, 128, 128), n0.reshape(dec, HEADS * ML_DK),
        jnp.pad(m0, ((0, 0), (0, 128 - HEADS))), prm["cw"], prm["dec_plane"], prm["gng"], prm["gnm"], bb=8)
    y = _out_mlp(og, om, xs, prm["wo"], prm["g1"], prm["g2"], prm["wu"], prm["wd"], prm["g3"],
                 tm=dec).reshape(dec, 1, d)
    conv_new = jnp.concatenate([conv0[:, 1:, :], proj[:, None, COL_QKV:COL_QKV + CONV_CH]], axis=1)
    return y, (conv_new, s_new, c_new.reshape(dec, HEADS, ML_DK, ML_DV), n_new.reshape(dec, HEADS, ML_DK),
               m_new[:, :HEADS])


def kernel(x_prompt, x_sample, state_gdn_conv, state_gdn_S, state_mlstm_C, state_mlstm_n, state_mlstm_m,
           norm_pre_mix, w_in, conv_w, a_log, dt_bias, gdn_norm_g, b_igate, b_fgate, mlstm_norm_g, w_out,
           norm_post_mix, norm_pre_mlp, w_up, w_down, norm_post_mlp):
    bsz = x_prompt.shape[0]
    prm = _prep_params(norm_pre_mix, w_in, conv_w, a_log, dt_bias, gdn_norm_g, b_igate, b_fgate, mlstm_norm_g,
                       w_out, norm_post_mix, norm_pre_mlp, w_up, w_down, norm_post_mlp)
    y_p, p_st = _prompt_path(
        x_prompt, jnp.zeros((bsz, CONV_W - 1, CONV_CH), F32), jnp.zeros((bsz, HEADS, GDN_D, GDN_D), F32),
        jnp.zeros((bsz, HEADS, ML_DK, ML_DV), F32), jnp.zeros((bsz, HEADS, ML_DK), F32),
        jnp.zeros((bsz, HEADS), F32), prm, tb=512, tm=512)
    y_s, s_st = _sample_path(x_sample, state_gdn_conv[0], state_gdn_S[0], state_mlstm_C[0], state_mlstm_n[0],
                             state_mlstm_m[0], prm)
    return (y_p, y_s) + tuple(a[None] for a in p_st) + tuple(a[None] for a in s_st)
```

```python
import functools

import jax
import jax.numpy as jnp
from jax import lax
from jax.experimental import pallas as pl
from jax.experimental.pallas import tpu as pltpu

F32 = jnp.float32
BF16 = jnp.bfloat16
EPS = 1e-6

D_MODEL = 1024
HEADS = 4
GDN_D = 128
ML_DK = 64
ML_DV = 128
CONV_W = 4
CONV_CH = 3 * HEADS * GDN_D
D_FF = 4 * D_MODEL
CHUNK = 64

COL_QKV = 0
COL_Z = 1536
COL_MV = 2048
COL_MO = 2560
COL_MQ = 3072
COL_MK = 3328
COL_GATE = 3584
N_PROJ = COL_GATE + 128
LANE_GB, LANE_GA, LANE_MI, LANE_MF = 0, 4, 8, 12

VMEM_LIMIT = 56 * 1024 * 1024


def _rms(x, g):
    return x * lax.rsqrt(jnp.mean(x * x, axis=-1, keepdims=True) + EPS) * g


def _softplus(x):
    return jnp.maximum(x, 0.0) + jnp.log1p(jnp.exp(-jnp.abs(x)))


def _sigmoid(x):
    return 1.0 / (1.0 + jnp.exp(-x))


def _mm(a, b):
    return jnp.dot(a.astype(BF16), b.astype(BF16), preferred_element_type=F32)


def _mm_nt(a, b):
    return lax.dot_general(a.astype(BF16), b.astype(BF16), (((1,), (1,)), ((), ())),
                           preferred_element_type=F32)


def _in_proj_kernel(x_ref, g_ref, w_ref, o_ref, h_ref, *, n_chunk):
    h_ref[...] = _rms(x_ref[...], g_ref[...]).astype(BF16)
    n = o_ref.shape[1]
    for c0 in range(0, n, n_chunk):
        c1 = min(c0 + n_chunk, n)
        o_ref[:, c0:c1] = jnp.dot(h_ref[...], w_ref[:, c0:c1], preferred_element_type=F32)


def _in_proj(x2d, g, w, tm):
    m, k = x2d.shape
    n = w.shape[1]
    return pl.pallas_call(
        functools.partial(_in_proj_kernel, n_chunk=512),
        grid=(m // tm,),
        in_specs=[pl.BlockSpec((tm, k), lambda i: (i, 0)),
                  pl.BlockSpec((1, k), lambda i: (0, 0)),
                  pl.BlockSpec((k, n), lambda i: (0, 0))],
        out_specs=pl.BlockSpec((tm, n), lambda i: (i, 0)),
        out_shape=jax.ShapeDtypeStruct((m, n), F32),
        scratch_shapes=[pltpu.VMEM((tm, k), BF16)],
        compiler_params=pltpu.CompilerParams(dimension_semantics=("arbitrary",),
                                             vmem_limit_bytes=VMEM_LIMIT),
        name="in_proj",
    )(x2d, g, w)


def _out_mlp_kernel(og_ref, om_ref, x_ref, wo_ref, g1_ref, g2_ref, wu_ref, wd_ref, g3_ref,
                    y_ref, hn_ref, acc_ref, *, ff_chunk):
    half = og_ref.shape[1]
    mix = (jnp.dot(og_ref[...], wo_ref[0:half, :], preferred_element_type=F32)
           + jnp.dot(om_ref[...], wo_ref[half:2 * half, :], preferred_element_type=F32))
    x1 = x_ref[...] + _rms(mix, g1_ref[...])
    y_ref[...] = x1
    hn_ref[...] = _rms(x1, g2_ref[...]).astype(BF16)
    d_ff = wu_ref.shape[1]
    for c0 in range(0, d_ff, ff_chunk):
        u = jnp.dot(hn_ref[...], wu_ref[:, c0:c0 + ff_chunk], preferred_element_type=F32)
        u = jnp.square(jnp.maximum(u, 0.0)).astype(BF16)
        d = jnp.dot(u, wd_ref[c0:c0 + ff_chunk, :], preferred_element_type=F32)
        if c0 == 0:
            acc_ref[...] = d
        else:
            acc_ref[...] += d
    y_ref[...] = y_ref[...] + _rms(acc_ref[...], g3_ref[...])


def _out_mlp(og, om, x2d, wo, g1, g2, wu, wd, g3, tm):
    m, d = x2d.shape
    half = og.shape[1]
    d_ff = wu.shape[1]
    const = lambda i: (0, 0)
    return pl.pallas_call(
        functools.partial(_out_mlp_kernel, ff_chunk=1024),
        grid=(m // tm,),
        in_specs=[pl.BlockSpec((tm, half), lambda i: (i, 0)),
                  pl.BlockSpec((tm, half), lambda i: (i, 0)),
                  pl.BlockSpec((tm, d), lambda i: (i, 0)),
                  pl.BlockSpec((d, d), const, pipeline_mode=pl.Buffered(1)),
                  pl.BlockSpec((1, d), const),
                  pl.BlockSpec((1, d), const),
                  pl.BlockSpec((d, d_ff), const, pipeline_mode=pl.Buffered(1)),
                  pl.BlockSpec((d_ff, d), const, pipeline_mode=pl.Buffered(1)),
                  pl.BlockSpec((1, d), const)],
        out_specs=pl.BlockSpec((tm, d), lambda i: (i, 0)),
        out_shape=jax.ShapeDtypeStruct((m, d), F32),
        scratch_shapes=[pltpu.VMEM((tm, d), BF16), pltpu.VMEM((tm, d), F32)],
        compiler_params=pltpu.CompilerParams(dimension_semantics=("arbitrary",),
                                             vmem_limit_bytes=VMEM_LIMIT),
        name="out_mlp",
    )(og, om, x2d, wo, g1, g2, wu, wd, g3)


def _chunk_masks():
    ii = lax.broadcasted_iota(jnp.int32, (CHUNK, CHUNK), 0)
    jj = lax.broadcasted_iota(jnp.int32, (CHUNK, CHUNK), 1)
    return ii, jj


def _cumsum_col_row(x_col, x_row, ii, jj):
    c_col = jnp.sum(jnp.where(jj <= ii, x_row, 0.0), axis=1, keepdims=True)
    c_row = jnp.sum(jnp.where(ii <= jj, x_col, 0.0), axis=0, keepdims=True)
    return c_col, c_row


def _gdn_kernel(qkvz_ref, gcol_ref, grow_ref, cs_ref, s0_ref, cw_ref, plane_ref, psub_ref, gn_ref,
                o_ref, sout_ref, xp_ref, yc_ref, s_ref, wv_ref, lhs_ref, qk_ref, ket_ref, ge_ref,
                *, tb, unroll_a):
    t = pl.program_id(1)
    nt = pl.num_programs(1)

    @pl.when(t == 0)
    def _():
        xp_ref[0:8, :] = cs_ref[...]
        s_ref[...] = s0_ref[...]

    @pl.when(t > 0)
    def _():
        xp_ref[0:8, :] = xp_ref[tb:tb + 8, :]

    xp_ref[8:tb + 8, :] = qkvz_ref[:, COL_QKV:COL_QKV + CONV_CH]

    def conv_tile(ct, carry):
        c0 = pl.multiple_of(ct * 128, 128)
        w = cw_ref[:, pl.ds(c0, 128)]
        for r0 in range(0, tb, 128):
            acc = xp_ref[5 + r0:5 + r0 + 128, pl.ds(c0, 128)] * w[0:1, :]
            for j in range(1, CONV_W):
                acc = acc + xp_ref[5 + j + r0:5 + j + r0 + 128, pl.ds(c0, 128)] * w[j:j + 1, :]
            yc_ref[r0:r0 + 128, pl.ds(c0, 128)] = acc * _sigmoid(acc)
        return carry

    lax.fori_loop(0, CONV_CH // 128, conv_tile, 0)

    ii, jj = _chunk_masks()
    incl = jj <= ii
    strict = jj < ii
    eye = (ii == jj).astype(F32)
    neg_a_lane = -jnp.exp(plane_ref[0:1, :])
    dtb_lane = plane_ref[1:2, :]
    neg_a_sub = -jnp.exp(psub_ref[:, 0:1])
    dtb_sub = psub_ref[:, 1:2]
    gn = gn_ref[...]

    def phase_a(ci, carry):
        items = []
        for cu in range(unroll_a):
            c = ci * unroll_a + cu
            r0 = pl.multiple_of(c * CHUNK, CHUNK)
            gc = gcol_ref[pl.ds(r0, CHUNK), :]
            gr = grow_ref[c]
            beta_t = _sigmoid(gc)
            g_t = neg_a_lane * _softplus(gc + dtb_lane)
            g_r = neg_a_sub * _softplus(gr + dtb_sub)
            for h in range(HEADS):
                lo = h * GDN_D
                items.append(dict(c=c, h=h,
                                  q=yc_ref[pl.ds(r0, CHUNK), lo:lo + GDN_D],
                                  k=yc_ref[pl.ds(r0, CHUNK), 512 + lo:512 + lo + GDN_D],
                                  v=yc_ref[pl.ds(r0, CHUNK), 1024 + lo:1024 + lo + GDN_D],
                                  beta=beta_t[:, LANE_GB + h:LANE_GB + h + 1],
                                  gg_col=g_t[:, LANE_GA + h:LANE_GA + h + 1],
                                  gg_row=g_r[LANE_GA + h:LANE_GA + h + 1, :]))
        for it in items:
            it["k"] = it["k"] * lax.rsqrt(jnp.sum(it["k"] * it["k"], axis=-1, keepdims=True) + EPS)
        for it in items:
            it["kk"] = _mm_nt(it["k"], it["k"])
        for it in items:
            it["q"] = (it["q"] * lax.rsqrt(jnp.sum(it["q"] * it["q"], axis=-1, keepdims=True) + EPS)
                       * (GDN_D ** -0.5))
        for it in items:
            it["g_col"], g_row = _cumsum_col_row(it["gg_col"], it["gg_row"], ii, jj)
            it["decay"] = jnp.where(incl, jnp.exp(jnp.where(incl, it["g_col"] - g_row, 0.0)), 0.0)
        for it in items:
            n_mat = jnp.where(strict, it["beta"] * it["kk"] * it["decay"], 0.0)
            it["x"] = eye - n_mat
            it["p"] = -n_mat
        for _ in range(5):
            for it in items:
                it["p"] = _mm(it["p"], it["p"])
            for it in items:
                it["x"] = it["x"] + _mm(it["x"], it["p"])
        for it in items:
            e_g = jnp.exp(it["g_col"])
            it["e_g"] = e_g
            rhs = jnp.concatenate([it["beta"] * it["v"], (it["beta"] * e_g) * it["k"]], axis=1)
            it["w"] = _mm(it["x"], rhs)
        for it in items:
            it["qk"] = _mm_nt(it["q"], it["k"]) * it["decay"]
        for it in items:
            c, h = it["c"], it["h"]
            g_end = it["g_col"][CHUNK - 1:CHUNK, :]
            k_end = it["k"] * jnp.exp(g_end - it["g_col"])
            wv_ref[c, h] = it["w"][:, 0:GDN_D]
            lhs_ref[c, h] = jnp.concatenate([it["w"][:, GDN_D:2 * GDN_D], it["e_g"] * it["q"]],
                                            axis=0).astype(BF16)
            qk_ref[c, h] = it["qk"].astype(BF16)
            ket_ref[c, h] = k_end.T.astype(BF16)
            ge_ref[c, h] = jnp.broadcast_to(jnp.exp(g_end), (1, GDN_D))
        return carry

    lax.fori_loop(0, tb // CHUNK // unroll_a, phase_a, 0)

    def phase_b(c, carry):
        r0 = pl.multiple_of(c * CHUNK, CHUNK)
        s = [s_ref[h] for h in range(HEADS)]
        r = [jnp.dot(lhs_ref[c, h], s[h].astype(BF16), preferred_element_type=F32) for h in range(HEADS)]
        ub = [(wv_ref[c, h] - r[h][0:CHUNK]).astype(BF16) for h in range(HEADS)]
        o = [r[h][CHUNK:2 * CHUNK] + jnp.dot(qk_ref[c, h], ub[h], preferred_element_type=F32)
             for h in range(HEADS)]
        for h in range(HEADS):
            s_ref[h] = ge_ref[c, h] * s[h] + jnp.dot(ket_ref[c, h], ub[h], preferred_element_type=F32)
        for h in range(HEADS):
            lo = h * GDN_D
            z = qkvz_ref[pl.ds(r0, CHUNK), COL_Z + lo:COL_Z + lo + GDN_D]
            out = _rms(o[h], gn) * (z * _sigmoid(z))
            o_ref[pl.ds(r0, CHUNK), lo:lo + GDN_D] = out.astype(o_ref.dtype)
        return carry

    lax.fori_loop(0, tb // CHUNK, phase_b, 0)

    @pl.when(t == nt - 1)
    def _():
        sout_ref[...] = s_ref[...]


def _gdn_prompt(proj, grow, cs8, s0, conv_w, plane, psub, gn, tb, unroll_a=8):
    b, t, _ = proj.shape
    ncb = tb // CHUNK
    return pl.pallas_call(
        functools.partial(_gdn_kernel, tb=tb, unroll_a=unroll_a),
        grid=(b, t // tb),
        in_specs=[pl.BlockSpec((None, tb, 2048), lambda i, j: (i, j, 0)),
                  pl.BlockSpec((None, tb, 128), lambda i, j: (i, j, COL_GATE // 128)),
                  pl.BlockSpec((None, ncb, 16, CHUNK), lambda i, j: (i, j, 0, 0)),
                  pl.BlockSpec((None, 8, CONV_CH), lambda i, j: (i, 0, 0)),
                  pl.BlockSpec((None, HEADS, GDN_D, GDN_D), lambda i, j: (i, 0, 0, 0)),
                  pl.BlockSpec((CONV_W, CONV_CH), lambda i, j: (0, 0)),
                  pl.BlockSpec((2, 128), lambda i, j: (0, 0)),
                  pl.BlockSpec((16, 2), lambda i, j: (0, 0)),
                  pl.BlockSpec((1, GDN_D), lambda i, j: (0, 0))],
        out_specs=[pl.BlockSpec((None, tb, HEADS * GDN_D), lambda i, j: (i, j, 0)),
                   pl.BlockSpec((None, HEADS, GDN_D, GDN_D), lambda i, j: (i, 0, 0, 0))],
        out_shape=[jax.ShapeDtypeStruct((b, t, HEADS * GDN_D), BF16),
                   jax.ShapeDtypeStruct((b, HEADS, GDN_D, GDN_D), F32)],
        scratch_shapes=[pltpu.VMEM((tb + 8, CONV_CH), F32),
                        pltpu.VMEM((tb, CONV_CH), F32),
                        pltpu.VMEM((HEADS, GDN_D, GDN_D), F32),
                        pltpu.VMEM((ncb, HEADS, CHUNK, GDN_D), F32),
                        pltpu.VMEM((ncb, HEADS, 2 * CHUNK, GDN_D), BF16),
                        pltpu.VMEM((ncb, HEADS, CHUNK, CHUNK), BF16),
                        pltpu.VMEM((ncb, HEADS, GDN_D, CHUNK), BF16),
                        pltpu.VMEM((ncb, HEADS, 1, GDN_D), F32)],
        compiler_params=pltpu.CompilerParams(dimension_semantics=("arbitrary", "arbitrary"),
                                             vmem_limit_bytes=VMEM_LIMIT),
        name="gdn_prompt",
    )(proj, proj, grow, cs8, s0, conv_w, plane, psub, gn)


def _mlstm_kernel(mvo_ref, mqk_ref, gcol_ref, grow_ref, c0_ref, n0_ref, m0_ref, blane_ref, bsub_ref,
                  gn_ref, h_ref, cout_ref, nout_ref, mout_ref, c_ref, n_ref, m_ref, *, tb, unroll):
    t = pl.program_id(1)
    nt = pl.num_programs(1)

    @pl.when(t == 0)
    def _():
        c_ref[...] = c0_ref[...]
        n_ref[...] = n0_ref[...]
        m_ref[...] = m0_ref[...]

    ii, jj = _chunk_masks()
    incl = jj <= ii
    lane = lax.broadcasted_iota(jnp.int32, (1, 128), 1)
    row128 = lax.broadcasted_iota(jnp.int32, (128, 1), 0)
    gn = gn_ref[...]
    blane = blane_ref[...]
    bsub = bsub_ref[...]

    def chunk_body(ci, carry):
        items = []
        for cu in range(unroll):
            c = ci * unroll + cu
            r0 = pl.multiple_of(c * CHUNK, CHUNK)
            gc = gcol_ref[pl.ds(r0, CHUNK), :] + blane
            gr = grow_ref[c] + bsub
            lf_c = -_softplus(-gc)
            lf_r = -_softplus(-gr)
            for p in range(HEADS // 2):
                qb = mqk_ref[pl.ds(r0, CHUNK), p * 128:(p + 1) * 128]
                kb = mqk_ref[pl.ds(r0, CHUNK), 256 + p * 128:256 + (p + 1) * 128] * (ML_DK ** -0.5)
                for e in range(2):
                    h = 2 * p + e
                    lm = (lane >= e * ML_DK) & (lane < (e + 1) * ML_DK)
                    items.append(dict(cu=cu, r0=r0, p=p, e=e, h=h, qh=jnp.where(lm, qb, 0.0), kb=kb,
                                      kh=jnp.where(lm, kb, 0.0),
                                      lf_col=lf_c[:, LANE_MF + h:LANE_MF + h + 1],
                                      lf_row=lf_r[LANE_MF + h:LANE_MF + h + 1, :],
                                      ig_row=gr[LANE_MI + h:LANE_MI + h + 1, :],
                                      ig_col=gc[:, LANE_MI + h:LANE_MI + h + 1],
                                      v=mvo_ref[pl.ds(r0, CHUNK), h * ML_DV:(h + 1) * ML_DV]))
        for it in items:
            it["qk"] = _mm_nt(it["qh"], it["kb"])
        for it in items:
            it["f_col"], it["f_row"] = _cumsum_col_row(it["lf_col"], it["lf_row"], ii, jj)
        for it in items:
            it["d_mat"] = jnp.where(incl, it["f_col"] - it["f_row"] + it["ig_row"], -jnp.inf)
            it["d_max"] = jnp.max(it["d_mat"], axis=1, keepdims=True)
        for it in items:
            p0 = jnp.where(incl, jnp.exp(jnp.where(incl, it["d_mat"] - it["d_max"], 0.0)), 0.0)
            pend0 = jnp.exp(it["f_col"][CHUNK - 1:CHUNK, :] - it["f_col"] + it["ig_col"]
                            - it["d_max"][CHUNK - 1:CHUNK, :])
            it["kp0"] = it["kh"] * pend0
            it["pqk0"] = p0 * it["qk"]
        for it in items:
            it["pv0"] = _mm(it["pqk0"], it["v"])
            it["rs0"] = jnp.sum(it["pqk0"], axis=-1, keepdims=True)
        for it in items:
            it["cadd0"] = _mm(it["kp0"].T, it["v"])
            it["nadd0"] = jnp.sum(it["kp0"], axis=0, keepdims=True)

        m_cur = [m_ref[:, h:h + 1] for h in range(HEADS)]
        for it in items:
            h = it["h"]
            it["m_prev"] = m_cur[h]
            m_cur[h] = jnp.maximum(it["f_col"][CHUNK - 1:CHUNK, :] + m_cur[h], it["d_max"][CHUNK - 1:CHUNK, :])
        for it in items:
            bcol = it["f_col"] + it["m_prev"]
            mt = jnp.maximum(bcol, it["d_max"])
            it["mt"] = mt
            it["w_prev"] = jnp.exp(bcol - mt)
            it["sc"] = jnp.exp(it["d_max"] - mt)
        c_cur = [c_ref[p] for p in range(HEADS // 2)]
        n_cur = [n_ref[p] for p in range(HEADS // 2)]
        for cu in range(unroll):
            for p in range(HEADS // 2):
                pair = [it for it in items if it["cu"] == cu and it["p"] == p]
                w_end = [it["w_prev"][CHUNK - 1:CHUNK, :] for it in pair]
                s_end = [it["sc"][CHUNK - 1:CHUNK, :] for it in pair]
                for it in pair:
                    it["c_prev"] = c_cur[p]
                    it["n_prev"] = n_cur[p]
                c_cur[p] = (jnp.where(row128 < ML_DK, w_end[0], w_end[1]) * c_cur[p]
                            + s_end[0] * pair[0]["cadd0"] + s_end[1] * pair[1]["cadd0"])
                n_cur[p] = (jnp.where(lane < ML_DK, w_end[0], w_end[1]) * n_cur[p]
                            + s_end[0] * pair[0]["nadd0"] + s_end[1] * pair[1]["nadd0"])
        for it in items:
            it["qc"] = _mm(it["qh"], it["c_prev"])
        for it in items:
            num = it["w_prev"] * it["qc"] + it["sc"] * it["pv0"]
            den = (it["w_prev"] * jnp.sum(it["qh"] * it["n_prev"], axis=-1, keepdims=True)
                   + it["sc"] * it["rs0"])
            it["hh"] = num / jnp.maximum(jnp.abs(den), jnp.exp(-it["mt"]))
        for it in items:
            h = it["h"]
            mo = mvo_ref[pl.ds(it["r0"], CHUNK), 512 + h * ML_DV:512 + (h + 1) * ML_DV]
            out = _sigmoid(mo) * _rms(it["hh"], gn)
            h_ref[pl.ds(it["r0"], CHUNK), h * ML_DV:(h + 1) * ML_DV] = out.astype(h_ref.dtype)
        for p in range(HEADS // 2):
            c_ref[p] = c_cur[p]
            n_ref[p] = n_cur[p]
        for h in range(HEADS):
            m_ref[:, h:h + 1] = m_cur[h]
        return carry

    lax.fori_loop(0, tb // CHUNK // unroll, chunk_body, 0)

    @pl.when(t == nt - 1)
    def _():
        cout_ref[...] = c_ref[...]
        nout_ref[...] = n_ref[...]
        mout_ref[...] = m_ref[...]


def _mlstm_prompt(proj, grow, c0, n0, m0, blane, bsub, gn, tb, unroll=8):
    b, t, _ = proj.shape
    ncb = tb // CHUNK
    hp = HEADS // 2
    return pl.pallas_call(
        functools.partial(_mlstm_kernel, tb=tb, unroll=unroll),
        grid=(b, t // tb),
        in_specs=[pl.BlockSpec((None, tb, 1024), lambda i, j: (i, j, COL_MV // 1024)),
                  pl.BlockSpec((None, tb, 512), lambda i, j: (i, j, COL_MQ // 512)),
                  pl.BlockSpec((None, tb, 128), lambda i, j: (i, j, COL_GATE // 128)),
                  pl.BlockSpec((None, ncb, 16, CHUNK), lambda i, j: (i, j, 0, 0)),
                  pl.BlockSpec((None, hp, 128, 128), lambda i, j: (i, 0, 0, 0)),
                  pl.BlockSpec((None, hp, 1, 128), lambda i, j: (i, 0, 0, 0)),
                  pl.BlockSpec((None, 1, 128), lambda i, j: (i, 0, 0)),
                  pl.BlockSpec((1, 128), lambda i, j: (0, 0)),
                  pl.BlockSpec((16, 1), lambda i, j: (0, 0)),
                  pl.BlockSpec((1, ML_DV), lambda i, j: (0, 0))],
        out_specs=[pl.BlockSpec((None, tb, HEADS * ML_DV), lambda i, j: (i, j, 0)),
                   pl.BlockSpec((None, hp, 128, 128), lambda i, j: (i, 0, 0, 0)),
                   pl.BlockSpec((None, hp, 1, 128), lambda i, j: (i, 0, 0, 0)),
                   pl.BlockSpec((None, 1, 128), lambda i, j: (i, 0, 0))],
        out_shape=[jax.ShapeDtypeStruct((b, t, HEADS * ML_DV), BF16),
                   jax.ShapeDtypeStruct((b, hp, 128, 128), F32),
                   jax.ShapeDtypeStruct((b, hp, 1, 128), F32),
                   jax.ShapeDtypeStruct((b, 1, 128), F32)],
        scratch_shapes=[pltpu.VMEM((hp, 128, 128), F32),
                        pltpu.VMEM((hp, 1, 128), F32),
                        pltpu.VMEM((1, 128), F32)],
        compiler_params=pltpu.CompilerParams(dimension_semantics=("arbitrary", "arbitrary"),
                                             vmem_limit_bytes=VMEM_LIMIT),
        name="mlstm_prompt",
    )(proj, proj, proj, grow, c0, n0, m0, blane, bsub, gn)


def _decode_kernel(proj_ref, cs_ref, s0_ref, c0_ref, n0_ref, m0_ref, cw_ref, plane_ref, gng_ref, gnm_ref,
                   og_ref, om_ref, sout_ref, cout_ref, nout_ref, mout_ref, rowg_ref, rowm_ref, *, bb):
    lane = lax.broadcasted_iota(jnp.int32, (1, 128), 1)
    gt = proj_ref[:, COL_GATE:COL_GATE + 128]
    beta_t = _sigmoid(gt)
    g_t = -jnp.exp(plane_ref[0:1, :]) * _softplus(gt + plane_ref[1:2, :])
    eg_t = jnp.exp(g_t)
    gb_t = gt + plane_ref[2:3, :]
    lf_t = -_softplus(-gb_t)

    heads = []
    for h in range(HEADS):
        cols = []
        for part in range(3):
            c0 = part * 512 + h * GDN_D
            w = cw_ref[:, c0:c0 + GDN_D]
            acc = cs_ref[:, 0, c0:c0 + GDN_D] * w[0:1, :]
            acc = acc + cs_ref[:, 1, c0:c0 + GDN_D] * w[1:2, :]
            acc = acc + cs_ref[:, 2, c0:c0 + GDN_D] * w[2:3, :]
            acc = acc + proj_ref[:, COL_QKV + c0:COL_QKV + c0 + GDN_D] * w[3:4, :]
            cols.append(acc * _sigmoid(acc))
        q, k, v = cols
        q = q * lax.rsqrt(jnp.sum(q * q, axis=-1, keepdims=True) + EPS) * (GDN_D ** -0.5)
        k = k * lax.rsqrt(jnp.sum(k * k, axis=-1, keepdims=True) + EPS)
        heads.append(dict(v=v, qk=jnp.sum(q * k, axis=-1, keepdims=True), q_t=q.T, k_t=k.T,
                          beta=beta_t[:, LANE_GB + h:LANE_GB + h + 1],
                          eg=eg_t[:, LANE_GA + h:LANE_GA + h + 1]))
    items = [dict(h=h, b=b) for h in range(HEADS) for b in range(bb)]
    for it in items:
        hd, b = heads[it["h"]], it["b"]
        it["k_col"] = hd["k_t"][:, b:b + 1]
        it["q_col"] = hd["q_t"][:, b:b + 1]
    for it in items:
        s_bh = s0_ref[it["b"], it["h"]]
        it["ks"] = jnp.sum(it["k_col"] * s_bh, axis=0, keepdims=True)
        it["qs"] = jnp.sum(it["q_col"] * s_bh, axis=0, keepdims=True)
    for it in items:
        hd, b = heads[it["h"]], it["b"]
        eg_b = hd["eg"][b:b + 1, :]
        it["eg_b"] = eg_b
        it["u"] = hd["beta"][b:b + 1, :] * (hd["v"][b:b + 1, :] - eg_b * it["ks"])
        rowg_ref[it["h"], b:b + 1, :] = eg_b * it["qs"] + hd["qk"][b:b + 1, :] * it["u"]
    for it in items:
        sout_ref[it["b"], it["h"]] = it["eg_b"] * s0_ref[it["b"], it["h"]] + it["k_col"] * it["u"]
    for h in range(HEADS):
        z = proj_ref[:, COL_Z + h * GDN_D:COL_Z + (h + 1) * GDN_D]
        out = _rms(rowg_ref[h], gng_ref[...]) * (z * _sigmoid(z))
        og_ref[:, h * GDN_D:(h + 1) * GDN_D] = out.astype(og_ref.dtype)

    heads = []
    for p in range(HEADS // 2):
        qb = proj_ref[:, COL_MQ + p * 128:COL_MQ + (p + 1) * 128]
        kb = proj_ref[:, COL_MK + p * 128:COL_MK + (p + 1) * 128] * (ML_DK ** -0.5)
        n_p = n0_ref[:, p * 128:(p + 1) * 128]
        q_t = qb.T
        k_t = kb.T
        w_prev, p_in = [], []
        for e in range(2):
            h = 2 * p + e
            lm = (lane >= e * ML_DK) & (lane < (e + 1) * ML_DK)
            ig = gb_t[:, LANE_MI + h:LANE_MI + h + 1]
            lf = lf_t[:, LANE_MF + h:LANE_MF + h + 1]
            m_old = m0_ref[:, h:h + 1]
            m_new = jnp.maximum(lf + m_old, ig)
            w_prev.append(jnp.exp(lf + m_old - m_new))
            p_in.append(jnp.exp(ig - m_new))
            qk = jnp.sum(jnp.where(lm, qb * kb, 0.0), axis=-1, keepdims=True)
            qn = jnp.sum(jnp.where(lm, qb * n_p, 0.0), axis=-1, keepdims=True)
            mout_ref[:, h:h + 1] = m_new
            heads.append(dict(p=p, e=e, q_t=q_t, k_t=k_t, w_prev=w_prev[e], p_in=p_in[e],
                              pqk=p_in[e] * qk, wqn=w_prev[e] * qn, floor=jnp.exp(-m_new),
                              v=proj_ref[:, COL_MV + h * ML_DV:COL_MV + (h + 1) * ML_DV]))
        lo_lane = lane < ML_DK
        nout_ref[:, p * 128:(p + 1) * 128] = (jnp.where(lo_lane, w_prev[0], w_prev[1]) * n_p
                                              + jnp.where(lo_lane, p_in[0], p_in[1]) * kb)
    items = [dict(h=h, b=b) for h in range(HEADS) for b in range(bb)]
    for it in items:
        hd, b = heads[it["h"]], it["b"]
        rs = slice(hd["e"] * ML_DK, (hd["e"] + 1) * ML_DK)
        it["rs"] = rs
        it["q_col"] = hd["q_t"][rs, b:b + 1]
        it["k_col"] = hd["k_t"][rs, b:b + 1]
    for it in items:
        hd, b = heads[it["h"]], it["b"]
        c_bh = c0_ref[b, hd["p"], it["rs"], :]
        it["qc"] = jnp.sum(it["q_col"] * c_bh, axis=0, keepdims=True)
    for it in items:
        hd, b = heads[it["h"]], it["b"]
        pqk = hd["pqk"][b:b + 1, :]
        num = hd["w_prev"][b:b + 1, :] * it["qc"] + pqk * hd["v"][b:b + 1, :]
        den = hd["wqn"][b:b + 1, :] + pqk
        rowm_ref[it["h"], b:b + 1, :] = num / jnp.maximum(jnp.abs(den), hd["floor"][b:b + 1, :])
    for it in items:
        hd, b = heads[it["h"]], it["b"]
        cout_ref[b, hd["p"], it["rs"], :] = (hd["w_prev"][b:b + 1, :] * c0_ref[b, hd["p"], it["rs"], :]
                                             + (hd["p_in"][b:b + 1, :] * it["k_col"]) * hd["v"][b:b + 1, :])
    for h in range(HEADS):
        mo = proj_ref[:, COL_MO + h * ML_DV:COL_MO + (h + 1) * ML_DV]
        out = _sigmoid(mo) * _rms(rowm_ref[h], gnm_ref[...])
        om_ref[:, h * ML_DV:(h + 1) * ML_DV] = out.astype(om_ref.dtype)
    mout_ref[:, HEADS:128] = m0_ref[:, HEADS:128]


def _decode(proj, cs, s0, c0, n0, m0, conv_w, plane, gng, gnm, bb):
    b = proj.shape[0]
    hp = HEADS // 2
    return pl.pallas_call(
        functools.partial(_decode_kernel, bb=bb),
        grid=(b // bb,),
        in_specs=[pl.BlockSpec((bb, N_PROJ), lambda i: (i, 0)),
                  pl.BlockSpec((bb, CONV_W - 1, CONV_CH), lambda i: (i, 0, 0)),
                  pl.BlockSpec((bb, HEADS, GDN_D, GDN_D), lambda i: (i, 0, 0, 0)),
                  pl.BlockSpec((bb, hp, 128, 128), lambda i: (i, 0, 0, 0)),
                  pl.BlockSpec((bb, HEADS * ML_DK), lambda i: (i, 0)),
                  pl.BlockSpec((bb, 128), lambda i: (i, 0)),
                  pl.BlockSpec((CONV_W, CONV_CH), lambda i: (0, 0)),
                  pl.BlockSpec((3, 128), lambda i: (0, 0)),
                  pl.BlockSpec((1, GDN_D), lambda i: (0, 0)),
                  pl.BlockSpec((1, ML_DV), lambda i: (0, 0))],
        out_specs=[pl.BlockSpec((bb, HEADS * GDN_D), lambda i: (i, 0)),
                   pl.BlockSpec((bb, HEADS * ML_DV), lambda i: (i, 0)),
                   pl.BlockSpec((bb, HEADS, GDN_D, GDN_D), lambda i: (i, 0, 0, 0)),
                   pl.BlockSpec((bb, hp, 128, 128), lambda i: (i, 0, 0, 0)),
                   pl.BlockSpec((bb, HEADS * ML_DK), lambda i: (i, 0)),
                   pl.BlockSpec((bb, 128), lambda i: (i, 0))],
        out_shape=[jax.ShapeDtypeStruct((b, HEADS * GDN_D), BF16),
                   jax.ShapeDtypeStruct((b, HEADS * ML_DV), BF16),
                   jax.ShapeDtypeStruct((b, HEADS, GDN_D, GDN_D), F32),
                   jax.ShapeDtypeStruct((b, hp, 128, 128), F32),
                   jax.ShapeDtypeStruct((b, HEADS * ML_DK), F32),
                   jax.ShapeDtypeStruct((b, 128), F32)],
        scratch_shapes=[pltpu.VMEM((HEADS, bb, 128), F32), pltpu.VMEM((HEADS, bb, 128), F32)],
        compiler_params=pltpu.CompilerParams(dimension_semantics=("arbitrary",),
                                             vmem_limit_bytes=VMEM_LIMIT),
        name="decode_step",
    )(proj, cs, s0, c0, n0, m0, conv_w, plane, gng, gnm)


def _regroup_w_in(w_in):
    sizes = (512, 512, 512, 512, 4, 4, 256, 256, 512, 512, 4, 4)
    offs = [0]
    for s in sizes:
        offs.append(offs[-1] + s)
    gq, gk, gv, gz, gb, ga, mq, mk, mv, mo, mi, mf = (w_in[:, offs[i]:offs[i + 1]] for i in range(12))
    pad = jnp.zeros((w_in.shape[0], 128 - 16), w_in.dtype)
    return jnp.concatenate([gq, gk, gv, gz, mv, mo, mq, mk, gb, ga, mi, mf, pad], axis=1).astype(BF16)


def _lane_vec(pairs):
    v = jnp.zeros((128,), F32)
    for off, val in pairs:
        v = v.at[off:off + HEADS].set(val.astype(F32))
    return v


def _prep_params(norm_pre_mix, w_in, conv_w, a_log, dt_bias, gdn_norm_g, b_igate, b_fgate, mlstm_norm_g,
                 w_out, norm_post_mix, norm_pre_mlp, w_up, w_down, norm_post_mlp):
    alog_v = _lane_vec([(LANE_GA, a_log[0])])
    dtb_v = _lane_vec([(LANE_GA, dt_bias[0])])
    bias_v = _lane_vec([(LANE_MI, b_igate[0]), (LANE_MF, b_fgate[0])])
    return dict(
        w_in=_regroup_w_in(w_in[0]), wo=w_out[0].astype(BF16), wu=w_up[0].astype(BF16),
        wd=w_down[0].astype(BF16),
        g_pre=norm_pre_mix[0][None, :], g1=norm_post_mix[0][None, :], g2=norm_pre_mlp[0][None, :],
        g3=norm_post_mlp[0][None, :], cw=conv_w[0], gng=gdn_norm_g[0][None, :], gnm=mlstm_norm_g[0][None, :],
        gdn_plane=jnp.stack([alog_v, dtb_v]),
        gdn_psub=jnp.stack([alog_v[:16], dtb_v[:16]], axis=1),
        ml_blane=bias_v[None, :],
        ml_bsub=bias_v[:16, None],
        dec_plane=jnp.stack([alog_v, dtb_v, bias_v]),
    )


def _prompt_path(x, conv0, s0, c0, n0, m0, prm, tb, tm):
    bsz, seq, d = x.shape
    hp = HEADS // 2
    x2d = x.reshape(bsz * seq, d)
    proj = _in_proj(x2d, prm["g_pre"], prm["w_in"], tm=tm).reshape(bsz, seq, N_PROJ)
    gates = proj[:, :, COL_GATE:COL_GATE + 16]
    grow = gates.reshape(bsz, seq // CHUNK, CHUNK, 16).transpose(0, 1, 3, 2)
    cs8 = jnp.pad(conv0, ((0, 0), (8 - (CONV_W - 1), 0), (0, 0)))
    og, s_new = _gdn_prompt(proj, grow, cs8, s0, prm["cw"], prm["gdn_plane"], prm["gdn_psub"], prm["gng"], tb=tb)
    om, c_new, n_new, m_new = _mlstm_prompt(
        proj, grow, c0.reshape(bsz, hp, 128, 128), n0.reshape(bsz, hp, 1, 128),
        jnp.pad(m0, ((0, 0), (0, 128 - HEADS)))[:, None, :], prm["ml_blane"], prm["ml_bsub"], prm["gnm"], tb=tb)
    y = _out_mlp(og.reshape(bsz * seq, -1), om.reshape(bsz * seq, -1), x2d, prm["wo"], prm["g1"], prm["g2"],
                 prm["wu"], prm["wd"], prm["g3"], tm=tm).reshape(bsz, seq, d)
    xp = jnp.concatenate([conv0, proj[:, seq - (CONV_W - 1):, COL_QKV:COL_QKV + CONV_CH]], axis=1)
    conv_new = xp[:, -(CONV_W - 1):]
    return y, (conv_new, s_new, c_new.reshape(bsz, HEADS, ML_DK, ML_DV), n_new.reshape(bsz, HEADS, ML_DK),
               m_new[:, 0, :HEADS])


def _sample_path(x, conv0, s0, c0, n0, m0, prm):
    dec, _, d = x.shape
    hp = HEADS // 2
    xs = x.reshape(dec, d)
    proj = _in_proj(xs, prm["g_pre"], prm["w_in"], tm=dec)
    og, om, s_new, c_new, n_new, m_new = _decode(
        proj, conv0, s0, c0.reshape(dec, hp, 128, 128), n0.reshape(dec, HEADS * ML_DK),
        jnp.pad(m0, ((0, 0), (0, 128 - HEADS))), prm["cw"], prm["dec_plane"], prm["gng"], prm["gnm"], bb=8)
    y = _out_mlp(og, om, xs, prm["wo"], prm["g1"], prm["g2"], prm["wu"], prm["wd"], prm["g3"],
                 tm=dec).reshape(dec, 1, d)
    conv_new = jnp.concatenate([conv0[:, 1:, :], proj[:, None, COL_QKV:COL_QKV + CONV_CH]], axis=1)
    return y, (conv_new, s_new, c_new.reshape(dec, HEADS, ML_DK, ML_DV), n_new.reshape(dec, HEADS, ML_DK),
               m_new[:, :HEADS])


def kernel(x_prompt, x_sample, state_gdn_conv, state_gdn_S, state_mlstm_C, state_mlstm_n, state_mlstm_m,
           norm_pre_mix, w_in, conv_w, a_log, dt_bias, gdn_norm_g, b_igate, b_fgate, mlstm_norm_g, w_out,
           norm_post_mix, norm_pre_mlp, w_up, w_down, norm_post_mlp):
    bsz = x_prompt.shape[0]
    prm = _prep_params(norm_pre_mix, w_in, conv_w, a_log, dt_bias, gdn_norm_g, b_igate, b_fgate, mlstm_norm_g,
                       w_out, norm_post_mix, norm_pre_mlp, w_up, w_down, norm_post_mlp)
    y_p, p_st = _prompt_path(
        x_prompt, jnp.zeros((bsz, CONV_W - 1, CONV_CH), F32), jnp.zeros((bsz, HEADS, GDN_D, GDN_D), F32),
        jnp.zeros((bsz, HEADS, ML_DK, ML_DV), F32), jnp.zeros((bsz, HEADS, ML_DK), F32),
        jnp.zeros((bsz, HEADS), F32), prm, tb=512, tm=512)
    y_s, s_st = _sample_path(x_sample, state_gdn_conv[0], state_gdn_S[0], state_mlstm_C[0], state_mlstm_n[0],
                             state_mlstm_m[0], prm)
    return (y_p, y_s) + tuple(a[None] for a in p_st) + tuple(a[None] for a in s_st)
```

```python
import functools

import jax
import jax.numpy as jnp
from jax import lax
from jax.experimental import pallas as pl
from jax.experimental.pallas import tpu as pltpu

F32 = jnp.float32
BF16 = jnp.bfloat16
EPS = 1e-6

D_MODEL = 1024
HEADS = 4
GDN_D = 128
ML_DK = 64
ML_DV = 128
CONV_W = 4
CONV_CH = 3 * HEADS * GDN_D
D_FF = 4 * D_MODEL
CHUNK = 64

COL_QKV = 0
COL_Z = 1536
COL_MV = 2048
COL_MO = 2560
COL_MQ = 3072
COL_MK = 3328
COL_GATE = 3584
N_PROJ = COL_GATE + 128
LANE_GB, LANE_GA, LANE_MI, LANE_MF = 0, 4, 8, 12

VMEM_LIMIT = 56 * 1024 * 1024


def _rms(x, g):
    return x * lax.rsqrt(jnp.mean(x * x, axis=-1, keepdims=True) + EPS) * g


def _softplus(x):
    return jnp.maximum(x, 0.0) + jnp.log1p(jnp.exp(-jnp.abs(x)))


def _sigmoid(x):
    return 1.0 / (1.0 + jnp.exp(-x))


def _mm(a, b):
    return jnp.dot(a.astype(BF16), b.astype(BF16), preferred_element_type=F32)


def _mm_nt(a, b):
    return lax.dot_general(a.astype(BF16), b.astype(BF16), (((1,), (1,)), ((), ())),
                           preferred_element_type=F32)


def _in_proj_kernel(x_ref, g_ref, w_ref, o_ref, h_ref, *, n_chunk):
    h_ref[...] = _rms(x_ref[...], g_ref[...]).astype(BF16)
    n = o_ref.shape[1]
    for c0 in range(0, n, n_chunk):
        c1 = min(c0 + n_chunk, n)
        o_ref[:, c0:c1] = jnp.dot(h_ref[...], w_ref[:, c0:c1], preferred_element_type=F32)


def _in_proj(x2d, g, w, tm):
    m, k = x2d.shape
    n = w.shape[1]
    return pl.pallas_call(
        functools.partial(_in_proj_kernel, n_chunk=512),
        grid=(m // tm,),
        in_specs=[pl.BlockSpec((tm, k), lambda i: (i, 0)),
                  pl.BlockSpec((1, k), lambda i: (0, 0)),
                  pl.BlockSpec((k, n), lambda i: (0, 0))],
        out_specs=pl.BlockSpec((tm, n), lambda i: (i, 0)),
        out_shape=jax.ShapeDtypeStruct((m, n), F32),
        scratch_shapes=[pltpu.VMEM((tm, k), BF16)],
        compiler_params=pltpu.CompilerParams(dimension_semantics=("arbitrary",),
                                             vmem_limit_bytes=VMEM_LIMIT),
        name="in_proj",
    )(x2d, g, w)


def _out_mlp_kernel(og_ref, om_ref, x_ref, wo_ref, g1_ref, g2_ref, wu_ref, wd_ref, g3_ref,
                    y_ref, hn_ref, acc_ref, *, ff_chunk):
    half = og_ref.shape[1]
    mix = (jnp.dot(og_ref[...], wo_ref[0:half, :], preferred_element_type=F32)
           + jnp.dot(om_ref[...], wo_ref[half:2 * half, :], preferred_element_type=F32))
    x1 = x_ref[...] + _rms(mix, g1_ref[...])
    y_ref[...] = x1
    hn_ref[...] = _rms(x1, g2_ref[...]).astype(BF16)
    d_ff = wu_ref.shape[1]
    for c0 in range(0, d_ff, ff_chunk):
        u = jnp.dot(hn_ref[...], wu_ref[:, c0:c0 + ff_chunk], preferred_element_type=F32)
        u = jnp.square(jnp.maximum(u, 0.0)).astype(BF16)
        d = jnp.dot(u, wd_ref[c0:c0 + ff_chunk, :], preferred_element_type=F32)
        if c0 == 0:
            acc_ref[...] = d
        else:
            acc_ref[...] += d
    y_ref[...] = y_ref[...] + _rms(acc_ref[...], g3_ref[...])


def _out_mlp(og, om, x2d, wo, g1, g2, wu, wd, g3, tm):
    m, d = x2d.shape
    half = og.shape[1]
    d_ff = wu.shape[1]
    const = lambda i: (0, 0)
    return pl.pallas_call(
        functools.partial(_out_mlp_kernel, ff_chunk=1024),
        grid=(m // tm,),
        in_specs=[pl.BlockSpec((tm, half), lambda i: (i, 0)),
                  pl.BlockSpec((tm, half), lambda i: (i, 0)),
                  pl.BlockSpec((tm, d), lambda i: (i, 0)),
                  pl.BlockSpec((d, d), const, pipeline_mode=pl.Buffered(1)),
                  pl.BlockSpec((1, d), const),
                  pl.BlockSpec((1, d), const),
                  pl.BlockSpec((d, d_ff), const, pipeline_mode=pl.Buffered(1)),
                  pl.BlockSpec((d_ff, d), const, pipeline_mode=pl.Buffered(1)),
                  pl.BlockSpec((1, d), const)],
        out_specs=pl.BlockSpec((tm, d), lambda i: (i, 0)),
        out_shape=jax.ShapeDtypeStruct((m, d), F32),
        scratch_shapes=[pltpu.VMEM((tm, d), BF16), pltpu.VMEM((tm, d), F32)],
        compiler_params=pltpu.CompilerParams(dimension_semantics=("arbitrary",),
                                             vmem_limit_bytes=VMEM_LIMIT),
        name="out_mlp",
    )(og, om, x2d, wo, g1, g2, wu, wd, g3)


def _chunk_masks():
    ii = lax.broadcasted_iota(jnp.int32, (CHUNK, CHUNK), 0)
    jj = lax.broadcasted_iota(jnp.int32, (CHUNK, CHUNK), 1)
    return ii, jj


def _cumsum_col_row(x_col, x_row, ii, jj):
    c_col = jnp.sum(jnp.where(jj <= ii, x_row, 0.0), axis=1, keepdims=True)
    c_row = jnp.sum(jnp.where(ii <= jj, x_col, 0.0), axis=0, keepdims=True)
    return c_col, c_row


def _gdn_kernel(qkvz_ref, gcol_ref, grow_ref, cs_ref, s0_ref, cw_ref, plane_ref, psub_ref, gn_ref,
                o_ref, sout_ref, xp_ref, yc_ref, s_ref, wv_ref, lhs_ref, qk_ref, ket_ref, ge_ref,
                *, tb, unroll_a):
    t = pl.program_id(1)
    nt = pl.num_programs(1)

    @pl.when(t == 0)
    def _():
        xp_ref[0:8, :] = cs_ref[...]
        s_ref[...] = s0_ref[...]

    @pl.when(t > 0)
    def _():
        xp_ref[0:8, :] = xp_ref[tb:tb + 8, :]

    xp_ref[8:tb + 8, :] = qkvz_ref[:, COL_QKV:COL_QKV + CONV_CH]

    def conv_tile(ct, carry):
        c0 = pl.multiple_of(ct * 128, 128)
        w = cw_ref[:, pl.ds(c0, 128)]
        for r0 in range(0, tb, 128):
            acc = xp_ref[5 + r0:5 + r0 + 128, pl.ds(c0, 128)] * w[0:1, :]
            for j in range(1, CONV_W):
                acc = acc + xp_ref[5 + j + r0:5 + j + r0 + 128, pl.ds(c0, 128)] * w[j:j + 1, :]
            yc_ref[r0:r0 + 128, pl.ds(c0, 128)] = acc * _sigmoid(acc)
        return carry

    lax.fori_loop(0, CONV_CH // 128, conv_tile, 0)

    ii, jj = _chunk_masks()
    incl = jj <= ii
    strict = jj < ii
    eye = (ii == jj).astype(F32)
    neg_a_lane = -jnp.exp(plane_ref[0:1, :])
    dtb_lane = plane_ref[1:2, :]
    neg_a_sub = -jnp.exp(psub_ref[:, 0:1])
    dtb_sub = psub_ref[:, 1:2]
    gn = gn_ref[...]

    def phase_a(ci, carry):
        items = []
        for cu in range(unroll_a):
            c = ci * unroll_a + cu
            r0 = pl.multiple_of(c * CHUNK, CHUNK)
            gc = gcol_ref[pl.ds(r0, CHUNK), :]
            gr = grow_ref[c]
            beta_t = _sigmoid(gc)
            g_t = neg_a_lane * _softplus(gc + dtb_lane)
            g_r = neg_a_sub * _softplus(gr + dtb_sub)
            for h in range(HEADS):
                lo = h * GDN_D
                items.append(dict(c=c, h=h,
                                  q=yc_ref[pl.ds(r0, CHUNK), lo:lo + GDN_D],
                                  k=yc_ref[pl.ds(r0, CHUNK), 512 + lo:512 + lo + GDN_D],
                                  v=yc_ref[pl.ds(r0, CHUNK), 1024 + lo:1024 + lo + GDN_D],
                                  beta=beta_t[:, LANE_GB + h:LANE_GB + h + 1],
                                  gg_col=g_t[:, LANE_GA + h:LANE_GA + h + 1],
                                  gg_row=g_r[LANE_GA + h:LANE_GA + h + 1, :]))
        for it in items:
            it["k"] = it["k"] * lax.rsqrt(jnp.sum(it["k"] * it["k"], axis=-1, keepdims=True) + EPS)
        for it in items:
            it["kk"] = _mm_nt(it["k"], it["k"])
        for it in items:
            it["q"] = (it["q"] * lax.rsqrt(jnp.sum(it["q"] * it["q"], axis=-1, keepdims=True) + EPS)
                       * (GDN_D ** -0.5))
        for it in items:
            it["g_col"], g_row = _cumsum_col_row(it["gg_col"], it["gg_row"], ii, jj)
            it["decay"] = jnp.where(incl, jnp.exp(jnp.where(incl, it["g_col"] - g_row, 0.0)), 0.0)
        for it in items:
            n_mat = jnp.where(strict, it["beta"] * it["kk"] * it["decay"], 0.0)
            it["x"] = eye - n_mat
            it["p"] = -n_mat
        for _ in range(5):
            for it in items:
                it["p"] = _mm(it["p"], it["p"])
            for it in items:
                it["x"] = it["x"] + _mm(it["x"], it["p"])
        for it in items:
            e_g = jnp.exp(it["g_col"])
            it["e_g"] = e_g
            rhs = jnp.concatenate([it["beta"] * it["v"], (it["beta"] * e_g) * it["k"]], axis=1)
            it["w"] = _mm(it["x"], rhs)
        for it in items:
            it["qk"] = _mm_nt(it["q"], it["k"]) * it["decay"]
        for it in items:
            c, h = it["c"], it["h"]
            g_end = it["g_col"][CHUNK - 1:CHUNK, :]
            k_end = it["k"] * jnp.exp(g_end - it["g_col"])
            wv_ref[c, h] = it["w"][:, 0:GDN_D]
            lhs_ref[c, h] = jnp.concatenate([it["w"][:, GDN_D:2 * GDN_D], it["e_g"] * it["q"]],
                                            axis=0).astype(BF16)
            qk_ref[c, h] = it["qk"].astype(BF16)
            ket_ref[c, h] = k_end.T.astype(BF16)
            ge_ref[c, h] = jnp.broadcast_to(jnp.exp(g_end), (1, GDN_D))
        return carry

    lax.fori_loop(0, tb // CHUNK // unroll_a, phase_a, 0)

    def epilogue(c, o):
        for h in range(HEADS):
            lo = h * GDN_D
            z = qkvz_ref[c * CHUNK:(c + 1) * CHUNK, COL_Z + lo:COL_Z + lo + GDN_D]
            out = _rms(o[h], gn) * (z * _sigmoid(z))
            o_ref[c * CHUNK:(c + 1) * CHUNK, lo:lo + GDN_D] = out.astype(o_ref.dtype)

    s = [s_ref[h] for h in range(HEADS)]
    o_prev = None
    for c in range(tb // CHUNK):
        r = [jnp.dot(lhs_ref[c, h], s[h].astype(BF16), preferred_element_type=F32) for h in range(HEADS)]
        if o_prev is not None:
            epilogue(c - 1, o_prev)
        ub = [(wv_ref[c, h] - r[h][0:CHUNK]).astype(BF16) for h in range(HEADS)]
        s = [ge_ref[c, h] * s[h] + jnp.dot(ket_ref[c, h], ub[h], preferred_element_type=F32)
             for h in range(HEADS)]
        o_prev = [r[h][CHUNK:2 * CHUNK] + jnp.dot(qk_ref[c, h], ub[h], preferred_element_type=F32)
                  for h in range(HEADS)]
    epilogue(tb // CHUNK - 1, o_prev)
    for h in range(HEADS):
        s_ref[h] = s[h]

    @pl.when(t == nt - 1)
    def _():
        sout_ref[...] = s_ref[...]


def _gdn_prompt(proj, grow, cs8, s0, conv_w, plane, psub, gn, tb, unroll_a=8):
    b, t, _ = proj.shape
    ncb = tb // CHUNK
    return pl.pallas_call(
        functools.partial(_gdn_kernel, tb=tb, unroll_a=unroll_a),
        grid=(b, t // tb),
        in_specs=[pl.BlockSpec((None, tb, 2048), lambda i, j: (i, j, 0)),
                  pl.BlockSpec((None, tb, 128), lambda i, j: (i, j, COL_GATE // 128)),
                  pl.BlockSpec((None, ncb, 16, CHUNK), lambda i, j: (i, j, 0, 0)),
                  pl.BlockSpec((None, 8, CONV_CH), lambda i, j: (i, 0, 0)),
                  pl.BlockSpec((None, HEADS, GDN_D, GDN_D), lambda i, j: (i, 0, 0, 0)),
                  pl.BlockSpec((CONV_W, CONV_CH), lambda i, j: (0, 0)),
                  pl.BlockSpec((2, 128), lambda i, j: (0, 0)),
                  pl.BlockSpec((16, 2), lambda i, j: (0, 0)),
                  pl.BlockSpec((1, GDN_D), lambda i, j: (0, 0))],
        out_specs=[pl.BlockSpec((None, tb, HEADS * GDN_D), lambda i, j: (i, j, 0)),
                   pl.BlockSpec((None, HEADS, GDN_D, GDN_D), lambda i, j: (i, 0, 0, 0))],
        out_shape=[jax.ShapeDtypeStruct((b, t, HEADS * GDN_D), BF16),
                   jax.ShapeDtypeStruct((b, HEADS, GDN_D, GDN_D), F32)],
        scratch_shapes=[pltpu.VMEM((tb + 8, CONV_CH), F32),
                        pltpu.VMEM((tb, CONV_CH), F32),
                        pltpu.VMEM((HEADS, GDN_D, GDN_D), F32),
                        pltpu.VMEM((ncb, HEADS, CHUNK, GDN_D), F32),
                        pltpu.VMEM((ncb, HEADS, 2 * CHUNK, GDN_D), BF16),
                        pltpu.VMEM((ncb, HEADS, CHUNK, CHUNK), BF16),
                        pltpu.VMEM((ncb, HEADS, GDN_D, CHUNK), BF16),
                        pltpu.VMEM((ncb, HEADS, 1, GDN_D), F32)],
        compiler_params=pltpu.CompilerParams(dimension_semantics=("arbitrary", "arbitrary"),
                                             vmem_limit_bytes=VMEM_LIMIT),
        name="gdn_prompt",
    )(proj, proj, grow, cs8, s0, conv_w, plane, psub, gn)


def _mlstm_kernel(mvo_ref, mqk_ref, gcol_ref, grow_ref, c0_ref, n0_ref, m0_ref, blane_ref, bsub_ref,
                  gn_ref, h_ref, cout_ref, nout_ref, mout_ref, c_ref, n_ref, m_ref, *, tb, unroll):
    t = pl.program_id(1)
    nt = pl.num_programs(1)

    @pl.when(t == 0)
    def _():
        c_ref[...] = c0_ref[...]
        n_ref[...] = n0_ref[...]
        m_ref[...] = m0_ref[...]

    ii, jj = _chunk_masks()
    incl = jj <= ii
    lane = lax.broadcasted_iota(jnp.int32, (1, 128), 1)
    row128 = lax.broadcasted_iota(jnp.int32, (128, 1), 0)
    gn = gn_ref[...]
    blane = blane_ref[...]
    bsub = bsub_ref[...]

    def chunk_body(ci, carry):
        items = []
        for cu in range(unroll):
            c = ci * unroll + cu
            r0 = pl.multiple_of(c * CHUNK, CHUNK)
            gc = gcol_ref[pl.ds(r0, CHUNK), :] + blane
            gr = grow_ref[c] + bsub
            lf_c = -_softplus(-gc)
            lf_r = -_softplus(-gr)
            for p in range(HEADS // 2):
                qb = mqk_ref[pl.ds(r0, CHUNK), p * 128:(p + 1) * 128]
                kb = mqk_ref[pl.ds(r0, CHUNK), 256 + p * 128:256 + (p + 1) * 128] * (ML_DK ** -0.5)
                for e in range(2):
                    h = 2 * p + e
                    lm = (lane >= e * ML_DK) & (lane < (e + 1) * ML_DK)
                    items.append(dict(cu=cu, r0=r0, p=p, e=e, h=h, qh=jnp.where(lm, qb, 0.0), kb=kb,
                                      kh=jnp.where(lm, kb, 0.0),
                                      lf_col=lf_c[:, LANE_MF + h:LANE_MF + h + 1],
                                      lf_row=lf_r[LANE_MF + h:LANE_MF + h + 1, :],
                                      ig_row=gr[LANE_MI + h:LANE_MI + h + 1, :],
                                      ig_col=gc[:, LANE_MI + h:LANE_MI + h + 1],
                                      v=mvo_ref[pl.ds(r0, CHUNK), h * ML_DV:(h + 1) * ML_DV]))
        for it in items:
            it["qk"] = _mm_nt(it["qh"], it["kb"])
        for it in items:
            it["f_col"], it["f_row"] = _cumsum_col_row(it["lf_col"], it["lf_row"], ii, jj)
        for it in items:
            it["d_mat"] = jnp.where(incl, it["f_col"] - it["f_row"] + it["ig_row"], -jnp.inf)
            it["d_max"] = jnp.max(it["d_mat"], axis=1, keepdims=True)
        for it in items:
            p0 = jnp.where(incl, jnp.exp(jnp.where(incl, it["d_mat"] - it["d_max"], 0.0)), 0.0)
            pend0 = jnp.exp(it["f_col"][CHUNK - 1:CHUNK, :] - it["f_col"] + it["ig_col"]
                            - it["d_max"][CHUNK - 1:CHUNK, :])
            it["kp0"] = it["kh"] * pend0
            it["pqk0"] = p0 * it["qk"]
        for it in items:
            it["pv0"] = _mm(it["pqk0"], it["v"])
            it["rs0"] = jnp.sum(it["pqk0"], axis=-1, keepdims=True)
        for it in items:
            it["cadd0"] = _mm(it["kp0"].T, it["v"])
            it["nadd0"] = jnp.sum(it["kp0"], axis=0, keepdims=True)

        m_cur = [m_ref[:, h:h + 1] for h in range(HEADS)]
        for it in items:
            h = it["h"]
            it["m_prev"] = m_cur[h]
            m_cur[h] = jnp.maximum(it["f_col"][CHUNK - 1:CHUNK, :] + m_cur[h], it["d_max"][CHUNK - 1:CHUNK, :])
        for it in items:
            bcol = it["f_col"] + it["m_prev"]
            mt = jnp.maximum(bcol, it["d_max"])
            it["mt"] = mt
            it["w_prev"] = jnp.exp(bcol - mt)
            it["sc"] = jnp.exp(it["d_max"] - mt)
        c_cur = [c_ref[p] for p in range(HEADS // 2)]
        n_cur = [n_ref[p] for p in range(HEADS // 2)]
        for cu in range(unroll):
            for p in range(HEADS // 2):
                pair = [it for it in items if it["cu"] == cu and it["p"] == p]
                w_end = [it["w_prev"][CHUNK - 1:CHUNK, :] for it in pair]
                s_end = [it["sc"][CHUNK - 1:CHUNK, :] for it in pair]
                for it in pair:
                    it["c_prev"] = c_cur[p]
                    it["n_prev"] = n_cur[p]
                c_cur[p] = (jnp.where(row128 < ML_DK, w_end[0], w_end[1]) * c_cur[p]
                            + s_end[0] * pair[0]["cadd0"] + s_end[1] * pair[1]["cadd0"])
                n_cur[p] = (jnp.where(lane < ML_DK, w_end[0], w_end[1]) * n_cur[p]
                            + s_end[0] * pair[0]["nadd0"] + s_end[1] * pair[1]["nadd0"])
        for it in items:
            it["qc"] = _mm(it["qh"], it["c_prev"])
        for it in items:
            num = it["w_prev"] * it["qc"] + it["sc"] * it["pv0"]
            den = (it["w_prev"] * jnp.sum(it["qh"] * it["n_prev"], axis=-1, keepdims=True)
                   + it["sc"] * it["rs0"])
            it["hh"] = num / jnp.maximum(jnp.abs(den), jnp.exp(-it["mt"]))
        for it in items:
            h = it["h"]
            mo = mvo_ref[pl.ds(it["r0"], CHUNK), 512 + h * ML_DV:512 + (h + 1) * ML_DV]
            out = _sigmoid(mo) * _rms(it["hh"], gn)
            h_ref[pl.ds(it["r0"], CHUNK), h * ML_DV:(h + 1) * ML_DV] = out.astype(h_ref.dtype)
        for p in range(HEADS // 2):
            c_ref[p] = c_cur[p]
            n_ref[p] = n_cur[p]
        for h in range(HEADS):
            m_ref[:, h:h + 1] = m_cur[h]
        return carry

    lax.fori_loop(0, tb // CHUNK // unroll, chunk_body, 0)

    @pl.when(t == nt - 1)
    def _():
        cout_ref[...] = c_ref[...]
        nout_ref[...] = n_ref[...]
        mout_ref[...] = m_ref[...]


def _mlstm_prompt(proj, grow, c0, n0, m0, blane, bsub, gn, tb, unroll=8):
    b, t, _ = proj.shape
    ncb = tb // CHUNK
    hp = HEADS // 2
    return pl.pallas_call(
        functools.partial(_mlstm_kernel, tb=tb, unroll=unroll),
        grid=(b, t // tb),
        in_specs=[pl.BlockSpec((None, tb, 1024), lambda i, j: (i, j, COL_MV // 1024)),
                  pl.BlockSpec((None, tb, 512), lambda i, j: (i, j, COL_MQ // 512)),
                  pl.BlockSpec((None, tb, 128), lambda i, j: (i, j, COL_GATE // 128)),
                  pl.BlockSpec((None, ncb, 16, CHUNK), lambda i, j: (i, j, 0, 0)),
                  pl.BlockSpec((None, hp, 128, 128), lambda i, j: (i, 0, 0, 0)),
                  pl.BlockSpec((None, hp, 1, 128), lambda i, j: (i, 0, 0, 0)),
                  pl.BlockSpec((None, 1, 128), lambda i, j: (i, 0, 0)),
                  pl.BlockSpec((1, 128), lambda i, j: (0, 0)),
                  pl.BlockSpec((16, 1), lambda i, j: (0, 0)),
                  pl.BlockSpec((1, ML_DV), lambda i, j: (0, 0))],
        out_specs=[pl.BlockSpec((None, tb, HEADS * ML_DV), lambda i, j: (i, j, 0)),
                   pl.BlockSpec((None, hp, 128, 128), lambda i, j: (i, 0, 0, 0)),
                   pl.BlockSpec((None, hp, 1, 128), lambda i, j: (i, 0, 0, 0)),
                   pl.BlockSpec((None, 1, 128), lambda i, j: (i, 0, 0))],
        out_shape=[jax.ShapeDtypeStruct((b, t, HEADS * ML_DV), BF16),
                   jax.ShapeDtypeStruct((b, hp, 128, 128), F32),
                   jax.ShapeDtypeStruct((b, hp, 1, 128), F32),
                   jax.ShapeDtypeStruct((b, 1, 128), F32)],
        scratch_shapes=[pltpu.VMEM((hp, 128, 128), F32),
                        pltpu.VMEM((hp, 1, 128), F32),
                        pltpu.VMEM((1, 128), F32)],
        compiler_params=pltpu.CompilerParams(dimension_semantics=("arbitrary", "arbitrary"),
                                             vmem_limit_bytes=VMEM_LIMIT),
        name="mlstm_prompt",
    )(proj, proj, proj, grow, c0, n0, m0, blane, bsub, gn)


def _decode_kernel(proj_ref, cs_ref, s0_ref, c0_ref, n0_ref, m0_ref, cw_ref, plane_ref, gng_ref, gnm_ref,
                   og_ref, om_ref, sout_ref, cout_ref, nout_ref, mout_ref, rowg_ref, rowm_ref, *, bb):
    lane = lax.broadcasted_iota(jnp.int32, (1, 128), 1)
    gt = proj_ref[:, COL_GATE:COL_GATE + 128]
    beta_t = _sigmoid(gt)
    g_t = -jnp.exp(plane_ref[0:1, :]) * _softplus(gt + plane_ref[1:2, :])
    eg_t = jnp.exp(g_t)
    gb_t = gt + plane_ref[2:3, :]
    lf_t = -_softplus(-gb_t)

    heads = []
    for h in range(HEADS):
        cols = []
        for part in range(3):
            c0 = part * 512 + h * GDN_D
            w = cw_ref[:, c0:c0 + GDN_D]
            acc = cs_ref[:, 0, c0:c0 + GDN_D] * w[0:1, :]
            acc = acc + cs_ref[:, 1, c0:c0 + GDN_D] * w[1:2, :]
            acc = acc + cs_ref[:, 2, c0:c0 + GDN_D] * w[2:3, :]
            acc = acc + proj_ref[:, COL_QKV + c0:COL_QKV + c0 + GDN_D] * w[3:4, :]
            cols.append(acc * _sigmoid(acc))
        q, k, v = cols
        q = q * lax.rsqrt(jnp.sum(q * q, axis=-1, keepdims=True) + EPS) * (GDN_D ** -0.5)
        k = k * lax.rsqrt(jnp.sum(k * k, axis=-1, keepdims=True) + EPS)
        heads.append(dict(v=v, qk=jnp.sum(q * k, axis=-1, keepdims=True), q_t=q.T, k_t=k.T,
                          beta=beta_t[:, LANE_GB + h:LANE_GB + h + 1],
                          eg=eg_t[:, LANE_GA + h:LANE_GA + h + 1]))
    items = [dict(h=h, b=b) for h in range(HEADS) for b in range(bb)]
    for it in items:
        hd, b = heads[it["h"]], it["b"]
        it["k_col"] = hd["k_t"][:, b:b + 1]
        it["q_col"] = hd["q_t"][:, b:b + 1]
    for it in items:
        s_bh = s0_ref[it["b"], it["h"]]
        it["ks"] = jnp.sum(it["k_col"] * s_bh, axis=0, keepdims=True)
        it["qs"] = jnp.sum(it["q_col"] * s_bh, axis=0, keepdims=True)
    for it in items:
        hd, b = heads[it["h"]], it["b"]
        eg_b = hd["eg"][b:b + 1, :]
        it["eg_b"] = eg_b
        it["u"] = hd["beta"][b:b + 1, :] * (hd["v"][b:b + 1, :] - eg_b * it["ks"])
        rowg_ref[it["h"], b:b + 1, :] = eg_b * it["qs"] + hd["qk"][b:b + 1, :] * it["u"]
    for it in items:
        sout_ref[it["b"], it["h"]] = it["eg_b"] * s0_ref[it["b"], it["h"]] + it["k_col"] * it["u"]
    for h in range(HEADS):
        z = proj_ref[:, COL_Z + h * GDN_D:COL_Z + (h + 1) * GDN_D]
        out = _rms(rowg_ref[h], gng_ref[...]) * (z * _sigmoid(z))
        og_ref[:, h * GDN_D:(h + 1) * GDN_D] = out.astype(og_ref.dtype)

    heads = []
    for p in range(HEADS // 2):
        qb = proj_ref[:, COL_MQ + p * 128:COL_MQ + (p + 1) * 128]
        kb = proj_ref[:, COL_MK + p * 128:COL_MK + (p + 1) * 128] * (ML_DK ** -0.5)
        n_p = n0_ref[:, p * 128:(p + 1) * 128]
        q_t = qb.T
        k_t = kb.T
        w_prev, p_in = [], []
        for e in range(2):
            h = 2 * p + e
            lm = (lane >= e * ML_DK) & (lane < (e + 1) * ML_DK)
            ig = gb_t[:, LANE_MI + h:LANE_MI + h + 1]
            lf = lf_t[:, LANE_MF + h:LANE_MF + h + 1]
            m_old = m0_ref[:, h:h + 1]
            m_new = jnp.maximum(lf + m_old, ig)
            w_prev.append(jnp.exp(lf + m_old - m_new))
            p_in.append(jnp.exp(ig - m_new))
            qk = jnp.sum(jnp.where(lm, qb * kb, 0.0), axis=-1, keepdims=True)
            qn = jnp.sum(jnp.where(lm, qb * n_p, 0.0), axis=-1, keepdims=True)
            mout_ref[:, h:h + 1] = m_new
            heads.append(dict(p=p, e=e, q_t=q_t, k_t=k_t, w_prev=w_prev[e], p_in=p_in[e],
                              pqk=p_in[e] * qk, wqn=w_prev[e] * qn, floor=jnp.exp(-m_new),
                              v=proj_ref[:, COL_MV + h * ML_DV:COL_MV + (h + 1) * ML_DV]))
        lo_lane = lane < ML_DK
        nout_ref[:, p * 128:(p + 1) * 128] = (jnp.where(lo_lane, w_prev[0], w_prev[1]) * n_p
                                              + jnp.where(lo_lane, p_in[0], p_in[1]) * kb)
    items = [dict(h=h, b=b) for h in range(HEADS) for b in range(bb)]
    for it in items:
        hd, b = heads[it["h"]], it["b"]
        rs = slice(hd["e"] * ML_DK, (hd["e"] + 1) * ML_DK)
        it["rs"] = rs
        it["q_col"] = hd["q_t"][rs, b:b + 1]
        it["k_col"] = hd["k_t"][rs, b:b + 1]
    for it in items:
        hd, b = heads[it["h"]], it["b"]
        c_bh = c0_ref[b, hd["p"], it["rs"], :]
        it["qc"] = jnp.sum(it["q_col"] * c_bh, axis=0, keepdims=True)
    for it in items:
        hd, b = heads[it["h"]], it["b"]
        pqk = hd["pqk"][b:b + 1, :]
        num = hd["w_prev"][b:b + 1, :] * it["qc"] + pqk * hd["v"][b:b + 1, :]
        den = hd["wqn"][b:b + 1, :] + pqk
        rowm_ref[it["h"], b:b + 1, :] = num / jnp.maximum(jnp.abs(den), hd["floor"][b:b + 1, :])
    for it in items:
        hd, b = heads[it["h"]], it["b"]
        cout_ref[b, hd["p"], it["rs"], :] = (hd["w_prev"][b:b + 1, :] * c0_ref[b, hd["p"], it["rs"], :]
                                             + (hd["p_in"][b:b + 1, :] * it["k_col"]) * hd["v"][b:b + 1, :])
    for h in range(HEADS):
        mo = proj_ref[:, COL_MO + h * ML_DV:COL_MO + (h + 1) * ML_DV]
        out = _sigmoid(mo) * _rms(rowm_ref[h], gnm_ref[...])
        om_ref[:, h * ML_DV:(h + 1) * ML_DV] = out.astype(om_ref.dtype)
    mout_ref[:, HEADS:128] = m0_ref[:, HEADS:128]


def _decode(proj, cs, s0, c0, n0, m0, conv_w, plane, gng, gnm, bb):
    b = proj.shape[0]
    hp = HEADS // 2
    return pl.pallas_call(
        functools.partial(_decode_kernel, bb=bb),
        grid=(b // bb,),
        in_specs=[pl.BlockSpec((bb, N_PROJ), lambda i: (i, 0)),
                  pl.BlockSpec((bb, CONV_W - 1, CONV_CH), lambda i: (i, 0, 0)),
                  pl.BlockSpec((bb, HEADS, GDN_D, GDN_D), lambda i: (i, 0, 0, 0)),
                  pl.BlockSpec((bb, hp, 128, 128), lambda i: (i, 0, 0, 0)),
                  pl.BlockSpec((bb, HEADS * ML_DK), lambda i: (i, 0)),
                  pl.BlockSpec((bb, 128), lambda i: (i, 0)),
                  pl.BlockSpec((CONV_W, CONV_CH), lambda i: (0, 0)),
                  pl.BlockSpec((3, 128), lambda i: (0, 0)),
                  pl.BlockSpec((1, GDN_D), lambda i: (0, 0)),
                  pl.BlockSpec((1, ML_DV), lambda i: (0, 0))],
        out_specs=[pl.BlockSpec((bb, HEADS * GDN_D), lambda i: (i, 0)),
                   pl.BlockSpec((bb, HEADS * ML_DV), lambda i: (i, 0)),
                   pl.BlockSpec((bb, HEADS, GDN_D, GDN_D), lambda i: (i, 0, 0, 0)),
                   pl.BlockSpec((bb, hp, 128, 128), lambda i: (i, 0, 0, 0)),
                   pl.BlockSpec((bb, HEADS * ML_DK), lambda i: (i, 0)),
                   pl.BlockSpec((bb, 128), lambda i: (i, 0))],
        out_shape=[jax.ShapeDtypeStruct((b, HEADS * GDN_D), BF16),
                   jax.ShapeDtypeStruct((b, HEADS * ML_DV), BF16),
                   jax.ShapeDtypeStruct((b, HEADS, GDN_D, GDN_D), F32),
                   jax.ShapeDtypeStruct((b, hp, 128, 128), F32),
                   jax.ShapeDtypeStruct((b, HEADS * ML_DK), F32),
                   jax.ShapeDtypeStruct((b, 128), F32)],
        scratch_shapes=[pltpu.VMEM((HEADS, bb, 128), F32), pltpu.VMEM((HEADS, bb, 128), F32)],
        compiler_params=pltpu.CompilerParams(dimension_semantics=("arbitrary",),
                                             vmem_limit_bytes=VMEM_LIMIT),
        name="decode_step",
    )(proj, cs, s0, c0, n0, m0, conv_w, plane, gng, gnm)


def _regroup_w_in(w_in):
    sizes = (512, 512, 512, 512, 4, 4, 256, 256, 512, 512, 4, 4)
    offs = [0]
    for s in sizes:
        offs.append(offs[-1] + s)
    gq, gk, gv, gz, gb, ga, mq, mk, mv, mo, mi, mf = (w_in[:, offs[i]:offs[i + 1]] for i in range(12))
    pad = jnp.zeros((w_in.shape[0], 128 - 16), w_in.dtype)
    return jnp.concatenate([gq, gk, gv, gz, mv, mo, mq, mk, gb, ga, mi, mf, pad], axis=1).astype(BF16)


def _lane_vec(pairs):
    v = jnp.zeros((128,), F32)
    for off, val in pairs:
        v = v.at[off:off + HEADS].set(val.astype(F32))
    return v


def _prep_params(norm_pre_mix, w_in, conv_w, a_log, dt_bias, gdn_norm_g, b_igate, b_fgate, mlstm_norm_g,
                 w_out, norm_post_mix, norm_pre_mlp, w_up, w_down, norm_post_mlp):
    alog_v = _lane_vec([(LANE_GA, a_log[0])])
    dtb_v = _lane_vec([(LANE_GA, dt_bias[0])])
    bias_v = _lane_vec([(LANE_MI, b_igate[0]), (LANE_MF, b_fgate[0])])
    return dict(
        w_in=_regroup_w_in(w_in[0]), wo=w_out[0].astype(BF16), wu=w_up[0].astype(BF16),
        wd=w_down[0].astype(BF16),
        g_pre=norm_pre_mix[0][None, :], g1=norm_post_mix[0][None, :], g2=norm_pre_mlp[0][None, :],
        g3=norm_post_mlp[0][None, :], cw=conv_w[0], gng=gdn_norm_g[0][None, :], gnm=mlstm_norm_g[0][None, :],
        gdn_plane=jnp.stack([alog_v, dtb_v]),
        gdn_psub=jnp.stack([alog_v[:16], dtb_v[:16]], axis=1),
        ml_blane=bias_v[None, :],
        ml_bsub=bias_v[:16, None],
        dec_plane=jnp.stack([alog_v, dtb_v, bias_v]),
    )


def _prompt_path(x, conv0, s0, c0, n0, m0, prm, tb, tm):
    bsz, seq, d = x.shape
    hp = HEADS // 2
    x2d = x.reshape(bsz * seq, d)
    proj = _in_proj(x2d, prm["g_pre"], prm["w_in"], tm=tm).reshape(bsz, seq, N_PROJ)
    gates = proj[:, :, COL_GATE:COL_GATE + 16]
    grow = gates.reshape(bsz, seq // CHUNK, CHUNK, 16).transpose(0, 1, 3, 2)
    cs8 = jnp.pad(conv0, ((0, 0), (8 - (CONV_W - 1), 0), (0, 0)))
    og, s_new = _gdn_prompt(proj, grow, cs8, s0, prm["cw"], prm["gdn_plane"], prm["gdn_psub"], prm["gng"], tb=tb)
    om, c_new, n_new, m_new = _mlstm_prompt(
        proj, grow, c0.reshape(bsz, hp, 128, 128), n0.reshape(bsz, hp, 1, 128),
        jnp.pad(m0, ((0, 0), (0, 128 - HEADS)))[:, None, :], prm["ml_blane"], prm["ml_bsub"], prm["gnm"], tb=tb)
    y = _out_mlp(og.reshape(bsz * seq, -1), om.reshape(bsz * seq, -1), x2d, prm["wo"], prm["g1"], prm["g2"],
                 prm["wu"], prm["wd"], prm["g3"], tm=tm).reshape(bsz, seq, d)
    xp = jnp.concatenate([conv0, proj[:, seq - (CONV_W - 1):, COL_QKV:COL_QKV + CONV_CH]], axis=1)
    conv_new = xp[:, -(CONV_W - 1):]
    return y, (conv_new, s_new, c_new.reshape(bsz, HEADS, ML_DK, ML_DV), n_new.reshape(bsz, HEADS, ML_DK),
               m_new[:, 0, :HEADS])


def _sample_path(x, conv0, s0, c0, n0, m0, prm):
    dec, _, d = x.shape
    hp = HEADS // 2
    xs = x.reshape(dec, d)
    proj = _in_proj(xs, prm["g_pre"], prm["w_in"], tm=dec)
    og, om, s_new, c_new, n_new, m_new = _decode(
        proj, conv0, s0, c0.reshape(dec, hp, 128, 128), n0.reshape(dec, HEADS * ML_DK),
        jnp.pad(m0, ((0, 0), (0, 128 - HEADS))), prm["cw"], prm["dec_plane"], prm["gng"], prm["gnm"], bb=8)
    y = _out_mlp(og, om, xs, prm["wo"], prm["g1"], prm["g2"], prm["wu"], prm["wd"], prm["g3"],
                 tm=dec).reshape(dec, 1, d)
    conv_new = jnp.concatenate([conv0[:, 1:, :], proj[:, None, COL_QKV:COL_QKV + CONV_CH]], axis=1)
    return y, (conv_new, s_new, c_new.reshape(dec, HEADS, ML_DK, ML_DV), n_new.reshape(dec, HEADS, ML_DK),
               m_new[:, :HEADS])


def kernel(x_prompt, x_sample, state_gdn_conv, state_gdn_S, state_mlstm_C, state_mlstm_n, state_mlstm_m,
           norm_pre_mix, w_in, conv_w, a_log, dt_bias, gdn_norm_g, b_igate, b_fgate, mlstm_norm_g, w_out,
           norm_post_mix, norm_pre_mlp, w_up, w_down, norm_post_mlp):
    bsz = x_prompt.shape[0]
    prm = _prep_params(norm_pre_mix, w_in, conv_w, a_log, dt_bias, gdn_norm_g, b_igate, b_fgate, mlstm_norm_g,
                       w_out, norm_post_mix, norm_pre_mlp, w_up, w_down, norm_post_mlp)
    y_p, p_st = _prompt_path(
        x_prompt, jnp.zeros((bsz, CONV_W - 1, CONV_CH), F32), jnp.zeros((bsz, HEADS, GDN_D, GDN_D), F32),
        jnp.zeros((bsz, HEADS, ML_DK, ML_DV), F32), jnp.zeros((bsz, HEADS, ML_DK), F32),
        jnp.zeros((bsz, HEADS), F32), prm, tb=512, tm=512)
    y_s, s_st = _sample_path(x_sample, state_gdn_conv[0], state_gdn_S[0], state_mlstm_C[0], state_mlstm_n[0],
                             state_mlstm_m[0], prm)
    return (y_p, y_s) + tuple(a[None] for a in p_st) + tuple(a[None] for a in s_st)
```

```python
import functools

import jax
import jax.numpy as jnp
from jax import lax
from jax.experimental import pallas as pl
from jax.experimental.pallas import tpu as pltpu

F32 = jnp.float32
BF16 = jnp.bfloat16
EPS = 1e-6

D_MODEL = 1024
HEADS = 4
GDN_D = 128
ML_DK = 64
ML_DV = 128
CONV_W = 4
CONV_CH = 3 * HEADS * GDN_D
D_FF = 4 * D_MODEL
CHUNK = 64

COL_QKV = 0
COL_Z = 1536
COL_MV = 2048
COL_MO = 2560
COL_MQ = 3072
COL_MK = 3328
COL_GATE = 3584
N_PROJ = COL_GATE + 128
LANE_GB, LANE_GA, LANE_MI, LANE_MF = 0, 4, 8, 12

VMEM_LIMIT = 56 * 1024 * 1024


def _rms(x, g):
    return x * lax.rsqrt(jnp.mean(x * x, axis=-1, keepdims=True) + EPS) * g


def _softplus(x):
    return jnp.maximum(x, 0.0) + jnp.log1p(jnp.exp(-jnp.abs(x)))


def _sigmoid(x):
    return 1.0 / (1.0 + jnp.exp(-x))


def _mm(a, b):
    return jnp.dot(a.astype(BF16), b.astype(BF16), preferred_element_type=F32)


def _mm_nt(a, b):
    return lax.dot_general(a.astype(BF16), b.astype(BF16), (((1,), (1,)), ((), ())),
                           preferred_element_type=F32)


def _in_proj_kernel(x_ref, g_ref, w_ref, o_ref, gt_ref, h_ref, *, n_chunk):
    h_ref[...] = _rms(x_ref[...], g_ref[...]).astype(BF16)
    n = o_ref.shape[1]
    for c0 in range(0, n, n_chunk):
        c1 = min(c0 + n_chunk, n)
        o_ref[:, c0:c1] = jnp.dot(h_ref[...], w_ref[:, c0:c1], preferred_element_type=F32)
    gt = o_ref[:, COL_GATE:COL_GATE + 128].T
    for c in range(gt_ref.shape[0]):
        gt_ref[c] = gt[0:16, c * CHUNK:(c + 1) * CHUNK]


def _in_proj(x2d, g, w, tm):
    m, k = x2d.shape
    n = w.shape[1]
    return pl.pallas_call(
        functools.partial(_in_proj_kernel, n_chunk=512),
        grid=(m // tm,),
        in_specs=[pl.BlockSpec((tm, k), lambda i: (i, 0)),
                  pl.BlockSpec((1, k), lambda i: (0, 0)),
                  pl.BlockSpec((k, n), lambda i: (0, 0))],
        out_specs=[pl.BlockSpec((tm, n), lambda i: (i, 0)),
                   pl.BlockSpec((tm // CHUNK, 16, CHUNK), lambda i: (i, 0, 0))],
        out_shape=[jax.ShapeDtypeStruct((m, n), F32),
                   jax.ShapeDtypeStruct((m // CHUNK, 16, CHUNK), F32)],
        scratch_shapes=[pltpu.VMEM((tm, k), BF16)],
        compiler_params=pltpu.CompilerParams(dimension_semantics=("arbitrary",),
                                             vmem_limit_bytes=VMEM_LIMIT),
        name="in_proj",
    )(x2d, g, w)


def _out_mlp_kernel(og_ref, om_ref, x_ref, wo_ref, g1_ref, g2_ref, wu_ref, wd_ref, g3_ref,
                    y_ref, hn_ref, acc_ref, *, ff_chunk):
    half = og_ref.shape[1]
    mix = (jnp.dot(og_ref[...], wo_ref[0:half, :], preferred_element_type=F32)
           + jnp.dot(om_ref[...], wo_ref[half:2 * half, :], preferred_element_type=F32))
    x1 = x_ref[...] + _rms(mix, g1_ref[...])
    y_ref[...] = x1
    hn_ref[...] = _rms(x1, g2_ref[...]).astype(BF16)
    d_ff = wu_ref.shape[1]
    for c0 in range(0, d_ff, ff_chunk):
        u = jnp.dot(hn_ref[...], wu_ref[:, c0:c0 + ff_chunk], preferred_element_type=F32)
        u = jnp.square(jnp.maximum(u, 0.0)).astype(BF16)
        d = jnp.dot(u, wd_ref[c0:c0 + ff_chunk, :], preferred_element_type=F32)
        if c0 == 0:
            acc_ref[...] = d
        else:
            acc_ref[...] += d
    y_ref[...] = y_ref[...] + _rms(acc_ref[...], g3_ref[...])


def _out_mlp(og, om, x2d, wo, g1, g2, wu, wd, g3, tm):
    m, d = x2d.shape
    half = og.shape[1]
    d_ff = wu.shape[1]
    const = lambda i: (0, 0)
    return pl.pallas_call(
        functools.partial(_out_mlp_kernel, ff_chunk=1024),
        grid=(m // tm,),
        in_specs=[pl.BlockSpec((tm, half), lambda i: (i, 0)),
                  pl.BlockSpec((tm, half), lambda i: (i, 0)),
                  pl.BlockSpec((tm, d), lambda i: (i, 0)),
                  pl.BlockSpec((d, d), const, pipeline_mode=pl.Buffered(1)),
                  pl.BlockSpec((1, d), const),
                  pl.BlockSpec((1, d), const),
                  pl.BlockSpec((d, d_ff), const, pipeline_mode=pl.Buffered(1)),
                  pl.BlockSpec((d_ff, d), const, pipeline_mode=pl.Buffered(1)),
                  pl.BlockSpec((1, d), const)],
        out_specs=pl.BlockSpec((tm, d), lambda i: (i, 0)),
        out_shape=jax.ShapeDtypeStruct((m, d), F32),
        scratch_shapes=[pltpu.VMEM((tm, d), BF16), pltpu.VMEM((tm, d), F32)],
        compiler_params=pltpu.CompilerParams(dimension_semantics=("arbitrary",),
                                             vmem_limit_bytes=VMEM_LIMIT),
        name="out_mlp",
    )(og, om, x2d, wo, g1, g2, wu, wd, g3)


def _chunk_masks():
    ii = lax.broadcasted_iota(jnp.int32, (CHUNK, CHUNK), 0)
    jj = lax.broadcasted_iota(jnp.int32, (CHUNK, CHUNK), 1)
    return ii, jj


def _cumsum_col_row(x_col, x_row, ii, jj):
    c_col = jnp.sum(jnp.where(jj <= ii, x_row, 0.0), axis=1, keepdims=True)
    c_row = jnp.sum(jnp.where(ii <= jj, x_col, 0.0), axis=0, keepdims=True)
    return c_col, c_row


def _interleave(*gens):
    gens = list(gens)
    while gens:
        for g in list(gens):
            try:
                next(g)
            except StopIteration:
                gens.remove(g)


def _halves(items):
    mid = len(items) // 2
    return items[:mid], items[mid:]


def _gdn_phase_a(yc_ref, gcol_ref, grow_ref, plane_ref, psub_ref, wv_ref, lhs_ref, qk_ref, ket_ref, ge_ref,
                 ncb):
    ii, jj = _chunk_masks()
    incl = jj <= ii
    strict = jj < ii
    eye = (ii == jj).astype(F32)
    neg_a_lane = -jnp.exp(plane_ref[0:1, :])
    dtb_lane = plane_ref[1:2, :]
    neg_a_sub = -jnp.exp(psub_ref[:, 0:1])
    dtb_sub = psub_ref[:, 1:2]
    items = []
    for c in range(ncb):
        rows = slice(c * CHUNK, (c + 1) * CHUNK)
        gc = gcol_ref[rows, :]
        gr = grow_ref[c]
        beta_t = _sigmoid(gc)
        g_t = neg_a_lane * _softplus(gc + dtb_lane)
        g_r = neg_a_sub * _softplus(gr + dtb_sub)
        for h in range(HEADS):
            lo = h * GDN_D
            items.append(dict(c=c, h=h,
                              q=yc_ref[rows, lo:lo + GDN_D],
                              k=yc_ref[rows, 512 + lo:512 + lo + GDN_D],
                              v=yc_ref[rows, 1024 + lo:1024 + lo + GDN_D],
                              beta=beta_t[:, LANE_GB + h:LANE_GB + h + 1],
                              gg_col=g_t[:, LANE_GA + h:LANE_GA + h + 1],
                              gg_row=g_r[LANE_GA + h:LANE_GA + h + 1, :]))
    for it in items:
        it["k"] = it["k"] * lax.rsqrt(jnp.sum(it["k"] * it["k"], axis=-1, keepdims=True) + EPS)
    for it in items:
        it["kk"] = _mm_nt(it["k"], it["k"])
    for it in items:
        it["q"] = (it["q"] * lax.rsqrt(jnp.sum(it["q"] * it["q"], axis=-1, keepdims=True) + EPS)
                   * (GDN_D ** -0.5))
    for it in items:
        it["g_col"], g_row = _cumsum_col_row(it["gg_col"], it["gg_row"], ii, jj)
        it["decay"] = jnp.where(incl, jnp.exp(jnp.where(incl, it["g_col"] - g_row, 0.0)), 0.0)
    for it in items:
        n_mat = jnp.where(strict, it["beta"] * it["kk"] * it["decay"], 0.0)
        it["x"] = eye - n_mat
        it["p"] = -n_mat
    for _ in range(5):
        for it in items:
            it["p"] = _mm(it["p"], it["p"])
        for it in items:
            it["x"] = it["x"] + _mm(it["x"], it["p"])
    for it in items:
        e_g = jnp.exp(it["g_col"])
        it["e_g"] = e_g
        rhs = jnp.concatenate([it["beta"] * it["v"], (it["beta"] * e_g) * it["k"]], axis=1)
        it["w"] = _mm(it["x"], rhs)
    for it in items:
        it["qk"] = _mm_nt(it["q"], it["k"]) * it["decay"]
    for it in items:
        c, h = it["c"], it["h"]
        g_end = it["g_col"][CHUNK - 1:CHUNK, :]
        k_end = it["k"] * jnp.exp(g_end - it["g_col"])
        wv_ref[c, h] = it["w"][:, 0:GDN_D]
        lhs_ref[c, h] = jnp.concatenate([it["w"][:, GDN_D:2 * GDN_D], it["e_g"] * it["q"]],
                                        axis=0).astype(BF16)
        qk_ref[c, h] = it["qk"].astype(BF16)
        ket_ref[c, h] = k_end.T.astype(BF16)
        ge_ref[c, h] = jnp.broadcast_to(jnp.exp(g_end), (1, GDN_D))


def _gdn_phase_b(qkvz_ref, gn_ref, og_ref, s_ref, wv_ref, lhs_ref, qk_ref, ket_ref, ge_ref, ncb):
    gn = gn_ref[...]

    def epilogue(c, o):
        rows = slice(c * CHUNK, (c + 1) * CHUNK)
        for h in range(HEADS):
            lo = h * GDN_D
            z = qkvz_ref[rows, COL_Z + lo:COL_Z + lo + GDN_D]
            out = _rms(o[h], gn) * (z * _sigmoid(z))
            og_ref[rows, lo:lo + GDN_D] = out.astype(og_ref.dtype)

    s = [s_ref[h] for h in range(HEADS)]
    o_prev = None
    for c in range(ncb):
        r = [jnp.dot(lhs_ref[c, h], s[h].astype(BF16), preferred_element_type=F32) for h in range(HEADS)]
        yield
        if o_prev is not None:
            epilogue(c - 1, o_prev)
        ub = [(wv_ref[c, h] - r[h][0:CHUNK]).astype(BF16) for h in range(HEADS)]
        s = [ge_ref[c, h] * s[h] + jnp.dot(ket_ref[c, h], ub[h], preferred_element_type=F32)
             for h in range(HEADS)]
        o_prev = [r[h][CHUNK:2 * CHUNK] + jnp.dot(qk_ref[c, h], ub[h], preferred_element_type=F32)
                  for h in range(HEADS)]
        yield
    epilogue(ncb - 1, o_prev)
    for h in range(HEADS):
        s_ref[h] = s[h]


def _mlstm_block(mvo_ref, mqk_ref, gcol_ref, grow_ref, plane_ref, psub_ref, gn_ref, om_ref,
                 c_ref, n_ref, m_ref, ncb):
    ii, jj = _chunk_masks()
    incl = jj <= ii
    lane = lax.broadcasted_iota(jnp.int32, (1, 128), 1)
    row128 = lax.broadcasted_iota(jnp.int32, (128, 1), 0)
    gn = gn_ref[...]
    blane = plane_ref[2:3, :]
    bsub = psub_ref[:, 2:3]
    items = []
    for c in range(ncb):
        rows = slice(c * CHUNK, (c + 1) * CHUNK)
        gc = gcol_ref[rows, :] + blane
        gr = grow_ref[c] + bsub
        lf_c = -_softplus(-gc)
        lf_r = -_softplus(-gr)
        for p in range(HEADS // 2):
            qb = mqk_ref[rows, p * 128:(p + 1) * 128]
            kb = mqk_ref[rows, 256 + p * 128:256 + (p + 1) * 128] * (ML_DK ** -0.5)
            for e in range(2):
                h = 2 * p + e
                lm = (lane >= e * ML_DK) & (lane < (e + 1) * ML_DK)
                items.append(dict(c=c, rows=rows, p=p, e=e, h=h, qh=jnp.where(lm, qb, 0.0), kb=kb,
                                  kh=jnp.where(lm, kb, 0.0),
                                  lf_col=lf_c[:, LANE_MF + h:LANE_MF + h + 1],
                                  lf_row=lf_r[LANE_MF + h:LANE_MF + h + 1, :],
                                  ig_row=gr[LANE_MI + h:LANE_MI + h + 1, :],
                                  ig_col=gc[:, LANE_MI + h:LANE_MI + h + 1],
                                  v=mvo_ref[rows, h * ML_DV:(h + 1) * ML_DV]))
        if c % 2 == 1:
            yield
    for part in _halves(items):
        for it in part:
            it["qk"] = _mm_nt(it["qh"], it["kb"])
        yield
    for part in _halves(items):
        for it in part:
            it["f_col"], it["f_row"] = _cumsum_col_row(it["lf_col"], it["lf_row"], ii, jj)
        yield
    for part in _halves(items):
        for it in part:
            it["d_mat"] = jnp.where(incl, it["f_col"] - it["f_row"] + it["ig_row"], -jnp.inf)
            it["d_max"] = jnp.max(it["d_mat"], axis=1, keepdims=True)
        yield
    for part in _halves(items):
        for it in part:
            p0 = jnp.where(incl, jnp.exp(jnp.where(incl, it["d_mat"] - it["d_max"], 0.0)), 0.0)
            pend0 = jnp.exp(it["f_col"][CHUNK - 1:CHUNK, :] - it["f_col"] + it["ig_col"]
                            - it["d_max"][CHUNK - 1:CHUNK, :])
            it["kp0"] = it["kh"] * pend0
            it["pqk0"] = p0 * it["qk"]
        yield
    for part in _halves(items):
        for it in part:
            it["pv0"] = _mm(it["pqk0"], it["v"])
            it["rs0"] = jnp.sum(it["pqk0"], axis=-1, keepdims=True)
        yield
    for part in _halves(items):
        for it in part:
            it["cadd0"] = _mm(it["kp0"].T, it["v"])
            it["nadd0"] = jnp.sum(it["kp0"], axis=0, keepdims=True)
        yield

    m_cur = [m_ref[:, h:h + 1] for h in range(HEADS)]
    for it in items:
        h = it["h"]
        it["m_prev"] = m_cur[h]
        m_cur[h] = jnp.maximum(it["f_col"][CHUNK - 1:CHUNK, :] + m_cur[h], it["d_max"][CHUNK - 1:CHUNK, :])
    yield
    for part in _halves(items):
        for it in part:
            bcol = it["f_col"] + it["m_prev"]
            mt = jnp.maximum(bcol, it["d_max"])
            it["mt"] = mt
            it["w_prev"] = jnp.exp(bcol - mt)
            it["sc"] = jnp.exp(it["d_max"] - mt)
        yield
    c_cur = [c_ref[p] for p in range(HEADS // 2)]
    n_cur = [n_ref[p] for p in range(HEADS // 2)]
    for c in range(ncb):
        for p in range(HEADS // 2):
            pair = [it for it in items if it["c"] == c and it["p"] == p]
            w_end = [it["w_prev"][CHUNK - 1:CHUNK, :] for it in pair]
            s_end = [it["sc"][CHUNK - 1:CHUNK, :] for it in pair]
            for it in pair:
                it["c_prev"] = c_cur[p]
                it["n_prev"] = n_cur[p]
            c_cur[p] = (jnp.where(row128 < ML_DK, w_end[0], w_end[1]) * c_cur[p]
                        + s_end[0] * pair[0]["cadd0"] + s_end[1] * pair[1]["cadd0"])
            n_cur[p] = (jnp.where(lane < ML_DK, w_end[0], w_end[1]) * n_cur[p]
                        + s_end[0] * pair[0]["nadd0"] + s_end[1] * pair[1]["nadd0"])
        if c % 2 == 1:
            yield
    for part in _halves(items):
        for it in part:
            it["qc"] = _mm(it["qh"], it["c_prev"])
        yield
    for part in _halves(items):
        for it in part:
            num = it["w_prev"] * it["qc"] + it["sc"] * it["pv0"]
            den = (it["w_prev"] * jnp.sum(it["qh"] * it["n_prev"], axis=-1, keepdims=True)
                   + it["sc"] * it["rs0"])
            it["hh"] = num / jnp.maximum(jnp.abs(den), jnp.exp(-it["mt"]))
        yield
    for part in _halves(items):
        for it in part:
            h = it["h"]
            mo = mvo_ref[it["rows"], 512 + h * ML_DV:512 + (h + 1) * ML_DV]
            out = _sigmoid(mo) * _rms(it["hh"], gn)
            om_ref[it["rows"], h * ML_DV:(h + 1) * ML_DV] = out.astype(om_ref.dtype)
        yield
    for p in range(HEADS // 2):
        c_ref[p] = c_cur[p]
        n_ref[p] = n_cur[p]
    for h in range(HEADS):
        m_ref[:, h:h + 1] = m_cur[h]


def _mixer_kernel(qkvz_ref, mvo_ref, mqk_ref, gcol_ref, grow_ref, cs_ref, s0_ref, c0_ref, n0_ref, m0_ref,
                  cw_ref, plane_ref, psub_ref, gng_ref, gnm_ref,
                  og_ref, om_ref, sout_ref, cout_ref, nout_ref, mout_ref,
                  xp_ref, yc_ref, s_ref, wv_ref, lhs_ref, qk_ref, ket_ref, ge_ref, c_ref, n_ref, m_ref,
                  *, tb):
    t = pl.program_id(1)
    nt = pl.num_programs(1)
    ncb = tb // CHUNK

    @pl.when(t == 0)
    def _():
        xp_ref[0:8, :] = cs_ref[...]
        s_ref[...] = s0_ref[...]
        c_ref[...] = c0_ref[...]
        n_ref[...] = n0_ref[...]
        m_ref[...] = m0_ref[...]

    @pl.when(t > 0)
    def _():
        xp_ref[0:8, :] = xp_ref[tb:tb + 8, :]

    xp_ref[8:tb + 8, :] = qkvz_ref[:, COL_QKV:COL_QKV + CONV_CH]

    def conv_tile(ct, carry):
        c0 = pl.multiple_of(ct * 128, 128)
        w = cw_ref[:, pl.ds(c0, 128)]
        for r0 in range(0, tb, 128):
            acc = xp_ref[5 + r0:5 + r0 + 128, pl.ds(c0, 128)] * w[0:1, :]
            for j in range(1, CONV_W):
                acc = acc + xp_ref[5 + j + r0:5 + j + r0 + 128, pl.ds(c0, 128)] * w[j:j + 1, :]
            yc_ref[r0:r0 + 128, pl.ds(c0, 128)] = acc * _sigmoid(acc)
        return carry

    lax.fori_loop(0, CONV_CH // 128, conv_tile, 0)

    _gdn_phase_a(yc_ref, gcol_ref, grow_ref, plane_ref, psub_ref, wv_ref, lhs_ref, qk_ref, ket_ref, ge_ref, ncb)
    _interleave(
        _gdn_phase_b(qkvz_ref, gng_ref, og_ref, s_ref, wv_ref, lhs_ref, qk_ref, ket_ref, ge_ref, ncb),
        _mlstm_block(mvo_ref, mqk_ref, gcol_ref, grow_ref, plane_ref, psub_ref, gnm_ref, om_ref,
                     c_ref, n_ref, m_ref, ncb))

    @pl.when(t == nt - 1)
    def _():
        sout_ref[...] = s_ref[...]
        cout_ref[...] = c_ref[...]
        nout_ref[...] = n_ref[...]
        mout_ref[...] = m_ref[...]


def _mixer_prompt(proj, gates_t, cs8, s0, c0, n0, m0, conv_w, plane, psub, gng, gnm, tb):
    b, t, _ = proj.shape
    nt = t // tb
    ncb = tb // CHUNK
    hp = HEADS // 2
    per_seq4 = lambda i, j: (i, 0, 0, 0)
    const = lambda i, j: (0, 0)
    return pl.pallas_call(
        functools.partial(_mixer_kernel, tb=tb),
        grid=(b, nt),
        in_specs=[pl.BlockSpec((None, tb, 2048), lambda i, j: (i, j, 0)),
                  pl.BlockSpec((None, tb, 1024), lambda i, j: (i, j, COL_MV // 1024)),
                  pl.BlockSpec((None, tb, 512), lambda i, j: (i, j, COL_MQ // 512)),
                  pl.BlockSpec((None, tb, 128), lambda i, j: (i, j, COL_GATE // 128)),
                  pl.BlockSpec((None, ncb, 16, CHUNK), lambda i, j: (i, j, 0, 0)),
                  pl.BlockSpec((None, 8, CONV_CH), lambda i, j: (i, 0, 0)),
                  pl.BlockSpec((None, HEADS, GDN_D, GDN_D), per_seq4),
                  pl.BlockSpec((None, hp, 128, 128), per_seq4),
                  pl.BlockSpec((None, hp, 1, 128), per_seq4),
                  pl.BlockSpec((None, 1, 128), lambda i, j: (i, 0, 0)),
                  pl.BlockSpec((CONV_W, CONV_CH), const),
                  pl.BlockSpec((3, 128), const),
                  pl.BlockSpec((16, 3), const),
                  pl.BlockSpec((1, GDN_D), const),
                  pl.BlockSpec((1, ML_DV), const)],
        out_specs=[pl.BlockSpec((None, tb, HEADS * GDN_D), lambda i, j: (i, j, 0)),
                   pl.BlockSpec((None, tb, HEADS * ML_DV), lambda i, j: (i, j, 0)),
                   pl.BlockSpec((None, HEADS, GDN_D, GDN_D), per_seq4),
                   pl.BlockSpec((None, hp, 128, 128), per_seq4),
                   pl.BlockSpec((None, hp, 1, 128), per_seq4),
                   pl.BlockSpec((None, 1, 128), lambda i, j: (i, 0, 0))],
        out_shape=[jax.ShapeDtypeStruct((b, t, HEADS * GDN_D), BF16),
                   jax.ShapeDtypeStruct((b, t, HEADS * ML_DV), BF16),
                   jax.ShapeDtypeStruct((b, HEADS, GDN_D, GDN_D), F32),
                   jax.ShapeDtypeStruct((b, hp, 128, 128), F32),
                   jax.ShapeDtypeStruct((b, hp, 1, 128), F32),
                   jax.ShapeDtypeStruct((b, 1, 128), F32)],
        scratch_shapes=[pltpu.VMEM((tb + 8, CONV_CH), F32),
                        pltpu.VMEM((tb, CONV_CH), F32),
                        pltpu.VMEM((HEADS, GDN_D, GDN_D), F32),
                        pltpu.VMEM((ncb, HEADS, CHUNK, GDN_D), F32),
                        pltpu.VMEM((ncb, HEADS, 2 * CHUNK, GDN_D), BF16),
                        pltpu.VMEM((ncb, HEADS, CHUNK, CHUNK), BF16),
                        pltpu.VMEM((ncb, HEADS, GDN_D, CHUNK), BF16),
                        pltpu.VMEM((ncb, HEADS, 1, GDN_D), F32),
                        pltpu.VMEM((hp, 128, 128), F32),
                        pltpu.VMEM((hp, 1, 128), F32),
                        pltpu.VMEM((1, 128), F32)],
        compiler_params=pltpu.CompilerParams(dimension_semantics=("arbitrary", "arbitrary"),
                                             vmem_limit_bytes=VMEM_LIMIT),
        name="mixer_prompt",
    )(proj, proj, proj, proj, gates_t, cs8, s0, c0, n0, m0, conv_w, plane, psub, gng, gnm)


def _decode_kernel(proj_ref, cs_ref, s0_ref, c0_ref, n0_ref, m0_ref, cw_ref, plane_ref, gng_ref, gnm_ref,
                   og_ref, om_ref, sout_ref, cout_ref, nout_ref, mout_ref, rowg_ref, rowm_ref, *, bb):
    lane = lax.broadcasted_iota(jnp.int32, (1, 128), 1)
    gt = proj_ref[:, COL_GATE:COL_GATE + 128]
    beta_t = _sigmoid(gt)
    g_t = -jnp.exp(plane_ref[0:1, :]) * _softplus(gt + plane_ref[1:2, :])
    eg_t = jnp.exp(g_t)
    gb_t = gt + plane_ref[2:3, :]
    lf_t = -_softplus(-gb_t)

    heads = []
    for h in range(HEADS):
        cols = []
        for part in range(3):
            c0 = part * 512 + h * GDN_D
            w = cw_ref[:, c0:c0 + GDN_D]
            acc = cs_ref[:, 0, c0:c0 + GDN_D] * w[0:1, :]
            acc = acc + cs_ref[:, 1, c0:c0 + GDN_D] * w[1:2, :]
            acc = acc + cs_ref[:, 2, c0:c0 + GDN_D] * w[2:3, :]
            acc = acc + proj_ref[:, COL_QKV + c0:COL_QKV + c0 + GDN_D] * w[3:4, :]
            cols.append(acc * _sigmoid(acc))
        q, k, v = cols
        q = q * lax.rsqrt(jnp.sum(q * q, axis=-1, keepdims=True) + EPS) * (GDN_D ** -0.5)
        k = k * lax.rsqrt(jnp.sum(k * k, axis=-1, keepdims=True) + EPS)
        heads.append(dict(v=v, qk=jnp.sum(q * k, axis=-1, keepdims=True), q_t=q.T, k_t=k.T,
                          beta=beta_t[:, LANE_GB + h:LANE_GB + h + 1],
                          eg=eg_t[:, LANE_GA + h:LANE_GA + h + 1]))
    items = [dict(h=h, b=b) for h in range(HEADS) for b in range(bb)]
    for it in items:
        hd, b = heads[it["h"]], it["b"]
        it["k_col"] = hd["k_t"][:, b:b + 1]
        it["q_col"] = hd["q_t"][:, b:b + 1]
    for it in items:
        s_bh = s0_ref[it["b"], it["h"]]
        it["ks"] = jnp.sum(it["k_col"] * s_bh, axis=0, keepdims=True)
        it["qs"] = jnp.sum(it["q_col"] * s_bh, axis=0, keepdims=True)
    for it in items:
        hd, b = heads[it["h"]], it["b"]
        eg_b = hd["eg"][b:b + 1, :]
        it["eg_b"] = eg_b
        it["u"] = hd["beta"][b:b + 1, :] * (hd["v"][b:b + 1, :] - eg_b * it["ks"])
        rowg_ref[it["h"], b:b + 1, :] = eg_b * it["qs"] + hd["qk"][b:b + 1, :] * it["u"]
    for it in items:
        sout_ref[it["b"], it["h"]] = it["eg_b"] * s0_ref[it["b"], it["h"]] + it["k_col"] * it["u"]
    for h in range(HEADS):
        z = proj_ref[:, COL_Z + h * GDN_D:COL_Z + (h + 1) * GDN_D]
        out = _rms(rowg_ref[h], gng_ref[...]) * (z * _sigmoid(z))
        og_ref[:, h * GDN_D:(h + 1) * GDN_D] = out.astype(og_ref.dtype)

    heads = []
    for p in range(HEADS // 2):
        qb = proj_ref[:, COL_MQ + p * 128:COL_MQ + (p + 1) * 128]
        kb = proj_ref[:, COL_MK + p * 128:COL_MK + (p + 1) * 128] * (ML_DK ** -0.5)
        n_p = n0_ref[:, p * 128:(p + 1) * 128]
        q_t = qb.T
        k_t = kb.T
        w_prev, p_in = [], []
        for e in range(2):
            h = 2 * p + e
            lm = (lane >= e * ML_DK) & (lane < (e + 1) * ML_DK)
            ig = gb_t[:, LANE_MI + h:LANE_MI + h + 1]
            lf = lf_t[:, LANE_MF + h:LANE_MF + h + 1]
            m_old = m0_ref[:, h:h + 1]
            m_new = jnp.maximum(lf + m_old, ig)
            w_prev.append(jnp.exp(lf + m_old - m_new))
            p_in.append(jnp.exp(ig - m_new))
            qk = jnp.sum(jnp.where(lm, qb * kb, 0.0), axis=-1, keepdims=True)
            qn = jnp.sum(jnp.where(lm, qb * n_p, 0.0), axis=-1, keepdims=True)
            mout_ref[:, h:h + 1] = m_new
            heads.append(dict(p=p, e=e, q_t=q_t, k_t=k_t, w_prev=w_prev[e], p_in=p_in[e],
                              pqk=p_in[e] * qk, wqn=w_prev[e] * qn, floor=jnp.exp(-m_new),
                              v=proj_ref[:, COL_MV + h * ML_DV:COL_MV + (h + 1) * ML_DV]))
        lo_lane = lane < ML_DK
        nout_ref[:, p * 128:(p + 1) * 128] = (jnp.where(lo_lane, w_prev[0], w_prev[1]) * n_p
                                              + jnp.where(lo_lane, p_in[0], p_in[1]) * kb)
    items = [dict(h=h, b=b) for h in range(HEADS) for b in range(bb)]
    for it in items:
        hd, b = heads[it["h"]], it["b"]
        rs = slice(hd["e"] * ML_DK, (hd["e"] + 1) * ML_DK)
        it["rs"] = rs
        it["q_col"] = hd["q_t"][rs, b:b + 1]
        it["k_col"] = hd["k_t"][rs, b:b + 1]
    for it in items:
        hd, b = heads[it["h"]], it["b"]
        c_bh = c0_ref[b, hd["p"], it["rs"], :]
        it["qc"] = jnp.sum(it["q_col"] * c_bh, axis=0, keepdims=True)
    for it in items:
        hd, b = heads[it["h"]], it["b"]
        pqk = hd["pqk"][b:b + 1, :]
        num = hd["w_prev"][b:b + 1, :] * it["qc"] + pqk * hd["v"][b:b + 1, :]
        den = hd["wqn"][b:b + 1, :] + pqk
        rowm_ref[it["h"], b:b + 1, :] = num / jnp.maximum(jnp.abs(den), hd["floor"][b:b + 1, :])
    for it in items:
        hd, b = heads[it["h"]], it["b"]
        cout_ref[b, hd["p"], it["rs"], :] = (hd["w_prev"][b:b + 1, :] * c0_ref[b, hd["p"], it["rs"], :]
                                             + (hd["p_in"][b:b + 1, :] * it["k_col"]) * hd["v"][b:b + 1, :])
    for h in range(HEADS):
        mo = proj_ref[:, COL_MO + h * ML_DV:COL_MO + (h + 1) * ML_DV]
        out = _sigmoid(mo) * _rms(rowm_ref[h], gnm_ref[...])
        om_ref[:, h * ML_DV:(h + 1) * ML_DV] = out.astype(om_ref.dtype)
    mout_ref[:, HEADS:128] = m0_ref[:, HEADS:128]


def _decode(proj, cs, s0, c0, n0, m0, conv_w, plane, gng, gnm, bb):
    b = proj.shape[0]
    hp = HEADS // 2
    return pl.pallas_call(
        functools.partial(_decode_kernel, bb=bb),
        grid=(b // bb,),
        in_specs=[pl.BlockSpec((bb, N_PROJ), lambda i: (i, 0)),
                  pl.BlockSpec((bb, CONV_W - 1, CONV_CH), lambda i: (i, 0, 0)),
                  pl.BlockSpec((bb, HEADS, GDN_D, GDN_D), lambda i: (i, 0, 0, 0)),
                  pl.BlockSpec((bb, hp, 128, 128), lambda i: (i, 0, 0, 0)),
                  pl.BlockSpec((bb, HEADS * ML_DK), lambda i: (i, 0)),
                  pl.BlockSpec((bb, 128), lambda i: (i, 0)),
                  pl.BlockSpec((CONV_W, CONV_CH), lambda i: (0, 0)),
                  pl.BlockSpec((3, 128), lambda i: (0, 0)),
                  pl.BlockSpec((1, GDN_D), lambda i: (0, 0)),
                  pl.BlockSpec((1, ML_DV), lambda i: (0, 0))],
        out_specs=[pl.BlockSpec((bb, HEADS * GDN_D), lambda i: (i, 0)),
                   pl.BlockSpec((bb, HEADS * ML_DV), lambda i: (i, 0)),
                   pl.BlockSpec((bb, HEADS, GDN_D, GDN_D), lambda i: (i, 0, 0, 0)),
                   pl.BlockSpec((bb, hp, 128, 128), lambda i: (i, 0, 0, 0)),
                   pl.BlockSpec((bb, HEADS * ML_DK), lambda i: (i, 0)),
                   pl.BlockSpec((bb, 128), lambda i: (i, 0))],
        out_shape=[jax.ShapeDtypeStruct((b, HEADS * GDN_D), BF16),
                   jax.ShapeDtypeStruct((b, HEADS * ML_DV), BF16),
                   jax.ShapeDtypeStruct((b, HEADS, GDN_D, GDN_D), F32),
                   jax.ShapeDtypeStruct((b, hp, 128, 128), F32),
                   jax.ShapeDtypeStruct((b, HEADS * ML_DK), F32),
                   jax.ShapeDtypeStruct((b, 128), F32)],
        scratch_shapes=[pltpu.VMEM((HEADS, bb, 128), F32), pltpu.VMEM((HEADS, bb, 128), F32)],
        compiler_params=pltpu.CompilerParams(dimension_semantics=("arbitrary",),
                                             vmem_limit_bytes=VMEM_LIMIT),
        name="decode_step",
    )(proj, cs, s0, c0, n0, m0, conv_w, plane, gng, gnm)


def _regroup_w_in(w_in):
    sizes = (512, 512, 512, 512, 4, 4, 256, 256, 512, 512, 4, 4)
    offs = [0]
    for s in sizes:
        offs.append(offs[-1] + s)
    gq, gk, gv, gz, gb, ga, mq, mk, mv, mo, mi, mf = (w_in[:, offs[i]:offs[i + 1]] for i in range(12))
    pad = jnp.zeros((w_in.shape[0], 128 - 16), w_in.dtype)
    return jnp.concatenate([gq, gk, gv, gz, mv, mo, mq, mk, gb, ga, mi, mf, pad], axis=1).astype(BF16)


def _lane_vec(pairs):
    v = jnp.zeros((128,), F32)
    for off, val in pairs:
        v = v.at[off:off + HEADS].set(val.astype(F32))
    return v


def _prep_params(norm_pre_mix, w_in, conv_w, a_log, dt_bias, gdn_norm_g, b_igate, b_fgate, mlstm_norm_g,
                 w_out, norm_post_mix, norm_pre_mlp, w_up, w_down, norm_post_mlp):
    alog_v = _lane_vec([(LANE_GA, a_log[0])])
    dtb_v = _lane_vec([(LANE_GA, dt_bias[0])])
    bias_v = _lane_vec([(LANE_MI, b_igate[0]), (LANE_MF, b_fgate[0])])
    return dict(
        w_in=_regroup_w_in(w_in[0]), wo=w_out[0].astype(BF16), wu=w_up[0].astype(BF16),
        wd=w_down[0].astype(BF16),
        g_pre=norm_pre_mix[0][None, :], g1=norm_post_mix[0][None, :], g2=norm_pre_mlp[0][None, :],
        g3=norm_post_mlp[0][None, :], cw=conv_w[0], gng=gdn_norm_g[0][None, :], gnm=mlstm_norm_g[0][None, :],
        plane=jnp.stack([alog_v, dtb_v, bias_v]),
        psub=jnp.stack([alog_v[:16], dtb_v[:16], bias_v[:16]], axis=1),
    )


def _prompt_path(x, conv0, s0, c0, n0, m0, prm, tb, tm):
    bsz, seq, d = x.shape
    hp = HEADS // 2
    x2d = x.reshape(bsz * seq, d)
    proj, gates_t = _in_proj(x2d, prm["g_pre"], prm["w_in"], tm=tm)
    proj = proj.reshape(bsz, seq, N_PROJ)
    gates_t = gates_t.reshape(bsz, seq // CHUNK, 16, CHUNK)
    cs8 =jnp.pad(conv0, ((0, 0), (8 - (CONV_W - 1), 0), (0, 0)))
    og, om, s_new, c_new, n_new, m_new = _mixer_prompt(
        proj, gates_t, cs8, s0, c0.reshape(bsz, hp, 128, 128), n0.reshape(bsz, hp, 1, 128),
        jnp.pad(m0, ((0, 0), (0, 128 - HEADS)))[:, None, :], prm["cw"], prm["plane"], prm["psub"],
        prm["gng"], prm["gnm"], tb=tb)
    y = _out_mlp(og.reshape(bsz * seq, -1), om.reshape(bsz * seq, -1), x2d, prm["wo"], prm["g1"], prm["g2"],
                 prm["wu"], prm["wd"], prm["g3"], tm=tm).reshape(bsz, seq, d)
    xp = jnp.concatenate([conv0, proj[:, seq - (CONV_W - 1):, COL_QKV:COL_QKV + CONV_CH]], axis=1)
    conv_new = xp[:, -(CONV_W - 1):]
    return y, (conv_new, s_new, c_new.reshape(bsz, HEADS, ML_DK, ML_DV), n_new.reshape(bsz, HEADS, ML_DK),
               m_new[:, 0, :HEADS])


def _sample_path(x, conv0, s0, c0, n0, m0, prm):
    dec, _, d = x.shape
    hp = HEADS // 2
    xs = x.reshape(dec, d)
    proj, _ = _in_proj(xs, prm["g_pre"], prm["w_in"], tm=dec)
    og, om, s_new, c_new, n_new, m_new = _decode(
        proj, conv0, s0, c0.reshape(dec, hp, 128, 128), n0.reshape(dec, HEADS * ML_DK),
        jnp.pad(m0, ((0, 0), (0, 128 - HEADS))), prm["cw"], prm["plane"], prm["gng"], prm["gnm"], bb=8)
    y = _out_mlp(og, om, xs, prm["wo"], prm["g1"], prm["g2"], prm["wu"], prm["wd"], prm["g3"],
                 tm=dec).reshape(dec, 1, d)
    conv_new = jnp.concatenate([conv0[:, 1:, :], proj[:, None, COL_QKV:COL_QKV + CONV_CH]], axis=1)
    return y, (conv_new, s_new, c_new.reshape(dec, HEADS, ML_DK, ML_DV), n_new.reshape(dec, HEADS, ML_DK),
               m_new[:, :HEADS])


def kernel(x_prompt, x_sample, state_gdn_conv, state_gdn_S, state_mlstm_C, state_mlstm_n, state_mlstm_m,
           norm_pre_mix, w_in, conv_w, a_log, dt_bias, gdn_norm_g, b_igate, b_fgate, mlstm_norm_g, w_out,
           norm_post_mix, norm_pre_mlp, w_up, w_down, norm_post_mlp):
    bsz = x_prompt.shape[0]
    prm = _prep_params(norm_pre_mix, w_in, conv_w, a_log, dt_bias, gdn_norm_g, b_igate, b_fgate, mlstm_norm_g,
                       w_out, norm_post_mix, norm_pre_mlp, w_up, w_down, norm_post_mlp)
    y_p, p_st = _prompt_path(
        x_prompt, jnp.zeros((bsz, CONV_W - 1, CONV_CH), F32), jnp.zeros((bsz, HEADS, GDN_D, GDN_D), F32),
        jnp.zeros((bsz, HEADS, ML_DK, ML_DV), F32), jnp.zeros((bsz, HEADS, ML_DK), F32),
        jnp.zeros((bsz, HEADS), F32), prm, tb=512, tm=512)
    y_s, s_st = _sample_path(x_sample, state_gdn_conv[0], state_gdn_S[0], state_mlstm_C[0], state_mlstm_n[0],
                             state_mlstm_m[0], prm)
    return (y_p, y_s) + tuple(a[None] for a in p_st) + tuple(a[None] for a in s_st)
```

```python
import functools

import jax
import jax.numpy as jnp
from jax import lax
from jax.experimental import pallas as pl
from jax.experimental.pallas import tpu as pltpu

F32 = jnp.float32
BF16 = jnp.bfloat16
EPS = 1e-6

D_MODEL = 1024
HEADS = 4
GDN_D = 128
ML_DK = 64
ML_DV = 128
CONV_W = 4
CONV_CH = 3 * HEADS * GDN_D
D_FF = 4 * D_MODEL
CHUNK = 64

COL_QKV = 0
COL_Z = 1536
COL_MV = 2048
COL_MO = 2560
COL_MQ = 3072
COL_MK = 3328
COL_GATE = 3584
N_PROJ = COL_GATE + 128
LANE_GB, LANE_GA, LANE_MI, LANE_MF = 0, 4, 8, 12

VMEM_LIMIT = 56 * 1024 * 1024


def _rms(x, g):
    return x * lax.rsqrt(jnp.mean(x * x, axis=-1, keepdims=True) + EPS) * g


def _softplus(x):
    return jnp.maximum(x, 0.0) + jnp.log1p(jnp.exp(-jnp.abs(x)))


def _sigmoid(x):
    return 1.0 / (1.0 + jnp.exp(-x))


def _mm(a, b):
    return jnp.dot(a.astype(BF16), b.astype(BF16), preferred_element_type=F32)


def _mm_nt(a, b):
    return lax.dot_general(a.astype(BF16), b.astype(BF16), (((1,), (1,)), ((), ())),
                           preferred_element_type=F32)


def _in_proj_kernel(x_ref, g_ref, w_ref, o_ref, gc_ref, gt_ref, h_ref, *, n_chunk):
    h_ref[...] = _rms(x_ref[...], g_ref[...]).astype(BF16)
    n = o_ref.shape[1]
    for c0 in range(0, n, n_chunk):
        c1 = min(c0 + n_chunk, n)
        o_ref[:, c0:c1] = jnp.dot(h_ref[...], w_ref[:, c0:c1], preferred_element_type=F32).astype(o_ref.dtype)
    gates = jnp.dot(h_ref[...], w_ref[:, COL_GATE:COL_GATE + 128], preferred_element_type=F32)
    gc_ref[...] = gates
    gt = gates.T
    for c in range(gt_ref.shape[0]):
        gt_ref[c] = gt[0:16, c * CHUNK:(c + 1) * CHUNK]


def _in_proj(x2d, g, w, tm, out_dtype):
    m, k = x2d.shape
    n = w.shape[1]
    return pl.pallas_call(
        functools.partial(_in_proj_kernel, n_chunk=512),
        grid=(m // tm,),
        in_specs=[pl.BlockSpec((tm, k), lambda i: (i, 0)),
                  pl.BlockSpec((1, k), lambda i: (0, 0)),
                  pl.BlockSpec((k, n), lambda i: (0, 0))],
        out_specs=[pl.BlockSpec((tm, COL_GATE), lambda i: (i, 0)),
                   pl.BlockSpec((tm, 128), lambda i: (i, 0)),
                   pl.BlockSpec((tm // CHUNK, 16, CHUNK), lambda i: (i, 0, 0))],
        out_shape=[jax.ShapeDtypeStruct((m, COL_GATE), out_dtype),
                   jax.ShapeDtypeStruct((m, 128), F32),
                   jax.ShapeDtypeStruct((m // CHUNK, 16, CHUNK), F32)],
        scratch_shapes=[pltpu.VMEM((tm, k), BF16)],
        compiler_params=pltpu.CompilerParams(dimension_semantics=("arbitrary",),
                                             vmem_limit_bytes=VMEM_LIMIT),
        name="in_proj",
    )(x2d, g, w)


def _out_mlp_kernel(og_ref, om_ref, x_ref, wo_ref, g1_ref, g2_ref, wu_ref, wd_ref, g3_ref,
                    y_ref, hn_ref, acc_ref, *, ff_chunk):
    half = og_ref.shape[1]
    mix = (jnp.dot(og_ref[...], wo_ref[0:half, :], preferred_element_type=F32)
           + jnp.dot(om_ref[...], wo_ref[half:2 * half, :], preferred_element_type=F32))
    x1 = x_ref[...] + _rms(mix, g1_ref[...])
    y_ref[...] = x1
    hn_ref[...] = _rms(x1, g2_ref[...]).astype(BF16)
    d_ff = wu_ref.shape[1]
    for c0 in range(0, d_ff, ff_chunk):
        u = jnp.dot(hn_ref[...], wu_ref[:, c0:c0 + ff_chunk], preferred_element_type=F32)
        u = jnp.square(jnp.maximum(u, 0.0)).astype(BF16)
        d = jnp.dot(u, wd_ref[c0:c0 + ff_chunk, :], preferred_element_type=F32)
        if c0 == 0:
            acc_ref[...] = d
        else:
            acc_ref[...] += d
    y_ref[...] = y_ref[...] + _rms(acc_ref[...], g3_ref[...])


def _out_mlp(og, om, x2d, wo, g1, g2, wu, wd, g3, tm):
    m, d = x2d.shape
    half = og.shape[1]
    d_ff = wu.shape[1]
    const = lambda i: (0, 0)
    return pl.pallas_call(
        functools.partial(_out_mlp_kernel, ff_chunk=1024),
        grid=(m // tm,),
        in_specs=[pl.BlockSpec((tm, half), lambda i: (i, 0)),
                  pl.BlockSpec((tm, half), lambda i: (i, 0)),
                  pl.BlockSpec((tm, d), lambda i: (i, 0)),
                  pl.BlockSpec((d, d), const, pipeline_mode=pl.Buffered(1)),
                  pl.BlockSpec((1, d), const),
                  pl.BlockSpec((1, d), const),
                  pl.BlockSpec((d, d_ff), const, pipeline_mode=pl.Buffered(1)),
                  pl.BlockSpec((d_ff, d), const, pipeline_mode=pl.Buffered(1)),
                  pl.BlockSpec((1, d), const)],
        out_specs=pl.BlockSpec((tm, d), lambda i: (i, 0)),
        out_shape=jax.ShapeDtypeStruct((m, d), F32),
        scratch_shapes=[pltpu.VMEM((tm, d), BF16), pltpu.VMEM((tm, d), F32)],
        compiler_params=pltpu.CompilerParams(dimension_semantics=("arbitrary",),
                                             vmem_limit_bytes=VMEM_LIMIT),
        name="out_mlp",
    )(og, om, x2d, wo, g1, g2, wu, wd, g3)


def _chunk_masks():
    ii = lax.broadcasted_iota(jnp.int32, (CHUNK, CHUNK), 0)
    jj = lax.broadcasted_iota(jnp.int32, (CHUNK, CHUNK), 1)
    return ii, jj


def _cumsum_col_row(x_col, x_row, ii, jj):
    c_col = jnp.sum(jnp.where(jj <= ii, x_row, 0.0), axis=1, keepdims=True)
    c_row = jnp.sum(jnp.where(ii <= jj, x_col, 0.0), axis=0, keepdims=True)
    return c_col, c_row


def _interleave(*gens):
    gens = list(gens)
    while gens:
        for g in list(gens):
            try:
                next(g)
            except StopIteration:
                gens.remove(g)


def _halves(items):
    mid = len(items) // 2
    return items[:mid], items[mid:]


def _gdn_phase_a(yc_ref, gcol_ref, grow_ref, plane_ref, psub_ref, wv_ref, lhs_ref, qk_ref, ket_ref, ge_ref,
                 ncb):
    ii, jj = _chunk_masks()
    incl = jj <= ii
    strict = jj < ii
    eye = (ii == jj).astype(F32)
    neg_a_lane = -jnp.exp(plane_ref[0:1, :])
    dtb_lane = plane_ref[1:2, :]
    neg_a_sub = -jnp.exp(psub_ref[:, 0:1])
    dtb_sub = psub_ref[:, 1:2]
    items = []
    for c in range(ncb):
        rows = slice(c * CHUNK, (c + 1) * CHUNK)
        gc = gcol_ref[rows, :]
        gr = grow_ref[c]
        beta_t = _sigmoid(gc)
        g_t = neg_a_lane * _softplus(gc + dtb_lane)
        g_r = neg_a_sub * _softplus(gr + dtb_sub)
        for h in range(HEADS):
            lo = h * GDN_D
            items.append(dict(c=c, h=h,
                              q=yc_ref[rows, lo:lo + GDN_D],
                              k=yc_ref[rows, 512 + lo:512 + lo + GDN_D],
                              v=yc_ref[rows, 1024 + lo:1024 + lo + GDN_D],
                              beta=beta_t[:, LANE_GB + h:LANE_GB + h + 1],
                              gg_col=g_t[:, LANE_GA + h:LANE_GA + h + 1],
                              gg_row=g_r[LANE_GA + h:LANE_GA + h + 1, :]))
    for it in items:
        it["k"] = it["k"] * lax.rsqrt(jnp.sum(it["k"] * it["k"], axis=-1, keepdims=True) + EPS)
    for it in items:
        it["kk"] = _mm_nt(it["k"], it["k"])
    for it in items:
        it["q"] = (it["q"] * lax.rsqrt(jnp.sum(it["q"] * it["q"], axis=-1, keepdims=True) + EPS)
                   * (GDN_D ** -0.5))
    for it in items:
        it["g_col"], g_row = _cumsum_col_row(it["gg_col"], it["gg_row"], ii, jj)
        it["decay"] = jnp.where(incl, jnp.exp(jnp.where(incl, it["g_col"] - g_row, 0.0)), 0.0)
    for it in items:
        n_mat = jnp.where(strict, it["beta"] * it["kk"] * it["decay"], 0.0)
        it["x"] = eye - n_mat
        it["p"] = -n_mat
    for _ in range(5):
        for it in items:
            it["p"] = _mm(it["p"], it["p"])
        for it in items:
            it["x"] = it["x"] + _mm(it["x"], it["p"])
    for it in items:
        e_g = jnp.exp(it["g_col"])
        it["e_g"] = e_g
        rhs = jnp.concatenate([it["beta"] * it["v"], (it["beta"] * e_g) * it["k"]], axis=1)
        it["w"] = _mm(it["x"], rhs)
    for it in items:
        it["qk"] = _mm_nt(it["q"], it["k"]) * it["decay"]
    for it in items:
        c, h = it["c"], it["h"]
        g_end = it["g_col"][CHUNK - 1:CHUNK, :]
        k_end = it["k"] * jnp.exp(g_end - it["g_col"])
        wv_ref[c, h] = it["w"][:, 0:GDN_D]
        lhs_ref[c, h] = jnp.concatenate([it["w"][:, GDN_D:2 * GDN_D], it["e_g"] * it["q"]],
                                        axis=0).astype(BF16)
        qk_ref[c, h] = it["qk"].astype(BF16)
        ket_ref[c, h] = k_end.T.astype(BF16)
        ge_ref[c, h] = jnp.broadcast_to(jnp.exp(g_end), (1, GDN_D))


def _gdn_phase_b(qkvz_ref, gn_ref, og_ref, s_ref, wv_ref, lhs_ref, qk_ref, ket_ref, ge_ref, ncb):
    gn = gn_ref[...]

    def epilogue(c, o):
        rows = slice(c * CHUNK, (c + 1) * CHUNK)
        for h in range(HEADS):
            lo = h * GDN_D
            z = qkvz_ref[rows, COL_Z + lo:COL_Z + lo + GDN_D].astype(F32)
            out = _rms(o[h], gn) * (z * _sigmoid(z))
            og_ref[rows, lo:lo + GDN_D] = out.astype(og_ref.dtype)

    s = [s_ref[h] for h in range(HEADS)]
    o_prev = None
    for c in range(ncb):
        r = [jnp.dot(lhs_ref[c, h], s[h].astype(BF16), preferred_element_type=F32) for h in range(HEADS)]
        yield
        if o_prev is not None:
            epilogue(c - 1, o_prev)
        ub = [(wv_ref[c, h] - r[h][0:CHUNK]).astype(BF16) for h in range(HEADS)]
        s = [ge_ref[c, h] * s[h] + jnp.dot(ket_ref[c, h], ub[h], preferred_element_type=F32)
             for h in range(HEADS)]
        o_prev = [r[h][CHUNK:2 * CHUNK] + jnp.dot(qk_ref[c, h], ub[h], preferred_element_type=F32)
                  for h in range(HEADS)]
        yield
    epilogue(ncb - 1, o_prev)
    for h in range(HEADS):
        s_ref[h] = s[h]


def _mlstm_block(mvo_ref, mqk_ref, gcol_ref, grow_ref, plane_ref, psub_ref, gn_ref, om_ref,
                 c_ref, n_ref, m_ref, ncb):
    ii, jj = _chunk_masks()
    incl = jj <= ii
    lane = lax.broadcasted_iota(jnp.int32, (1, 128), 1)
    row128 = lax.broadcasted_iota(jnp.int32, (128, 1), 0)
    gn = gn_ref[...]
    blane = plane_ref[2:3, :]
    bsub = psub_ref[:, 2:3]
    items = []
    for c in range(ncb):
        rows = slice(c * CHUNK, (c + 1) * CHUNK)
        gc = gcol_ref[rows, :] + blane
        gr = grow_ref[c] + bsub
        lf_c = -_softplus(-gc)
        lf_r = -_softplus(-gr)
        for p in range(HEADS // 2):
            qb = mqk_ref[rows, p * 128:(p + 1) * 128].astype(F32)
            kb = mqk_ref[rows, 256 + p * 128:256 + (p + 1) * 128].astype(F32) * (ML_DK ** -0.5)
            for e in range(2):
                h = 2 * p + e
                lm = (lane >= e * ML_DK) & (lane < (e + 1) * ML_DK)
                items.append(dict(c=c, rows=rows, p=p, e=e, h=h, qh=jnp.where(lm, qb, 0.0), kb=kb,
                                  kh=jnp.where(lm, kb, 0.0),
                                  lf_col=lf_c[:, LANE_MF + h:LANE_MF + h + 1],
                                  lf_row=lf_r[LANE_MF + h:LANE_MF + h + 1, :],
                                  ig_row=gr[LANE_MI + h:LANE_MI + h + 1, :],
                                  ig_col=gc[:, LANE_MI + h:LANE_MI + h + 1],
                                  v=mvo_ref[rows, h * ML_DV:(h + 1) * ML_DV]))
        if c % 2 == 1:
            yield
    for part in _halves(items):
        for it in part:
            it["qk"] = _mm_nt(it["qh"], it["kb"])
        yield
    for part in _halves(items):
        for it in part:
            it["f_col"], it["f_row"] = _cumsum_col_row(it["lf_col"], it["lf_row"], ii, jj)
        yield
    for part in _halves(items):
        for it in part:
            it["d_mat"] = jnp.where(incl, it["f_col"] - it["f_row"] + it["ig_row"], -jnp.inf)
            it["d_max"] = jnp.max(it["d_mat"], axis=1, keepdims=True)
        yield
    for part in _halves(items):
        for it in part:
            p0 = jnp.where(incl, jnp.exp(jnp.where(incl, it["d_mat"] - it["d_max"], 0.0)), 0.0)
            pend0 = jnp.exp(it["f_col"][CHUNK - 1:CHUNK, :] - it["f_col"] + it["ig_col"]
                            - it["d_max"][CHUNK - 1:CHUNK, :])
            it["kp0"] = it["kh"] * pend0
            it["pqk0"] = p0 * it["qk"]
        yield
    for part in _halves(items):
        for it in part:
            it["pv0"] = _mm(it["pqk0"], it["v"])
            it["rs0"] = jnp.sum(it["pqk0"], axis=-1, keepdims=True)
        yield
    for part in _halves(items):
        for it in part:
            it["cadd0"] = _mm(it["kp0"].T, it["v"])
            it["nadd0"] = jnp.sum(it["kp0"], axis=0, keepdims=True)
        yield

    m_cur = [m_ref[:, h:h + 1] for h in range(HEADS)]
    for it in items:
        h = it["h"]
        it["m_prev"] = m_cur[h]
        m_cur[h] = jnp.maximum(it["f_col"][CHUNK - 1:CHUNK, :] + m_cur[h], it["d_max"][CHUNK - 1:CHUNK, :])
    yield
    for part in _halves(items):
        for it in part:
            bcol = it["f_col"] + it["m_prev"]
            mt = jnp.maximum(bcol, it["d_max"])
            it["mt"] = mt
            it["w_prev"] = jnp.exp(bcol - mt)
            it["sc"] = jnp.exp(it["d_max"] - mt)
        yield
    c_cur = [c_ref[p] for p in range(HEADS // 2)]
    n_cur = [n_ref[p] for p in range(HEADS // 2)]
    for c in range(ncb):
        for p in range(HEADS // 2):
            pair = [it for it in items if it["c"] == c and it["p"] == p]
            w_end = [it["w_prev"][CHUNK - 1:CHUNK, :] for it in pair]
            s_end = [it["sc"][CHUNK - 1:CHUNK, :] for it in pair]
            for it in pair:
                it["c_prev"] = c_cur[p]
                it["n_prev"] = n_cur[p]
            c_cur[p] = (jnp.where(row128 < ML_DK, w_end[0], w_end[1]) * c_cur[p]
                        + s_end[0] * pair[0]["cadd0"] + s_end[1] * pair[1]["cadd0"])
            n_cur[p] = (jnp.where(lane < ML_DK, w_end[0], w_end[1]) * n_cur[p]
                        + s_end[0] * pair[0]["nadd0"] + s_end[1] * pair[1]["nadd0"])
        if c % 2 == 1:
            yield
    for part in _halves(items):
        for it in part:
            it["qc"] = _mm(it["qh"], it["c_prev"])
        yield
    for part in _halves(items):
        for it in part:
            num = it["w_prev"] * it["qc"] + it["sc"] * it["pv0"]
            den = (it["w_prev"] * jnp.sum(it["qh"] * it["n_prev"], axis=-1, keepdims=True)
                   + it["sc"] * it["rs0"])
            it["hh"] = num / jnp.maximum(jnp.abs(den), jnp.exp(-it["mt"]))
        yield
    for part in _halves(items):
        for it in part:
            h = it["h"]
            mo = mvo_ref[it["rows"], 512 + h * ML_DV:512 + (h + 1) * ML_DV].astype(F32)
            out = _sigmoid(mo) * _rms(it["hh"], gn)
            om_ref[it["rows"], h * ML_DV:(h + 1) * ML_DV] = out.astype(om_ref.dtype)
        yield
    for p in range(HEADS // 2):
        c_ref[p] = c_cur[p]
        n_ref[p] = n_cur[p]
    for h in range(HEADS):
        m_ref[:, h:h + 1] = m_cur[h]


def _mixer_kernel(qkvz_ref, mvo_ref, mqk_ref, gcol_ref, grow_ref, cs_ref, s0_ref, c0_ref, n0_ref, m0_ref,
                  cw_ref, plane_ref, psub_ref, gng_ref, gnm_ref,
                  og_ref, om_ref, sout_ref, cout_ref, nout_ref, mout_ref,
                  xp_ref, yc_ref, s_ref, wv_ref, lhs_ref, qk_ref, ket_ref, ge_ref, c_ref, n_ref, m_ref,
                  *, tb):
    t = pl.program_id(1)
    nt = pl.num_programs(1)
    ncb = tb // CHUNK

    @pl.when(t == 0)
    def _():
        xp_ref[0:8, :] = cs_ref[...]
        s_ref[...] = s0_ref[...]
        c_ref[...] = c0_ref[...]
        n_ref[...] = n0_ref[...]
        m_ref[...] = m0_ref[...]

    @pl.when(t > 0)
    def _():
        xp_ref[0:8, :] = xp_ref[tb:tb + 8, :]

    xp_ref[8:tb + 8, :] = qkvz_ref[:, COL_QKV:COL_QKV + CONV_CH].astype(F32)

    def conv_tile(ct, carry):
        c0 = pl.multiple_of(ct * 128, 128)
        w = cw_ref[:, pl.ds(c0, 128)]
        for r0 in range(0, tb, 128):
            acc = xp_ref[5 + r0:5 + r0 + 128, pl.ds(c0, 128)] * w[0:1, :]
            for j in range(1, CONV_W):
                acc = acc + xp_ref[5 + j + r0:5 + j + r0 + 128, pl.ds(c0, 128)] * w[j:j + 1, :]
            yc_ref[r0:r0 + 128, pl.ds(c0, 128)] = acc * _sigmoid(acc)
        return carry

    lax.fori_loop(0, CONV_CH // 128, conv_tile, 0)

    _gdn_phase_a(yc_ref, gcol_ref, grow_ref, plane_ref, psub_ref, wv_ref, lhs_ref, qk_ref, ket_ref, ge_ref, ncb)
    _interleave(
        _gdn_phase_b(qkvz_ref, gng_ref, og_ref, s_ref, wv_ref, lhs_ref, qk_ref, ket_ref, ge_ref, ncb),
        _mlstm_block(mvo_ref, mqk_ref, gcol_ref, grow_ref, plane_ref, psub_ref, gnm_ref, om_ref,
                     c_ref, n_ref, m_ref, ncb))

    @pl.when(t == nt - 1)
    def _():
        sout_ref[...] = s_ref[...]
        cout_ref[...] = c_ref[...]
        nout_ref[...] = n_ref[...]
        mout_ref[...] = m_ref[...]


def _mixer_prompt(proj, gates, gates_t, cs8, s0, c0, n0, m0, conv_w, plane, psub, gng, gnm, tb):
    b, t, _ = proj.shape
    nt = t // tb
    ncb = tb // CHUNK
    hp = HEADS // 2
    per_seq4 = lambda i, j: (i, 0, 0, 0)
    const = lambda i, j: (0, 0)
    return pl.pallas_call(
        functools.partial(_mixer_kernel, tb=tb),
        grid=(b, nt),
        in_specs=[pl.BlockSpec((None, tb, 2048), lambda i, j: (i, j, 0)),
                  pl.BlockSpec((None, tb, 1024), lambda i, j: (i, j, COL_MV // 1024)),
                  pl.BlockSpec((None, tb, 512), lambda i, j: (i, j, COL_MQ // 512)),
                  pl.BlockSpec((None, tb, 128), lambda i, j: (i, j, 0)),
                  pl.BlockSpec((None, ncb, 16, CHUNK), lambda i, j: (i, j, 0, 0)),
                  pl.BlockSpec((None, 8, CONV_CH), lambda i, j: (i, 0, 0)),
                  pl.BlockSpec((None, HEADS, GDN_D, GDN_D), per_seq4),
                  pl.BlockSpec((None, hp, 128, 128), per_seq4),
                  pl.BlockSpec((None, hp, 1, 128), per_seq4),
                  pl.BlockSpec((None, 1, 128), lambda i, j: (i, 0, 0)),
                  pl.BlockSpec((CONV_W, CONV_CH), const),
                  pl.BlockSpec((3, 128), const),
                  pl.BlockSpec((16, 3), const),
                  pl.BlockSpec((1, GDN_D), const),
                  pl.BlockSpec((1, ML_DV), const)],
        out_specs=[pl.BlockSpec((None, tb, HEADS * GDN_D), lambda i, j: (i, j, 0)),
                   pl.BlockSpec((None, tb, HEADS * ML_DV), lambda i, j: (i, j, 0)),
                   pl.BlockSpec((None, HEADS, GDN_D, GDN_D), per_seq4),
                   pl.BlockSpec((None, hp, 128, 128), per_seq4),
                   pl.BlockSpec((None, hp, 1, 128), per_seq4),
                   pl.BlockSpec((None, 1, 128), lambda i, j: (i, 0, 0))],
        out_shape=[jax.ShapeDtypeStruct((b, t, HEADS * GDN_D), BF16),
                   jax.ShapeDtypeStruct((b, t, HEADS * ML_DV), BF16),
                   jax.ShapeDtypeStruct((b, HEADS, GDN_D, GDN_D), F32),
                   jax.ShapeDtypeStruct((b, hp, 128, 128), F32),
                   jax.ShapeDtypeStruct((b, hp, 1, 128), F32),
                   jax.ShapeDtypeStruct((b, 1, 128), F32)],
        scratch_shapes=[pltpu.VMEM((tb + 8, CONV_CH), F32),
                        pltpu.VMEM((tb, CONV_CH), F32),
                        pltpu.VMEM((HEADS, GDN_D, GDN_D), F32),
                        pltpu.VMEM((ncb, HEADS, CHUNK, GDN_D), F32),
                        pltpu.VMEM((ncb, HEADS, 2 * CHUNK, GDN_D), BF16),
                        pltpu.VMEM((ncb, HEADS, CHUNK, CHUNK), BF16),
                        pltpu.VMEM((ncb, HEADS, GDN_D, CHUNK), BF16),
                        pltpu.VMEM((ncb, HEADS, 1, GDN_D), F32),
                        pltpu.VMEM((hp, 128, 128), F32),
                        pltpu.VMEM((hp, 1, 128), F32),
                        pltpu.VMEM((1, 128), F32)],
        compiler_params=pltpu.CompilerParams(dimension_semantics=("arbitrary", "arbitrary"),
                                             vmem_limit_bytes=VMEM_LIMIT),
        name="mixer_prompt",
    )(proj, proj, proj, gates, gates_t, cs8, s0, c0, n0, m0, conv_w, plane, psub, gng, gnm)


def _decode_kernel(proj_ref, gates_ref, cs_ref, s0_ref, c0_ref, n0_ref, m0_ref, cw_ref, plane_ref, gng_ref, gnm_ref,
                   og_ref, om_ref, sout_ref, cout_ref, nout_ref, mout_ref, rowg_ref, rowm_ref, *, bb):
    lane = lax.broadcasted_iota(jnp.int32, (1, 128), 1)
    gt = gates_ref[...]
    beta_t = _sigmoid(gt)
    g_t = -jnp.exp(plane_ref[0:1, :]) * _softplus(gt + plane_ref[1:2, :])
    eg_t = jnp.exp(g_t)
    gb_t = gt + plane_ref[2:3, :]
    lf_t = -_softplus(-gb_t)

    heads = []
    for h in range(HEADS):
        cols = []
        for part in range(3):
            c0 = part * 512 + h * GDN_D
            w = cw_ref[:, c0:c0 + GDN_D]
            acc = cs_ref[:, 0, c0:c0 + GDN_D] * w[0:1, :]
            acc = acc + cs_ref[:, 1, c0:c0 + GDN_D] * w[1:2, :]
            acc = acc + cs_ref[:, 2, c0:c0 + GDN_D] * w[2:3, :]
            acc = acc + proj_ref[:, COL_QKV + c0:COL_QKV + c0 + GDN_D] * w[3:4, :]
            cols.append(acc * _sigmoid(acc))
        q, k, v = cols
        q = q * lax.rsqrt(jnp.sum(q * q, axis=-1, keepdims=True) + EPS) * (GDN_D ** -0.5)
        k = k * lax.rsqrt(jnp.sum(k * k, axis=-1, keepdims=True) + EPS)
        heads.append(dict(v=v, qk=jnp.sum(q * k, axis=-1, keepdims=True), q_t=q.T, k_t=k.T,
                          beta=beta_t[:, LANE_GB + h:LANE_GB + h + 1],
                          eg=eg_t[:, LANE_GA + h:LANE_GA + h + 1]))
    items = [dict(h=h, b=b) for h in range(HEADS) for b in range(bb)]
    for it in items:
        hd, b = heads[it["h"]], it["b"]
        it["k_col"] = hd["k_t"][:, b:b + 1]
        it["q_col"] = hd["q_t"][:, b:b + 1]
    for it in items:
        s_bh = s0_ref[it["b"], it["h"]]
        it["ks"] = jnp.sum(it["k_col"] * s_bh, axis=0, keepdims=True)
        it["qs"] = jnp.sum(it["q_col"] * s_bh, axis=0, keepdims=True)
    for it in items:
        hd, b = heads[it["h"]], it["b"]
        eg_b = hd["eg"][b:b + 1, :]
        it["eg_b"] = eg_b
        it["u"] = hd["beta"][b:b + 1, :] * (hd["v"][b:b + 1, :] - eg_b * it["ks"])
        rowg_ref[it["h"], b:b + 1, :] = eg_b * it["qs"] + hd["qk"][b:b + 1, :] * it["u"]
    for it in items:
        sout_ref[it["b"], it["h"]] = it["eg_b"] * s0_ref[it["b"], it["h"]] + it["k_col"] * it["u"]
    for h in range(HEADS):
        z = proj_ref[:, COL_Z + h * GDN_D:COL_Z + (h + 1) * GDN_D]
        out = _rms(rowg_ref[h], gng_ref[...]) * (z * _sigmoid(z))
        og_ref[:, h * GDN_D:(h + 1) * GDN_D] = out.astype(og_ref.dtype)

    heads = []
    for p in range(HEADS // 2):
        qb = proj_ref[:, COL_MQ + p * 128:COL_MQ + (p + 1) * 128]
        kb = proj_ref[:, COL_MK + p * 128:COL_MK + (p + 1) * 128] * (ML_DK ** -0.5)
        n_p = n0_ref[:, p * 128:(p + 1) * 128]
        q_t = qb.T
        k_t = kb.T
        w_prev, p_in = [], []
        for e in range(2):
            h = 2 * p + e
            lm = (lane >= e * ML_DK) & (lane < (e + 1) * ML_DK)
            ig = gb_t[:, LANE_MI + h:LANE_MI + h + 1]
            lf = lf_t[:, LANE_MF + h:LANE_MF + h + 1]
            m_old = m0_ref[:, h:h + 1]
            m_new = jnp.maximum(lf + m_old, ig)
            w_prev.append(jnp.exp(lf + m_old - m_new))
            p_in.append(jnp.exp(ig - m_new))
            qk = jnp.sum(jnp.where(lm, qb * kb, 0.0), axis=-1, keepdims=True)
            qn = jnp.sum(jnp.where(lm, qb * n_p, 0.0), axis=-1, keepdims=True)
            mout_ref[:, h:h + 1] = m_new
            heads.append(dict(p=p, e=e, q_t=q_t, k_t=k_t, w_prev=w_prev[e], p_in=p_in[e],
                              pqk=p_in[e] * qk, wqn=w_prev[e] * qn, floor=jnp.exp(-m_new),
                              v=proj_ref[:, COL_MV + h * ML_DV:COL_MV + (h + 1) * ML_DV]))
        lo_lane = lane < ML_DK
        nout_ref[:, p * 128:(p + 1) * 128] = (jnp.where(lo_lane, w_prev[0], w_prev[1]) * n_p
                                              + jnp.where(lo_lane, p_in[0], p_in[1]) * kb)
    items = [dict(h=h, b=b) for h in range(HEADS) for b in range(bb)]
    for it in items:
        hd, b = heads[it["h"]], it["b"]
        rs = slice(hd["e"] * ML_DK, (hd["e"] + 1) * ML_DK)
        it["rs"] = rs
        it["q_col"] = hd["q_t"][rs, b:b + 1]
        it["k_col"] = hd["k_t"][rs, b:b + 1]
    for it in items:
        hd, b = heads[it["h"]], it["b"]
        c_bh = c0_ref[b, hd["p"], it["rs"], :]
        it["qc"] = jnp.sum(it["q_col"] * c_bh, axis=0, keepdims=True)
    for it in items:
        hd, b = heads[it["h"]], it["b"]
        pqk = hd["pqk"][b:b + 1, :]
        num = hd["w_prev"][b:b + 1, :] * it["qc"] + pqk * hd["v"][b:b + 1, :]
        den = hd["wqn"][b:b + 1, :] + pqk
        rowm_ref[it["h"], b:b + 1, :] = num / jnp.maximum(jnp.abs(den), hd["floor"][b:b + 1, :])
    for it in items:
        hd, b = heads[it["h"]], it["b"]
        cout_ref[b, hd["p"], it["rs"], :] = (hd["w_prev"][b:b + 1, :] * c0_ref[b, hd["p"], it["rs"], :]
                                             + (hd["p_in"][b:b + 1, :] * it["k_col"]) * hd["v"][b:b + 1, :])
    for h in range(HEADS):
        mo = proj_ref[:, COL_MO + h * ML_DV:COL_MO + (h + 1) * ML_DV]
        out = _sigmoid(mo) * _rms(rowm_ref[h], gnm_ref[...])
        om_ref[:, h * ML_DV:(h + 1) * ML_DV] = out.astype(om_ref.dtype)
    mout_ref[:, HEADS:128] = m0_ref[:, HEADS:128]


def _decode(proj, gates, cs, s0, c0, n0, m0, conv_w, plane, gng, gnm, bb):
    b = proj.shape[0]
    hp = HEADS // 2
    return pl.pallas_call(
        functools.partial(_decode_kernel, bb=bb),
        grid=(b // bb,),
        in_specs=[pl.BlockSpec((bb, COL_GATE), lambda i: (i, 0)),
                  pl.BlockSpec((bb, 128), lambda i: (i, 0)),
                  pl.BlockSpec((bb, CONV_W - 1, CONV_CH), lambda i: (i, 0, 0)),
                  pl.BlockSpec((bb, HEADS, GDN_D, GDN_D), lambda i: (i, 0, 0, 0)),
                  pl.BlockSpec((bb, hp, 128, 128), lambda i: (i, 0, 0, 0)),
                  pl.BlockSpec((bb, HEADS * ML_DK), lambda i: (i, 0)),
                  pl.BlockSpec((bb, 128), lambda i: (i, 0)),
                  pl.BlockSpec((CONV_W, CONV_CH), lambda i: (0, 0)),
                  pl.BlockSpec((3, 128), lambda i: (0, 0)),
                  pl.BlockSpec((1, GDN_D), lambda i: (0, 0)),
                  pl.BlockSpec((1, ML_DV), lambda i: (0, 0))],
        out_specs=[pl.BlockSpec((bb, HEADS * GDN_D), lambda i: (i, 0)),
                   pl.BlockSpec((bb, HEADS * ML_DV), lambda i: (i, 0)),
                   pl.BlockSpec((bb, HEADS, GDN_D, GDN_D), lambda i: (i, 0, 0, 0)),
                   pl.BlockSpec((bb, hp, 128, 128), lambda i: (i, 0, 0, 0)),
                   pl.BlockSpec((bb, HEADS * ML_DK), lambda i: (i, 0)),
                   pl.BlockSpec((bb, 128), lambda i: (i, 0))],
        out_shape=[jax.ShapeDtypeStruct((b, HEADS * GDN_D), BF16),
                   jax.ShapeDtypeStruct((b, HEADS * ML_DV), BF16),
                   jax.ShapeDtypeStruct((b, HEADS, GDN_D, GDN_D), F32),
                   jax.ShapeDtypeStruct((b, hp, 128, 128), F32),
                   jax.ShapeDtypeStruct((b, HEADS * ML_DK), F32),
                   jax.ShapeDtypeStruct((b, 128), F32)],
        scratch_shapes=[pltpu.VMEM((HEADS, bb, 128), F32), pltpu.VMEM((HEADS, bb, 128), F32)],
        compiler_params=pltpu.CompilerParams(dimension_semantics=("arbitrary",),
                                             vmem_limit_bytes=VMEM_LIMIT),
        name="decode_step",
    )(proj, gates, cs, s0, c0, n0, m0, conv_w, plane, gng, gnm)


def _regroup_w_in(w_in):
    sizes = (512, 512, 512, 512, 4, 4, 256, 256, 512, 512, 4, 4)
    offs = [0]
    for s in sizes:
        offs.append(offs[-1] + s)
    gq, gk, gv, gz, gb, ga, mq, mk, mv, mo, mi, mf = (w_in[:, offs[i]:offs[i + 1]] for i in range(12))
    pad = jnp.zeros((w_in.shape[0], 128 - 16), w_in.dtype)
    return jnp.concatenate([gq, gk, gv, gz, mv, mo, mq, mk, gb, ga, mi, mf, pad], axis=1).astype(BF16)


def _lane_vec(pairs):
    v = jnp.zeros((128,), F32)
    for off, val in pairs:
        v = v.at[off:off + HEADS].set(val.astype(F32))
    return v


def _prep_params(norm_pre_mix, w_in, conv_w, a_log, dt_bias, gdn_norm_g, b_igate, b_fgate, mlstm_norm_g,
                 w_out, norm_post_mix, norm_pre_mlp, w_up, w_down, norm_post_mlp):
    alog_v = _lane_vec([(LANE_GA, a_log[0])])
    dtb_v = _lane_vec([(LANE_GA, dt_bias[0])])
    bias_v = _lane_vec([(LANE_MI, b_igate[0]), (LANE_MF, b_fgate[0])])
    return dict(
        w_in=_regroup_w_in(w_in[0]), wo=w_out[0].astype(BF16), wu=w_up[0].astype(BF16),
        wd=w_down[0].astype(BF16),
        g_pre=norm_pre_mix[0][None, :], g1=norm_post_mix[0][None, :], g2=norm_pre_mlp[0][None, :],
        g3=norm_post_mlp[0][None, :], cw=conv_w[0], gng=gdn_norm_g[0][None, :], gnm=mlstm_norm_g[0][None, :],
        plane=jnp.stack([alog_v, dtb_v, bias_v]),
        psub=jnp.stack([alog_v[:16], dtb_v[:16], bias_v[:16]], axis=1),
    )


def _prompt_path(x, conv0, s0, c0, n0, m0, prm, tb, tm):
    bsz, seq, d = x.shape
    hp = HEADS // 2
    x2d = x.reshape(bsz * seq, d)
    proj, gates, gates_t = _in_proj(x2d, prm["g_pre"], prm["w_in"], tm=tm, out_dtype=BF16)
    proj = proj.reshape(bsz, seq, COL_GATE)
    gates = gates.reshape(bsz, seq, 128)
    gates_t = gates_t.reshape(bsz, seq // CHUNK, 16, CHUNK)
    cs8 =jnp.pad(conv0, ((0, 0), (8 - (CONV_W - 1), 0), (0, 0)))
    og, om, s_new, c_new, n_new, m_new = _mixer_prompt(
        proj, gates, gates_t, cs8, s0, c0.reshape(bsz, hp, 128, 128), n0.reshape(bsz, hp, 1, 128),
        jnp.pad(m0, ((0, 0), (0, 128 - HEADS)))[:, None, :], prm["cw"], prm["plane"], prm["psub"],
        prm["gng"], prm["gnm"], tb=tb)
    y = _out_mlp(og.reshape(bsz * seq, -1), om.reshape(bsz * seq, -1), x2d, prm["wo"], prm["g1"], prm["g2"],
                 prm["wu"], prm["wd"], prm["g3"], tm=tm).reshape(bsz, seq, d)
    xp = jnp.concatenate([conv0, proj[:, seq - (CONV_W - 1):, COL_QKV:COL_QKV + CONV_CH].astype(F32)], axis=1)
    conv_new = xp[:, -(CONV_W - 1):]
    return y, (conv_new, s_new, c_new.reshape(bsz, HEADS, ML_DK, ML_DV), n_new.reshape(bsz, HEADS, ML_DK),
               m_new[:, 0, :HEADS])


def _sample_path(x, conv0, s0, c0, n0, m0, prm):
    dec, _, d = x.shape
    hp = HEADS // 2
    xs = x.reshape(dec, d)
    proj, gates, _ = _in_proj(xs, prm["g_pre"], prm["w_in"], tm=dec, out_dtype=F32)
    og, om, s_new, c_new, n_new, m_new = _decode(
        proj, gates, conv0, s0, c0.reshape(dec, hp, 128, 128), n0.reshape(dec, HEADS * ML_DK),
        jnp.pad(m0, ((0, 0), (0, 128 - HEADS))), prm["cw"], prm["plane"], prm["gng"], prm["gnm"], bb=8)
    y = _out_mlp(og, om, xs, prm["wo"], prm["g1"], prm["g2"], prm["wu"], prm["wd"], prm["g3"],
                 tm=dec).reshape(dec, 1, d)
    conv_new = jnp.concatenate([conv0[:, 1:, :], proj[:, None, COL_QKV:COL_QKV + CONV_CH]], axis=1)
    return y, (conv_new, s_new, c_new.reshape(dec, HEADS, ML_DK, ML_DV), n_new.reshape(dec, HEADS, ML_DK),
               m_new[:, :HEADS])


def kernel(x_prompt, x_sample, state_gdn_conv, state_gdn_S, state_mlstm_C, state_mlstm_n, state_mlstm_m,
           norm_pre_mix, w_in, conv_w, a_log, dt_bias, gdn_norm_g, b_igate, b_fgate, mlstm_norm_g, w_out,
           norm_post_mix, norm_pre_mlp, w_up, w_down, norm_post_mlp):
    bsz = x_prompt.shape[0]
    prm = _prep_params(norm_pre_mix, w_in, conv_w, a_log, dt_bias, gdn_norm_g, b_igate, b_fgate, mlstm_norm_g,
                       w_out, norm_post_mix, norm_pre_mlp, w_up, w_down, norm_post_mlp)
    y_p, p_st = _prompt_path(
        x_prompt, jnp.zeros((bsz, CONV_W - 1, CONV_CH), F32), jnp.zeros((bsz, HEADS, GDN_D, GDN_D), F32),
        jnp.zeros((bsz, HEADS, ML_DK, ML_DV), F32), jnp.zeros((bsz, HEADS, ML_DK), F32),
        jnp.zeros((bsz, HEADS), F32), prm, tb=512, tm=512)
    y_s, s_st = _sample_path(x_sample, state_gdn_conv[0], state_gdn_S[0], state_mlstm_C[0], state_mlstm_n[0],
                             state_mlstm_m[0], prm)
    return (y_p, y_s) + tuple(a[None] for a in p_st) + tuple(a[None] for a in s_st)
```

```python
import functools

import jax
import jax.numpy as jnp
from jax import lax
from jax.experimental import pallas as pl
from jax.experimental.pallas import tpu as pltpu

F32 = jnp.float32
BF16 = jnp.bfloat16
EPS = 1e-6

D_MODEL = 1024
HEADS = 4
GDN_D = 128
ML_DK = 64
ML_DV = 128
CONV_W = 4
CONV_CH = 3 * HEADS * GDN_D
D_FF = 4 * D_MODEL
CHUNK = 64

COL_QKV = 0
COL_Z = 1536
COL_MV = 2048
COL_MO = 2560
COL_MQ = 3072
COL_MK = 3328
COL_GATE = 3584
N_PROJ = COL_GATE + 128
LANE_GB, LANE_GA, LANE_MI, LANE_MF = 0, 4, 8, 12

VMEM_LIMIT = 56 * 1024 * 1024


def _rms(x, g):
    return x * lax.rsqrt(jnp.mean(x * x, axis=-1, keepdims=True) + EPS) * g


def _softplus(x):
    return jnp.maximum(x, 0.0) + jnp.log1p(jnp.exp(-jnp.abs(x)))


def _sigmoid(x):
    return 1.0 / (1.0 + jnp.exp(-x))


def _mm(a, b):
    return jnp.dot(a.astype(BF16), b.astype(BF16), preferred_element_type=F32)


def _mm_nt(a, b):
    return lax.dot_general(a.astype(BF16), b.astype(BF16), (((1,), (1,)), ((), ())),
                           preferred_element_type=F32)


def _in_proj_kernel(x_ref, g_ref, w_ref, o_ref, gc_ref, gt_ref, h_ref, *, n_chunk):
    h_ref[...] = _rms(x_ref[...], g_ref[...]).astype(BF16)
    n = o_ref.shape[1]
    for c0 in range(0, n, n_chunk):
        c1 = min(c0 + n_chunk, n)
        o_ref[:, c0:c1] = jnp.dot(h_ref[...], w_ref[:, c0:c1], preferred_element_type=F32).astype(o_ref.dtype)
    gates = jnp.dot(h_ref[...], w_ref[:, COL_GATE:COL_GATE + 128], preferred_element_type=F32)
    gc_ref[...] = gates
    gt = gates.T
    for c in range(gt_ref.shape[0]):
        gt_ref[c] = gt[0:16, c * CHUNK:(c + 1) * CHUNK]


def _in_proj(x2d, g, w, tm, out_dtype):
    m, k = x2d.shape
    n = w.shape[1]
    return pl.pallas_call(
        functools.partial(_in_proj_kernel, n_chunk=512),
        grid=(m // tm,),
        in_specs=[pl.BlockSpec((tm, k), lambda i: (i, 0)),
                  pl.BlockSpec((1, k), lambda i: (0, 0)),
                  pl.BlockSpec((k, n), lambda i: (0, 0))],
        out_specs=[pl.BlockSpec((tm, COL_GATE), lambda i: (i, 0)),
                   pl.BlockSpec((tm, 128), lambda i: (i, 0)),
                   pl.BlockSpec((tm // CHUNK, 16, CHUNK), lambda i: (i, 0, 0))],
        out_shape=[jax.ShapeDtypeStruct((m, COL_GATE), out_dtype),
                   jax.ShapeDtypeStruct((m, 128), F32),
                   jax.ShapeDtypeStruct((m // CHUNK, 16, CHUNK), F32)],
        scratch_shapes=[pltpu.VMEM((tm, k), BF16)],
        compiler_params=pltpu.CompilerParams(dimension_semantics=("arbitrary",),
                                             vmem_limit_bytes=VMEM_LIMIT),
        name="in_proj",
    )(x2d, g, w)


def _out_mlp_kernel(og_ref, om_ref, x_ref, wo_ref, g1_ref, g2_ref, wu_ref, wd_ref, g3_ref,
                    y_ref, hn_ref, acc_ref, *, ff_chunk):
    half = og_ref.shape[1]
    mix = (jnp.dot(og_ref[...], wo_ref[0:half, :], preferred_element_type=F32)
           + jnp.dot(om_ref[...], wo_ref[half:2 * half, :], preferred_element_type=F32))
    x1 = x_ref[...] + _rms(mix, g1_ref[...])
    y_ref[...] = x1
    hn_ref[...] = _rms(x1, g2_ref[...]).astype(BF16)
    d_ff = wu_ref.shape[1]
    for c0 in range(0, d_ff, ff_chunk):
        u = jnp.dot(hn_ref[...], wu_ref[:, c0:c0 + ff_chunk], preferred_element_type=F32)
        u = jnp.square(jnp.maximum(u, 0.0)).astype(BF16)
        d = jnp.dot(u, wd_ref[c0:c0 + ff_chunk, :], preferred_element_type=F32)
        if c0 == 0:
            acc_ref[...] = d
        else:
            acc_ref[...] += d
    y_ref[...] = y_ref[...] + _rms(acc_ref[...], g3_ref[...])


def _out_mlp(og, om, x2d, wo, g1, g2, wu, wd, g3, tm):
    m, d = x2d.shape
    half = og.shape[1]
    d_ff = wu.shape[1]
    const = lambda i: (0, 0)
    return pl.pallas_call(
        functools.partial(_out_mlp_kernel, ff_chunk=1024),
        grid=(m // tm,),
        in_specs=[pl.BlockSpec((tm, half), lambda i: (i, 0)),
                  pl.BlockSpec((tm, half), lambda i: (i, 0)),
                  pl.BlockSpec((tm, d), lambda i: (i, 0)),
                  pl.BlockSpec((d, d), const, pipeline_mode=pl.Buffered(1)),
                  pl.BlockSpec((1, d), const),
                  pl.BlockSpec((1, d), const),
                  pl.BlockSpec((d, d_ff), const, pipeline_mode=pl.Buffered(1)),
                  pl.BlockSpec((d_ff, d), const, pipeline_mode=pl.Buffered(1)),
                  pl.BlockSpec((1, d), const)],
        out_specs=pl.BlockSpec((tm, d), lambda i: (i, 0)),
        out_shape=jax.ShapeDtypeStruct((m, d), F32),
        scratch_shapes=[pltpu.VMEM((tm, d), BF16), pltpu.VMEM((tm, d), F32)],
        compiler_params=pltpu.CompilerParams(dimension_semantics=("arbitrary",),
                                             vmem_limit_bytes=VMEM_LIMIT),
        name="out_mlp",
    )(og, om, x2d, wo, g1, g2, wu, wd, g3)


def _chunk_masks():
    ii = lax.broadcasted_iota(jnp.int32, (CHUNK, CHUNK), 0)
    jj = lax.broadcasted_iota(jnp.int32, (CHUNK, CHUNK), 1)
    return ii, jj


def _cumsum_col_row(x_col, x_row, ii, jj):
    c_col = jnp.sum(jnp.where(jj <= ii, x_row, 0.0), axis=1, keepdims=True)
    c_row = jnp.sum(jnp.where(ii <= jj, x_col, 0.0), axis=0, keepdims=True)
    return c_col, c_row


def _interleave(*gens):
    gens = list(gens)
    while gens:
        for g in list(gens):
            try:
                next(g)
            except StopIteration:
                gens.remove(g)


def _halves(items):
    mid = len(items) // 2
    return items[:mid], items[mid:]


def _gdn_phase_a(yc_ref, gcol_ref, grow_ref, plane_ref, psub_ref, wv_ref, lhs_ref, qk_ref, ket_ref, ge_ref,
                 ncb):
    ii, jj = _chunk_masks()
    incl = jj <= ii
    strict = jj < ii
    eye = (ii == jj).astype(F32)
    neg_a_lane = -jnp.exp(plane_ref[0:1, :])
    dtb_lane = plane_ref[1:2, :]
    neg_a_sub = -jnp.exp(psub_ref[:, 0:1])
    dtb_sub = psub_ref[:, 1:2]
    items = []
    for c in range(ncb):
        rows = slice(c * CHUNK, (c + 1) * CHUNK)
        gc = gcol_ref[rows, :]
        gr = grow_ref[c]
        beta_t = _sigmoid(gc)
        g_t = neg_a_lane * _softplus(gc + dtb_lane)
        g_r = neg_a_sub * _softplus(gr + dtb_sub)
        for h in range(HEADS):
            lo = h * GDN_D
            items.append(dict(c=c, h=h,
                              q=yc_ref[rows, lo:lo + GDN_D],
                              k=yc_ref[rows, 512 + lo:512 + lo + GDN_D],
                              v=yc_ref[rows, 1024 + lo:1024 + lo + GDN_D],
                              beta=beta_t[:, LANE_GB + h:LANE_GB + h + 1],
                              gg_col=g_t[:, LANE_GA + h:LANE_GA + h + 1],
                              gg_row=g_r[LANE_GA + h:LANE_GA + h + 1, :]))
    yield
    for it in items:
        it["k"] = it["k"] * lax.rsqrt(jnp.sum(it["k"] * it["k"], axis=-1, keepdims=True) + EPS)
    yield
    for it in items:
        it["kk"] = _mm_nt(it["k"], it["k"])
    yield
    for it in items:
        it["q"] = (it["q"] * lax.rsqrt(jnp.sum(it["q"] * it["q"], axis=-1, keepdims=True) + EPS)
                   * (GDN_D ** -0.5))
    yield
    for it in items:
        it["g_col"], g_row = _cumsum_col_row(it["gg_col"], it["gg_row"], ii, jj)
        it["decay"] = jnp.where(incl, jnp.exp(jnp.where(incl, it["g_col"] - g_row, 0.0)), 0.0)
    yield
    for it in items:
        n_mat = jnp.where(strict, it["beta"] * it["kk"] * it["decay"], 0.0)
        it["x"] = eye - n_mat
        it["p"] = -n_mat
    yield
    for _ in range(5):
        yield
        for it in items:
            it["p"] = _mm(it["p"], it["p"])
        yield
        for it in items:
            it["x"] = it["x"] + _mm(it["x"], it["p"])
    yield
    for it in items:
        e_g = jnp.exp(it["g_col"])
        it["e_g"] = e_g
        rhs = jnp.concatenate([it["beta"] * it["v"], (it["beta"] * e_g) * it["k"]], axis=1)
        it["w"] = _mm(it["x"], rhs)
    yield
    for it in items:
        it["qk"] = _mm_nt(it["q"], it["k"]) * it["decay"]
    yield
    for it in items:
        c, h = it["c"], it["h"]
        g_end = it["g_col"][CHUNK - 1:CHUNK, :]
        k_end = it["k"] * jnp.exp(g_end - it["g_col"])
        wv_ref[c, h] = it["w"][:, 0:GDN_D]
        lhs_ref[c, h] = jnp.concatenate([it["w"][:, GDN_D:2 * GDN_D], it["e_g"] * it["q"]],
                                        axis=0).astype(BF16)
        qk_ref[c, h] = it["qk"].astype(BF16)
        ket_ref[c, h] = k_end.T.astype(BF16)
        ge_ref[c, h] = jnp.broadcast_to(jnp.exp(g_end), (1, GDN_D))


def _gdn_phase_b(qkvz_ref, gn_ref, og_ref, s_ref, wv_ref, lhs_ref, qk_ref, ket_ref, ge_ref, ncb):
    gn = gn_ref[...]

    def epilogue(c, o):
        rows = slice(c * CHUNK, (c + 1) * CHUNK)
        for h in range(HEADS):
            lo = h * GDN_D
            z = qkvz_ref[rows, COL_Z + lo:COL_Z + lo + GDN_D].astype(F32)
            out = _rms(o[h], gn) * (z * _sigmoid(z))
            og_ref[rows, lo:lo + GDN_D] = out.astype(og_ref.dtype)

    s = [s_ref[h] for h in range(HEADS)]
    o_prev = None
    for c in range(ncb):
        r = [jnp.dot(lhs_ref[c, h], s[h].astype(BF16), preferred_element_type=F32) for h in range(HEADS)]
        yield
        if o_prev is not None:
            epilogue(c - 1, o_prev)
        ub = [(wv_ref[c, h] - r[h][0:CHUNK]).astype(BF16) for h in range(HEADS)]
        s = [ge_ref[c, h] * s[h] + jnp.dot(ket_ref[c, h], ub[h], preferred_element_type=F32)
             for h in range(HEADS)]
        o_prev = [r[h][CHUNK:2 * CHUNK] + jnp.dot(qk_ref[c, h], ub[h], preferred_element_type=F32)
                  for h in range(HEADS)]
        yield
    epilogue(ncb - 1, o_prev)
    for h in range(HEADS):
        s_ref[h] = s[h]


def _mlstm_block(mvo_ref, mqk_ref, gcol_ref, grow_ref, plane_ref, psub_ref, gn_ref, om_ref,
                 c_ref, n_ref, m_ref, ncb):
    ii, jj = _chunk_masks()
    incl = jj <= ii
    lane = lax.broadcasted_iota(jnp.int32, (1, 128), 1)
    row128 = lax.broadcasted_iota(jnp.int32, (128, 1), 0)
    gn = gn_ref[...]
    blane = plane_ref[2:3, :]
    bsub = psub_ref[:, 2:3]
    items = []
    for c in range(ncb):
        rows = slice(c * CHUNK, (c + 1) * CHUNK)
        gc = gcol_ref[rows, :] + blane
        gr = grow_ref[c] + bsub
        lf_c = -_softplus(-gc)
        lf_r = -_softplus(-gr)
        for p in range(HEADS // 2):
            qb = mqk_ref[rows, p * 128:(p + 1) * 128].astype(F32)
            kb = mqk_ref[rows, 256 + p * 128:256 + (p + 1) * 128].astype(F32) * (ML_DK ** -0.5)
            for e in range(2):
                h = 2 * p + e
                lm = (lane >= e * ML_DK) & (lane < (e + 1) * ML_DK)
                items.append(dict(c=c, rows=rows, p=p, e=e, h=h, qh=jnp.where(lm, qb, 0.0), kb=kb,
                                  kh=jnp.where(lm, kb, 0.0),
                                  lf_col=lf_c[:, LANE_MF + h:LANE_MF + h + 1],
                                  lf_row=lf_r[LANE_MF + h:LANE_MF + h + 1, :],
                                  ig_row=gr[LANE_MI + h:LANE_MI + h + 1, :],
                                  ig_col=gc[:, LANE_MI + h:LANE_MI + h + 1],
                                  v=mvo_ref[rows, h * ML_DV:(h + 1) * ML_DV]))
        if c % 2 == 1:
            yield
    for part in _halves(items):
        for it in part:
            it["qk"] = _mm_nt(it["qh"], it["kb"])
        yield
    for part in _halves(items):
        for it in part:
            it["f_col"], it["f_row"] = _cumsum_col_row(it["lf_col"], it["lf_row"], ii, jj)
        yield
    for part in _halves(items):
        for it in part:
            it["d_mat"] = jnp.where(incl, it["f_col"] - it["f_row"] + it["ig_row"], -jnp.inf)
            it["d_max"] = jnp.max(it["d_mat"], axis=1, keepdims=True)
        yield
    for part in _halves(items):
        for it in part:
            p0 = jnp.where(incl, jnp.exp(jnp.where(incl, it["d_mat"] - it["d_max"], 0.0)), 0.0)
            pend0 = jnp.exp(it["f_col"][CHUNK - 1:CHUNK, :] - it["f_col"] + it["ig_col"]
                            - it["d_max"][CHUNK - 1:CHUNK, :])
            it["kp0"] = it["kh"] * pend0
            it["pqk0"] = p0 * it["qk"]
        yield
    for part in _halves(items):
        for it in part:
            it["pv0"] = _mm(it["pqk0"], it["v"])
            it["rs0"] = jnp.sum(it["pqk0"], axis=-1, keepdims=True)
        yield
    for part in _halves(items):
        for it in part:
            it["cadd0"] = _mm(it["kp0"].T, it["v"])
            it["nadd0"] = jnp.sum(it["kp0"], axis=0, keepdims=True)
        yield

    m_cur = [m_ref[:, h:h + 1] for h in range(HEADS)]
    for it in items:
        h = it["h"]
        it["m_prev"] = m_cur[h]
        m_cur[h] = jnp.maximum(it["f_col"][CHUNK - 1:CHUNK, :] + m_cur[h], it["d_max"][CHUNK - 1:CHUNK, :])
    yield
    for part in _halves(items):
        for it in part:
            bcol = it["f_col"] + it["m_prev"]
            mt = jnp.maximum(bcol, it["d_max"])
            it["mt"] = mt
            it["w_prev"] = jnp.exp(bcol - mt)
            it["sc"] = jnp.exp(it["d_max"] - mt)
        yield
    c_cur = [c_ref[p] for p in range(HEADS // 2)]
    n_cur = [n_ref[p] for p in range(HEADS // 2)]
    for c in range(ncb):
        for p in range(HEADS // 2):
            pair = [it for it in items if it["c"] == c and it["p"] == p]
            w_end = [it["w_prev"][CHUNK - 1:CHUNK, :] for it in pair]
            s_end = [it["sc"][CHUNK - 1:CHUNK, :] for it in pair]
            for it in pair:
                it["c_prev"] = c_cur[p]
                it["n_prev"] = n_cur[p]
            c_cur[p] = (jnp.where(row128 < ML_DK, w_end[0], w_end[1]) * c_cur[p]
                        + s_end[0] * pair[0]["cadd0"] + s_end[1] * pair[1]["cadd0"])
            n_cur[p] = (jnp.where(lane < ML_DK, w_end[0], w_end[1]) * n_cur[p]
                        + s_end[0] * pair[0]["nadd0"] + s_end[1] * pair[1]["nadd0"])
        if c % 2 == 1:
            yield
    for part in _halves(items):
        for it in part:
            it["qc"] = _mm(it["qh"], it["c_prev"])
        yield
    for part in _halves(items):
        for it in part:
            num = it["w_prev"] * it["qc"] + it["sc"] * it["pv0"]
            den = (it["w_prev"] * jnp.sum(it["qh"] * it["n_prev"], axis=-1, keepdims=True)
                   + it["sc"] * it["rs0"])
            it["hh"] = num / jnp.maximum(jnp.abs(den), jnp.exp(-it["mt"]))
        yield
    for part in _halves(items):
        for it in part:
            h = it["h"]
            mo = mvo_ref[it["rows"], 512 + h * ML_DV:512 + (h + 1) * ML_DV].astype(F32)
            out = _sigmoid(mo) * _rms(it["hh"], gn)
            om_ref[it["rows"], h * ML_DV:(h + 1) * ML_DV] = out.astype(om_ref.dtype)
        yield
    for p in range(HEADS // 2):
        c_ref[p] = c_cur[p]
        n_ref[p] = n_cur[p]
    for h in range(HEADS):
        m_ref[:, h:h + 1] = m_cur[h]


def _chain(*gens):
    for g in gens:
        yield from g


def _conv_stage(xp_ref, yc_ref, cw_ref, tb):
    for ct in range(CONV_CH // 128):
        cols = slice(ct * 128, (ct + 1) * 128)
        w = cw_ref[:, cols]
        for r0 in range(0, tb, 128):
            acc = xp_ref[5 + r0:5 + r0 + 128, cols] * w[0:1, :]
            for j in range(1, CONV_W):
                acc = acc + xp_ref[5 + j + r0:5 + j + r0 + 128, cols] * w[j:j + 1, :]
            yc_ref[r0:r0 + 128, cols] = acc * _sigmoid(acc)
        yield


def _out_mlp_block(og_ref, om_ref, x_ref, wo_ref, g1_ref, g2_ref, wu_ref, wd_ref, g3_ref,
                   y_ref, hn_ref, acc_ref, ff_chunk, n_split):
    half = og_ref.shape[1]
    d = x_ref.shape[1]
    d_ff = wu_ref.shape[1]
    col_groups = [slice(c, c + d // n_split) for c in range(0, d, d // n_split)]
    for cg in col_groups:
        acc_ref[:, cg] = (jnp.dot(og_ref[...], wo_ref[0:half, cg], preferred_element_type=F32)
                          + jnp.dot(om_ref[...], wo_ref[half:2 * half, cg], preferred_element_type=F32))
        yield
    x1 = x_ref[...] + _rms(acc_ref[...], g1_ref[...])
    y_ref[...] = x1
    hn_ref[...] = _rms(x1, g2_ref[...]).astype(BF16)
    yield
    for c0 in range(0, d_ff, ff_chunk):
        u = jnp.dot(hn_ref[...], wu_ref[:, c0:c0 + ff_chunk], preferred_element_type=F32)
        u = jnp.square(jnp.maximum(u, 0.0)).astype(BF16)
        for cg in col_groups:
            dd = jnp.dot(u, wd_ref[c0:c0 + ff_chunk, cg], preferred_element_type=F32)
            if c0 == 0:
                acc_ref[:, cg] = dd
            else:
                acc_ref[:, cg] += dd
        yield
    y_ref[...] = y_ref[...] + _rms(acc_ref[...], g3_ref[...])
    yield


def _layer_kernel(qkvz_ref, mvo_ref, mqk_ref, gcol_ref, grow_ref, cs_ref, s0_ref, c0_ref, n0_ref, m0_ref,
                  x_ref, cw_ref, plane_ref, psub_ref, gng_ref, gnm_ref,
                  wo_ref, g1_ref, g2_ref, wu_ref, wd_ref, g3_ref,
                  y_ref, sout_ref, cout_ref, nout_ref, mout_ref,
                  xp_ref, yc_ref, s_ref, wv_ref, lhs_ref, qk_ref, ket_ref, ge_ref, c_ref, n_ref, m_ref,
                  og_ref, om_ref, hn_ref, acc_ref, *, tb, nt, n_blocks):
    g = pl.program_id(0)
    t = jnp.minimum(g, n_blocks - 1) % nt
    ncb = tb // CHUNK
    par = g % 2

    @pl.when(g == 0)
    def _():
        og_ref[1] = jnp.zeros(og_ref.shape[1:], og_ref.dtype)
        om_ref[1] = jnp.zeros(om_ref.shape[1:], om_ref.dtype)

    @pl.when(t == 0)
    def _():
        xp_ref[0:8, :] = cs_ref[...]
        s_ref[...] = s0_ref[...]
        c_ref[...] = c0_ref[...]
        n_ref[...] = n0_ref[...]
        m_ref[...] = m0_ref[...]

    @pl.when(t > 0)
    def _():
        xp_ref[0:8, :] = xp_ref[tb:tb + 8, :]

    xp_ref[8:tb + 8, :] = qkvz_ref[:, COL_QKV:COL_QKV + CONV_CH].astype(F32)

    og_w, om_w = og_ref.at[par], om_ref.at[par]
    og_r, om_r = og_ref.at[1 - par], om_ref.at[1 - par]
    _interleave(
        _chain(_conv_stage(xp_ref, yc_ref, cw_ref, tb),
               _gdn_phase_a(yc_ref, gcol_ref, grow_ref, plane_ref, psub_ref, wv_ref, lhs_ref, qk_ref, ket_ref,
                            ge_ref, ncb),
               _gdn_phase_b(qkvz_ref, gng_ref, og_w, s_ref, wv_ref, lhs_ref, qk_ref, ket_ref, ge_ref, ncb)),
        _out_mlp_block(og_r, om_r, x_ref, wo_ref, g1_ref, g2_ref, wu_ref, wd_ref, g3_ref, y_ref, hn_ref, acc_ref,
                       ff_chunk=256, n_split=2),
        _mlstm_block(mvo_ref, mqk_ref, gcol_ref, grow_ref, plane_ref, psub_ref, gnm_ref, om_w,
                     c_ref, n_ref, m_ref, ncb))

    @pl.when((t == nt - 1) & (g < n_blocks))
    def _():
        sout_ref[...] = s_ref[...]
        cout_ref[...] = c_ref[...]
        nout_ref[...] = n_ref[...]
        mout_ref[...] = m_ref[...]


def _layer_prompt(proj, gates, gates_t, cs8, s0, c0, n0, m0, x2d, conv_w, plane, psub, gng, gnm,
                  wo, g1, g2, wu, wd, g3, tb):
    b, t, _ = proj.shape
    d = x2d.shape[1]
    d_ff = wu.shape[1]
    nt = t // tb
    n_blocks = b * nt
    ncb = tb // CHUNK
    hp = HEADS // 2

    def blk(g):
        gm = jnp.minimum(g, n_blocks - 1)
        return gm // nt, gm % nt

    def tok3(col):
        return lambda g: blk(g) + (col,)

    per_seq3 = lambda g: (blk(g)[0], 0, 0)
    per_seq4 = lambda g: (blk(g)[0], 0, 0, 0)
    prev_rows = lambda g: (jnp.maximum(g - 1, 0), 0)
    const = lambda g: (0, 0)
    resident = dict(pipeline_mode=pl.Buffered(1))
    return pl.pallas_call(
        functools.partial(_layer_kernel, tb=tb, nt=nt, n_blocks=n_blocks),
        grid=(n_blocks + 1,),
        in_specs=[pl.BlockSpec((None, tb, 2048), tok3(0)),
                  pl.BlockSpec((None, tb, 1024), tok3(COL_MV // 1024)),
                  pl.BlockSpec((None, tb, 512), tok3(COL_MQ // 512)),
                  pl.BlockSpec((None, tb, 128), tok3(0)),
                  pl.BlockSpec((None, ncb, 16, CHUNK), lambda g: blk(g) + (0, 0)),
                  pl.BlockSpec((None, 8, CONV_CH), per_seq3),
                  pl.BlockSpec((None, HEADS, GDN_D, GDN_D), per_seq4),
                  pl.BlockSpec((None, hp, 128, 128), per_seq4),
                  pl.BlockSpec((None, hp, 1, 128), per_seq4),
                  pl.BlockSpec((None, 1, 128), per_seq3),
                  pl.BlockSpec((tb, d), prev_rows),
                  pl.BlockSpec((CONV_W, CONV_CH), const),
                  pl.BlockSpec((3, 128), const),
                  pl.BlockSpec((16, 3), const),
                  pl.BlockSpec((1, GDN_D), const),
                  pl.BlockSpec((1, ML_DV), const),
                  pl.BlockSpec((d, d), const, **resident),
                  pl.BlockSpec((1, d), const),
                  pl.BlockSpec((1, d), const),
                  pl.BlockSpec((d, d_ff), const, **resident),
                  pl.BlockSpec((d_ff, d), const, **resident),
                  pl.BlockSpec((1, d), const)],
        out_specs=[pl.BlockSpec((tb, d), prev_rows),
                   pl.BlockSpec((None, HEADS, GDN_D, GDN_D), per_seq4),
                   pl.BlockSpec((None, hp, 128, 128), per_seq4),
                   pl.BlockSpec((None, hp, 1, 128), per_seq4),
                   pl.BlockSpec((None, 1, 128), per_seq3)],
        out_shape=[jax.ShapeDtypeStruct((b * t, d), F32),
                   jax.ShapeDtypeStruct((b, HEADS, GDN_D, GDN_D), F32),
                   jax.ShapeDtypeStruct((b, hp, 128, 128), F32),
                   jax.ShapeDtypeStruct((b, hp, 1, 128), F32),
                   jax.ShapeDtypeStruct((b, 1, 128), F32)],
        scratch_shapes=[pltpu.VMEM((tb + 8, CONV_CH), F32),
                        pltpu.VMEM((tb, CONV_CH), F32),
                        pltpu.VMEM((HEADS, GDN_D, GDN_D), F32),
                        pltpu.VMEM((ncb, HEADS, CHUNK, GDN_D), F32),
                        pltpu.VMEM((ncb, HEADS, 2 * CHUNK, GDN_D), BF16),
                        pltpu.VMEM((ncb, HEADS, CHUNK, CHUNK), BF16),
                        pltpu.VMEM((ncb, HEADS, GDN_D, CHUNK), BF16),
                        pltpu.VMEM((ncb, HEADS, 1, GDN_D), F32),
                        pltpu.VMEM((hp, 128, 128), F32),
                        pltpu.VMEM((hp, 1, 128), F32),
                        pltpu.VMEM((1, 128), F32),
                        pltpu.VMEM((2, tb, HEADS * GDN_D), BF16),
                        pltpu.VMEM((2, tb, HEADS * ML_DV), BF16),
                        pltpu.VMEM((tb, d), BF16),
                        pltpu.VMEM((tb, d), F32)],
        compiler_params=pltpu.CompilerParams(dimension_semantics=("arbitrary",),
                                             vmem_limit_bytes=VMEM_LIMIT),
        name="layer_prompt",
    )(proj, proj, proj, gates, gates_t, cs8, s0, c0, n0, m0, x2d, conv_w, plane, psub, gng, gnm,
      wo, g1, g2, wu, wd, g3)


def _decode_kernel(proj_ref, gates_ref, cs_ref, s0_ref, c0_ref, n0_ref, m0_ref, cw_ref, plane_ref, gng_ref, gnm_ref,
                   og_ref, om_ref, sout_ref, cout_ref, nout_ref, mout_ref, rowg_ref, rowm_ref, *, bb):
    lane = lax.broadcasted_iota(jnp.int32, (1, 128), 1)
    gt = gates_ref[...]
    beta_t = _sigmoid(gt)
    g_t = -jnp.exp(plane_ref[0:1, :]) * _softplus(gt + plane_ref[1:2, :])
    eg_t = jnp.exp(g_t)
    gb_t = gt + plane_ref[2:3, :]
    lf_t = -_softplus(-gb_t)

    heads = []
    for h in range(HEADS):
        cols = []
        for part in range(3):
            c0 = part * 512 + h * GDN_D
            w = cw_ref[:, c0:c0 + GDN_D]
            acc = cs_ref[:, 0, c0:c0 + GDN_D] * w[0:1, :]
            acc = acc + cs_ref[:, 1, c0:c0 + GDN_D] * w[1:2, :]
            acc = acc + cs_ref[:, 2, c0:c0 + GDN_D] * w[2:3, :]
            acc = acc + proj_ref[:, COL_QKV + c0:COL_QKV + c0 + GDN_D] * w[3:4, :]
            cols.append(acc * _sigmoid(acc))
        q, k, v = cols
        q = q * lax.rsqrt(jnp.sum(q * q, axis=-1, keepdims=True) + EPS) * (GDN_D ** -0.5)
        k = k * lax.rsqrt(jnp.sum(k * k, axis=-1, keepdims=True) + EPS)
        heads.append(dict(v=v, qk=jnp.sum(q * k, axis=-1, keepdims=True), q_t=q.T, k_t=k.T,
                          beta=beta_t[:, LANE_GB + h:LANE_GB + h + 1],
                          eg=eg_t[:, LANE_GA + h:LANE_GA + h + 1]))
    items = [dict(h=h, b=b) for h in range(HEADS) for b in range(bb)]
    for it in items:
        hd, b = heads[it["h"]], it["b"]
        it["k_col"] = hd["k_t"][:, b:b + 1]
        it["q_col"] = hd["q_t"][:, b:b + 1]
    for it in items:
        s_bh = s0_ref[it["b"], it["h"]]
        it["ks"] = jnp.sum(it["k_col"] * s_bh, axis=0, keepdims=True)
        it["qs"] = jnp.sum(it["q_col"] * s_bh, axis=0, keepdims=True)
    for it in items:
        hd, b = heads[it["h"]], it["b"]
        eg_b = hd["eg"][b:b + 1, :]
        it["eg_b"] = eg_b
        it["u"] = hd["beta"][b:b + 1, :] * (hd["v"][b:b + 1, :] - eg_b * it["ks"])
        rowg_ref[it["h"], b:b + 1, :] = eg_b * it["qs"] + hd["qk"][b:b + 1, :] * it["u"]
    for it in items:
        sout_ref[it["b"], it["h"]] = it["eg_b"] * s0_ref[it["b"], it["h"]] + it["k_col"] * it["u"]
    for h in range(HEADS):
        z = proj_ref[:, COL_Z + h * GDN_D:COL_Z + (h + 1) * GDN_D]
        out = _rms(rowg_ref[h], gng_ref[...]) * (z * _sigmoid(z))
        og_ref[:, h * GDN_D:(h + 1) * GDN_D] = out.astype(og_ref.dtype)

    heads = []
    for p in range(HEADS // 2):
        qb = proj_ref[:, COL_MQ + p * 128:COL_MQ + (p + 1) * 128]
        kb = proj_ref[:, COL_MK + p * 128:COL_MK + (p + 1) * 128] * (ML_DK ** -0.5)
        n_p = n0_ref[:, p * 128:(p + 1) * 128]
        q_t = qb.T
        k_t = kb.T
        w_prev, p_in = [], []
        for e in range(2):
            h = 2 * p + e
            lm = (lane >= e * ML_DK) & (lane < (e + 1) * ML_DK)
            ig = gb_t[:, LANE_MI + h:LANE_MI + h + 1]
            lf = lf_t[:, LANE_MF + h:LANE_MF + h + 1]
            m_old = m0_ref[:, h:h + 1]
            m_new = jnp.maximum(lf + m_old, ig)
            w_prev.append(jnp.exp(lf + m_old - m_new))
            p_in.append(jnp.exp(ig - m_new))
            qk = jnp.sum(jnp.where(lm, qb * kb, 0.0), axis=-1, keepdims=True)
            qn = jnp.sum(jnp.where(lm, qb * n_p, 0.0), axis=-1, keepdims=True)
            mout_ref[:, h:h + 1] = m_new
            heads.append(dict(p=p, e=e, q_t=q_t, k_t=k_t, w_prev=w_prev[e], p_in=p_in[e],
                              pqk=p_in[e] * qk, wqn=w_prev[e] * qn, floor=jnp.exp(-m_new),
                              v=proj_ref[:, COL_MV + h * ML_DV:COL_MV + (h + 1) * ML_DV]))
        lo_lane = lane < ML_DK
        nout_ref[:, p * 128:(p + 1) * 128] = (jnp.where(lo_lane, w_prev[0], w_prev[1]) * n_p
                                              + jnp.where(lo_lane, p_in[0], p_in[1]) * kb)
    items = [dict(h=h, b=b) for h in range(HEADS) for b in range(bb)]
    for it in items:
        hd, b = heads[it["h"]], it["b"]
        rs = slice(hd["e"] * ML_DK, (hd["e"] + 1) * ML_DK)
        it["rs"] = rs
        it["q_col"] = hd["q_t"][rs, b:b + 1]
        it["k_col"] = hd["k_t"][rs, b:b + 1]
    for it in items:
        hd, b = heads[it["h"]], it["b"]
        c_bh = c0_ref[b, hd["p"], it["rs"], :]
        it["qc"] = jnp.sum(it["q_col"] * c_bh, axis=0, keepdims=True)
    for it in items:
        hd, b = heads[it["h"]], it["b"]
        pqk = hd["pqk"][b:b + 1, :]
        num = hd["w_prev"][b:b + 1, :] * it["qc"] + pqk * hd["v"][b:b + 1, :]
        den = hd["wqn"][b:b + 1, :] + pqk
        rowm_ref[it["h"], b:b + 1, :] = num / jnp.maximum(jnp.abs(den), hd["floor"][b:b + 1, :])
    for it in items:
        hd, b = heads[it["h"]], it["b"]
        cout_ref[b, hd["p"], it["rs"], :] = (hd["w_prev"][b:b + 1, :] * c0_ref[b, hd["p"], it["rs"], :]
                                             + (hd["p_in"][b:b + 1, :] * it["k_col"]) * hd["v"][b:b + 1, :])
    for h in range(HEADS):
        mo = proj_ref[:, COL_MO + h * ML_DV:COL_MO + (h + 1) * ML_DV]
        out = _sigmoid(mo) * _rms(rowm_ref[h], gnm_ref[...])
        om_ref[:, h * ML_DV:(h + 1) * ML_DV] = out.astype(om_ref.dtype)
    mout_ref[:, HEADS:128] = m0_ref[:, HEADS:128]


def _decode(proj, gates, cs, s0, c0, n0, m0, conv_w, plane, gng, gnm, bb):
    b = proj.shape[0]
    hp = HEADS // 2
    return pl.pallas_call(
        functools.partial(_decode_kernel, bb=bb),
        grid=(b // bb,),
        in_specs=[pl.BlockSpec((bb, COL_GATE), lambda i: (i, 0)),
                  pl.BlockSpec((bb, 128), lambda i: (i, 0)),
                  pl.BlockSpec((bb, CONV_W - 1, CONV_CH), lambda i: (i, 0, 0)),
                  pl.BlockSpec((bb, HEADS, GDN_D, GDN_D), lambda i: (i, 0, 0, 0)),
                  pl.BlockSpec((bb, hp, 128, 128), lambda i: (i, 0, 0, 0)),
                  pl.BlockSpec((bb, HEADS * ML_DK), lambda i: (i, 0)),
                  pl.BlockSpec((bb, 128), lambda i: (i, 0)),
                  pl.BlockSpec((CONV_W, CONV_CH), lambda i: (0, 0)),
                  pl.BlockSpec((3, 128), lambda i: (0, 0)),
                  pl.BlockSpec((1, GDN_D), lambda i: (0, 0)),
                  pl.BlockSpec((1, ML_DV), lambda i: (0, 0))],
        out_specs=[pl.BlockSpec((bb, HEADS * GDN_D), lambda i: (i, 0)),
                   pl.BlockSpec((bb, HEADS * ML_DV), lambda i: (i, 0)),
                   pl.BlockSpec((bb, HEADS, GDN_D, GDN_D), lambda i: (i, 0, 0, 0)),
                   pl.BlockSpec((bb, hp, 128, 128), lambda i: (i, 0, 0, 0)),
                   pl.BlockSpec((bb, HEADS * ML_DK), lambda i: (i, 0)),
                   pl.BlockSpec((bb, 128), lambda i: (i, 0))],
        out_shape=[jax.ShapeDtypeStruct((b, HEADS * GDN_D), BF16),
                   jax.ShapeDtypeStruct((b, HEADS * ML_DV), BF16),
                   jax.ShapeDtypeStruct((b, HEADS, GDN_D, GDN_D), F32),
                   jax.ShapeDtypeStruct((b, hp, 128, 128), F32),
                   jax.ShapeDtypeStruct((b, HEADS * ML_DK), F32),
                   jax.ShapeDtypeStruct((b, 128), F32)],
        scratch_shapes=[pltpu.VMEM((HEADS, bb, 128), F32), pltpu.VMEM((HEADS, bb, 128), F32)],
        compiler_params=pltpu.CompilerParams(dimension_semantics=("arbitrary",),
                                             vmem_limit_bytes=VMEM_LIMIT),
        name="decode_step",
    )(proj, gates, cs, s0, c0, n0, m0, conv_w, plane, gng, gnm)


def _regroup_w_in(w_in):
    sizes = (512, 512, 512, 512, 4, 4, 256, 256, 512, 512, 4, 4)
    offs = [0]
    for s in sizes:
        offs.append(offs[-1] + s)
    gq, gk, gv, gz, gb, ga, mq, mk, mv, mo, mi, mf = (w_in[:, offs[i]:offs[i + 1]] for i in range(12))
    pad = jnp.zeros((w_in.shape[0], 128 - 16), w_in.dtype)
    return jnp.concatenate([gq, gk, gv, gz, mv, mo, mq, mk, gb, ga, mi, mf, pad], axis=1).astype(BF16)


def _lane_vec(pairs):
    v = jnp.zeros((128,), F32)
    for off, val in pairs:
        v = v.at[off:off + HEADS].set(val.astype(F32))
    return v


def _prep_params(norm_pre_mix, w_in, conv_w, a_log, dt_bias, gdn_norm_g, b_igate, b_fgate, mlstm_norm_g,
                 w_out, norm_post_mix, norm_pre_mlp, w_up, w_down, norm_post_mlp):
    alog_v = _lane_vec([(LANE_GA, a_log[0])])
    dtb_v = _lane_vec([(LANE_GA, dt_bias[0])])
    bias_v = _lane_vec([(LANE_MI, b_igate[0]), (LANE_MF, b_fgate[0])])
    return dict(
        w_in=_regroup_w_in(w_in[0]), wo=w_out[0].astype(BF16), wu=w_up[0].astype(BF16),
        wd=w_down[0].astype(BF16),
        g_pre=norm_pre_mix[0][None, :], g1=norm_post_mix[0][None, :], g2=norm_pre_mlp[0][None, :],
        g3=norm_post_mlp[0][None, :], cw=conv_w[0], gng=gdn_norm_g[0][None, :], gnm=mlstm_norm_g[0][None, :],
        plane=jnp.stack([alog_v, dtb_v, bias_v]),
        psub=jnp.stack([alog_v[:16], dtb_v[:16], bias_v[:16]], axis=1),
    )


def _prompt_path(x, conv0, s0, c0, n0, m0, prm, tb, tm):
    bsz, seq, d = x.shape
    hp = HEADS // 2
    x2d = x.reshape(bsz * seq, d)
    proj, gates, gates_t = _in_proj(x2d, prm["g_pre"], prm["w_in"], tm=tm, out_dtype=BF16)
    proj = proj.reshape(bsz, seq, COL_GATE)
    gates = gates.reshape(bsz, seq, 128)
    gates_t = gates_t.reshape(bsz, seq // CHUNK, 16, CHUNK)
    cs8 =jnp.pad(conv0, ((0, 0), (8 - (CONV_W - 1), 0), (0, 0)))
    y, s_new, c_new, n_new, m_new = _layer_prompt(
        proj, gates, gates_t, cs8, s0, c0.reshape(bsz, hp, 128, 128), n0.reshape(bsz, hp, 1, 128),
        jnp.pad(m0, ((0, 0), (0, 128 - HEADS)))[:, None, :], x2d, prm["cw"], prm["plane"], prm["psub"],
        prm["gng"], prm["gnm"], prm["wo"], prm["g1"], prm["g2"], prm["wu"], prm["wd"], prm["g3"], tb=tb)
    y = y.reshape(bsz, seq, d)
    xp = jnp.concatenate([conv0, proj[:, seq - (CONV_W - 1):, COL_QKV:COL_QKV + CONV_CH].astype(F32)], axis=1)
    conv_new = xp[:, -(CONV_W - 1):]
    return y, (conv_new, s_new, c_new.reshape(bsz, HEADS, ML_DK, ML_DV), n_new.reshape(bsz, HEADS, ML_DK),
               m_new[:, 0, :HEADS])


def _sample_path(x, conv0, s0, c0, n0, m0, prm):
    dec, _, d = x.shape
    hp = HEADS // 2
    xs = x.reshape(dec, d)
    proj, gates, _ = _in_proj(xs, prm["g_pre"], prm["w_in"], tm=dec, out_dtype=F32)
    og, om, s_new, c_new, n_new, m_new = _decode(
        proj, gates, conv0, s0, c0.reshape(dec, hp, 128, 128), n0.reshape(dec, HEADS * ML_DK),
        jnp.pad(m0, ((0, 0), (0, 128 - HEADS))), prm["cw"], prm["plane"], prm["gng"], prm["gnm"], bb=8)
    y = _out_mlp(og, om, xs, prm["wo"], prm["g1"], prm["g2"], prm["wu"], prm["wd"], prm["g3"],
                 tm=dec).reshape(dec, 1, d)
    conv_new = jnp.concatenate([conv0[:, 1:, :], proj[:, None, COL_QKV:COL_QKV + CONV_CH]], axis=1)
    return y, (conv_new, s_new, c_new.reshape(dec, HEADS, ML_DK, ML_DV), n_new.reshape(dec, HEADS, ML_DK),
               m_new[:, :HEADS])


def kernel(x_prompt, x_sample, state_gdn_conv, state_gdn_S, state_mlstm_C, state_mlstm_n, state_mlstm_m,
           norm_pre_mix, w_in, conv_w, a_log, dt_bias, gdn_norm_g, b_igate, b_fgate, mlstm_norm_g, w_out,
           norm_post_mix, norm_pre_mlp, w_up, w_down, norm_post_mlp):
    bsz = x_prompt.shape[0]
    prm = _prep_params(norm_pre_mix, w_in, conv_w, a_log, dt_bias, gdn_norm_g, b_igate, b_fgate, mlstm_norm_g,
                       w_out, norm_post_mix, norm_pre_mlp, w_up, w_down, norm_post_mlp)
    y_p, p_st = _prompt_path(
        x_prompt, jnp.zeros((bsz, CONV_W - 1, CONV_CH), F32), jnp.zeros((bsz, HEADS, GDN_D, GDN_D), F32),
        jnp.zeros((bsz, HEADS, ML_DK, ML_DV), F32), jnp.zeros((bsz, HEADS, ML_DK), F32),
        jnp.zeros((bsz, HEADS), F32), prm, tb=256, tm=512)
    y_s, s_st = _sample_path(x_sample, state_gdn_conv[0], state_gdn_S[0], state_mlstm_C[0], state_mlstm_n[0],
                             state_mlstm_m[0], prm)
    return (y_p, y_s) + tuple(a[None] for a in p_st) + tuple(a[None] for a in s_st)
```

```python
import functools

import jax
import jax.numpy as jnp
from jax import lax
from jax.experimental import pallas as pl
from jax.experimental.pallas import tpu as pltpu

F32 = jnp.float32
BF16 = jnp.bfloat16
EPS = 1e-6

D_MODEL = 1024
HEADS = 4
GDN_D = 128
ML_DK = 64
ML_DV = 128
CONV_W = 4
CONV_CH = 3 * HEADS * GDN_D
D_FF = 4 * D_MODEL
CHUNK = 64

COL_QKV = 0
COL_Z = 1536
COL_MV = 2048
COL_MO = 2560
COL_MQ = 3072
COL_MK = 3328
COL_GATE = 3584
N_PROJ = COL_GATE + 128
LANE_GB, LANE_GA, LANE_MI, LANE_MF = 0, 4, 8, 12

VMEM_LIMIT = 56 * 1024 * 1024


def _rms(x, g):
    return x * lax.rsqrt(jnp.mean(x * x, axis=-1, keepdims=True) + EPS) * g


def _softplus(x):
    return jnp.maximum(x, 0.0) + jnp.log1p(jnp.exp(-jnp.abs(x)))


def _sigmoid(x):
    return 1.0 / (1.0 + jnp.exp(-x))


def _mm(a, b):
    return jnp.dot(a.astype(BF16), b.astype(BF16), preferred_element_type=F32)


def _mm_nt(a, b):
    return lax.dot_general(a.astype(BF16), b.astype(BF16), (((1,), (1,)), ((), ())),
                           preferred_element_type=F32)


W_IN_COLS = 3600
W_IN_SPANS = ((0, COL_QKV, 2048),
              (2568, COL_MV, 1024),
              (2056, COL_MQ, 512),
              (2048, COL_GATE, 8),
              (3592, COL_GATE + 8, 8))


def _in_proj_kernel(x_ref, g_ref, win_ref, o_ref, gc_ref, gt_ref, h_ref, w_ref, *, n_chunk):
    @pl.when(pl.program_id(0) == 0)
    def _():
        w_ref[:, COL_GATE:N_PROJ] = jnp.zeros((w_ref.shape[0], N_PROJ - COL_GATE), BF16)
        for r0 in range(0, w_ref.shape[0], 256):
            for src, dst, width in W_IN_SPANS:
                w_ref[r0:r0 + 256, dst:dst + width] = win_ref[r0:r0 + 256, src:src + width].astype(BF16)

    h_ref[...] = _rms(x_ref[...], g_ref[...]).astype(BF16)
    n = o_ref.shape[1]
    for c0 in range(0, n, n_chunk):
        c1 = min(c0 + n_chunk, n)
        o_ref[:, c0:c1] = jnp.dot(h_ref[...], w_ref[:, c0:c1], preferred_element_type=F32).astype(o_ref.dtype)
    gates = jnp.dot(h_ref[...], w_ref[:, COL_GATE:COL_GATE + 128], preferred_element_type=F32)
    gc_ref[...] = gates
    gt = gates.T
    for c in range(gt_ref.shape[0]):
        gt_ref[c] = gt[0:16, c * CHUNK:(c + 1) * CHUNK]


def _in_proj(x2d, g, w, tm, out_dtype):
    m, k = x2d.shape
    return pl.pallas_call(
        functools.partial(_in_proj_kernel, n_chunk=512),
        grid=(m // tm,),
        in_specs=[pl.BlockSpec((tm, k), lambda i: (i, 0)),
                  pl.BlockSpec((1, k), lambda i: (0, 0)),
                  pl.BlockSpec((k, W_IN_COLS), lambda i: (0, 0), pipeline_mode=pl.Buffered(1))],
        out_specs=[pl.BlockSpec((tm, COL_GATE), lambda i: (i, 0)),
                   pl.BlockSpec((tm, 128), lambda i: (i, 0)),
                   pl.BlockSpec((tm // CHUNK, 16, CHUNK), lambda i: (i, 0, 0))],
        out_shape=[jax.ShapeDtypeStruct((m, COL_GATE), out_dtype),
                   jax.ShapeDtypeStruct((m, 128), F32),
                   jax.ShapeDtypeStruct((m // CHUNK, 16, CHUNK), F32)],
        scratch_shapes=[pltpu.VMEM((tm, k), BF16), pltpu.VMEM((k, N_PROJ), BF16)],
        compiler_params=pltpu.CompilerParams(dimension_semantics=("arbitrary",),
                                             vmem_limit_bytes=VMEM_LIMIT),
        name="in_proj",
    )(x2d, g, w)


def _out_mlp_kernel(og_ref, om_ref, x_ref, wo_ref, g1_ref, g2_ref, wu_ref, wd_ref, g3_ref,
                    y_ref, hn_ref, acc_ref, *, ff_chunk):
    half = og_ref.shape[1]
    mix = (jnp.dot(og_ref[...], wo_ref[0:half, :], preferred_element_type=F32)
           + jnp.dot(om_ref[...], wo_ref[half:2 * half, :], preferred_element_type=F32))
    x1 = x_ref[...] + _rms(mix, g1_ref[...])
    y_ref[...] = x1
    hn_ref[...] = _rms(x1, g2_ref[...]).astype(BF16)
    d_ff = wu_ref.shape[1]
    for c0 in range(0, d_ff, ff_chunk):
        u = jnp.dot(hn_ref[...], wu_ref[:, c0:c0 + ff_chunk], preferred_element_type=F32)
        u = jnp.square(jnp.maximum(u, 0.0)).astype(BF16)
        d = jnp.dot(u, wd_ref[c0:c0 + ff_chunk, :], preferred_element_type=F32)
        if c0 == 0:
            acc_ref[...] = d
        else:
            acc_ref[...] += d
    y_ref[...] = y_ref[...] + _rms(acc_ref[...], g3_ref[...])


def _out_mlp(og, om, x2d, wo, g1, g2, wu, wd, g3, tm):
    m, d = x2d.shape
    half = og.shape[1]
    d_ff = wu.shape[1]
    const = lambda i: (0, 0)
    return pl.pallas_call(
        functools.partial(_out_mlp_kernel, ff_chunk=1024),
        grid=(m // tm,),
        in_specs=[pl.BlockSpec((tm, half), lambda i: (i, 0)),
                  pl.BlockSpec((tm, half), lambda i: (i, 0)),
                  pl.BlockSpec((tm, d), lambda i: (i, 0)),
                  pl.BlockSpec((d, d), const, pipeline_mode=pl.Buffered(1)),
                  pl.BlockSpec((1, d), const),
                  pl.BlockSpec((1, d), const),
                  pl.BlockSpec((d, d_ff), const, pipeline_mode=pl.Buffered(1)),
                  pl.BlockSpec((d_ff, d), const, pipeline_mode=pl.Buffered(1)),
                  pl.BlockSpec((1, d), const)],
        out_specs=pl.BlockSpec((tm, d), lambda i: (i, 0)),
        out_shape=jax.ShapeDtypeStruct((m, d), F32),
        scratch_shapes=[pltpu.VMEM((tm, d), BF16), pltpu.VMEM((tm, d), F32)],
        compiler_params=pltpu.CompilerParams(dimension_semantics=("arbitrary",),
                                             vmem_limit_bytes=VMEM_LIMIT),
        name="out_mlp",
    )(og, om, x2d, wo, g1, g2, wu, wd, g3)


def _chunk_masks():
    ii = lax.broadcasted_iota(jnp.int32, (CHUNK, CHUNK), 0)
    jj = lax.broadcasted_iota(jnp.int32, (CHUNK, CHUNK), 1)
    return ii, jj


def _cumsum_col_row(x_col, x_row, ii, jj):
    c_col = jnp.sum(jnp.where(jj <= ii, x_row, 0.0), axis=1, keepdims=True)
    c_row = jnp.sum(jnp.where(ii <= jj, x_col, 0.0), axis=0, keepdims=True)
    return c_col, c_row


def _interleave(*tasks):
    live = [[g, n, 0] for g, n in tasks]
    while live:
        entry = min(live, key=lambda e: (e[2] + 1) / e[1])
        try:
            next(entry[0])
            entry[2] += 1
        except StopIteration:
            live.remove(entry)


def _halves(items):
    mid = len(items) // 2
    return items[:mid], items[mid:]


def _gdn_phase_a(yc_ref, gcol_ref, grow_ref, plane_ref, psub_ref, wv_ref, lhs_ref, qk_ref, ket_ref, ge_ref,
                 ncb):
    ii, jj = _chunk_masks()
    incl = jj <= ii
    strict = jj < ii
    eye = (ii == jj).astype(F32)
    neg_a_lane = -jnp.exp(plane_ref[0:1, :])
    dtb_lane = plane_ref[1:2, :]
    neg_a_sub = -jnp.exp(psub_ref[:, 0:1])
    dtb_sub = psub_ref[:, 1:2]
    items = []
    for c in range(ncb):
        rows = slice(c * CHUNK, (c + 1) * CHUNK)
        gc = gcol_ref[rows, :]
        gr = grow_ref[c]
        beta_t = _sigmoid(gc)
        g_t = neg_a_lane * _softplus(gc + dtb_lane)
        g_r = neg_a_sub * _softplus(gr + dtb_sub)
        for h in range(HEADS):
            lo = h * GDN_D
            items.append(dict(c=c, h=h,
                              q=yc_ref[rows, lo:lo + GDN_D],
                              k=yc_ref[rows, 512 + lo:512 + lo + GDN_D],
                              v=yc_ref[rows, 1024 + lo:1024 + lo + GDN_D],
                              beta=beta_t[:, LANE_GB + h:LANE_GB + h + 1],
                              gg_col=g_t[:, LANE_GA + h:LANE_GA + h + 1],
                              gg_row=g_r[LANE_GA + h:LANE_GA + h + 1, :]))
    yield
    for it in items:
        it["k"] = it["k"] * lax.rsqrt(jnp.sum(it["k"] * it["k"], axis=-1, keepdims=True) + EPS)
    yield
    for it in items:
        it["kk"] = _mm_nt(it["k"], it["k"])
    yield
    for it in items:
        it["q"] = (it["q"] * lax.rsqrt(jnp.sum(it["q"] * it["q"], axis=-1, keepdims=True) + EPS)
                   * (GDN_D ** -0.5))
    yield
    for it in items:
        it["g_col"], g_row = _cumsum_col_row(it["gg_col"], it["gg_row"], ii, jj)
        it["decay"] = jnp.where(incl, jnp.exp(jnp.where(incl, it["g_col"] - g_row, 0.0)), 0.0)
    yield
    for it in items:
        n_mat = jnp.where(strict, it["beta"] * it["kk"] * it["decay"], 0.0)
        it["x"] = eye - n_mat
        it["p"] = -n_mat
    yield
    for _ in range(5):
        yield
        for it in items:
            it["p"] = _mm(it["p"], it["p"])
        yield
        for it in items:
            it["x"] = it["x"] + _mm(it["x"], it["p"])
    yield
    for it in items:
        e_g = jnp.exp(it["g_col"])
        it["e_g"] = e_g
        rhs = jnp.concatenate([it["beta"] * it["v"], (it["beta"] * e_g) * it["k"]], axis=1)
        it["w"] = _mm(it["x"], rhs)
    yield
    for it in items:
        it["qk"] = _mm_nt(it["q"], it["k"]) * it["decay"]
    yield
    for it in items:
        c, h = it["c"], it["h"]
        g_end = it["g_col"][CHUNK - 1:CHUNK, :]
        k_end = it["k"] * jnp.exp(g_end - it["g_col"])
        wv_ref[c, h] = it["w"][:, 0:GDN_D]
        lhs_ref[c, h] = jnp.concatenate([it["w"][:, GDN_D:2 * GDN_D], it["e_g"] * it["q"]],
                                        axis=0).astype(BF16)
        qk_ref[c, h] = it["qk"].astype(BF16)
        ket_ref[c, h] = k_end.T.astype(BF16)
        ge_ref[c, h] = jnp.broadcast_to(jnp.exp(g_end), (1, GDN_D))


def _gdn_phase_b(qkvz_ref, gn_ref, og_ref, s_ref, wv_ref, lhs_ref, qk_ref, ket_ref, ge_ref, ncb):
    gn = gn_ref[...]

    def epilogue(c, o):
        rows = slice(c * CHUNK, (c + 1) * CHUNK)
        for h in range(HEADS):
            lo = h * GDN_D
            z = qkvz_ref[rows, COL_Z + lo:COL_Z + lo + GDN_D].astype(F32)
            out = _rms(o[h], gn) * (z * _sigmoid(z))
            og_ref[rows, lo:lo + GDN_D] = out.astype(og_ref.dtype)

    s = [s_ref[h] for h in range(HEADS)]
    o_prev = None
    for c in range(ncb):
        r = [jnp.dot(lhs_ref[c, h], s[h].astype(BF16), preferred_element_type=F32) for h in range(HEADS)]
        yield
        if o_prev is not None:
            epilogue(c - 1, o_prev)
        ub = [(wv_ref[c, h] - r[h][0:CHUNK]).astype(BF16) for h in range(HEADS)]
        s = [ge_ref[c, h] * s[h] + jnp.dot(ket_ref[c, h], ub[h], preferred_element_type=F32)
             for h in range(HEADS)]
        o_prev = [r[h][CHUNK:2 * CHUNK] + jnp.dot(qk_ref[c, h], ub[h], preferred_element_type=F32)
                  for h in range(HEADS)]
        yield
    epilogue(ncb - 1, o_prev)
    for h in range(HEADS):
        s_ref[h] = s[h]


def _mlstm_block(mvo_ref, mqk_ref, gcol_ref, grow_ref, plane_ref, psub_ref, gn_ref, om_ref,
                 c_ref, n_ref, m_ref, ncb):
    ii, jj = _chunk_masks()
    incl = jj <= ii
    lane = lax.broadcasted_iota(jnp.int32, (1, 128), 1)
    row128 = lax.broadcasted_iota(jnp.int32, (128, 1), 0)
    gn = gn_ref[...]
    blane = plane_ref[2:3, :]
    bsub = psub_ref[:, 2:3]
    items = []
    for c in range(ncb):
        rows = slice(c * CHUNK, (c + 1) * CHUNK)
        gc = gcol_ref[rows, :] + blane
        gr = grow_ref[c] + bsub
        lf_c = -_softplus(-gc)
        lf_r = -_softplus(-gr)
        for p in range(HEADS // 2):
            qb = mqk_ref[rows, p * 128:(p + 1) * 128].astype(F32)
            kb = mqk_ref[rows, 256 + p * 128:256 + (p + 1) * 128].astype(F32) * (ML_DK ** -0.5)
            for e in range(2):
                h = 2 * p + e
                lm = (lane >= e * ML_DK) & (lane < (e + 1) * ML_DK)
                items.append(dict(c=c, rows=rows, p=p, e=e, h=h, qh=jnp.where(lm, qb, 0.0), kb=kb,
                                  kh=jnp.where(lm, kb, 0.0),
                                  lf_col=lf_c[:, LANE_MF + h:LANE_MF + h + 1],
                                  lf_row=lf_r[LANE_MF + h:LANE_MF + h + 1, :],
                                  ig_row=gr[LANE_MI + h:LANE_MI + h + 1, :],
                                  ig_col=gc[:, LANE_MI + h:LANE_MI + h + 1],
                                  v=mvo_ref[rows, h * ML_DV:(h + 1) * ML_DV]))
        if c % 2 == 1:
            yield
    for part in _halves(items):
        for it in part:
            it["qk"] = _mm_nt(it["qh"], it["kb"])
        yield
    for part in _halves(items):
        for it in part:
            it["f_col"], it["f_row"] = _cumsum_col_row(it["lf_col"], it["lf_row"], ii, jj)
        yield
    for part in _halves(items):
        for it in part:
            it["d_mat"] = jnp.where(incl, it["f_col"] - it["f_row"] + it["ig_row"], -jnp.inf)
            it["d_max"] = jnp.max(it["d_mat"], axis=1, keepdims=True)
        yield
    for part in _halves(items):
        for it in part:
            p0 = jnp.where(incl, jnp.exp(jnp.where(incl, it["d_mat"] - it["d_max"], 0.0)), 0.0)
            pend0 = jnp.exp(it["f_col"][CHUNK - 1:CHUNK, :] - it["f_col"] + it["ig_col"]
                            - it["d_max"][CHUNK - 1:CHUNK, :])
            it["kp0"] = it["kh"] * pend0
            it["pqk0"] = p0 * it["qk"]
        yield
    for part in _halves(items):
        for it in part:
            it["pv0"] = _mm(it["pqk0"], it["v"])
            it["rs0"] = jnp.sum(it["pqk0"], axis=-1, keepdims=True)
        yield
    for part in _halves(items):
        for it in part:
            it["cadd0"] = _mm(it["kp0"].T, it["v"])
            it["nadd0"] = jnp.sum(it["kp0"], axis=0, keepdims=True)
        yield

    m_cur = [m_ref[:, h:h + 1] for h in range(HEADS)]
    for it in items:
        h = it["h"]
        it["m_prev"] = m_cur[h]
        m_cur[h] = jnp.maximum(it["f_col"][CHUNK - 1:CHUNK, :] + m_cur[h], it["d_max"][CHUNK - 1:CHUNK, :])
    yield
    for part in _halves(items):
        for it in part:
            bcol = it["f_col"] + it["m_prev"]
            mt = jnp.maximum(bcol, it["d_max"])
            it["mt"] = mt
            it["w_prev"] = jnp.exp(bcol - mt)
            it["sc"] = jnp.exp(it["d_max"] - mt)
        yield
    c_cur = [c_ref[p] for p in range(HEADS // 2)]
    n_cur = [n_ref[p] for p in range(HEADS // 2)]
    for c in range(ncb):
        for p in range(HEADS // 2):
            pair = [it for it in items if it["c"] == c and it["p"] == p]
            w_end = [it["w_prev"][CHUNK - 1:CHUNK, :] for it in pair]
            s_end = [it["sc"][CHUNK - 1:CHUNK, :] for it in pair]
            for it in pair:
                it["c_prev"] = c_cur[p]
                it["n_prev"] = n_cur[p]
            c_cur[p] = (jnp.where(row128 < ML_DK, w_end[0], w_end[1]) * c_cur[p]
                        + s_end[0] * pair[0]["cadd0"] + s_end[1] * pair[1]["cadd0"])
            n_cur[p] = (jnp.where(lane < ML_DK, w_end[0], w_end[1]) * n_cur[p]
                        + s_end[0] * pair[0]["nadd0"] + s_end[1] * pair[1]["nadd0"])
        if c % 2 == 1:
            yield
    for part in _halves(items):
        for it in part:
            it["qc"] = _mm(it["qh"], it["c_prev"])
        yield
    for part in _halves(items):
        for it in part:
            num = it["w_prev"] * it["qc"] + it["sc"] * it["pv0"]
            den = (it["w_prev"] * jnp.sum(it["qh"] * it["n_prev"], axis=-1, keepdims=True)
                   + it["sc"] * it["rs0"])
            it["hh"] = num / jnp.maximum(jnp.abs(den), jnp.exp(-it["mt"]))
        yield
    for part in _halves(items):
        for it in part:
            h = it["h"]
            mo = mvo_ref[it["rows"], 512 + h * ML_DV:512 + (h + 1) * ML_DV].astype(F32)
            out = _sigmoid(mo) * _rms(it["hh"], gn)
            om_ref[it["rows"], h * ML_DV:(h + 1) * ML_DV] = out.astype(om_ref.dtype)
        yield
    for p in range(HEADS // 2):
        c_ref[p] = c_cur[p]
        n_ref[p] = n_cur[p]
    for h in range(HEADS):
        m_ref[:, h:h + 1] = m_cur[h]


def _chain(*gens):
    for g in gens:
        yield from g


def _conv_stage(xp_ref, yc_ref, cw_ref, tb):
    for ct in range(CONV_CH // 128):
        cols = slice(ct * 128, (ct + 1) * 128)
        w = cw_ref[:, cols]
        for r0 in range(0, tb, 128):
            acc = xp_ref[5 + r0:5 + r0 + 128, cols] * w[0:1, :]
            for j in range(1, CONV_W):
                acc = acc + xp_ref[5 + j + r0:5 + j + r0 + 128, cols] * w[j:j + 1, :]
            yc_ref[r0:r0 + 128, cols] = acc * _sigmoid(acc)
        yield


def _out_mlp_block(og_ref, om_ref, x_ref, wo_ref, g1_ref, g2_ref, wu_ref, wd_ref, g3_ref,
                   y_ref, hn_ref, acc_ref, ff_chunk, n_split):
    half = og_ref.shape[1]
    d = x_ref.shape[1]
    d_ff = wu_ref.shape[1]
    col_groups = [slice(c, c + d // n_split) for c in range(0, d, d // n_split)]
    for cg in col_groups:
        acc_ref[:, cg] = (jnp.dot(og_ref[...], wo_ref[0:half, cg], preferred_element_type=F32)
                          + jnp.dot(om_ref[...], wo_ref[half:2 * half, cg], preferred_element_type=F32))
        yield
    x1 = x_ref[...] + _rms(acc_ref[...], g1_ref[...])
    y_ref[...] = x1
    hn_ref[...] = _rms(x1, g2_ref[...]).astype(BF16)
    yield
    for c0 in range(0, d_ff, ff_chunk):
        u = jnp.dot(hn_ref[...], wu_ref[:, c0:c0 + ff_chunk], preferred_element_type=F32)
        u = jnp.square(jnp.maximum(u, 0.0)).astype(BF16)
        for cg in col_groups:
            dd = jnp.dot(u, wd_ref[c0:c0 + ff_chunk, cg], preferred_element_type=F32)
            if c0 == 0:
                acc_ref[:, cg] = dd
            else:
                acc_ref[:, cg] += dd
        yield
    y_ref[...] = y_ref[...] + _rms(acc_ref[...], g3_ref[...])
    yield


def _layer_kernel(qkvz_ref, mvo_ref, mqk_ref, gcol_ref, grow_ref, cs_ref, s0_ref, c0_ref, n0_ref, m0_ref,
                  x_ref, cw_ref, plane_ref, psub_ref, gng_ref, gnm_ref,
                  wo_ref, g1_ref, g2_ref, wu_ref, wd_ref, g3_ref,
                  y_ref, sout_ref, cout_ref, nout_ref, mout_ref,
                  xp_ref, yc_ref, s_ref, wv_ref, lhs_ref, qk_ref, ket_ref, ge_ref, c_ref, n_ref, m_ref,
                  og_ref, om_ref, hn_ref, acc_ref, *, tb, nt, n_blocks):
    g = pl.program_id(0)
    t = jnp.minimum(g, n_blocks - 1) % nt
    ncb = tb // CHUNK
    par = g % 2

    @pl.when(g == 0)
    def _():
        og_ref[1] = jnp.zeros(og_ref.shape[1:], og_ref.dtype)
        om_ref[1] = jnp.zeros(om_ref.shape[1:], om_ref.dtype)

    @pl.when(t == 0)
    def _():
        xp_ref[0:8, :] = cs_ref[...]
        s_ref[...] = s0_ref[...]
        c_ref[...] = c0_ref[...]
        n_ref[...] = n0_ref[...]
        m_ref[...] = m0_ref[...]

    @pl.when(t > 0)
    def _():
        xp_ref[0:8, :] = xp_ref[tb:tb + 8, :]

    xp_ref[8:tb + 8, :] = qkvz_ref[:, COL_QKV:COL_QKV + CONV_CH].astype(F32)

    og_w, om_w = og_ref.at[par], om_ref.at[par]
    og_r, om_r = og_ref.at[1 - par], om_ref.at[1 - par]
    _interleave(
        (_chain(_conv_stage(xp_ref, yc_ref, cw_ref, tb),
                _gdn_phase_a(yc_ref, gcol_ref, grow_ref, plane_ref, psub_ref, wv_ref, lhs_ref, qk_ref, ket_ref,
                             ge_ref, ncb),
                _gdn_phase_b(qkvz_ref, gng_ref, og_w, s_ref, wv_ref, lhs_ref, qk_ref, ket_ref, ge_ref, ncb)),
         CONV_CH // 128 + 20 + 2 * ncb),
        (_out_mlp_block(og_r, om_r, x_ref, wo_ref, g1_ref, g2_ref, wu_ref, wd_ref, g3_ref, y_ref, hn_ref, acc_ref,
                        ff_chunk=256, n_split=2), 2 + 1 + wu_ref.shape[1] // 256 + 1),
        (_mlstm_block(mvo_ref, mqk_ref, gcol_ref, grow_ref, plane_ref, psub_ref, gnm_ref, om_w,
                      c_ref, n_ref, m_ref, ncb), 21 + ncb))

    @pl.when((t == nt - 1) & (g < n_blocks))
    def _():
        sout_ref[...] = s_ref[...]
        cout_ref[...] = c_ref[...]
        nout_ref[...] = n_ref[...]
        mout_ref[...] = m_ref[...]


def _layer_prompt(proj, gates, gates_t, cs8, s0, c0, n0, m0, x2d, conv_w, plane, psub, gng, gnm,
                  wo, g1, g2, wu, wd, g3, tb):
    b, t, _ = proj.shape
    d = x2d.shape[1]
    d_ff = wu.shape[1]
    nt = t // tb
    n_blocks = b * nt
    ncb = tb // CHUNK
    hp = HEADS // 2

    def blk(g):
        gm = jnp.minimum(g, n_blocks - 1)
        return gm // nt, gm % nt

    def tok3(col):
        return lambda g: blk(g) + (col,)

    per_seq3 = lambda g: (blk(g)[0], 0, 0)
    per_seq4 = lambda g: (blk(g)[0], 0, 0, 0)
    prev_rows = lambda g: (jnp.maximum(g - 1, 0), 0)
    const = lambda g: (0, 0)
    resident = dict(pipeline_mode=pl.Buffered(1))
    return pl.pallas_call(
        functools.partial(_layer_kernel, tb=tb, nt=nt, n_blocks=n_blocks),
        grid=(n_blocks + 1,),
        in_specs=[pl.BlockSpec((None, tb, 2048), tok3(0)),
                  pl.BlockSpec((None, tb, 1024), tok3(COL_MV // 1024)),
                  pl.BlockSpec((None, tb, 512), tok3(COL_MQ // 512)),
                  pl.BlockSpec((None, tb, 128), tok3(0)),
                  pl.BlockSpec((None, ncb, 16, CHUNK), lambda g: blk(g) + (0, 0)),
                  pl.BlockSpec((None, 8, CONV_CH), per_seq3),
                  pl.BlockSpec((None, HEADS, GDN_D, GDN_D), per_seq4),
                  pl.BlockSpec((None, hp, 128, 128), per_seq4),
                  pl.BlockSpec((None, hp, 1, 128), per_seq4),
                  pl.BlockSpec((None, 1, 128), per_seq3),
                  pl.BlockSpec((tb, d), prev_rows),
                  pl.BlockSpec((CONV_W, CONV_CH), const),
                  pl.BlockSpec((3, 128), const),
                  pl.BlockSpec((16, 3), const),
                  pl.BlockSpec((1, GDN_D), const),
                  pl.BlockSpec((1, ML_DV), const),
                  pl.BlockSpec((d, d), const, **resident),
                  pl.BlockSpec((1, d), const),
                  pl.BlockSpec((1, d), const),
                  pl.BlockSpec((d, d_ff), const, **resident),
                  pl.BlockSpec((d_ff, d), const, **resident),
                  pl.BlockSpec((1, d), const)],
        out_specs=[pl.BlockSpec((tb, d), prev_rows),
                   pl.BlockSpec((None, HEADS, GDN_D, GDN_D), per_seq4),
                   pl.BlockSpec((None, hp, 128, 128), per_seq4),
                   pl.BlockSpec((None, hp, 1, 128), per_seq4),
                   pl.BlockSpec((None, 1, 128), per_seq3)],
        out_shape=[jax.ShapeDtypeStruct((b * t, d), F32),
                   jax.ShapeDtypeStruct((b, HEADS, GDN_D, GDN_D), F32),
                   jax.ShapeDtypeStruct((b, hp, 128, 128), F32),
                   jax.ShapeDtypeStruct((b, hp, 1, 128), F32),
                   jax.ShapeDtypeStruct((b, 1, 128), F32)],
        scratch_shapes=[pltpu.VMEM((tb + 8, CONV_CH), F32),
                        pltpu.VMEM((tb, CONV_CH), F32),
                        pltpu.VMEM((HEADS, GDN_D, GDN_D), F32),
                        pltpu.VMEM((ncb, HEADS, CHUNK, GDN_D), F32),
                        pltpu.VMEM((ncb, HEADS, 2 * CHUNK, GDN_D), BF16),
                        pltpu.VMEM((ncb, HEADS, CHUNK, CHUNK), BF16),
                        pltpu.VMEM((ncb, HEADS, GDN_D, CHUNK), BF16),
                        pltpu.VMEM((ncb, HEADS, 1, GDN_D), F32),
                        pltpu.VMEM((hp, 128, 128), F32),
                        pltpu.VMEM((hp, 1, 128), F32),
                        pltpu.VMEM((1, 128), F32),
                        pltpu.VMEM((2, tb, HEADS * GDN_D), BF16),
                        pltpu.VMEM((2, tb, HEADS * ML_DV), BF16),
                        pltpu.VMEM((tb, d), BF16),
                        pltpu.VMEM((tb, d), F32)],
        compiler_params=pltpu.CompilerParams(dimension_semantics=("arbitrary",),
                                             vmem_limit_bytes=VMEM_LIMIT),
        name="layer_prompt",
    )(proj, proj, proj, gates, gates_t, cs8, s0, c0, n0, m0, x2d, conv_w, plane, psub, gng, gnm,
      wo, g1, g2, wu, wd, g3)


def _decode_kernel(proj_ref, gates_ref, cs_ref, s0_ref, c0_ref, n0_ref, m0_ref, cw_ref, plane_ref, gng_ref, gnm_ref,
                   og_ref, om_ref, sout_ref, cout_ref, nout_ref, mout_ref, rowg_ref, rowm_ref, *, bb):
    lane = lax.broadcasted_iota(jnp.int32, (1, 128), 1)
    gt = gates_ref[...]
    beta_t = _sigmoid(gt)
    g_t = -jnp.exp(plane_ref[0:1, :]) * _softplus(gt + plane_ref[1:2, :])
    eg_t = jnp.exp(g_t)
    gb_t = gt + plane_ref[2:3, :]
    lf_t = -_softplus(-gb_t)

    heads = []
    for h in range(HEADS):
        cols = []
        for part in range(3):
            c0 = part * 512 + h * GDN_D
            w = cw_ref[:, c0:c0 + GDN_D]
            acc = cs_ref[:, 0, c0:c0 + GDN_D] * w[0:1, :]
            acc = acc + cs_ref[:, 1, c0:c0 + GDN_D] * w[1:2, :]
            acc = acc + cs_ref[:, 2, c0:c0 + GDN_D] * w[2:3, :]
            acc = acc + proj_ref[:, COL_QKV + c0:COL_QKV + c0 + GDN_D] * w[3:4, :]
            cols.append(acc * _sigmoid(acc))
        q, k, v = cols
        q = q * lax.rsqrt(jnp.sum(q * q, axis=-1, keepdims=True) + EPS) * (GDN_D ** -0.5)
        k = k * lax.rsqrt(jnp.sum(k * k, axis=-1, keepdims=True) + EPS)
        heads.append(dict(v=v, qk=jnp.sum(q * k, axis=-1, keepdims=True), q_t=q.T, k_t=k.T,
                          beta=beta_t[:, LANE_GB + h:LANE_GB + h + 1],
                          eg=eg_t[:, LANE_GA + h:LANE_GA + h + 1]))
    items = [dict(h=h, b=b) for h in range(HEADS) for b in range(bb)]
    for it in items:
        hd, b = heads[it["h"]], it["b"]
        it["k_col"] = hd["k_t"][:, b:b + 1]
        it["q_col"] = hd["q_t"][:, b:b + 1]
    for it in items:
        s_bh = s0_ref[it["b"], it["h"]]
        it["ks"] = jnp.sum(it["k_col"] * s_bh, axis=0, keepdims=True)
        it["qs"] = jnp.sum(it["q_col"] * s_bh, axis=0, keepdims=True)
    for it in items:
        hd, b = heads[it["h"]], it["b"]
        eg_b = hd["eg"][b:b + 1, :]
        it["eg_b"] = eg_b
        it["u"] = hd["beta"][b:b + 1, :] * (hd["v"][b:b + 1, :] - eg_b * it["ks"])
        rowg_ref[it["h"], b:b + 1, :] = eg_b * it["qs"] + hd["qk"][b:b + 1, :] * it["u"]
    for it in items:
        sout_ref[it["b"], it["h"]] = it["eg_b"] * s0_ref[it["b"], it["h"]] + it["k_col"] * it["u"]
    for h in range(HEADS):
        z = proj_ref[:, COL_Z + h * GDN_D:COL_Z + (h + 1) * GDN_D]
        out = _rms(rowg_ref[h], gng_ref[...]) * (z * _sigmoid(z))
        og_ref[:, h * GDN_D:(h + 1) * GDN_D] = out.astype(og_ref.dtype)

    heads = []
    for p in range(HEADS // 2):
        qb = proj_ref[:, COL_MQ + p * 128:COL_MQ + (p + 1) * 128]
        kb = proj_ref[:, COL_MK + p * 128:COL_MK + (p + 1) * 128] * (ML_DK ** -0.5)
        n_p = n0_ref[:, p * 128:(p + 1) * 128]
        q_t = qb.T
        k_t = kb.T
        w_prev, p_in = [], []
        for e in range(2):
            h = 2 * p + e
            lm = (lane >= e * ML_DK) & (lane < (e + 1) * ML_DK)
            ig = gb_t[:, LANE_MI + h:LANE_MI + h + 1]
            lf = lf_t[:, LANE_MF + h:LANE_MF + h + 1]
            m_old = m0_ref[:, h:h + 1]
            m_new = jnp.maximum(lf + m_old, ig)
            w_prev.append(jnp.exp(lf + m_old - m_new))
            p_in.append(jnp.exp(ig - m_new))
            qk = jnp.sum(jnp.where(lm, qb * kb, 0.0), axis=-1, keepdims=True)
            qn = jnp.sum(jnp.where(lm, qb * n_p, 0.0), axis=-1, keepdims=True)
            mout_ref[:, h:h + 1] = m_new
            heads.append(dict(p=p, e=e, q_t=q_t, k_t=k_t, w_prev=w_prev[e], p_in=p_in[e],
                              pqk=p_in[e] * qk, wqn=w_prev[e] * qn, floor=jnp.exp(-m_new),
                              v=proj_ref[:, COL_MV + h * ML_DV:COL_MV + (h + 1) * ML_DV]))
        lo_lane = lane < ML_DK
        nout_ref[:, p * 128:(p + 1) * 128] = (jnp.where(lo_lane, w_prev[0], w_prev[1]) * n_p
                                              + jnp.where(lo_lane, p_in[0], p_in[1]) * kb)
    items = [dict(h=h, b=b) for h in range(HEADS) for b in range(bb)]
    for it in items:
        hd, b = heads[it["h"]], it["b"]
        rs = slice(hd["e"] * ML_DK, (hd["e"] + 1) * ML_DK)
        it["rs"] = rs
        it["q_col"] = hd["q_t"][rs, b:b + 1]
        it["k_col"] = hd["k_t"][rs, b:b + 1]
    for it in items:
        hd, b = heads[it["h"]], it["b"]
        c_bh = c0_ref[b, hd["p"], it["rs"], :]
        it["qc"] = jnp.sum(it["q_col"] * c_bh, axis=0, keepdims=True)
    for it in items:
        hd, b = heads[it["h"]], it["b"]
        pqk = hd["pqk"][b:b + 1, :]
        num = hd["w_prev"][b:b + 1, :] * it["qc"] + pqk * hd["v"][b:b + 1, :]
        den = hd["wqn"][b:b + 1, :] + pqk
        rowm_ref[it["h"], b:b + 1, :] = num / jnp.maximum(jnp.abs(den), hd["floor"][b:b + 1, :])
    for it in items:
        hd, b = heads[it["h"]], it["b"]
        cout_ref[b, hd["p"], it["rs"], :] = (hd["w_prev"][b:b + 1, :] * c0_ref[b, hd["p"], it["rs"], :]
                                             + (hd["p_in"][b:b + 1, :] * it["k_col"]) * hd["v"][b:b + 1, :])
    for h in range(HEADS):
        mo = proj_ref[:, COL_MO + h * ML_DV:COL_MO + (h + 1) * ML_DV]
        out = _sigmoid(mo) * _rms(rowm_ref[h], gnm_ref[...])
        om_ref[:, h * ML_DV:(h + 1) * ML_DV] = out.astype(om_ref.dtype)
    mout_ref[:, HEADS:128] = m0_ref[:, HEADS:128]


def _decode(proj, gates, cs, s0, c0, n0, m0, conv_w, plane, gng, gnm, bb):
    b = proj.shape[0]
    hp = HEADS // 2
    return pl.pallas_call(
        functools.partial(_decode_kernel, bb=bb),
        grid=(b // bb,),
        in_specs=[pl.BlockSpec((bb, COL_GATE), lambda i: (i, 0)),
                  pl.BlockSpec((bb, 128), lambda i: (i, 0)),
                  pl.BlockSpec((bb, CONV_W - 1, CONV_CH), lambda i: (i, 0, 0)),
                  pl.BlockSpec((bb, HEADS, GDN_D, GDN_D), lambda i: (i, 0, 0, 0)),
                  pl.BlockSpec((bb, hp, 128, 128), lambda i: (i, 0, 0, 0)),
                  pl.BlockSpec((bb, HEADS * ML_DK), lambda i: (i, 0)),
                  pl.BlockSpec((bb, 128), lambda i: (i, 0)),
                  pl.BlockSpec((CONV_W, CONV_CH), lambda i: (0, 0)),
                  pl.BlockSpec((3, 128), lambda i: (0, 0)),
                  pl.BlockSpec((1, GDN_D), lambda i: (0, 0)),
                  pl.BlockSpec((1, ML_DV), lambda i: (0, 0))],
        out_specs=[pl.BlockSpec((bb, HEADS * GDN_D), lambda i: (i, 0)),
                   pl.BlockSpec((bb, HEADS * ML_DV), lambda i: (i, 0)),
                   pl.BlockSpec((bb, HEADS, GDN_D, GDN_D), lambda i: (i, 0, 0, 0)),
                   pl.BlockSpec((bb, hp, 128, 128), lambda i: (i, 0, 0, 0)),
                   pl.BlockSpec((bb, HEADS * ML_DK), lambda i: (i, 0)),
                   pl.BlockSpec((bb, 128), lambda i: (i, 0))],
        out_shape=[jax.ShapeDtypeStruct((b, HEADS * GDN_D), BF16),
                   jax.ShapeDtypeStruct((b, HEADS * ML_DV), BF16),
                   jax.ShapeDtypeStruct((b, HEADS, GDN_D, GDN_D), F32),
                   jax.ShapeDtypeStruct((b, hp, 128, 128), F32),
                   jax.ShapeDtypeStruct((b, HEADS * ML_DK), F32),
                   jax.ShapeDtypeStruct((b, 128), F32)],
        scratch_shapes=[pltpu.VMEM((HEADS, bb, 128), F32), pltpu.VMEM((HEADS, bb, 128), F32)],
        compiler_params=pltpu.CompilerParams(dimension_semantics=("arbitrary",),
                                             vmem_limit_bytes=VMEM_LIMIT),
        name="decode_step",
    )(proj, gates, cs, s0, c0, n0, m0, conv_w, plane, gng, gnm)


def _lane_vec(pairs):
    v = jnp.zeros((128,), F32)
    for off, val in pairs:
        v = v.at[off:off + HEADS].set(val.astype(F32))
    return v


def _prep_params(norm_pre_mix, w_in, conv_w, a_log, dt_bias, gdn_norm_g, b_igate, b_fgate, mlstm_norm_g,
                 w_out, norm_post_mix, norm_pre_mlp, w_up, w_down, norm_post_mlp):
    alog_v = _lane_vec([(LANE_GA, a_log[0])])
    dtb_v = _lane_vec([(LANE_GA, dt_bias[0])])
    bias_v = _lane_vec([(LANE_MI, b_igate[0]), (LANE_MF, b_fgate[0])])
    return dict(
        w_in=w_in[0], wo=w_out[0].astype(BF16), wu=w_up[0].astype(BF16),
        wd=w_down[0].astype(BF16),
        g_pre=norm_pre_mix[0][None, :], g1=norm_post_mix[0][None, :], g2=norm_pre_mlp[0][None, :],
        g3=norm_post_mlp[0][None, :], cw=conv_w[0], gng=gdn_norm_g[0][None, :], gnm=mlstm_norm_g[0][None, :],
        plane=jnp.stack([alog_v, dtb_v, bias_v]),
        psub=jnp.stack([alog_v[:16], dtb_v[:16], bias_v[:16]], axis=1),
    )


def _prompt_path(x, conv0, s0, c0, n0, m0, prm, tb, tm):
    bsz, seq, d = x.shape
    hp = HEADS // 2
    x2d = x.reshape(bsz * seq, d)
    proj, gates, gates_t = _in_proj(x2d, prm["g_pre"], prm["w_in"], tm=tm, out_dtype=BF16)
    proj = proj.reshape(bsz, seq, COL_GATE)
    gates = gates.reshape(bsz, seq, 128)
    gates_t = gates_t.reshape(bsz, seq // CHUNK, 16, CHUNK)
    cs8 =jnp.pad(conv0, ((0, 0), (8 - (CONV_W - 1), 0), (0, 0)))
    y, s_new, c_new, n_new, m_new = _layer_prompt(
        proj, gates, gates_t, cs8, s0, c0.reshape(bsz, hp, 128, 128), n0.reshape(bsz, hp, 1, 128),
        jnp.pad(m0, ((0, 0), (0, 128 - HEADS)))[:, None, :], x2d, prm["cw"], prm["plane"], prm["psub"],
        prm["gng"], prm["gnm"], prm["wo"], prm["g1"], prm["g2"], prm["wu"], prm["wd"], prm["g3"], tb=tb)
    y = y.reshape(bsz, seq, d)
    xp = jnp.concatenate([conv0, proj[:, seq - (CONV_W - 1):, COL_QKV:COL_QKV + CONV_CH].astype(F32)], axis=1)
    conv_new = xp[:, -(CONV_W - 1):]
    return y, (conv_new, s_new, c_new.reshape(bsz, HEADS, ML_DK, ML_DV), n_new.reshape(bsz, HEADS, ML_DK),
               m_new[:, 0, :HEADS])


def _sample_path(x, conv0, s0, c0, n0, m0, prm):
    dec, _, d = x.shape
    hp = HEADS // 2
    xs = x.reshape(dec, d)
    proj, gates, _ = _in_proj(xs, prm["g_pre"], prm["w_in"], tm=dec, out_dtype=F32)
    og, om, s_new, c_new, n_new, m_new = _decode(
        proj, gates, conv0, s0, c0.reshape(dec, hp, 128, 128), n0.reshape(dec, HEADS * ML_DK),
        jnp.pad(m0, ((0, 0), (0, 128 - HEADS))), prm["cw"], prm["plane"], prm["gng"], prm["gnm"], bb=8)
    y = _out_mlp(og, om, xs, prm["wo"], prm["g1"], prm["g2"], prm["wu"], prm["wd"], prm["g3"],
                 tm=dec).reshape(dec, 1, d)
    conv_new = jnp.concatenate([conv0[:, 1:, :], proj[:, None, COL_QKV:COL_QKV + CONV_CH]], axis=1)
    return y, (conv_new, s_new, c_new.reshape(dec, HEADS, ML_DK, ML_DV), n_new.reshape(dec, HEADS, ML_DK),
               m_new[:, :HEADS])


def kernel(x_prompt, x_sample, state_gdn_conv, state_gdn_S, state_mlstm_C, state_mlstm_n, state_mlstm_m,
           norm_pre_mix, w_in, conv_w, a_log, dt_bias, gdn_norm_g, b_igate, b_fgate, mlstm_norm_g, w_out,
           norm_post_mix, norm_pre_mlp, w_up, w_down, norm_post_mlp):
    bsz = x_prompt.shape[0]
    prm = _prep_params(norm_pre_mix, w_in, conv_w, a_log, dt_bias, gdn_norm_g, b_igate, b_fgate, mlstm_norm_g,
                       w_out, norm_post_mix, norm_pre_mlp, w_up, w_down, norm_post_mlp)
    y_p, p_st = _prompt_path(
        x_prompt, jnp.zeros((bsz, CONV_W - 1, CONV_CH), F32), jnp.zeros((bsz, HEADS, GDN_D, GDN_D), F32),
        jnp.zeros((bsz, HEADS, ML_DK, ML_DV), F32), jnp.zeros((bsz, HEADS, ML_DK), F32),
        jnp.zeros((bsz, HEADS), F32), prm, tb=256, tm=512)
    y_s, s_st = _sample_path(x_sample, state_gdn_conv[0], state_gdn_S[0], state_mlstm_C[0], state_mlstm_n[0],
                             state_mlstm_m[0], prm)
    return (y_p, y_s) + tuple(a[None] for a in p_st) + tuple(a[None] for a in s_st)
```

```python
import functools

import jax
import jax.numpy as jnp
from jax import lax
from jax.experimental import pallas as pl
from jax.experimental.pallas import tpu as pltpu

F32 = jnp.float32
BF16 = jnp.bfloat16
EPS = 1e-6

D_MODEL = 1024
HEADS = 4
GDN_D = 128
ML_DK = 64
ML_DV = 128
CONV_W = 4
CONV_CH = 3 * HEADS * GDN_D
D_FF = 4 * D_MODEL
CHUNK = 64

COL_QKV = 0
COL_Z = 1536
COL_MV = 2048
COL_MO = 2560
COL_MQ = 3072
COL_MK = 3328
COL_GATE = 3584
N_PROJ = COL_GATE + 128
LANE_GB, LANE_GA, LANE_MI, LANE_MF = 0, 4, 8, 12

VMEM_LIMIT = 56 * 1024 * 1024


def _rms(x, g):
    return x * lax.rsqrt(jnp.mean(x * x, axis=-1, keepdims=True) + EPS) * g


def _softplus(x):
    return jnp.maximum(x, 0.0) + jnp.log1p(jnp.exp(-jnp.abs(x)))


def _sigmoid(x):
    return 1.0 / (1.0 + jnp.exp(-x))


def _mm(a, b):
    return jnp.dot(a.astype(BF16), b.astype(BF16), preferred_element_type=F32)


def _mm_nt(a, b):
    return lax.dot_general(a.astype(BF16), b.astype(BF16), (((1,), (1,)), ((), ())),
                           preferred_element_type=F32)


W_IN_COLS = 3600
W_IN_SPANS = ((0, COL_QKV, 2048),
              (2568, COL_MV, 1024),
              (2056, COL_MQ, 512),
              (2048, COL_GATE, 8),
              (3592, COL_GATE + 8, 8))


def _in_proj_kernel(x_ref, g_ref, win_ref, o_ref, gc_ref, gt_ref, h_ref, w_ref, *, n_chunk):
    @pl.when(pl.program_id(0) == 0)
    def _():
        w_ref[COL_GATE:N_PROJ, :] = jnp.zeros((N_PROJ - COL_GATE, w_ref.shape[1]), BF16)
        for src, dst, width in W_IN_SPANS:
            for r0 in range(0, width, 512):
                r1 = min(r0 + 512, width)
                w_ref[dst + r0:dst + r1, :] = win_ref[src + r0:src + r1, :].astype(BF16)

    h_ref[...] = _rms(x_ref[...], g_ref[...]).astype(BF16)
    n = o_ref.shape[1]
    for c0 in range(0, n, n_chunk):
        c1 = min(c0 + n_chunk, n)
        o_ref[:, c0:c1] = _mm_nt(h_ref[...], w_ref[c0:c1, :]).astype(o_ref.dtype)
    gates = _mm_nt(h_ref[...], w_ref[COL_GATE:COL_GATE + 128, :])
    gc_ref[...] = gates
    gt = gates.T
    for c in range(gt_ref.shape[0]):
        gt_ref[c] = gt[0:16, c * CHUNK:(c + 1) * CHUNK]


def _in_proj(x2d, g, w, tm, out_dtype):
    m, k = x2d.shape
    return pl.pallas_call(
        functools.partial(_in_proj_kernel, n_chunk=512),
        grid=(m // tm,),
        in_specs=[pl.BlockSpec((tm, k), lambda i: (i, 0)),
                  pl.BlockSpec((1, k), lambda i: (0, 0)),
                  pl.BlockSpec((W_IN_COLS, k), lambda i: (0, 0), pipeline_mode=pl.Buffered(1))],
        out_specs=[pl.BlockSpec((tm, COL_GATE), lambda i: (i, 0)),
                   pl.BlockSpec((tm, 128), lambda i: (i, 0)),
                   pl.BlockSpec((tm // CHUNK, 16, CHUNK), lambda i: (i, 0, 0))],
        out_shape=[jax.ShapeDtypeStruct((m, COL_GATE), out_dtype),
                   jax.ShapeDtypeStruct((m, 128), F32),
                   jax.ShapeDtypeStruct((m // CHUNK, 16, CHUNK), F32)],
        scratch_shapes=[pltpu.VMEM((tm, k), BF16), pltpu.VMEM((N_PROJ, k), BF16)],
        compiler_params=pltpu.CompilerParams(dimension_semantics=("arbitrary",),
                                             vmem_limit_bytes=VMEM_LIMIT),
        name="in_proj",
    )(x2d, g, w)


def _out_mlp_kernel(og_ref, om_ref, x_ref, wo_ref, g1_ref, g2_ref, wu_ref, wd_ref, g3_ref,
                    y_ref, hn_ref, acc_ref, *, ff_chunk):
    half = og_ref.shape[1]
    mix = (jnp.dot(og_ref[...], wo_ref[0:half, :], preferred_element_type=F32)
           + jnp.dot(om_ref[...], wo_ref[half:2 * half, :], preferred_element_type=F32))
    x1 = x_ref[...] + _rms(mix, g1_ref[...])
    y_ref[...] = x1
    hn_ref[...] = _rms(x1, g2_ref[...]).astype(BF16)
    d_ff = wu_ref.shape[1]
    for c0 in range(0, d_ff, ff_chunk):
        u = jnp.dot(hn_ref[...], wu_ref[:, c0:c0 + ff_chunk], preferred_element_type=F32)
        u = jnp.square(jnp.maximum(u, 0.0)).astype(BF16)
        d = jnp.dot(u, wd_ref[c0:c0 + ff_chunk, :], preferred_element_type=F32)
        if c0 == 0:
            acc_ref[...] = d
        else:
            acc_ref[...] += d
    y_ref[...] = y_ref[...] + _rms(acc_ref[...], g3_ref[...])


def _out_mlp(og, om, x2d, wo, g1, g2, wu, wd, g3, tm):
    m, d = x2d.shape
    half = og.shape[1]
    d_ff = wu.shape[1]
    const = lambda i: (0, 0)
    return pl.pallas_call(
        functools.partial(_out_mlp_kernel, ff_chunk=1024),
        grid=(m // tm,),
        in_specs=[pl.BlockSpec((tm, half), lambda i: (i, 0)),
                  pl.BlockSpec((tm, half), lambda i: (i, 0)),
                  pl.BlockSpec((tm, d), lambda i: (i, 0)),
                  pl.BlockSpec((d, d), const, pipeline_mode=pl.Buffered(1)),
                  pl.BlockSpec((1, d), const),
                  pl.BlockSpec((1, d), const),
                  pl.BlockSpec((d, d_ff), const, pipeline_mode=pl.Buffered(1)),
                  pl.BlockSpec((d_ff, d), const, pipeline_mode=pl.Buffered(1)),
                  pl.BlockSpec((1, d), const)],
        out_specs=pl.BlockSpec((tm, d), lambda i: (i, 0)),
        out_shape=jax.ShapeDtypeStruct((m, d), F32),
        scratch_shapes=[pltpu.VMEM((tm, d), BF16), pltpu.VMEM((tm, d), F32)],
        compiler_params=pltpu.CompilerParams(dimension_semantics=("arbitrary",),
                                             vmem_limit_bytes=VMEM_LIMIT),
        name="out_mlp",
    )(og, om, x2d, wo, g1, g2, wu, wd, g3)


def _chunk_masks():
    ii = lax.broadcasted_iota(jnp.int32, (CHUNK, CHUNK), 0)
    jj = lax.broadcasted_iota(jnp.int32, (CHUNK, CHUNK), 1)
    return ii, jj


def _cumsum_col_row(x_col, x_row, ii, jj):
    c_col = jnp.sum(jnp.where(jj <= ii, x_row, 0.0), axis=1, keepdims=True)
    c_row = jnp.sum(jnp.where(ii <= jj, x_col, 0.0), axis=0, keepdims=True)
    return c_col, c_row


def _interleave(*tasks):
    live = [[g, n, 0] for g, n in tasks]
    while live:
        entry = min(live, key=lambda e: (e[2] + 1) / e[1])
        try:
            next(entry[0])
            entry[2] += 1
        except StopIteration:
            live.remove(entry)


def _halves(items):
    mid = len(items) // 2
    return items[:mid], items[mid:]


def _gdn_phase_a(yc_ref, gcol_ref, grow_ref, plane_ref, psub_ref, wv_ref, lhs_ref, qk_ref, ket_ref, ge_ref,
                 ncb):
    ii, jj = _chunk_masks()
    incl = jj <= ii
    strict = jj < ii
    eye = (ii == jj).astype(F32)
    neg_a_lane = -jnp.exp(plane_ref[0:1, :])
    dtb_lane = plane_ref[1:2, :]
    neg_a_sub = -jnp.exp(psub_ref[:, 0:1])
    dtb_sub = psub_ref[:, 1:2]
    items = []
    for c in range(ncb):
        rows = slice(c * CHUNK, (c + 1) * CHUNK)
        gc = gcol_ref[rows, :]
        gr = grow_ref[c]
        beta_t = _sigmoid(gc)
        g_t = neg_a_lane * _softplus(gc + dtb_lane)
        g_r = neg_a_sub * _softplus(gr + dtb_sub)
        for h in range(HEADS):
            lo = h * GDN_D
            items.append(dict(c=c, h=h,
                              q=yc_ref[rows, lo:lo + GDN_D],
                              k=yc_ref[rows, 512 + lo:512 + lo + GDN_D],
                              v=yc_ref[rows, 1024 + lo:1024 + lo + GDN_D],
                              beta=beta_t[:, LANE_GB + h:LANE_GB + h + 1],
                              gg_col=g_t[:, LANE_GA + h:LANE_GA + h + 1],
                              gg_row=g_r[LANE_GA + h:LANE_GA + h + 1, :]))
    yield
    for it in items:
        it["k"] = it["k"] * lax.rsqrt(jnp.sum(it["k"] * it["k"], axis=-1, keepdims=True) + EPS)
    yield
    for it in items:
        it["kk"] = _mm_nt(it["k"], it["k"])
    yield
    for it in items:
        it["q"] = (it["q"] * lax.rsqrt(jnp.sum(it["q"] * it["q"], axis=-1, keepdims=True) + EPS)
                   * (GDN_D ** -0.5))
    yield
    for it in items:
        it["g_col"], g_row = _cumsum_col_row(it["gg_col"], it["gg_row"], ii, jj)
        it["decay"] = jnp.where(incl, jnp.exp(jnp.where(incl, it["g_col"] - g_row, 0.0)), 0.0)
    yield
    for it in items:
        n_mat = jnp.where(strict, it["beta"] * it["kk"] * it["decay"], 0.0)
        it["x"] = eye - n_mat
        it["p"] = -n_mat
    yield
    for _ in range(5):
        yield
        for it in items:
            it["p"] = _mm(it["p"], it["p"])
        yield
        for it in items:
            it["x"] = it["x"] + _mm(it["x"], it["p"])
    yield
    for it in items:
        e_g = jnp.exp(it["g_col"])
        it["e_g"] = e_g
        rhs = jnp.concatenate([it["beta"] * it["v"], (it["beta"] * e_g) * it["k"]], axis=1)
        it["w"] = _mm(it["x"], rhs)
    yield
    for it in items:
        it["qk"] = _mm_nt(it["q"], it["k"]) * it["decay"]
    yield
    for it in items:
        c, h = it["c"], it["h"]
        g_end = it["g_col"][CHUNK - 1:CHUNK, :]
        k_end = it["k"] * jnp.exp(g_end - it["g_col"])
        wv_ref[c, h] = it["w"][:, 0:GDN_D]
        lhs_ref[c, h] = jnp.concatenate([it["w"][:, GDN_D:2 * GDN_D], it["e_g"] * it["q"]],
                                        axis=0).astype(BF16)
        qk_ref[c, h] = it["qk"].astype(BF16)
        ket_ref[c, h] = k_end.T.astype(BF16)
        ge_ref[c, h] = jnp.broadcast_to(jnp.exp(g_end), (1, GDN_D))


def _gdn_phase_b(qkvz_ref, gn_ref, og_ref, s_ref, wv_ref, lhs_ref, qk_ref, ket_ref, ge_ref, ncb):
    gn = gn_ref[...]

    def epilogue(c, o):
        rows = slice(c * CHUNK, (c + 1) * CHUNK)
        for h in range(HEADS):
            lo = h * GDN_D
            z = qkvz_ref[rows, COL_Z + lo:COL_Z + lo + GDN_D].astype(F32)
            out = _rms(o[h], gn) * (z * _sigmoid(z))
            og_ref[rows, lo:lo + GDN_D] = out.astype(og_ref.dtype)

    s = [s_ref[h] for h in range(HEADS)]
    o_prev = None
    for c in range(ncb):
        r = [jnp.dot(lhs_ref[c, h], s[h].astype(BF16), preferred_element_type=F32) for h in range(HEADS)]
        yield
        if o_prev is not None:
            epilogue(c - 1, o_prev)
        ub = [(wv_ref[c, h] - r[h][0:CHUNK]).astype(BF16) for h in range(HEADS)]
        s = [ge_ref[c, h] * s[h] + jnp.dot(ket_ref[c, h], ub[h], preferred_element_type=F32)
             for h in range(HEADS)]
        o_prev = [r[h][CHUNK:2 * CHUNK] + jnp.dot(qk_ref[c, h], ub[h], preferred_element_type=F32)
                  for h in range(HEADS)]
        yield
    epilogue(ncb - 1, o_prev)
    for h in range(HEADS):
        s_ref[h] = s[h]


def _mlstm_block(mvo_ref, mqk_ref, gcol_ref, grow_ref, plane_ref, psub_ref, gn_ref, om_ref,
                 c_ref, n_ref, m_ref, ncb):
    ii, jj = _chunk_masks()
    incl = jj <= ii
    lane = lax.broadcasted_iota(jnp.int32, (1, 128), 1)
    row128 = lax.broadcasted_iota(jnp.int32, (128, 1), 0)
    gn = gn_ref[...]
    blane = plane_ref[2:3, :]
    bsub = psub_ref[:, 2:3]
    items = []
    for c in range(ncb):
        rows = slice(c * CHUNK, (c + 1) * CHUNK)
        gc = gcol_ref[rows, :] + blane
        gr = grow_ref[c] + bsub
        lf_c = -_softplus(-gc)
        lf_r = -_softplus(-gr)
        for p in range(HEADS // 2):
            qb = mqk_ref[rows, p * 128:(p + 1) * 128].astype(F32)
            kb = mqk_ref[rows, 256 + p * 128:256 + (p + 1) * 128].astype(F32) * (ML_DK ** -0.5)
            for e in range(2):
                h = 2 * p + e
                lm = (lane >= e * ML_DK) & (lane < (e + 1) * ML_DK)
                items.append(dict(c=c, rows=rows, p=p, e=e, h=h, qh=jnp.where(lm, qb, 0.0), kb=kb,
                                  kh=jnp.where(lm, kb, 0.0),
                                  lf_col=lf_c[:, LANE_MF + h:LANE_MF + h + 1],
                                  lf_row=lf_r[LANE_MF + h:LANE_MF + h + 1, :],
                                  ig_row=gr[LANE_MI + h:LANE_MI + h + 1, :],
                                  ig_col=gc[:, LANE_MI + h:LANE_MI + h + 1],
                                  v=mvo_ref[rows, h * ML_DV:(h + 1) * ML_DV]))
        if c % 2 == 1:
            yield
    for part in _halves(items):
        for it in part:
            it["qk"] = _mm_nt(it["qh"], it["kb"])
        yield
    for part in _halves(items):
        for it in part:
            it["f_col"], it["f_row"] = _cumsum_col_row(it["lf_col"], it["lf_row"], ii, jj)
        yield
    for part in _halves(items):
        for it in part:
            it["d_mat"] = jnp.where(incl, it["f_col"] - it["f_row"] + it["ig_row"], -jnp.inf)
            it["d_max"] = jnp.max(it["d_mat"], axis=1, keepdims=True)
        yield
    for part in _halves(items):
        for it in part:
            p0 = jnp.where(incl, jnp.exp(jnp.where(incl, it["d_mat"] - it["d_max"], 0.0)), 0.0)
            pend0 = jnp.exp(it["f_col"][CHUNK - 1:CHUNK, :] - it["f_col"] + it["ig_col"]
                            - it["d_max"][CHUNK - 1:CHUNK, :])
            it["kp0"] = it["kh"] * pend0
            it["pqk0"] = p0 * it["qk"]
        yield
    for part in _halves(items):
        for it in part:
            it["pv0"] = _mm(it["pqk0"], it["v"])
            it["rs0"] = jnp.sum(it["pqk0"], axis=-1, keepdims=True)
        yield
    for part in _halves(items):
        for it in part:
            it["cadd0"] = _mm(it["kp0"].T, it["v"])
            it["nadd0"] = jnp.sum(it["kp0"], axis=0, keepdims=True)
        yield

    m_cur = [m_ref[:, h:h + 1] for h in range(HEADS)]
    for it in items:
        h = it["h"]
        it["m_prev"] = m_cur[h]
        m_cur[h] = jnp.maximum(it["f_col"][CHUNK - 1:CHUNK, :] + m_cur[h], it["d_max"][CHUNK - 1:CHUNK, :])
    yield
    for part in _halves(items):
        for it in part:
            bcol = it["f_col"] + it["m_prev"]
            mt = jnp.maximum(bcol, it["d_max"])
            it["mt"] = mt
            it["w_prev"] = jnp.exp(bcol - mt)
            it["sc"] = jnp.exp(it["d_max"] - mt)
        yield
    c_cur = [c_ref[p] for p in range(HEADS // 2)]
    n_cur = [n_ref[p] for p in range(HEADS // 2)]
    for c in range(ncb):
        for p in range(HEADS // 2):
            pair = [it for it in items if it["c"] == c and it["p"] == p]
            w_end = [it["w_prev"][CHUNK - 1:CHUNK, :] for it in pair]
            s_end = [it["sc"][CHUNK - 1:CHUNK, :] for it in pair]
            for it in pair:
                it["c_prev"] = c_cur[p]
                it["n_prev"] = n_cur[p]
            c_cur[p] = (jnp.where(row128 < ML_DK, w_end[0], w_end[1]) * c_cur[p]
                        + s_end[0] * pair[0]["cadd0"] + s_end[1] * pair[1]["cadd0"])
            n_cur[p] = (jnp.where(lane < ML_DK, w_end[0], w_end[1]) * n_cur[p]
                        + s_end[0] * pair[0]["nadd0"] + s_end[1] * pair[1]["nadd0"])
        if c % 2 == 1:
            yield
    for part in _halves(items):
        for it in part:
            it["qc"] = _mm(it["qh"], it["c_prev"])
        yield
    for part in _halves(items):
        for it in part:
            num = it["w_prev"] * it["qc"] + it["sc"] * it["pv0"]
            den = (it["w_prev"] * jnp.sum(it["qh"] * it["n_prev"], axis=-1, keepdims=True)
                   + it["sc"] * it["rs0"])
            it["hh"] = num / jnp.maximum(jnp.abs(den), jnp.exp(-it["mt"]))
        yield
    for part in _halves(items):
        for it in part:
            h = it["h"]
            mo = mvo_ref[it["rows"], 512 + h * ML_DV:512 + (h + 1) * ML_DV].astype(F32)
            out = _sigmoid(mo) * _rms(it["hh"], gn)
            om_ref[it["rows"], h * ML_DV:(h + 1) * ML_DV] = out.astype(om_ref.dtype)
        yield
    for p in range(HEADS // 2):
        c_ref[p] = c_cur[p]
        n_ref[p] = n_cur[p]
    for h in range(HEADS):
        m_ref[:, h:h + 1] = m_cur[h]


def _chain(*gens):
    for g in gens:
        yield from g


def _conv_stage(xp_ref, yc_ref, cw_ref, tb):
    for ct in range(CONV_CH // 128):
        cols = slice(ct * 128, (ct + 1) * 128)
        w = cw_ref[:, cols]
        for r0 in range(0, tb, 128):
            acc = xp_ref[5 + r0:5 + r0 + 128, cols] * w[0:1, :]
            for j in range(1, CONV_W):
                acc = acc + xp_ref[5 + j + r0:5 + j + r0 + 128, cols] * w[j:j + 1, :]
            yc_ref[r0:r0 + 128, cols] = acc * _sigmoid(acc)
        yield


def _out_mlp_block(og_ref, om_ref, x_ref, wo_ref, g1_ref, g2_ref, wu_ref, wd_ref, g3_ref,
                   y_ref, hn_ref, acc_ref, ff_chunk, n_split):
    half = og_ref.shape[1]
    d = x_ref.shape[1]
    d_ff = wu_ref.shape[1]
    col_groups = [slice(c, c + d // n_split) for c in range(0, d, d // n_split)]
    for cg in col_groups:
        acc_ref[:, cg] = (jnp.dot(og_ref[...], wo_ref[0:half, cg], preferred_element_type=F32)
                          + jnp.dot(om_ref[...], wo_ref[half:2 * half, cg], preferred_element_type=F32))
        yield
    x1 = x_ref[...] + _rms(acc_ref[...], g1_ref[...])
    y_ref[...] = x1
    hn_ref[...] = _rms(x1, g2_ref[...]).astype(BF16)
    yield
    for c0 in range(0, d_ff, ff_chunk):
        u = jnp.dot(hn_ref[...], wu_ref[:, c0:c0 + ff_chunk], preferred_element_type=F32)
        u = jnp.square(jnp.maximum(u, 0.0)).astype(BF16)
        for cg in col_groups:
            dd = jnp.dot(u, wd_ref[c0:c0 + ff_chunk, cg], preferred_element_type=F32)
            if c0 == 0:
                acc_ref[:, cg] = dd
            else:
                acc_ref[:, cg] += dd
        yield
    y_ref[...] = y_ref[...] + _rms(acc_ref[...], g3_ref[...])
    yield


def _layer_kernel(qkvz_ref, mvo_ref, mqk_ref, gcol_ref, grow_ref, cs_ref, s0_ref, c0_ref, n0_ref, m0_ref,
                  x_ref, cw_ref, plane_ref, psub_ref, gng_ref, gnm_ref,
                  wo_ref, g1_ref, g2_ref, wu_ref, wd_ref, g3_ref,
                  y_ref, sout_ref, cout_ref, nout_ref, mout_ref,
                  xp_ref, yc_ref, s_ref, wv_ref, lhs_ref, qk_ref, ket_ref, ge_ref, c_ref, n_ref, m_ref,
                  og_ref, om_ref, hn_ref, acc_ref, *, tb, nt, n_blocks):
    g = pl.program_id(0)
    t = jnp.minimum(g, n_blocks - 1) % nt
    ncb = tb // CHUNK
    par = g % 2

    @pl.when(g == 0)
    def _():
        og_ref[1] = jnp.zeros(og_ref.shape[1:], og_ref.dtype)
        om_ref[1] = jnp.zeros(om_ref.shape[1:], om_ref.dtype)

    @pl.when(t == 0)
    def _():
        xp_ref[0:8, :] = cs_ref[...]
        s_ref[...] = s0_ref[...]
        c_ref[...] = c0_ref[...]
        n_ref[...] = n0_ref[...]
        m_ref[...] = m0_ref[...]

    @pl.when(t > 0)
    def _():
        xp_ref[0:8, :] = xp_ref[tb:tb + 8, :]

    xp_ref[8:tb + 8, :] = qkvz_ref[:, COL_QKV:COL_QKV + CONV_CH].astype(F32)

    og_w, om_w = og_ref.at[par], om_ref.at[par]
    og_r, om_r = og_ref.at[1 - par], om_ref.at[1 - par]
    _interleave(
        (_chain(_conv_stage(xp_ref, yc_ref, cw_ref, tb),
                _gdn_phase_a(yc_ref, gcol_ref, grow_ref, plane_ref, psub_ref, wv_ref, lhs_ref, qk_ref, ket_ref,
                             ge_ref, ncb),
                _gdn_phase_b(qkvz_ref, gng_ref, og_w, s_ref, wv_ref, lhs_ref, qk_ref, ket_ref, ge_ref, ncb)),
         CONV_CH // 128 + 20 + 2 * ncb),
        (_out_mlp_block(og_r, om_r, x_ref, wo_ref, g1_ref, g2_ref, wu_ref, wd_ref, g3_ref, y_ref, hn_ref, acc_ref,
                        ff_chunk=256, n_split=2), 2 + 1 + wu_ref.shape[1] // 256 + 1),
        (_mlstm_block(mvo_ref, mqk_ref, gcol_ref, grow_ref, plane_ref, psub_ref, gnm_ref, om_w,
                      c_ref, n_ref, m_ref, ncb), 21 + ncb))

    @pl.when((t == nt - 1) & (g < n_blocks))
    def _():
        sout_ref[...] = s_ref[...]
        cout_ref[...] = c_ref[...]
        nout_ref[...] = n_ref[...]
        mout_ref[...] = m_ref[...]


def _layer_prompt(proj, gates, gates_t, cs8, s0, c0, n0, m0, x2d, conv_w, plane, psub, gng, gnm,
                  wo, g1, g2, wu, wd, g3, tb):
    b, t, _ = proj.shape
    d = x2d.shape[1]
    d_ff = wu.shape[1]
    nt = t // tb
    n_blocks = b * nt
    ncb = tb // CHUNK
    hp = HEADS // 2

    def blk(g):
        gm = jnp.minimum(g, n_blocks - 1)
        return gm // nt, gm % nt

    def tok3(col):
        return lambda g: blk(g) + (col,)

    per_seq3 = lambda g: (blk(g)[0], 0, 0)
    per_seq4 = lambda g: (blk(g)[0], 0, 0, 0)
    prev_rows = lambda g: (jnp.maximum(g - 1, 0), 0)
    const = lambda g: (0, 0)
    resident = dict(pipeline_mode=pl.Buffered(1))
    return pl.pallas_call(
        functools.partial(_layer_kernel, tb=tb, nt=nt, n_blocks=n_blocks),
        grid=(n_blocks + 1,),
        in_specs=[pl.BlockSpec((None, tb, 2048), tok3(0)),
                  pl.BlockSpec((None, tb, 1024), tok3(COL_MV // 1024)),
                  pl.BlockSpec((None, tb, 512), tok3(COL_MQ // 512)),
                  pl.BlockSpec((None, tb, 128), tok3(0)),
                  pl.BlockSpec((None, ncb, 16, CHUNK), lambda g: blk(g) + (0, 0)),
                  pl.BlockSpec((None, 8, CONV_CH), per_seq3),
                  pl.BlockSpec((None, HEADS, GDN_D, GDN_D), per_seq4),
                  pl.BlockSpec((None, hp, 128, 128), per_seq4),
                  pl.BlockSpec((None, hp, 1, 128), per_seq4),
                  pl.BlockSpec((None, 1, 128), per_seq3),
                  pl.BlockSpec((tb, d), prev_rows),
                  pl.BlockSpec((CONV_W, CONV_CH), const),
                  pl.BlockSpec((3, 128), const),
                  pl.BlockSpec((16, 3), const),
                  pl.BlockSpec((1, GDN_D), const),
                  pl.BlockSpec((1, ML_DV), const),
                  pl.BlockSpec((d, d), const, **resident),
                  pl.BlockSpec((1, d), const),
                  pl.BlockSpec((1, d), const),
                  pl.BlockSpec((d, d_ff), const, **resident),
                  pl.BlockSpec((d_ff, d), const, **resident),
                  pl.BlockSpec((1, d), const)],
        out_specs=[pl.BlockSpec((tb, d), prev_rows),
                   pl.BlockSpec((None, HEADS, GDN_D, GDN_D), per_seq4),
                   pl.BlockSpec((None, hp, 128, 128), per_seq4),
                   pl.BlockSpec((None, hp, 1, 128), per_seq4),
                   pl.BlockSpec((None, 1, 128), per_seq3)],
        out_shape=[jax.ShapeDtypeStruct((b * t, d), F32),
                   jax.ShapeDtypeStruct((b, HEADS, GDN_D, GDN_D), F32),
                   jax.ShapeDtypeStruct((b, hp, 128, 128), F32),
                   jax.ShapeDtypeStruct((b, hp, 1, 128), F32),
                   jax.ShapeDtypeStruct((b, 1, 128), F32)],
        scratch_shapes=[pltpu.VMEM((tb + 8, CONV_CH), F32),
                        pltpu.VMEM((tb, CONV_CH), F32),
                        pltpu.VMEM((HEADS, GDN_D, GDN_D), F32),
                        pltpu.VMEM((ncb, HEADS, CHUNK, GDN_D), F32),
                        pltpu.VMEM((ncb, HEADS, 2 * CHUNK, GDN_D), BF16),
                        pltpu.VMEM((ncb, HEADS, CHUNK, CHUNK), BF16),
                        pltpu.VMEM((ncb, HEADS, GDN_D, CHUNK), BF16),
                        pltpu.VMEM((ncb, HEADS, 1, GDN_D), F32),
                        pltpu.VMEM((hp, 128, 128), F32),
                        pltpu.VMEM((hp, 1, 128), F32),
                        pltpu.VMEM((1, 128), F32),
                        pltpu.VMEM((2, tb, HEADS * GDN_D), BF16),
                        pltpu.VMEM((2, tb, HEADS * ML_DV), BF16),
                        pltpu.VMEM((tb, d), BF16),
                        pltpu.VMEM((tb, d), F32)],
        compiler_params=pltpu.CompilerParams(dimension_semantics=("arbitrary",),
                                             vmem_limit_bytes=VMEM_LIMIT),
        name="layer_prompt",
    )(proj, proj, proj, gates, gates_t, cs8, s0, c0, n0, m0, x2d, conv_w, plane, psub, gng, gnm,
      wo, g1, g2, wu, wd, g3)


def _decode_kernel(proj_ref, gates_ref, cs_ref, s0_ref, c0_ref, n0_ref, m0_ref, cw_ref, plane_ref, gng_ref, gnm_ref,
                   og_ref, om_ref, sout_ref, cout_ref, nout_ref, mout_ref, rowg_ref, rowm_ref, *, bb):
    lane = lax.broadcasted_iota(jnp.int32, (1, 128), 1)
    gt = gates_ref[...]
    beta_t = _sigmoid(gt)
    g_t = -jnp.exp(plane_ref[0:1, :]) * _softplus(gt + plane_ref[1:2, :])
    eg_t = jnp.exp(g_t)
    gb_t = gt + plane_ref[2:3, :]
    lf_t = -_softplus(-gb_t)

    heads = []
    for h in range(HEADS):
        cols = []
        for part in range(3):
            c0 = part * 512 + h * GDN_D
            w = cw_ref[:, c0:c0 + GDN_D]
            acc = cs_ref[:, 0, c0:c0 + GDN_D] * w[0:1, :]
            acc = acc + cs_ref[:, 1, c0:c0 + GDN_D] * w[1:2, :]
            acc = acc + cs_ref[:, 2, c0:c0 + GDN_D] * w[2:3, :]
            acc = acc + proj_ref[:, COL_QKV + c0:COL_QKV + c0 + GDN_D] * w[3:4, :]
            cols.append(acc * _sigmoid(acc))
        q, k, v = cols
        q = q * lax.rsqrt(jnp.sum(q * q, axis=-1, keepdims=True) + EPS) * (GDN_D ** -0.5)
        k = k * lax.rsqrt(jnp.sum(k * k, axis=-1, keepdims=True) + EPS)
        heads.append(dict(v=v, qk=jnp.sum(q * k, axis=-1, keepdims=True), k_t=k.T,
                          kq=jnp.concatenate([k, q], axis=0).astype(BF16),
                          beta=beta_t[:, LANE_GB + h:LANE_GB + h + 1],
                          eg=eg_t[:, LANE_GA + h:LANE_GA + h + 1]))
    items = [dict(h=h, b=b) for h in range(HEADS) for b in range(bb)]
    for it in items:
        hd, b = heads[it["h"]], it["b"]
        it["k_col"] = hd["k_t"][:, b:b + 1]
    for it in items:
        r = jnp.dot(heads[it["h"]]["kq"], s0_ref[it["b"], it["h"]].astype(BF16), preferred_element_type=F32)
        it["ks"] = r[it["b"]:it["b"] + 1, :]
        it["qs"] = r[bb + it["b"]:bb + it["b"] + 1, :]
    for it in items:
        hd, b = heads[it["h"]], it["b"]
        eg_b = hd["eg"][b:b + 1, :]
        it["eg_b"] = eg_b
        it["u"] = hd["beta"][b:b + 1, :] * (hd["v"][b:b + 1, :] - eg_b * it["ks"])
        rowg_ref[it["h"], b:b + 1, :] = eg_b * it["qs"] + hd["qk"][b:b + 1, :] * it["u"]
    for it in items:
        sout_ref[it["b"], it["h"]] = it["eg_b"] * s0_ref[it["b"], it["h"]] + it["k_col"] * it["u"]
    for h in range(HEADS):
        z = proj_ref[:, COL_Z + h * GDN_D:COL_Z + (h + 1) * GDN_D]
        out = _rms(rowg_ref[h], gng_ref[...]) * (z * _sigmoid(z))
        og_ref[:, h * GDN_D:(h + 1) * GDN_D] = out.astype(og_ref.dtype)

    heads = []
    for p in range(HEADS // 2):
        qb = proj_ref[:, COL_MQ + p * 128:COL_MQ + (p + 1) * 128]
        kb = proj_ref[:, COL_MK + p * 128:COL_MK + (p + 1) * 128] * (ML_DK ** -0.5)
        n_p = n0_ref[:, p * 128:(p + 1) * 128]
        k_t = kb.T
        qq = jnp.concatenate([jnp.where(lane < ML_DK, qb, 0.0), jnp.where(lane >= ML_DK, qb, 0.0)],
                             axis=0).astype(BF16)
        w_prev, p_in = [], []
        for e in range(2):
            h = 2 * p + e
            lm = (lane >= e * ML_DK) & (lane < (e + 1) * ML_DK)
            ig = gb_t[:, LANE_MI + h:LANE_MI + h + 1]
            lf = lf_t[:, LANE_MF + h:LANE_MF + h + 1]
            m_old = m0_ref[:, h:h + 1]
            m_new = jnp.maximum(lf + m_old, ig)
            w_prev.append(jnp.exp(lf + m_old - m_new))
            p_in.append(jnp.exp(ig - m_new))
            qk = jnp.sum(jnp.where(lm, qb * kb, 0.0), axis=-1, keepdims=True)
            qn = jnp.sum(jnp.where(lm, qb * n_p, 0.0), axis=-1, keepdims=True)
            mout_ref[:, h:h + 1] = m_new
            heads.append(dict(p=p, e=e, qq=qq, k_t=k_t, w_prev=w_prev[e], p_in=p_in[e],
                              pqk=p_in[e] * qk, wqn=w_prev[e] * qn, floor=jnp.exp(-m_new),
                              v=proj_ref[:, COL_MV + h * ML_DV:COL_MV + (h + 1) * ML_DV]))
        lo_lane = lane < ML_DK
        nout_ref[:, p * 128:(p + 1) * 128] = (jnp.where(lo_lane, w_prev[0], w_prev[1]) * n_p
                                              + jnp.where(lo_lane, p_in[0], p_in[1]) * kb)
    items = [dict(h=h, b=b) for h in range(HEADS) for b in range(bb)]
    for it in items:
        hd, b = heads[it["h"]], it["b"]
        rs = slice(hd["e"] * ML_DK, (hd["e"] + 1) * ML_DK)
        it["rs"] = rs
        it["k_col"] = hd["k_t"][rs, b:b + 1]
    qc_pair = {(p, b): jnp.dot(heads[2 * p]["qq"], c0_ref[b, p].astype(BF16), preferred_element_type=F32)
               for p in range(HEADS // 2) for b in range(bb)}
    for it in items:
        hd, b = heads[it["h"]], it["b"]
        row = hd["e"] * bb + b
        it["qc"] = qc_pair[(hd["p"], b)][row:row + 1, :]
    for it in items:
        hd, b = heads[it["h"]], it["b"]
        pqk = hd["pqk"][b:b + 1, :]
        num = hd["w_prev"][b:b + 1, :] * it["qc"] + pqk * hd["v"][b:b + 1, :]
        den = hd["wqn"][b:b + 1, :] + pqk
        rowm_ref[it["h"], b:b + 1, :] = num / jnp.maximum(jnp.abs(den), hd["floor"][b:b + 1, :])
    for it in items:
        hd, b = heads[it["h"]], it["b"]
        cout_ref[b, hd["p"], it["rs"], :] = (hd["w_prev"][b:b + 1, :] * c0_ref[b, hd["p"], it["rs"], :]
                                             + (hd["p_in"][b:b + 1, :] * it["k_col"]) * hd["v"][b:b + 1, :])
    for h in range(HEADS):
        mo = proj_ref[:, COL_MO + h * ML_DV:COL_MO + (h + 1) * ML_DV]
        out = _sigmoid(mo) * _rms(rowm_ref[h], gnm_ref[...])
        om_ref[:, h * ML_DV:(h + 1) * ML_DV] = out.astype(om_ref.dtype)
    mout_ref[:, HEADS:128] = m0_ref[:, HEADS:128]


def _decode(proj, gates, cs, s0, c0, n0, m0, conv_w, plane, gng, gnm, bb):
    b = proj.shape[0]
    hp = HEADS // 2
    return pl.pallas_call(
        functools.partial(_decode_kernel, bb=bb),
        grid=(b // bb,),
        in_specs=[pl.BlockSpec((bb, COL_GATE), lambda i: (i, 0)),
                  pl.BlockSpec((bb, 128), lambda i: (i, 0)),
                  pl.BlockSpec((bb, CONV_W - 1, CONV_CH), lambda i: (i, 0, 0)),
                  pl.BlockSpec((bb, HEADS, GDN_D, GDN_D), lambda i: (i, 0, 0, 0)),
                  pl.BlockSpec((bb, hp, 128, 128), lambda i: (i, 0, 0, 0)),
                  pl.BlockSpec((bb, HEADS * ML_DK), lambda i: (i, 0)),
                  pl.BlockSpec((bb, 128), lambda i: (i, 0)),
                  pl.BlockSpec((CONV_W, CONV_CH), lambda i: (0, 0)),
                  pl.BlockSpec((3, 128), lambda i: (0, 0)),
                  pl.BlockSpec((1, GDN_D), lambda i: (0, 0)),
                  pl.BlockSpec((1, ML_DV), lambda i: (0, 0))],
        out_specs=[pl.BlockSpec((bb, HEADS * GDN_D), lambda i: (i, 0)),
                   pl.BlockSpec((bb, HEADS * ML_DV), lambda i: (i, 0)),
                   pl.BlockSpec((bb, HEADS, GDN_D, GDN_D), lambda i: (i, 0, 0, 0)),
                   pl.BlockSpec((bb, hp, 128, 128), lambda i: (i, 0, 0, 0)),
                   pl.BlockSpec((bb, HEADS * ML_DK), lambda i: (i, 0)),
                   pl.BlockSpec((bb, 128), lambda i: (i, 0))],
        out_shape=[jax.ShapeDtypeStruct((b, HEADS * GDN_D), BF16),
                   jax.ShapeDtypeStruct((b, HEADS * ML_DV), BF16),
                   jax.ShapeDtypeStruct((b, HEADS, GDN_D, GDN_D), F32),
                   jax.ShapeDtypeStruct((b, hp, 128, 128), F32),
                   jax.ShapeDtypeStruct((b, HEADS * ML_DK), F32),
                   jax.ShapeDtypeStruct((b, 128), F32)],
        scratch_shapes=[pltpu.VMEM((HEADS, bb, 128), F32), pltpu.VMEM((HEADS, bb, 128), F32)],
        compiler_params=pltpu.CompilerParams(dimension_semantics=("arbitrary",),
                                             vmem_limit_bytes=VMEM_LIMIT),
        name="decode_step",
    )(proj, gates, cs, s0, c0, n0, m0, conv_w, plane, gng, gnm)


def _lane_vec(pairs):
    v = jnp.zeros((128,), F32)
    for off, val in pairs:
        v = v.at[off:off + HEADS].set(val.astype(F32))
    return v


def _prep_params(norm_pre_mix, w_in, conv_w, a_log, dt_bias, gdn_norm_g, b_igate, b_fgate, mlstm_norm_g,
                 w_out, norm_post_mix, norm_pre_mlp, w_up, w_down, norm_post_mlp):
    alog_v = _lane_vec([(LANE_GA, a_log[0])])
    dtb_v = _lane_vec([(LANE_GA, dt_bias[0])])
    bias_v = _lane_vec([(LANE_MI, b_igate[0]), (LANE_MF, b_fgate[0])])
    return dict(
        w_in=jnp.swapaxes(w_in[0], 0, 1), wo=w_out[0].astype(BF16), wu=w_up[0].astype(BF16),
        wd=w_down[0].astype(BF16),
        g_pre=norm_pre_mix[0][None, :], g1=norm_post_mix[0][None, :], g2=norm_pre_mlp[0][None, :],
        g3=norm_post_mlp[0][None, :], cw=conv_w[0], gng=gdn_norm_g[0][None, :], gnm=mlstm_norm_g[0][None, :],
        plane=jnp.stack([alog_v, dtb_v, bias_v]),
        psub=jnp.stack([alog_v[:16], dtb_v[:16], bias_v[:16]], axis=1),
    )


def _prompt_path(x, conv0, s0, c0, n0, m0, prm, tb, tm):
    bsz, seq, d = x.shape
    hp = HEADS // 2
    x2d = x.reshape(bsz * seq, d)
    proj, gates, gates_t = _in_proj(x2d, prm["g_pre"], prm["w_in"], tm=tm, out_dtype=BF16)
    proj = proj.reshape(bsz, seq, COL_GATE)
    gates = gates.reshape(bsz, seq, 128)
    gates_t = gates_t.reshape(bsz, seq // CHUNK, 16, CHUNK)
    cs8 =jnp.pad(conv0, ((0, 0), (8 - (CONV_W - 1), 0), (0, 0)))
    y, s_new, c_new, n_new, m_new = _layer_prompt(
        proj, gates, gates_t, cs8, s0, c0.reshape(bsz, hp, 128, 128), n0.reshape(bsz, hp, 1, 128),
        jnp.pad(m0, ((0, 0), (0, 128 - HEADS)))[:, None, :], x2d, prm["cw"], prm["plane"], prm["psub"],
        prm["gng"], prm["gnm"], prm["wo"], prm["g1"], prm["g2"], prm["wu"], prm["wd"], prm["g3"], tb=tb)
    y = y.reshape(bsz, seq, d)
    xp = jnp.concatenate([conv0, proj[:, seq - (CONV_W - 1):, COL_QKV:COL_QKV + CONV_CH].astype(F32)], axis=1)
    conv_new = xp[:, -(CONV_W - 1):]
    return y, (conv_new, s_new, c_new.reshape(bsz, HEADS, ML_DK, ML_DV), n_new.reshape(bsz, HEADS, ML_DK),
               m_new[:, 0, :HEADS])


def _sample_path(x, conv0, s0, c0, n0, m0, prm):
    dec, _, d = x.shape
    hp = HEADS // 2
    xs = x.reshape(dec, d)
    proj, gates, _ = _in_proj(xs, prm["g_pre"], prm["w_in"], tm=dec, out_dtype=F32)
    og, om, s_new, c_new, n_new, m_new = _decode(
        proj, gates, conv0, s0, c0.reshape(dec, hp, 128, 128), n0.reshape(dec, HEADS * ML_DK),
        jnp.pad(m0, ((0, 0), (0, 128 - HEADS))), prm["cw"], prm["plane"], prm["gng"], prm["gnm"], bb=8)
    y = _out_mlp(og, om, xs, prm["wo"], prm["g1"], prm["g2"], prm["wu"], prm["wd"], prm["g3"],
                 tm=dec).reshape(dec, 1, d)
    conv_new = jnp.concatenate([conv0[:, 1:, :], proj[:, None, COL_QKV:COL_QKV + CONV_CH]], axis=1)
    return y, (conv_new, s_new, c_new.reshape(dec, HEADS, ML_DK, ML_DV), n_new.reshape(dec, HEADS, ML_DK),
               m_new[:, :HEADS])


def kernel(x_prompt, x_sample, state_gdn_conv, state_gdn_S, state_mlstm_C, state_mlstm_n, state_mlstm_m,
           norm_pre_mix, w_in, conv_w, a_log, dt_bias, gdn_norm_g, b_igate, b_fgate, mlstm_norm_g, w_out,
           norm_post_mix, norm_pre_mlp, w_up, w_down, norm_post_mlp):
    bsz = x_prompt.shape[0]
    prm = _prep_params(norm_pre_mix, w_in, conv_w, a_log, dt_bias, gdn_norm_g, b_igate, b_fgate, mlstm_norm_g,
                       w_out, norm_post_mix, norm_pre_mlp, w_up, w_down, norm_post_mlp)
    y_p, p_st = _prompt_path(
        x_prompt, jnp.zeros((bsz, CONV_W - 1, CONV_CH), F32), jnp.zeros((bsz, HEADS, GDN_D, GDN_D), F32),
        jnp.zeros((bsz, HEADS, ML_DK, ML_DV), F32), jnp.zeros((bsz, HEADS, ML_DK), F32),
        jnp.zeros((bsz, HEADS), F32), prm, tb=256, tm=512)
    y_s, s_st = _sample_path(x_sample, state_gdn_conv[0], state_gdn_S[0], state_mlstm_C[0], state_mlstm_n[0],
                             state_mlstm_m[0], prm)
    return (y_p, y_s) + tuple(a[None] for a in p_st) + tuple(a[None] for a in s_st)
```

```python
import functools

import jax
import jax.numpy as jnp
from jax import lax
from jax.experimental import pallas as pl
from jax.experimental.pallas import tpu as pltpu

F32 = jnp.float32
BF16 = jnp.bfloat16
EPS = 1e-6

D_MODEL = 1024
HEADS = 4
GDN_D = 128
ML_DK = 64
ML_DV = 128
CONV_W = 4
CONV_CH = 3 * HEADS * GDN_D
D_FF = 4 * D_MODEL
CHUNK = 64

COL_QKV = 0
COL_Z = 1536
COL_MV = 2048
COL_MO = 2560
COL_MQ = 3072
COL_MK = 3328
COL_GATE = 3584
N_PROJ = COL_GATE + 128
LANE_GB, LANE_GA, LANE_MI, LANE_MF = 0, 4, 8, 12

VMEM_LIMIT = 56 * 1024 * 1024


def _rms(x, g):
    return x * lax.rsqrt(jnp.mean(x * x, axis=-1, keepdims=True) + EPS) * g


def _softplus(x):
    return jnp.maximum(x, 0.0) + jnp.log1p(jnp.exp(-jnp.abs(x)))


def _sigmoid(x):
    return 1.0 / (1.0 + jnp.exp(-x))


def _mm(a, b):
    return jnp.dot(a.astype(BF16), b.astype(BF16), preferred_element_type=F32)


def _mm_nt(a, b):
    return lax.dot_general(a.astype(BF16), b.astype(BF16), (((1,), (1,)), ((), ())),
                           preferred_element_type=F32)


W_IN_COLS = 3600
W_IN_SPANS = ((0, COL_QKV, 2048),
              (2568, COL_MV, 1024),
              (2056, COL_MQ, 512),
              (2048, COL_GATE, 8),
              (3592, COL_GATE + 8, 8))


def _in_proj_kernel(x_ref, g_ref, win_ref, o_ref, gc_ref, gt_ref, h_ref, w_ref, *, n_chunk):
    @pl.when(pl.program_id(0) == 0)
    def _():
        w_ref[COL_GATE:N_PROJ, :] = jnp.zeros((N_PROJ - COL_GATE, w_ref.shape[1]), BF16)
        for src, dst, width in W_IN_SPANS:
            for r0 in range(0, width, 512):
                r1 = min(r0 + 512, width)
                w_ref[dst + r0:dst + r1, :] = win_ref[src + r0:src + r1, :].astype(BF16)

    h_ref[...] = _rms(x_ref[...], g_ref[...]).astype(BF16)
    n = o_ref.shape[1]
    for c0 in range(0, n, n_chunk):
        c1 = min(c0 + n_chunk, n)
        o_ref[:, c0:c1] = _mm_nt(h_ref[...], w_ref[c0:c1, :]).astype(o_ref.dtype)
    gates = _mm_nt(h_ref[...], w_ref[COL_GATE:COL_GATE + 128, :])
    gc_ref[...] = gates
    gt = gates.T
    for c in range(gt_ref.shape[0]):
        gt_ref[c] = gt[0:16, c * CHUNK:(c + 1) * CHUNK]


def _in_proj(x2d, g, w, tm, out_dtype):
    m, k = x2d.shape
    return pl.pallas_call(
        functools.partial(_in_proj_kernel, n_chunk=512),
        grid=(m // tm,),
        in_specs=[pl.BlockSpec((tm, k), lambda i: (i, 0)),
                  pl.BlockSpec((1, k), lambda i: (0, 0)),
                  pl.BlockSpec((W_IN_COLS, k), lambda i: (0, 0), pipeline_mode=pl.Buffered(1))],
        out_specs=[pl.BlockSpec((tm, COL_GATE), lambda i: (i, 0)),
                   pl.BlockSpec((tm, 128), lambda i: (i, 0)),
                   pl.BlockSpec((tm // CHUNK, 16, CHUNK), lambda i: (i, 0, 0))],
        out_shape=[jax.ShapeDtypeStruct((m, COL_GATE), out_dtype),
                   jax.ShapeDtypeStruct((m, 128), F32),
                   jax.ShapeDtypeStruct((m // CHUNK, 16, CHUNK), F32)],
        scratch_shapes=[pltpu.VMEM((tm, k), BF16), pltpu.VMEM((N_PROJ, k), BF16)],
        compiler_params=pltpu.CompilerParams(dimension_semantics=("arbitrary",),
                                             vmem_limit_bytes=VMEM_LIMIT),
        name="in_proj",
    )(x2d, g, w)


def _out_mlp_kernel(og_ref, om_ref, x_ref, wo_ref, g1_ref, g2_ref, wu_ref, wd_ref, g3_ref,
                    y_ref, hn_ref, acc_ref, *, ff_chunk):
    half = og_ref.shape[1]
    mix = (jnp.dot(og_ref[...], wo_ref[0:half, :], preferred_element_type=F32)
           + jnp.dot(om_ref[...], wo_ref[half:2 * half, :], preferred_element_type=F32))
    x1 = x_ref[...] + _rms(mix, g1_ref[...])
    y_ref[...] = x1
    hn_ref[...] = _rms(x1, g2_ref[...]).astype(BF16)
    d_ff = wu_ref.shape[1]
    for c0 in range(0, d_ff, ff_chunk):
        u = jnp.dot(hn_ref[...], wu_ref[:, c0:c0 + ff_chunk], preferred_element_type=F32)
        u = jnp.square(jnp.maximum(u, 0.0)).astype(BF16)
        d = jnp.dot(u, wd_ref[c0:c0 + ff_chunk, :], preferred_element_type=F32)
        if c0 == 0:
            acc_ref[...] = d
        else:
            acc_ref[...] += d
    y_ref[...] = y_ref[...] + _rms(acc_ref[...], g3_ref[...])


def _out_mlp(og, om, x2d, wo, g1, g2, wu, wd, g3, tm):
    m, d = x2d.shape
    half = og.shape[1]
    d_ff = wu.shape[1]
    const = lambda i: (0, 0)
    return pl.pallas_call(
        functools.partial(_out_mlp_kernel, ff_chunk=1024),
        grid=(m // tm,),
        in_specs=[pl.BlockSpec((tm, half), lambda i: (i, 0)),
                  pl.BlockSpec((tm, half), lambda i: (i, 0)),
                  pl.BlockSpec((tm, d), lambda i: (i, 0)),
                  pl.BlockSpec((d, d), const, pipeline_mode=pl.Buffered(1)),
                  pl.BlockSpec((1, d), const),
                  pl.BlockSpec((1, d), const),
                  pl.BlockSpec((d, d_ff), const, pipeline_mode=pl.Buffered(1)),
                  pl.BlockSpec((d_ff, d), const, pipeline_mode=pl.Buffered(1)),
                  pl.BlockSpec((1, d), const)],
        out_specs=pl.BlockSpec((tm, d), lambda i: (i, 0)),
        out_shape=jax.ShapeDtypeStruct((m, d), F32),
        scratch_shapes=[pltpu.VMEM((tm, d), BF16), pltpu.VMEM((tm, d), F32)],
        compiler_params=pltpu.CompilerParams(dimension_semantics=("arbitrary",),
                                             vmem_limit_bytes=VMEM_LIMIT),
        name="out_mlp",
    )(og, om, x2d, wo, g1, g2, wu, wd, g3)


def _chunk_masks():
    ii = lax.broadcasted_iota(jnp.int32, (CHUNK, CHUNK), 0)
    jj = lax.broadcasted_iota(jnp.int32, (CHUNK, CHUNK), 1)
    return ii, jj


def _cumsum_col_row(x_col, x_row, ii, jj):
    c_col = jnp.sum(jnp.where(jj <= ii, x_row, 0.0), axis=1, keepdims=True)
    c_row = jnp.sum(jnp.where(ii <= jj, x_col, 0.0), axis=0, keepdims=True)
    return c_col, c_row


def _interleave(*tasks):
    live = [[g, n, 0] for g, n in tasks]
    while live:
        entry = min(live, key=lambda e: (e[2] + 1) / e[1])
        try:
            next(entry[0])
            entry[2] += 1
        except StopIteration:
            live.remove(entry)


def _halves(items):
    mid = len(items) // 2
    return items[:mid], items[mid:]


def _gdn_phase_a(yc_ref, gcol_ref, grow_ref, plane_ref, psub_ref, wv_ref, lhs_ref, qk_ref, ket_ref, ge_ref,
                 ncb):
    ii, jj = _chunk_masks()
    incl = jj <= ii
    strict = jj < ii
    eye = (ii == jj).astype(F32)
    neg_a_lane = -jnp.exp(plane_ref[0:1, :])
    dtb_lane = plane_ref[1:2, :]
    neg_a_sub = -jnp.exp(psub_ref[:, 0:1])
    dtb_sub = psub_ref[:, 1:2]
    items = []
    for c in range(ncb):
        rows = slice(c * CHUNK, (c + 1) * CHUNK)
        gc = gcol_ref[rows, :]
        gr = grow_ref[c]
        beta_t = _sigmoid(gc)
        g_t = neg_a_lane * _softplus(gc + dtb_lane)
        g_r = neg_a_sub * _softplus(gr + dtb_sub)
        for h in range(HEADS):
            lo = h * GDN_D
            items.append(dict(c=c, h=h,
                              q=yc_ref[rows, lo:lo + GDN_D],
                              k=yc_ref[rows, 512 + lo:512 + lo + GDN_D],
                              v=yc_ref[rows, 1024 + lo:1024 + lo + GDN_D],
                              beta=beta_t[:, LANE_GB + h:LANE_GB + h + 1],
                              gg_col=g_t[:, LANE_GA + h:LANE_GA + h + 1],
                              gg_row=g_r[LANE_GA + h:LANE_GA + h + 1, :]))
    yield
    for it in items:
        it["k"] = it["k"] * lax.rsqrt(jnp.sum(it["k"] * it["k"], axis=-1, keepdims=True) + EPS)
    yield
    for it in items:
        it["kk"] = _mm_nt(it["k"], it["k"])
    yield
    for it in items:
        it["q"] = (it["q"] * lax.rsqrt(jnp.sum(it["q"] * it["q"], axis=-1, keepdims=True) + EPS)
                   * (GDN_D ** -0.5))
    yield
    for it in items:
        it["g_col"], g_row = _cumsum_col_row(it["gg_col"], it["gg_row"], ii, jj)
        it["decay"] = jnp.where(incl, jnp.exp(jnp.where(incl, it["g_col"] - g_row, 0.0)), 0.0)
    yield
    for it in items:
        n_mat = jnp.where(strict, it["beta"] * it["kk"] * it["decay"], 0.0)
        it["x"] = eye - n_mat
        it["p"] = -n_mat
    yield
    for _ in range(5):
        yield
        for it in items:
            it["p"] = _mm(it["p"], it["p"])
        yield
        for it in items:
            it["x"] = it["x"] + _mm(it["x"], it["p"])
    yield
    for it in items:
        e_g = jnp.exp(it["g_col"])
        it["e_g"] = e_g
        rhs = jnp.concatenate([it["beta"] * it["v"], (it["beta"] * e_g) * it["k"]], axis=1)
        it["w"] = _mm(it["x"], rhs)
    yield
    for it in items:
        it["qk"] = _mm_nt(it["q"], it["k"]) * it["decay"]
    yield
    for it in items:
        c, h = it["c"], it["h"]
        g_end = it["g_col"][CHUNK - 1:CHUNK, :]
        k_end = it["k"] * jnp.exp(g_end - it["g_col"])
        wv_ref[c, h] = it["w"][:, 0:GDN_D]
        lhs_ref[c, h] = jnp.concatenate([it["w"][:, GDN_D:2 * GDN_D], it["e_g"] * it["q"]],
                                        axis=0).astype(BF16)
        qk_ref[c, h] = it["qk"].astype(BF16)
        ket_ref[c, h] = k_end.T.astype(BF16)
        ge_ref[c, h] = jnp.broadcast_to(jnp.exp(g_end), (1, GDN_D))


def _gdn_phase_b(qkvz_ref, gn_ref, og_ref, s_ref, wv_ref, lhs_ref, qk_ref, ket_ref, ge_ref, ncb):
    gn = gn_ref[...]

    def epilogue(c, o):
        rows = slice(c * CHUNK, (c + 1) * CHUNK)
        for h in range(HEADS):
            lo = h * GDN_D
            z = qkvz_ref[rows, COL_Z + lo:COL_Z + lo + GDN_D].astype(F32)
            out = _rms(o[h], gn) * (z * _sigmoid(z))
            og_ref[rows, lo:lo + GDN_D] = out.astype(og_ref.dtype)

    s = [s_ref[h] for h in range(HEADS)]
    o_prev = None
    for c in range(ncb):
        r = [jnp.dot(lhs_ref[c, h], s[h].astype(BF16), preferred_element_type=F32) for h in range(HEADS)]
        yield
        if o_prev is not None:
            epilogue(c - 1, o_prev)
        ub = [(wv_ref[c, h] - r[h][0:CHUNK]).astype(BF16) for h in range(HEADS)]
        s = [ge_ref[c, h] * s[h] + jnp.dot(ket_ref[c, h], ub[h], preferred_element_type=F32)
             for h in range(HEADS)]
        o_prev = [r[h][CHUNK:2 * CHUNK] + jnp.dot(qk_ref[c, h], ub[h], preferred_element_type=F32)
                  for h in range(HEADS)]
        yield
    epilogue(ncb - 1, o_prev)
    for h in range(HEADS):
        s_ref[h] = s[h]


def _mlstm_block(mvo_ref, mqk_ref, gcol_ref, grow_ref, plane_ref, psub_ref, gn_ref, om_ref,
                 c_ref, n_ref, m_ref, ncb):
    ii, jj = _chunk_masks()
    incl = jj <= ii
    lane = lax.broadcasted_iota(jnp.int32, (1, 128), 1)
    row128 = lax.broadcasted_iota(jnp.int32, (128, 1), 0)
    gn = gn_ref[...]
    blane = plane_ref[2:3, :]
    bsub = psub_ref[:, 2:3]
    items = []
    for c in range(ncb):
        rows = slice(c * CHUNK, (c + 1) * CHUNK)
        gc = gcol_ref[rows, :] + blane
        gr = grow_ref[c] + bsub
        lf_c = -_softplus(-gc)
        lf_r = -_softplus(-gr)
        for p in range(HEADS // 2):
            qb = mqk_ref[rows, p * 128:(p + 1) * 128].astype(F32)
            kb = mqk_ref[rows, 256 + p * 128:256 + (p + 1) * 128].astype(F32) * (ML_DK ** -0.5)
            for e in range(2):
                h = 2 * p + e
                lm = (lane >= e * ML_DK) & (lane < (e + 1) * ML_DK)
                items.append(dict(c=c, rows=rows, p=p, e=e, h=h, qh=jnp.where(lm, qb, 0.0), kb=kb,
                                  kh=jnp.where(lm, kb, 0.0),
                                  lf_col=lf_c[:, LANE_MF + h:LANE_MF + h + 1],
                                  lf_row=lf_r[LANE_MF + h:LANE_MF + h + 1, :],
                                  ig_row=gr[LANE_MI + h:LANE_MI + h + 1, :],
                                  ig_col=gc[:, LANE_MI + h:LANE_MI + h + 1],
                                  v=mvo_ref[rows, h * ML_DV:(h + 1) * ML_DV]))
        if c % 2 == 1:
            yield
    for part in _halves(items):
        for it in part:
            it["qk"] = _mm_nt(it["qh"], it["kb"])
        yield
    for part in _halves(items):
        for it in part:
            it["f_col"], it["f_row"] = _cumsum_col_row(it["lf_col"], it["lf_row"], ii, jj)
        yield
    for part in _halves(items):
        for it in part:
            it["d_mat"] = jnp.where(incl, it["f_col"] - it["f_row"] + it["ig_row"], -jnp.inf)
            it["d_max"] = jnp.max(it["d_mat"], axis=1, keepdims=True)
        yield
    for part in _halves(items):
        for it in part:
            p0 = jnp.where(incl, jnp.exp(jnp.where(incl, it["d_mat"] - it["d_max"], 0.0)), 0.0)
            pend0 = jnp.exp(it["f_col"][CHUNK - 1:CHUNK, :] - it["f_col"] + it["ig_col"]
                            - it["d_max"][CHUNK - 1:CHUNK, :])
            it["kp0"] = it["kh"] * pend0
            it["pqk0"] = p0 * it["qk"]
        yield
    for part in _halves(items):
        for it in part:
            it["pv0"] = _mm(it["pqk0"], it["v"])
            it["rs0"] = jnp.sum(it["pqk0"], axis=-1, keepdims=True)
        yield
    for part in _halves(items):
        for it in part:
            it["cadd0"] = _mm(it["kp0"].T, it["v"])
            it["nadd0"] = jnp.sum(it["kp0"], axis=0, keepdims=True)
        yield

    m_cur = [m_ref[:, h:h + 1] for h in range(HEADS)]
    for it in items:
        h = it["h"]
        it["m_prev"] = m_cur[h]
        m_cur[h] = jnp.maximum(it["f_col"][CHUNK - 1:CHUNK, :] + m_cur[h], it["d_max"][CHUNK - 1:CHUNK, :])
    yield
    for part in _halves(items):
        for it in part:
            bcol = it["f_col"] + it["m_prev"]
            mt = jnp.maximum(bcol, it["d_max"])
            it["mt"] = mt
            it["w_prev"] = jnp.exp(bcol - mt)
            it["sc"] = jnp.exp(it["d_max"] - mt)
        yield
    c_cur = [c_ref[p] for p in range(HEADS // 2)]
    n_cur = [n_ref[p] for p in range(HEADS // 2)]
    for c in range(ncb):
        for p in range(HEADS // 2):
            pair = [it for it in items if it["c"] == c and it["p"] == p]
            w_end = [it["w_prev"][CHUNK - 1:CHUNK, :] for it in pair]
            s_end = [it["sc"][CHUNK - 1:CHUNK, :] for it in pair]
            for it in pair:
                it["c_prev"] = c_cur[p]
                it["n_prev"] = n_cur[p]
            c_cur[p] = (jnp.where(row128 < ML_DK, w_end[0], w_end[1]) * c_cur[p]
                        + s_end[0] * pair[0]["cadd0"] + s_end[1] * pair[1]["cadd0"])
            n_cur[p] = (jnp.where(lane < ML_DK, w_end[0], w_end[1]) * n_cur[p]
                        + s_end[0] * pair[0]["nadd0"] + s_end[1] * pair[1]["nadd0"])
        if c % 2 == 1:
            yield
    for part in _halves(items):
        for it in part:
            it["qc"] = _mm(it["qh"], it["c_prev"])
        yield
    for part in _halves(items):
        for it in part:
            num = it["w_prev"] * it["qc"] + it["sc"] * it["pv0"]
            den = (it["w_prev"] * jnp.sum(it["qh"] * it["n_prev"], axis=-1, keepdims=True)
                   + it["sc"] * it["rs0"])
            it["hh"] = num / jnp.maximum(jnp.abs(den), jnp.exp(-it["mt"]))
        yield
    for part in _halves(items):
        for it in part:
            h = it["h"]
            mo = mvo_ref[it["rows"], 512 + h * ML_DV:512 + (h + 1) * ML_DV].astype(F32)
            out = _sigmoid(mo) * _rms(it["hh"], gn)
            om_ref[it["rows"], h * ML_DV:(h + 1) * ML_DV] = out.astype(om_ref.dtype)
        yield
    for p in range(HEADS // 2):
        c_ref[p] = c_cur[p]
        n_ref[p] = n_cur[p]
    for h in range(HEADS):
        m_ref[:, h:h + 1] = m_cur[h]


def _chain(*gens):
    for g in gens:
        yield from g


def _conv_stage(xp_ref, yc_ref, cw_ref, tb):
    for ct in range(CONV_CH // 128):
        cols = slice(ct * 128, (ct + 1) * 128)
        w = cw_ref[:, cols]
        for r0 in range(0, tb, 128):
            acc = xp_ref[8 + r0:8 + r0 + 128, cols] * w[CONV_W - 1:CONV_W, :]
            for j in range(CONV_W - 1):
                acc = acc + xp_ref[5 + j + r0:5 + j + r0 + 128, cols] * w[j:j + 1, :]
            yc_ref[r0:r0 + 128, cols] = acc * _sigmoid(acc)
        yield


def _out_mlp_block(og_ref, om_ref, x_ref, wo_ref, g1_ref, g2_ref, wu_ref, wd_ref, g3_ref,
                   y_ref, hn_ref, acc_ref, ff_chunk, n_split):
    half = og_ref.shape[1]
    d = x_ref.shape[1]
    d_ff = wu_ref.shape[1]
    col_groups = [slice(c, c + d // n_split) for c in range(0, d, d // n_split)]
    for cg in col_groups:
        acc_ref[:, cg] = (jnp.dot(og_ref[...], wo_ref[0:half, cg], preferred_element_type=F32)
                          + jnp.dot(om_ref[...], wo_ref[half:2 * half, cg], preferred_element_type=F32))
        yield
    x1 = x_ref[...] + _rms(acc_ref[...], g1_ref[...])
    y_ref[...] = x1
    hn_ref[...] = _rms(x1, g2_ref[...]).astype(BF16)
    yield
    for c0 in range(0, d_ff, ff_chunk):
        u = jnp.dot(hn_ref[...], wu_ref[:, c0:c0 + ff_chunk], preferred_element_type=F32)
        u = jnp.square(jnp.maximum(u, 0.0)).astype(BF16)
        for cg in col_groups:
            dd = jnp.dot(u, wd_ref[c0:c0 + ff_chunk, cg], preferred_element_type=F32)
            if c0 == 0:
                acc_ref[:, cg] = dd
            else:
                acc_ref[:, cg] += dd
        yield
    y_ref[...] = y_ref[...] + _rms(acc_ref[...], g3_ref[...])
    yield


def _layer_kernel(qkvz_ref, mvo_ref, mqk_ref, gcol_ref, grow_ref, cs_ref, s0_ref, c0_ref, n0_ref, m0_ref,
                  x_ref, cw_ref, plane_ref, psub_ref, gng_ref, gnm_ref,
                  wo_ref, g1_ref, g2_ref, wu_ref, wd_ref, g3_ref,
                  y_ref, sout_ref, cout_ref, nout_ref, mout_ref,
                  xp_ref, yc_ref, s_ref, wv_ref, lhs_ref, qk_ref, ket_ref, ge_ref, c_ref, n_ref, m_ref,
                  og_ref, om_ref, hn_ref, acc_ref, *, tb, nt, n_blocks):
    g = pl.program_id(0)
    t = jnp.minimum(g, n_blocks - 1) % nt
    ncb = tb // CHUNK
    par = g % 2

    @pl.when(g == 0)
    def _():
        og_ref[1] = jnp.zeros(og_ref.shape[1:], og_ref.dtype)
        om_ref[1] = jnp.zeros(om_ref.shape[1:], om_ref.dtype)

    @pl.when(t == 0)
    def _():
        xp_ref[0:8, :] = cs_ref[...]
        s_ref[...] = s0_ref[...]
        c_ref[...] = c0_ref[...]
        n_ref[...] = n0_ref[...]
        m_ref[...] = m0_ref[...]

    @pl.when(t > 0)
    def _():
        xp_ref[0:8, :] = xp_ref[tb:tb + 8, :]

    xp_ref[8:tb + 8, :] = qkvz_ref[:, COL_QKV:COL_QKV + CONV_CH].astype(F32)

    og_w, om_w = og_ref.at[par], om_ref.at[par]
    og_r, om_r = og_ref.at[1 - par], om_ref.at[1 - par]
    _interleave(
        (_chain(_conv_stage(xp_ref, yc_ref, cw_ref, tb),
                _gdn_phase_a(yc_ref, gcol_ref, grow_ref, plane_ref, psub_ref, wv_ref, lhs_ref, qk_ref, ket_ref,
                             ge_ref, ncb),
                _gdn_phase_b(qkvz_ref, gng_ref, og_w, s_ref, wv_ref, lhs_ref, qk_ref, ket_ref, ge_ref, ncb)),
         CONV_CH // 128 + 20 + 2 * ncb),
        (_out_mlp_block(og_r, om_r, x_ref, wo_ref, g1_ref, g2_ref, wu_ref, wd_ref, g3_ref, y_ref, hn_ref, acc_ref,
                        ff_chunk=256, n_split=2), 2 + 1 + wu_ref.shape[1] // 256 + 1),
        (_mlstm_block(mvo_ref, mqk_ref, gcol_ref, grow_ref, plane_ref, psub_ref, gnm_ref, om_w,
                      c_ref, n_ref, m_ref, ncb), 21 + ncb))

    @pl.when((t == nt - 1) & (g < n_blocks))
    def _():
        sout_ref[...] = s_ref[...]
        cout_ref[...] = c_ref[...]
        nout_ref[...] = n_ref[...]
        mout_ref[...] = m_ref[...]


def _layer_prompt(proj, gates, gates_t, cs8, s0, c0, n0, m0, x2d, conv_w, plane, psub, gng, gnm,
                  wo, g1, g2, wu, wd, g3, tb):
    b, t, _ = proj.shape
    d = x2d.shape[1]
    d_ff = wu.shape[1]
    nt = t // tb
    n_blocks = b * nt
    ncb = tb // CHUNK
    hp = HEADS // 2

    def blk(g):
        gm = jnp.minimum(g, n_blocks - 1)
        return gm // nt, gm % nt

    def tok3(col):
        return lambda g: blk(g) + (col,)

    per_seq3 = lambda g: (blk(g)[0], 0, 0)
    per_seq4 = lambda g: (blk(g)[0], 0, 0, 0)
    prev_rows = lambda g: (jnp.maximum(g - 1, 0), 0)
    const = lambda g: (0, 0)
    resident = dict(pipeline_mode=pl.Buffered(1))
    return pl.pallas_call(
        functools.partial(_layer_kernel, tb=tb, nt=nt, n_blocks=n_blocks),
        grid=(n_blocks + 1,),
        in_specs=[pl.BlockSpec((None, tb, 2048), tok3(0)),
                  pl.BlockSpec((None, tb, 1024), tok3(COL_MV // 1024)),
                  pl.BlockSpec((None, tb, 512), tok3(COL_MQ // 512)),
                  pl.BlockSpec((None, tb, 128), tok3(0)),
                  pl.BlockSpec((None, ncb, 16, CHUNK), lambda g: blk(g) + (0, 0)),
                  pl.BlockSpec((None, 8, CONV_CH), per_seq3),
                  pl.BlockSpec((None, HEADS, GDN_D, GDN_D), per_seq4),
                  pl.BlockSpec((None, hp, 128, 128), per_seq4),
                  pl.BlockSpec((None, hp, 1, 128), per_seq4),
                  pl.BlockSpec((None, 1, 128), per_seq3),
                  pl.BlockSpec((tb, d), prev_rows),
                  pl.BlockSpec((CONV_W, CONV_CH), const),
                  pl.BlockSpec((3, 128), const),
                  pl.BlockSpec((16, 3), const),
                  pl.BlockSpec((1, GDN_D), const),
                  pl.BlockSpec((1, ML_DV), const),
                  pl.BlockSpec((d, d), const, **resident),
                  pl.BlockSpec((1, d), const),
                  pl.BlockSpec((1, d), const),
                  pl.BlockSpec((d, d_ff), const, **resident),
                  pl.BlockSpec((d_ff, d), const, **resident),
                  pl.BlockSpec((1, d), const)],
        out_specs=[pl.BlockSpec((tb, d), prev_rows),
                   pl.BlockSpec((None, HEADS, GDN_D, GDN_D), per_seq4),
                   pl.BlockSpec((None, hp, 128, 128), per_seq4),
                   pl.BlockSpec((None, hp, 1, 128), per_seq4),
                   pl.BlockSpec((None, 1, 128), per_seq3)],
        out_shape=[jax.ShapeDtypeStruct((b * t, d), F32),
                   jax.ShapeDtypeStruct((b, HEADS, GDN_D, GDN_D), F32),
                   jax.ShapeDtypeStruct((b, hp, 128, 128), F32),
                   jax.ShapeDtypeStruct((b, hp, 1, 128), F32),
                   jax.ShapeDtypeStruct((b, 1, 128), F32)],
        scratch_shapes=[pltpu.VMEM((tb + 8, CONV_CH), F32),
                        pltpu.VMEM((tb, CONV_CH), F32),
                        pltpu.VMEM((HEADS, GDN_D, GDN_D), F32),
                        pltpu.VMEM((ncb, HEADS, CHUNK, GDN_D), F32),
                        pltpu.VMEM((ncb, HEADS, 2 * CHUNK, GDN_D), BF16),
                        pltpu.VMEM((ncb, HEADS, CHUNK, CHUNK), BF16),
                        pltpu.VMEM((ncb, HEADS, GDN_D, CHUNK), BF16),
                        pltpu.VMEM((ncb, HEADS, 1, GDN_D), F32),
                        pltpu.VMEM((hp, 128, 128), F32),
                        pltpu.VMEM((hp, 1, 128), F32),
                        pltpu.VMEM((1, 128), F32),
                        pltpu.VMEM((2, tb, HEADS * GDN_D), BF16),
                        pltpu.VMEM((2, tb, HEADS * ML_DV), BF16),
                        pltpu.VMEM((tb, d), BF16),
                        pltpu.VMEM((tb, d), F32)],
        compiler_params=pltpu.CompilerParams(dimension_semantics=("arbitrary",),
                                             vmem_limit_bytes=VMEM_LIMIT),
        name="layer_prompt",
    )(proj, proj, proj, gates, gates_t, cs8, s0, c0, n0, m0, x2d, conv_w, plane, psub, gng, gnm,
      wo, g1, g2, wu, wd, g3)


def _decode_kernel(proj_ref, gates_ref, cs_ref, s0_ref, c0_ref, n0_ref, m0_ref, cw_ref, plane_ref, gng_ref, gnm_ref,
                   og_ref, om_ref, sout_ref, cout_ref, nout_ref, mout_ref, rowg_ref, rowm_ref, *, bb):
    lane = lax.broadcasted_iota(jnp.int32, (1, 128), 1)
    gt = gates_ref[...]
    beta_t = _sigmoid(gt)
    g_t = -jnp.exp(plane_ref[0:1, :]) * _softplus(gt + plane_ref[1:2, :])
    eg_t = jnp.exp(g_t)
    gb_t = gt + plane_ref[2:3, :]
    lf_t = -_softplus(-gb_t)

    heads = []
    for h in range(HEADS):
        cols = []
        for part in range(3):
            c0 = part * 512 + h * GDN_D
            w = cw_ref[:, c0:c0 + GDN_D]
            acc = cs_ref[:, 0, c0:c0 + GDN_D] * w[0:1, :]
            acc = acc + cs_ref[:, 1, c0:c0 + GDN_D] * w[1:2, :]
            acc = acc + cs_ref[:, 2, c0:c0 + GDN_D] * w[2:3, :]
            acc = acc + proj_ref[:, COL_QKV + c0:COL_QKV + c0 + GDN_D] * w[3:4, :]
            cols.append(acc * _sigmoid(acc))
        q, k, v = cols
        q = q * lax.rsqrt(jnp.sum(q * q, axis=-1, keepdims=True) + EPS) * (GDN_D ** -0.5)
        k = k * lax.rsqrt(jnp.sum(k * k, axis=-1, keepdims=True) + EPS)
        heads.append(dict(v=v, qk=jnp.sum(q * k, axis=-1, keepdims=True), k_t=k.T,
                          kq=jnp.concatenate([k, q], axis=0).astype(BF16),
                          beta=beta_t[:, LANE_GB + h:LANE_GB + h + 1],
                          eg=eg_t[:, LANE_GA + h:LANE_GA + h + 1]))
    items = [dict(h=h, b=b) for h in range(HEADS) for b in range(bb)]
    for it in items:
        hd, b = heads[it["h"]], it["b"]
        it["k_col"] = hd["k_t"][:, b:b + 1]
    for it in items:
        r = jnp.dot(heads[it["h"]]["kq"], s0_ref[it["b"], it["h"]].astype(BF16), preferred_element_type=F32)
        it["ks"] = r[it["b"]:it["b"] + 1, :]
        it["qs"] = r[bb + it["b"]:bb + it["b"] + 1, :]
    for it in items:
        hd, b = heads[it["h"]], it["b"]
        eg_b = hd["eg"][b:b + 1, :]
        it["eg_b"] = eg_b
        it["u"] = hd["beta"][b:b + 1, :] * (hd["v"][b:b + 1, :] - eg_b * it["ks"])
        rowg_ref[it["h"], b:b + 1, :] = eg_b * it["qs"] + hd["qk"][b:b + 1, :] * it["u"]
    for it in items:
        sout_ref[it["b"], it["h"]] = it["eg_b"] * s0_ref[it["b"], it["h"]] + it["k_col"] * it["u"]
    for h in range(HEADS):
        z = proj_ref[:, COL_Z + h * GDN_D:COL_Z + (h + 1) * GDN_D]
        out = _rms(rowg_ref[h], gng_ref[...]) * (z * _sigmoid(z))
        og_ref[:, h * GDN_D:(h + 1) * GDN_D] = out.astype(og_ref.dtype)

    heads = []
    for p in range(HEADS // 2):
        qb = proj_ref[:, COL_MQ + p * 128:COL_MQ + (p + 1) * 128]
        kb = proj_ref[:, COL_MK + p * 128:COL_MK + (p + 1) * 128] * (ML_DK ** -0.5)
        n_p = n0_ref[:, p * 128:(p + 1) * 128]
        k_t = kb.T
        qq = jnp.concatenate([jnp.where(lane < ML_DK, qb, 0.0), jnp.where(lane >= ML_DK, qb, 0.0)],
                             axis=0).astype(BF16)
        w_prev, p_in = [], []
        for e in range(2):
            h = 2 * p + e
            lm = (lane >= e * ML_DK) & (lane < (e + 1) * ML_DK)
            ig = gb_t[:, LANE_MI + h:LANE_MI + h + 1]
            lf = lf_t[:, LANE_MF + h:LANE_MF + h + 1]
            m_old = m0_ref[:, h:h + 1]
            m_new = jnp.maximum(lf + m_old, ig)
            w_prev.append(jnp.exp(lf + m_old - m_new))
            p_in.append(jnp.exp(ig - m_new))
            qk = jnp.sum(jnp.where(lm, qb * kb, 0.0), axis=-1, keepdims=True)
            qn = jnp.sum(jnp.where(lm, qb * n_p, 0.0), axis=-1, keepdims=True)
            mout_ref[:, h:h + 1] = m_new
            heads.append(dict(p=p, e=e, qq=qq, k_t=k_t, w_prev=w_prev[e], p_in=p_in[e],
                              pqk=p_in[e] * qk, wqn=w_prev[e] * qn, floor=jnp.exp(-m_new),
                              v=proj_ref[:, COL_MV + h * ML_DV:COL_MV + (h + 1) * ML_DV]))
        lo_lane = lane < ML_DK
        nout_ref[:, p * 128:(p + 1) * 128] = (jnp.where(lo_lane, w_prev[0], w_prev[1]) * n_p
                                              + jnp.where(lo_lane, p_in[0], p_in[1]) * kb)
    items = [dict(h=h, b=b) for h in range(HEADS) for b in range(bb)]
    for it in items:
        hd, b = heads[it["h"]], it["b"]
        rs = slice(hd["e"] * ML_DK, (hd["e"] + 1) * ML_DK)
        it["rs"] = rs
        it["k_col"] = hd["k_t"][rs, b:b + 1]
    qc_pair = {(p, b): jnp.dot(heads[2 * p]["qq"], c0_ref[b, p].astype(BF16), preferred_element_type=F32)
               for p in range(HEADS // 2) for b in range(bb)}
    for it in items:
        hd, b = heads[it["h"]], it["b"]
        row = hd["e"] * bb + b
        it["qc"] = qc_pair[(hd["p"], b)][row:row + 1, :]
    for it in items:
        hd, b = heads[it["h"]], it["b"]
        pqk = hd["pqk"][b:b + 1, :]
        num = hd["w_prev"][b:b + 1, :] * it["qc"] + pqk * hd["v"][b:b + 1, :]
        den = hd["wqn"][b:b + 1, :] + pqk
        rowm_ref[it["h"], b:b + 1, :] = num / jnp.maximum(jnp.abs(den), hd["floor"][b:b + 1, :])
    for it in items:
        hd, b = heads[it["h"]], it["b"]
        cout_ref[b, hd["p"], it["rs"], :] = (hd["w_prev"][b:b + 1, :] * c0_ref[b, hd["p"], it["rs"], :]
                                             + (hd["p_in"][b:b + 1, :] * it["k_col"]) * hd["v"][b:b + 1, :])
    for h in range(HEADS):
        mo = proj_ref[:, COL_MO + h * ML_DV:COL_MO + (h + 1) * ML_DV]
        out = _sigmoid(mo) * _rms(rowm_ref[h], gnm_ref[...])
        om_ref[:, h * ML_DV:(h + 1) * ML_DV] = out.astype(om_ref.dtype)
    mout_ref[:, HEADS:128] = m0_ref[:, HEADS:128]


def _decode(proj, gates, cs, s0, c0, n0, m0, conv_w, plane, gng, gnm, bb):
    b = proj.shape[0]
    hp = HEADS // 2
    return pl.pallas_call(
        functools.partial(_decode_kernel, bb=bb),
        grid=(b // bb,),
        in_specs=[pl.BlockSpec((bb, COL_GATE), lambda i: (i, 0)),
                  pl.BlockSpec((bb, 128), lambda i: (i, 0)),
                  pl.BlockSpec((bb, CONV_W - 1, CONV_CH), lambda i: (i, 0, 0)),
                  pl.BlockSpec((bb, HEADS, GDN_D, GDN_D), lambda i: (i, 0, 0, 0)),
                  pl.BlockSpec((bb, hp, 128, 128), lambda i: (i, 0, 0, 0)),
                  pl.BlockSpec((bb, HEADS * ML_DK), lambda i: (i, 0)),
                  pl.BlockSpec((bb, 128), lambda i: (i, 0)),
                  pl.BlockSpec((CONV_W, CONV_CH), lambda i: (0, 0)),
                  pl.BlockSpec((3, 128), lambda i: (0, 0)),
                  pl.BlockSpec((1, GDN_D), lambda i: (0, 0)),
                  pl.BlockSpec((1, ML_DV), lambda i: (0, 0))],
        out_specs=[pl.BlockSpec((bb, HEADS * GDN_D), lambda i: (i, 0)),
                   pl.BlockSpec((bb, HEADS * ML_DV), lambda i: (i, 0)),
                   pl.BlockSpec((bb, HEADS, GDN_D, GDN_D), lambda i: (i, 0, 0, 0)),
                   pl.BlockSpec((bb, hp, 128, 128), lambda i: (i, 0, 0, 0)),
                   pl.BlockSpec((bb, HEADS * ML_DK), lambda i: (i, 0)),
                   pl.BlockSpec((bb, 128), lambda i: (i, 0))],
        out_shape=[jax.ShapeDtypeStruct((b, HEADS * GDN_D), BF16),
                   jax.ShapeDtypeStruct((b, HEADS * ML_DV), BF16),
                   jax.ShapeDtypeStruct((b, HEADS, GDN_D, GDN_D), F32),
                   jax.ShapeDtypeStruct((b, hp, 128, 128), F32),
                   jax.ShapeDtypeStruct((b, HEADS * ML_DK), F32),
                   jax.ShapeDtypeStruct((b, 128), F32)],
        scratch_shapes=[pltpu.VMEM((HEADS, bb, 128), F32), pltpu.VMEM((HEADS, bb, 128), F32)],
        compiler_params=pltpu.CompilerParams(dimension_semantics=("arbitrary",),
                                             vmem_limit_bytes=VMEM_LIMIT),
        name="decode_step",
    )(proj, gates, cs, s0, c0, n0, m0, conv_w, plane, gng, gnm)


def _lane_vec(pairs):
    v = jnp.zeros((128,), F32)
    for off, val in pairs:
        v = v.at[off:off + HEADS].set(val.astype(F32))
    return v


def _prep_params(norm_pre_mix, w_in, conv_w, a_log, dt_bias, gdn_norm_g, b_igate, b_fgate, mlstm_norm_g,
                 w_out, norm_post_mix, norm_pre_mlp, w_up, w_down, norm_post_mlp):
    alog_v = _lane_vec([(LANE_GA, a_log[0])])
    dtb_v = _lane_vec([(LANE_GA, dt_bias[0])])
    bias_v = _lane_vec([(LANE_MI, b_igate[0]), (LANE_MF, b_fgate[0])])
    return dict(
        w_in=jnp.swapaxes(w_in[0], 0, 1), wo=w_out[0].astype(BF16), wu=w_up[0].astype(BF16),
        wd=w_down[0].astype(BF16),
        g_pre=norm_pre_mix[0][None, :], g1=norm_post_mix[0][None, :], g2=norm_pre_mlp[0][None, :],
        g3=norm_post_mlp[0][None, :], cw=conv_w[0], gng=gdn_norm_g[0][None, :], gnm=mlstm_norm_g[0][None, :],
        plane=jnp.stack([alog_v, dtb_v, bias_v]),
        psub=jnp.stack([alog_v[:16], dtb_v[:16], bias_v[:16]], axis=1),
    )


def _prompt_path(x, conv0, s0, c0, n0, m0, prm, tb, tm):
    bsz, seq, d = x.shape
    hp = HEADS // 2
    x2d = x.reshape(bsz * seq, d)
    proj, gates, gates_t = _in_proj(x2d, prm["g_pre"], prm["w_in"], tm=tm, out_dtype=BF16)
    proj = proj.reshape(bsz, seq, COL_GATE)
    gates = gates.reshape(bsz, seq, 128)
    gates_t = gates_t.reshape(bsz, seq // CHUNK, 16, CHUNK)
    cs8 = jnp.pad(conv0, ((0, 0), (8 - (CONV_W - 1), 0), (0, 0)))
    y, s_new, c_new, n_new, m_new = _layer_prompt(
        proj, gates, gates_t, cs8, s0, c0.reshape(bsz, hp, 128, 128), n0.reshape(bsz, hp, 1, 128),
        jnp.pad(m0, ((0, 0), (0, 128 - HEADS)))[:, None, :], x2d, prm["cw"], prm["plane"], prm["psub"],
        prm["gng"], prm["gnm"], prm["wo"], prm["g1"], prm["g2"], prm["wu"], prm["wd"], prm["g3"], tb=tb)
    y = y.reshape(bsz, seq, d)
    xp = jnp.concatenate([conv0, proj[:, seq - (CONV_W - 1):, COL_QKV:COL_QKV + CONV_CH].astype(F32)], axis=1)
    conv_new = xp[:, -(CONV_W - 1):]
    return y, (conv_new, s_new, c_new.reshape(bsz, HEADS, ML_DK, ML_DV), n_new.reshape(bsz, HEADS, ML_DK),
               m_new[:, 0, :HEADS])


def _sample_path(x, conv0, s0, c0, n0, m0, prm):
    dec, _, d = x.shape
    hp = HEADS // 2
    xs = x.reshape(dec, d)
    proj, gates, _ = _in_proj(xs, prm["g_pre"], prm["w_in"], tm=dec, out_dtype=F32)
    og, om, s_new, c_new, n_new, m_new = _decode(
        proj, gates, conv0, s0, c0.reshape(dec, hp, 128, 128), n0.reshape(dec, HEADS * ML_DK),
        jnp.pad(m0, ((0, 0), (0, 128 - HEADS))), prm["cw"], prm["plane"], prm["gng"], prm["gnm"], bb=8)
    y = _out_mlp(og, om, xs, prm["wo"], prm["g1"], prm["g2"], prm["wu"], prm["wd"], prm["g3"],
                 tm=dec).reshape(dec, 1, d)
    conv_new = jnp.concatenate([conv0[:, 1:, :], proj[:, None, COL_QKV:COL_QKV + CONV_CH]], axis=1)
    return y, (conv_new, s_new, c_new.reshape(dec, HEADS, ML_DK, ML_DV), n_new.reshape(dec, HEADS, ML_DK),
               m_new[:, :HEADS])


def kernel(x_prompt, x_sample, state_gdn_conv, state_gdn_S, state_mlstm_C, state_mlstm_n, state_mlstm_m,
           norm_pre_mix, w_in, conv_w, a_log, dt_bias, gdn_norm_g, b_igate, b_fgate, mlstm_norm_g, w_out,
           norm_post_mix, norm_pre_mlp, w_up, w_down, norm_post_mlp):
    bsz = x_prompt.shape[0]
    prm = _prep_params(norm_pre_mix, w_in, conv_w, a_log, dt_bias, gdn_norm_g, b_igate, b_fgate, mlstm_norm_g,
                       w_out, norm_post_mix, norm_pre_mlp, w_up, w_down, norm_post_mlp)
    y_p, p_st = _prompt_path(
        x_prompt, jnp.zeros((bsz, CONV_W - 1, CONV_CH), F32), jnp.zeros((bsz, HEADS, GDN_D, GDN_D), F32),
        jnp.zeros((bsz, HEADS, ML_DK, ML_DV), F32), jnp.zeros((bsz, HEADS, ML_DK), F32),
        jnp.zeros((bsz, HEADS), F32), prm, tb=256, tm=512)
    y_s, s_st = _sample_path(x_sample, state_gdn_conv[0], state_gdn_S[0], state_mlstm_C[0], state_mlstm_n[0],
                             state_mlstm_m[0], prm)
    return (y_p, y_s) + tuple(a[None] for a in p_st) + tuple(a[None] for a in s_st)
```

```python
import functools

import jax
import jax.numpy as jnp
from jax import lax
from jax.experimental import pallas as pl
from jax.experimental.pallas import tpu as pltpu

F32 = jnp.float32
BF16 = jnp.bfloat16
EPS = 1e-6

HEADS = 4
GDN_D = 128
ML_DK = 64
ML_DV = 128
CONV_W = 4
CONV_CH = 3 * HEADS * GDN_D
CHUNK = 64

COL_QKV = 0
COL_Z = 1536
COL_MV = 2048
COL_MO = 2560
COL_MQ = 3072
COL_MK = 3328
COL_GATE = 3584
N_PROJ = COL_GATE + 128
LANE_GB, LANE_GA, LANE_MI, LANE_MF = 0, 4, 8, 12

V7X_VMEM_BYTES = 64 * 1024 * 1024
VMEM_LIMIT = V7X_VMEM_BYTES - 8 * 1024 * 1024


def _rms(x, g):
    return x * lax.rsqrt(jnp.mean(x * x, axis=-1, keepdims=True) + EPS) * g


def _softplus(x):
    return jnp.maximum(x, 0.0) + jnp.log1p(jnp.exp(-jnp.abs(x)))


def _sigmoid(x):
    return 1.0 / (1.0 + jnp.exp(-x))


def _mm(a, b):
    return jnp.dot(a.astype(BF16), b.astype(BF16), preferred_element_type=F32)


def _mm_nt(a, b):
    return lax.dot_general(a.astype(BF16), b.astype(BF16), (((1,), (1,)), ((), ())),
                           preferred_element_type=F32)


W_IN_COLS = 3600
W_IN_SPANS = ((0, COL_QKV, 2048),
              (2568, COL_MV, 1024),
              (2056, COL_MQ, 512),
              (2048, COL_GATE, 8),
              (3592, COL_GATE + 8, 8))


def _in_proj_kernel(x_ref, g_ref, win_ref, o_ref, gc_ref, gt_ref, h_ref, w_ref, *, n_chunk):
    @pl.when(pl.program_id(0) == 0)
    def _():
        w_ref[COL_GATE:N_PROJ, :] = jnp.zeros((N_PROJ - COL_GATE, w_ref.shape[1]), BF16)
        for src, dst, width in W_IN_SPANS:
            for r0 in range(0, width, 512):
                r1 = min(r0 + 512, width)
                w_ref[dst + r0:dst + r1, :] = win_ref[src + r0:src + r1, :].astype(BF16)

    h_ref[...] = _rms(x_ref[...], g_ref[...]).astype(BF16)
    n = o_ref.shape[1]
    for c0 in range(0, n, n_chunk):
        c1 = min(c0 + n_chunk, n)
        o_ref[:, c0:c1] = _mm_nt(h_ref[...], w_ref[c0:c1, :]).astype(o_ref.dtype)
    gates = _mm_nt(h_ref[...], w_ref[COL_GATE:COL_GATE + 128, :])
    gc_ref[...] = gates
    gt = gates.T
    for c in range(gt_ref.shape[0]):
        gt_ref[c] = gt[0:16, c * CHUNK:(c + 1) * CHUNK]


def _in_proj(x2d, g, w, tm, out_dtype):
    m, k = x2d.shape
    return pl.pallas_call(
        functools.partial(_in_proj_kernel, n_chunk=512),
        grid=(m // tm,),
        in_specs=[pl.BlockSpec((tm, k), lambda i: (i, 0)),
                  pl.BlockSpec((1, k), lambda i: (0, 0)),
                  pl.BlockSpec((W_IN_COLS, k), lambda i: (0, 0), pipeline_mode=pl.Buffered(1))],
        out_specs=[pl.BlockSpec((tm, COL_GATE), lambda i: (i, 0)),
                   pl.BlockSpec((tm, 128), lambda i: (i, 0)),
                   pl.BlockSpec((tm // CHUNK, 16, CHUNK), lambda i: (i, 0, 0))],
        out_shape=[jax.ShapeDtypeStruct((m, COL_GATE), out_dtype),
                   jax.ShapeDtypeStruct((m, 128), F32),
                   jax.ShapeDtypeStruct((m // CHUNK, 16, CHUNK), F32)],
        scratch_shapes=[pltpu.VMEM((tm, k), BF16), pltpu.VMEM((N_PROJ, k), BF16)],
        compiler_params=pltpu.CompilerParams(dimension_semantics=("arbitrary",),
                                             vmem_limit_bytes=VMEM_LIMIT),
        name="in_proj",
    )(x2d, g, w)


def _out_mlp_kernel(og_ref, om_ref, x_ref, wo_ref, g1_ref, g2_ref, wu_ref, wd_ref, g3_ref,
                    y_ref, hn_ref, acc_ref, *, ff_chunk):
    half = og_ref.shape[1]
    mix = (jnp.dot(og_ref[...], wo_ref[0:half, :], preferred_element_type=F32)
           + jnp.dot(om_ref[...], wo_ref[half:2 * half, :], preferred_element_type=F32))
    x1 = x_ref[...] + _rms(mix, g1_ref[...])
    y_ref[...] = x1
    hn_ref[...] = _rms(x1, g2_ref[...]).astype(BF16)
    d_ff = wu_ref.shape[1]
    for c0 in range(0, d_ff, ff_chunk):
        u = jnp.dot(hn_ref[...], wu_ref[:, c0:c0 + ff_chunk], preferred_element_type=F32)
        u = jnp.square(jnp.maximum(u, 0.0)).astype(BF16)
        d = jnp.dot(u, wd_ref[c0:c0 + ff_chunk, :], preferred_element_type=F32)
        if c0 == 0:
            acc_ref[...] = d
        else:
            acc_ref[...] += d
    y_ref[...] = y_ref[...] + _rms(acc_ref[...], g3_ref[...])


def _out_mlp(og, om, x2d, wo, g1, g2, wu, wd, g3, tm):
    m, d = x2d.shape
    half = og.shape[1]
    d_ff = wu.shape[1]
    const = lambda i: (0, 0)
    return pl.pallas_call(
        functools.partial(_out_mlp_kernel, ff_chunk=1024),
        grid=(m // tm,),
        in_specs=[pl.BlockSpec((tm, half), lambda i: (i, 0)),
                  pl.BlockSpec((tm, half), lambda i: (i, 0)),
                  pl.BlockSpec((tm, d), lambda i: (i, 0)),
                  pl.BlockSpec((d, d), const, pipeline_mode=pl.Buffered(1)),
                  pl.BlockSpec((1, d), const),
                  pl.BlockSpec((1, d), const),
                  pl.BlockSpec((d, d_ff), const, pipeline_mode=pl.Buffered(1)),
                  pl.BlockSpec((d_ff, d), const, pipeline_mode=pl.Buffered(1)),
                  pl.BlockSpec((1, d), const)],
        out_specs=pl.BlockSpec((tm, d), lambda i: (i, 0)),
        out_shape=jax.ShapeDtypeStruct((m, d), F32),
        scratch_shapes=[pltpu.VMEM((tm, d), BF16), pltpu.VMEM((tm, d), F32)],
        compiler_params=pltpu.CompilerParams(dimension_semantics=("arbitrary",),
                                             vmem_limit_bytes=VMEM_LIMIT),
        name="out_mlp",
    )(og, om, x2d, wo, g1, g2, wu, wd, g3)


def _chunk_masks():
    ii = lax.broadcasted_iota(jnp.int32, (CHUNK, CHUNK), 0)
    jj = lax.broadcasted_iota(jnp.int32, (CHUNK, CHUNK), 1)
    return ii, jj


def _cumsum_col_row(x_col, x_row, ii, jj):
    c_col = jnp.sum(jnp.where(jj <= ii, x_row, 0.0), axis=1, keepdims=True)
    c_row = jnp.sum(jnp.where(ii <= jj, x_col, 0.0), axis=0, keepdims=True)
    return c_col, c_row


def _interleave(*tasks):
    live = [[g, n, 0] for g, n in tasks]
    while live:
        entry = min(live, key=lambda e: (e[2] + 1) / e[1])
        try:
            next(entry[0])
            entry[2] += 1
        except StopIteration:
            live.remove(entry)


def _halves(items):
    mid = len(items) // 2
    return items[:mid], items[mid:]


def _gdn_phase_a(yc_ref, gcol_ref, grow_ref, plane_ref, psub_ref, wv_ref, lhs_ref, qk_ref, ket_ref, ge_ref,
                 ncb):
    ii, jj = _chunk_masks()
    incl = jj <= ii
    strict = jj < ii
    eye = (ii == jj).astype(F32)
    neg_a_lane = -jnp.exp(plane_ref[0:1, :])
    dtb_lane = plane_ref[1:2, :]
    neg_a_sub = -jnp.exp(psub_ref[:, 0:1])
    dtb_sub = psub_ref[:, 1:2]
    items = []
    for c in range(ncb):
        rows = slice(c * CHUNK, (c + 1) * CHUNK)
        gc = gcol_ref[rows, :]
        gr = grow_ref[c]
        beta_t = _sigmoid(gc)
        g_t = neg_a_lane * _softplus(gc + dtb_lane)
        g_r = neg_a_sub * _softplus(gr + dtb_sub)
        for h in range(HEADS):
            lo = h * GDN_D
            items.append(dict(c=c, h=h,
                              q=yc_ref[rows, lo:lo + GDN_D],
                              k=yc_ref[rows, 512 + lo:512 + lo + GDN_D],
                              v=yc_ref[rows, 1024 + lo:1024 + lo + GDN_D],
                              beta=beta_t[:, LANE_GB + h:LANE_GB + h + 1],
                              gg_col=g_t[:, LANE_GA + h:LANE_GA + h + 1],
                              gg_row=g_r[LANE_GA + h:LANE_GA + h + 1, :]))
    yield
    for it in items:
        it["k"] = it["k"] * lax.rsqrt(jnp.sum(it["k"] * it["k"], axis=-1, keepdims=True) + EPS)
    yield
    for it in items:
        it["kk"] = _mm_nt(it["k"], it["k"])
    yield
    for it in items:
        it["q"] = (it["q"] * lax.rsqrt(jnp.sum(it["q"] * it["q"], axis=-1, keepdims=True) + EPS)
                   * (GDN_D ** -0.5))
    yield
    for it in items:
        it["g_col"], g_row = _cumsum_col_row(it["gg_col"], it["gg_row"], ii, jj)
        it["decay"] = jnp.where(incl, jnp.exp(jnp.where(incl, it["g_col"] - g_row, 0.0)), 0.0)
    yield
    for it in items:
        n_mat = jnp.where(strict, it["beta"] * it["kk"] * it["decay"], 0.0)
        it["x"] = eye - n_mat
        it["p"] = -n_mat
    yield
    for _ in range(5):
        yield
        for it in items:
            it["p"] = _mm(it["p"], it["p"])
        yield
        for it in items:
            it["x"] = it["x"] + _mm(it["x"], it["p"])
    yield
    for it in items:
        e_g = jnp.exp(it["g_col"])
        it["e_g"] = e_g
        rhs = jnp.concatenate([it["beta"] * it["v"], (it["beta"] * e_g) * it["k"]], axis=1)
        it["w"] = _mm(it["x"], rhs)
    yield
    for it in items:
        it["qk"] = _mm_nt(it["q"], it["k"]) * it["decay"]
    yield
    for it in items:
        c, h = it["c"], it["h"]
        g_end = it["g_col"][CHUNK - 1:CHUNK, :]
        k_end = it["k"] * jnp.exp(g_end - it["g_col"])
        wv_ref[c, h] = it["w"][:, 0:GDN_D]
        lhs_ref[c, h] = jnp.concatenate([it["w"][:, GDN_D:2 * GDN_D], it["e_g"] * it["q"]],
                                        axis=0).astype(BF16)
        qk_ref[c, h] = it["qk"].astype(BF16)
        ket_ref[c, h] = k_end.T.astype(BF16)
        ge_ref[c, h] = jnp.broadcast_to(jnp.exp(g_end), (1, GDN_D))


def _gdn_phase_b(qkvz_ref, gn_ref, og_ref, s_ref, wv_ref, lhs_ref, qk_ref, ket_ref, ge_ref, ncb):
    gn = gn_ref[...]

    def epilogue(c, o):
        rows = slice(c * CHUNK, (c + 1) * CHUNK)
        for h in range(HEADS):
            lo = h * GDN_D
            z = qkvz_ref[rows, COL_Z + lo:COL_Z + lo + GDN_D].astype(F32)
            out = _rms(o[h], gn) * (z * _sigmoid(z))
            og_ref[rows, lo:lo + GDN_D] = out.astype(og_ref.dtype)

    s = [s_ref[h] for h in range(HEADS)]
    o_prev = None
    for c in range(ncb):
        r = [jnp.dot(lhs_ref[c, h], s[h].astype(BF16), preferred_element_type=F32) for h in range(HEADS)]
        yield
        if o_prev is not None:
            epilogue(c - 1, o_prev)
        ub = [(wv_ref[c, h] - r[h][0:CHUNK]).astype(BF16) for h in range(HEADS)]
        s = [ge_ref[c, h] * s[h] + jnp.dot(ket_ref[c, h], ub[h], preferred_element_type=F32)
             for h in range(HEADS)]
        o_prev = [r[h][CHUNK:2 * CHUNK] + jnp.dot(qk_ref[c, h], ub[h], preferred_element_type=F32)
                  for h in range(HEADS)]
        yield
    epilogue(ncb - 1, o_prev)
    for h in range(HEADS):
        s_ref[h] = s[h]


def _mlstm_block(mvo_ref, mqk_ref, gcol_ref, grow_ref, plane_ref, psub_ref, gn_ref, om_ref,
                 c_ref, n_ref, m_ref, ncb):
    ii, jj = _chunk_masks()
    incl = jj <= ii
    lane = lax.broadcasted_iota(jnp.int32, (1, 128), 1)
    row128 = lax.broadcasted_iota(jnp.int32, (128, 1), 0)
    gn = gn_ref[...]
    blane = plane_ref[2:3, :]
    bsub = psub_ref[:, 2:3]
    items = []
    for c in range(ncb):
        rows = slice(c * CHUNK, (c + 1) * CHUNK)
        gc = gcol_ref[rows, :] + blane
        gr = grow_ref[c] + bsub
        lf_c = -_softplus(-gc)
        lf_r = -_softplus(-gr)
        for p in range(HEADS // 2):
            qb = mqk_ref[rows, p * 128:(p + 1) * 128].astype(F32)
            kb = mqk_ref[rows, 256 + p * 128:256 + (p + 1) * 128].astype(F32) * (ML_DK ** -0.5)
            for e in range(2):
                h = 2 * p + e
                lm = (lane >= e * ML_DK) & (lane < (e + 1) * ML_DK)
                items.append(dict(c=c, rows=rows, p=p, e=e, h=h, qh=jnp.where(lm, qb, 0.0), kb=kb,
                                  kh=jnp.where(lm, kb, 0.0),
                                  lf_col=lf_c[:, LANE_MF + h:LANE_MF + h + 1],
                                  lf_row=lf_r[LANE_MF + h:LANE_MF + h + 1, :],
                                  ig_row=gr[LANE_MI + h:LANE_MI + h + 1, :],
                                  ig_col=gc[:, LANE_MI + h:LANE_MI + h + 1],
                                  v=mvo_ref[rows, h * ML_DV:(h + 1) * ML_DV]))
        if c % 2 == 1:
            yield
    for part in _halves(items):
        for it in part:
            it["qk"] = _mm_nt(it["qh"], it["kb"])
        yield
    for part in _halves(items):
        for it in part:
            it["f_col"], it["f_row"] = _cumsum_col_row(it["lf_col"], it["lf_row"], ii, jj)
        yield
    for part in _halves(items):
        for it in part:
            it["d_mat"] = jnp.where(incl, it["f_col"] - it["f_row"] + it["ig_row"], -jnp.inf)
            it["d_max"] = jnp.max(it["d_mat"], axis=1, keepdims=True)
        yield
    for part in _halves(items):
        for it in part:
            p0 = jnp.where(incl, jnp.exp(jnp.where(incl, it["d_mat"] - it["d_max"], 0.0)), 0.0)
            pend0 = jnp.exp(it["f_col"][CHUNK - 1:CHUNK, :] - it["f_col"] + it["ig_col"]
                            - it["d_max"][CHUNK - 1:CHUNK, :])
            it["kp0"] = it["kh"] * pend0
            it["pqk0"] = p0 * it["qk"]
        yield
    for part in _halves(items):
        for it in part:
            it["pv0"] = _mm(it["pqk0"], it["v"])
            it["rs0"] = jnp.sum(it["pqk0"], axis=-1, keepdims=True)
        yield
    for part in _halves(items):
        for it in part:
            it["cadd0"] = _mm(it["kp0"].T, it["v"])
            it["nadd0"] = jnp.sum(it["kp0"], axis=0, keepdims=True)
        yield

    m_cur = [m_ref[:, h:h + 1] for h in range(HEADS)]
    for it in items:
        h = it["h"]
        it["m_prev"] = m_cur[h]
        m_cur[h] = jnp.maximum(it["f_col"][CHUNK - 1:CHUNK, :] + m_cur[h], it["d_max"][CHUNK - 1:CHUNK, :])
    yield
    for part in _halves(items):
        for it in part:
            bcol = it["f_col"] + it["m_prev"]
            mt = jnp.maximum(bcol, it["d_max"])
            it["mt"] = mt
            it["w_prev"] = jnp.exp(bcol - mt)
            it["sc"] = jnp.exp(it["d_max"] - mt)
        yield
    c_cur = [c_ref[p] for p in range(HEADS // 2)]
    n_cur = [n_ref[p] for p in range(HEADS // 2)]
    for c in range(ncb):
        for p in range(HEADS // 2):
            pair = [it for it in items if it["c"] == c and it["p"] == p]
            w_end = [it["w_prev"][CHUNK - 1:CHUNK, :] for it in pair]
            s_end = [it["sc"][CHUNK - 1:CHUNK, :] for it in pair]
            for it in pair:
                it["c_prev"] = c_cur[p]
                it["n_prev"] = n_cur[p]
            c_cur[p] = (jnp.where(row128 < ML_DK, w_end[0], w_end[1]) * c_cur[p]
                        + s_end[0] * pair[0]["cadd0"] + s_end[1] * pair[1]["cadd0"])
            n_cur[p] = (jnp.where(lane < ML_DK, w_end[0], w_end[1]) * n_cur[p]
                        + s_end[0] * pair[0]["nadd0"] + s_end[1] * pair[1]["nadd0"])
        if c % 2 == 1:
            yield
    for part in _halves(items):
        for it in part:
            it["qc"] = _mm(it["qh"], it["c_prev"])
        yield
    for part in _halves(items):
        for it in part:
            num = it["w_prev"] * it["qc"] + it["sc"] * it["pv0"]
            den = (it["w_prev"] * jnp.sum(it["qh"] * it["n_prev"], axis=-1, keepdims=True)
                   + it["sc"] * it["rs0"])
            it["hh"] = num / jnp.maximum(jnp.abs(den), jnp.exp(-it["mt"]))
        yield
    for part in _halves(items):
        for it in part:
            h = it["h"]
            mo = mvo_ref[it["rows"], 512 + h * ML_DV:512 + (h + 1) * ML_DV].astype(F32)
            out = _sigmoid(mo) * _rms(it["hh"], gn)
            om_ref[it["rows"], h * ML_DV:(h + 1) * ML_DV] = out.astype(om_ref.dtype)
        yield
    for p in range(HEADS // 2):
        c_ref[p] = c_cur[p]
        n_ref[p] = n_cur[p]
    for h in range(HEADS):
        m_ref[:, h:h + 1] = m_cur[h]


def _chain(*gens):
    for g in gens:
        yield from g


def _conv_stage(xp_ref, yc_ref, cw_ref, tb):
    for ct in range(CONV_CH // 128):
        cols = slice(ct * 128, (ct + 1) * 128)
        w = cw_ref[:, cols]
        for r0 in range(0, tb, 128):
            acc = xp_ref[8 + r0:8 + r0 + 128, cols] * w[CONV_W - 1:CONV_W, :]
            for j in range(CONV_W - 1):
                acc = acc + xp_ref[5 + j + r0:5 + j + r0 + 128, cols] * w[j:j + 1, :]
            yc_ref[r0:r0 + 128, cols] = acc * _sigmoid(acc)
        yield


def _out_mlp_block(og_ref, om_ref, x_ref, wo_ref, g1_ref, g2_ref, wu_ref, wd_ref, g3_ref,
                   y_ref, hn_ref, acc_ref, ff_chunk, n_split):
    half = og_ref.shape[1]
    d = x_ref.shape[1]
    d_ff = wu_ref.shape[1]
    col_groups = [slice(c, c + d // n_split) for c in range(0, d, d // n_split)]
    for cg in col_groups:
        acc_ref[:, cg] = (jnp.dot(og_ref[...], wo_ref[0:half, cg], preferred_element_type=F32)
                          + jnp.dot(om_ref[...], wo_ref[half:2 * half, cg], preferred_element_type=F32))
        yield
    x1 = x_ref[...] + _rms(acc_ref[...], g1_ref[...])
    y_ref[...] = x1
    hn_ref[...] = _rms(x1, g2_ref[...]).astype(BF16)
    yield
    for c0 in range(0, d_ff, ff_chunk):
        u = jnp.dot(hn_ref[...], wu_ref[:, c0:c0 + ff_chunk], preferred_element_type=F32)
        u = jnp.square(jnp.maximum(u, 0.0)).astype(BF16)
        for cg in col_groups:
            dd = jnp.dot(u, wd_ref[c0:c0 + ff_chunk, cg], preferred_element_type=F32)
            if c0 == 0:
                acc_ref[:, cg] = dd
            else:
                acc_ref[:, cg] += dd
        yield
    y_ref[...] = y_ref[...] + _rms(acc_ref[...], g3_ref[...])
    yield


def _layer_kernel(qkvz_ref, mvo_ref, mqk_ref, gcol_ref, grow_ref, cs_ref, s0_ref, c0_ref, n0_ref, m0_ref,
                  x_ref, cw_ref, plane_ref, psub_ref, gng_ref, gnm_ref,
                  wo_ref, g1_ref, g2_ref, wu_ref, wd_ref, g3_ref,
                  y_ref, sout_ref, cout_ref, nout_ref, mout_ref,
                  xp_ref, yc_ref, s_ref, wv_ref, lhs_ref, qk_ref, ket_ref, ge_ref, c_ref, n_ref, m_ref,
                  og_ref, om_ref, hn_ref, acc_ref, *, tb, nt, n_blocks):
    g = pl.program_id(0)
    t = jnp.minimum(g, n_blocks - 1) % nt
    ncb = tb // CHUNK
    par = g % 2

    @pl.when(g == 0)
    def _():
        og_ref[1] = jnp.zeros(og_ref.shape[1:], og_ref.dtype)
        om_ref[1] = jnp.zeros(om_ref.shape[1:], om_ref.dtype)

    @pl.when(t == 0)
    def _():
        xp_ref[0:8, :] = cs_ref[...]
        s_ref[...] = s0_ref[...]
        c_ref[...] = c0_ref[...]
        n_ref[...] = n0_ref[...]
        m_ref[...] = m0_ref[...]

    @pl.when(t > 0)
    def _():
        xp_ref[0:8, :] = xp_ref[tb:tb + 8, :]

    xp_ref[8:tb + 8, :] = qkvz_ref[:, COL_QKV:COL_QKV + CONV_CH].astype(F32)

    og_w, om_w = og_ref.at[par], om_ref.at[par]
    og_r, om_r = og_ref.at[1 - par], om_ref.at[1 - par]
    _interleave(
        (_chain(_conv_stage(xp_ref, yc_ref, cw_ref, tb),
                _gdn_phase_a(yc_ref, gcol_ref, grow_ref, plane_ref, psub_ref, wv_ref, lhs_ref, qk_ref, ket_ref,
                             ge_ref, ncb),
                _gdn_phase_b(qkvz_ref, gng_ref, og_w, s_ref, wv_ref, lhs_ref, qk_ref, ket_ref, ge_ref, ncb)),
         CONV_CH // 128 + 20 + 2 * ncb),
        (_out_mlp_block(og_r, om_r, x_ref, wo_ref, g1_ref, g2_ref, wu_ref, wd_ref, g3_ref, y_ref, hn_ref, acc_ref,
                        ff_chunk=256, n_split=2), 2 + 1 + wu_ref.shape[1] // 256 + 1),
        (_mlstm_block(mvo_ref, mqk_ref, gcol_ref, grow_ref, plane_ref, psub_ref, gnm_ref, om_w,
                      c_ref, n_ref, m_ref, ncb), 21 + ncb))

    @pl.when((t == nt - 1) & (g < n_blocks))
    def _():
        sout_ref[...] = s_ref[...]
        cout_ref[...] = c_ref[...]
        nout_ref[...] = n_ref[...]
        mout_ref[...] = m_ref[...]


def _layer_prompt(proj, gates, gates_t, cs8, s0, c0, n0, m0, x2d, conv_w, plane, psub, gng, gnm,
                  wo, g1, g2, wu, wd, g3, tb):
    b, t, _ = proj.shape
    d = x2d.shape[1]
    d_ff = wu.shape[1]
    nt = t // tb
    n_blocks = b * nt
    ncb = tb // CHUNK
    hp = HEADS // 2

    def blk(g):
        gm = jnp.minimum(g, n_blocks - 1)
        return gm // nt, gm % nt

    def tok3(col):
        return lambda g: blk(g) + (col,)

    per_seq3 = lambda g: (blk(g)[0], 0, 0)
    per_seq4 = lambda g: (blk(g)[0], 0, 0, 0)
    prev_rows = lambda g: (jnp.maximum(g - 1, 0), 0)
    const = lambda g: (0, 0)
    resident = dict(pipeline_mode=pl.Buffered(1))
    return pl.pallas_call(
        functools.partial(_layer_kernel, tb=tb, nt=nt, n_blocks=n_blocks),
        grid=(n_blocks + 1,),
        in_specs=[pl.BlockSpec((None, tb, 2048), tok3(0)),
                  pl.BlockSpec((None, tb, 1024), tok3(COL_MV // 1024)),
                  pl.BlockSpec((None, tb, 512), tok3(COL_MQ // 512)),
                  pl.BlockSpec((None, tb, 128), tok3(0)),
                  pl.BlockSpec((None, ncb, 16, CHUNK), lambda g: blk(g) + (0, 0)),
                  pl.BlockSpec((None, 8, CONV_CH), per_seq3),
                  pl.BlockSpec((None, HEADS, GDN_D, GDN_D), per_seq4),
                  pl.BlockSpec((None, hp, 128, 128), per_seq4),
                  pl.BlockSpec((None, hp, 1, 128), per_seq4),
                  pl.BlockSpec((None, 1, 128), per_seq3),
                  pl.BlockSpec((tb, d), prev_rows),
                  pl.BlockSpec((CONV_W, CONV_CH), const),
                  pl.BlockSpec((3, 128), const),
                  pl.BlockSpec((16, 3), const),
                  pl.BlockSpec((1, GDN_D), const),
                  pl.BlockSpec((1, ML_DV), const),
                  pl.BlockSpec((d, d), const, **resident),
                  pl.BlockSpec((1, d), const),
                  pl.BlockSpec((1, d), const),
                  pl.BlockSpec((d, d_ff), const, **resident),
                  pl.BlockSpec((d_ff, d), const, **resident),
                  pl.BlockSpec((1, d), const)],
        out_specs=[pl.BlockSpec((tb, d), prev_rows),
                   pl.BlockSpec((None, HEADS, GDN_D, GDN_D), per_seq4),
                   pl.BlockSpec((None, hp, 128, 128), per_seq4),
                   pl.BlockSpec((None, hp, 1, 128), per_seq4),
                   pl.BlockSpec((None, 1, 128), per_seq3)],
        out_shape=[jax.ShapeDtypeStruct((b * t, d), F32),
                   jax.ShapeDtypeStruct((b, HEADS, GDN_D, GDN_D), F32),
                   jax.ShapeDtypeStruct((b, hp, 128, 128), F32),
                   jax.ShapeDtypeStruct((b, hp, 1, 128), F32),
                   jax.ShapeDtypeStruct((b, 1, 128), F32)],
        scratch_shapes=[pltpu.VMEM((tb + 8, CONV_CH), F32),
                        pltpu.VMEM((tb, CONV_CH), F32),
                        pltpu.VMEM((HEADS, GDN_D, GDN_D), F32),
                        pltpu.VMEM((ncb, HEADS, CHUNK, GDN_D), F32),
                        pltpu.VMEM((ncb, HEADS, 2 * CHUNK, GDN_D), BF16),
                        pltpu.VMEM((ncb, HEADS, CHUNK, CHUNK), BF16),
                        pltpu.VMEM((ncb, HEADS, GDN_D, CHUNK), BF16),
                        pltpu.VMEM((ncb, HEADS, 1, GDN_D), F32),
                        pltpu.VMEM((hp, 128, 128), F32),
                        pltpu.VMEM((hp, 1, 128), F32),
                        pltpu.VMEM((1, 128), F32),
                        pltpu.VMEM((2, tb, HEADS * GDN_D), BF16),
                        pltpu.VMEM((2, tb, HEADS * ML_DV), BF16),
                        pltpu.VMEM((tb, d), BF16),
                        pltpu.VMEM((tb, d), F32)],
        compiler_params=pltpu.CompilerParams(dimension_semantics=("arbitrary",),
                                             vmem_limit_bytes=VMEM_LIMIT),
        name="layer_prompt",
    )(proj, proj, proj, gates, gates_t, cs8, s0, c0, n0, m0, x2d, conv_w, plane, psub, gng, gnm,
      wo, g1, g2, wu, wd, g3)


def _decode_kernel(proj_ref, gates_ref, cs_ref, s0_ref, c0_ref, n0_ref, m0_ref, cw_ref, plane_ref, gng_ref, gnm_ref,
                   og_ref, om_ref, sout_ref, cout_ref, nout_ref, mout_ref, rowg_ref, rowm_ref, *, bb):
    lane = lax.broadcasted_iota(jnp.int32, (1, 128), 1)
    gt = gates_ref[...]
    beta_t = _sigmoid(gt)
    g_t = -jnp.exp(plane_ref[0:1, :]) * _softplus(gt + plane_ref[1:2, :])
    eg_t = jnp.exp(g_t)
    gb_t = gt + plane_ref[2:3, :]
    lf_t = -_softplus(-gb_t)

    heads = []
    for h in range(HEADS):
        cols = []
        for part in range(3):
            c0 = part * 512 + h * GDN_D
            w = cw_ref[:, c0:c0 + GDN_D]
            acc = cs_ref[:, 0, c0:c0 + GDN_D] * w[0:1, :]
            acc = acc + cs_ref[:, 1, c0:c0 + GDN_D] * w[1:2, :]
            acc = acc + cs_ref[:, 2, c0:c0 + GDN_D] * w[2:3, :]
            acc = acc + proj_ref[:, COL_QKV + c0:COL_QKV + c0 + GDN_D] * w[3:4, :]
            cols.append(acc * _sigmoid(acc))
        q, k, v = cols
        q = q * lax.rsqrt(jnp.sum(q * q, axis=-1, keepdims=True) + EPS) * (GDN_D ** -0.5)
        k = k * lax.rsqrt(jnp.sum(k * k, axis=-1, keepdims=True) + EPS)
        heads.append(dict(v=v, qk=jnp.sum(q * k, axis=-1, keepdims=True), k_t=k.T,
                          kq=jnp.concatenate([k, q], axis=0).astype(BF16),
                          beta=beta_t[:, LANE_GB + h:LANE_GB + h + 1],
                          eg=eg_t[:, LANE_GA + h:LANE_GA + h + 1]))
    items = [dict(h=h, b=b) for h in range(HEADS) for b in range(bb)]
    for it in items:
        hd, b = heads[it["h"]], it["b"]
        it["k_col"] = hd["k_t"][:, b:b + 1]
    for it in items:
        r = jnp.dot(heads[it["h"]]["kq"], s0_ref[it["b"], it["h"]].astype(BF16), preferred_element_type=F32)
        it["ks"] = r[it["b"]:it["b"] + 1, :]
        it["qs"] = r[bb + it["b"]:bb + it["b"] + 1, :]
    for it in items:
        hd, b = heads[it["h"]], it["b"]
        eg_b = hd["eg"][b:b + 1, :]
        it["eg_b"] = eg_b
        it["u"] = hd["beta"][b:b + 1, :] * (hd["v"][b:b + 1, :] - eg_b * it["ks"])
        rowg_ref[it["h"], b:b + 1, :] = eg_b * it["qs"] + hd["qk"][b:b + 1, :] * it["u"]
    for it in items:
        sout_ref[it["b"], it["h"]] = it["eg_b"] * s0_ref[it["b"], it["h"]] + it["k_col"] * it["u"]
    for h in range(HEADS):
        z = proj_ref[:, COL_Z + h * GDN_D:COL_Z + (h + 1) * GDN_D]
        out = _rms(rowg_ref[h], gng_ref[...]) * (z * _sigmoid(z))
        og_ref[:, h * GDN_D:(h + 1) * GDN_D] = out.astype(og_ref.dtype)

    heads = []
    for p in range(HEADS // 2):
        qb = proj_ref[:, COL_MQ + p * 128:COL_MQ + (p + 1) * 128]
        kb = proj_ref[:, COL_MK + p * 128:COL_MK + (p + 1) * 128] * (ML_DK ** -0.5)
        n_p = n0_ref[:, p * 128:(p + 1) * 128]
        k_t = kb.T
        qq = jnp.concatenate([jnp.where(lane < ML_DK, qb, 0.0), jnp.where(lane >= ML_DK, qb, 0.0)],
                             axis=0).astype(BF16)
        w_prev, p_in = [], []
        for e in range(2):
            h = 2 * p + e
            lm = (lane >= e * ML_DK) & (lane < (e + 1) * ML_DK)
            ig = gb_t[:, LANE_MI + h:LANE_MI + h + 1]
            lf = lf_t[:, LANE_MF + h:LANE_MF + h + 1]
            m_old = m0_ref[:, h:h + 1]
            m_new = jnp.maximum(lf + m_old, ig)
            w_prev.append(jnp.exp(lf + m_old - m_new))
            p_in.append(jnp.exp(ig - m_new))
            qk = jnp.sum(jnp.where(lm, qb * kb, 0.0), axis=-1, keepdims=True)
            qn = jnp.sum(jnp.where(lm, qb * n_p, 0.0), axis=-1, keepdims=True)
            mout_ref[:, h:h + 1] = m_new
            heads.append(dict(p=p, e=e, qq=qq, k_t=k_t, w_prev=w_prev[e], p_in=p_in[e],
                              pqk=p_in[e] * qk, wqn=w_prev[e] * qn, floor=jnp.exp(-m_new),
                              v=proj_ref[:, COL_MV + h * ML_DV:COL_MV + (h + 1) * ML_DV]))
        lo_lane = lane < ML_DK
        nout_ref[:, p * 128:(p + 1) * 128] = (jnp.where(lo_lane, w_prev[0], w_prev[1]) * n_p
                                              + jnp.where(lo_lane, p_in[0], p_in[1]) * kb)
    items = [dict(h=h, b=b) for h in range(HEADS) for b in range(bb)]
    for it in items:
        hd, b = heads[it["h"]], it["b"]
        rs = slice(hd["e"] * ML_DK, (hd["e"] + 1) * ML_DK)
        it["rs"] = rs
        it["k_col"] = hd["k_t"][rs, b:b + 1]
    qc_pair = {(p, b): jnp.dot(heads[2 * p]["qq"], c0_ref[b, p].astype(BF16), preferred_element_type=F32)
               for p in range(HEADS // 2) for b in range(bb)}
    for it in items:
        hd, b = heads[it["h"]], it["b"]
        row = hd["e"] * bb + b
        it["qc"] = qc_pair[(hd["p"], b)][row:row + 1, :]
    for it in items:
        hd, b = heads[it["h"]], it["b"]
        pqk = hd["pqk"][b:b + 1, :]
        num = hd["w_prev"][b:b + 1, :] * it["qc"] + pqk * hd["v"][b:b + 1, :]
        den = hd["wqn"][b:b + 1, :] + pqk
        rowm_ref[it["h"], b:b + 1, :] = num / jnp.maximum(jnp.abs(den), hd["floor"][b:b + 1, :])
    for it in items:
        hd, b = heads[it["h"]], it["b"]
        cout_ref[b, hd["p"], it["rs"], :] = (hd["w_prev"][b:b + 1, :] * c0_ref[b, hd["p"], it["rs"], :]
                                             + (hd["p_in"][b:b + 1, :] * it["k_col"]) * hd["v"][b:b + 1, :])
    for h in range(HEADS):
        mo = proj_ref[:, COL_MO + h * ML_DV:COL_MO + (h + 1) * ML_DV]
        out = _sigmoid(mo) * _rms(rowm_ref[h], gnm_ref[...])
        om_ref[:, h * ML_DV:(h + 1) * ML_DV] = out.astype(om_ref.dtype)
    mout_ref[:, HEADS:128] = m0_ref[:, HEADS:128]


def _decode(proj, gates, cs, s0, c0, n0, m0, conv_w, plane, gng, gnm, bb):
    b = proj.shape[0]
    hp = HEADS // 2
    return pl.pallas_call(
        functools.partial(_decode_kernel, bb=bb),
        grid=(b // bb,),
        in_specs=[pl.BlockSpec((bb, COL_GATE), lambda i: (i, 0)),
                  pl.BlockSpec((bb, 128), lambda i: (i, 0)),
                  pl.BlockSpec((bb, CONV_W - 1, CONV_CH), lambda i: (i, 0, 0)),
                  pl.BlockSpec((bb, HEADS, GDN_D, GDN_D), lambda i: (i, 0, 0, 0)),
                  pl.BlockSpec((bb, hp, 128, 128), lambda i: (i, 0, 0, 0)),
                  pl.BlockSpec((bb, HEADS * ML_DK), lambda i: (i, 0)),
                  pl.BlockSpec((bb, 128), lambda i: (i, 0)),
                  pl.BlockSpec((CONV_W, CONV_CH), lambda i: (0, 0)),
                  pl.BlockSpec((3, 128), lambda i: (0, 0)),
                  pl.BlockSpec((1, GDN_D), lambda i: (0, 0)),
                  pl.BlockSpec((1, ML_DV), lambda i: (0, 0))],
        out_specs=[pl.BlockSpec((bb, HEADS * GDN_D), lambda i: (i, 0)),
                   pl.BlockSpec((bb, HEADS * ML_DV), lambda i: (i, 0)),
                   pl.BlockSpec((bb, HEADS, GDN_D, GDN_D), lambda i: (i, 0, 0, 0)),
                   pl.BlockSpec((bb, hp, 128, 128), lambda i: (i, 0, 0, 0)),
                   pl.BlockSpec((bb, HEADS * ML_DK), lambda i: (i, 0)),
                   pl.BlockSpec((bb, 128), lambda i: (i, 0))],
        out_shape=[jax.ShapeDtypeStruct((b, HEADS * GDN_D), BF16),
                   jax.ShapeDtypeStruct((b, HEADS * ML_DV), BF16),
                   jax.ShapeDtypeStruct((b, HEADS, GDN_D, GDN_D), F32),
                   jax.ShapeDtypeStruct((b, hp, 128, 128), F32),
                   jax.ShapeDtypeStruct((b, HEADS * ML_DK), F32),
                   jax.ShapeDtypeStruct((b, 128), F32)],
        scratch_shapes=[pltpu.VMEM((HEADS, bb, 128), F32), pltpu.VMEM((HEADS, bb, 128), F32)],
        compiler_params=pltpu.CompilerParams(dimension_semantics=("arbitrary",),
                                             vmem_limit_bytes=VMEM_LIMIT),
        name="decode_step",
    )(proj, gates, cs, s0, c0, n0, m0, conv_w, plane, gng, gnm)


def _lane_vec(pairs):
    v = jnp.zeros((128,), F32)
    for off, val in pairs:
        v = v.at[off:off + HEADS].set(val.astype(F32))
    return v


def _prep_params(norm_pre_mix, w_in, conv_w, a_log, dt_bias, gdn_norm_g, b_igate, b_fgate, mlstm_norm_g,
                 w_out, norm_post_mix, norm_pre_mlp, w_up, w_down, norm_post_mlp):
    alog_v = _lane_vec([(LANE_GA, a_log[0])])
    dtb_v = _lane_vec([(LANE_GA, dt_bias[0])])
    bias_v = _lane_vec([(LANE_MI, b_igate[0]), (LANE_MF, b_fgate[0])])
    return dict(
        w_in=jnp.swapaxes(w_in[0], 0, 1), wo=w_out[0].astype(BF16), wu=w_up[0].astype(BF16),
        wd=w_down[0].astype(BF16),
        g_pre=norm_pre_mix[0][None, :], g1=norm_post_mix[0][None, :], g2=norm_pre_mlp[0][None, :],
        g3=norm_post_mlp[0][None, :], cw=conv_w[0], gng=gdn_norm_g[0][None, :], gnm=mlstm_norm_g[0][None, :],
        plane=jnp.stack([alog_v, dtb_v, bias_v]),
        psub=jnp.stack([alog_v[:16], dtb_v[:16], bias_v[:16]], axis=1),
    )


def _prompt_path(x, conv0, s0, c0, n0, m0, prm, tb, tm):
    bsz, seq, d = x.shape
    hp = HEADS // 2
    x2d = x.reshape(bsz * seq, d)
    proj, gates, gates_t = _in_proj(x2d, prm["g_pre"], prm["w_in"], tm=tm, out_dtype=BF16)
    proj = proj.reshape(bsz, seq, COL_GATE)
    gates = gates.reshape(bsz, seq, 128)
    gates_t = gates_t.reshape(bsz, seq // CHUNK, 16, CHUNK)
    cs8 = jnp.pad(conv0, ((0, 0), (8 - (CONV_W - 1), 0), (0, 0)))
    y, s_new, c_new, n_new, m_new = _layer_prompt(
        proj, gates, gates_t, cs8, s0, c0.reshape(bsz, hp, 128, 128), n0.reshape(bsz, hp, 1, 128),
        jnp.pad(m0, ((0, 0), (0, 128 - HEADS)))[:, None, :], x2d, prm["cw"], prm["plane"], prm["psub"],
        prm["gng"], prm["gnm"], prm["wo"], prm["g1"], prm["g2"], prm["wu"], prm["wd"], prm["g3"], tb=tb)
    y = y.reshape(bsz, seq, d)
    xp = jnp.concatenate([conv0, proj[:, seq - (CONV_W - 1):, COL_QKV:COL_QKV + CONV_CH].astype(F32)], axis=1)
    conv_new = xp[:, -(CONV_W - 1):]
    return y, (conv_new, s_new, c_new.reshape(bsz, HEADS, ML_DK, ML_DV), n_new.reshape(bsz, HEADS, ML_DK),
               m_new[:, 0, :HEADS])


def _sample_path(x, conv0, s0, c0, n0, m0, prm):
    dec, _, d = x.shape
    hp = HEADS // 2
    xs = x.reshape(dec, d)
    proj, gates, _ = _in_proj(xs, prm["g_pre"], prm["w_in"], tm=dec, out_dtype=F32)
    og, om, s_new, c_new, n_new, m_new = _decode(
        proj, gates, conv0, s0, c0.reshape(dec, hp, 128, 128), n0.reshape(dec, HEADS * ML_DK),
        jnp.pad(m0, ((0, 0), (0, 128 - HEADS))), prm["cw"], prm["plane"], prm["gng"], prm["gnm"], bb=8)
    y = _out_mlp(og, om, xs, prm["wo"], prm["g1"], prm["g2"], prm["wu"], prm["wd"], prm["g3"],
                 tm=dec).reshape(dec, 1, d)
    conv_new = jnp.concatenate([conv0[:, 1:, :], proj[:, None, COL_QKV:COL_QKV + CONV_CH]], axis=1)
    return y, (conv_new, s_new, c_new.reshape(dec, HEADS, ML_DK, ML_DV), n_new.reshape(dec, HEADS, ML_DK),
               m_new[:, :HEADS])


def kernel(x_prompt, x_sample, state_gdn_conv, state_gdn_S, state_mlstm_C, state_mlstm_n, state_mlstm_m,
           norm_pre_mix, w_in, conv_w, a_log, dt_bias, gdn_norm_g, b_igate, b_fgate, mlstm_norm_g, w_out,
           norm_post_mix, norm_pre_mlp, w_up, w_down, norm_post_mlp):
    bsz = x_prompt.shape[0]
    prm = _prep_params(norm_pre_mix, w_in, conv_w, a_log, dt_bias, gdn_norm_g, b_igate, b_fgate, mlstm_norm_g,
                       w_out, norm_post_mix, norm_pre_mlp, w_up, w_down, norm_post_mlp)
    y_p, p_st = _prompt_path(
        x_prompt, jnp.zeros((bsz, CONV_W - 1, CONV_CH), F32), jnp.zeros((bsz, HEADS, GDN_D, GDN_D), F32),
        jnp.zeros((bsz, HEADS, ML_DK, ML_DV), F32), jnp.zeros((bsz, HEADS, ML_DK), F32),
        jnp.zeros((bsz, HEADS), F32), prm, tb=256, tm=512)
    y_s, s_st = _sample_path(x_sample, state_gdn_conv[0], state_gdn_S[0], state_mlstm_C[0], state_mlstm_n[0],
                             state_mlstm_m[0], prm)
    return (y_p, y_s) + tuple(a[None] for a in p_st) + tuple(a[None] for a in s_st)
```

```python
import functools

import jax
import jax.numpy as jnp
from jax import lax
from jax.experimental import pallas as pl
from jax.experimental.pallas import tpu as pltpu

F32 = jnp.float32
BF16 = jnp.bfloat16
EPS = 1e-6

HEADS = 4
GDN_D = 128
ML_DK = 64
ML_DV = 128
CONV_W = 4
CONV_CH = 3 * HEADS * GDN_D
CHUNK = 64

COL_QKV = 0
COL_Z = 1536
COL_MV = 2048
COL_MO = 2560
COL_MQ = 3072
COL_MK = 3328
COL_GATE = 3584
N_PROJ = COL_GATE + 128
LANE_GB, LANE_GA, LANE_MI, LANE_MF = 0, 4, 8, 12

V7X_VMEM_BYTES = 64 * 1024 * 1024
VMEM_LIMIT = V7X_VMEM_BYTES - 8 * 1024 * 1024


def _rms(x, g):
    return x * lax.rsqrt(jnp.mean(x * x, axis=-1, keepdims=True) + EPS) * g


def _softplus(x):
    return jnp.maximum(x, 0.0) + jnp.log1p(jnp.exp(-jnp.abs(x)))


def _sigmoid(x):
    return 1.0 / (1.0 + jnp.exp(-x))


def _mm(a, b):
    return jnp.dot(a.astype(BF16), b.astype(BF16), preferred_element_type=F32)


def _mm_nt(a, b):
    return lax.dot_general(a.astype(BF16), b.astype(BF16), (((1,), (1,)), ((), ())),
                           preferred_element_type=F32)


W_IN_COLS = 3600
W_IN_SPANS = ((0, COL_QKV, 2048),
              (2568, COL_MV, 1024),
              (2056, COL_MQ, 512),
              (2048, COL_GATE, 8),
              (3592, COL_GATE + 8, 8))


def _in_proj_kernel(x_ref, g_ref, win_ref, o_ref, gc_ref, gt_ref, h_ref, w_ref, *, n_chunk):
    @pl.when(pl.program_id(0) == 0)
    def _():
        w_ref[COL_GATE:N_PROJ, :] = jnp.zeros((N_PROJ - COL_GATE, w_ref.shape[1]), BF16)
        for src, dst, width in W_IN_SPANS:
            for r0 in range(0, width, 512):
                r1 = min(r0 + 512, width)
                w_ref[dst + r0:dst + r1, :] = win_ref[src + r0:src + r1, :].astype(BF16)

    h_ref[...] = _rms(x_ref[...], g_ref[...]).astype(BF16)
    n = o_ref.shape[1]
    for c0 in range(0, n, n_chunk):
        c1 = min(c0 + n_chunk, n)
        o_ref[:, c0:c1] = _mm_nt(h_ref[...], w_ref[c0:c1, :]).astype(o_ref.dtype)
    gates = _mm_nt(h_ref[...], w_ref[COL_GATE:COL_GATE + 128, :])
    gc_ref[...] = gates
    gt = gates.T
    for c in range(gt_ref.shape[0]):
        gt_ref[c] = gt[0:16, c * CHUNK:(c + 1) * CHUNK]


def _in_proj(x2d, g, w, tm, out_dtype):
    m, k = x2d.shape
    return pl.pallas_call(
        functools.partial(_in_proj_kernel, n_chunk=512),
        grid=(m // tm,),
        in_specs=[pl.BlockSpec((tm, k), lambda i: (i, 0)),
                  pl.BlockSpec((1, k), lambda i: (0, 0)),
                  pl.BlockSpec((W_IN_COLS, k), lambda i: (0, 0), pipeline_mode=pl.Buffered(1))],
        out_specs=[pl.BlockSpec((tm, COL_GATE), lambda i: (i, 0)),
                   pl.BlockSpec((tm, 128), lambda i: (i, 0)),
                   pl.BlockSpec((tm // CHUNK, 16, CHUNK), lambda i: (i, 0, 0))],
        out_shape=[jax.ShapeDtypeStruct((m, COL_GATE), out_dtype),
                   jax.ShapeDtypeStruct((m, 128), F32),
                   jax.ShapeDtypeStruct((m // CHUNK, 16, CHUNK), F32)],
        scratch_shapes=[pltpu.VMEM((tm, k), BF16), pltpu.VMEM((N_PROJ, k), BF16)],
        compiler_params=pltpu.CompilerParams(dimension_semantics=("arbitrary",),
                                             vmem_limit_bytes=VMEM_LIMIT),
        name="in_proj",
    )(x2d, g, w)


def _out_mlp_kernel(og_ref, om_ref, x_ref, wo_ref, g1_ref, g2_ref, wu_ref, wd_ref, g3_ref,
                    y_ref, hn_ref, acc_ref, *, ff_chunk):
    half = og_ref.shape[1]
    mix = (jnp.dot(og_ref[...], wo_ref[0:half, :], preferred_element_type=F32)
           + jnp.dot(om_ref[...], wo_ref[half:2 * half, :], preferred_element_type=F32))
    x1 = x_ref[...] + _rms(mix, g1_ref[...])
    y_ref[...] = x1
    hn_ref[...] = _rms(x1, g2_ref[...]).astype(BF16)
    d_ff = wu_ref.shape[1]
    for c0 in range(0, d_ff, ff_chunk):
        u = jnp.dot(hn_ref[...], wu_ref[:, c0:c0 + ff_chunk], preferred_element_type=F32)
        u = jnp.square(jnp.maximum(u, 0.0)).astype(BF16)
        d = jnp.dot(u, wd_ref[c0:c0 + ff_chunk, :], preferred_element_type=F32)
        if c0 == 0:
            acc_ref[...] = d
        else:
            acc_ref[...] += d
    y_ref[...] = y_ref[...] + _rms(acc_ref[...], g3_ref[...])


def _out_mlp(og, om, x2d, wo, g1, g2, wu, wd, g3, tm):
    m, d = x2d.shape
    half = og.shape[1]
    d_ff = wu.shape[1]
    const = lambda i: (0, 0)
    return pl.pallas_call(
        functools.partial(_out_mlp_kernel, ff_chunk=1024),
        grid=(m // tm,),
        in_specs=[pl.BlockSpec((tm, half), lambda i: (i, 0)),
                  pl.BlockSpec((tm, half), lambda i: (i, 0)),
                  pl.BlockSpec((tm, d), lambda i: (i, 0)),
                  pl.BlockSpec((d, d), const, pipeline_mode=pl.Buffered(1)),
                  pl.BlockSpec((1, d), const),
                  pl.BlockSpec((1, d), const),
                  pl.BlockSpec((d, d_ff), const, pipeline_mode=pl.Buffered(1)),
                  pl.BlockSpec((d_ff, d), const, pipeline_mode=pl.Buffered(1)),
                  pl.BlockSpec((1, d), const)],
        out_specs=pl.BlockSpec((tm, d), lambda i: (i, 0)),
        out_shape=jax.ShapeDtypeStruct((m, d), F32),
        scratch_shapes=[pltpu.VMEM((tm, d), BF16), pltpu.VMEM((tm, d), F32)],
        compiler_params=pltpu.CompilerParams(dimension_semantics=("arbitrary",),
                                             vmem_limit_bytes=VMEM_LIMIT),
        name="out_mlp",
    )(og, om, x2d, wo, g1, g2, wu, wd, g3)


def _chunk_masks():
    ii = lax.broadcasted_iota(jnp.int32, (CHUNK, CHUNK), 0)
    jj = lax.broadcasted_iota(jnp.int32, (CHUNK, CHUNK), 1)
    return ii, jj


def _cumsum_col_row(x_col, x_row, ii, jj):
    c_col = jnp.sum(jnp.where(jj <= ii, x_row, 0.0), axis=1, keepdims=True)
    c_row = jnp.sum(jnp.where(ii <= jj, x_col, 0.0), axis=0, keepdims=True)
    return c_col, c_row


def _interleave(*tasks):
    live = [[g, n, 0] for g, n in tasks]
    while live:
        entry = min(live, key=lambda e: (e[2] + 1) / e[1])
        try:
            next(entry[0])
            entry[2] += 1
        except StopIteration:
            live.remove(entry)


def _halves(items):
    mid = len(items) // 2
    return items[:mid], items[mid:]


def _gdn_phase_a(yc_ref, gcol_ref, grow_ref, plane_ref, psub_ref, wv_ref, lhs_ref, qk_ref, ket_ref, ge_ref,
                 ncb):
    ii, jj = _chunk_masks()
    incl = jj <= ii
    strict = jj < ii
    eye = (ii == jj).astype(F32)
    neg_a_lane = -jnp.exp(plane_ref[0:1, :])
    dtb_lane = plane_ref[1:2, :]
    neg_a_sub = -jnp.exp(psub_ref[:, 0:1])
    dtb_sub = psub_ref[:, 1:2]
    items = []
    for c in range(ncb):
        rows = slice(c * CHUNK, (c + 1) * CHUNK)
        gc = gcol_ref[rows, :]
        gr = grow_ref[c]
        beta_t = _sigmoid(gc)
        g_t = neg_a_lane * _softplus(gc + dtb_lane)
        g_r = neg_a_sub * _softplus(gr + dtb_sub)
        for h in range(HEADS):
            lo = h * GDN_D
            items.append(dict(c=c, h=h,
                              q=yc_ref[rows, lo:lo + GDN_D],
                              k=yc_ref[rows, 512 + lo:512 + lo + GDN_D],
                              v=yc_ref[rows, 1024 + lo:1024 + lo + GDN_D],
                              beta=beta_t[:, LANE_GB + h:LANE_GB + h + 1],
                              gg_col=g_t[:, LANE_GA + h:LANE_GA + h + 1],
                              gg_row=g_r[LANE_GA + h:LANE_GA + h + 1, :]))
    yield
    for it in items:
        it["k"] = it["k"] * lax.rsqrt(jnp.sum(it["k"] * it["k"], axis=-1, keepdims=True) + EPS)
    yield
    for it in items:
        it["kk"] = _mm_nt(it["k"], it["k"])
    yield
    for it in items:
        it["q"] = (it["q"] * lax.rsqrt(jnp.sum(it["q"] * it["q"], axis=-1, keepdims=True) + EPS)
                   * (GDN_D ** -0.5))
    yield
    for it in items:
        it["g_col"], g_row = _cumsum_col_row(it["gg_col"], it["gg_row"], ii, jj)
        it["decay"] = jnp.where(incl, jnp.exp(jnp.where(incl, it["g_col"] - g_row, 0.0)), 0.0)
    yield
    for it in items:
        n_mat = jnp.where(strict, it["beta"] * it["kk"] * it["decay"], 0.0)
        it["x"] = eye - n_mat
        it["p"] = -n_mat
    yield
    for _ in range(5):
        yield
        for it in items:
            it["p"] = _mm(it["p"], it["p"])
        yield
        for it in items:
            it["x"] = it["x"] + _mm(it["x"], it["p"])
    yield
    for it in items:
        e_g = jnp.exp(it["g_col"])
        it["e_g"] = e_g
        rhs = jnp.concatenate([it["beta"] * it["v"], (it["beta"] * e_g) * it["k"]], axis=1)
        it["w"] = _mm(it["x"], rhs)
    yield
    for it in items:
        it["qk"] = _mm_nt(it["q"], it["k"]) * it["decay"]
    yield
    for it in items:
        c, h = it["c"], it["h"]
        g_end = it["g_col"][CHUNK - 1:CHUNK, :]
        k_end = it["k"] * jnp.exp(g_end - it["g_col"])
        wv_ref[c, h] = it["w"][:, 0:GDN_D]
        lhs_ref[c, h] = jnp.concatenate([it["w"][:, GDN_D:2 * GDN_D], it["e_g"] * it["q"]],
                                        axis=0).astype(BF16)
        qk_ref[c, h] = it["qk"].astype(BF16)
        ket_ref[c, h] = k_end.T.astype(BF16)
        ge_ref[c, h] = jnp.broadcast_to(jnp.exp(g_end), (1, GDN_D))


def _gdn_phase_b(qkvz_ref, gn_ref, og_ref, s_ref, wv_ref, lhs_ref, qk_ref, ket_ref, ge_ref, ncb):
    gn = gn_ref[...]

    def epilogue(c, o):
        rows = slice(c * CHUNK, (c + 1) * CHUNK)
        for h in range(HEADS):
            lo = h * GDN_D
            z = qkvz_ref[rows, COL_Z + lo:COL_Z + lo + GDN_D].astype(F32)
            out = _rms(o[h], gn) * (z * _sigmoid(z))
            og_ref[rows, lo:lo + GDN_D] = out.astype(og_ref.dtype)

    s = [s_ref[h] for h in range(HEADS)]
    o_prev = None
    for c in range(ncb):
        r = [jnp.dot(lhs_ref[c, h], s[h].astype(BF16), preferred_element_type=F32) for h in range(HEADS)]
        yield
        if o_prev is not None:
            epilogue(c - 1, o_prev)
        ub = [(wv_ref[c, h] - r[h][0:CHUNK]).astype(BF16) for h in range(HEADS)]
        s = [ge_ref[c, h] * s[h] + jnp.dot(ket_ref[c, h], ub[h], preferred_element_type=F32)
             for h in range(HEADS)]
        o_prev = [r[h][CHUNK:2 * CHUNK] + jnp.dot(qk_ref[c, h], ub[h], preferred_element_type=F32)
                  for h in range(HEADS)]
        yield
    epilogue(ncb - 1, o_prev)
    for h in range(HEADS):
        s_ref[h] = s[h]


def _mlstm_block(mvo_ref, mqk_ref, gcol_ref, grow_ref, plane_ref, psub_ref, gn_ref, om_ref,
                 c_ref, n_ref, m_ref, ncb):
    ii, jj = _chunk_masks()
    incl = jj <= ii
    lane = lax.broadcasted_iota(jnp.int32, (1, 128), 1)
    row128 = lax.broadcasted_iota(jnp.int32, (128, 1), 0)
    gn = gn_ref[...]
    blane = plane_ref[2:3, :]
    bsub = psub_ref[:, 2:3]
    items = []
    for c in range(ncb):
        rows = slice(c * CHUNK, (c + 1) * CHUNK)
        gc = gcol_ref[rows, :] + blane
        gr = grow_ref[c] + bsub
        lf_c = -_softplus(-gc)
        lf_r = -_softplus(-gr)
        for p in range(HEADS // 2):
            qb = mqk_ref[rows, p * 128:(p + 1) * 128].astype(F32)
            kb = mqk_ref[rows, 256 + p * 128:256 + (p + 1) * 128].astype(F32) * (ML_DK ** -0.5)
            for e in range(2):
                h = 2 * p + e
                lm = (lane >= e * ML_DK) & (lane < (e + 1) * ML_DK)
                items.append(dict(c=c, rows=rows, p=p, e=e, h=h, qh=jnp.where(lm, qb, 0.0), kb=kb,
                                  kh=jnp.where(lm, kb, 0.0),
                                  lf_col=lf_c[:, LANE_MF + h:LANE_MF + h + 1],
                                  lf_row=lf_r[LANE_MF + h:LANE_MF + h + 1, :],
                                  ig_row=gr[LANE_MI + h:LANE_MI + h + 1, :],
                                  ig_col=gc[:, LANE_MI + h:LANE_MI + h + 1],
                                  v=mvo_ref[rows, h * ML_DV:(h + 1) * ML_DV]))
        if c % 2 == 1:
            yield
    for part in _halves(items):
        for it in part:
            it["qk"] = _mm_nt(it["qh"], it["kb"])
        yield
    for part in _halves(items):
        for it in part:
            it["f_col"], it["f_row"] = _cumsum_col_row(it["lf_col"], it["lf_row"], ii, jj)
        yield
    for part in _halves(items):
        for it in part:
            it["d_mat"] = jnp.where(incl, it["f_col"] - it["f_row"] + it["ig_row"], -jnp.inf)
            it["d_max"] = jnp.max(it["d_mat"], axis=1, keepdims=True)
        yield
    for part in _halves(items):
        for it in part:
            p0 = jnp.where(incl, jnp.exp(jnp.where(incl, it["d_mat"] - it["d_max"], 0.0)), 0.0)
            pend0 = jnp.exp(it["f_col"][CHUNK - 1:CHUNK, :] - it["f_col"] + it["ig_col"]
                            - it["d_max"][CHUNK - 1:CHUNK, :])
            it["kp0"] = it["kh"] * pend0
            it["pqk0"] = p0 * it["qk"]
        yield
    for part in _halves(items):
        for it in part:
            it["pv0"] = _mm(it["pqk0"], it["v"])
            it["rs0"] = jnp.sum(it["pqk0"], axis=-1, keepdims=True)
        yield
    for part in _halves(items):
        for it in part:
            it["cadd0"] = _mm(it["kp0"].T, it["v"])
            it["nadd0"] = jnp.sum(it["kp0"], axis=0, keepdims=True)
        yield

    m_cur = [m_ref[:, h:h + 1] for h in range(HEADS)]
    for it in items:
        h = it["h"]
        it["m_prev"] = m_cur[h]
        m_cur[h] = jnp.maximum(it["f_col"][CHUNK - 1:CHUNK, :] + m_cur[h], it["d_max"][CHUNK - 1:CHUNK, :])
    yield
    for part in _halves(items):
        for it in part:
            bcol = it["f_col"] + it["m_prev"]
            mt = jnp.maximum(bcol, it["d_max"])
            it["mt"] = mt
            it["w_prev"] = jnp.exp(bcol - mt)
            it["sc"] = jnp.exp(it["d_max"] - mt)
        yield
    c_cur = [c_ref[p] for p in range(HEADS // 2)]
    n_cur = [n_ref[p] for p in range(HEADS // 2)]
    for c in range(ncb):
        for p in range(HEADS // 2):
            pair = [it for it in items if it["c"] == c and it["p"] == p]
            w_end = [it["w_prev"][CHUNK - 1:CHUNK, :] for it in pair]
            s_end = [it["sc"][CHUNK - 1:CHUNK, :] for it in pair]
            for it in pair:
                it["c_prev"] = c_cur[p]
                it["n_prev"] = n_cur[p]
            c_cur[p] = (jnp.where(row128 < ML_DK, w_end[0], w_end[1]) * c_cur[p]
                        + s_end[0] * pair[0]["cadd0"] + s_end[1] * pair[1]["cadd0"])
            n_cur[p] = (jnp.where(lane < ML_DK, w_end[0], w_end[1]) * n_cur[p]
                        + s_end[0] * pair[0]["nadd0"] + s_end[1] * pair[1]["nadd0"])
        if c % 2 == 1:
            yield
    for part in _halves(items):
        for it in part:
            it["qc"] = _mm(it["qh"], it["c_prev"])
        yield
    for part in _halves(items):
        for it in part:
            num = it["w_prev"] * it["qc"] + it["sc"] * it["pv0"]
            den = (it["w_prev"] * jnp.sum(it["qh"] * it["n_prev"], axis=-1, keepdims=True)
                   + it["sc"] * it["rs0"])
            it["hh"] = num / jnp.maximum(jnp.abs(den), jnp.exp(-it["mt"]))
        yield
    for part in _halves(items):
        for it in part:
            h = it["h"]
            mo = mvo_ref[it["rows"], 512 + h * ML_DV:512 + (h + 1) * ML_DV].astype(F32)
            out = _sigmoid(mo) * _rms(it["hh"], gn)
            om_ref[it["rows"], h * ML_DV:(h + 1) * ML_DV] = out.astype(om_ref.dtype)
        yield
    for p in range(HEADS // 2):
        c_ref[p] = c_cur[p]
        n_ref[p] = n_cur[p]
    for h in range(HEADS):
        m_ref[:, h:h + 1] = m_cur[h]


def _chain(*gens):
    for g in gens:
        yield from g


def _conv_stage(xp_ref, yc_ref, cw_ref, tb):
    for ct in range(CONV_CH // 128):
        cols = slice(ct * 128, (ct + 1) * 128)
        w = cw_ref[:, cols]
        for r0 in range(0, tb, 128):
            acc = xp_ref[8 + r0:8 + r0 + 128, cols] * w[CONV_W - 1:CONV_W, :]
            for j in range(CONV_W - 1):
                acc = acc + xp_ref[5 + j + r0:5 + j + r0 + 128, cols] * w[j:j + 1, :]
            yc_ref[r0:r0 + 128, cols] = acc * _sigmoid(acc)
        yield


def _out_mlp_block(og_ref, om_ref, x_ref, wo_ref, g1_ref, g2_ref, wu_ref, wd_ref, g3_ref,
                   y_ref, hn_ref, acc_ref, ff_chunk, n_split):
    half = og_ref.shape[1]
    d = x_ref.shape[1]
    d_ff = wu_ref.shape[1]
    col_groups = [slice(c, c + d // n_split) for c in range(0, d, d // n_split)]
    for cg in col_groups:
        acc_ref[:, cg] = (jnp.dot(og_ref[...], wo_ref[0:half, cg], preferred_element_type=F32)
                          + jnp.dot(om_ref[...], wo_ref[half:2 * half, cg], preferred_element_type=F32))
        yield
    x1 = x_ref[...] + _rms(acc_ref[...], g1_ref[...])
    y_ref[...] = x1
    hn_ref[...] = _rms(x1, g2_ref[...]).astype(BF16)
    yield
    for c0 in range(0, d_ff, ff_chunk):
        u = jnp.dot(hn_ref[...], wu_ref[:, c0:c0 + ff_chunk], preferred_element_type=F32)
        u = jnp.square(jnp.maximum(u, 0.0)).astype(BF16)
        for cg in col_groups:
            dd = jnp.dot(u, wd_ref[c0:c0 + ff_chunk, cg], preferred_element_type=F32)
            if c0 == 0:
                acc_ref[:, cg] = dd
            else:
                acc_ref[:, cg] += dd
        yield
    y_ref[...] = y_ref[...] + _rms(acc_ref[...], g3_ref[...])
    yield


def _layer_kernel(qkvz_ref, mvo_ref, mqk_ref, gcol_ref, grow_ref, cs_ref, s0_ref, c0_ref, n0_ref, m0_ref,
                  x_ref, cw_ref, plane_ref, psub_ref, gng_ref, gnm_ref,
                  wo_ref, g1_ref, g2_ref, wu_ref, wd_ref, g3_ref,
                  y_ref, sout_ref, cout_ref, nout_ref, mout_ref,
                  xp_ref, yc_ref, s_ref, wv_ref, lhs_ref, qk_ref, ket_ref, ge_ref, c_ref, n_ref, m_ref,
                  og_ref, om_ref, hn_ref, acc_ref, *, tb, nt, n_blocks):
    g = pl.program_id(0)
    t = jnp.minimum(g, n_blocks - 1) % nt
    ncb = tb // CHUNK
    par = g % 2

    @pl.when(g == 0)
    def _():
        og_ref[1] = jnp.zeros(og_ref.shape[1:], og_ref.dtype)
        om_ref[1] = jnp.zeros(om_ref.shape[1:], om_ref.dtype)

    @pl.when(t == 0)
    def _():
        xp_ref[0:8, :] = cs_ref[...]
        s_ref[...] = s0_ref[...]
        c_ref[...] = c0_ref[...]
        n_ref[...] = n0_ref[...]
        m_ref[...] = m0_ref[...]

    @pl.when(t > 0)
    def _():
        xp_ref[0:8, :] = xp_ref[tb:tb + 8, :]

    xp_ref[8:tb + 8, :] = qkvz_ref[:, COL_QKV:COL_QKV + CONV_CH].astype(F32)

    og_w, om_w = og_ref.at[par], om_ref.at[par]
    og_r, om_r = og_ref.at[1 - par], om_ref.at[1 - par]
    _interleave(
        (_chain(_conv_stage(xp_ref, yc_ref, cw_ref, tb),
                _gdn_phase_a(yc_ref, gcol_ref, grow_ref, plane_ref, psub_ref, wv_ref, lhs_ref, qk_ref, ket_ref,
                             ge_ref, ncb),
                _gdn_phase_b(qkvz_ref, gng_ref, og_w, s_ref, wv_ref, lhs_ref, qk_ref, ket_ref, ge_ref, ncb)),
         CONV_CH // 128 + 20 + 2 * ncb),
        (_out_mlp_block(og_r, om_r, x_ref, wo_ref, g1_ref, g2_ref, wu_ref, wd_ref, g3_ref, y_ref, hn_ref, acc_ref,
                        ff_chunk=256, n_split=2), 2 + 1 + wu_ref.shape[1] // 256 + 1),
        (_mlstm_block(mvo_ref, mqk_ref, gcol_ref, grow_ref, plane_ref, psub_ref, gnm_ref, om_w,
                      c_ref, n_ref, m_ref, ncb), 21 + ncb))

    @pl.when((t == nt - 1) & (g < n_blocks))
    def _():
        sout_ref[...] = s_ref[...]
        cout_ref[...] = c_ref[...]
        nout_ref[...] = n_ref[...]
        mout_ref[...] = m_ref[...]


def _layer_prompt(proj, gates, gates_t, cs8, s0, c0, n0, m0, x2d, conv_w, plane, psub, gng, gnm,
                  wo, g1, g2, wu, wd, g3, tb):
    b, t, _ = proj.shape
    d = x2d.shape[1]
    d_ff = wu.shape[1]
    nt = t // tb
    n_blocks = b * nt
    ncb = tb // CHUNK
    hp = HEADS // 2

    def blk(g):
        gm = jnp.minimum(g, n_blocks - 1)
        return gm // nt, gm % nt

    def tok3(col):
        return lambda g: blk(g) + (col,)

    per_seq3 = lambda g: (blk(g)[0], 0, 0)
    per_seq4 = lambda g: (blk(g)[0], 0, 0, 0)
    prev_rows = lambda g: (jnp.maximum(g - 1, 0), 0)
    const = lambda g: (0, 0)
    resident = dict(pipeline_mode=pl.Buffered(1))
    return pl.pallas_call(
        functools.partial(_layer_kernel, tb=tb, nt=nt, n_blocks=n_blocks),
        grid=(n_blocks + 1,),
        in_specs=[pl.BlockSpec((None, tb, 2048), tok3(0)),
                  pl.BlockSpec((None, tb, 1024), tok3(COL_MV // 1024)),
                  pl.BlockSpec((None, tb, 512), tok3(COL_MQ // 512)),
                  pl.BlockSpec((None, tb, 128), tok3(0)),
                  pl.BlockSpec((None, ncb, 16, CHUNK), lambda g: blk(g) + (0, 0)),
                  pl.BlockSpec((None, 8, CONV_CH), per_seq3),
                  pl.BlockSpec((None, HEADS, GDN_D, GDN_D), per_seq4),
                  pl.BlockSpec((None, hp, 128, 128), per_seq4),
                  pl.BlockSpec((None, hp, 1, 128), per_seq4),
                  pl.BlockSpec((None, 1, 128), per_seq3),
                  pl.BlockSpec((tb, d), prev_rows),
                  pl.BlockSpec((CONV_W, CONV_CH), const),
                  pl.BlockSpec((3, 128), const),
                  pl.BlockSpec((16, 3), const),
                  pl.BlockSpec((1, GDN_D), const),
                  pl.BlockSpec((1, ML_DV), const),
                  pl.BlockSpec((d, d), const, **resident),
                  pl.BlockSpec((1, d), const),
                  pl.BlockSpec((1, d), const),
                  pl.BlockSpec((d, d_ff), const, **resident),
                  pl.BlockSpec((d_ff, d), const, **resident),
                  pl.BlockSpec((1, d), const)],
        out_specs=[pl.BlockSpec((tb, d), prev_rows),
                   pl.BlockSpec((None, HEADS, GDN_D, GDN_D), per_seq4),
                   pl.BlockSpec((None, hp, 128, 128), per_seq4),
                   pl.BlockSpec((None, hp, 1, 128), per_seq4),
                   pl.BlockSpec((None, 1, 128), per_seq3)],
        out_shape=[jax.ShapeDtypeStruct((b * t, d), F32),
                   jax.ShapeDtypeStruct((b, HEADS, GDN_D, GDN_D), F32),
                   jax.ShapeDtypeStruct((b, hp, 128, 128), F32),
                   jax.ShapeDtypeStruct((b, hp, 1, 128), F32),
                   jax.ShapeDtypeStruct((b, 1, 128), F32)],
        scratch_shapes=[pltpu.VMEM((tb + 8, CONV_CH), F32),
                        pltpu.VMEM((tb, CONV_CH), F32),
                        pltpu.VMEM((HEADS, GDN_D, GDN_D), F32),
                        pltpu.VMEM((ncb, HEADS, CHUNK, GDN_D), F32),
                        pltpu.VMEM((ncb, HEADS, 2 * CHUNK, GDN_D), BF16),
                        pltpu.VMEM((ncb, HEADS, CHUNK, CHUNK), BF16),
                        pltpu.VMEM((ncb, HEADS, GDN_D, CHUNK), BF16),
                        pltpu.VMEM((ncb, HEADS, 1, GDN_D), F32),
                        pltpu.VMEM((hp, 128, 128), F32),
                        pltpu.VMEM((hp, 1, 128), F32),
                        pltpu.VMEM((1, 128), F32),
                        pltpu.VMEM((2, tb, HEADS * GDN_D), BF16),
                        pltpu.VMEM((2, tb, HEADS * ML_DV), BF16),
                        pltpu.VMEM((tb, d), BF16),
                        pltpu.VMEM((tb, d), F32)],
        compiler_params=pltpu.CompilerParams(dimension_semantics=("arbitrary",),
                                             vmem_limit_bytes=VMEM_LIMIT),
        name="layer_prompt",
    )(proj, proj, proj, gates, gates_t, cs8, s0, c0, n0, m0, x2d, conv_w, plane, psub, gng, gnm,
      wo, g1, g2, wu, wd, g3)


def _decode_kernel(proj_ref, gates_ref, cs_ref, s0_ref, c0_ref, n0_ref, m0_ref, cw_ref, plane_ref, gng_ref, gnm_ref,
                   og_ref, om_ref, sout_ref, cout_ref, nout_ref, mout_ref, rowg_ref, rowm_ref, *, bb):
    lane = lax.broadcasted_iota(jnp.int32, (1, 128), 1)
    gt = gates_ref[...]
    beta_t = _sigmoid(gt)
    g_t = -jnp.exp(plane_ref[0:1, :]) * _softplus(gt + plane_ref[1:2, :])
    eg_t = jnp.exp(g_t)
    gb_t = gt + plane_ref[2:3, :]
    lf_t = -_softplus(-gb_t)

    heads = []
    for h in range(HEADS):
        cols = []
        for part in range(3):
            c0 = part * 512 + h * GDN_D
            w = cw_ref[:, c0:c0 + GDN_D]
            acc = cs_ref[:, 0, c0:c0 + GDN_D] * w[0:1, :]
            acc = acc + cs_ref[:, 1, c0:c0 + GDN_D] * w[1:2, :]
            acc = acc + cs_ref[:, 2, c0:c0 + GDN_D] * w[2:3, :]
            acc = acc + proj_ref[:, COL_QKV + c0:COL_QKV + c0 + GDN_D] * w[3:4, :]
            cols.append(acc * _sigmoid(acc))
        q, k, v = cols
        q = q * lax.rsqrt(jnp.sum(q * q, axis=-1, keepdims=True) + EPS) * (GDN_D ** -0.5)
        k = k * lax.rsqrt(jnp.sum(k * k, axis=-1, keepdims=True) + EPS)
        heads.append(dict(v=v, qk=jnp.sum(q * k, axis=-1, keepdims=True), k_t=k.T.astype(BF16),
                          kq=jnp.concatenate([k, q], axis=0).astype(BF16),
                          beta=beta_t[:, LANE_GB + h:LANE_GB + h + 1],
                          eg=eg_t[:, LANE_GA + h:LANE_GA + h + 1]))
    items = [dict(h=h, b=b) for h in range(HEADS) for b in range(bb)]
    row_b = lax.broadcasted_iota(jnp.int32, (bb, 1), 0)
    for it in items:
        r = jnp.dot(heads[it["h"]]["kq"], s0_ref[it["b"], it["h"]].astype(BF16), preferred_element_type=F32)
        it["ks"] = r[it["b"]:it["b"] + 1, :]
        it["qs"] = r[bb + it["b"]:bb + it["b"] + 1, :]
    for it in items:
        hd, b = heads[it["h"]], it["b"]
        eg_b = hd["eg"][b:b + 1, :]
        it["eg_b"] = eg_b
        it["u"] = hd["beta"][b:b + 1, :] * (hd["v"][b:b + 1, :] - eg_b * it["ks"])
        rowg_ref[it["h"], b:b + 1, :] = eg_b * it["qs"] + hd["qk"][b:b + 1, :] * it["u"]
    for it in items:
        u_sel = jnp.where(row_b == it["b"], it["u"], 0.0).astype(BF16)
        upd = jnp.dot(heads[it["h"]]["k_t"], u_sel, preferred_element_type=F32)
        sout_ref[it["b"], it["h"]] = it["eg_b"] * s0_ref[it["b"], it["h"]] + upd
    for h in range(HEADS):
        z = proj_ref[:, COL_Z + h * GDN_D:COL_Z + (h + 1) * GDN_D]
        out = _rms(rowg_ref[h], gng_ref[...]) * (z * _sigmoid(z))
        og_ref[:, h * GDN_D:(h + 1) * GDN_D] = out.astype(og_ref.dtype)

    heads = []
    for p in range(HEADS // 2):
        qb = proj_ref[:, COL_MQ + p * 128:COL_MQ + (p + 1) * 128]
        kb = proj_ref[:, COL_MK + p * 128:COL_MK + (p + 1) * 128] * (ML_DK ** -0.5)
        n_p = n0_ref[:, p * 128:(p + 1) * 128]
        k_t = jnp.concatenate([jnp.where(lane < ML_DK, kb, 0.0), jnp.where(lane >= ML_DK, kb, 0.0)],
                              axis=0).T.astype(BF16)
        qq = jnp.concatenate([jnp.where(lane < ML_DK, qb, 0.0), jnp.where(lane >= ML_DK, qb, 0.0)],
                             axis=0).astype(BF16)
        w_prev, p_in = [], []
        for e in range(2):
            h = 2 * p + e
            lm = (lane >= e * ML_DK) & (lane < (e + 1) * ML_DK)
            ig = gb_t[:, LANE_MI + h:LANE_MI + h + 1]
            lf = lf_t[:, LANE_MF + h:LANE_MF + h + 1]
            m_old = m0_ref[:, h:h + 1]
            m_new = jnp.maximum(lf + m_old, ig)
            w_prev.append(jnp.exp(lf + m_old - m_new))
            p_in.append(jnp.exp(ig - m_new))
            qk = jnp.sum(jnp.where(lm, qb * kb, 0.0), axis=-1, keepdims=True)
            qn = jnp.sum(jnp.where(lm, qb * n_p, 0.0), axis=-1, keepdims=True)
            mout_ref[:, h:h + 1] = m_new
            heads.append(dict(p=p, e=e, qq=qq, k_t=k_t, w_prev=w_prev[e], p_in=p_in[e],
                              pqk=p_in[e] * qk, wqn=w_prev[e] * qn, floor=jnp.exp(-m_new),
                              v=proj_ref[:, COL_MV + h * ML_DV:COL_MV + (h + 1) * ML_DV]))
        lo_lane = lane < ML_DK
        nout_ref[:, p * 128:(p + 1) * 128] = (jnp.where(lo_lane, w_prev[0], w_prev[1]) * n_p
                                              + jnp.where(lo_lane, p_in[0], p_in[1]) * kb)
    items = [dict(h=h, b=b) for h in range(HEADS) for b in range(bb)]
    qc_pair = {(p, b): jnp.dot(heads[2 * p]["qq"], c0_ref[b, p].astype(BF16), preferred_element_type=F32)
               for p in range(HEADS // 2) for b in range(bb)}
    for it in items:
        hd, b = heads[it["h"]], it["b"]
        row = hd["e"] * bb + b
        it["qc"] = qc_pair[(hd["p"], b)][row:row + 1, :]
    for it in items:
        hd, b = heads[it["h"]], it["b"]
        pqk = hd["pqk"][b:b + 1, :]
        num = hd["w_prev"][b:b + 1, :] * it["qc"] + pqk * hd["v"][b:b + 1, :]
        den = hd["wqn"][b:b + 1, :] + pqk
        rowm_ref[it["h"], b:b + 1, :] = num / jnp.maximum(jnp.abs(den), hd["floor"][b:b + 1, :])
    row128 = lax.broadcasted_iota(jnp.int32, (128, 1), 0)
    for p in range(HEADS // 2):
        h0, h1 = heads[2 * p], heads[2 * p + 1]
        pv = jnp.concatenate([h0["p_in"] * h0["v"], h1["p_in"] * h1["v"]], axis=0)
        row_2b = lax.broadcasted_iota(jnp.int32, (2 * bb, 1), 0)
        for b in range(bb):
            pv_sel = jnp.where((row_2b == b) | (row_2b == bb + b), pv, 0.0).astype(BF16)
            upd = jnp.dot(h0["k_t"], pv_sel, preferred_element_type=F32)
            w_col = jnp.where(row128 < ML_DK, h0["w_prev"][b:b + 1, :], h1["w_prev"][b:b + 1, :])
            cout_ref[b, p] = w_col * c0_ref[b, p] + upd
    for h in range(HEADS):
        mo = proj_ref[:, COL_MO + h * ML_DV:COL_MO + (h + 1) * ML_DV]
        out = _sigmoid(mo) * _rms(rowm_ref[h], gnm_ref[...])
        om_ref[:, h * ML_DV:(h + 1) * ML_DV] = out.astype(om_ref.dtype)
    mout_ref[:, HEADS:128] = m0_ref[:, HEADS:128]


def _decode(proj, gates, cs, s0, c0, n0, m0, conv_w, plane, gng, gnm, bb):
    b = proj.shape[0]
    hp = HEADS // 2
    return pl.pallas_call(
        functools.partial(_decode_kernel, bb=bb),
        grid=(b // bb,),
        in_specs=[pl.BlockSpec((bb, COL_GATE), lambda i: (i, 0)),
                  pl.BlockSpec((bb, 128), lambda i: (i, 0)),
                  pl.BlockSpec((bb, CONV_W - 1, CONV_CH), lambda i: (i, 0, 0)),
                  pl.BlockSpec((bb, HEADS, GDN_D, GDN_D), lambda i: (i, 0, 0, 0)),
                  pl.BlockSpec((bb, hp, 128, 128), lambda i: (i, 0, 0, 0)),
                  pl.BlockSpec((bb, HEADS * ML_DK), lambda i: (i, 0)),
                  pl.BlockSpec((bb, 128), lambda i: (i, 0)),
                  pl.BlockSpec((CONV_W, CONV_CH), lambda i: (0, 0)),
                  pl.BlockSpec((3, 128), lambda i: (0, 0)),
                  pl.BlockSpec((1, GDN_D), lambda i: (0, 0)),
                  pl.BlockSpec((1, ML_DV), lambda i: (0, 0))],
        out_specs=[pl.BlockSpec((bb, HEADS * GDN_D), lambda i: (i, 0)),
                   pl.BlockSpec((bb, HEADS * ML_DV), lambda i: (i, 0)),
                   pl.BlockSpec((bb, HEADS, GDN_D, GDN_D), lambda i: (i, 0, 0, 0)),
                   pl.BlockSpec((bb, hp, 128, 128), lambda i: (i, 0, 0, 0)),
                   pl.BlockSpec((bb, HEADS * ML_DK), lambda i: (i, 0)),
                   pl.BlockSpec((bb, 128), lambda i: (i, 0))],
        out_shape=[jax.ShapeDtypeStruct((b, HEADS * GDN_D), BF16),
                   jax.ShapeDtypeStruct((b, HEADS * ML_DV), BF16),
                   jax.ShapeDtypeStruct((b, HEADS, GDN_D, GDN_D), F32),
                   jax.ShapeDtypeStruct((b, hp, 128, 128), F32),
                   jax.ShapeDtypeStruct((b, HEADS * ML_DK), F32),
                   jax.ShapeDtypeStruct((b, 128), F32)],
        scratch_shapes=[pltpu.VMEM((HEADS, bb, 128), F32), pltpu.VMEM((HEADS, bb, 128), F32)],
        compiler_params=pltpu.CompilerParams(dimension_semantics=("arbitrary",),
                                             vmem_limit_bytes=VMEM_LIMIT),
        name="decode_step",
    )(proj, gates, cs, s0, c0, n0, m0, conv_w, plane, gng, gnm)


def _lane_vec(pairs):
    v = jnp.zeros((128,), F32)
    for off, val in pairs:
        v = v.at[off:off + HEADS].set(val.astype(F32))
    return v


def _prep_params(norm_pre_mix, w_in, conv_w, a_log, dt_bias, gdn_norm_g, b_igate, b_fgate, mlstm_norm_g,
                 w_out, norm_post_mix, norm_pre_mlp, w_up, w_down, norm_post_mlp):
    alog_v = _lane_vec([(LANE_GA, a_log[0])])
    dtb_v = _lane_vec([(LANE_GA, dt_bias[0])])
    bias_v = _lane_vec([(LANE_MI, b_igate[0]), (LANE_MF, b_fgate[0])])
    return dict(
        w_in=jnp.swapaxes(w_in[0], 0, 1), wo=w_out[0].astype(BF16), wu=w_up[0].astype(BF16),
        wd=w_down[0].astype(BF16),
        g_pre=norm_pre_mix[0][None, :], g1=norm_post_mix[0][None, :], g2=norm_pre_mlp[0][None, :],
        g3=norm_post_mlp[0][None, :], cw=conv_w[0], gng=gdn_norm_g[0][None, :], gnm=mlstm_norm_g[0][None, :],
        plane=jnp.stack([alog_v, dtb_v, bias_v]),
        psub=jnp.stack([alog_v[:16], dtb_v[:16], bias_v[:16]], axis=1),
    )


def _prompt_path(x, conv0, s0, c0, n0, m0, prm, tb, tm):
    bsz, seq, d = x.shape
    hp = HEADS // 2
    x2d = x.reshape(bsz * seq, d)
    proj, gates, gates_t = _in_proj(x2d, prm["g_pre"], prm["w_in"], tm=tm, out_dtype=BF16)
    proj = proj.reshape(bsz, seq, COL_GATE)
    gates = gates.reshape(bsz, seq, 128)
    gates_t = gates_t.reshape(bsz, seq // CHUNK, 16, CHUNK)
    cs8 = jnp.pad(conv0, ((0, 0), (8 - (CONV_W - 1), 0), (0, 0)))
    y, s_new, c_new, n_new, m_new = _layer_prompt(
        proj, gates, gates_t, cs8, s0, c0.reshape(bsz, hp, 128, 128), n0.reshape(bsz, hp, 1, 128),
        jnp.pad(m0, ((0, 0), (0, 128 - HEADS)))[:, None, :], x2d, prm["cw"], prm["plane"], prm["psub"],
        prm["gng"], prm["gnm"], prm["wo"], prm["g1"], prm["g2"], prm["wu"], prm["wd"], prm["g3"], tb=tb)
    y = y.reshape(bsz, seq, d)
    xp = jnp.concatenate([conv0, proj[:, seq - (CONV_W - 1):, COL_QKV:COL_QKV + CONV_CH].astype(F32)], axis=1)
    conv_new = xp[:, -(CONV_W - 1):]
    return y, (conv_new, s_new, c_new.reshape(bsz, HEADS, ML_DK, ML_DV), n_new.reshape(bsz, HEADS, ML_DK),
               m_new[:, 0, :HEADS])


def _sample_path(x, conv0, s0, c0, n0, m0, prm):
    dec, _, d = x.shape
    hp = HEADS // 2
    xs = x.reshape(dec, d)
    proj, gates, _ = _in_proj(xs, prm["g_pre"], prm["w_in"], tm=dec, out_dtype=F32)
    og, om, s_new, c_new, n_new, m_new = _decode(
        proj, gates, conv0, s0, c0.reshape(dec, hp, 128, 128), n0.reshape(dec, HEADS * ML_DK),
        jnp.pad(m0, ((0, 0), (0, 128 - HEADS))), prm["cw"], prm["plane"], prm["gng"], prm["gnm"], bb=8)
    y = _out_mlp(og, om, xs, prm["wo"], prm["g1"], prm["g2"], prm["wu"], prm["wd"], prm["g3"],
                 tm=dec).reshape(dec, 1, d)
    conv_new = jnp.concatenate([conv0[:, 1:, :], proj[:, None, COL_QKV:COL_QKV + CONV_CH]], axis=1)
    return y, (conv_new, s_new, c_new.reshape(dec, HEADS, ML_DK, ML_DV), n_new.reshape(dec, HEADS, ML_DK),
               m_new[:, :HEADS])


def kernel(x_prompt, x_sample, state_gdn_conv, state_gdn_S, state_mlstm_C, state_mlstm_n, state_mlstm_m,
           norm_pre_mix, w_in, conv_w, a_log, dt_bias, gdn_norm_g, b_igate, b_fgate, mlstm_norm_g, w_out,
           norm_post_mix, norm_pre_mlp, w_up, w_down, norm_post_mlp):
    bsz = x_prompt.shape[0]
    prm = _prep_params(norm_pre_mix, w_in, conv_w, a_log, dt_bias, gdn_norm_g, b_igate, b_fgate, mlstm_norm_g,
                       w_out, norm_post_mix, norm_pre_mlp, w_up, w_down, norm_post_mlp)
    y_p, p_st = _prompt_path(
        x_prompt, jnp.zeros((bsz, CONV_W - 1, CONV_CH), F32), jnp.zeros((bsz, HEADS, GDN_D, GDN_D), F32),
        jnp.zeros((bsz, HEADS, ML_DK, ML_DV), F32), jnp.zeros((bsz, HEADS, ML_DK), F32),
        jnp.zeros((bsz, HEADS), F32), prm, tb=256, tm=512)
    y_s, s_st = _sample_path(x_sample, state_gdn_conv[0], state_gdn_S[0], state_mlstm_C[0], state_mlstm_n[0],
                             state_mlstm_m[0], prm)
    return (y_p, y_s) + tuple(a[None] for a in p_st) + tuple(a[None] for a in s_st)
```

```python
import functools

import jax
import jax.numpy as jnp
from jax import lax
from jax.experimental import pallas as pl
from jax.experimental.pallas import tpu as pltpu

F32 = jnp.float32
BF16 = jnp.bfloat16
EPS = 1e-6

HEADS = 4
GDN_D = 128
ML_DK = 64
ML_DV = 128
CONV_W = 4
CONV_CH = 3 * HEADS * GDN_D
CHUNK = 64

COL_QKV = 0
COL_Z = 1536
COL_MV = 2048
COL_MO = 2560
COL_MQ = 3072
COL_MK = 3328
COL_GATE = 3584
N_PROJ = COL_GATE + 128
LANE_GB, LANE_GA, LANE_MI, LANE_MF = 0, 4, 8, 12

V7X_VMEM_BYTES = 64 * 1024 * 1024
VMEM_LIMIT = V7X_VMEM_BYTES - 8 * 1024 * 1024


def _rms(x, g):
    return x * lax.rsqrt(jnp.mean(x * x, axis=-1, keepdims=True) + EPS) * g


def _softplus(x):
    return jnp.maximum(x, 0.0) + jnp.log1p(jnp.exp(-jnp.abs(x)))


def _sigmoid(x):
    return 1.0 / (1.0 + jnp.exp(-x))


def _mm(a, b):
    return jnp.dot(a.astype(BF16), b.astype(BF16), preferred_element_type=F32)


def _mm_nt(a, b):
    return lax.dot_general(a.astype(BF16), b.astype(BF16), (((1,), (1,)), ((), ())),
                           preferred_element_type=F32)


W_IN_COLS = 3600
W_IN_SPANS = ((0, COL_QKV, 2048),
              (2568, COL_MV, 1024),
              (2056, COL_MQ, 512),
              (2048, COL_GATE, 8),
              (3592, COL_GATE + 8, 8))


def _project_rows(x_ref, g_ref, w_ref, o_ref, gc_ref, gt_ref, h_ref, n_chunk):
    h_ref[...] = _rms(x_ref[...], g_ref[...]).astype(BF16)
    n = o_ref.shape[1]
    for c0 in range(0, n, n_chunk):
        c1 = min(c0 + n_chunk, n)
        o_ref[:, c0:c1] = _mm_nt(h_ref[...], w_ref[c0:c1, :]).astype(o_ref.dtype)
    gates = _mm_nt(h_ref[...], w_ref[COL_GATE:COL_GATE + 128, :])
    gc_ref[...] = gates
    gt = gates.T
    for c in range(gt_ref.shape[0]):
        gt_ref[c] = gt[0:16, c * CHUNK:(c + 1) * CHUNK]


def _in_proj_regroup_kernel(x_ref, g_ref, win_ref, o_ref, gc_ref, gt_ref, w_ref, h_ref, *, n_chunk):
    w_ref[COL_GATE:N_PROJ, :] = jnp.zeros((N_PROJ - COL_GATE, w_ref.shape[1]), BF16)
    for src, dst, width in W_IN_SPANS:
        for r0 in range(0, width, 512):
            r1 = min(r0 + 512, width)
            w_ref[dst + r0:dst + r1, :] = win_ref[src + r0:src + r1, :].astype(BF16)
    _project_rows(x_ref, g_ref, w_ref, o_ref, gc_ref, gt_ref, h_ref, n_chunk)


def _in_proj_kernel(x_ref, g_ref, w_ref, o_ref, gc_ref, gt_ref, h_ref, *, n_chunk):
    _project_rows(x_ref, g_ref, w_ref, o_ref, gc_ref, gt_ref, h_ref, n_chunk)


def _in_proj_regroup(x2d, g, w_t):
    m, k = x2d.shape
    full = lambda i: (0, 0)
    return pl.pallas_call(
        functools.partial(_in_proj_regroup_kernel, n_chunk=512),
        grid=(1,),
        in_specs=[pl.BlockSpec((m, k), full), pl.BlockSpec((1, k), full), pl.BlockSpec((W_IN_COLS, k), full)],
        out_specs=[pl.BlockSpec((m, COL_GATE), full), pl.BlockSpec((m, 128), full),
                   pl.BlockSpec((m // CHUNK, 16, CHUNK), lambda i: (0, 0, 0)), pl.BlockSpec((N_PROJ, k), full)],
        out_shape=[jax.ShapeDtypeStruct((m, COL_GATE), F32),
                   jax.ShapeDtypeStruct((m, 128), F32),
                   jax.ShapeDtypeStruct((m // CHUNK, 16, CHUNK), F32),
                   jax.ShapeDtypeStruct((N_PROJ, k), BF16)],
        scratch_shapes=[pltpu.VMEM((m, k), BF16)],
        compiler_params=pltpu.CompilerParams(dimension_semantics=("arbitrary",),
                                             vmem_limit_bytes=VMEM_LIMIT),
        name="in_proj_regroup",
    )(x2d, g, w_t)


def _in_proj(x2d, g, w, tm, out_dtype):
    m, k = x2d.shape
    return pl.pallas_call(
        functools.partial(_in_proj_kernel, n_chunk=512),
        grid=(m // tm,),
        in_specs=[pl.BlockSpec((tm, k), lambda i: (i, 0)),
                  pl.BlockSpec((1, k), lambda i: (0, 0)),
                  pl.BlockSpec((N_PROJ, k), lambda i: (0, 0), pipeline_mode=pl.Buffered(1))],
        out_specs=[pl.BlockSpec((tm, COL_GATE), lambda i: (i, 0)),
                   pl.BlockSpec((tm, 128), lambda i: (i, 0)),
                   pl.BlockSpec((tm // CHUNK, 16, CHUNK), lambda i: (i, 0, 0))],
        out_shape=[jax.ShapeDtypeStruct((m, COL_GATE), out_dtype),
                   jax.ShapeDtypeStruct((m, 128), F32),
                   jax.ShapeDtypeStruct((m // CHUNK, 16, CHUNK), F32)],
        scratch_shapes=[pltpu.VMEM((tm, k), BF16)],
        compiler_params=pltpu.CompilerParams(dimension_semantics=("arbitrary",),
                                             vmem_limit_bytes=VMEM_LIMIT),
        name="in_proj",
    )(x2d, g, w)


def _out_mlp_kernel(og_ref, om_ref, x_ref, wo_ref, g1_ref, g2_ref, wu_ref, wd_ref, g3_ref,
                    y_ref, hn_ref, acc_ref, *, ff_chunk):
    half = og_ref.shape[1]
    mix = (jnp.dot(og_ref[...], wo_ref[0:half, :], preferred_element_type=F32)
           + jnp.dot(om_ref[...], wo_ref[half:2 * half, :], preferred_element_type=F32))
    x1 = x_ref[...] + _rms(mix, g1_ref[...])
    y_ref[...] = x1
    hn_ref[...] = _rms(x1, g2_ref[...]).astype(BF16)
    d_ff = wu_ref.shape[1]
    for c0 in range(0, d_ff, ff_chunk):
        u = jnp.dot(hn_ref[...], wu_ref[:, c0:c0 + ff_chunk], preferred_element_type=F32)
        u = jnp.square(jnp.maximum(u, 0.0)).astype(BF16)
        d = jnp.dot(u, wd_ref[c0:c0 + ff_chunk, :], preferred_element_type=F32)
        if c0 == 0:
            acc_ref[...] = d
        else:
            acc_ref[...] += d
    y_ref[...] = y_ref[...] + _rms(acc_ref[...], g3_ref[...])


def _out_mlp(og, om, x2d, wo, g1, g2, wu, wd, g3, tm):
    m, d = x2d.shape
    half = og.shape[1]
    d_ff = wu.shape[1]
    const = lambda i: (0, 0)
    return pl.pallas_call(
        functools.partial(_out_mlp_kernel, ff_chunk=1024),
        grid=(m // tm,),
        in_specs=[pl.BlockSpec((tm, half), lambda i: (i, 0)),
                  pl.BlockSpec((tm, half), lambda i: (i, 0)),
                  pl.BlockSpec((tm, d), lambda i: (i, 0)),
                  pl.BlockSpec((d, d), const, pipeline_mode=pl.Buffered(1)),
                  pl.BlockSpec((1, d), const),
                  pl.BlockSpec((1, d), const),
                  pl.BlockSpec((d, d_ff), const, pipeline_mode=pl.Buffered(1)),
                  pl.BlockSpec((d_ff, d), const, pipeline_mode=pl.Buffered(1)),
                  pl.BlockSpec((1, d), const)],
        out_specs=pl.BlockSpec((tm, d), lambda i: (i, 0)),
        out_shape=jax.ShapeDtypeStruct((m, d), F32),
        scratch_shapes=[pltpu.VMEM((tm, d), BF16), pltpu.VMEM((tm, d), F32)],
        compiler_params=pltpu.CompilerParams(dimension_semantics=("arbitrary",),
                                             vmem_limit_bytes=VMEM_LIMIT),
        name="out_mlp",
    )(og, om, x2d, wo, g1, g2, wu, wd, g3)


def _chunk_masks():
    ii = lax.broadcasted_iota(jnp.int32, (CHUNK, CHUNK), 0)
    jj = lax.broadcasted_iota(jnp.int32, (CHUNK, CHUNK), 1)
    return ii, jj


def _cumsum_col_row(x_col, x_row, ii, jj):
    c_col = jnp.sum(jnp.where(jj <= ii, x_row, 0.0), axis=1, keepdims=True)
    c_row = jnp.sum(jnp.where(ii <= jj, x_col, 0.0), axis=0, keepdims=True)
    return c_col, c_row


def _interleave(*tasks):
    live = [[g, n, 0] for g, n in tasks]
    while live:
        entry = min(live, key=lambda e: (e[2] + 1) / e[1])
        try:
            next(entry[0])
            entry[2] += 1
        except StopIteration:
            live.remove(entry)


def _halves(items):
    mid = len(items) // 2
    return items[:mid], items[mid:]


def _gdn_phase_a(yc_ref, gcol_ref, grow_ref, plane_ref, psub_ref, wv_ref, lhs_ref, qk_ref, ket_ref, ge_ref,
                 ncb):
    ii, jj = _chunk_masks()
    incl = jj <= ii
    strict = jj < ii
    eye = (ii == jj).astype(F32)
    neg_a_lane = -jnp.exp(plane_ref[0:1, :])
    dtb_lane = plane_ref[1:2, :]
    neg_a_sub = -jnp.exp(psub_ref[:, 0:1])
    dtb_sub = psub_ref[:, 1:2]
    items = []
    for c in range(ncb):
        rows = slice(c * CHUNK, (c + 1) * CHUNK)
        gc = gcol_ref[rows, :]
        gr = grow_ref[c]
        beta_t = _sigmoid(gc)
        g_t = neg_a_lane * _softplus(gc + dtb_lane)
        g_r = neg_a_sub * _softplus(gr + dtb_sub)
        for h in range(HEADS):
            lo = h * GDN_D
            items.append(dict(c=c, h=h,
                              q=yc_ref[rows, lo:lo + GDN_D],
                              k=yc_ref[rows, 512 + lo:512 + lo + GDN_D],
                              v=yc_ref[rows, 1024 + lo:1024 + lo + GDN_D],
                              beta=beta_t[:, LANE_GB + h:LANE_GB + h + 1],
                              gg_col=g_t[:, LANE_GA + h:LANE_GA + h + 1],
                              gg_row=g_r[LANE_GA + h:LANE_GA + h + 1, :]))
    yield
    for it in items:
        it["k"] = it["k"] * lax.rsqrt(jnp.sum(it["k"] * it["k"], axis=-1, keepdims=True) + EPS)
    yield
    for it in items:
        it["kk"] = _mm_nt(it["k"], it["k"])
    yield
    for it in items:
        it["q"] = (it["q"] * lax.rsqrt(jnp.sum(it["q"] * it["q"], axis=-1, keepdims=True) + EPS)
                   * (GDN_D ** -0.5))
    yield
    for it in items:
        it["g_col"], g_row = _cumsum_col_row(it["gg_col"], it["gg_row"], ii, jj)
        it["decay"] = jnp.where(incl, jnp.exp(jnp.where(incl, it["g_col"] - g_row, 0.0)), 0.0)
    yield
    for it in items:
        n_mat = jnp.where(strict, it["beta"] * it["kk"] * it["decay"], 0.0)
        it["x"] = eye - n_mat
        it["p"] = -n_mat
    yield
    for _ in range(5):
        yield
        for it in items:
            it["p"] = _mm(it["p"], it["p"])
        yield
        for it in items:
            it["x"] = it["x"] + _mm(it["x"], it["p"])
    yield
    for it in items:
        e_g = jnp.exp(it["g_col"])
        it["e_g"] = e_g
        rhs = jnp.concatenate([it["beta"] * it["v"], (it["beta"] * e_g) * it["k"]], axis=1)
        it["w"] = _mm(it["x"], rhs)
    yield
    for it in items:
        it["qk"] = _mm_nt(it["q"], it["k"]) * it["decay"]
    yield
    for it in items:
        c, h = it["c"], it["h"]
        g_end = it["g_col"][CHUNK - 1:CHUNK, :]
        k_end = it["k"] * jnp.exp(g_end - it["g_col"])
        wv_ref[c, h] = it["w"][:, 0:GDN_D]
        lhs_ref[c, h] = jnp.concatenate([it["w"][:, GDN_D:2 * GDN_D], it["e_g"] * it["q"]],
                                        axis=0).astype(BF16)
        qk_ref[c, h] = it["qk"].astype(BF16)
        ket_ref[c, h] = k_end.T.astype(BF16)
        ge_ref[c, h] = jnp.broadcast_to(jnp.exp(g_end), (1, GDN_D))


def _gdn_phase_b(qkvz_ref, gn_ref, og_ref, s_ref, wv_ref, lhs_ref, qk_ref, ket_ref, ge_ref, ncb):
    gn = gn_ref[...]

    def epilogue(c, o):
        rows = slice(c * CHUNK, (c + 1) * CHUNK)
        for h in range(HEADS):
            lo = h * GDN_D
            z = qkvz_ref[rows, COL_Z + lo:COL_Z + lo + GDN_D].astype(F32)
            out = _rms(o[h], gn) * (z * _sigmoid(z))
            og_ref[rows, lo:lo + GDN_D] = out.astype(og_ref.dtype)

    s = [s_ref[h] for h in range(HEADS)]
    o_prev = None
    for c in range(ncb):
        r = [jnp.dot(lhs_ref[c, h], s[h].astype(BF16), preferred_element_type=F32) for h in range(HEADS)]
        yield
        if o_prev is not None:
            epilogue(c - 1, o_prev)
        ub = [(wv_ref[c, h] - r[h][0:CHUNK]).astype(BF16) for h in range(HEADS)]
        s = [ge_ref[c, h] * s[h] + jnp.dot(ket_ref[c, h], ub[h], preferred_element_type=F32)
             for h in range(HEADS)]
        o_prev = [r[h][CHUNK:2 * CHUNK] + jnp.dot(qk_ref[c, h], ub[h], preferred_element_type=F32)
                  for h in range(HEADS)]
        yield
    epilogue(ncb - 1, o_prev)
    for h in range(HEADS):
        s_ref[h] = s[h]


def _mlstm_block(mvo_ref, mqk_ref, gcol_ref, grow_ref, plane_ref, psub_ref, gn_ref, om_ref,
                 c_ref, n_ref, m_ref, ncb):
    ii, jj = _chunk_masks()
    incl = jj <= ii
    lane = lax.broadcasted_iota(jnp.int32, (1, 128), 1)
    row128 = lax.broadcasted_iota(jnp.int32, (128, 1), 0)
    gn = gn_ref[...]
    blane = plane_ref[2:3, :]
    bsub = psub_ref[:, 2:3]
    items = []
    for c in range(ncb):
        rows = slice(c * CHUNK, (c + 1) * CHUNK)
        gc = gcol_ref[rows, :] + blane
        gr = grow_ref[c] + bsub
        lf_c = -_softplus(-gc)
        lf_r = -_softplus(-gr)
        for p in range(HEADS // 2):
            qb = mqk_ref[rows, p * 128:(p + 1) * 128].astype(F32)
            kb = mqk_ref[rows, 256 + p * 128:256 + (p + 1) * 128].astype(F32) * (ML_DK ** -0.5)
            for e in range(2):
                h = 2 * p + e
                lm = (lane >= e * ML_DK) & (lane < (e + 1) * ML_DK)
                items.append(dict(c=c, rows=rows, p=p, e=e, h=h, qh=jnp.where(lm, qb, 0.0), kb=kb,
                                  kh=jnp.where(lm, kb, 0.0),
                                  lf_col=lf_c[:, LANE_MF + h:LANE_MF + h + 1],
                                  lf_row=lf_r[LANE_MF + h:LANE_MF + h + 1, :],
                                  ig_row=gr[LANE_MI + h:LANE_MI + h + 1, :],
                                  ig_col=gc[:, LANE_MI + h:LANE_MI + h + 1],
                                  v=mvo_ref[rows, h * ML_DV:(h + 1) * ML_DV]))
        if c % 2 == 1:
            yield
    for part in _halves(items):
        for it in part:
            it["qk"] = _mm_nt(it["qh"], it["kb"])
        yield
    for part in _halves(items):
        for it in part:
            it["f_col"], it["f_row"] = _cumsum_col_row(it["lf_col"], it["lf_row"], ii, jj)
        yield
    for part in _halves(items):
        for it in part:
            it["d_mat"] = jnp.where(incl, it["f_col"] - it["f_row"] + it["ig_row"], -jnp.inf)
            it["d_max"] = jnp.max(it["d_mat"], axis=1, keepdims=True)
        yield
    for part in _halves(items):
        for it in part:
            p0 = jnp.where(incl, jnp.exp(jnp.where(incl, it["d_mat"] - it["d_max"], 0.0)), 0.0)
            pend0 = jnp.exp(it["f_col"][CHUNK - 1:CHUNK, :] - it["f_col"] + it["ig_col"]
                            - it["d_max"][CHUNK - 1:CHUNK, :])
            it["kp0"] = it["kh"] * pend0
            it["pqk0"] = p0 * it["qk"]
        yield
    for part in _halves(items):
        for it in part:
            it["pv0"] = _mm(it["pqk0"], it["v"])
            it["rs0"] = jnp.sum(it["pqk0"], axis=-1, keepdims=True)
        yield
    for part in _halves(items):
        for it in part:
            it["cadd0"] = _mm(it["kp0"].T, it["v"])
            it["nadd0"] = jnp.sum(it["kp0"], axis=0, keepdims=True)
        yield

    m_cur = [m_ref[:, h:h + 1] for h in range(HEADS)]
    for it in items:
        h = it["h"]
        it["m_prev"] = m_cur[h]
        m_cur[h] = jnp.maximum(it["f_col"][CHUNK - 1:CHUNK, :] + m_cur[h], it["d_max"][CHUNK - 1:CHUNK, :])
    yield
    for part in _halves(items):
        for it in part:
            bcol = it["f_col"] + it["m_prev"]
            mt = jnp.maximum(bcol, it["d_max"])
            it["mt"] = mt
            it["w_prev"] = jnp.exp(bcol - mt)
            it["sc"] = jnp.exp(it["d_max"] - mt)
        yield
    c_cur = [c_ref[p] for p in range(HEADS // 2)]
    n_cur = [n_ref[p] for p in range(HEADS // 2)]
    for c in range(ncb):
        for p in range(HEADS // 2):
            pair = [it for it in items if it["c"] == c and it["p"] == p]
            w_end = [it["w_prev"][CHUNK - 1:CHUNK, :] for it in pair]
            s_end = [it["sc"][CHUNK - 1:CHUNK, :] for it in pair]
            for it in pair:
                it["c_prev"] = c_cur[p]
                it["n_prev"] = n_cur[p]
            c_cur[p] = (jnp.where(row128 < ML_DK, w_end[0], w_end[1]) * c_cur[p]
                        + s_end[0] * pair[0]["cadd0"] + s_end[1] * pair[1]["cadd0"])
            n_cur[p] = (jnp.where(lane < ML_DK, w_end[0], w_end[1]) * n_cur[p]
                        + s_end[0] * pair[0]["nadd0"] + s_end[1] * pair[1]["nadd0"])
        if c % 2 == 1:
            yield
    for part in _halves(items):
        for it in part:
            it["qc"] = _mm(it["qh"], it["c_prev"])
        yield
    for part in _halves(items):
        for it in part:
            num = it["w_prev"] * it["qc"] + it["sc"] * it["pv0"]
            den = (it["w_prev"] * jnp.sum(it["qh"] * it["n_prev"], axis=-1, keepdims=True)
                   + it["sc"] * it["rs0"])
            it["hh"] = num / jnp.maximum(jnp.abs(den), jnp.exp(-it["mt"]))
        yield
    for part in _halves(items):
        for it in part:
            h = it["h"]
            mo = mvo_ref[it["rows"], 512 + h * ML_DV:512 + (h + 1) * ML_DV].astype(F32)
            out = _sigmoid(mo) * _rms(it["hh"], gn)
            om_ref[it["rows"], h * ML_DV:(h + 1) * ML_DV] = out.astype(om_ref.dtype)
        yield
    for p in range(HEADS // 2):
        c_ref[p] = c_cur[p]
        n_ref[p] = n_cur[p]
    for h in range(HEADS):
        m_ref[:, h:h + 1] = m_cur[h]


def _chain(*gens):
    for g in gens:
        yield from g


def _conv_stage(xp_ref, yc_ref, cw_ref, tb):
    for ct in range(CONV_CH // 128):
        cols = slice(ct * 128, (ct + 1) * 128)
        w = cw_ref[:, cols]
        for r0 in range(0, tb, 128):
            acc = xp_ref[8 + r0:8 + r0 + 128, cols] * w[CONV_W - 1:CONV_W, :]
            for j in range(CONV_W - 1):
                acc = acc + xp_ref[5 + j + r0:5 + j + r0 + 128, cols] * w[j:j + 1, :]
            yc_ref[r0:r0 + 128, cols] = acc * _sigmoid(acc)
        yield


def _out_mlp_block(og_ref, om_ref, x_ref, wo_ref, g1_ref, g2_ref, wu_ref, wd_ref, g3_ref,
                   y_ref, hn_ref, acc_ref, ff_chunk, n_split):
    half = og_ref.shape[1]
    d = x_ref.shape[1]
    d_ff = wu_ref.shape[1]
    col_groups = [slice(c, c + d // n_split) for c in range(0, d, d // n_split)]
    for cg in col_groups:
        acc_ref[:, cg] = (jnp.dot(og_ref[...], wo_ref[0:half, cg], preferred_element_type=F32)
                          + jnp.dot(om_ref[...], wo_ref[half:2 * half, cg], preferred_element_type=F32))
        yield
    x1 = x_ref[...] + _rms(acc_ref[...], g1_ref[...])
    y_ref[...] = x1
    hn_ref[...] = _rms(x1, g2_ref[...]).astype(BF16)
    yield
    for c0 in range(0, d_ff, ff_chunk):
        u = jnp.dot(hn_ref[...], wu_ref[:, c0:c0 + ff_chunk], preferred_element_type=F32)
        u = jnp.square(jnp.maximum(u, 0.0)).astype(BF16)
        for cg in col_groups:
            dd = jnp.dot(u, wd_ref[c0:c0 + ff_chunk, cg], preferred_element_type=F32)
            if c0 == 0:
                acc_ref[:, cg] = dd
            else:
                acc_ref[:, cg] += dd
        yield
    y_ref[...] = y_ref[...] + _rms(acc_ref[...], g3_ref[...])
    yield


def _layer_kernel(qkvz_ref, mvo_ref, mqk_ref, gcol_ref, grow_ref, cs_ref, s0_ref, c0_ref, n0_ref, m0_ref,
                  x_ref, cw_ref, plane_ref, psub_ref, gng_ref, gnm_ref,
                  wo_ref, g1_ref, g2_ref, wu_ref, wd_ref, g3_ref,
                  y_ref, sout_ref, cout_ref, nout_ref, mout_ref,
                  xp_ref, yc_ref, s_ref, wv_ref, lhs_ref, qk_ref, ket_ref, ge_ref, c_ref, n_ref, m_ref,
                  og_ref, om_ref, hn_ref, acc_ref, *, tb, nt, n_blocks):
    g = pl.program_id(0)
    t = jnp.minimum(g, n_blocks - 1) % nt
    ncb = tb // CHUNK
    par = g % 2

    @pl.when(g == 0)
    def _():
        og_ref[1] = jnp.zeros(og_ref.shape[1:], og_ref.dtype)
        om_ref[1] = jnp.zeros(om_ref.shape[1:], om_ref.dtype)

    @pl.when(t == 0)
    def _():
        xp_ref[0:8, :] = cs_ref[...]
        s_ref[...] = s0_ref[...]
        c_ref[...] = c0_ref[...]
        n_ref[...] = n0_ref[...]
        m_ref[...] = m0_ref[...]

    @pl.when(t > 0)
    def _():
        xp_ref[0:8, :] = xp_ref[tb:tb + 8, :]

    xp_ref[8:tb + 8, :] = qkvz_ref[:, COL_QKV:COL_QKV + CONV_CH].astype(F32)

    og_w, om_w = og_ref.at[par], om_ref.at[par]
    og_r, om_r = og_ref.at[1 - par], om_ref.at[1 - par]
    _interleave(
        (_chain(_conv_stage(xp_ref, yc_ref, cw_ref, tb),
                _gdn_phase_a(yc_ref, gcol_ref, grow_ref, plane_ref, psub_ref, wv_ref, lhs_ref, qk_ref, ket_ref,
                             ge_ref, ncb),
                _gdn_phase_b(qkvz_ref, gng_ref, og_w, s_ref, wv_ref, lhs_ref, qk_ref, ket_ref, ge_ref, ncb)),
         CONV_CH // 128 + 20 + 2 * ncb),
        (_out_mlp_block(og_r, om_r, x_ref, wo_ref, g1_ref, g2_ref, wu_ref, wd_ref, g3_ref, y_ref, hn_ref, acc_ref,
                        ff_chunk=256, n_split=2), 2 + 1 + wu_ref.shape[1] // 256 + 1),
        (_mlstm_block(mvo_ref, mqk_ref, gcol_ref, grow_ref, plane_ref, psub_ref, gnm_ref, om_w,
                      c_ref, n_ref, m_ref, ncb), 21 + ncb))

    @pl.when((t == nt - 1) & (g < n_blocks))
    def _():
        sout_ref[...] = s_ref[...]
        cout_ref[...] = c_ref[...]
        nout_ref[...] = n_ref[...]
        mout_ref[...] = m_ref[...]


def _layer_prompt(proj, gates, gates_t, cs8, s0, c0, n0, m0, x2d, conv_w, plane, psub, gng, gnm,
                  wo, g1, g2, wu, wd, g3, tb):
    b, t, _ = proj.shape
    d = x2d.shape[1]
    d_ff = wu.shape[1]
    nt = t // tb
    n_blocks = b * nt
    ncb = tb // CHUNK
    hp = HEADS // 2

    def blk(g):
        gm = jnp.minimum(g, n_blocks - 1)
        return gm // nt, gm % nt

    def tok3(col):
        return lambda g: blk(g) + (col,)

    per_seq3 = lambda g: (blk(g)[0], 0, 0)
    per_seq4 = lambda g: (blk(g)[0], 0, 0, 0)
    prev_rows = lambda g: (jnp.maximum(g - 1, 0), 0)
    const = lambda g: (0, 0)
    resident = dict(pipeline_mode=pl.Buffered(1))
    return pl.pallas_call(
        functools.partial(_layer_kernel, tb=tb, nt=nt, n_blocks=n_blocks),
        grid=(n_blocks + 1,),
        in_specs=[pl.BlockSpec((None, tb, 2048), tok3(0)),
                  pl.BlockSpec((None, tb, 1024), tok3(COL_MV // 1024)),
                  pl.BlockSpec((None, tb, 512), tok3(COL_MQ // 512)),
                  pl.BlockSpec((None, tb, 128), tok3(0)),
                  pl.BlockSpec((None, ncb, 16, CHUNK), lambda g: blk(g) + (0, 0)),
                  pl.BlockSpec((None, 8, CONV_CH), per_seq3),
                  pl.BlockSpec((None, HEADS, GDN_D, GDN_D), per_seq4),
                  pl.BlockSpec((None, hp, 128, 128), per_seq4),
                  pl.BlockSpec((None, hp, 1, 128), per_seq4),
                  pl.BlockSpec((None, 1, 128), per_seq3),
                  pl.BlockSpec((tb, d), prev_rows),
                  pl.BlockSpec((CONV_W, CONV_CH), const),
                  pl.BlockSpec((3, 128), const),
                  pl.BlockSpec((16, 3), const),
                  pl.BlockSpec((1, GDN_D), const),
                  pl.BlockSpec((1, ML_DV), const),
                  pl.BlockSpec((d, d), const, **resident),
                  pl.BlockSpec((1, d), const),
                  pl.BlockSpec((1, d), const),
                  pl.BlockSpec((d, d_ff), const, **resident),
                  pl.BlockSpec((d_ff, d), const, **resident),
                  pl.BlockSpec((1, d), const)],
        out_specs=[pl.BlockSpec((tb, d), prev_rows),
                   pl.BlockSpec((None, HEADS, GDN_D, GDN_D), per_seq4),
                   pl.BlockSpec((None, hp, 128, 128), per_seq4),
                   pl.BlockSpec((None, hp, 1, 128), per_seq4),
                   pl.BlockSpec((None, 1, 128), per_seq3)],
        out_shape=[jax.ShapeDtypeStruct((b * t, d), F32),
                   jax.ShapeDtypeStruct((b, HEADS, GDN_D, GDN_D), F32),
                   jax.ShapeDtypeStruct((b, hp, 128, 128), F32),
                   jax.ShapeDtypeStruct((b, hp, 1, 128), F32),
                   jax.ShapeDtypeStruct((b, 1, 128), F32)],
        scratch_shapes=[pltpu.VMEM((tb + 8, CONV_CH), F32),
                        pltpu.VMEM((tb, CONV_CH), F32),
                        pltpu.VMEM((HEADS, GDN_D, GDN_D), F32),
                        pltpu.VMEM((ncb, HEADS, CHUNK, GDN_D), F32),
                        pltpu.VMEM((ncb, HEADS, 2 * CHUNK, GDN_D), BF16),
                        pltpu.VMEM((ncb, HEADS, CHUNK, CHUNK), BF16),
                        pltpu.VMEM((ncb, HEADS, GDN_D, CHUNK), BF16),
                        pltpu.VMEM((ncb, HEADS, 1, GDN_D), F32),
                        pltpu.VMEM((hp, 128, 128), F32),
                        pltpu.VMEM((hp, 1, 128), F32),
                        pltpu.VMEM((1, 128), F32),
                        pltpu.VMEM((2, tb, HEADS * GDN_D), BF16),
                        pltpu.VMEM((2, tb, HEADS * ML_DV), BF16),
                        pltpu.VMEM((tb, d), BF16),
                        pltpu.VMEM((tb, d), F32)],
        compiler_params=pltpu.CompilerParams(dimension_semantics=("arbitrary",),
                                             vmem_limit_bytes=VMEM_LIMIT),
        name="layer_prompt",
    )(proj, proj, proj, gates, gates_t, cs8, s0, c0, n0, m0, x2d, conv_w, plane, psub, gng, gnm,
      wo, g1, g2, wu, wd, g3)


def _decode_kernel(proj_ref, gates_ref, cs_ref, s0_ref, c0_ref, n0_ref, m0_ref, cw_ref, plane_ref, gng_ref, gnm_ref,
                   og_ref, om_ref, sout_ref, cout_ref, nout_ref, mout_ref, rowg_ref, rowm_ref, *, bb):
    lane = lax.broadcasted_iota(jnp.int32, (1, 128), 1)
    gt = gates_ref[...]
    beta_t = _sigmoid(gt)
    g_t = -jnp.exp(plane_ref[0:1, :]) * _softplus(gt + plane_ref[1:2, :])
    eg_t = jnp.exp(g_t)
    gb_t = gt + plane_ref[2:3, :]
    lf_t = -_softplus(-gb_t)

    heads = []
    for h in range(HEADS):
        cols = []
        for part in range(3):
            c0 = part * 512 + h * GDN_D
            w = cw_ref[:, c0:c0 + GDN_D]
            acc = cs_ref[:, 0, c0:c0 + GDN_D] * w[0:1, :]
            acc = acc + cs_ref[:, 1, c0:c0 + GDN_D] * w[1:2, :]
            acc = acc + cs_ref[:, 2, c0:c0 + GDN_D] * w[2:3, :]
            acc = acc + proj_ref[:, COL_QKV + c0:COL_QKV + c0 + GDN_D] * w[3:4, :]
            cols.append(acc * _sigmoid(acc))
        q, k, v = cols
        q = q * lax.rsqrt(jnp.sum(q * q, axis=-1, keepdims=True) + EPS) * (GDN_D ** -0.5)
        k = k * lax.rsqrt(jnp.sum(k * k, axis=-1, keepdims=True) + EPS)
        heads.append(dict(v=v, qk=jnp.sum(q * k, axis=-1, keepdims=True), k_t=k.T.astype(BF16),
                          kq=jnp.concatenate([k, q], axis=0).astype(BF16),
                          beta=beta_t[:, LANE_GB + h:LANE_GB + h + 1],
                          eg=eg_t[:, LANE_GA + h:LANE_GA + h + 1]))
    items = [dict(h=h, b=b) for h in range(HEADS) for b in range(bb)]
    row_b = lax.broadcasted_iota(jnp.int32, (bb, 1), 0)
    for it in items:
        r = jnp.dot(heads[it["h"]]["kq"], s0_ref[it["b"], it["h"]].astype(BF16), preferred_element_type=F32)
        it["ks"] = r[it["b"]:it["b"] + 1, :]
        it["qs"] = r[bb + it["b"]:bb + it["b"] + 1, :]
    for it in items:
        hd, b = heads[it["h"]], it["b"]
        eg_b = hd["eg"][b:b + 1, :]
        it["eg_b"] = eg_b
        it["u"] = hd["beta"][b:b + 1, :] * (hd["v"][b:b + 1, :] - eg_b * it["ks"])
        rowg_ref[it["h"], b:b + 1, :] = eg_b * it["qs"] + hd["qk"][b:b + 1, :] * it["u"]
    for it in items:
        u_sel = jnp.where(row_b == it["b"], it["u"], 0.0).astype(BF16)
        upd = jnp.dot(heads[it["h"]]["k_t"], u_sel, preferred_element_type=F32)
        sout_ref[it["b"], it["h"]] = it["eg_b"] * s0_ref[it["b"], it["h"]] + upd
    for h in range(HEADS):
        z = proj_ref[:, COL_Z + h * GDN_D:COL_Z + (h + 1) * GDN_D]
        out = _rms(rowg_ref[h], gng_ref[...]) * (z * _sigmoid(z))
        og_ref[:, h * GDN_D:(h + 1) * GDN_D] = out.astype(og_ref.dtype)

    heads = []
    for p in range(HEADS // 2):
        qb = proj_ref[:, COL_MQ + p * 128:COL_MQ + (p + 1) * 128]
        kb = proj_ref[:, COL_MK + p * 128:COL_MK + (p + 1) * 128] * (ML_DK ** -0.5)
        n_p = n0_ref[:, p * 128:(p + 1) * 128]
        k_t = jnp.concatenate([jnp.where(lane < ML_DK, kb, 0.0), jnp.where(lane >= ML_DK, kb, 0.0)],
                              axis=0).T.astype(BF16)
        qq = jnp.concatenate([jnp.where(lane < ML_DK, qb, 0.0), jnp.where(lane >= ML_DK, qb, 0.0)],
                             axis=0).astype(BF16)
        w_prev, p_in = [], []
        for e in range(2):
            h = 2 * p + e
            lm = (lane >= e * ML_DK) & (lane < (e + 1) * ML_DK)
            ig = gb_t[:, LANE_MI + h:LANE_MI + h + 1]
            lf = lf_t[:, LANE_MF + h:LANE_MF + h + 1]
            m_old = m0_ref[:, h:h + 1]
            m_new = jnp.maximum(lf + m_old, ig)
            w_prev.append(jnp.exp(lf + m_old - m_new))
            p_in.append(jnp.exp(ig - m_new))
            qk = jnp.sum(jnp.where(lm, qb * kb, 0.0), axis=-1, keepdims=True)
            qn = jnp.sum(jnp.where(lm, qb * n_p, 0.0), axis=-1, keepdims=True)
            mout_ref[:, h:h + 1] = m_new
            heads.append(dict(p=p, e=e, qq=qq, k_t=k_t, w_prev=w_prev[e], p_in=p_in[e],
                              pqk=p_in[e] * qk, wqn=w_prev[e] * qn, floor=jnp.exp(-m_new),
                              v=proj_ref[:, COL_MV + h * ML_DV:COL_MV + (h + 1) * ML_DV]))
        lo_lane = lane < ML_DK
        nout_ref[:, p * 128:(p + 1) * 128] = (jnp.where(lo_lane, w_prev[0], w_prev[1]) * n_p
                                              + jnp.where(lo_lane, p_in[0], p_in[1]) * kb)
    items = [dict(h=h, b=b) for h in range(HEADS) for b in range(bb)]
    qc_pair = {(p, b): jnp.dot(heads[2 * p]["qq"], c0_ref[b, p].astype(BF16), preferred_element_type=F32)
               for p in range(HEADS // 2) for b in range(bb)}
    for it in items:
        hd, b = heads[it["h"]], it["b"]
        row = hd["e"] * bb + b
        it["qc"] = qc_pair[(hd["p"], b)][row:row + 1, :]
    for it in items:
        hd, b = heads[it["h"]], it["b"]
        pqk = hd["pqk"][b:b + 1, :]
        num = hd["w_prev"][b:b + 1, :] * it["qc"] + pqk * hd["v"][b:b + 1, :]
        den = hd["wqn"][b:b + 1, :] + pqk
        rowm_ref[it["h"], b:b + 1, :] = num / jnp.maximum(jnp.abs(den), hd["floor"][b:b + 1, :])
    row128 = lax.broadcasted_iota(jnp.int32, (128, 1), 0)
    for p in range(HEADS // 2):
        h0, h1 = heads[2 * p], heads[2 * p + 1]
        pv = jnp.concatenate([h0["p_in"] * h0["v"], h1["p_in"] * h1["v"]], axis=0)
        row_2b = lax.broadcasted_iota(jnp.int32, (2 * bb, 1), 0)
        for b in range(bb):
            pv_sel = jnp.where((row_2b == b) | (row_2b == bb + b), pv, 0.0).astype(BF16)
            upd = jnp.dot(h0["k_t"], pv_sel, preferred_element_type=F32)
            w_col = jnp.where(row128 < ML_DK, h0["w_prev"][b:b + 1, :], h1["w_prev"][b:b + 1, :])
            cout_ref[b, p] = w_col * c0_ref[b, p] + upd
    for h in range(HEADS):
        mo = proj_ref[:, COL_MO + h * ML_DV:COL_MO + (h + 1) * ML_DV]
        out = _sigmoid(mo) * _rms(rowm_ref[h], gnm_ref[...])
        om_ref[:, h * ML_DV:(h + 1) * ML_DV] = out.astype(om_ref.dtype)
    mout_ref[:, HEADS:128] = m0_ref[:, HEADS:128]


def _decode(proj, gates, cs, s0, c0, n0, m0, conv_w, plane, gng, gnm, bb):
    b = proj.shape[0]
    hp = HEADS // 2
    return pl.pallas_call(
        functools.partial(_decode_kernel, bb=bb),
        grid=(b // bb,),
        in_specs=[pl.BlockSpec((bb, COL_GATE), lambda i: (i, 0)),
                  pl.BlockSpec((bb, 128), lambda i: (i, 0)),
                  pl.BlockSpec((bb, CONV_W - 1, CONV_CH), lambda i: (i, 0, 0)),
                  pl.BlockSpec((bb, HEADS, GDN_D, GDN_D), lambda i: (i, 0, 0, 0)),
                  pl.BlockSpec((bb, hp, 128, 128), lambda i: (i, 0, 0, 0)),
                  pl.BlockSpec((bb, HEADS * ML_DK), lambda i: (i, 0)),
                  pl.BlockSpec((bb, 128), lambda i: (i, 0)),
                  pl.BlockSpec((CONV_W, CONV_CH), lambda i: (0, 0)),
                  pl.BlockSpec((3, 128), lambda i: (0, 0)),
                  pl.BlockSpec((1, GDN_D), lambda i: (0, 0)),
                  pl.BlockSpec((1, ML_DV), lambda i: (0, 0))],
        out_specs=[pl.BlockSpec((bb, HEADS * GDN_D), lambda i: (i, 0)),
                   pl.BlockSpec((bb, HEADS * ML_DV), lambda i: (i, 0)),
                   pl.BlockSpec((bb, HEADS, GDN_D, GDN_D), lambda i: (i, 0, 0, 0)),
                   pl.BlockSpec((bb, hp, 128, 128), lambda i: (i, 0, 0, 0)),
                   pl.BlockSpec((bb, HEADS * ML_DK), lambda i: (i, 0)),
                   pl.BlockSpec((bb, 128), lambda i: (i, 0))],
        out_shape=[jax.ShapeDtypeStruct((b, HEADS * GDN_D), BF16),
                   jax.ShapeDtypeStruct((b, HEADS * ML_DV), BF16),
                   jax.ShapeDtypeStruct((b, HEADS, GDN_D, GDN_D), F32),
                   jax.ShapeDtypeStruct((b, hp, 128, 128), F32),
                   jax.ShapeDtypeStruct((b, HEADS * ML_DK), F32),
                   jax.ShapeDtypeStruct((b, 128), F32)],
        scratch_shapes=[pltpu.VMEM((HEADS, bb, 128), F32), pltpu.VMEM((HEADS, bb, 128), F32)],
        compiler_params=pltpu.CompilerParams(dimension_semantics=("arbitrary",),
                                             vmem_limit_bytes=VMEM_LIMIT),
        name="decode_step",
    )(proj, gates, cs, s0, c0, n0, m0, conv_w, plane, gng, gnm)


def _lane_vec(pairs):
    v = jnp.zeros((128,), F32)
    for off, val in pairs:
        v = v.at[off:off + HEADS].set(val.astype(F32))
    return v


def _prep_params(norm_pre_mix, w_in, conv_w, a_log, dt_bias, gdn_norm_g, b_igate, b_fgate, mlstm_norm_g,
                 w_out, norm_post_mix, norm_pre_mlp, w_up, w_down, norm_post_mlp):
    alog_v = _lane_vec([(LANE_GA, a_log[0])])
    dtb_v = _lane_vec([(LANE_GA, dt_bias[0])])
    bias_v = _lane_vec([(LANE_MI, b_igate[0]), (LANE_MF, b_fgate[0])])
    return dict(
        w_in=jnp.swapaxes(w_in[0], 0, 1), wo=w_out[0].astype(BF16), wu=w_up[0].astype(BF16),
        wd=w_down[0].astype(BF16),
        g_pre=norm_pre_mix[0][None, :], g1=norm_post_mix[0][None, :], g2=norm_pre_mlp[0][None, :],
        g3=norm_post_mlp[0][None, :], cw=conv_w[0], gng=gdn_norm_g[0][None, :], gnm=mlstm_norm_g[0][None, :],
        plane=jnp.stack([alog_v, dtb_v, bias_v]),
        psub=jnp.stack([alog_v[:16], dtb_v[:16], bias_v[:16]], axis=1),
    )


def _prompt_path(x, conv0, s0, c0, n0, m0, prm, w_in_r, tb, tm):
    bsz, seq, d = x.shape
    hp = HEADS // 2
    x2d = x.reshape(bsz * seq, d)
    proj, gates, gates_t = _in_proj(x2d, prm["g_pre"], w_in_r, tm=tm, out_dtype=BF16)
    proj = proj.reshape(bsz, seq, COL_GATE)
    gates = gates.reshape(bsz, seq, 128)
    gates_t = gates_t.reshape(bsz, seq // CHUNK, 16, CHUNK)
    cs8 = jnp.pad(conv0, ((0, 0), (8 - (CONV_W - 1), 0), (0, 0)))
    y, s_new, c_new, n_new, m_new = _layer_prompt(
        proj, gates, gates_t, cs8, s0, c0.reshape(bsz, hp, 128, 128), n0.reshape(bsz, hp, 1, 128),
        jnp.pad(m0, ((0, 0), (0, 128 - HEADS)))[:, None, :], x2d, prm["cw"], prm["plane"], prm["psub"],
        prm["gng"], prm["gnm"], prm["wo"], prm["g1"], prm["g2"], prm["wu"], prm["wd"], prm["g3"], tb=tb)
    y = y.reshape(bsz, seq, d)
    xp = jnp.concatenate([conv0, proj[:, seq - (CONV_W - 1):, COL_QKV:COL_QKV + CONV_CH].astype(F32)], axis=1)
    conv_new = xp[:, -(CONV_W - 1):]
    return y, (conv_new, s_new, c_new.reshape(bsz, HEADS, ML_DK, ML_DV), n_new.reshape(bsz, HEADS, ML_DK),
               m_new[:, 0, :HEADS])


def _sample_path(x, conv0, s0, c0, n0, m0, prm):
    dec, _, d = x.shape
    hp = HEADS // 2
    xs = x.reshape(dec, d)
    proj, gates, _, w_in_r = _in_proj_regroup(xs, prm["g_pre"], prm["w_in"])
    og, om, s_new, c_new, n_new, m_new = _decode(
        proj, gates, conv0, s0, c0.reshape(dec, hp, 128, 128), n0.reshape(dec, HEADS * ML_DK),
        jnp.pad(m0, ((0, 0), (0, 128 - HEADS))), prm["cw"], prm["plane"], prm["gng"], prm["gnm"], bb=8)
    y = _out_mlp(og, om, xs, prm["wo"], prm["g1"], prm["g2"], prm["wu"], prm["wd"], prm["g3"],
                 tm=dec).reshape(dec, 1, d)
    conv_new = jnp.concatenate([conv0[:, 1:, :], proj[:, None, COL_QKV:COL_QKV + CONV_CH]], axis=1)
    return y, (conv_new, s_new, c_new.reshape(dec, HEADS, ML_DK, ML_DV), n_new.reshape(dec, HEADS, ML_DK),
               m_new[:, :HEADS]), w_in_r


def kernel(x_prompt, x_sample, state_gdn_conv, state_gdn_S, state_mlstm_C, state_mlstm_n, state_mlstm_m,
           norm_pre_mix, w_in, conv_w, a_log, dt_bias, gdn_norm_g, b_igate, b_fgate, mlstm_norm_g, w_out,
           norm_post_mix, norm_pre_mlp, w_up, w_down, norm_post_mlp):
    bsz = x_prompt.shape[0]
    prm = _prep_params(norm_pre_mix, w_in, conv_w, a_log, dt_bias, gdn_norm_g, b_igate, b_fgate, mlstm_norm_g,
                       w_out, norm_post_mix, norm_pre_mlp, w_up, w_down, norm_post_mlp)
    y_s, s_st, w_in_r = _sample_path(x_sample, state_gdn_conv[0], state_gdn_S[0], state_mlstm_C[0],
                                     state_mlstm_n[0], state_mlstm_m[0], prm)
    y_p, p_st = _prompt_path(
        x_prompt, jnp.zeros((bsz, CONV_W - 1, CONV_CH), F32), jnp.zeros((bsz, HEADS, GDN_D, GDN_D), F32),
        jnp.zeros((bsz, HEADS, ML_DK, ML_DV), F32), jnp.zeros((bsz, HEADS, ML_DK), F32),
        jnp.zeros((bsz, HEADS), F32), prm, w_in_r, tb=256, tm=1024)
    return (y_p, y_s) + tuple(a[None] for a in p_st) + tuple(a[None] for a in s_st)
```

```python
import functools

import jax
import jax.numpy as jnp
from jax import lax
from jax.experimental import pallas as pl
from jax.experimental.pallas import tpu as pltpu

F32 = jnp.float32
BF16 = jnp.bfloat16
EPS = 1e-6

HEADS = 4
GDN_D = 128
ML_DK = 64
ML_DV = 128
CONV_W = 4
CONV_CH = 3 * HEADS * GDN_D
CHUNK = 64

COL_QKV = 0
COL_Z = 1536
COL_MV = 2048
COL_MO = 2560
COL_MQ = 3072
COL_MK = 3328
COL_GATE = 3584
N_PROJ = COL_GATE + 128
LANE_GB, LANE_GA, LANE_MI, LANE_MF = 0, 4, 8, 12

V7X_VMEM_BYTES = 64 * 1024 * 1024
VMEM_LIMIT = V7X_VMEM_BYTES - 8 * 1024 * 1024


def _rms(x, g):
    return x * lax.rsqrt(jnp.mean(x * x, axis=-1, keepdims=True) + EPS) * g


def _softplus(x):
    return jnp.maximum(x, 0.0) + jnp.log1p(jnp.exp(-jnp.abs(x)))


def _sigmoid(x):
    return 1.0 / (1.0 + jnp.exp(-x))


def _mm(a, b):
    return jnp.dot(a.astype(BF16), b.astype(BF16), preferred_element_type=F32)


def _mm_nt(a, b):
    return lax.dot_general(a.astype(BF16), b.astype(BF16), (((1,), (1,)), ((), ())),
                           preferred_element_type=F32)


W_IN_COLS = 3600
W_IN_SPANS = ((0, COL_QKV, 2048),
              (2568, COL_MV, 1024),
              (2056, COL_MQ, 512),
              (2048, COL_GATE, 8),
              (3592, COL_GATE + 8, 8))


def _project_rows(x_ref, g_ref, w_ref, o_ref, gc_ref, gt_ref, h_ref, n_chunk):
    h_ref[...] = _rms(x_ref[...], g_ref[...]).astype(BF16)
    n = o_ref.shape[1]
    for c0 in range(0, n, n_chunk):
        c1 = min(c0 + n_chunk, n)
        o_ref[:, c0:c1] = _mm_nt(h_ref[...], w_ref[c0:c1, :]).astype(o_ref.dtype)
    gates = _mm_nt(h_ref[...], w_ref[COL_GATE:COL_GATE + 128, :])
    gc_ref[...] = gates
    gt = gates.T
    for c in range(gt_ref.shape[0]):
        gt_ref[c] = gt[0:16, c * CHUNK:(c + 1) * CHUNK]


def _in_proj_regroup_kernel(x_ref, g_ref, win_ref, o_ref, gc_ref, gt_ref, w_ref, h_ref, *, n_chunk):
    w_ref[COL_GATE:N_PROJ, :] = jnp.zeros((N_PROJ - COL_GATE, w_ref.shape[1]), BF16)
    for src, dst, width in W_IN_SPANS:
        for r0 in range(0, width, 512):
            r1 = min(r0 + 512, width)
            w_ref[dst + r0:dst + r1, :] = win_ref[src + r0:src + r1, :].astype(BF16)
    _project_rows(x_ref, g_ref, w_ref, o_ref, gc_ref, gt_ref, h_ref, n_chunk)


def _in_proj_kernel(x_ref, g_ref, w_ref, o_ref, gc_ref, gt_ref, h_ref, *, n_chunk):
    _project_rows(x_ref, g_ref, w_ref, o_ref, gc_ref, gt_ref, h_ref, n_chunk)


def _in_proj_regroup(x2d, g, w_t):
    m, k = x2d.shape
    full = lambda i: (0, 0)
    return pl.pallas_call(
        functools.partial(_in_proj_regroup_kernel, n_chunk=512),
        grid=(1,),
        in_specs=[pl.BlockSpec((m, k), full), pl.BlockSpec((1, k), full), pl.BlockSpec((W_IN_COLS, k), full)],
        out_specs=[pl.BlockSpec((m, COL_GATE), full), pl.BlockSpec((m, 128), full),
                   pl.BlockSpec((m // CHUNK, 16, CHUNK), lambda i: (0, 0, 0)), pl.BlockSpec((N_PROJ, k), full)],
        out_shape=[jax.ShapeDtypeStruct((m, COL_GATE), F32),
                   jax.ShapeDtypeStruct((m, 128), F32),
                   jax.ShapeDtypeStruct((m // CHUNK, 16, CHUNK), F32),
                   jax.ShapeDtypeStruct((N_PROJ, k), BF16)],
        scratch_shapes=[pltpu.VMEM((m, k), BF16)],
        compiler_params=pltpu.CompilerParams(dimension_semantics=("arbitrary",),
                                             vmem_limit_bytes=VMEM_LIMIT),
        name="in_proj_regroup",
    )(x2d, g, w_t)


def _in_proj(x2d, g, w, tm, out_dtype):
    m, k = x2d.shape
    return pl.pallas_call(
        functools.partial(_in_proj_kernel, n_chunk=512),
        grid=(m // tm,),
        in_specs=[pl.BlockSpec((tm, k), lambda i: (i, 0)),
                  pl.BlockSpec((1, k), lambda i: (0, 0)),
                  pl.BlockSpec((N_PROJ, k), lambda i: (0, 0), pipeline_mode=pl.Buffered(1))],
        out_specs=[pl.BlockSpec((tm, COL_GATE), lambda i: (i, 0)),
                   pl.BlockSpec((tm, 128), lambda i: (i, 0)),
                   pl.BlockSpec((tm // CHUNK, 16, CHUNK), lambda i: (i, 0, 0))],
        out_shape=[jax.ShapeDtypeStruct((m, COL_GATE), out_dtype),
                   jax.ShapeDtypeStruct((m, 128), F32),
                   jax.ShapeDtypeStruct((m // CHUNK, 16, CHUNK), F32)],
        scratch_shapes=[pltpu.VMEM((tm, k), BF16)],
        compiler_params=pltpu.CompilerParams(dimension_semantics=("arbitrary",),
                                             vmem_limit_bytes=VMEM_LIMIT),
        name="in_proj",
    )(x2d, g, w)


def _out_mlp_kernel(og_ref, om_ref, x_ref, wo_ref, g1_ref, g2_ref, wu_ref, wd_ref, g3_ref,
                    y_ref, hn_ref, acc_ref, *, ff_chunk):
    half = og_ref.shape[1]
    mix = (jnp.dot(og_ref[...], wo_ref[0:half, :], preferred_element_type=F32)
           + jnp.dot(om_ref[...], wo_ref[half:2 * half, :], preferred_element_type=F32))
    x1 = x_ref[...] + _rms(mix, g1_ref[...])
    y_ref[...] = x1
    hn_ref[...] = _rms(x1, g2_ref[...]).astype(BF16)
    d_ff = wu_ref.shape[1]
    for c0 in range(0, d_ff, ff_chunk):
        u = jnp.dot(hn_ref[...], wu_ref[:, c0:c0 + ff_chunk], preferred_element_type=F32)
        u = jnp.square(jnp.maximum(u, 0.0)).astype(BF16)
        d = jnp.dot(u, wd_ref[c0:c0 + ff_chunk, :], preferred_element_type=F32)
        if c0 == 0:
            acc_ref[...] = d
        else:
            acc_ref[...] += d
    y_ref[...] = y_ref[...] + _rms(acc_ref[...], g3_ref[...])


def _out_mlp(og, om, x2d, wo, g1, g2, wu, wd, g3, tm):
    m, d = x2d.shape
    half = og.shape[1]
    d_ff = wu.shape[1]
    const = lambda i: (0, 0)
    return pl.pallas_call(
        functools.partial(_out_mlp_kernel, ff_chunk=1024),
        grid=(m // tm,),
        in_specs=[pl.BlockSpec((tm, half), lambda i: (i, 0)),
                  pl.BlockSpec((tm, half), lambda i: (i, 0)),
                  pl.BlockSpec((tm, d), lambda i: (i, 0)),
                  pl.BlockSpec((d, d), const, pipeline_mode=pl.Buffered(1)),
                  pl.BlockSpec((1, d), const),
                  pl.BlockSpec((1, d), const),
                  pl.BlockSpec((d, d_ff), const, pipeline_mode=pl.Buffered(1)),
                  pl.BlockSpec((d_ff, d), const, pipeline_mode=pl.Buffered(1)),
                  pl.BlockSpec((1, d), const)],
        out_specs=pl.BlockSpec((tm, d), lambda i: (i, 0)),
        out_shape=jax.ShapeDtypeStruct((m, d), F32),
        scratch_shapes=[pltpu.VMEM((tm, d), BF16), pltpu.VMEM((tm, d), F32)],
        compiler_params=pltpu.CompilerParams(dimension_semantics=("arbitrary",),
                                             vmem_limit_bytes=VMEM_LIMIT),
        name="out_mlp",
    )(og, om, x2d, wo, g1, g2, wu, wd, g3)


def _chunk_masks():
    ii = lax.broadcasted_iota(jnp.int32, (CHUNK, CHUNK), 0)
    jj = lax.broadcasted_iota(jnp.int32, (CHUNK, CHUNK), 1)
    return ii, jj


def _cumsum_col_row(x_col, x_row, ii, jj):
    c_col = jnp.sum(jnp.where(jj <= ii, x_row, 0.0), axis=1, keepdims=True)
    c_row = jnp.sum(jnp.where(ii <= jj, x_col, 0.0), axis=0, keepdims=True)
    return c_col, c_row


def _interleave(*tasks):
    live = [[g, n, 0] for g, n in tasks]
    while live:
        entry = min(live, key=lambda e: (e[2] + 1) / e[1])
        try:
            next(entry[0])
            entry[2] += 1
        except StopIteration:
            live.remove(entry)


def _halves(items):
    mid = len(items) // 2
    return items[:mid], items[mid:]


def _gdn_phase_a(yc_ref, gcol_ref, grow_ref, plane_ref, psub_ref, wv_ref, lhs_ref, qk_ref, ket_ref, ge_ref,
                 ncb):
    ii, jj = _chunk_masks()
    incl = jj <= ii
    strict = jj < ii
    eye = (ii == jj).astype(F32)
    neg_a_lane = -jnp.exp(plane_ref[0:1, :])
    dtb_lane = plane_ref[1:2, :]
    neg_a_sub = -jnp.exp(psub_ref[:, 0:1])
    dtb_sub = psub_ref[:, 1:2]
    items = []
    for c in range(ncb):
        rows = slice(c * CHUNK, (c + 1) * CHUNK)
        gc = gcol_ref[rows, :]
        gr = grow_ref[c]
        beta_t = _sigmoid(gc)
        g_t = neg_a_lane * _softplus(gc + dtb_lane)
        g_r = neg_a_sub * _softplus(gr + dtb_sub)
        for h in range(HEADS):
            lo = h * GDN_D
            items.append(dict(c=c, h=h,
                              q=yc_ref[rows, lo:lo + GDN_D],
                              k=yc_ref[rows, 512 + lo:512 + lo + GDN_D],
                              v=yc_ref[rows, 1024 + lo:1024 + lo + GDN_D],
                              beta=beta_t[:, LANE_GB + h:LANE_GB + h + 1],
                              gg_col=g_t[:, LANE_GA + h:LANE_GA + h + 1],
                              gg_row=g_r[LANE_GA + h:LANE_GA + h + 1, :]))
    yield
    for it in items:
        it["k"] = it["k"] * lax.rsqrt(jnp.sum(it["k"] * it["k"], axis=-1, keepdims=True) + EPS)
    yield
    for it in items:
        it["kk"] = _mm_nt(it["k"], it["k"])
    yield
    for it in items:
        it["q"] = (it["q"] * lax.rsqrt(jnp.sum(it["q"] * it["q"], axis=-1, keepdims=True) + EPS)
                   * (GDN_D ** -0.5))
    yield
    for it in items:
        it["g_col"], g_row = _cumsum_col_row(it["gg_col"], it["gg_row"], ii, jj)
        it["decay"] = jnp.where(incl, jnp.exp(jnp.where(incl, it["g_col"] - g_row, 0.0)), 0.0)
    yield
    for it in items:
        n_mat = jnp.where(strict, it["beta"] * it["kk"] * it["decay"], 0.0)
        it["x"] = eye - n_mat
        it["p"] = -n_mat
    yield
    for _ in range(5):
        yield
        for it in items:
            it["p"] = _mm(it["p"], it["p"])
        yield
        for it in items:
            it["x"] = it["x"] + _mm(it["x"], it["p"])
    yield
    for it in items:
        e_g = jnp.exp(it["g_col"])
        it["e_g"] = e_g
        rhs = jnp.concatenate([it["beta"] * it["v"], (it["beta"] * e_g) * it["k"]], axis=1)
        it["w"] = _mm(it["x"], rhs)
    yield
    for it in items:
        it["qk"] = _mm_nt(it["q"], it["k"]) * it["decay"]
    yield
    for it in items:
        c, h = it["c"], it["h"]
        g_end = it["g_col"][CHUNK - 1:CHUNK, :]
        k_end = it["k"] * jnp.exp(g_end - it["g_col"])
        wv_ref[c, h] = it["w"][:, 0:GDN_D]
        lhs_ref[c, h] = jnp.concatenate([it["w"][:, GDN_D:2 * GDN_D], it["e_g"] * it["q"]],
                                        axis=0).astype(BF16)
        qk_ref[c, h] = it["qk"].astype(BF16)
        ket_ref[c, h] = k_end.T.astype(BF16)
        ge_ref[c, h] = jnp.broadcast_to(jnp.exp(g_end), (1, GDN_D))


def _gdn_phase_b(qkvz_ref, gn_ref, og_ref, s_ref, wv_ref, lhs_ref, qk_ref, ket_ref, ge_ref, ncb):
    gn = gn_ref[...]

    def epilogue(c, o):
        rows = slice(c * CHUNK, (c + 1) * CHUNK)
        for h in range(HEADS):
            lo = h * GDN_D
            z = qkvz_ref[rows, COL_Z + lo:COL_Z + lo + GDN_D].astype(F32)
            out = _rms(o[h], gn) * (z * _sigmoid(z))
            og_ref[rows, lo:lo + GDN_D] = out.astype(og_ref.dtype)

    s = [s_ref[h] for h in range(HEADS)]
    o_prev = None
    for c in range(ncb):
        r = [jnp.dot(lhs_ref[c, h], s[h].astype(BF16), preferred_element_type=F32) for h in range(HEADS)]
        yield
        if o_prev is not None:
            epilogue(c - 1, o_prev)
        ub = [(wv_ref[c, h] - r[h][0:CHUNK]).astype(BF16) for h in range(HEADS)]
        s = [ge_ref[c, h] * s[h] + jnp.dot(ket_ref[c, h], ub[h], preferred_element_type=F32)
             for h in range(HEADS)]
        o_prev = [r[h][CHUNK:2 * CHUNK] + jnp.dot(qk_ref[c, h], ub[h], preferred_element_type=F32)
                  for h in range(HEADS)]
        yield
    epilogue(ncb - 1, o_prev)
    for h in range(HEADS):
        s_ref[h] = s[h]


def _mlstm_block(mvo_ref, mqk_ref, gcol_ref, grow_ref, plane_ref, psub_ref, gn_ref, om_ref,
                 c_ref, n_ref, m_ref, ncb):
    ii, jj = _chunk_masks()
    incl = jj <= ii
    lane = lax.broadcasted_iota(jnp.int32, (1, 128), 1)
    row128 = lax.broadcasted_iota(jnp.int32, (128, 1), 0)
    gn = gn_ref[...]
    blane = plane_ref[2:3, :]
    bsub = psub_ref[:, 2:3]
    items = []
    for c in range(ncb):
        rows = slice(c * CHUNK, (c + 1) * CHUNK)
        gc = gcol_ref[rows, :] + blane
        gr = grow_ref[c] + bsub
        lf_c = -_softplus(-gc)
        lf_r = -_softplus(-gr)
        for p in range(HEADS // 2):
            qb = mqk_ref[rows, p * 128:(p + 1) * 128].astype(F32)
            kb = mqk_ref[rows, 256 + p * 128:256 + (p + 1) * 128].astype(F32) * (ML_DK ** -0.5)
            for e in range(2):
                h = 2 * p + e
                lm = (lane >= e * ML_DK) & (lane < (e + 1) * ML_DK)
                items.append(dict(c=c, rows=rows, p=p, e=e, h=h, qh=jnp.where(lm, qb, 0.0), kb=kb,
                                  kh=jnp.where(lm, kb, 0.0),
                                  lf_col=lf_c[:, LANE_MF + h:LANE_MF + h + 1],
                                  lf_row=lf_r[LANE_MF + h:LANE_MF + h + 1, :],
                                  ig_row=gr[LANE_MI + h:LANE_MI + h + 1, :],
                                  ig_col=gc[:, LANE_MI + h:LANE_MI + h + 1],
                                  v=mvo_ref[rows, h * ML_DV:(h + 1) * ML_DV]))
        if c % 2 == 1:
            yield
    for part in _halves(items):
        for it in part:
            it["qk"] = _mm_nt(it["qh"], it["kb"])
        yield
    for part in _halves(items):
        for it in part:
            it["f_col"], it["f_row"] = _cumsum_col_row(it["lf_col"], it["lf_row"], ii, jj)
        yield
    for part in _halves(items):
        for it in part:
            it["d_mat"] = jnp.where(incl, it["f_col"] - it["f_row"] + it["ig_row"], -jnp.inf)
            it["d_max"] = jnp.max(it["d_mat"], axis=1, keepdims=True)
        yield
    for part in _halves(items):
        for it in part:
            p0 = jnp.where(incl, jnp.exp(jnp.where(incl, it["d_mat"] - it["d_max"], 0.0)), 0.0)
            pend0 = jnp.exp(it["f_col"][CHUNK - 1:CHUNK, :] - it["f_col"] + it["ig_col"]
                            - it["d_max"][CHUNK - 1:CHUNK, :])
            it["kp0"] = it["kh"] * pend0
            it["pqk0"] = p0 * it["qk"]
        yield
    for part in _halves(items):
        for it in part:
            it["pv0"] = _mm(it["pqk0"], it["v"])
            it["rs0"] = jnp.sum(it["pqk0"], axis=-1, keepdims=True)
        yield
    for part in _halves(items):
        for it in part:
            it["cadd0"] = _mm(it["kp0"].T, it["v"])
            it["nadd0"] = jnp.sum(it["kp0"], axis=0, keepdims=True)
        yield

    m_cur = [m_ref[:, h:h + 1] for h in range(HEADS)]
    for it in items:
        h = it["h"]
        it["m_prev"] = m_cur[h]
        m_cur[h] = jnp.maximum(it["f_col"][CHUNK - 1:CHUNK, :] + m_cur[h], it["d_max"][CHUNK - 1:CHUNK, :])
    yield
    for part in _halves(items):
        for it in part:
            bcol = it["f_col"] + it["m_prev"]
            mt = jnp.maximum(bcol, it["d_max"])
            it["mt"] = mt
            it["w_prev"] = jnp.exp(bcol - mt)
            it["sc"] = jnp.exp(it["d_max"] - mt)
        yield
    c_cur = [c_ref[p] for p in range(HEADS // 2)]
    n_cur = [n_ref[p] for p in range(HEADS // 2)]
    for c in range(ncb):
        for p in range(HEADS // 2):
            pair = [it for it in items if it["c"] == c and it["p"] == p]
            w_end = [it["w_prev"][CHUNK - 1:CHUNK, :] for it in pair]
            s_end = [it["sc"][CHUNK - 1:CHUNK, :] for it in pair]
            for it in pair:
                it["c_prev"] = c_cur[p]
                it["n_prev"] = n_cur[p]
            c_cur[p] = (jnp.where(row128 < ML_DK, w_end[0], w_end[1]) * c_cur[p]
                        + s_end[0] * pair[0]["cadd0"] + s_end[1] * pair[1]["cadd0"])
            n_cur[p] = (jnp.where(lane < ML_DK, w_end[0], w_end[1]) * n_cur[p]
                        + s_end[0] * pair[0]["nadd0"] + s_end[1] * pair[1]["nadd0"])
        if c % 2 == 1:
            yield
    for part in _halves(items):
        for it in part:
            it["qc"] = _mm(it["qh"], it["c_prev"])
        yield
    for part in _halves(items):
        for it in part:
            num = it["w_prev"] * it["qc"] + it["sc"] * it["pv0"]
            den = (it["w_prev"] * jnp.sum(it["qh"] * it["n_prev"], axis=-1, keepdims=True)
                   + it["sc"] * it["rs0"])
            it["hh"] = num / jnp.maximum(jnp.abs(den), jnp.exp(-it["mt"]))
        yield
    for part in _halves(items):
        for it in part:
            h = it["h"]
            mo = mvo_ref[it["rows"], 512 + h * ML_DV:512 + (h + 1) * ML_DV].astype(F32)
            out = _sigmoid(mo) * _rms(it["hh"], gn)
            om_ref[it["rows"], h * ML_DV:(h + 1) * ML_DV] = out.astype(om_ref.dtype)
        yield
    for p in range(HEADS // 2):
        c_ref[p] = c_cur[p]
        n_ref[p] = n_cur[p]
    for h in range(HEADS):
        m_ref[:, h:h + 1] = m_cur[h]


def _chain(*gens):
    for g in gens:
        yield from g


def _conv_stage(xp_ref, yc_ref, cw_ref, tb):
    for ct in range(CONV_CH // 128):
        cols = slice(ct * 128, (ct + 1) * 128)
        w = cw_ref[:, cols]
        for r0 in range(0, tb, 128):
            acc = xp_ref[8 + r0:8 + r0 + 128, cols] * w[CONV_W - 1:CONV_W, :]
            for j in range(CONV_W - 1):
                acc = acc + xp_ref[5 + j + r0:5 + j + r0 + 128, cols] * w[j:j + 1, :]
            yc_ref[r0:r0 + 128, cols] = acc * _sigmoid(acc)
        yield


def _out_mlp_block(og_ref, om_ref, x_ref, wo_ref, g1_ref, g2_ref, wu_ref, wd_ref, g3_ref,
                   y_ref, hn_ref, acc_ref, ff_chunk, n_split):
    half = og_ref.shape[1]
    d = x_ref.shape[1]
    d_ff = wu_ref.shape[1]
    col_groups = [slice(c, c + d // n_split) for c in range(0, d, d // n_split)]
    for cg in col_groups:
        acc_ref[:, cg] = (jnp.dot(og_ref[...], wo_ref[0:half, cg], preferred_element_type=F32)
                          + jnp.dot(om_ref[...], wo_ref[half:2 * half, cg], preferred_element_type=F32))
        yield
    x1 = x_ref[...] + _rms(acc_ref[...], g1_ref[...])
    y_ref[...] = x1
    hn_ref[...] = _rms(x1, g2_ref[...]).astype(BF16)
    yield
    for c0 in range(0, d_ff, ff_chunk):
        u = jnp.dot(hn_ref[...], wu_ref[:, c0:c0 + ff_chunk], preferred_element_type=F32)
        u = jnp.square(jnp.maximum(u, 0.0)).astype(BF16)
        for cg in col_groups:
            dd = jnp.dot(u, wd_ref[c0:c0 + ff_chunk, cg], preferred_element_type=F32)
            if c0 == 0:
                acc_ref[:, cg] = dd
            else:
                acc_ref[:, cg] += dd
        yield
    y_ref[...] = y_ref[...] + _rms(acc_ref[...], g3_ref[...])
    yield


def _layer_kernel(qkvz_ref, mvo_ref, mqk_ref, gcol_ref, grow_ref, cs_ref, s0_ref, c0_ref, n0_ref, m0_ref,
                  x_ref, cw_ref, plane_ref, psub_ref, gng_ref, gnm_ref,
                  wo_ref, g1_ref, g2_ref, wu_ref, wd_ref, g3_ref,
                  y_ref, sout_ref, cout_ref, nout_ref, mout_ref,
                  xp_ref, yc_ref, s_ref, wv_ref, lhs_ref, qk_ref, ket_ref, ge_ref, c_ref, n_ref, m_ref,
                  og_ref, om_ref, hn_ref, acc_ref, *, tb, nt, n_blocks):
    g = pl.program_id(0)
    t = jnp.minimum(g, n_blocks - 1) % nt
    ncb = tb // CHUNK
    par = g % 2

    @pl.when(g == 0)
    def _():
        og_ref[1] = jnp.zeros(og_ref.shape[1:], og_ref.dtype)
        om_ref[1] = jnp.zeros(om_ref.shape[1:], om_ref.dtype)

    @pl.when(t == 0)
    def _():
        xp_ref[0:8, :] = cs_ref[...]
        s_ref[...] = s0_ref[...]
        c_ref[...] = c0_ref[...]
        n_ref[...] = n0_ref[...]
        m_ref[...] = m0_ref[...]

    @pl.when(t > 0)
    def _():
        xp_ref[0:8, :] = xp_ref[tb:tb + 8, :]

    xp_ref[8:tb + 8, :] = qkvz_ref[:, COL_QKV:COL_QKV + CONV_CH].astype(F32)

    og_w, om_w = og_ref.at[par], om_ref.at[par]
    og_r, om_r = og_ref.at[1 - par], om_ref.at[1 - par]
    _interleave(
        (_chain(_conv_stage(xp_ref, yc_ref, cw_ref, tb),
                _gdn_phase_a(yc_ref, gcol_ref, grow_ref, plane_ref, psub_ref, wv_ref, lhs_ref, qk_ref, ket_ref,
                             ge_ref, ncb),
                _gdn_phase_b(qkvz_ref, gng_ref, og_w, s_ref, wv_ref, lhs_ref, qk_ref, ket_ref, ge_ref, ncb)),
         CONV_CH // 128 + 20 + 2 * ncb),
        (_out_mlp_block(og_r, om_r, x_ref, wo_ref, g1_ref, g2_ref, wu_ref, wd_ref, g3_ref, y_ref, hn_ref, acc_ref,
                        ff_chunk=256, n_split=2), 2 + 1 + wu_ref.shape[1] // 256 + 1),
        (_mlstm_block(mvo_ref, mqk_ref, gcol_ref, grow_ref, plane_ref, psub_ref, gnm_ref, om_w,
                      c_ref, n_ref, m_ref, ncb), 21 + ncb))

    @pl.when((t == nt - 1) & (g < n_blocks))
    def _():
        sout_ref[...] = s_ref[...]
        cout_ref[...] = c_ref[...]
        nout_ref[...] = n_ref[...]
        mout_ref[...] = m_ref[...]


def _layer_prompt(proj, gates, gates_t, cs8, s0, c0, n0, m0, x2d, conv_w, plane, psub, gng, gnm,
                  wo, g1, g2, wu, wd, g3, tb):
    b, t, _ = proj.shape
    d = x2d.shape[1]
    d_ff = wu.shape[1]
    nt = t // tb
    n_blocks = b * nt
    ncb = tb // CHUNK
    hp = HEADS // 2

    def blk(g):
        gm = jnp.minimum(g, n_blocks - 1)
        return gm // nt, gm % nt

    def tok3(col):
        return lambda g: blk(g) + (col,)

    per_seq3 = lambda g: (blk(g)[0], 0, 0)
    per_seq4 = lambda g: (blk(g)[0], 0, 0, 0)
    prev_rows = lambda g: (jnp.maximum(g - 1, 0), 0)
    const = lambda g: (0, 0)
    resident = dict(pipeline_mode=pl.Buffered(1))
    return pl.pallas_call(
        functools.partial(_layer_kernel, tb=tb, nt=nt, n_blocks=n_blocks),
        grid=(n_blocks + 1,),
        in_specs=[pl.BlockSpec((None, tb, 2048), tok3(0)),
                  pl.BlockSpec((None, tb, 1024), tok3(COL_MV // 1024)),
                  pl.BlockSpec((None, tb, 512), tok3(COL_MQ // 512)),
                  pl.BlockSpec((None, tb, 128), tok3(0)),
                  pl.BlockSpec((None, ncb, 16, CHUNK), lambda g: blk(g) + (0, 0)),
                  pl.BlockSpec((None, 8, CONV_CH), per_seq3),
                  pl.BlockSpec((None, HEADS, GDN_D, GDN_D), per_seq4),
                  pl.BlockSpec((None, hp, 128, 128), per_seq4),
                  pl.BlockSpec((None, hp, 1, 128), per_seq4),
                  pl.BlockSpec((None, 1, 128), per_seq3),
                  pl.BlockSpec((tb, d), prev_rows),
                  pl.BlockSpec((CONV_W, CONV_CH), const),
                  pl.BlockSpec((3, 128), const),
                  pl.BlockSpec((16, 3), const),
                  pl.BlockSpec((1, GDN_D), const),
                  pl.BlockSpec((1, ML_DV), const),
                  pl.BlockSpec((d, d), const, **resident),
                  pl.BlockSpec((1, d), const),
                  pl.BlockSpec((1, d), const),
                  pl.BlockSpec((d, d_ff), const, **resident),
                  pl.BlockSpec((d_ff, d), const, **resident),
                  pl.BlockSpec((1, d), const)],
        out_specs=[pl.BlockSpec((tb, d), prev_rows),
                   pl.BlockSpec((None, HEADS, GDN_D, GDN_D), per_seq4),
                   pl.BlockSpec((None, hp, 128, 128), per_seq4),
                   pl.BlockSpec((None, hp, 1, 128), per_seq4),
                   pl.BlockSpec((None, 1, 128), per_seq3)],
        out_shape=[jax.ShapeDtypeStruct((b * t, d), F32),
                   jax.ShapeDtypeStruct((b, HEADS, GDN_D, GDN_D), F32),
                   jax.ShapeDtypeStruct((b, hp, 128, 128), F32),
                   jax.ShapeDtypeStruct((b, hp, 1, 128), F32),
                   jax.ShapeDtypeStruct((b, 1, 128), F32)],
        scratch_shapes=[pltpu.VMEM((tb + 8, CONV_CH), F32),
                        pltpu.VMEM((tb, CONV_CH), F32),
                        pltpu.VMEM((HEADS, GDN_D, GDN_D), F32),
                        pltpu.VMEM((ncb, HEADS, CHUNK, GDN_D), F32),
                        pltpu.VMEM((ncb, HEADS, 2 * CHUNK, GDN_D), BF16),
                        pltpu.VMEM((ncb, HEADS, CHUNK, CHUNK), BF16),
                        pltpu.VMEM((ncb, HEADS, GDN_D, CHUNK), BF16),
                        pltpu.VMEM((ncb, HEADS, 1, GDN_D), F32),
                        pltpu.VMEM((hp, 128, 128), F32),
                        pltpu.VMEM((hp, 1, 128), F32),
                        pltpu.VMEM((1, 128), F32),
                        pltpu.VMEM((2, tb, HEADS * GDN_D), BF16),
                        pltpu.VMEM((2, tb, HEADS * ML_DV), BF16),
                        pltpu.VMEM((tb, d), BF16),
                        pltpu.VMEM((tb, d), F32)],
        compiler_params=pltpu.CompilerParams(dimension_semantics=("arbitrary",),
                                             vmem_limit_bytes=VMEM_LIMIT),
        name="layer_prompt",
    )(proj, proj, proj, gates, gates_t, cs8, s0, c0, n0, m0, x2d, conv_w, plane, psub, gng, gnm,
      wo, g1, g2, wu, wd, g3)


def _decode_kernel(proj_ref, gates_ref, cs_ref, s0_ref, c0_ref, n0_ref, m0_ref, cw_ref, plane_ref, gng_ref, gnm_ref,
                   og_ref, om_ref, sout_ref, cout_ref, nout_ref, mout_ref, rowg_ref, rowm_ref, *, bb):
    lane = lax.broadcasted_iota(jnp.int32, (1, 128), 1)
    gt = gates_ref[...]
    beta_t = _sigmoid(gt)
    g_t = -jnp.exp(plane_ref[0:1, :]) * _softplus(gt + plane_ref[1:2, :])
    eg_t = jnp.exp(g_t)
    gb_t = gt + plane_ref[2:3, :]
    lf_t = -_softplus(-gb_t)

    heads = []
    for h in range(HEADS):
        cols = []
        for part in range(3):
            c0 = part * 512 + h * GDN_D
            w = cw_ref[:, c0:c0 + GDN_D]
            acc = cs_ref[:, 0, c0:c0 + GDN_D] * w[0:1, :]
            acc = acc + cs_ref[:, 1, c0:c0 + GDN_D] * w[1:2, :]
            acc = acc + cs_ref[:, 2, c0:c0 + GDN_D] * w[2:3, :]
            acc = acc + proj_ref[:, COL_QKV + c0:COL_QKV + c0 + GDN_D] * w[3:4, :]
            cols.append(acc * _sigmoid(acc))
        q, k, v = cols
        q = q * lax.rsqrt(jnp.sum(q * q, axis=-1, keepdims=True) + EPS) * (GDN_D ** -0.5)
        k = k * lax.rsqrt(jnp.sum(k * k, axis=-1, keepdims=True) + EPS)
        heads.append(dict(v=v, qk=jnp.sum(q * k, axis=-1, keepdims=True), k_t=k.T.astype(BF16),
                          kq=jnp.concatenate([k, q], axis=0).astype(BF16),
                          beta=beta_t[:, LANE_GB + h:LANE_GB + h + 1],
                          eg=eg_t[:, LANE_GA + h:LANE_GA + h + 1]))
    items = [dict(h=h, b=b) for h in range(HEADS) for b in range(bb)]
    row_b = lax.broadcasted_iota(jnp.int32, (bb, 1), 0)
    for it in items:
        r = jnp.dot(heads[it["h"]]["kq"], s0_ref[it["b"], it["h"]].astype(BF16), preferred_element_type=F32)
        it["ks"] = r[it["b"]:it["b"] + 1, :]
        it["qs"] = r[bb + it["b"]:bb + it["b"] + 1, :]
    for it in items:
        hd, b = heads[it["h"]], it["b"]
        eg_b = hd["eg"][b:b + 1, :]
        it["eg_b"] = eg_b
        it["u"] = hd["beta"][b:b + 1, :] * (hd["v"][b:b + 1, :] - eg_b * it["ks"])
        rowg_ref[it["h"], b:b + 1, :] = eg_b * it["qs"] + hd["qk"][b:b + 1, :] * it["u"]
    for it in items:
        u_sel = jnp.where(row_b == it["b"], it["u"], 0.0).astype(BF16)
        upd = jnp.dot(heads[it["h"]]["k_t"], u_sel, preferred_element_type=F32)
        sout_ref[it["b"], it["h"]] = it["eg_b"] * s0_ref[it["b"], it["h"]] + upd
    for h in range(HEADS):
        z = proj_ref[:, COL_Z + h * GDN_D:COL_Z + (h + 1) * GDN_D]
        out = _rms(rowg_ref[h], gng_ref[...]) * (z * _sigmoid(z))
        og_ref[:, h * GDN_D:(h + 1) * GDN_D] = out.astype(og_ref.dtype)

    heads = []
    for p in range(HEADS // 2):
        qb = proj_ref[:, COL_MQ + p * 128:COL_MQ + (p + 1) * 128]
        kb = proj_ref[:, COL_MK + p * 128:COL_MK + (p + 1) * 128] * (ML_DK ** -0.5)
        n_p = n0_ref[:, p * 128:(p + 1) * 128]
        k_t = jnp.concatenate([jnp.where(lane < ML_DK, kb, 0.0), jnp.where(lane >= ML_DK, kb, 0.0)],
                              axis=0).T.astype(BF16)
        qq = jnp.concatenate([jnp.where(lane < ML_DK, qb, 0.0), jnp.where(lane >= ML_DK, qb, 0.0)],
                             axis=0).astype(BF16)
        w_prev, p_in = [], []
        for e in range(2):
            h = 2 * p + e
            lm = (lane >= e * ML_DK) & (lane < (e + 1) * ML_DK)
            ig = gb_t[:, LANE_MI + h:LANE_MI + h + 1]
            lf = lf_t[:, LANE_MF + h:LANE_MF + h + 1]
            m_old = m0_ref[:, h:h + 1]
            m_new = jnp.maximum(lf + m_old, ig)
            w_prev.append(jnp.exp(lf + m_old - m_new))
            p_in.append(jnp.exp(ig - m_new))
            qk = jnp.sum(jnp.where(lm, qb * kb, 0.0), axis=-1, keepdims=True)
            qn = jnp.sum(jnp.where(lm, qb * n_p, 0.0), axis=-1, keepdims=True)
            mout_ref[:, h:h + 1] = m_new
            heads.append(dict(p=p, e=e, qq=qq, k_t=k_t, w_prev=w_prev[e], p_in=p_in[e],
                              pqk=p_in[e] * qk, wqn=w_prev[e] * qn, floor=jnp.exp(-m_new),
                              v=proj_ref[:, COL_MV + h * ML_DV:COL_MV + (h + 1) * ML_DV]))
        lo_lane = lane < ML_DK
        nout_ref[:, p * 128:(p + 1) * 128] = (jnp.where(lo_lane, w_prev[0], w_prev[1]) * n_p
                                              + jnp.where(lo_lane, p_in[0], p_in[1]) * kb)
    items = [dict(h=h, b=b) for h in range(HEADS) for b in range(bb)]
    qc_pair = {(p, b): jnp.dot(heads[2 * p]["qq"], c0_ref[b, p].astype(BF16), preferred_element_type=F32)
               for p in range(HEADS // 2) for b in range(bb)}
    for it in items:
        hd, b = heads[it["h"]], it["b"]
        row = hd["e"] * bb + b
        it["qc"] = qc_pair[(hd["p"], b)][row:row + 1, :]
    for it in items:
        hd, b = heads[it["h"]], it["b"]
        pqk = hd["pqk"][b:b + 1, :]
        num = hd["w_prev"][b:b + 1, :] * it["qc"] + pqk * hd["v"][b:b + 1, :]
        den = hd["wqn"][b:b + 1, :] + pqk
        rowm_ref[it["h"], b:b + 1, :] = num / jnp.maximum(jnp.abs(den), hd["floor"][b:b + 1, :])
    row128 = lax.broadcasted_iota(jnp.int32, (128, 1), 0)
    for p in range(HEADS // 2):
        h0, h1 = heads[2 * p], heads[2 * p + 1]
        pv = jnp.concatenate([h0["p_in"] * h0["v"], h1["p_in"] * h1["v"]], axis=0)
        row_2b = lax.broadcasted_iota(jnp.int32, (2 * bb, 1), 0)
        for b in range(bb):
            pv_sel = jnp.where((row_2b == b) | (row_2b == bb + b), pv, 0.0).astype(BF16)
            upd = jnp.dot(h0["k_t"], pv_sel, preferred_element_type=F32)
            w_col = jnp.where(row128 < ML_DK, h0["w_prev"][b:b + 1, :], h1["w_prev"][b:b + 1, :])
            cout_ref[b, p] = w_col * c0_ref[b, p] + upd
    for h in range(HEADS):
        mo = proj_ref[:, COL_MO + h * ML_DV:COL_MO + (h + 1) * ML_DV]
        out = _sigmoid(mo) * _rms(rowm_ref[h], gnm_ref[...])
        om_ref[:, h * ML_DV:(h + 1) * ML_DV] = out.astype(om_ref.dtype)
    mout_ref[:, HEADS:128] = m0_ref[:, HEADS:128]


def _decode(proj, gates, cs, s0, c0, n0, m0, conv_w, plane, gng, gnm, bb):
    b = proj.shape[0]
    hp = HEADS // 2
    return pl.pallas_call(
        functools.partial(_decode_kernel, bb=bb),
        grid=(b // bb,),
        in_specs=[pl.BlockSpec((bb, COL_GATE), lambda i: (i, 0)),
                  pl.BlockSpec((bb, 128), lambda i: (i, 0)),
                  pl.BlockSpec((bb, CONV_W - 1, CONV_CH), lambda i: (i, 0, 0)),
                  pl.BlockSpec((bb, HEADS, GDN_D, GDN_D), lambda i: (i, 0, 0, 0)),
                  pl.BlockSpec((bb, hp, 128, 128), lambda i: (i, 0, 0, 0)),
                  pl.BlockSpec((bb, HEADS * ML_DK), lambda i: (i, 0)),
                  pl.BlockSpec((bb, 128), lambda i: (i, 0)),
                  pl.BlockSpec((CONV_W, CONV_CH), lambda i: (0, 0)),
                  pl.BlockSpec((3, 128), lambda i: (0, 0)),
                  pl.BlockSpec((1, GDN_D), lambda i: (0, 0)),
                  pl.BlockSpec((1, ML_DV), lambda i: (0, 0))],
        out_specs=[pl.BlockSpec((bb, HEADS * GDN_D), lambda i: (i, 0)),
                   pl.BlockSpec((bb, HEADS * ML_DV), lambda i: (i, 0)),
                   pl.BlockSpec((bb, HEADS, GDN_D, GDN_D), lambda i: (i, 0, 0, 0)),
                   pl.BlockSpec((bb, hp, 128, 128), lambda i: (i, 0, 0, 0)),
                   pl.BlockSpec((bb, HEADS * ML_DK), lambda i: (i, 0)),
                   pl.BlockSpec((bb, 128), lambda i: (i, 0))],
        out_shape=[jax.ShapeDtypeStruct((b, HEADS * GDN_D), BF16),
                   jax.ShapeDtypeStruct((b, HEADS * ML_DV), BF16),
                   jax.ShapeDtypeStruct((b, HEADS, GDN_D, GDN_D), F32),
                   jax.ShapeDtypeStruct((b, hp, 128, 128), F32),
                   jax.ShapeDtypeStruct((b, HEADS * ML_DK), F32),
                   jax.ShapeDtypeStruct((b, 128), F32)],
        scratch_shapes=[pltpu.VMEM((HEADS, bb, 128), F32), pltpu.VMEM((HEADS, bb, 128), F32)],
        compiler_params=pltpu.CompilerParams(dimension_semantics=("arbitrary",),
                                             vmem_limit_bytes=VMEM_LIMIT),
        name="decode_step",
    )(proj, gates, cs, s0, c0, n0, m0, conv_w, plane, gng, gnm)


def _lane_vec(pairs):
    v = jnp.zeros((128,), F32)
    for off, val in pairs:
        v = v.at[off:off + HEADS].set(val.astype(F32))
    return v


def _prep_params(norm_pre_mix, w_in, conv_w, a_log, dt_bias, gdn_norm_g, b_igate, b_fgate, mlstm_norm_g,
                 w_out, norm_post_mix, norm_pre_mlp, w_up, w_down, norm_post_mlp):
    alog_v = _lane_vec([(LANE_GA, a_log[0])])
    dtb_v = _lane_vec([(LANE_GA, dt_bias[0])])
    bias_v = _lane_vec([(LANE_MI, b_igate[0]), (LANE_MF, b_fgate[0])])
    return dict(
        w_in=jnp.swapaxes(w_in[0], 0, 1), wo=w_out[0].astype(BF16), wu=w_up[0].astype(BF16),
        wd=w_down[0].astype(BF16),
        g_pre=norm_pre_mix[0][None, :], g1=norm_post_mix[0][None, :], g2=norm_pre_mlp[0][None, :],
        g3=norm_post_mlp[0][None, :], cw=conv_w[0], gng=gdn_norm_g[0][None, :], gnm=mlstm_norm_g[0][None, :],
        plane=jnp.stack([alog_v, dtb_v, bias_v]),
        psub=jnp.stack([alog_v[:16], dtb_v[:16], bias_v[:16]], axis=1),
    )


def _prompt_path(x, conv0, s0, c0, n0, m0, prm, w_in_r, tb, tm):
    bsz, seq, d = x.shape
    hp = HEADS // 2
    x2d = x.reshape(bsz * seq, d)
    proj, gates, gates_t = _in_proj(x2d, prm["g_pre"], w_in_r, tm=tm, out_dtype=BF16)
    proj = proj.reshape(bsz, seq, COL_GATE)
    gates = gates.reshape(bsz, seq, 128)
    gates_t = gates_t.reshape(bsz, seq // CHUNK, 16, CHUNK)
    cs8 = jnp.pad(conv0, ((0, 0), (8 - (CONV_W - 1), 0), (0, 0)))
    y, s_new, c_new, n_new, m_new = _layer_prompt(
        proj, gates, gates_t, cs8, s0, c0.reshape(bsz, hp, 128, 128), n0.reshape(bsz, hp, 1, 128),
        jnp.pad(m0, ((0, 0), (0, 128 - HEADS)))[:, None, :], x2d, prm["cw"], prm["plane"], prm["psub"],
        prm["gng"], prm["gnm"], prm["wo"], prm["g1"], prm["g2"], prm["wu"], prm["wd"], prm["g3"], tb=tb)
    y = y.reshape(bsz, seq, d)
    xp = jnp.concatenate([conv0, proj[:, seq - (CONV_W - 1):, COL_QKV:COL_QKV + CONV_CH].astype(F32)], axis=1)
    conv_new = xp[:, -(CONV_W - 1):]
    return y, (conv_new, s_new, c_new.reshape(bsz, HEADS, ML_DK, ML_DV), n_new.reshape(bsz, HEADS, ML_DK),
               m_new[:, 0, :HEADS])


def _sample_path(x, conv0, s0, c0, n0, m0, prm):
    dec, _, d = x.shape
    hp = HEADS // 2
    xs = x.reshape(dec, d)
    proj, gates, _, w_in_r = _in_proj_regroup(xs, prm["g_pre"], prm["w_in"])
    og, om, s_new, c_new, n_new, m_new = _decode(
        proj, gates, conv0, s0, c0.reshape(dec, hp, 128, 128), n0.reshape(dec, HEADS * ML_DK),
        jnp.pad(m0, ((0, 0), (0, 128 - HEADS))), prm["cw"], prm["plane"], prm["gng"], prm["gnm"], bb=16)
    y = _out_mlp(og, om, xs, prm["wo"], prm["g1"], prm["g2"], prm["wu"], prm["wd"], prm["g3"],
                 tm=dec).reshape(dec, 1, d)
    conv_new = jnp.concatenate([conv0[:, 1:, :], proj[:, None, COL_QKV:COL_QKV + CONV_CH]], axis=1)
    return y, (conv_new, s_new, c_new.reshape(dec, HEADS, ML_DK, ML_DV), n_new.reshape(dec, HEADS, ML_DK),
               m_new[:, :HEADS]), w_in_r


def kernel(x_prompt, x_sample, state_gdn_conv, state_gdn_S, state_mlstm_C, state_mlstm_n, state_mlstm_m,
           norm_pre_mix, w_in, conv_w, a_log, dt_bias, gdn_norm_g, b_igate, b_fgate, mlstm_norm_g, w_out,
           norm_post_mix, norm_pre_mlp, w_up, w_down, norm_post_mlp):
    bsz = x_prompt.shape[0]
    prm = _prep_params(norm_pre_mix, w_in, conv_w, a_log, dt_bias, gdn_norm_g, b_igate, b_fgate, mlstm_norm_g,
                       w_out, norm_post_mix, norm_pre_mlp, w_up, w_down, norm_post_mlp)
    y_s, s_st, w_in_r = _sample_path(x_sample, state_gdn_conv[0], state_gdn_S[0], state_mlstm_C[0],
                                     state_mlstm_n[0], state_mlstm_m[0], prm)
    y_p, p_st = _prompt_path(
        x_prompt, jnp.zeros((bsz, CONV_W - 1, CONV_CH), F32), jnp.zeros((bsz, HEADS, GDN_D, GDN_D), F32),
        jnp.zeros((bsz, HEADS, ML_DK, ML_DV), F32), jnp.zeros((bsz, HEADS, ML_DK), F32),
        jnp.zeros((bsz, HEADS), F32), prm, w_in_r, tb=256, tm=1024)
    return (y_p, y_s) + tuple(a[None] for a in p_st) + tuple(a[None] for a in s_st)
```

```python
import functools

import jax
import jax.numpy as jnp
from jax import lax
from jax.experimental import pallas as pl
from jax.experimental.pallas import tpu as pltpu

F32 = jnp.float32
BF16 = jnp.bfloat16
EPS = 1e-6

HEADS = 4
GDN_D = 128
ML_DK = 64
ML_DV = 128
CONV_W = 4
CONV_CH = 3 * HEADS * GDN_D
CHUNK = 64

COL_QKV = 0
COL_Z = 1536
COL_MV = 2048
COL_MO = 2560
COL_MQ = 3072
COL_MK = 3328
COL_GATE = 3584
N_PROJ = COL_GATE + 128
LANE_GB, LANE_GA, LANE_MI, LANE_MF = 0, 4, 8, 12

V7X_VMEM_BYTES = 64 * 1024 * 1024
VMEM_LIMIT = V7X_VMEM_BYTES - 8 * 1024 * 1024


def _rms(x, g):
    return x * lax.rsqrt(jnp.mean(x * x, axis=-1, keepdims=True) + EPS) * g


def _softplus(x):
    return jnp.maximum(x, 0.0) + jnp.log1p(jnp.exp(-jnp.abs(x)))


def _sigmoid(x):
    return 1.0 / (1.0 + jnp.exp(-x))


def _mm(a, b):
    return jnp.dot(a.astype(BF16), b.astype(BF16), preferred_element_type=F32)


def _mm_nt(a, b):
    return lax.dot_general(a.astype(BF16), b.astype(BF16), (((1,), (1,)), ((), ())),
                           preferred_element_type=F32)


W_IN_COLS = 3600
W_IN_SPANS = ((0, COL_QKV, 2048),
              (2568, COL_MV, 1024),
              (2056, COL_MQ, 512),
              (2048, COL_GATE, 8),
              (3592, COL_GATE + 8, 8))


def _project_rows(x_ref, g_ref, w_ref, o_ref, gc_ref, gt_ref, h_ref, n_chunk):
    h_ref[...] = _rms(x_ref[...], g_ref[...]).astype(BF16)
    n = o_ref.shape[1]
    for c0 in range(0, n, n_chunk):
        c1 = min(c0 + n_chunk, n)
        o_ref[:, c0:c1] = _mm_nt(h_ref[...], w_ref[c0:c1, :]).astype(o_ref.dtype)
    gates = _mm_nt(h_ref[...], w_ref[COL_GATE:COL_GATE + 128, :])
    gc_ref[...] = gates
    gt = gates.T
    for c in range(gt_ref.shape[0]):
        gt_ref[c] = gt[0:16, c * CHUNK:(c + 1) * CHUNK]


def _in_proj_regroup_kernel(x_ref, g_ref, win_ref, o_ref, gc_ref, gt_ref, w_ref, h_ref, *, n_chunk):
    w_ref[COL_GATE:N_PROJ, :] = jnp.zeros((N_PROJ - COL_GATE, w_ref.shape[1]), BF16)
    for src, dst, width in W_IN_SPANS:
        for r0 in range(0, width, 512):
            r1 = min(r0 + 512, width)
            w_ref[dst + r0:dst + r1, :] = win_ref[src + r0:src + r1, :].astype(BF16)
    _project_rows(x_ref, g_ref, w_ref, o_ref, gc_ref, gt_ref, h_ref, n_chunk)


def _in_proj_kernel(x_ref, g_ref, w_ref, o_ref, gc_ref, gt_ref, h_ref, *, n_chunk):
    _project_rows(x_ref, g_ref, w_ref, o_ref, gc_ref, gt_ref, h_ref, n_chunk)


def _in_proj_regroup(x2d, g, w_t):
    m, k = x2d.shape
    full = lambda i: (0, 0)
    return pl.pallas_call(
        functools.partial(_in_proj_regroup_kernel, n_chunk=512),
        grid=(1,),
        in_specs=[pl.BlockSpec((m, k), full), pl.BlockSpec((1, k), full), pl.BlockSpec((W_IN_COLS, k), full)],
        out_specs=[pl.BlockSpec((m, COL_GATE), full), pl.BlockSpec((m, 128), full),
                   pl.BlockSpec((m // CHUNK, 16, CHUNK), lambda i: (0, 0, 0)), pl.BlockSpec((N_PROJ, k), full)],
        out_shape=[jax.ShapeDtypeStruct((m, COL_GATE), F32),
                   jax.ShapeDtypeStruct((m, 128), F32),
                   jax.ShapeDtypeStruct((m // CHUNK, 16, CHUNK), F32),
                   jax.ShapeDtypeStruct((N_PROJ, k), BF16)],
        scratch_shapes=[pltpu.VMEM((m, k), BF16)],
        compiler_params=pltpu.CompilerParams(dimension_semantics=("arbitrary",),
                                             vmem_limit_bytes=VMEM_LIMIT),
        name="in_proj_regroup",
    )(x2d, g, w_t)


def _in_proj(x2d, g, w, tm, out_dtype):
    m, k = x2d.shape
    return pl.pallas_call(
        functools.partial(_in_proj_kernel, n_chunk=512),
        grid=(m // tm,),
        in_specs=[pl.BlockSpec((tm, k), lambda i: (i, 0)),
                  pl.BlockSpec((1, k), lambda i: (0, 0)),
                  pl.BlockSpec((N_PROJ, k), lambda i: (0, 0), pipeline_mode=pl.Buffered(1))],
        out_specs=[pl.BlockSpec((tm, COL_GATE), lambda i: (i, 0)),
                   pl.BlockSpec((tm, 128), lambda i: (i, 0)),
                   pl.BlockSpec((tm // CHUNK, 16, CHUNK), lambda i: (i, 0, 0))],
        out_shape=[jax.ShapeDtypeStruct((m, COL_GATE), out_dtype),
                   jax.ShapeDtypeStruct((m, 128), F32),
                   jax.ShapeDtypeStruct((m // CHUNK, 16, CHUNK), F32)],
        scratch_shapes=[pltpu.VMEM((tm, k), BF16)],
        compiler_params=pltpu.CompilerParams(dimension_semantics=("arbitrary",),
                                             vmem_limit_bytes=VMEM_LIMIT),
        name="in_proj",
    )(x2d, g, w)


def _out_mlp_kernel(og_ref, om_ref, x_ref, wo_ref, g1_ref, g2_ref, wu_ref, wd_ref, g3_ref,
                    y_ref, wob_ref, wub_ref, wdb_ref, hn_ref, acc_ref):
    j = pl.program_id(0)
    half = og_ref.shape[1]

    @pl.when(j == 0)
    def _():
        wo = wo_ref[...].astype(BF16)
        wob_ref[...] = wo
        mix = (jnp.dot(og_ref[...], wo[0:half, :], preferred_element_type=F32)
               + jnp.dot(om_ref[...], wo[half:2 * half, :], preferred_element_type=F32))
        x1 = x_ref[...] + _rms(mix, g1_ref[...])
        y_ref[...] = x1
        hn_ref[...] = _rms(x1, g2_ref[...]).astype(BF16)
        acc_ref[...] = jnp.zeros(acc_ref.shape, F32)

    wu = wu_ref[...].astype(BF16)
    wub_ref[...] = wu
    wd = wd_ref[...].astype(BF16)
    wdb_ref[...] = wd
    u = jnp.dot(hn_ref[...], wu, preferred_element_type=F32)
    u = jnp.square(jnp.maximum(u, 0.0)).astype(BF16)
    acc_ref[...] += jnp.dot(u, wd, preferred_element_type=F32)

    @pl.when(j == pl.num_programs(0) - 1)
    def _():
        y_ref[...] = y_ref[...] + _rms(acc_ref[...], g3_ref[...])


def _out_mlp(og, om, x2d, wo, g1, g2, wu, wd, g3, ff_chunk):
    m, d = x2d.shape
    half = og.shape[1]
    d_ff = wu.shape[1]
    const = lambda j: (0, 0)
    return pl.pallas_call(
        _out_mlp_kernel,
        grid=(d_ff // ff_chunk,),
        in_specs=[pl.BlockSpec((m, half), const),
                  pl.BlockSpec((m, half), const),
                  pl.BlockSpec((m, d), const),
                  pl.BlockSpec((d, d), const, pipeline_mode=pl.Buffered(1)),
                  pl.BlockSpec((1, d), const),
                  pl.BlockSpec((1, d), const),
                  pl.BlockSpec((d, ff_chunk), lambda j: (0, j)),
                  pl.BlockSpec((ff_chunk, d), lambda j: (j, 0)),
                  pl.BlockSpec((1, d), const)],
        out_specs=[pl.BlockSpec((m, d), const),
                   pl.BlockSpec((d, d), const),
                   pl.BlockSpec((d, ff_chunk), lambda j: (0, j)),
                   pl.BlockSpec((ff_chunk, d), lambda j: (j, 0))],
        out_shape=[jax.ShapeDtypeStruct((m, d), F32),
                   jax.ShapeDtypeStruct((d, d), BF16),
                   jax.ShapeDtypeStruct((d, d_ff), BF16),
                   jax.ShapeDtypeStruct((d_ff, d), BF16)],
        scratch_shapes=[pltpu.VMEM((m, d), BF16), pltpu.VMEM((m, d), F32)],
        compiler_params=pltpu.CompilerParams(dimension_semantics=("arbitrary",),
                                             vmem_limit_bytes=VMEM_LIMIT),
        name="out_mlp",
    )(og, om, x2d, wo, g1, g2, wu, wd, g3)


def _chunk_masks():
    ii = lax.broadcasted_iota(jnp.int32, (CHUNK, CHUNK), 0)
    jj = lax.broadcasted_iota(jnp.int32, (CHUNK, CHUNK), 1)
    return ii, jj


def _cumsum_col_row(x_col, x_row, ii, jj):
    c_col = jnp.sum(jnp.where(jj <= ii, x_row, 0.0), axis=1, keepdims=True)
    c_row = jnp.sum(jnp.where(ii <= jj, x_col, 0.0), axis=0, keepdims=True)
    return c_col, c_row


def _interleave(*tasks):
    live = [[g, n, 0] for g, n in tasks]
    while live:
        entry = min(live, key=lambda e: (e[2] + 1) / e[1])
        try:
            next(entry[0])
            entry[2] += 1
        except StopIteration:
            live.remove(entry)


def _halves(items):
    mid = len(items) // 2
    return items[:mid], items[mid:]


def _gdn_phase_a(yc_ref, gcol_ref, grow_ref, plane_ref, psub_ref, wv_ref, lhs_ref, qk_ref, ket_ref, ge_ref,
                 ncb):
    ii, jj = _chunk_masks()
    incl = jj <= ii
    strict = jj < ii
    eye = (ii == jj).astype(F32)
    neg_a_lane = -jnp.exp(plane_ref[0:1, :])
    dtb_lane = plane_ref[1:2, :]
    neg_a_sub = -jnp.exp(psub_ref[:, 0:1])
    dtb_sub = psub_ref[:, 1:2]
    items = []
    for c in range(ncb):
        rows = slice(c * CHUNK, (c + 1) * CHUNK)
        gc = gcol_ref[rows, :]
        gr = grow_ref[c]
        beta_t = _sigmoid(gc)
        g_t = neg_a_lane * _softplus(gc + dtb_lane)
        g_r = neg_a_sub * _softplus(gr + dtb_sub)
        for h in range(HEADS):
            lo = h * GDN_D
            items.append(dict(c=c, h=h,
                              q=yc_ref[rows, lo:lo + GDN_D],
                              k=yc_ref[rows, 512 + lo:512 + lo + GDN_D],
                              v=yc_ref[rows, 1024 + lo:1024 + lo + GDN_D],
                              beta=beta_t[:, LANE_GB + h:LANE_GB + h + 1],
                              gg_col=g_t[:, LANE_GA + h:LANE_GA + h + 1],
                              gg_row=g_r[LANE_GA + h:LANE_GA + h + 1, :]))
    yield
    for it in items:
        it["k"] = it["k"] * lax.rsqrt(jnp.sum(it["k"] * it["k"], axis=-1, keepdims=True) + EPS)
    yield
    for it in items:
        it["kk"] = _mm_nt(it["k"], it["k"])
    yield
    for it in items:
        it["q"] = (it["q"] * lax.rsqrt(jnp.sum(it["q"] * it["q"], axis=-1, keepdims=True) + EPS)
                   * (GDN_D ** -0.5))
    yield
    for it in items:
        it["g_col"], g_row = _cumsum_col_row(it["gg_col"], it["gg_row"], ii, jj)
        it["decay"] = jnp.where(incl, jnp.exp(jnp.where(incl, it["g_col"] - g_row, 0.0)), 0.0)
    yield
    for it in items:
        n_mat = jnp.where(strict, it["beta"] * it["kk"] * it["decay"], 0.0)
        it["x"] = eye - n_mat
        it["p"] = -n_mat
    yield
    for _ in range(5):
        yield
        for it in items:
            it["p"] = _mm(it["p"], it["p"])
        yield
        for it in items:
            it["x"] = it["x"] + _mm(it["x"], it["p"])
    yield
    for it in items:
        e_g = jnp.exp(it["g_col"])
        it["e_g"] = e_g
        rhs = jnp.concatenate([it["beta"] * it["v"], (it["beta"] * e_g) * it["k"]], axis=1)
        it["w"] = _mm(it["x"], rhs)
    yield
    for it in items:
        it["qk"] = _mm_nt(it["q"], it["k"]) * it["decay"]
    yield
    for it in items:
        c, h = it["c"], it["h"]
        g_end = it["g_col"][CHUNK - 1:CHUNK, :]
        k_end = it["k"] * jnp.exp(g_end - it["g_col"])
        wv_ref[c, h] = it["w"][:, 0:GDN_D]
        lhs_ref[c, h] = jnp.concatenate([it["w"][:, GDN_D:2 * GDN_D], it["e_g"] * it["q"]],
                                        axis=0).astype(BF16)
        qk_ref[c, h] = it["qk"].astype(BF16)
        ket_ref[c, h] = k_end.T.astype(BF16)
        ge_ref[c, h] = jnp.broadcast_to(jnp.exp(g_end), (1, GDN_D))


def _gdn_phase_b(qkvz_ref, gn_ref, og_ref, s_ref, wv_ref, lhs_ref, qk_ref, ket_ref, ge_ref, ncb):
    gn = gn_ref[...]

    def epilogue(c, o):
        rows = slice(c * CHUNK, (c + 1) * CHUNK)
        for h in range(HEADS):
            lo = h * GDN_D
            z = qkvz_ref[rows, COL_Z + lo:COL_Z + lo + GDN_D].astype(F32)
            out = _rms(o[h], gn) * (z * _sigmoid(z))
            og_ref[rows, lo:lo + GDN_D] = out.astype(og_ref.dtype)

    s = [s_ref[h] for h in range(HEADS)]
    o_prev = None
    for c in range(ncb):
        r = [jnp.dot(lhs_ref[c, h], s[h].astype(BF16), preferred_element_type=F32) for h in range(HEADS)]
        yield
        if o_prev is not None:
            epilogue(c - 1, o_prev)
        ub = [(wv_ref[c, h] - r[h][0:CHUNK]).astype(BF16) for h in range(HEADS)]
        s = [ge_ref[c, h] * s[h] + jnp.dot(ket_ref[c, h], ub[h], preferred_element_type=F32)
             for h in range(HEADS)]
        o_prev = [r[h][CHUNK:2 * CHUNK] + jnp.dot(qk_ref[c, h], ub[h], preferred_element_type=F32)
                  for h in range(HEADS)]
        yield
    epilogue(ncb - 1, o_prev)
    for h in range(HEADS):
        s_ref[h] = s[h]


def _mlstm_block(mvo_ref, mqk_ref, gcol_ref, grow_ref, plane_ref, psub_ref, gn_ref, om_ref,
                 c_ref, n_ref, m_ref, ncb):
    ii, jj = _chunk_masks()
    incl = jj <= ii
    lane = lax.broadcasted_iota(jnp.int32, (1, 128), 1)
    row128 = lax.broadcasted_iota(jnp.int32, (128, 1), 0)
    gn = gn_ref[...]
    blane = plane_ref[2:3, :]
    bsub = psub_ref[:, 2:3]
    items = []
    for c in range(ncb):
        rows = slice(c * CHUNK, (c + 1) * CHUNK)
        gc = gcol_ref[rows, :] + blane
        gr = grow_ref[c] + bsub
        lf_c = -_softplus(-gc)
        lf_r = -_softplus(-gr)
        for p in range(HEADS // 2):
            qb = mqk_ref[rows, p * 128:(p + 1) * 128].astype(F32)
            kb = mqk_ref[rows, 256 + p * 128:256 + (p + 1) * 128].astype(F32) * (ML_DK ** -0.5)
            for e in range(2):
                h = 2 * p + e
                lm = (lane >= e * ML_DK) & (lane < (e + 1) * ML_DK)
                items.append(dict(c=c, rows=rows, p=p, e=e, h=h, qh=jnp.where(lm, qb, 0.0), kb=kb,
                                  kh=jnp.where(lm, kb, 0.0),
                                  lf_col=lf_c[:, LANE_MF + h:LANE_MF + h + 1],
                                  lf_row=lf_r[LANE_MF + h:LANE_MF + h + 1, :],
                                  ig_row=gr[LANE_MI + h:LANE_MI + h + 1, :],
                                  ig_col=gc[:, LANE_MI + h:LANE_MI + h + 1],
                                  v=mvo_ref[rows, h * ML_DV:(h + 1) * ML_DV]))
        if c % 2 == 1:
            yield
    for part in _halves(items):
        for it in part:
            it["qk"] = _mm_nt(it["qh"], it["kb"])
        yield
    for part in _halves(items):
        for it in part:
            it["f_col"], it["f_row"] = _cumsum_col_row(it["lf_col"], it["lf_row"], ii, jj)
        yield
    for part in _halves(items):
        for it in part:
            it["d_mat"] = jnp.where(incl, it["f_col"] - it["f_row"] + it["ig_row"], -jnp.inf)
            it["d_max"] = jnp.max(it["d_mat"], axis=1, keepdims=True)
        yield
    for part in _halves(items):
        for it in part:
            p0 = jnp.where(incl, jnp.exp(jnp.where(incl, it["d_mat"] - it["d_max"], 0.0)), 0.0)
            pend0 = jnp.exp(it["f_col"][CHUNK - 1:CHUNK, :] - it["f_col"] + it["ig_col"]
                            - it["d_max"][CHUNK - 1:CHUNK, :])
            it["kp0"] = it["kh"] * pend0
            it["pqk0"] = p0 * it["qk"]
        yield
    for part in _halves(items):
        for it in part:
            it["pv0"] = _mm(it["pqk0"], it["v"])
            it["rs0"] = jnp.sum(it["pqk0"], axis=-1, keepdims=True)
        yield
    for part in _halves(items):
        for it in part:
            it["cadd0"] = _mm(it["kp0"].T, it["v"])
            it["nadd0"] = jnp.sum(it["kp0"], axis=0, keepdims=True)
        yield

    m_cur = [m_ref[:, h:h + 1] for h in range(HEADS)]
    for it in items:
        h = it["h"]
        it["m_prev"] = m_cur[h]
        m_cur[h] = jnp.maximum(it["f_col"][CHUNK - 1:CHUNK, :] + m_cur[h], it["d_max"][CHUNK - 1:CHUNK, :])
    yield
    for part in _halves(items):
        for it in part:
            bcol = it["f_col"] + it["m_prev"]
            mt = jnp.maximum(bcol, it["d_max"])
            it["mt"] = mt
            it["w_prev"] = jnp.exp(bcol - mt)
            it["sc"] = jnp.exp(it["d_max"] - mt)
        yield
    c_cur = [c_ref[p] for p in range(HEADS // 2)]
    n_cur = [n_ref[p] for p in range(HEADS // 2)]
    for c in range(ncb):
        for p in range(HEADS // 2):
            pair = [it for it in items if it["c"] == c and it["p"] == p]
            w_end = [it["w_prev"][CHUNK - 1:CHUNK, :] for it in pair]
            s_end = [it["sc"][CHUNK - 1:CHUNK, :] for it in pair]
            for it in pair:
                it["c_prev"] = c_cur[p]
                it["n_prev"] = n_cur[p]
            c_cur[p] = (jnp.where(row128 < ML_DK, w_end[0], w_end[1]) * c_cur[p]
                        + s_end[0] * pair[0]["cadd0"] + s_end[1] * pair[1]["cadd0"])
            n_cur[p] = (jnp.where(lane < ML_DK, w_end[0], w_end[1]) * n_cur[p]
                        + s_end[0] * pair[0]["nadd0"] + s_end[1] * pair[1]["nadd0"])
        if c % 2 == 1:
            yield
    for part in _halves(items):
        for it in part:
            it["qc"] = _mm(it["qh"], it["c_prev"])
        yield
    for part in _halves(items):
        for it in part:
            num = it["w_prev"] * it["qc"] + it["sc"] * it["pv0"]
            den = (it["w_prev"] * jnp.sum(it["qh"] * it["n_prev"], axis=-1, keepdims=True)
                   + it["sc"] * it["rs0"])
            it["hh"] = num / jnp.maximum(jnp.abs(den), jnp.exp(-it["mt"]))
        yield
    for part in _halves(items):
        for it in part:
            h = it["h"]
            mo = mvo_ref[it["rows"], 512 + h * ML_DV:512 + (h + 1) * ML_DV].astype(F32)
            out = _sigmoid(mo) * _rms(it["hh"], gn)
            om_ref[it["rows"], h * ML_DV:(h + 1) * ML_DV] = out.astype(om_ref.dtype)
        yield
    for p in range(HEADS // 2):
        c_ref[p] = c_cur[p]
        n_ref[p] = n_cur[p]
    for h in range(HEADS):
        m_ref[:, h:h + 1] = m_cur[h]


def _chain(*gens):
    for g in gens:
        yield from g


def _conv_stage(xp_ref, yc_ref, cw_ref, tb):
    for ct in range(CONV_CH // 128):
        cols = slice(ct * 128, (ct + 1) * 128)
        w = cw_ref[:, cols]
        for r0 in range(0, tb, 128):
            acc = xp_ref[8 + r0:8 + r0 + 128, cols] * w[CONV_W - 1:CONV_W, :]
            for j in range(CONV_W - 1):
                acc = acc + xp_ref[5 + j + r0:5 + j + r0 + 128, cols] * w[j:j + 1, :]
            yc_ref[r0:r0 + 128, cols] = acc * _sigmoid(acc)
        yield


def _out_mlp_block(og_ref, om_ref, x_ref, wo_ref, g1_ref, g2_ref, wu_ref, wd_ref, g3_ref,
                   y_ref, hn_ref, acc_ref, ff_chunk, n_split):
    half = og_ref.shape[1]
    d = x_ref.shape[1]
    d_ff = wu_ref.shape[1]
    col_groups = [slice(c, c + d // n_split) for c in range(0, d, d // n_split)]
    for cg in col_groups:
        acc_ref[:, cg] = (jnp.dot(og_ref[...], wo_ref[0:half, cg], preferred_element_type=F32)
                          + jnp.dot(om_ref[...], wo_ref[half:2 * half, cg], preferred_element_type=F32))
        yield
    x1 = x_ref[...] + _rms(acc_ref[...], g1_ref[...])
    y_ref[...] = x1
    hn_ref[...] = _rms(x1, g2_ref[...]).astype(BF16)
    yield
    for c0 in range(0, d_ff, ff_chunk):
        u = jnp.dot(hn_ref[...], wu_ref[:, c0:c0 + ff_chunk], preferred_element_type=F32)
        u = jnp.square(jnp.maximum(u, 0.0)).astype(BF16)
        for cg in col_groups:
            dd = jnp.dot(u, wd_ref[c0:c0 + ff_chunk, cg], preferred_element_type=F32)
            if c0 == 0:
                acc_ref[:, cg] = dd
            else:
                acc_ref[:, cg] += dd
        yield
    y_ref[...] = y_ref[...] + _rms(acc_ref[...], g3_ref[...])
    yield


def _layer_kernel(qkvz_ref, mvo_ref, mqk_ref, gcol_ref, grow_ref, cs_ref, s0_ref, c0_ref, n0_ref, m0_ref,
                  x_ref, cw_ref, plane_ref, psub_ref, gng_ref, gnm_ref,
                  wo_ref, g1_ref, g2_ref, wu_ref, wd_ref, g3_ref,
                  y_ref, sout_ref, cout_ref, nout_ref, mout_ref,
                  xp_ref, yc_ref, s_ref, wv_ref, lhs_ref, qk_ref, ket_ref, ge_ref, c_ref, n_ref, m_ref,
                  og_ref, om_ref, hn_ref, acc_ref, *, tb, nt, n_blocks):
    g = pl.program_id(0)
    t = jnp.minimum(g, n_blocks - 1) % nt
    ncb = tb // CHUNK
    par = g % 2

    @pl.when(g == 0)
    def _():
        og_ref[1] = jnp.zeros(og_ref.shape[1:], og_ref.dtype)
        om_ref[1] = jnp.zeros(om_ref.shape[1:], om_ref.dtype)

    @pl.when(t == 0)
    def _():
        xp_ref[0:8, :] = cs_ref[...]
        s_ref[...] = s0_ref[...]
        c_ref[...] = c0_ref[...]
        n_ref[...] = n0_ref[...]
        m_ref[...] = m0_ref[...]

    @pl.when(t > 0)
    def _():
        xp_ref[0:8, :] = xp_ref[tb:tb + 8, :]

    xp_ref[8:tb + 8, :] = qkvz_ref[:, COL_QKV:COL_QKV + CONV_CH].astype(F32)

    og_w, om_w = og_ref.at[par], om_ref.at[par]
    og_r, om_r = og_ref.at[1 - par], om_ref.at[1 - par]
    _interleave(
        (_chain(_conv_stage(xp_ref, yc_ref, cw_ref, tb),
                _gdn_phase_a(yc_ref, gcol_ref, grow_ref, plane_ref, psub_ref, wv_ref, lhs_ref, qk_ref, ket_ref,
                             ge_ref, ncb),
                _gdn_phase_b(qkvz_ref, gng_ref, og_w, s_ref, wv_ref, lhs_ref, qk_ref, ket_ref, ge_ref, ncb)),
         CONV_CH // 128 + 20 + 2 * ncb),
        (_out_mlp_block(og_r, om_r, x_ref, wo_ref, g1_ref, g2_ref, wu_ref, wd_ref, g3_ref, y_ref, hn_ref, acc_ref,
                        ff_chunk=256, n_split=2), 2 + 1 + wu_ref.shape[1] // 256 + 1),
        (_mlstm_block(mvo_ref, mqk_ref, gcol_ref, grow_ref, plane_ref, psub_ref, gnm_ref, om_w,
                      c_ref, n_ref, m_ref, ncb), 21 + ncb))

    @pl.when((t == nt - 1) & (g < n_blocks))
    def _():
        sout_ref[...] = s_ref[...]
        cout_ref[...] = c_ref[...]
        nout_ref[...] = n_ref[...]
        mout_ref[...] = m_ref[...]


def _layer_prompt(proj, gates, gates_t, cs8, s0, c0, n0, m0, x2d, conv_w, plane, psub, gng, gnm,
                  wo, g1, g2, wu, wd, g3, tb):
    b, t, _ = proj.shape
    d = x2d.shape[1]
    d_ff = wu.shape[1]
    nt = t // tb
    n_blocks = b * nt
    ncb = tb // CHUNK
    hp = HEADS // 2

    def blk(g):
        gm = jnp.minimum(g, n_blocks - 1)
        return gm // nt, gm % nt

    def tok3(col):
        return lambda g: blk(g) + (col,)

    per_seq3 = lambda g: (blk(g)[0], 0, 0)
    per_seq4 = lambda g: (blk(g)[0], 0, 0, 0)
    prev_rows = lambda g: (jnp.maximum(g - 1, 0), 0)
    const = lambda g: (0, 0)
    resident = dict(pipeline_mode=pl.Buffered(1))
    return pl.pallas_call(
        functools.partial(_layer_kernel, tb=tb, nt=nt, n_blocks=n_blocks),
        grid=(n_blocks + 1,),
        in_specs=[pl.BlockSpec((None, tb, 2048), tok3(0)),
                  pl.BlockSpec((None, tb, 1024), tok3(COL_MV // 1024)),
                  pl.BlockSpec((None, tb, 512), tok3(COL_MQ // 512)),
                  pl.BlockSpec((None, tb, 128), tok3(0)),
                  pl.BlockSpec((None, ncb, 16, CHUNK), lambda g: blk(g) + (0, 0)),
                  pl.BlockSpec((None, 8, CONV_CH), per_seq3),
                  pl.BlockSpec((None, HEADS, GDN_D, GDN_D), per_seq4),
                  pl.BlockSpec((None, hp, 128, 128), per_seq4),
                  pl.BlockSpec((None, hp, 1, 128), per_seq4),
                  pl.BlockSpec((None, 1, 128), per_seq3),
                  pl.BlockSpec((tb, d), prev_rows),
                  pl.BlockSpec((CONV_W, CONV_CH), const),
                  pl.BlockSpec((3, 128), const),
                  pl.BlockSpec((16, 3), const),
                  pl.BlockSpec((1, GDN_D), const),
                  pl.BlockSpec((1, ML_DV), const),
                  pl.BlockSpec((d, d), const, **resident),
                  pl.BlockSpec((1, d), const),
                  pl.BlockSpec((1, d), const),
                  pl.BlockSpec((d, d_ff), const, **resident),
                  pl.BlockSpec((d_ff, d), const, **resident),
                  pl.BlockSpec((1, d), const)],
        out_specs=[pl.BlockSpec((tb, d), prev_rows),
                   pl.BlockSpec((None, HEADS, GDN_D, GDN_D), per_seq4),
                   pl.BlockSpec((None, hp, 128, 128), per_seq4),
                   pl.BlockSpec((None, hp, 1, 128), per_seq4),
                   pl.BlockSpec((None, 1, 128), per_seq3)],
        out_shape=[jax.ShapeDtypeStruct((b * t, d), F32),
                   jax.ShapeDtypeStruct((b, HEADS, GDN_D, GDN_D), F32),
                   jax.ShapeDtypeStruct((b, hp, 128, 128), F32),
                   jax.ShapeDtypeStruct((b, hp, 1, 128), F32),
                   jax.ShapeDtypeStruct((b, 1, 128), F32)],
        scratch_shapes=[pltpu.VMEM((tb + 8, CONV_CH), F32),
                        pltpu.VMEM((tb, CONV_CH), F32),
                        pltpu.VMEM((HEADS, GDN_D, GDN_D), F32),
                        pltpu.VMEM((ncb, HEADS, CHUNK, GDN_D), F32),
                        pltpu.VMEM((ncb, HEADS, 2 * CHUNK, GDN_D), BF16),
                        pltpu.VMEM((ncb, HEADS, CHUNK, CHUNK), BF16),
                        pltpu.VMEM((ncb, HEADS, GDN_D, CHUNK), BF16),
                        pltpu.VMEM((ncb, HEADS, 1, GDN_D), F32),
                        pltpu.VMEM((hp, 128, 128), F32),
                        pltpu.VMEM((hp, 1, 128), F32),
                        pltpu.VMEM((1, 128), F32),
                        pltpu.VMEM((2, tb, HEADS * GDN_D), BF16),
                        pltpu.VMEM((2, tb, HEADS * ML_DV), BF16),
                        pltpu.VMEM((tb, d), BF16),
                        pltpu.VMEM((tb, d), F32)],
        compiler_params=pltpu.CompilerParams(dimension_semantics=("arbitrary",),
                                             vmem_limit_bytes=VMEM_LIMIT),
        name="layer_prompt",
    )(proj, proj, proj, gates, gates_t, cs8, s0, c0, n0, m0, x2d, conv_w, plane, psub, gng, gnm,
      wo, g1, g2, wu, wd, g3)


def _decode_kernel(proj_ref, gates_ref, cs_ref, s0_ref, c0_ref, n0_ref, m0_ref, cw_ref, plane_ref, gng_ref, gnm_ref,
                   og_ref, om_ref, sout_ref, cout_ref, nout_ref, mout_ref, rowg_ref, rowm_ref, *, bb):
    lane = lax.broadcasted_iota(jnp.int32, (1, 128), 1)
    gt = gates_ref[...]
    beta_t = _sigmoid(gt)
    g_t = -jnp.exp(plane_ref[0:1, :]) * _softplus(gt + plane_ref[1:2, :])
    eg_t = jnp.exp(g_t)
    gb_t = gt + plane_ref[2:3, :]
    lf_t = -_softplus(-gb_t)

    heads = []
    for h in range(HEADS):
        cols = []
        for part in range(3):
            c0 = part * 512 + h * GDN_D
            w = cw_ref[:, c0:c0 + GDN_D]
            acc = cs_ref[:, 0, c0:c0 + GDN_D] * w[0:1, :]
            acc = acc + cs_ref[:, 1, c0:c0 + GDN_D] * w[1:2, :]
            acc = acc + cs_ref[:, 2, c0:c0 + GDN_D] * w[2:3, :]
            acc = acc + proj_ref[:, COL_QKV + c0:COL_QKV + c0 + GDN_D] * w[3:4, :]
            cols.append(acc * _sigmoid(acc))
        q, k, v = cols
        q = q * lax.rsqrt(jnp.sum(q * q, axis=-1, keepdims=True) + EPS) * (GDN_D ** -0.5)
        k = k * lax.rsqrt(jnp.sum(k * k, axis=-1, keepdims=True) + EPS)
        heads.append(dict(v=v, qk=jnp.sum(q * k, axis=-1, keepdims=True), k_t=k.T.astype(BF16),
                          kq=jnp.concatenate([k, q], axis=0).astype(BF16),
                          beta=beta_t[:, LANE_GB + h:LANE_GB + h + 1],
                          eg=eg_t[:, LANE_GA + h:LANE_GA + h + 1]))
    items = [dict(h=h, b=b) for h in range(HEADS) for b in range(bb)]
    row_b = lax.broadcasted_iota(jnp.int32, (bb, 1), 0)
    for it in items:
        r = jnp.dot(heads[it["h"]]["kq"], s0_ref[it["b"], it["h"]].astype(BF16), preferred_element_type=F32)
        it["ks"] = r[it["b"]:it["b"] + 1, :]
        it["qs"] = r[bb + it["b"]:bb + it["b"] + 1, :]
    for it in items:
        hd, b = heads[it["h"]], it["b"]
        eg_b = hd["eg"][b:b + 1, :]
        it["eg_b"] = eg_b
        it["u"] = hd["beta"][b:b + 1, :] * (hd["v"][b:b + 1, :] - eg_b * it["ks"])
        rowg_ref[it["h"], b:b + 1, :] = eg_b * it["qs"] + hd["qk"][b:b + 1, :] * it["u"]
    for it in items:
        u_sel = jnp.where(row_b == it["b"], it["u"], 0.0).astype(BF16)
        upd = jnp.dot(heads[it["h"]]["k_t"], u_sel, preferred_element_type=F32)
        sout_ref[it["b"], it["h"]] = it["eg_b"] * s0_ref[it["b"], it["h"]] + upd
    for h in range(HEADS):
        z = proj_ref[:, COL_Z + h * GDN_D:COL_Z + (h + 1) * GDN_D]
        out = _rms(rowg_ref[h], gng_ref[...]) * (z * _sigmoid(z))
        og_ref[:, h * GDN_D:(h + 1) * GDN_D] = out.astype(og_ref.dtype)

    heads = []
    for p in range(HEADS // 2):
        qb = proj_ref[:, COL_MQ + p * 128:COL_MQ + (p + 1) * 128]
        kb = proj_ref[:, COL_MK + p * 128:COL_MK + (p + 1) * 128] * (ML_DK ** -0.5)
        n_p = n0_ref[:, p * 128:(p + 1) * 128]
        k_t = jnp.concatenate([jnp.where(lane < ML_DK, kb, 0.0), jnp.where(lane >= ML_DK, kb, 0.0)],
                              axis=0).T.astype(BF16)
        qq = jnp.concatenate([jnp.where(lane < ML_DK, qb, 0.0), jnp.where(lane >= ML_DK, qb, 0.0)],
                             axis=0).astype(BF16)
        w_prev, p_in = [], []
        for e in range(2):
            h = 2 * p + e
            lm = (lane >= e * ML_DK) & (lane < (e + 1) * ML_DK)
            ig = gb_t[:, LANE_MI + h:LANE_MI + h + 1]
            lf = lf_t[:, LANE_MF + h:LANE_MF + h + 1]
            m_old = m0_ref[:, h:h + 1]
            m_new = jnp.maximum(lf + m_old, ig)
            w_prev.append(jnp.exp(lf + m_old - m_new))
            p_in.append(jnp.exp(ig - m_new))
            qk = jnp.sum(jnp.where(lm, qb * kb, 0.0), axis=-1, keepdims=True)
            qn = jnp.sum(jnp.where(lm, qb * n_p, 0.0), axis=-1, keepdims=True)
            mout_ref[:, h:h + 1] = m_new
            heads.append(dict(p=p, e=e, qq=qq, k_t=k_t, w_prev=w_prev[e], p_in=p_in[e],
                              pqk=p_in[e] * qk, wqn=w_prev[e] * qn, floor=jnp.exp(-m_new),
                              v=proj_ref[:, COL_MV + h * ML_DV:COL_MV + (h + 1) * ML_DV]))
        lo_lane = lane < ML_DK
        nout_ref[:, p * 128:(p + 1) * 128] = (jnp.where(lo_lane, w_prev[0], w_prev[1]) * n_p
                                              + jnp.where(lo_lane, p_in[0], p_in[1]) * kb)
    items = [dict(h=h, b=b) for h in range(HEADS) for b in range(bb)]
    qc_pair = {(p, b): jnp.dot(heads[2 * p]["qq"], c0_ref[b, p].astype(BF16), preferred_element_type=F32)
               for p in range(HEADS // 2) for b in range(bb)}
    for it in items:
        hd, b = heads[it["h"]], it["b"]
        row = hd["e"] * bb + b
        it["qc"] = qc_pair[(hd["p"], b)][row:row + 1, :]
    for it in items:
        hd, b = heads[it["h"]], it["b"]
        pqk = hd["pqk"][b:b + 1, :]
        num = hd["w_prev"][b:b + 1, :] * it["qc"] + pqk * hd["v"][b:b + 1, :]
        den = hd["wqn"][b:b + 1, :] + pqk
        rowm_ref[it["h"], b:b + 1, :] = num / jnp.maximum(jnp.abs(den), hd["floor"][b:b + 1, :])
    row128 = lax.broadcasted_iota(jnp.int32, (128, 1), 0)
    for p in range(HEADS // 2):
        h0, h1 = heads[2 * p], heads[2 * p + 1]
        pv = jnp.concatenate([h0["p_in"] * h0["v"], h1["p_in"] * h1["v"]], axis=0)
        row_2b = lax.broadcasted_iota(jnp.int32, (2 * bb, 1), 0)
        for b in range(bb):
            pv_sel = jnp.where((row_2b == b) | (row_2b == bb + b), pv, 0.0).astype(BF16)
            upd = jnp.dot(h0["k_t"], pv_sel, preferred_element_type=F32)
            w_col = jnp.where(row128 < ML_DK, h0["w_prev"][b:b + 1, :], h1["w_prev"][b:b + 1, :])
            cout_ref[b, p] = w_col * c0_ref[b, p] + upd
    for h in range(HEADS):
        mo = proj_ref[:, COL_MO + h * ML_DV:COL_MO + (h + 1) * ML_DV]
        out = _sigmoid(mo) * _rms(rowm_ref[h], gnm_ref[...])
        om_ref[:, h * ML_DV:(h + 1) * ML_DV] = out.astype(om_ref.dtype)
    mout_ref[:, HEADS:128] = m0_ref[:, HEADS:128]


def _decode(proj, gates, cs, s0, c0, n0, m0, conv_w, plane, gng, gnm, bb):
    b = proj.shape[0]
    hp = HEADS // 2
    return pl.pallas_call(
        functools.partial(_decode_kernel, bb=bb),
        grid=(b // bb,),
        in_specs=[pl.BlockSpec((bb, COL_GATE), lambda i: (i, 0)),
                  pl.BlockSpec((bb, 128), lambda i: (i, 0)),
                  pl.BlockSpec((bb, CONV_W - 1, CONV_CH), lambda i: (i, 0, 0)),
                  pl.BlockSpec((bb, HEADS, GDN_D, GDN_D), lambda i: (i, 0, 0, 0)),
                  pl.BlockSpec((bb, hp, 128, 128), lambda i: (i, 0, 0, 0)),
                  pl.BlockSpec((bb, HEADS * ML_DK), lambda i: (i, 0)),
                  pl.BlockSpec((bb, 128), lambda i: (i, 0)),
                  pl.BlockSpec((CONV_W, CONV_CH), lambda i: (0, 0)),
                  pl.BlockSpec((3, 128), lambda i: (0, 0)),
                  pl.BlockSpec((1, GDN_D), lambda i: (0, 0)),
                  pl.BlockSpec((1, ML_DV), lambda i: (0, 0))],
        out_specs=[pl.BlockSpec((bb, HEADS * GDN_D), lambda i: (i, 0)),
                   pl.BlockSpec((bb, HEADS * ML_DV), lambda i: (i, 0)),
                   pl.BlockSpec((bb, HEADS, GDN_D, GDN_D), lambda i: (i, 0, 0, 0)),
                   pl.BlockSpec((bb, hp, 128, 128), lambda i: (i, 0, 0, 0)),
                   pl.BlockSpec((bb, HEADS * ML_DK), lambda i: (i, 0)),
                   pl.BlockSpec((bb, 128), lambda i: (i, 0))],
        out_shape=[jax.ShapeDtypeStruct((b, HEADS * GDN_D), BF16),
                   jax.ShapeDtypeStruct((b, HEADS * ML_DV), BF16),
                   jax.ShapeDtypeStruct((b, HEADS, GDN_D, GDN_D), F32),
                   jax.ShapeDtypeStruct((b, hp, 128, 128), F32),
                   jax.ShapeDtypeStruct((b, HEADS * ML_DK), F32),
                   jax.ShapeDtypeStruct((b, 128), F32)],
        scratch_shapes=[pltpu.VMEM((HEADS, bb, 128), F32), pltpu.VMEM((HEADS, bb, 128), F32)],
        compiler_params=pltpu.CompilerParams(dimension_semantics=("arbitrary",),
                                             vmem_limit_bytes=VMEM_LIMIT),
        name="decode_step",
    )(proj, gates, cs, s0, c0, n0, m0, conv_w, plane, gng, gnm)


def _lane_vec(pairs):
    v = jnp.zeros((128,), F32)
    for off, val in pairs:
        v = v.at[off:off + HEADS].set(val.astype(F32))
    return v


def _prep_params(norm_pre_mix, w_in, conv_w, a_log, dt_bias, gdn_norm_g, b_igate, b_fgate, mlstm_norm_g,
                 w_out, norm_post_mix, norm_pre_mlp, w_up, w_down, norm_post_mlp):
    alog_v = _lane_vec([(LANE_GA, a_log[0])])
    dtb_v = _lane_vec([(LANE_GA, dt_bias[0])])
    bias_v = _lane_vec([(LANE_MI, b_igate[0]), (LANE_MF, b_fgate[0])])
    return dict(
        w_in=jnp.swapaxes(w_in[0], 0, 1), wo=w_out[0], wu=w_up[0], wd=w_down[0],
        g_pre=norm_pre_mix[0][None, :], g1=norm_post_mix[0][None, :], g2=norm_pre_mlp[0][None, :],
        g3=norm_post_mlp[0][None, :], cw=conv_w[0], gng=gdn_norm_g[0][None, :], gnm=mlstm_norm_g[0][None, :],
        plane=jnp.stack([alog_v, dtb_v, bias_v]),
        psub=jnp.stack([alog_v[:16], dtb_v[:16], bias_v[:16]], axis=1),
    )


def _prompt_path(x, conv0, s0, c0, n0, m0, prm, w_in_r, w_mlp, tb, tm):
    bsz, seq, d = x.shape
    hp = HEADS // 2
    x2d = x.reshape(bsz * seq, d)
    proj, gates, gates_t = _in_proj(x2d, prm["g_pre"], w_in_r, tm=tm, out_dtype=BF16)
    proj = proj.reshape(bsz, seq, COL_GATE)
    gates = gates.reshape(bsz, seq, 128)
    gates_t = gates_t.reshape(bsz, seq // CHUNK, 16, CHUNK)
    cs8 = jnp.pad(conv0, ((0, 0), (8 - (CONV_W - 1), 0), (0, 0)))
    y, s_new, c_new, n_new, m_new = _layer_prompt(
        proj, gates, gates_t, cs8, s0, c0.reshape(bsz, hp, 128, 128), n0.reshape(bsz, hp, 1, 128),
        jnp.pad(m0, ((0, 0), (0, 128 - HEADS)))[:, None, :], x2d, prm["cw"], prm["plane"], prm["psub"],
        prm["gng"], prm["gnm"], w_mlp[0], prm["g1"], prm["g2"], w_mlp[1], w_mlp[2], prm["g3"], tb=tb)
    y = y.reshape(bsz, seq, d)
    xp = jnp.concatenate([conv0, proj[:, seq - (CONV_W - 1):, COL_QKV:COL_QKV + CONV_CH].astype(F32)], axis=1)
    conv_new = xp[:, -(CONV_W - 1):]
    return y, (conv_new, s_new, c_new.reshape(bsz, HEADS, ML_DK, ML_DV), n_new.reshape(bsz, HEADS, ML_DK),
               m_new[:, 0, :HEADS])


def _sample_path(x, conv0, s0, c0, n0, m0, prm):
    dec, _, d = x.shape
    hp = HEADS // 2
    xs = x.reshape(dec, d)
    proj, gates, _, w_in_r = _in_proj_regroup(xs, prm["g_pre"], prm["w_in"])
    og, om, s_new, c_new, n_new, m_new = _decode(
        proj, gates, conv0, s0, c0.reshape(dec, hp, 128, 128), n0.reshape(dec, HEADS * ML_DK),
        jnp.pad(m0, ((0, 0), (0, 128 - HEADS))), prm["cw"], prm["plane"], prm["gng"], prm["gnm"], bb=16)
    y, wo_b, wu_b, wd_b = _out_mlp(og, om, xs, prm["wo"], prm["g1"], prm["g2"], prm["wu"], prm["wd"], prm["g3"],
                                   ff_chunk=1024)
    y = y.reshape(dec, 1, d)
    conv_new = jnp.concatenate([conv0[:, 1:, :], proj[:, None, COL_QKV:COL_QKV + CONV_CH]], axis=1)
    return y, (conv_new, s_new, c_new.reshape(dec, HEADS, ML_DK, ML_DV), n_new.reshape(dec, HEADS, ML_DK),
               m_new[:, :HEADS]), w_in_r, (wo_b, wu_b, wd_b)


def kernel(x_prompt, x_sample, state_gdn_conv, state_gdn_S, state_mlstm_C, state_mlstm_n, state_mlstm_m,
           norm_pre_mix, w_in, conv_w, a_log, dt_bias, gdn_norm_g, b_igate, b_fgate, mlstm_norm_g, w_out,
           norm_post_mix, norm_pre_mlp, w_up, w_down, norm_post_mlp):
    bsz = x_prompt.shape[0]
    prm = _prep_params(norm_pre_mix, w_in, conv_w, a_log, dt_bias, gdn_norm_g, b_igate, b_fgate, mlstm_norm_g,
                       w_out, norm_post_mix, norm_pre_mlp, w_up, w_down, norm_post_mlp)
    y_s, s_st, w_in_r, w_mlp = _sample_path(x_sample, state_gdn_conv[0], state_gdn_S[0], state_mlstm_C[0],
                                     state_mlstm_n[0], state_mlstm_m[0], prm)
    y_p, p_st = _prompt_path(
        x_prompt, jnp.zeros((bsz, CONV_W - 1, CONV_CH), F32), jnp.zeros((bsz, HEADS, GDN_D, GDN_D), F32),
        jnp.zeros((bsz, HEADS, ML_DK, ML_DV), F32), jnp.zeros((bsz, HEADS, ML_DK), F32),
        jnp.zeros((bsz, HEADS), F32), prm, w_in_r, w_mlp, tb=256, tm=1024)
    return (y_p, y_s) + tuple(a[None] for a in p_st) + tuple(a[None] for a in s_st)
```

```python
import functools

import jax
import jax.numpy as jnp
from jax import lax
from jax.experimental import pallas as pl
from jax.experimental.pallas import tpu as pltpu

F32 = jnp.float32
BF16 = jnp.bfloat16
EPS = 1e-6

HEADS = 4
GDN_D = 128
ML_DK = 64
ML_DV = 128
CONV_W = 4
CONV_CH = 3 * HEADS * GDN_D
CHUNK = 64

COL_QKV = 0
COL_Z = 1536
COL_MV = 2048
COL_MO = 2560
COL_MQ = 3072
COL_MK = 3328
COL_GATE = 3584
N_PROJ = COL_GATE + 128
LANE_GB, LANE_GA, LANE_MI, LANE_MF = 0, 4, 8, 12

V7X_VMEM_BYTES = 64 * 1024 * 1024
VMEM_LIMIT = V7X_VMEM_BYTES - 8 * 1024 * 1024


def _rms(x, g):
    return x * lax.rsqrt(jnp.mean(x * x, axis=-1, keepdims=True) + EPS) * g


def _softplus(x):
    return jnp.maximum(x, 0.0) + jnp.log1p(jnp.exp(-jnp.abs(x)))


def _sigmoid(x):
    return 1.0 / (1.0 + jnp.exp(-x))


def _mm(a, b):
    return jnp.dot(a.astype(BF16), b.astype(BF16), preferred_element_type=F32)


def _mm_nt(a, b):
    return lax.dot_general(a.astype(BF16), b.astype(BF16), (((1,), (1,)), ((), ())),
                           preferred_element_type=F32)


W_IN_COLS = 3600
W_IN_SPANS = ((0, COL_QKV, 2048),
              (2568, COL_MV, 1024),
              (2056, COL_MQ, 512),
              (2048, COL_GATE, 8),
              (3592, COL_GATE + 8, 8))


def _project_rows(x_ref, g_ref, w_ref, o_ref, gc_ref, gt_ref, h_ref, n_chunk):
    h_ref[...] = _rms(x_ref[...], g_ref[...]).astype(BF16)
    n = o_ref.shape[1]
    for c0 in range(0, n, n_chunk):
        c1 = min(c0 + n_chunk, n)
        o_ref[:, c0:c1] = _mm_nt(h_ref[...], w_ref[c0:c1, :]).astype(o_ref.dtype)
    gates = _mm_nt(h_ref[...], w_ref[COL_GATE:COL_GATE + 128, :])
    gc_ref[...] = gates
    gt = gates.T
    for c in range(gt_ref.shape[0]):
        gt_ref[c] = gt[0:16, c * CHUNK:(c + 1) * CHUNK]


def _in_proj_regroup_kernel(x_ref, g_ref, win_ref, o_ref, gc_ref, gt_ref, w_ref, h_ref, *, n_chunk):
    w_ref[COL_GATE:N_PROJ, :] = jnp.zeros((N_PROJ - COL_GATE, w_ref.shape[1]), BF16)
    for src, dst, width in W_IN_SPANS:
        for r0 in range(0, width, 512):
            r1 = min(r0 + 512, width)
            w_ref[dst + r0:dst + r1, :] = win_ref[src + r0:src + r1, :].astype(BF16)
    _project_rows(x_ref, g_ref, w_ref, o_ref, gc_ref, gt_ref, h_ref, n_chunk)


def _in_proj_kernel(x_ref, g_ref, w_ref, o_ref, gc_ref, gt_ref, h_ref, *, n_chunk):
    _project_rows(x_ref, g_ref, w_ref, o_ref, gc_ref, gt_ref, h_ref, n_chunk)


def _in_proj_regroup(x2d, g, w_t):
    m, k = x2d.shape
    full = lambda i: (0, 0)
    return pl.pallas_call(
        functools.partial(_in_proj_regroup_kernel, n_chunk=512),
        grid=(1,),
        in_specs=[pl.BlockSpec((m, k), full), pl.BlockSpec((1, k), full), pl.BlockSpec((W_IN_COLS, k), full)],
        out_specs=[pl.BlockSpec((m, COL_GATE), full), pl.BlockSpec((m, 128), full),
                   pl.BlockSpec((m // CHUNK, 16, CHUNK), lambda i: (0, 0, 0)), pl.BlockSpec((N_PROJ, k), full)],
        out_shape=[jax.ShapeDtypeStruct((m, COL_GATE), F32),
                   jax.ShapeDtypeStruct((m, 128), F32),
                   jax.ShapeDtypeStruct((m // CHUNK, 16, CHUNK), F32),
                   jax.ShapeDtypeStruct((N_PROJ, k), BF16)],
        scratch_shapes=[pltpu.VMEM((m, k), BF16)],
        compiler_params=pltpu.CompilerParams(dimension_semantics=("arbitrary",),
                                             vmem_limit_bytes=VMEM_LIMIT),
        name="in_proj_regroup",
    )(x2d, g, w_t)


def _in_proj(x2d, g, w, tm, out_dtype):
    m, k = x2d.shape
    return pl.pallas_call(
        functools.partial(_in_proj_kernel, n_chunk=512),
        grid=(m // tm,),
        in_specs=[pl.BlockSpec((tm, k), lambda i: (i, 0)),
                  pl.BlockSpec((1, k), lambda i: (0, 0)),
                  pl.BlockSpec((N_PROJ, k), lambda i: (0, 0), pipeline_mode=pl.Buffered(1))],
        out_specs=[pl.BlockSpec((tm, COL_GATE), lambda i: (i, 0)),
                   pl.BlockSpec((tm, 128), lambda i: (i, 0)),
                   pl.BlockSpec((tm // CHUNK, 16, CHUNK), lambda i: (i, 0, 0))],
        out_shape=[jax.ShapeDtypeStruct((m, COL_GATE), out_dtype),
                   jax.ShapeDtypeStruct((m, 128), F32),
                   jax.ShapeDtypeStruct((m // CHUNK, 16, CHUNK), F32)],
        scratch_shapes=[pltpu.VMEM((tm, k), BF16)],
        compiler_params=pltpu.CompilerParams(dimension_semantics=("arbitrary",),
                                             vmem_limit_bytes=VMEM_LIMIT),
        name="in_proj",
    )(x2d, g, w)


def _out_mlp_kernel(og_ref, om_ref, x_ref, wo_ref, g1_ref, g2_ref, wu_ref, wd_ref, g3_ref,
                    y_ref, hn_ref, acc_ref, *, ff_chunk):
    half = og_ref.shape[1]
    mix = (jnp.dot(og_ref[...], wo_ref[0:half, :], preferred_element_type=F32)
           + jnp.dot(om_ref[...], wo_ref[half:2 * half, :], preferred_element_type=F32))
    x1 = x_ref[...] + _rms(mix, g1_ref[...])
    y_ref[...] = x1
    hn_ref[...] = _rms(x1, g2_ref[...]).astype(BF16)
    d_ff = wu_ref.shape[1]
    for c0 in range(0, d_ff, ff_chunk):
        u = jnp.dot(hn_ref[...], wu_ref[:, c0:c0 + ff_chunk], preferred_element_type=F32)
        u = jnp.square(jnp.maximum(u, 0.0)).astype(BF16)
        d = jnp.dot(u, wd_ref[c0:c0 + ff_chunk, :], preferred_element_type=F32)
        if c0 == 0:
            acc_ref[...] = d
        else:
            acc_ref[...] += d
    y_ref[...] = y_ref[...] + _rms(acc_ref[...], g3_ref[...])


def _out_mlp(og, om, x2d, wo, g1, g2, wu, wd, g3, tm):
    m, d = x2d.shape
    half = og.shape[1]
    d_ff = wu.shape[1]
    const = lambda i: (0, 0)
    return pl.pallas_call(
        functools.partial(_out_mlp_kernel, ff_chunk=1024),
        grid=(m // tm,),
        in_specs=[pl.BlockSpec((tm, half), lambda i: (i, 0)),
                  pl.BlockSpec((tm, half), lambda i: (i, 0)),
                  pl.BlockSpec((tm, d), lambda i: (i, 0)),
                  pl.BlockSpec((d, d), const, pipeline_mode=pl.Buffered(1)),
                  pl.BlockSpec((1, d), const),
                  pl.BlockSpec((1, d), const),
                  pl.BlockSpec((d, d_ff), const, pipeline_mode=pl.Buffered(1)),
                  pl.BlockSpec((d_ff, d), const, pipeline_mode=pl.Buffered(1)),
                  pl.BlockSpec((1, d), const)],
        out_specs=pl.BlockSpec((tm, d), lambda i: (i, 0)),
        out_shape=jax.ShapeDtypeStruct((m, d), F32),
        scratch_shapes=[pltpu.VMEM((tm, d), BF16), pltpu.VMEM((tm, d), F32)],
        compiler_params=pltpu.CompilerParams(dimension_semantics=("arbitrary",),
                                             vmem_limit_bytes=VMEM_LIMIT),
        name="out_mlp",
    )(og, om, x2d, wo, g1, g2, wu, wd, g3)


def _chunk_masks():
    ii = lax.broadcasted_iota(jnp.int32, (CHUNK, CHUNK), 0)
    jj = lax.broadcasted_iota(jnp.int32, (CHUNK, CHUNK), 1)
    return ii, jj


def _cumsum_col_row(x_col, x_row, ii, jj):
    c_col = jnp.sum(jnp.where(jj <= ii, x_row, 0.0), axis=1, keepdims=True)
    c_row = jnp.sum(jnp.where(ii <= jj, x_col, 0.0), axis=0, keepdims=True)
    return c_col, c_row


def _interleave(*tasks):
    live = [[g, n, 0] for g, n in tasks]
    while live:
        entry = min(live, key=lambda e: (e[2] + 1) / e[1])
        try:
            next(entry[0])
            entry[2] += 1
        except StopIteration:
            live.remove(entry)


def _halves(items):
    mid = len(items) // 2
    return items[:mid], items[mid:]


def _gdn_phase_a(yc_ref, gcol_ref, grow_ref, plane_ref, psub_ref, wv_ref, lhs_ref, qk_ref, ket_ref, ge_ref,
                 ncb):
    ii, jj = _chunk_masks()
    incl = jj <= ii
    strict = jj < ii
    eye = (ii == jj).astype(F32)
    neg_a_lane = -jnp.exp(plane_ref[0:1, :])
    dtb_lane = plane_ref[1:2, :]
    neg_a_sub = -jnp.exp(psub_ref[:, 0:1])
    dtb_sub = psub_ref[:, 1:2]
    items = []
    for c in range(ncb):
        rows = slice(c * CHUNK, (c + 1) * CHUNK)
        gc = gcol_ref[rows, :]
        gr = grow_ref[c]
        beta_t = _sigmoid(gc)
        g_t = neg_a_lane * _softplus(gc + dtb_lane)
        g_r = neg_a_sub * _softplus(gr + dtb_sub)
        for h in range(HEADS):
            lo = h * GDN_D
            items.append(dict(c=c, h=h,
                              q=yc_ref[rows, lo:lo + GDN_D],
                              k=yc_ref[rows, 512 + lo:512 + lo + GDN_D],
                              v=yc_ref[rows, 1024 + lo:1024 + lo + GDN_D],
                              beta=beta_t[:, LANE_GB + h:LANE_GB + h + 1],
                              gg_col=g_t[:, LANE_GA + h:LANE_GA + h + 1],
                              gg_row=g_r[LANE_GA + h:LANE_GA + h + 1, :]))
    yield
    for it in items:
        it["k"] = it["k"] * lax.rsqrt(jnp.sum(it["k"] * it["k"], axis=-1, keepdims=True) + EPS)
    yield
    for it in items:
        it["kk"] = _mm_nt(it["k"], it["k"])
    yield
    for it in items:
        it["q"] = (it["q"] * lax.rsqrt(jnp.sum(it["q"] * it["q"], axis=-1, keepdims=True) + EPS)
                   * (GDN_D ** -0.5))
    yield
    for it in items:
        it["g_col"], g_row = _cumsum_col_row(it["gg_col"], it["gg_row"], ii, jj)
        it["decay"] = jnp.where(incl, jnp.exp(jnp.where(incl, it["g_col"] - g_row, 0.0)), 0.0)
    yield
    for it in items:
        n_mat = jnp.where(strict, it["beta"] * it["kk"] * it["decay"], 0.0)
        it["x"] = eye - n_mat
        it["p"] = -n_mat
    yield
    for _ in range(5):
        yield
        for it in items:
            it["p"] = _mm(it["p"], it["p"])
        yield
        for it in items:
            it["x"] = it["x"] + _mm(it["x"], it["p"])
    yield
    for it in items:
        e_g = jnp.exp(it["g_col"])
        it["e_g"] = e_g
        rhs = jnp.concatenate([it["beta"] * it["v"], (it["beta"] * e_g) * it["k"]], axis=1)
        it["w"] = _mm(it["x"], rhs)
    yield
    for it in items:
        it["qk"] = _mm_nt(it["q"], it["k"]) * it["decay"]
    yield
    for it in items:
        c, h = it["c"], it["h"]
        g_end = it["g_col"][CHUNK - 1:CHUNK, :]
        k_end = it["k"] * jnp.exp(g_end - it["g_col"])
        wv_ref[c, h] = it["w"][:, 0:GDN_D]
        lhs_ref[c, h] = jnp.concatenate([it["w"][:, GDN_D:2 * GDN_D], it["e_g"] * it["q"]],
                                        axis=0).astype(BF16)
        qk_ref[c, h] = it["qk"].astype(BF16)
        ket_ref[c, h] = k_end.T.astype(BF16)
        ge_ref[c, h] = jnp.broadcast_to(jnp.exp(g_end), (1, GDN_D))


def _gdn_phase_b(qkvz_ref, gn_ref, og_ref, s_ref, wv_ref, lhs_ref, qk_ref, ket_ref, ge_ref, ncb):
    gn = gn_ref[...]

    def epilogue(c, o):
        rows = slice(c * CHUNK, (c + 1) * CHUNK)
        for h in range(HEADS):
            lo = h * GDN_D
            z = qkvz_ref[rows, COL_Z + lo:COL_Z + lo + GDN_D].astype(F32)
            out = _rms(o[h], gn) * (z * _sigmoid(z))
            og_ref[rows, lo:lo + GDN_D] = out.astype(og_ref.dtype)

    s = [s_ref[h] for h in range(HEADS)]
    o_prev = None
    for c in range(ncb):
        r = [jnp.dot(lhs_ref[c, h], s[h].astype(BF16), preferred_element_type=F32) for h in range(HEADS)]
        yield
        if o_prev is not None:
            epilogue(c - 1, o_prev)
        ub = [(wv_ref[c, h] - r[h][0:CHUNK]).astype(BF16) for h in range(HEADS)]
        s = [ge_ref[c, h] * s[h] + jnp.dot(ket_ref[c, h], ub[h], preferred_element_type=F32)
             for h in range(HEADS)]
        o_prev = [r[h][CHUNK:2 * CHUNK] + jnp.dot(qk_ref[c, h], ub[h], preferred_element_type=F32)
                  for h in range(HEADS)]
        yield
    epilogue(ncb - 1, o_prev)
    for h in range(HEADS):
        s_ref[h] = s[h]


def _mlstm_block(mvo_ref, mqk_ref, gcol_ref, grow_ref, plane_ref, psub_ref, gn_ref, om_ref,
                 c_ref, n_ref, m_ref, ncb):
    ii, jj = _chunk_masks()
    incl = jj <= ii
    lane = lax.broadcasted_iota(jnp.int32, (1, 128), 1)
    row128 = lax.broadcasted_iota(jnp.int32, (128, 1), 0)
    gn = gn_ref[...]
    blane = plane_ref[2:3, :]
    bsub = psub_ref[:, 2:3]
    items = []
    for c in range(ncb):
        rows = slice(c * CHUNK, (c + 1) * CHUNK)
        gc = gcol_ref[rows, :] + blane
        gr = grow_ref[c] + bsub
        lf_c = -_softplus(-gc)
        lf_r = -_softplus(-gr)
        for p in range(HEADS // 2):
            qb = mqk_ref[rows, p * 128:(p + 1) * 128].astype(F32)
            kb = mqk_ref[rows, 256 + p * 128:256 + (p + 1) * 128].astype(F32) * (ML_DK ** -0.5)
            for e in range(2):
                h = 2 * p + e
                lm = (lane >= e * ML_DK) & (lane < (e + 1) * ML_DK)
                items.append(dict(c=c, rows=rows, p=p, e=e, h=h, qh=jnp.where(lm, qb, 0.0), kb=kb,
                                  kh=jnp.where(lm, kb, 0.0),
                                  lf_col=lf_c[:, LANE_MF + h:LANE_MF + h + 1],
                                  lf_row=lf_r[LANE_MF + h:LANE_MF + h + 1, :],
                                  ig_row=gr[LANE_MI + h:LANE_MI + h + 1, :],
                                  ig_col=gc[:, LANE_MI + h:LANE_MI + h + 1],
                                  v=mvo_ref[rows, h * ML_DV:(h + 1) * ML_DV]))
        if c % 2 == 1:
            yield
    for part in _halves(items):
        for it in part:
            it["qk"] = _mm_nt(it["qh"], it["kb"])
        yield
    for part in _halves(items):
        for it in part:
            it["f_col"], it["f_row"] = _cumsum_col_row(it["lf_col"], it["lf_row"], ii, jj)
        yield
    for part in _halves(items):
        for it in part:
            it["d_mat"] = jnp.where(incl, it["f_col"] - it["f_row"] + it["ig_row"], -jnp.inf)
            it["d_max"] = jnp.max(it["d_mat"], axis=1, keepdims=True)
        yield
    for part in _halves(items):
        for it in part:
            p0 = jnp.where(incl, jnp.exp(jnp.where(incl, it["d_mat"] - it["d_max"], 0.0)), 0.0)
            pend0 = jnp.exp(it["f_col"][CHUNK - 1:CHUNK, :] - it["f_col"] + it["ig_col"]
                            - it["d_max"][CHUNK - 1:CHUNK, :])
            it["kp0"] = it["kh"] * pend0
            it["pqk0"] = p0 * it["qk"]
        yield
    for part in _halves(items):
        for it in part:
            it["pv0"] = _mm(it["pqk0"], it["v"])
            it["rs0"] = jnp.sum(it["pqk0"], axis=-1, keepdims=True)
        yield
    for part in _halves(items):
        for it in part:
            it["cadd0"] = _mm(it["kp0"].T, it["v"])
            it["nadd0"] = jnp.sum(it["kp0"], axis=0, keepdims=True)
        yield

    m_cur = [m_ref[:, h:h + 1] for h in range(HEADS)]
    for it in items:
        h = it["h"]
        it["m_prev"] = m_cur[h]
        m_cur[h] = jnp.maximum(it["f_col"][CHUNK - 1:CHUNK, :] + m_cur[h], it["d_max"][CHUNK - 1:CHUNK, :])
    yield
    for part in _halves(items):
        for it in part:
            bcol = it["f_col"] + it["m_prev"]
            mt = jnp.maximum(bcol, it["d_max"])
            it["mt"] = mt
            it["w_prev"] = jnp.exp(bcol - mt)
            it["sc"] = jnp.exp(it["d_max"] - mt)
        yield
    c_cur = [c_ref[p] for p in range(HEADS // 2)]
    n_cur = [n_ref[p] for p in range(HEADS // 2)]
    for c in range(ncb):
        for p in range(HEADS // 2):
            pair = [it for it in items if it["c"] == c and it["p"] == p]
            w_end = [it["w_prev"][CHUNK - 1:CHUNK, :] for it in pair]
            s_end = [it["sc"][CHUNK - 1:CHUNK, :] for it in pair]
            for it in pair:
                it["c_prev"] = c_cur[p]
                it["n_prev"] = n_cur[p]
            c_cur[p] = (jnp.where(row128 < ML_DK, w_end[0], w_end[1]) * c_cur[p]
                        + s_end[0] * pair[0]["cadd0"] + s_end[1] * pair[1]["cadd0"])
            n_cur[p] = (jnp.where(lane < ML_DK, w_end[0], w_end[1]) * n_cur[p]
                        + s_end[0] * pair[0]["nadd0"] + s_end[1] * pair[1]["nadd0"])
        if c % 2 == 1:
            yield
    for part in _halves(items):
        for it in part:
            it["qc"] = _mm(it["qh"], it["c_prev"])
        yield
    for part in _halves(items):
        for it in part:
            num = it["w_prev"] * it["qc"] + it["sc"] * it["pv0"]
            den = (it["w_prev"] * jnp.sum(it["qh"] * it["n_prev"], axis=-1, keepdims=True)
                   + it["sc"] * it["rs0"])
            it["hh"] = num / jnp.maximum(jnp.abs(den), jnp.exp(-it["mt"]))
        yield
    for part in _halves(items):
        for it in part:
            h = it["h"]
            mo = mvo_ref[it["rows"], 512 + h * ML_DV:512 + (h + 1) * ML_DV].astype(F32)
            out = _sigmoid(mo) * _rms(it["hh"], gn)
            om_ref[it["rows"], h * ML_DV:(h + 1) * ML_DV] = out.astype(om_ref.dtype)
        yield
    for p in range(HEADS // 2):
        c_ref[p] = c_cur[p]
        n_ref[p] = n_cur[p]
    for h in range(HEADS):
        m_ref[:, h:h + 1] = m_cur[h]


def _chain(*gens):
    for g in gens:
        yield from g


def _conv_stage(xp_ref, yc_ref, cw_ref, tb):
    for ct in range(CONV_CH // 128):
        cols = slice(ct * 128, (ct + 1) * 128)
        w = cw_ref[:, cols]
        for r0 in range(0, tb, 128):
            acc = xp_ref[8 + r0:8 + r0 + 128, cols] * w[CONV_W - 1:CONV_W, :]
            for j in range(CONV_W - 1):
                acc = acc + xp_ref[5 + j + r0:5 + j + r0 + 128, cols] * w[j:j + 1, :]
            yc_ref[r0:r0 + 128, cols] = acc * _sigmoid(acc)
        yield


def _out_mlp_block(og_ref, om_ref, x_ref, wo_ref, g1_ref, g2_ref, wu_ref, wd_ref, g3_ref,
                   y_ref, hn_ref, acc_ref, ff_chunk, n_split):
    half = og_ref.shape[1]
    d = x_ref.shape[1]
    d_ff = wu_ref.shape[1]
    col_groups = [slice(c, c + d // n_split) for c in range(0, d, d // n_split)]
    for cg in col_groups:
        acc_ref[:, cg] = (jnp.dot(og_ref[...], wo_ref[0:half, cg], preferred_element_type=F32)
                          + jnp.dot(om_ref[...], wo_ref[half:2 * half, cg], preferred_element_type=F32))
        yield
    x1 = x_ref[...] + _rms(acc_ref[...], g1_ref[...])
    y_ref[...] = x1
    hn_ref[...] = _rms(x1, g2_ref[...]).astype(BF16)
    yield
    for c0 in range(0, d_ff, ff_chunk):
        u = jnp.dot(hn_ref[...], wu_ref[:, c0:c0 + ff_chunk], preferred_element_type=F32)
        u = jnp.square(jnp.maximum(u, 0.0)).astype(BF16)
        for cg in col_groups:
            dd = jnp.dot(u, wd_ref[c0:c0 + ff_chunk, cg], preferred_element_type=F32)
            if c0 == 0:
                acc_ref[:, cg] = dd
            else:
                acc_ref[:, cg] += dd
        yield
    y_ref[...] = y_ref[...] + _rms(acc_ref[...], g3_ref[...])
    yield


def _layer_kernel(qkvz_ref, mvo_ref, mqk_ref, gcol_ref, grow_ref, cs_ref, s0_ref, c0_ref, n0_ref, m0_ref,
                  x_ref, cw_ref, plane_ref, psub_ref, gng_ref, gnm_ref,
                  wo_ref, g1_ref, g2_ref, wu_ref, wd_ref, g3_ref,
                  y_ref, sout_ref, cout_ref, nout_ref, mout_ref,
                  xp_ref, yc_ref, s_ref, wv_ref, lhs_ref, qk_ref, ket_ref, ge_ref, c_ref, n_ref, m_ref,
                  og_ref, om_ref, hn_ref, acc_ref, *, tb, nt, n_blocks):
    g = pl.program_id(0)
    t = jnp.minimum(g, n_blocks - 1) % nt
    ncb = tb // CHUNK
    par = g % 2

    @pl.when(g == 0)
    def _():
        og_ref[1] = jnp.zeros(og_ref.shape[1:], og_ref.dtype)
        om_ref[1] = jnp.zeros(om_ref.shape[1:], om_ref.dtype)

    @pl.when(t == 0)
    def _():
        xp_ref[0:8, :] = cs_ref[...]
        s_ref[...] = s0_ref[...]
        c_ref[...] = c0_ref[...]
        n_ref[...] = n0_ref[...]
        m_ref[...] = m0_ref[...]

    @pl.when(t > 0)
    def _():
        xp_ref[0:8, :] = xp_ref[tb:tb + 8, :]

    xp_ref[8:tb + 8, :] = qkvz_ref[:, COL_QKV:COL_QKV + CONV_CH].astype(F32)

    og_w, om_w = og_ref.at[par], om_ref.at[par]
    og_r, om_r = og_ref.at[1 - par], om_ref.at[1 - par]
    mlp = _out_mlp_block(og_r, om_r, x_ref, wo_ref, g1_ref, g2_ref, wu_ref, wd_ref, g3_ref, y_ref, hn_ref, acc_ref,
                         ff_chunk=256, n_split=2)
    n_mlp = 2 + 1 + wu_ref.shape[1] // 256 + 1
    n_first = 8

    def _first(n):
        for _ in range(n):
            next(mlp)
            yield

    @pl.when(g >= 0)
    def _():
        _interleave(
            (_chain(_conv_stage(xp_ref, yc_ref, cw_ref, tb),
                    _gdn_phase_a(yc_ref, gcol_ref, grow_ref, plane_ref, psub_ref, wv_ref, lhs_ref, qk_ref, ket_ref,
                                 ge_ref, ncb)), CONV_CH // 128 + 20),
            (_first(n_first), n_first))

    @pl.when(g >= 0)
    def _():
        _interleave(
            (_gdn_phase_b(qkvz_ref, gng_ref, og_w, s_ref, wv_ref, lhs_ref, qk_ref, ket_ref, ge_ref, ncb), 2 * ncb),
            (mlp, n_mlp - n_first),
            (_mlstm_block(mvo_ref, mqk_ref, gcol_ref, grow_ref, plane_ref, psub_ref, gnm_ref, om_w,
                          c_ref, n_ref, m_ref, ncb), 21 + ncb))

    @pl.when((t == nt - 1) & (g < n_blocks))
    def _():
        sout_ref[...] = s_ref[...]
        cout_ref[...] = c_ref[...]
        nout_ref[...] = n_ref[...]
        mout_ref[...] = m_ref[...]


def _layer_prompt(proj, gates, gates_t, cs8, s0, c0, n0, m0, x2d, conv_w, plane, psub, gng, gnm,
                  wo, g1, g2, wu, wd, g3, tb):
    b, t, _ = proj.shape
    d = x2d.shape[1]
    d_ff = wu.shape[1]
    nt = t // tb
    n_blocks = b * nt
    ncb = tb // CHUNK
    hp = HEADS // 2

    def blk(g):
        gm = jnp.minimum(g, n_blocks - 1)
        return gm // nt, gm % nt

    def tok3(col):
        return lambda g: blk(g) + (col,)

    per_seq3 = lambda g: (blk(g)[0], 0, 0)
    per_seq4 = lambda g: (blk(g)[0], 0, 0, 0)
    prev_rows = lambda g: (jnp.maximum(g - 1, 0), 0)
    const = lambda g: (0, 0)
    resident = dict(pipeline_mode=pl.Buffered(1))
    return pl.pallas_call(
        functools.partial(_layer_kernel, tb=tb, nt=nt, n_blocks=n_blocks),
        grid=(n_blocks + 1,),
        in_specs=[pl.BlockSpec((None, tb, 2048), tok3(0)),
                  pl.BlockSpec((None, tb, 1024), tok3(COL_MV // 1024)),
                  pl.BlockSpec((None, tb, 512), tok3(COL_MQ // 512)),
                  pl.BlockSpec((None, tb, 128), tok3(0)),
                  pl.BlockSpec((None, ncb, 16, CHUNK), lambda g: blk(g) + (0, 0)),
                  pl.BlockSpec((None, 8, CONV_CH), per_seq3),
                  pl.BlockSpec((None, HEADS, GDN_D, GDN_D), per_seq4),
                  pl.BlockSpec((None, hp, 128, 128), per_seq4),
                  pl.BlockSpec((None, hp, 1, 128), per_seq4),
                  pl.BlockSpec((None, 1, 128), per_seq3),
                  pl.BlockSpec((tb, d), prev_rows),
                  pl.BlockSpec((CONV_W, CONV_CH), const),
                  pl.BlockSpec((3, 128), const),
                  pl.BlockSpec((16, 3), const),
                  pl.BlockSpec((1, GDN_D), const),
                  pl.BlockSpec((1, ML_DV), const),
                  pl.BlockSpec((d, d), const, **resident),
                  pl.BlockSpec((1, d), const),
                  pl.BlockSpec((1, d), const),
                  pl.BlockSpec((d, d_ff), const, **resident),
                  pl.BlockSpec((d_ff, d), const, **resident),
                  pl.BlockSpec((1, d), const)],
        out_specs=[pl.BlockSpec((tb, d), prev_rows),
                   pl.BlockSpec((None, HEADS, GDN_D, GDN_D), per_seq4),
                   pl.BlockSpec((None, hp, 128, 128), per_seq4),
                   pl.BlockSpec((None, hp, 1, 128), per_seq4),
                   pl.BlockSpec((None, 1, 128), per_seq3)],
        out_shape=[jax.ShapeDtypeStruct((b * t, d), F32),
                   jax.ShapeDtypeStruct((b, HEADS, GDN_D, GDN_D), F32),
                   jax.ShapeDtypeStruct((b, hp, 128, 128), F32),
                   jax.ShapeDtypeStruct((b, hp, 1, 128), F32),
                   jax.ShapeDtypeStruct((b, 1, 128), F32)],
        scratch_shapes=[pltpu.VMEM((tb + 8, CONV_CH), F32),
                        pltpu.VMEM((tb, CONV_CH), F32),
                        pltpu.VMEM((HEADS, GDN_D, GDN_D), F32),
                        pltpu.VMEM((ncb, HEADS, CHUNK, GDN_D), F32),
                        pltpu.VMEM((ncb, HEADS, 2 * CHUNK, GDN_D), BF16),
                        pltpu.VMEM((ncb, HEADS, CHUNK, CHUNK), BF16),
                        pltpu.VMEM((ncb, HEADS, GDN_D, CHUNK), BF16),
                        pltpu.VMEM((ncb, HEADS, 1, GDN_D), F32),
                        pltpu.VMEM((hp, 128, 128), F32),
                        pltpu.VMEM((hp, 1, 128), F32),
                        pltpu.VMEM((1, 128), F32),
                        pltpu.VMEM((2, tb, HEADS * GDN_D), BF16),
                        pltpu.VMEM((2, tb, HEADS * ML_DV), BF16),
                        pltpu.VMEM((tb, d), BF16),
                        pltpu.VMEM((tb, d), F32)],
        compiler_params=pltpu.CompilerParams(dimension_semantics=("arbitrary",),
                                             vmem_limit_bytes=VMEM_LIMIT),
        name="layer_prompt",
    )(proj, proj, proj, gates, gates_t, cs8, s0, c0, n0, m0, x2d, conv_w, plane, psub, gng, gnm,
      wo, g1, g2, wu, wd, g3)


def _decode_kernel(proj_ref, gates_ref, cs_ref, s0_ref, c0_ref, n0_ref, m0_ref, cw_ref, plane_ref, gng_ref, gnm_ref,
                   og_ref, om_ref, sout_ref, cout_ref, nout_ref, mout_ref, rowg_ref, rowm_ref, *, bb):
    lane = lax.broadcasted_iota(jnp.int32, (1, 128), 1)
    gt = gates_ref[...]
    beta_t = _sigmoid(gt)
    g_t = -jnp.exp(plane_ref[0:1, :]) * _softplus(gt + plane_ref[1:2, :])
    eg_t = jnp.exp(g_t)
    gb_t = gt + plane_ref[2:3, :]
    lf_t = -_softplus(-gb_t)

    heads = []
    for h in range(HEADS):
        cols = []
        for part in range(3):
            c0 = part * 512 + h * GDN_D
            w = cw_ref[:, c0:c0 + GDN_D]
            acc = cs_ref[:, 0, c0:c0 + GDN_D] * w[0:1, :]
            acc = acc + cs_ref[:, 1, c0:c0 + GDN_D] * w[1:2, :]
            acc = acc + cs_ref[:, 2, c0:c0 + GDN_D] * w[2:3, :]
            acc = acc + proj_ref[:, COL_QKV + c0:COL_QKV + c0 + GDN_D] * w[3:4, :]
            cols.append(acc * _sigmoid(acc))
        q, k, v = cols
        q = q * lax.rsqrt(jnp.sum(q * q, axis=-1, keepdims=True) + EPS) * (GDN_D ** -0.5)
        k = k * lax.rsqrt(jnp.sum(k * k, axis=-1, keepdims=True) + EPS)
        heads.append(dict(v=v, qk=jnp.sum(q * k, axis=-1, keepdims=True), k_t=k.T.astype(BF16),
                          kq=jnp.concatenate([k, q], axis=0).astype(BF16),
                          beta=beta_t[:, LANE_GB + h:LANE_GB + h + 1],
                          eg=eg_t[:, LANE_GA + h:LANE_GA + h + 1]))
    items = [dict(h=h, b=b) for h in range(HEADS) for b in range(bb)]
    row_b = lax.broadcasted_iota(jnp.int32, (bb, 1), 0)
    for it in items:
        r = jnp.dot(heads[it["h"]]["kq"], s0_ref[it["b"], it["h"]].astype(BF16), preferred_element_type=F32)
        it["ks"] = r[it["b"]:it["b"] + 1, :]
        it["qs"] = r[bb + it["b"]:bb + it["b"] + 1, :]
    for it in items:
        hd, b = heads[it["h"]], it["b"]
        eg_b = hd["eg"][b:b + 1, :]
        it["eg_b"] = eg_b
        it["u"] = hd["beta"][b:b + 1, :] * (hd["v"][b:b + 1, :] - eg_b * it["ks"])
        rowg_ref[it["h"], b:b + 1, :] = eg_b * it["qs"] + hd["qk"][b:b + 1, :] * it["u"]
    for it in items:
        u_sel = jnp.where(row_b == it["b"], it["u"], 0.0).astype(BF16)
        upd = jnp.dot(heads[it["h"]]["k_t"], u_sel, preferred_element_type=F32)
        sout_ref[it["b"], it["h"]] = it["eg_b"] * s0_ref[it["b"], it["h"]] + upd
    for h in range(HEADS):
        z = proj_ref[:, COL_Z + h * GDN_D:COL_Z + (h + 1) * GDN_D]
        out = _rms(rowg_ref[h], gng_ref[...]) * (z * _sigmoid(z))
        og_ref[:, h * GDN_D:(h + 1) * GDN_D] = out.astype(og_ref.dtype)

    heads = []
    for p in range(HEADS // 2):
        qb = proj_ref[:, COL_MQ + p * 128:COL_MQ + (p + 1) * 128]
        kb = proj_ref[:, COL_MK + p * 128:COL_MK + (p + 1) * 128] * (ML_DK ** -0.5)
        n_p = n0_ref[:, p * 128:(p + 1) * 128]
        k_t = jnp.concatenate([jnp.where(lane < ML_DK, kb, 0.0), jnp.where(lane >= ML_DK, kb, 0.0)],
                              axis=0).T.astype(BF16)
        qq = jnp.concatenate([jnp.where(lane < ML_DK, qb, 0.0), jnp.where(lane >= ML_DK, qb, 0.0)],
                             axis=0).astype(BF16)
        w_prev, p_in = [], []
        for e in range(2):
            h = 2 * p + e
            lm = (lane >= e * ML_DK) & (lane < (e + 1) * ML_DK)
            ig = gb_t[:, LANE_MI + h:LANE_MI + h + 1]
            lf = lf_t[:, LANE_MF + h:LANE_MF + h + 1]
            m_old = m0_ref[:, h:h + 1]
            m_new = jnp.maximum(lf + m_old, ig)
            w_prev.append(jnp.exp(lf + m_old - m_new))
            p_in.append(jnp.exp(ig - m_new))
            qk = jnp.sum(jnp.where(lm, qb * kb, 0.0), axis=-1, keepdims=True)
            qn = jnp.sum(jnp.where(lm, qb * n_p, 0.0), axis=-1, keepdims=True)
            mout_ref[:, h:h + 1] = m_new
            heads.append(dict(p=p, e=e, qq=qq, k_t=k_t, w_prev=w_prev[e], p_in=p_in[e],
                              pqk=p_in[e] * qk, wqn=w_prev[e] * qn, floor=jnp.exp(-m_new),
                              v=proj_ref[:, COL_MV + h * ML_DV:COL_MV + (h + 1) * ML_DV]))
        lo_lane = lane < ML_DK
        nout_ref[:, p * 128:(p + 1) * 128] = (jnp.where(lo_lane, w_prev[0], w_prev[1]) * n_p
                                              + jnp.where(lo_lane, p_in[0], p_in[1]) * kb)
    items = [dict(h=h, b=b) for h in range(HEADS) for b in range(bb)]
    qc_pair = {(p, b): jnp.dot(heads[2 * p]["qq"], c0_ref[b, p].astype(BF16), preferred_element_type=F32)
               for p in range(HEADS // 2) for b in range(bb)}
    for it in items:
        hd, b = heads[it["h"]], it["b"]
        row = hd["e"] * bb + b
        it["qc"] = qc_pair[(hd["p"], b)][row:row + 1, :]
    for it in items:
        hd, b = heads[it["h"]], it["b"]
        pqk = hd["pqk"][b:b + 1, :]
        num = hd["w_prev"][b:b + 1, :] * it["qc"] + pqk * hd["v"][b:b + 1, :]
        den = hd["wqn"][b:b + 1, :] + pqk
        rowm_ref[it["h"], b:b + 1, :] = num / jnp.maximum(jnp.abs(den), hd["floor"][b:b + 1, :])
    row128 = lax.broadcasted_iota(jnp.int32, (128, 1), 0)
    for p in range(HEADS // 2):
        h0, h1 = heads[2 * p], heads[2 * p + 1]
        pv = jnp.concatenate([h0["p_in"] * h0["v"], h1["p_in"] * h1["v"]], axis=0)
        row_2b = lax.broadcasted_iota(jnp.int32, (2 * bb, 1), 0)
        for b in range(bb):
            pv_sel = jnp.where((row_2b == b) | (row_2b == bb + b), pv, 0.0).astype(BF16)
            upd = jnp.dot(h0["k_t"], pv_sel, preferred_element_type=F32)
            w_col = jnp.where(row128 < ML_DK, h0["w_prev"][b:b + 1, :], h1["w_prev"][b:b + 1, :])
            cout_ref[b, p] = w_col * c0_ref[b, p] + upd
    for h in range(HEADS):
        mo = proj_ref[:, COL_MO + h * ML_DV:COL_MO + (h + 1) * ML_DV]
        out = _sigmoid(mo) * _rms(rowm_ref[h], gnm_ref[...])
        om_ref[:, h * ML_DV:(h + 1) * ML_DV] = out.astype(om_ref.dtype)
    mout_ref[:, HEADS:128] = m0_ref[:, HEADS:128]


def _decode(proj, gates, cs, s0, c0, n0, m0, conv_w, plane, gng, gnm, bb):
    b = proj.shape[0]
    hp = HEADS // 2
    return pl.pallas_call(
        functools.partial(_decode_kernel, bb=bb),
        grid=(b // bb,),
        in_specs=[pl.BlockSpec((bb, COL_GATE), lambda i: (i, 0)),
                  pl.BlockSpec((bb, 128), lambda i: (i, 0)),
                  pl.BlockSpec((bb, CONV_W - 1, CONV_CH), lambda i: (i, 0, 0)),
                  pl.BlockSpec((bb, HEADS, GDN_D, GDN_D), lambda i: (i, 0, 0, 0)),
                  pl.BlockSpec((bb, hp, 128, 128), lambda i: (i, 0, 0, 0)),
                  pl.BlockSpec((bb, HEADS * ML_DK), lambda i: (i, 0)),
                  pl.BlockSpec((bb, 128), lambda i: (i, 0)),
                  pl.BlockSpec((CONV_W, CONV_CH), lambda i: (0, 0)),
                  pl.BlockSpec((3, 128), lambda i: (0, 0)),
                  pl.BlockSpec((1, GDN_D), lambda i: (0, 0)),
                  pl.BlockSpec((1, ML_DV), lambda i: (0, 0))],
        out_specs=[pl.BlockSpec((bb, HEADS * GDN_D), lambda i: (i, 0)),
                   pl.BlockSpec((bb, HEADS * ML_DV), lambda i: (i, 0)),
                   pl.BlockSpec((bb, HEADS, GDN_D, GDN_D), lambda i: (i, 0, 0, 0)),
                   pl.BlockSpec((bb, hp, 128, 128), lambda i: (i, 0, 0, 0)),
                   pl.BlockSpec((bb, HEADS * ML_DK), lambda i: (i, 0)),
                   pl.BlockSpec((bb, 128), lambda i: (i, 0))],
        out_shape=[jax.ShapeDtypeStruct((b, HEADS * GDN_D), BF16),
                   jax.ShapeDtypeStruct((b, HEADS * ML_DV), BF16),
                   jax.ShapeDtypeStruct((b, HEADS, GDN_D, GDN_D), F32),
                   jax.ShapeDtypeStruct((b, hp, 128, 128), F32),
                   jax.ShapeDtypeStruct((b, HEADS * ML_DK), F32),
                   jax.ShapeDtypeStruct((b, 128), F32)],
        scratch_shapes=[pltpu.VMEM((HEADS, bb, 128), F32), pltpu.VMEM((HEADS, bb, 128), F32)],
        compiler_params=pltpu.CompilerParams(dimension_semantics=("arbitrary",),
                                             vmem_limit_bytes=VMEM_LIMIT),
        name="decode_step",
    )(proj, gates, cs, s0, c0, n0, m0, conv_w, plane, gng, gnm)


def _lane_vec(pairs):
    v = jnp.zeros((128,), F32)
    for off, val in pairs:
        v = v.at[off:off + HEADS].set(val.astype(F32))
    return v


def _prep_params(norm_pre_mix, w_in, conv_w, a_log, dt_bias, gdn_norm_g, b_igate, b_fgate, mlstm_norm_g,
                 w_out, norm_post_mix, norm_pre_mlp, w_up, w_down, norm_post_mlp):
    alog_v = _lane_vec([(LANE_GA, a_log[0])])
    dtb_v = _lane_vec([(LANE_GA, dt_bias[0])])
    bias_v = _lane_vec([(LANE_MI, b_igate[0]), (LANE_MF, b_fgate[0])])
    return dict(
        w_in=jnp.swapaxes(w_in[0], 0, 1), wo=w_out[0].astype(BF16), wu=w_up[0].astype(BF16),
        wd=w_down[0].astype(BF16),
        g_pre=norm_pre_mix[0][None, :], g1=norm_post_mix[0][None, :], g2=norm_pre_mlp[0][None, :],
        g3=norm_post_mlp[0][None, :], cw=conv_w[0], gng=gdn_norm_g[0][None, :], gnm=mlstm_norm_g[0][None, :],
        plane=jnp.stack([alog_v, dtb_v, bias_v]),
        psub=jnp.stack([alog_v[:16], dtb_v[:16], bias_v[:16]], axis=1),
    )


def _prompt_path(x, conv0, s0, c0, n0, m0, prm, w_in_r, tb, tm):
    bsz, seq, d = x.shape
    hp = HEADS // 2
    x2d = x.reshape(bsz * seq, d)
    proj, gates, gates_t = _in_proj(x2d, prm["g_pre"], w_in_r, tm=tm, out_dtype=BF16)
    proj = proj.reshape(bsz, seq, COL_GATE)
    gates = gates.reshape(bsz, seq, 128)
    gates_t = gates_t.reshape(bsz, seq // CHUNK, 16, CHUNK)
    cs8 = jnp.pad(conv0, ((0, 0), (8 - (CONV_W - 1), 0), (0, 0)))
    y, s_new, c_new, n_new, m_new = _layer_prompt(
        proj, gates, gates_t, cs8, s0, c0.reshape(bsz, hp, 128, 128), n0.reshape(bsz, hp, 1, 128),
        jnp.pad(m0, ((0, 0), (0, 128 - HEADS)))[:, None, :], x2d, prm["cw"], prm["plane"], prm["psub"],
        prm["gng"], prm["gnm"], prm["wo"], prm["g1"], prm["g2"], prm["wu"], prm["wd"], prm["g3"], tb=tb)
    y = y.reshape(bsz, seq, d)
    xp = jnp.concatenate([conv0, proj[:, seq - (CONV_W - 1):, COL_QKV:COL_QKV + CONV_CH].astype(F32)], axis=1)
    conv_new = xp[:, -(CONV_W - 1):]
    return y, (conv_new, s_new, c_new.reshape(bsz, HEADS, ML_DK, ML_DV), n_new.reshape(bsz, HEADS, ML_DK),
               m_new[:, 0, :HEADS])


def _sample_path(x, conv0, s0, c0, n0, m0, prm):
    dec, _, d = x.shape
    hp = HEADS // 2
    xs = x.reshape(dec, d)
    proj, gates, _, w_in_r = _in_proj_regroup(xs, prm["g_pre"], prm["w_in"])
    og, om, s_new, c_new, n_new, m_new = _decode(
        proj, gates, conv0, s0, c0.reshape(dec, hp, 128, 128), n0.reshape(dec, HEADS * ML_DK),
        jnp.pad(m0, ((0, 0), (0, 128 - HEADS))), prm["cw"], prm["plane"], prm["gng"], prm["gnm"], bb=16)
    y = _out_mlp(og, om, xs, prm["wo"], prm["g1"], prm["g2"], prm["wu"], prm["wd"], prm["g3"],
                 tm=dec).reshape(dec, 1, d)
    conv_new = jnp.concatenate([conv0[:, 1:, :], proj[:, None, COL_QKV:COL_QKV + CONV_CH]], axis=1)
    return y, (conv_new, s_new, c_new.reshape(dec, HEADS, ML_DK, ML_DV), n_new.reshape(dec, HEADS, ML_DK),
               m_new[:, :HEADS]), w_in_r


def kernel(x_prompt, x_sample, state_gdn_conv, state_gdn_S, state_mlstm_C, state_mlstm_n, state_mlstm_m,
           norm_pre_mix, w_in, conv_w, a_log, dt_bias, gdn_norm_g, b_igate, b_fgate, mlstm_norm_g, w_out,
           norm_post_mix, norm_pre_mlp, w_up, w_down, norm_post_mlp):
    bsz = x_prompt.shape[0]
    prm = _prep_params(norm_pre_mix, w_in, conv_w, a_log, dt_bias, gdn_norm_g, b_igate, b_fgate, mlstm_norm_g,
                       w_out, norm_post_mix, norm_pre_mlp, w_up, w_down, norm_post_mlp)
    y_s, s_st, w_in_r = _sample_path(x_sample, state_gdn_conv[0], state_gdn_S[0], state_mlstm_C[0],
                                     state_mlstm_n[0], state_mlstm_m[0], prm)
    y_p, p_st = _prompt_path(
        x_prompt, jnp.zeros((bsz, CONV_W - 1, CONV_CH), F32), jnp.zeros((bsz, HEADS, GDN_D, GDN_D), F32),
        jnp.zeros((bsz, HEADS, ML_DK, ML_DV), F32), jnp.zeros((bsz, HEADS, ML_DK), F32),
        jnp.zeros((bsz, HEADS), F32), prm, w_in_r, tb=256, tm=1024)
    return (y_p, y_s) + tuple(a[None] for a in p_st) + tuple(a[None] for a in s_st)
```

```python
import functools

import jax
import jax.numpy as jnp
from jax import lax
from jax.experimental import pallas as pl
from jax.experimental.pallas import tpu as pltpu

F32 = jnp.float32
BF16 = jnp.bfloat16
EPS = 1e-6

HEADS = 4
GDN_D = 128
ML_DK = 64
ML_DV = 128
CONV_W = 4
CONV_CH = 3 * HEADS * GDN_D
CHUNK = 64

COL_QKV = 0
COL_Z = 1536
COL_MV = 2048
COL_MO = 2560
COL_MQ = 3072
COL_MK = 3328
COL_GATE = 3584
N_PROJ = COL_GATE + 128
LANE_GB, LANE_GA, LANE_MI, LANE_MF = 0, 4, 8, 12

V7X_VMEM_BYTES = 64 * 1024 * 1024
VMEM_LIMIT = V7X_VMEM_BYTES - 8 * 1024 * 1024


def _rms(x, g):
    return x * lax.rsqrt(jnp.mean(x * x, axis=-1, keepdims=True) + EPS) * g


def _softplus(x):
    return jnp.maximum(x, 0.0) + jnp.log1p(jnp.exp(-jnp.abs(x)))


def _sigmoid(x):
    return 1.0 / (1.0 + jnp.exp(-x))


def _mm(a, b):
    return jnp.dot(a.astype(BF16), b.astype(BF16), preferred_element_type=F32)


def _mm_nt(a, b):
    return lax.dot_general(a.astype(BF16), b.astype(BF16), (((1,), (1,)), ((), ())),
                           preferred_element_type=F32)


W_IN_COLS = 3600
W_IN_SPANS = ((0, COL_QKV, 2048),
              (2568, COL_MV, 1024),
              (2056, COL_MQ, 512),
              (2048, COL_GATE, 8),
              (3592, COL_GATE + 8, 8))


def _project_rows(x_ref, g_ref, w_ref, o_ref, gc_ref, gt_ref, h_ref, n_chunk):
    h_ref[...] = _rms(x_ref[...], g_ref[...]).astype(BF16)
    n = o_ref.shape[1]
    for c0 in range(0, n, n_chunk):
        c1 = min(c0 + n_chunk, n)
        o_ref[:, c0:c1] = _mm_nt(h_ref[...], w_ref[c0:c1, :]).astype(o_ref.dtype)
    gates = _mm_nt(h_ref[...], w_ref[COL_GATE:COL_GATE + 128, :])
    gc_ref[...] = gates
    gt = gates.T
    for c in range(gt_ref.shape[0]):
        gt_ref[c] = gt[0:16, c * CHUNK:(c + 1) * CHUNK]


def _in_proj_regroup_kernel(x_ref, g_ref, win_ref, o_ref, gc_ref, gt_ref, w_ref, h_ref, *, n_chunk):
    w_ref[COL_GATE:N_PROJ, :] = jnp.zeros((N_PROJ - COL_GATE, w_ref.shape[1]), BF16)
    for src, dst, width in W_IN_SPANS:
        for r0 in range(0, width, 512):
            r1 = min(r0 + 512, width)
            w_ref[dst + r0:dst + r1, :] = win_ref[src + r0:src + r1, :].astype(BF16)
    _project_rows(x_ref, g_ref, w_ref, o_ref, gc_ref, gt_ref, h_ref, n_chunk)


def _in_proj_kernel(x_ref, g_ref, w_ref, o_ref, gc_ref, gt_ref, h_ref, *, n_chunk):
    _project_rows(x_ref, g_ref, w_ref, o_ref, gc_ref, gt_ref, h_ref, n_chunk)


def _in_proj_regroup(x2d, g, w_t):
    m, k = x2d.shape
    full = lambda i: (0, 0)
    return pl.pallas_call(
        functools.partial(_in_proj_regroup_kernel, n_chunk=512),
        grid=(1,),
        in_specs=[pl.BlockSpec((m, k), full), pl.BlockSpec((1, k), full), pl.BlockSpec((W_IN_COLS, k), full)],
        out_specs=[pl.BlockSpec((m, COL_GATE), full), pl.BlockSpec((m, 128), full),
                   pl.BlockSpec((m // CHUNK, 16, CHUNK), lambda i: (0, 0, 0)), pl.BlockSpec((N_PROJ, k), full)],
        out_shape=[jax.ShapeDtypeStruct((m, COL_GATE), F32),
                   jax.ShapeDtypeStruct((m, 128), F32),
                   jax.ShapeDtypeStruct((m // CHUNK, 16, CHUNK), F32),
                   jax.ShapeDtypeStruct((N_PROJ, k), BF16)],
        scratch_shapes=[pltpu.VMEM((m, k), BF16)],
        compiler_params=pltpu.CompilerParams(dimension_semantics=("arbitrary",),
                                             vmem_limit_bytes=VMEM_LIMIT),
        name="in_proj_regroup",
    )(x2d, g, w_t)


def _in_proj(x2d, g, w, tm, out_dtype):
    m, k = x2d.shape
    return pl.pallas_call(
        functools.partial(_in_proj_kernel, n_chunk=512),
        grid=(m // tm,),
        in_specs=[pl.BlockSpec((tm, k), lambda i: (i, 0)),
                  pl.BlockSpec((1, k), lambda i: (0, 0)),
                  pl.BlockSpec((N_PROJ, k), lambda i: (0, 0), pipeline_mode=pl.Buffered(1))],
        out_specs=[pl.BlockSpec((tm, COL_GATE), lambda i: (i, 0)),
                   pl.BlockSpec((tm, 128), lambda i: (i, 0)),
                   pl.BlockSpec((tm // CHUNK, 16, CHUNK), lambda i: (i, 0, 0))],
        out_shape=[jax.ShapeDtypeStruct((m, COL_GATE), out_dtype),
                   jax.ShapeDtypeStruct((m, 128), F32),
                   jax.ShapeDtypeStruct((m // CHUNK, 16, CHUNK), F32)],
        scratch_shapes=[pltpu.VMEM((tm, k), BF16)],
        compiler_params=pltpu.CompilerParams(dimension_semantics=("arbitrary",),
                                             vmem_limit_bytes=VMEM_LIMIT),
        name="in_proj",
    )(x2d, g, w)


def _out_mlp_kernel(og_ref, om_ref, x_ref, wo_ref, g1_ref, g2_ref, wu_ref, wd_ref, g3_ref,
                    y_ref, hn_ref, acc_ref, *, ff_chunk):
    half = og_ref.shape[1]
    mix = (jnp.dot(og_ref[...], wo_ref[0:half, :], preferred_element_type=F32)
           + jnp.dot(om_ref[...], wo_ref[half:2 * half, :], preferred_element_type=F32))
    x1 = x_ref[...] + _rms(mix, g1_ref[...])
    y_ref[...] = x1
    hn_ref[...] = _rms(x1, g2_ref[...]).astype(BF16)
    d_ff = wu_ref.shape[1]
    for c0 in range(0, d_ff, ff_chunk):
        u = jnp.dot(hn_ref[...], wu_ref[:, c0:c0 + ff_chunk], preferred_element_type=F32)
        u = jnp.square(jnp.maximum(u, 0.0)).astype(BF16)
        d = jnp.dot(u, wd_ref[c0:c0 + ff_chunk, :], preferred_element_type=F32)
        if c0 == 0:
            acc_ref[...] = d
        else:
            acc_ref[...] += d
    y_ref[...] = y_ref[...] + _rms(acc_ref[...], g3_ref[...])


def _out_mlp(og, om, x2d, wo, g1, g2, wu, wd, g3, tm):
    m, d = x2d.shape
    half = og.shape[1]
    d_ff = wu.shape[1]
    const = lambda i: (0, 0)
    return pl.pallas_call(
        functools.partial(_out_mlp_kernel, ff_chunk=1024),
        grid=(m // tm,),
        in_specs=[pl.BlockSpec((tm, half), lambda i: (i, 0)),
                  pl.BlockSpec((tm, half), lambda i: (i, 0)),
                  pl.BlockSpec((tm, d), lambda i: (i, 0)),
                  pl.BlockSpec((d, d), const, pipeline_mode=pl.Buffered(1)),
                  pl.BlockSpec((1, d), const),
                  pl.BlockSpec((1, d), const),
                  pl.BlockSpec((d, d_ff), const, pipeline_mode=pl.Buffered(1)),
                  pl.BlockSpec((d_ff, d), const, pipeline_mode=pl.Buffered(1)),
                  pl.BlockSpec((1, d), const)],
        out_specs=pl.BlockSpec((tm, d), lambda i: (i, 0)),
        out_shape=jax.ShapeDtypeStruct((m, d), F32),
        scratch_shapes=[pltpu.VMEM((tm, d), BF16), pltpu.VMEM((tm, d), F32)],
        compiler_params=pltpu.CompilerParams(dimension_semantics=("arbitrary",),
                                             vmem_limit_bytes=VMEM_LIMIT),
        name="out_mlp",
    )(og, om, x2d, wo, g1, g2, wu, wd, g3)


def _chunk_masks():
    ii = lax.broadcasted_iota(jnp.int32, (CHUNK, CHUNK), 0)
    jj = lax.broadcasted_iota(jnp.int32, (CHUNK, CHUNK), 1)
    return ii, jj


def _cumsum_col_row(x_col, x_row, ii, jj):
    c_col = jnp.sum(jnp.where(jj <= ii, x_row, 0.0), axis=1, keepdims=True)
    c_row = jnp.sum(jnp.where(ii <= jj, x_col, 0.0), axis=0, keepdims=True)
    return c_col, c_row


def _interleave(*tasks):
    live = [[g, n, 0] for g, n in tasks]
    while live:
        entry = min(live, key=lambda e: (e[2] + 1) / e[1])
        try:
            next(entry[0])
            entry[2] += 1
        except StopIteration:
            live.remove(entry)


def _halves(items):
    mid = len(items) // 2
    return items[:mid], items[mid:]


def _gdn_phase_a(yc_ref, gcol_ref, grow_ref, plane_ref, psub_ref, wv_ref, lhs_ref, qk_ref, ket_ref, ge_ref,
                 ncb):
    ii, jj = _chunk_masks()
    incl = jj <= ii
    strict = jj < ii
    eye = (ii == jj).astype(F32)
    neg_a_lane = -jnp.exp(plane_ref[0:1, :])
    dtb_lane = plane_ref[1:2, :]
    neg_a_sub = -jnp.exp(psub_ref[:, 0:1])
    dtb_sub = psub_ref[:, 1:2]
    items = []
    for c in range(ncb):
        rows = slice(c * CHUNK, (c + 1) * CHUNK)
        gc = gcol_ref[rows, :]
        gr = grow_ref[c]
        beta_t = _sigmoid(gc)
        g_t = neg_a_lane * _softplus(gc + dtb_lane)
        g_r = neg_a_sub * _softplus(gr + dtb_sub)
        for h in range(HEADS):
            lo = h * GDN_D
            items.append(dict(c=c, h=h,
                              q=yc_ref[rows, lo:lo + GDN_D],
                              k=yc_ref[rows, 512 + lo:512 + lo + GDN_D],
                              v=yc_ref[rows, 1024 + lo:1024 + lo + GDN_D],
                              beta=beta_t[:, LANE_GB + h:LANE_GB + h + 1],
                              gg_col=g_t[:, LANE_GA + h:LANE_GA + h + 1],
                              gg_row=g_r[LANE_GA + h:LANE_GA + h + 1, :]))
    yield
    for it in items:
        it["k"] = it["k"] * lax.rsqrt(jnp.sum(it["k"] * it["k"], axis=-1, keepdims=True) + EPS)
    yield
    for it in items:
        it["kk"] = _mm_nt(it["k"], it["k"])
    yield
    for it in items:
        it["q"] = (it["q"] * lax.rsqrt(jnp.sum(it["q"] * it["q"], axis=-1, keepdims=True) + EPS)
                   * (GDN_D ** -0.5))
    yield
    for it in items:
        it["g_col"], g_row = _cumsum_col_row(it["gg_col"], it["gg_row"], ii, jj)
        it["decay"] = jnp.where(incl, jnp.exp(jnp.where(incl, it["g_col"] - g_row, 0.0)), 0.0)
    yield
    for it in items:
        n_mat = jnp.where(strict, it["beta"] * it["kk"] * it["decay"], 0.0)
        it["x"] = eye - n_mat
        it["p"] = -n_mat
    yield
    for _ in range(5):
        yield
        for it in items:
            it["p"] = _mm(it["p"], it["p"])
        yield
        for it in items:
            it["x"] = it["x"] + _mm(it["x"], it["p"])
    yield
    for it in items:
        e_g = jnp.exp(it["g_col"])
        it["e_g"] = e_g
        rhs = jnp.concatenate([it["beta"] * it["v"], (it["beta"] * e_g) * it["k"]], axis=1)
        it["w"] = _mm(it["x"], rhs)
    yield
    for it in items:
        it["qk"] = _mm_nt(it["q"], it["k"]) * it["decay"]
    yield
    for it in items:
        c, h = it["c"], it["h"]
        g_end = it["g_col"][CHUNK - 1:CHUNK, :]
        k_end = it["k"] * jnp.exp(g_end - it["g_col"])
        wv_ref[c, h] = it["w"][:, 0:GDN_D]
        lhs_ref[c, h] = jnp.concatenate([it["w"][:, GDN_D:2 * GDN_D], it["e_g"] * it["q"]],
                                        axis=0).astype(BF16)
        qk_ref[c, h] = it["qk"].astype(BF16)
        ket_ref[c, h] = k_end.T.astype(BF16)
        ge_ref[c, h] = jnp.broadcast_to(jnp.exp(g_end), (1, GDN_D))


def _gdn_phase_b(qkvz_ref, gn_ref, og_ref, s_ref, wv_ref, lhs_ref, qk_ref, ket_ref, ge_ref, ncb):
    gn = gn_ref[...]

    def epilogue(c, o):
        rows = slice(c * CHUNK, (c + 1) * CHUNK)
        for h in range(HEADS):
            lo = h * GDN_D
            z = qkvz_ref[rows, COL_Z + lo:COL_Z + lo + GDN_D].astype(F32)
            out = _rms(o[h], gn) * (z * _sigmoid(z))
            og_ref[rows, lo:lo + GDN_D] = out.astype(og_ref.dtype)

    s = [s_ref[h] for h in range(HEADS)]
    o_prev = None
    for c in range(ncb):
        r = [jnp.dot(lhs_ref[c, h], s[h].astype(BF16), preferred_element_type=F32) for h in range(HEADS)]
        yield
        if o_prev is not None:
            epilogue(c - 1, o_prev)
        ub = [(wv_ref[c, h] - r[h][0:CHUNK]).astype(BF16) for h in range(HEADS)]
        s = [ge_ref[c, h] * s[h] + jnp.dot(ket_ref[c, h], ub[h], preferred_element_type=F32)
             for h in range(HEADS)]
        o_prev = [r[h][CHUNK:2 * CHUNK] + jnp.dot(qk_ref[c, h], ub[h], preferred_element_type=F32)
                  for h in range(HEADS)]
        yield
    epilogue(ncb - 1, o_prev)
    for h in range(HEADS):
        s_ref[h] = s[h]


def _mlstm_block(mvo_ref, mqk_ref, gcol_ref, grow_ref, plane_ref, psub_ref, gn_ref, om_ref,
                 c_ref, n_ref, m_ref, ncb):
    ii, jj = _chunk_masks()
    incl = jj <= ii
    lane = lax.broadcasted_iota(jnp.int32, (1, 128), 1)
    row128 = lax.broadcasted_iota(jnp.int32, (128, 1), 0)
    gn = gn_ref[...]
    blane = plane_ref[2:3, :]
    bsub = psub_ref[:, 2:3]
    items = []
    for c in range(ncb):
        rows = slice(c * CHUNK, (c + 1) * CHUNK)
        gc = gcol_ref[rows, :] + blane
        gr = grow_ref[c] + bsub
        lf_c = -_softplus(-gc)
        lf_r = -_softplus(-gr)
        for p in range(HEADS // 2):
            qb = mqk_ref[rows, p * 128:(p + 1) * 128].astype(F32)
            kb = mqk_ref[rows, 256 + p * 128:256 + (p + 1) * 128].astype(F32) * (ML_DK ** -0.5)
            for e in range(2):
                h = 2 * p + e
                lm = (lane >= e * ML_DK) & (lane < (e + 1) * ML_DK)
                items.append(dict(c=c, rows=rows, p=p, e=e, h=h, qh=jnp.where(lm, qb, 0.0), kb=kb,
                                  kh=jnp.where(lm, kb, 0.0),
                                  lf_col=lf_c[:, LANE_MF + h:LANE_MF + h + 1],
                                  lf_row=lf_r[LANE_MF + h:LANE_MF + h + 1, :],
                                  ig_row=gr[LANE_MI + h:LANE_MI + h + 1, :],
                                  ig_col=gc[:, LANE_MI + h:LANE_MI + h + 1],
                                  v=mvo_ref[rows, h * ML_DV:(h + 1) * ML_DV]))
        if c % 2 == 1:
            yield
    for part in _halves(items):
        for it in part:
            it["qk"] = _mm_nt(it["qh"], it["kb"])
        yield
    for part in _halves(items):
        for it in part:
            it["f_col"], it["f_row"] = _cumsum_col_row(it["lf_col"], it["lf_row"], ii, jj)
        yield
    for part in _halves(items):
        for it in part:
            it["d_mat"] = jnp.where(incl, it["f_col"] - it["f_row"] + it["ig_row"], -jnp.inf)
            it["d_max"] = jnp.max(it["d_mat"], axis=1, keepdims=True)
        yield
    for part in _halves(items):
        for it in part:
            p0 = jnp.where(incl, jnp.exp(jnp.where(incl, it["d_mat"] - it["d_max"], 0.0)), 0.0)
            pend0 = jnp.exp(it["f_col"][CHUNK - 1:CHUNK, :] - it["f_col"] + it["ig_col"]
                            - it["d_max"][CHUNK - 1:CHUNK, :])
            it["kp0"] = it["kh"] * pend0
            it["pqk0"] = p0 * it["qk"]
        yield
    for part in _halves(items):
        for it in part:
            it["pv0"] = _mm(it["pqk0"], it["v"])
            it["rs0"] = jnp.sum(it["pqk0"], axis=-1, keepdims=True)
        yield
    for part in _halves(items):
        for it in part:
            it["cadd0"] = _mm(it["kp0"].T, it["v"])
            it["nadd0"] = jnp.sum(it["kp0"], axis=0, keepdims=True)
        yield

    m_cur = [m_ref[:, h:h + 1] for h in range(HEADS)]
    for it in items:
        h = it["h"]
        it["m_prev"] = m_cur[h]
        m_cur[h] = jnp.maximum(it["f_col"][CHUNK - 1:CHUNK, :] + m_cur[h], it["d_max"][CHUNK - 1:CHUNK, :])
    yield
    for part in _halves(items):
        for it in part:
            bcol = it["f_col"] + it["m_prev"]
            mt = jnp.maximum(bcol, it["d_max"])
            it["mt"] = mt
            it["w_prev"] = jnp.exp(bcol - mt)
            it["sc"] = jnp.exp(it["d_max"] - mt)
        yield
    c_cur = [c_ref[p] for p in range(HEADS // 2)]
    n_cur = [n_ref[p] for p in range(HEADS // 2)]
    for c in range(ncb):
        for p in range(HEADS // 2):
            pair = [it for it in items if it["c"] == c and it["p"] == p]
            w_end = [it["w_prev"][CHUNK - 1:CHUNK, :] for it in pair]
            s_end = [it["sc"][CHUNK - 1:CHUNK, :] for it in pair]
            for it in pair:
                it["c_prev"] = c_cur[p]
                it["n_prev"] = n_cur[p]
            c_cur[p] = (jnp.where(row128 < ML_DK, w_end[0], w_end[1]) * c_cur[p]
                        + s_end[0] * pair[0]["cadd0"] + s_end[1] * pair[1]["cadd0"])
            n_cur[p] = (jnp.where(lane < ML_DK, w_end[0], w_end[1]) * n_cur[p]
                        + s_end[0] * pair[0]["nadd0"] + s_end[1] * pair[1]["nadd0"])
        if c % 2 == 1:
            yield
    for part in _halves(items):
        for it in part:
            it["qc"] = _mm(it["qh"], it["c_prev"])
        yield
    for part in _halves(items):
        for it in part:
            num = it["w_prev"] * it["qc"] + it["sc"] * it["pv0"]
            den = (it["w_prev"] * jnp.sum(it["qh"] * it["n_prev"], axis=-1, keepdims=True)
                   + it["sc"] * it["rs0"])
            it["hh"] = num / jnp.maximum(jnp.abs(den), jnp.exp(-it["mt"]))
        yield
    for part in _halves(items):
        for it in part:
            h = it["h"]
            mo = mvo_ref[it["rows"], 512 + h * ML_DV:512 + (h + 1) * ML_DV].astype(F32)
            out = _sigmoid(mo) * _rms(it["hh"], gn)
            om_ref[it["rows"], h * ML_DV:(h + 1) * ML_DV] = out.astype(om_ref.dtype)
        yield
    for p in range(HEADS // 2):
        c_ref[p] = c_cur[p]
        n_ref[p] = n_cur[p]
    for h in range(HEADS):
        m_ref[:, h:h + 1] = m_cur[h]


def _chain(*gens):
    for g in gens:
        yield from g


def _conv_stage(xp_ref, yc_ref, cw_ref, tb):
    for ct in range(CONV_CH // 128):
        cols = slice(ct * 128, (ct + 1) * 128)
        w = cw_ref[:, cols]
        for r0 in range(0, tb, 128):
            acc = xp_ref[8 + r0:8 + r0 + 128, cols] * w[CONV_W - 1:CONV_W, :]
            for j in range(CONV_W - 1):
                acc = acc + xp_ref[5 + j + r0:5 + j + r0 + 128, cols] * w[j:j + 1, :]
            yc_ref[r0:r0 + 128, cols] = acc * _sigmoid(acc)
        yield


def _out_mlp_block(og_ref, om_ref, x_ref, wo_ref, g1_ref, g2_ref, wu_ref, wd_ref, g3_ref,
                   y_ref, hn_ref, acc_ref, ff_chunk, n_split):
    half = og_ref.shape[1]
    d = x_ref.shape[1]
    d_ff = wu_ref.shape[1]
    col_groups = [slice(c, c + d // n_split) for c in range(0, d, d // n_split)]
    for cg in col_groups:
        acc_ref[:, cg] = (jnp.dot(og_ref[...], wo_ref[0:half, cg], preferred_element_type=F32)
                          + jnp.dot(om_ref[...], wo_ref[half:2 * half, cg], preferred_element_type=F32))
        yield
    x1 = x_ref[...] + _rms(acc_ref[...], g1_ref[...])
    y_ref[...] = x1
    hn_ref[...] = _rms(x1, g2_ref[...]).astype(BF16)
    yield
    for c0 in range(0, d_ff, ff_chunk):
        u = jnp.dot(hn_ref[...], wu_ref[:, c0:c0 + ff_chunk], preferred_element_type=F32)
        u = jnp.square(jnp.maximum(u, 0.0)).astype(BF16)
        for cg in col_groups:
            dd = jnp.dot(u, wd_ref[c0:c0 + ff_chunk, cg], preferred_element_type=F32)
            if c0 == 0:
                acc_ref[:, cg] = dd
            else:
                acc_ref[:, cg] += dd
        yield
    y_ref[...] = y_ref[...] + _rms(acc_ref[...], g3_ref[...])
    yield


def _layer_kernel(qkvz_ref, mvo_ref, mqk_ref, gcol_ref, grow_ref, cs_ref, s0_ref, c0_ref, n0_ref, m0_ref,
                  x_ref, cw_ref, plane_ref, psub_ref, gng_ref, gnm_ref,
                  wo_ref, g1_ref, g2_ref, wu_ref, wd_ref, g3_ref,
                  y_ref, sout_ref, cout_ref, nout_ref, mout_ref,
                  xp_ref, yc_ref, s_ref, wv_ref, lhs_ref, qk_ref, ket_ref, ge_ref, c_ref, n_ref, m_ref,
                  og_ref, om_ref, hn_ref, acc_ref, *, tb, nt, n_blocks):
    g = pl.program_id(0)
    t = jnp.minimum(g, n_blocks - 1) % nt
    ncb = tb // CHUNK
    par = g % 2

    @pl.when(g == 0)
    def _():
        og_ref[1] = jnp.zeros(og_ref.shape[1:], og_ref.dtype)
        om_ref[1] = jnp.zeros(om_ref.shape[1:], om_ref.dtype)

    @pl.when(t == 0)
    def _():
        xp_ref[0:8, :] = cs_ref[...]
        s_ref[...] = s0_ref[...]
        c_ref[...] = c0_ref[...]
        n_ref[...] = n0_ref[...]
        m_ref[...] = m0_ref[...]

    @pl.when(t > 0)
    def _():
        xp_ref[0:8, :] = xp_ref[tb:tb + 8, :]

    xp_ref[8:tb + 8, :] = qkvz_ref[:, COL_QKV:COL_QKV + CONV_CH].astype(F32)

    og_w, om_w = og_ref.at[par], om_ref.at[par]
    og_r, om_r = og_ref.at[1 - par], om_ref.at[1 - par]
    _interleave(
        (_chain(_conv_stage(xp_ref, yc_ref, cw_ref, tb),
                _gdn_phase_a(yc_ref, gcol_ref, grow_ref, plane_ref, psub_ref, wv_ref, lhs_ref, qk_ref, ket_ref,
                             ge_ref, ncb),
                _gdn_phase_b(qkvz_ref, gng_ref, og_w, s_ref, wv_ref, lhs_ref, qk_ref, ket_ref, ge_ref, ncb)),
         CONV_CH // 128 + 20 + 2 * ncb),
        (_out_mlp_block(og_r, om_r, x_ref, wo_ref, g1_ref, g2_ref, wu_ref, wd_ref, g3_ref, y_ref, hn_ref, acc_ref,
                        ff_chunk=256, n_split=2), 2 + 1 + wu_ref.shape[1] // 256 + 1),
        (_mlstm_block(mvo_ref, mqk_ref, gcol_ref, grow_ref, plane_ref, psub_ref, gnm_ref, om_w,
                      c_ref, n_ref, m_ref, ncb), 21 + ncb))

    @pl.when((t == nt - 1) & (g < n_blocks))
    def _():
        sout_ref[...] = s_ref[...]
        cout_ref[...] = c_ref[...]
        nout_ref[...] = n_ref[...]
        mout_ref[...] = m_ref[...]


def _layer_prompt(proj, gates, gates_t, cs8, s0, c0, n0, m0, x2d, conv_w, plane, psub, gng, gnm,
                  wo, g1, g2, wu, wd, g3, tb):
    b, t, _ = proj.shape
    d = x2d.shape[1]
    d_ff = wu.shape[1]
    nt = t // tb
    n_blocks = b * nt
    ncb = tb // CHUNK
    hp = HEADS // 2

    def blk(g):
        gm = jnp.minimum(g, n_blocks - 1)
        return gm // nt, gm % nt

    def tok3(col):
        return lambda g: blk(g) + (col,)

    per_seq3 = lambda g: (blk(g)[0], 0, 0)
    per_seq4 = lambda g: (blk(g)[0], 0, 0, 0)
    prev_rows = lambda g: (jnp.maximum(g - 1, 0), 0)
    const = lambda g: (0, 0)
    resident = dict(pipeline_mode=pl.Buffered(1))
    return pl.pallas_call(
        functools.partial(_layer_kernel, tb=tb, nt=nt, n_blocks=n_blocks),
        grid=(n_blocks + 1,),
        in_specs=[pl.BlockSpec((None, tb, 2048), tok3(0)),
                  pl.BlockSpec((None, tb, 1024), tok3(COL_MV // 1024)),
                  pl.BlockSpec((None, tb, 512), tok3(COL_MQ // 512)),
                  pl.BlockSpec((None, tb, 128), tok3(0)),
                  pl.BlockSpec((None, ncb, 16, CHUNK), lambda g: blk(g) + (0, 0)),
                  pl.BlockSpec((None, 8, CONV_CH), per_seq3),
                  pl.BlockSpec((None, HEADS, GDN_D, GDN_D), per_seq4),
                  pl.BlockSpec((None, hp, 128, 128), per_seq4),
                  pl.BlockSpec((None, hp, 1, 128), per_seq4),
                  pl.BlockSpec((None, 1, 128), per_seq3),
                  pl.BlockSpec((tb, d), prev_rows),
                  pl.BlockSpec((CONV_W, CONV_CH), const),
                  pl.BlockSpec((3, 128), const),
                  pl.BlockSpec((16, 3), const),
                  pl.BlockSpec((1, GDN_D), const),
                  pl.BlockSpec((1, ML_DV), const),
                  pl.BlockSpec((d, d), const, **resident),
                  pl.BlockSpec((1, d), const),
                  pl.BlockSpec((1, d), const),
                  pl.BlockSpec((d, d_ff), const, **resident),
                  pl.BlockSpec((d_ff, d), const, **resident),
                  pl.BlockSpec((1, d), const)],
        out_specs=[pl.BlockSpec((tb, d), prev_rows),
                   pl.BlockSpec((None, HEADS, GDN_D, GDN_D), per_seq4),
                   pl.BlockSpec((None, hp, 128, 128), per_seq4),
                   pl.BlockSpec((None, hp, 1, 128), per_seq4),
                   pl.BlockSpec((None, 1, 128), per_seq3)],
        out_shape=[jax.ShapeDtypeStruct((b * t, d), F32),
                   jax.ShapeDtypeStruct((b, HEADS, GDN_D, GDN_D), F32),
                   jax.ShapeDtypeStruct((b, hp, 128, 128), F32),
                   jax.ShapeDtypeStruct((b, hp, 1, 128), F32),
                   jax.ShapeDtypeStruct((b, 1, 128), F32)],
        scratch_shapes=[pltpu.VMEM((tb + 8, CONV_CH), F32),
                        pltpu.VMEM((tb, CONV_CH), F32),
                        pltpu.VMEM((HEADS, GDN_D, GDN_D), F32),
                        pltpu.VMEM((ncb, HEADS, CHUNK, GDN_D), F32),
                        pltpu.VMEM((ncb, HEADS, 2 * CHUNK, GDN_D), BF16),
                        pltpu.VMEM((ncb, HEADS, CHUNK, CHUNK), BF16),
                        pltpu.VMEM((ncb, HEADS, GDN_D, CHUNK), BF16),
                        pltpu.VMEM((ncb, HEADS, 1, GDN_D), F32),
                        pltpu.VMEM((hp, 128, 128), F32),
                        pltpu.VMEM((hp, 1, 128), F32),
                        pltpu.VMEM((1, 128), F32),
                        pltpu.VMEM((2, tb, HEADS * GDN_D), BF16),
                        pltpu.VMEM((2, tb, HEADS * ML_DV), BF16),
                        pltpu.VMEM((tb, d), BF16),
                        pltpu.VMEM((tb, d), F32)],
        compiler_params=pltpu.CompilerParams(dimension_semantics=("arbitrary",),
                                             vmem_limit_bytes=VMEM_LIMIT),
        name="layer_prompt",
    )(proj, proj, proj, gates, gates_t, cs8, s0, c0, n0, m0, x2d, conv_w, plane, psub, gng, gnm,
      wo, g1, g2, wu, wd, g3)


def _decode_kernel(proj_ref, gates_ref, cs_ref, s0_ref, c0_ref, n0_ref, m0_ref, cw_ref, plane_ref, gng_ref, gnm_ref,
                   og_ref, om_ref, sout_ref, cout_ref, nout_ref, mout_ref, rowg_ref, rowm_ref, *, bb):
    lane = lax.broadcasted_iota(jnp.int32, (1, 128), 1)
    gt = gates_ref[...]
    beta_t = _sigmoid(gt)
    g_t = -jnp.exp(plane_ref[0:1, :]) * _softplus(gt + plane_ref[1:2, :])
    eg_t = jnp.exp(g_t)
    gb_t = gt + plane_ref[2:3, :]
    lf_t = -_softplus(-gb_t)

    heads = []
    for h in range(HEADS):
        cols = []
        for part in range(3):
            c0 = part * 512 + h * GDN_D
            w = cw_ref[:, c0:c0 + GDN_D]
            acc = cs_ref[:, 0, c0:c0 + GDN_D] * w[0:1, :]
            acc = acc + cs_ref[:, 1, c0:c0 + GDN_D] * w[1:2, :]
            acc = acc + cs_ref[:, 2, c0:c0 + GDN_D] * w[2:3, :]
            acc = acc + proj_ref[:, COL_QKV + c0:COL_QKV + c0 + GDN_D] * w[3:4, :]
            cols.append(acc * _sigmoid(acc))
        q, k, v = cols
        q = q * lax.rsqrt(jnp.sum(q * q, axis=-1, keepdims=True) + EPS) * (GDN_D ** -0.5)
        k = k * lax.rsqrt(jnp.sum(k * k, axis=-1, keepdims=True) + EPS)
        heads.append(dict(v=v, qk=jnp.sum(q * k, axis=-1, keepdims=True), k_t=k.T.astype(BF16),
                          kq=jnp.concatenate([k, q], axis=0).astype(BF16),
                          beta=beta_t[:, LANE_GB + h:LANE_GB + h + 1],
                          eg=eg_t[:, LANE_GA + h:LANE_GA + h + 1]))
    items = [dict(h=h, b=b) for h in range(HEADS) for b in range(bb)]
    row_b = lax.broadcasted_iota(jnp.int32, (bb, 1), 0)
    for it in items:
        r = jnp.dot(heads[it["h"]]["kq"], s0_ref[it["b"], it["h"]].astype(BF16), preferred_element_type=F32)
        it["ks"] = r[it["b"]:it["b"] + 1, :]
        it["qs"] = r[bb + it["b"]:bb + it["b"] + 1, :]
    for it in items:
        hd, b = heads[it["h"]], it["b"]
        eg_b = hd["eg"][b:b + 1, :]
        it["eg_b"] = eg_b
        it["u"] = hd["beta"][b:b + 1, :] * (hd["v"][b:b + 1, :] - eg_b * it["ks"])
        rowg_ref[it["h"], b:b + 1, :] = eg_b * it["qs"] + hd["qk"][b:b + 1, :] * it["u"]
    for it in items:
        u_sel = jnp.where(row_b == it["b"], it["u"], 0.0).astype(BF16)
        upd = jnp.dot(heads[it["h"]]["k_t"], u_sel, preferred_element_type=F32)
        sout_ref[it["b"], it["h"]] = it["eg_b"] * s0_ref[it["b"], it["h"]] + upd
    for h in range(HEADS):
        z = proj_ref[:, COL_Z + h * GDN_D:COL_Z + (h + 1) * GDN_D]
        out = _rms(rowg_ref[h], gng_ref[...]) * (z * _sigmoid(z))
        og_ref[:, h * GDN_D:(h + 1) * GDN_D] = out.astype(og_ref.dtype)

    heads = []
    for p in range(HEADS // 2):
        qb = proj_ref[:, COL_MQ + p * 128:COL_MQ + (p + 1) * 128]
        kb = proj_ref[:, COL_MK + p * 128:COL_MK + (p + 1) * 128] * (ML_DK ** -0.5)
        n_p = n0_ref[:, p * 128:(p + 1) * 128]
        k_t = jnp.concatenate([jnp.where(lane < ML_DK, kb, 0.0), jnp.where(lane >= ML_DK, kb, 0.0)],
                              axis=0).T.astype(BF16)
        qq = jnp.concatenate([jnp.where(lane < ML_DK, qb, 0.0), jnp.where(lane >= ML_DK, qb, 0.0)],
                             axis=0).astype(BF16)
        w_prev, p_in = [], []
        for e in range(2):
            h = 2 * p + e
            lm = (lane >= e * ML_DK) & (lane < (e + 1) * ML_DK)
            ig = gb_t[:, LANE_MI + h:LANE_MI + h + 1]
            lf = lf_t[:, LANE_MF + h:LANE_MF + h + 1]
            m_old = m0_ref[:, h:h + 1]
            m_new = jnp.maximum(lf + m_old, ig)
            w_prev.append(jnp.exp(lf + m_old - m_new))
            p_in.append(jnp.exp(ig - m_new))
            qk = jnp.sum(jnp.where(lm, qb * kb, 0.0), axis=-1, keepdims=True)
            qn = jnp.sum(jnp.where(lm, qb * n_p, 0.0), axis=-1, keepdims=True)
            mout_ref[:, h:h + 1] = m_new
            heads.append(dict(p=p, e=e, qq=qq, k_t=k_t, w_prev=w_prev[e], p_in=p_in[e],
                              pqk=p_in[e] * qk, wqn=w_prev[e] * qn, floor=jnp.exp(-m_new),
                              v=proj_ref[:, COL_MV + h * ML_DV:COL_MV + (h + 1) * ML_DV]))
        lo_lane = lane < ML_DK
        nout_ref[:, p * 128:(p + 1) * 128] = (jnp.where(lo_lane, w_prev[0], w_prev[1]) * n_p
                                              + jnp.where(lo_lane, p_in[0], p_in[1]) * kb)
    items = [dict(h=h, b=b) for h in range(HEADS) for b in range(bb)]
    qc_pair = {(p, b): jnp.dot(heads[2 * p]["qq"], c0_ref[b, p].astype(BF16), preferred_element_type=F32)
               for p in range(HEADS // 2) for b in range(bb)}
    for it in items:
        hd, b = heads[it["h"]], it["b"]
        row = hd["e"] * bb + b
        it["qc"] = qc_pair[(hd["p"], b)][row:row + 1, :]
    for it in items:
        hd, b = heads[it["h"]], it["b"]
        pqk = hd["pqk"][b:b + 1, :]
        num = hd["w_prev"][b:b + 1, :] * it["qc"] + pqk * hd["v"][b:b + 1, :]
        den = hd["wqn"][b:b + 1, :] + pqk
        rowm_ref[it["h"], b:b + 1, :] = num / jnp.maximum(jnp.abs(den), hd["floor"][b:b + 1, :])
    row128 = lax.broadcasted_iota(jnp.int32, (128, 1), 0)
    for p in range(HEADS // 2):
        h0, h1 = heads[2 * p], heads[2 * p + 1]
        pv = jnp.concatenate([h0["p_in"] * h0["v"], h1["p_in"] * h1["v"]], axis=0)
        row_2b = lax.broadcasted_iota(jnp.int32, (2 * bb, 1), 0)
        for b in range(bb):
            pv_sel = jnp.where((row_2b == b) | (row_2b == bb + b), pv, 0.0).astype(BF16)
            upd = jnp.dot(h0["k_t"], pv_sel, preferred_element_type=F32)
            w_col = jnp.where(row128 < ML_DK, h0["w_prev"][b:b + 1, :], h1["w_prev"][b:b + 1, :])
            cout_ref[b, p] = w_col * c0_ref[b, p] + upd
    for h in range(HEADS):
        mo = proj_ref[:, COL_MO + h * ML_DV:COL_MO + (h + 1) * ML_DV]
        out = _sigmoid(mo) * _rms(rowm_ref[h], gnm_ref[...])
        om_ref[:, h * ML_DV:(h + 1) * ML_DV] = out.astype(om_ref.dtype)
    mout_ref[:, HEADS:128] = m0_ref[:, HEADS:128]


def _decode(proj, gates, cs, s0, c0, n0, m0, conv_w, plane, gng, gnm, bb):
    b = proj.shape[0]
    hp = HEADS // 2
    return pl.pallas_call(
        functools.partial(_decode_kernel, bb=bb),
        grid=(b // bb,),
        in_specs=[pl.BlockSpec((bb, COL_GATE), lambda i: (i, 0)),
                  pl.BlockSpec((bb, 128), lambda i: (i, 0)),
                  pl.BlockSpec((bb, CONV_W - 1, CONV_CH), lambda i: (i, 0, 0)),
                  pl.BlockSpec((bb, HEADS, GDN_D, GDN_D), lambda i: (i, 0, 0, 0)),
                  pl.BlockSpec((bb, hp, 128, 128), lambda i: (i, 0, 0, 0)),
                  pl.BlockSpec((bb, HEADS * ML_DK), lambda i: (i, 0)),
                  pl.BlockSpec((bb, 128), lambda i: (i, 0)),
                  pl.BlockSpec((CONV_W, CONV_CH), lambda i: (0, 0)),
                  pl.BlockSpec((3, 128), lambda i: (0, 0)),
                  pl.BlockSpec((1, GDN_D), lambda i: (0, 0)),
                  pl.BlockSpec((1, ML_DV), lambda i: (0, 0))],
        out_specs=[pl.BlockSpec((bb, HEADS * GDN_D), lambda i: (i, 0)),
                   pl.BlockSpec((bb, HEADS * ML_DV), lambda i: (i, 0)),
                   pl.BlockSpec((bb, HEADS, GDN_D, GDN_D), lambda i: (i, 0, 0, 0)),
                   pl.BlockSpec((bb, hp, 128, 128), lambda i: (i, 0, 0, 0)),
                   pl.BlockSpec((bb, HEADS * ML_DK), lambda i: (i, 0)),
                   pl.BlockSpec((bb, 128), lambda i: (i, 0))],
        out_shape=[jax.ShapeDtypeStruct((b, HEADS * GDN_D), BF16),
                   jax.ShapeDtypeStruct((b, HEADS * ML_DV), BF16),
                   jax.ShapeDtypeStruct((b, HEADS, GDN_D, GDN_D), F32),
                   jax.ShapeDtypeStruct((b, hp, 128, 128), F32),
                   jax.ShapeDtypeStruct((b, HEADS * ML_DK), F32),
                   jax.ShapeDtypeStruct((b, 128), F32)],
        scratch_shapes=[pltpu.VMEM((HEADS, bb, 128), F32), pltpu.VMEM((HEADS, bb, 128), F32)],
        compiler_params=pltpu.CompilerParams(dimension_semantics=("arbitrary",),
                                             vmem_limit_bytes=VMEM_LIMIT),
        name="decode_step",
    )(proj, gates, cs, s0, c0, n0, m0, conv_w, plane, gng, gnm)


def _lane_vec(pairs):
    v = jnp.zeros((128,), F32)
    for off, val in pairs:
        v = v.at[off:off + HEADS].set(val.astype(F32))
    return v


def _prep_params(norm_pre_mix, w_in, conv_w, a_log, dt_bias, gdn_norm_g, b_igate, b_fgate, mlstm_norm_g,
                 w_out, norm_post_mix, norm_pre_mlp, w_up, w_down, norm_post_mlp):
    alog_v = _lane_vec([(LANE_GA, a_log[0])])
    dtb_v = _lane_vec([(LANE_GA, dt_bias[0])])
    bias_v = _lane_vec([(LANE_MI, b_igate[0]), (LANE_MF, b_fgate[0])])
    return dict(
        w_in=jnp.swapaxes(w_in[0], 0, 1), wo=w_out[0].astype(BF16), wu=w_up[0].astype(BF16),
        wd=w_down[0].astype(BF16),
        g_pre=norm_pre_mix[0][None, :], g1=norm_post_mix[0][None, :], g2=norm_pre_mlp[0][None, :],
        g3=norm_post_mlp[0][None, :], cw=conv_w[0], gng=gdn_norm_g[0][None, :], gnm=mlstm_norm_g[0][None, :],
        plane=jnp.stack([alog_v, dtb_v, bias_v]),
        psub=jnp.stack([alog_v[:16], dtb_v[:16], bias_v[:16]], axis=1),
    )


def _prompt_path(x, conv0, s0, c0, n0, m0, prm, w_in_r, tb, tm):
    bsz, seq, d = x.shape
    hp = HEADS // 2
    x2d = x.reshape(bsz * seq, d)
    proj, gates, gates_t = _in_proj(x2d, prm["g_pre"], w_in_r, tm=tm, out_dtype=F32)
    proj = proj.reshape(bsz, seq, COL_GATE)
    gates = gates.reshape(bsz, seq, 128)
    gates_t = gates_t.reshape(bsz, seq // CHUNK, 16, CHUNK)
    cs8 = jnp.pad(conv0, ((0, 0), (8 - (CONV_W - 1), 0), (0, 0)))
    y, s_new, c_new, n_new, m_new = _layer_prompt(
        proj, gates, gates_t, cs8, s0, c0.reshape(bsz, hp, 128, 128), n0.reshape(bsz, hp, 1, 128),
        jnp.pad(m0, ((0, 0), (0, 128 - HEADS)))[:, None, :], x2d, prm["cw"], prm["plane"], prm["psub"],
        prm["gng"], prm["gnm"], prm["wo"], prm["g1"], prm["g2"], prm["wu"], prm["wd"], prm["g3"], tb=tb)
    y = y.reshape(bsz, seq, d)
    xp = jnp.concatenate([conv0, proj[:, seq - (CONV_W - 1):, COL_QKV:COL_QKV + CONV_CH].astype(F32)], axis=1)
    conv_new = xp[:, -(CONV_W - 1):]
    return y, (conv_new, s_new, c_new.reshape(bsz, HEADS, ML_DK, ML_DV), n_new.reshape(bsz, HEADS, ML_DK),
               m_new[:, 0, :HEADS])


def _sample_path(x, conv0, s0, c0, n0, m0, prm):
    dec, _, d = x.shape
    hp = HEADS // 2
    xs = x.reshape(dec, d)
    proj, gates, _, w_in_r = _in_proj_regroup(xs, prm["g_pre"], prm["w_in"])
    og, om, s_new, c_new, n_new, m_new = _decode(
        proj, gates, conv0, s0, c0.reshape(dec, hp, 128, 128), n0.reshape(dec, HEADS * ML_DK),
        jnp.pad(m0, ((0, 0), (0, 128 - HEADS))), prm["cw"], prm["plane"], prm["gng"], prm["gnm"], bb=16)
    y = _out_mlp(og, om, xs, prm["wo"], prm["g1"], prm["g2"], prm["wu"], prm["wd"], prm["g3"],
                 tm=dec).reshape(dec, 1, d)
    conv_new = jnp.concatenate([conv0[:, 1:, :], proj[:, None, COL_QKV:COL_QKV + CONV_CH]], axis=1)
    return y, (conv_new, s_new, c_new.reshape(dec, HEADS, ML_DK, ML_DV), n_new.reshape(dec, HEADS, ML_DK),
               m_new[:, :HEADS]), w_in_r


def kernel(x_prompt, x_sample, state_gdn_conv, state_gdn_S, state_mlstm_C, state_mlstm_n, state_mlstm_m,
           norm_pre_mix, w_in, conv_w, a_log, dt_bias, gdn_norm_g, b_igate, b_fgate, mlstm_norm_g, w_out,
           norm_post_mix, norm_pre_mlp, w_up, w_down, norm_post_mlp):
    bsz = x_prompt.shape[0]
    prm = _prep_params(norm_pre_mix, w_in, conv_w, a_log, dt_bias, gdn_norm_g, b_igate, b_fgate, mlstm_norm_g,
                       w_out, norm_post_mix, norm_pre_mlp, w_up, w_down, norm_post_mlp)
    y_s, s_st, w_in_r = _sample_path(x_sample, state_gdn_conv[0], state_gdn_S[0], state_mlstm_C[0],
                                     state_mlstm_n[0], state_mlstm_m[0], prm)
    y_p, p_st = _prompt_path(
        x_prompt, jnp.zeros((bsz, CONV_W - 1, CONV_CH), F32), jnp.zeros((bsz, HEADS, GDN_D, GDN_D), F32),
        jnp.zeros((bsz, HEADS, ML_DK, ML_DV), F32), jnp.zeros((bsz, HEADS, ML_DK), F32),
        jnp.zeros((bsz, HEADS), F32), prm, w_in_r, tb=256, tm=1024)
    return (y_p, y_s) + tuple(a[None] for a in p_st) + tuple(a[None] for a in s_st)
```

```python
import functools

import jax
import jax.numpy as jnp
from jax import lax
from jax.experimental import pallas as pl
from jax.experimental.pallas import tpu as pltpu

F32 = jnp.float32
BF16 = jnp.bfloat16
EPS = 1e-6

HEADS = 4
GDN_D = 128
ML_DK = 64
ML_DV = 128
CONV_W = 4
CONV_CH = 3 * HEADS * GDN_D
CHUNK = 64

COL_QKV = 0
COL_Z = 1536
COL_MV = 2048
COL_MO = 2560
COL_MQ = 3072
COL_MK = 3328
COL_GATE = 3584
N_PROJ = COL_GATE + 128
LANE_GB, LANE_GA, LANE_MI, LANE_MF = 0, 4, 8, 12

V7X_VMEM_BYTES = 64 * 1024 * 1024
VMEM_LIMIT = V7X_VMEM_BYTES - 8 * 1024 * 1024


def _rms(x, g):
    return x * lax.rsqrt(jnp.mean(x * x, axis=-1, keepdims=True) + EPS) * g


def _softplus(x):
    return jnp.maximum(x, 0.0) + jnp.log1p(jnp.exp(-jnp.abs(x)))


def _sigmoid(x):
    return 1.0 / (1.0 + jnp.exp(-x))


def _mm(a, b):
    return jnp.dot(a.astype(BF16), b.astype(BF16), preferred_element_type=F32)


def _mm_nt(a, b):
    return lax.dot_general(a.astype(BF16), b.astype(BF16), (((1,), (1,)), ((), ())),
                           preferred_element_type=F32)


W_IN_COLS = 3600
W_IN_SPANS = ((0, COL_QKV, 2048),
              (2568, COL_MV, 1024),
              (2056, COL_MQ, 512),
              (2048, COL_GATE, 8),
              (3592, COL_GATE + 8, 8))


def _project_rows(x_ref, g_ref, w_ref, o_ref, gc_ref, gt_ref, h_ref, n_chunk):
    h_ref[...] = _rms(x_ref[...], g_ref[...]).astype(BF16)
    n = o_ref.shape[1]
    for c0 in range(0, n, n_chunk):
        c1 = min(c0 + n_chunk, n)
        o_ref[:, c0:c1] = _mm_nt(h_ref[...], w_ref[c0:c1, :]).astype(o_ref.dtype)
    gates = _mm_nt(h_ref[...], w_ref[COL_GATE:COL_GATE + 128, :])
    gc_ref[...] = gates
    gt = gates.T
    for c in range(gt_ref.shape[0]):
        gt_ref[c] = gt[0:16, c * CHUNK:(c + 1) * CHUNK]


def _in_proj_regroup_kernel(x_ref, g_ref, win_ref, o_ref, gc_ref, gt_ref, w_ref, h_ref, *, n_chunk):
    w_ref[COL_GATE:N_PROJ, :] = jnp.zeros((N_PROJ - COL_GATE, w_ref.shape[1]), BF16)
    for src, dst, width in W_IN_SPANS:
        for r0 in range(0, width, 512):
            r1 = min(r0 + 512, width)
            w_ref[dst + r0:dst + r1, :] = win_ref[src + r0:src + r1, :].astype(BF16)
    _project_rows(x_ref, g_ref, w_ref, o_ref, gc_ref, gt_ref, h_ref, n_chunk)


def _in_proj_kernel(x_ref, g_ref, w_ref, o_ref, gc_ref, gt_ref, h_ref, *, n_chunk):
    _project_rows(x_ref, g_ref, w_ref, o_ref, gc_ref, gt_ref, h_ref, n_chunk)


def _in_proj_regroup(x2d, g, w_t):
    m, k = x2d.shape
    full = lambda i: (0, 0)
    return pl.pallas_call(
        functools.partial(_in_proj_regroup_kernel, n_chunk=512),
        grid=(1,),
        in_specs=[pl.BlockSpec((m, k), full), pl.BlockSpec((1, k), full), pl.BlockSpec((W_IN_COLS, k), full)],
        out_specs=[pl.BlockSpec((m, COL_GATE), full), pl.BlockSpec((m, 128), full),
                   pl.BlockSpec((m // CHUNK, 16, CHUNK), lambda i: (0, 0, 0)), pl.BlockSpec((N_PROJ, k), full)],
        out_shape=[jax.ShapeDtypeStruct((m, COL_GATE), F32),
                   jax.ShapeDtypeStruct((m, 128), F32),
                   jax.ShapeDtypeStruct((m // CHUNK, 16, CHUNK), F32),
                   jax.ShapeDtypeStruct((N_PROJ, k), BF16)],
        scratch_shapes=[pltpu.VMEM((m, k), BF16)],
        compiler_params=pltpu.CompilerParams(dimension_semantics=("arbitrary",),
                                             vmem_limit_bytes=VMEM_LIMIT),
        name="in_proj_regroup",
    )(x2d, g, w_t)


def _in_proj(x2d, g, w, tm, out_dtype):
    m, k = x2d.shape
    return pl.pallas_call(
        functools.partial(_in_proj_kernel, n_chunk=512),
        grid=(m // tm,),
        in_specs=[pl.BlockSpec((tm, k), lambda i: (i, 0)),
                  pl.BlockSpec((1, k), lambda i: (0, 0)),
                  pl.BlockSpec((N_PROJ, k), lambda i: (0, 0), pipeline_mode=pl.Buffered(1))],
        out_specs=[pl.BlockSpec((tm, COL_GATE), lambda i: (i, 0)),
                   pl.BlockSpec((tm, 128), lambda i: (i, 0)),
                   pl.BlockSpec((tm // CHUNK, 16, CHUNK), lambda i: (i, 0, 0))],
        out_shape=[jax.ShapeDtypeStruct((m, COL_GATE), out_dtype),
                   jax.ShapeDtypeStruct((m, 128), F32),
                   jax.ShapeDtypeStruct((m // CHUNK, 16, CHUNK), F32)],
        scratch_shapes=[pltpu.VMEM((tm, k), BF16)],
        compiler_params=pltpu.CompilerParams(dimension_semantics=("arbitrary",),
                                             vmem_limit_bytes=VMEM_LIMIT),
        name="in_proj",
    )(x2d, g, w)


def _out_mlp_kernel(og_ref, om_ref, x_ref, wo_ref, g1_ref, g2_ref, wu_ref, wd_ref, g3_ref,
                    y_ref, hn_ref, acc_ref, *, ff_chunk):
    half = og_ref.shape[1]
    mix = (jnp.dot(og_ref[...], wo_ref[0:half, :], preferred_element_type=F32)
           + jnp.dot(om_ref[...], wo_ref[half:2 * half, :], preferred_element_type=F32))
    x1 = x_ref[...] + _rms(mix, g1_ref[...])
    y_ref[...] = x1
    hn_ref[...] = _rms(x1, g2_ref[...]).astype(BF16)
    d_ff = wu_ref.shape[1]
    for c0 in range(0, d_ff, ff_chunk):
        u = jnp.dot(hn_ref[...], wu_ref[:, c0:c0 + ff_chunk], preferred_element_type=F32)
        u = jnp.square(jnp.maximum(u, 0.0)).astype(BF16)
        d = jnp.dot(u, wd_ref[c0:c0 + ff_chunk, :], preferred_element_type=F32)
        if c0 == 0:
            acc_ref[...] = d
        else:
            acc_ref[...] += d
    y_ref[...] = y_ref[...] + _rms(acc_ref[...], g3_ref[...])


def _out_mlp(og, om, x2d, wo, g1, g2, wu, wd, g3, tm):
    m, d = x2d.shape
    half = og.shape[1]
    d_ff = wu.shape[1]
    const = lambda i: (0, 0)
    return pl.pallas_call(
        functools.partial(_out_mlp_kernel, ff_chunk=1024),
        grid=(m // tm,),
        in_specs=[pl.BlockSpec((tm, half), lambda i: (i, 0)),
                  pl.BlockSpec((tm, half), lambda i: (i, 0)),
                  pl.BlockSpec((tm, d), lambda i: (i, 0)),
                  pl.BlockSpec((d, d), const, pipeline_mode=pl.Buffered(1)),
                  pl.BlockSpec((1, d), const),
                  pl.BlockSpec((1, d), const),
                  pl.BlockSpec((d, d_ff), const, pipeline_mode=pl.Buffered(1)),
                  pl.BlockSpec((d_ff, d), const, pipeline_mode=pl.Buffered(1)),
                  pl.BlockSpec((1, d), const)],
        out_specs=pl.BlockSpec((tm, d), lambda i: (i, 0)),
        out_shape=jax.ShapeDtypeStruct((m, d), F32),
        scratch_shapes=[pltpu.VMEM((tm, d), BF16), pltpu.VMEM((tm, d), F32)],
        compiler_params=pltpu.CompilerParams(dimension_semantics=("arbitrary",),
                                             vmem_limit_bytes=VMEM_LIMIT),
        name="out_mlp",
    )(og, om, x2d, wo, g1, g2, wu, wd, g3)


def _chunk_masks():
    ii = lax.broadcasted_iota(jnp.int32, (CHUNK, CHUNK), 0)
    jj = lax.broadcasted_iota(jnp.int32, (CHUNK, CHUNK), 1)
    return ii, jj


def _cumsum_col_row(x_col, x_row, ii, jj):
    c_col = jnp.sum(jnp.where(jj <= ii, x_row, 0.0), axis=1, keepdims=True)
    c_row = jnp.sum(jnp.where(ii <= jj, x_col, 0.0), axis=0, keepdims=True)
    return c_col, c_row


def _interleave(*tasks):
    live = [[g, n, 0] for g, n in tasks]
    while live:
        entry = min(live, key=lambda e: (e[2] + 1) / e[1])
        try:
            next(entry[0])
            entry[2] += 1
        except StopIteration:
            live.remove(entry)


def _halves(items):
    mid = len(items) // 2
    return items[:mid], items[mid:]


def _gdn_phase_a(yc_ref, gcol_ref, grow_ref, plane_ref, psub_ref, wv_ref, lhs_ref, qk_ref, ket_ref, ge_ref,
                 ncb):
    ii, jj = _chunk_masks()
    incl = jj <= ii
    strict = jj < ii
    eye = (ii == jj).astype(F32)
    neg_a_lane = -jnp.exp(plane_ref[0:1, :])
    dtb_lane = plane_ref[1:2, :]
    neg_a_sub = -jnp.exp(psub_ref[:, 0:1])
    dtb_sub = psub_ref[:, 1:2]
    items = []
    for c in range(ncb):
        rows = slice(c * CHUNK, (c + 1) * CHUNK)
        gc = gcol_ref[rows, :]
        gr = grow_ref[c]
        beta_t = _sigmoid(gc)
        g_t = neg_a_lane * _softplus(gc + dtb_lane)
        g_r = neg_a_sub * _softplus(gr + dtb_sub)
        for h in range(HEADS):
            lo = h * GDN_D
            items.append(dict(c=c, h=h,
                              q=yc_ref[rows, lo:lo + GDN_D],
                              k=yc_ref[rows, 512 + lo:512 + lo + GDN_D],
                              v=yc_ref[rows, 1024 + lo:1024 + lo + GDN_D],
                              beta=beta_t[:, LANE_GB + h:LANE_GB + h + 1],
                              gg_col=g_t[:, LANE_GA + h:LANE_GA + h + 1],
                              gg_row=g_r[LANE_GA + h:LANE_GA + h + 1, :]))
    yield
    for it in items:
        it["k"] = it["k"] * lax.rsqrt(jnp.sum(it["k"] * it["k"], axis=-1, keepdims=True) + EPS)
    yield
    for it in items:
        it["kk"] = _mm_nt(it["k"], it["k"])
    yield
    for it in items:
        it["q"] = (it["q"] * lax.rsqrt(jnp.sum(it["q"] * it["q"], axis=-1, keepdims=True) + EPS)
                   * (GDN_D ** -0.5))
    yield
    for it in items:
        it["g_col"], g_row = _cumsum_col_row(it["gg_col"], it["gg_row"], ii, jj)
        it["decay"] = jnp.where(incl, jnp.exp(jnp.where(incl, it["g_col"] - g_row, 0.0)), 0.0)
    yield
    for it in items:
        n_mat = jnp.where(strict, it["beta"] * it["kk"] * it["decay"], 0.0)
        it["x"] = eye - n_mat
        it["p"] = -n_mat
    yield
    for _ in range(5):
        yield
        for it in items:
            it["p"] = _mm(it["p"], it["p"])
        yield
        for it in items:
            it["x"] = it["x"] + _mm(it["x"], it["p"])
    yield
    for it in items:
        e_g = jnp.exp(it["g_col"])
        it["e_g"] = e_g
        rhs = jnp.concatenate([it["beta"] * it["v"], (it["beta"] * e_g) * it["k"]], axis=1)
        it["w"] = _mm(it["x"], rhs)
    yield
    for it in items:
        it["qk"] = _mm_nt(it["q"], it["k"]) * it["decay"]
    yield
    for it in items:
        c, h = it["c"], it["h"]
        g_end = it["g_col"][CHUNK - 1:CHUNK, :]
        k_end = it["k"] * jnp.exp(g_end - it["g_col"])
        wv_ref[c, h] = it["w"][:, 0:GDN_D]
        lhs_ref[c, h] = jnp.concatenate([it["w"][:, GDN_D:2 * GDN_D], it["e_g"] * it["q"]],
                                        axis=0).astype(BF16)
        qk_ref[c, h] = it["qk"].astype(BF16)
        ket_ref[c, h] = k_end.T.astype(BF16)
        ge_ref[c, h] = jnp.broadcast_to(jnp.exp(g_end), (1, GDN_D))


def _gdn_phase_b(qkvz_ref, gn_ref, og_ref, s_ref, wv_ref, lhs_ref, qk_ref, ket_ref, ge_ref, ncb):
    gn = gn_ref[...]

    def epilogue(c, o):
        rows = slice(c * CHUNK, (c + 1) * CHUNK)
        for h in range(HEADS):
            lo = h * GDN_D
            z = qkvz_ref[rows, COL_Z + lo:COL_Z + lo + GDN_D].astype(F32)
            out = _rms(o[h], gn) * (z * _sigmoid(z))
            og_ref[rows, lo:lo + GDN_D] = out.astype(og_ref.dtype)

    s = [s_ref[h] for h in range(HEADS)]
    o_prev = None
    for c in range(ncb):
        r = [jnp.dot(lhs_ref[c, h], s[h].astype(BF16), preferred_element_type=F32) for h in range(HEADS)]
        yield
        if o_prev is not None:
            epilogue(c - 1, o_prev)
        ub = [(wv_ref[c, h] - r[h][0:CHUNK]).astype(BF16) for h in range(HEADS)]
        s = [ge_ref[c, h] * s[h] + jnp.dot(ket_ref[c, h], ub[h], preferred_element_type=F32)
             for h in range(HEADS)]
        o_prev = [r[h][CHUNK:2 * CHUNK] + jnp.dot(qk_ref[c, h], ub[h], preferred_element_type=F32)
                  for h in range(HEADS)]
        yield
    epilogue(ncb - 1, o_prev)
    for h in range(HEADS):
        s_ref[h] = s[h]


def _mlstm_block(mvo_ref, mqk_ref, gcol_ref, grow_ref, plane_ref, psub_ref, gn_ref, om_ref,
                 c_ref, n_ref, m_ref, ncb):
    ii, jj = _chunk_masks()
    incl = jj <= ii
    lane = lax.broadcasted_iota(jnp.int32, (1, 128), 1)
    row128 = lax.broadcasted_iota(jnp.int32, (128, 1), 0)
    gn = gn_ref[...]
    blane = plane_ref[2:3, :]
    bsub = psub_ref[:, 2:3]
    items = []
    for c in range(ncb):
        rows = slice(c * CHUNK, (c + 1) * CHUNK)
        gc = gcol_ref[rows, :] + blane
        gr = grow_ref[c] + bsub
        lf_c = -_softplus(-gc)
        lf_r = -_softplus(-gr)
        for p in range(HEADS // 2):
            qb = mqk_ref[rows, p * 128:(p + 1) * 128].astype(F32)
            kb = mqk_ref[rows, 256 + p * 128:256 + (p + 1) * 128].astype(F32) * (ML_DK ** -0.5)
            for e in range(2):
                h = 2 * p + e
                lm = (lane >= e * ML_DK) & (lane < (e + 1) * ML_DK)
                items.append(dict(c=c, rows=rows, p=p, e=e, h=h, qh=jnp.where(lm, qb, 0.0), kb=kb,
                                  kh=jnp.where(lm, kb, 0.0),
                                  lf_col=lf_c[:, LANE_MF + h:LANE_MF + h + 1],
                                  lf_row=lf_r[LANE_MF + h:LANE_MF + h + 1, :],
                                  ig_row=gr[LANE_MI + h:LANE_MI + h + 1, :],
                                  ig_col=gc[:, LANE_MI + h:LANE_MI + h + 1],
                                  v=mvo_ref[rows, h * ML_DV:(h + 1) * ML_DV]))
        if c % 2 == 1:
            yield
    for part in _halves(items):
        for it in part:
            it["qk"] = _mm_nt(it["qh"], it["kb"])
        yield
    for part in _halves(items):
        for it in part:
            it["f_col"], it["f_row"] = _cumsum_col_row(it["lf_col"], it["lf_row"], ii, jj)
        yield
    for part in _halves(items):
        for it in part:
            it["d_mat"] = jnp.where(incl, it["f_col"] - it["f_row"] + it["ig_row"], -jnp.inf)
            it["d_max"] = jnp.max(it["d_mat"], axis=1, keepdims=True)
        yield
    for part in _halves(items):
        for it in part:
            p0 = jnp.where(incl, jnp.exp(jnp.where(incl, it["d_mat"] - it["d_max"], 0.0)), 0.0)
            pend0 = jnp.exp(it["f_col"][CHUNK - 1:CHUNK, :] - it["f_col"] + it["ig_col"]
                            - it["d_max"][CHUNK - 1:CHUNK, :])
            it["kp0"] = it["kh"] * pend0
            it["pqk0"] = p0 * it["qk"]
        yield
    for part in _halves(items):
        for it in part:
            it["pv0"] = _mm(it["pqk0"], it["v"])
            it["rs0"] = jnp.sum(it["pqk0"], axis=-1, keepdims=True)
        yield
    for part in _halves(items):
        for it in part:
            it["cadd0"] = _mm(it["kp0"].T, it["v"])
            it["nadd0"] = jnp.sum(it["kp0"], axis=0, keepdims=True)
        yield

    m_cur = [m_ref[:, h:h + 1] for h in range(HEADS)]
    for it in items:
        h = it["h"]
        it["m_prev"] = m_cur[h]
        m_cur[h] = jnp.maximum(it["f_col"][CHUNK - 1:CHUNK, :] + m_cur[h], it["d_max"][CHUNK - 1:CHUNK, :])
    yield
    for part in _halves(items):
        for it in part:
            bcol = it["f_col"] + it["m_prev"]
            mt = jnp.maximum(bcol, it["d_max"])
            it["mt"] = mt
            it["w_prev"] = jnp.exp(bcol - mt)
            it["sc"] = jnp.exp(it["d_max"] - mt)
        yield
    c_cur = [c_ref[p] for p in range(HEADS // 2)]
    n_cur = [n_ref[p] for p in range(HEADS // 2)]
    for c in range(ncb):
        for p in range(HEADS // 2):
            pair = [it for it in items if it["c"] == c and it["p"] == p]
            w_end = [it["w_prev"][CHUNK - 1:CHUNK, :] for it in pair]
            s_end = [it["sc"][CHUNK - 1:CHUNK, :] for it in pair]
            for it in pair:
                it["c_prev"] = c_cur[p]
                it["n_prev"] = n_cur[p]
            c_cur[p] = (jnp.where(row128 < ML_DK, w_end[0], w_end[1]) * c_cur[p]
                        + s_end[0] * pair[0]["cadd0"] + s_end[1] * pair[1]["cadd0"])
            n_cur[p] = (jnp.where(lane < ML_DK, w_end[0], w_end[1]) * n_cur[p]
                        + s_end[0] * pair[0]["nadd0"] + s_end[1] * pair[1]["nadd0"])
        if c % 2 == 1:
            yield
    for part in _halves(items):
        for it in part:
            it["qc"] = _mm(it["qh"], it["c_prev"])
        yield
    for part in _halves(items):
        for it in part:
            num = it["w_prev"] * it["qc"] + it["sc"] * it["pv0"]
            den = (it["w_prev"] * jnp.sum(it["qh"] * it["n_prev"], axis=-1, keepdims=True)
                   + it["sc"] * it["rs0"])
            it["hh"] = num / jnp.maximum(jnp.abs(den), jnp.exp(-it["mt"]))
        yield
    for part in _halves(items):
        for it in part:
            h = it["h"]
            mo = mvo_ref[it["rows"], 512 + h * ML_DV:512 + (h + 1) * ML_DV].astype(F32)
            out = _sigmoid(mo) * _rms(it["hh"], gn)
            om_ref[it["rows"], h * ML_DV:(h + 1) * ML_DV] = out.astype(om_ref.dtype)
        yield
    for p in range(HEADS // 2):
        c_ref[p] = c_cur[p]
        n_ref[p] = n_cur[p]
    for h in range(HEADS):
        m_ref[:, h:h + 1] = m_cur[h]


def _chain(*gens):
    for g in gens:
        yield from g


def _conv_stage(xp_ref, yc_ref, cw_ref, tb):
    for ct in range(CONV_CH // 128):
        cols = slice(ct * 128, (ct + 1) * 128)
        w = cw_ref[:, cols]
        for r0 in range(0, tb, 128):
            acc = xp_ref[8 + r0:8 + r0 + 128, cols] * w[CONV_W - 1:CONV_W, :]
            for j in range(CONV_W - 1):
                acc = acc + xp_ref[5 + j + r0:5 + j + r0 + 128, cols] * w[j:j + 1, :]
            yc_ref[r0:r0 + 128, cols] = acc * _sigmoid(acc)
        yield


def _out_mlp_block(og_ref, om_ref, x_ref, wo_ref, g1_ref, g2_ref, wu_ref, wd_ref, g3_ref,
                   y_ref, hn_ref, acc_ref, ff_chunk, n_split):
    half = og_ref.shape[1]
    d = x_ref.shape[1]
    d_ff = wu_ref.shape[1]
    col_groups = [slice(c, c + d // n_split) for c in range(0, d, d // n_split)]
    for cg in col_groups:
        acc_ref[:, cg] = (jnp.dot(og_ref[...], wo_ref[0:half, cg], preferred_element_type=F32)
                          + jnp.dot(om_ref[...], wo_ref[half:2 * half, cg], preferred_element_type=F32))
        yield
    x1 = x_ref[...] + _rms(acc_ref[...], g1_ref[...])
    y_ref[...] = x1
    hn_ref[...] = _rms(x1, g2_ref[...]).astype(BF16)
    yield
    for c0 in range(0, d_ff, ff_chunk):
        u = jnp.dot(hn_ref[...], wu_ref[:, c0:c0 + ff_chunk], preferred_element_type=F32)
        u = jnp.square(jnp.maximum(u, 0.0)).astype(BF16)
        for cg in col_groups:
            dd = jnp.dot(u, wd_ref[c0:c0 + ff_chunk, cg], preferred_element_type=F32)
            if c0 == 0:
                acc_ref[:, cg] = dd
            else:
                acc_ref[:, cg] += dd
        yield
    y_ref[...] = y_ref[...] + _rms(acc_ref[...], g3_ref[...])
    yield


def _layer_kernel(qkvz_ref, mvo_ref, mqk_ref, gcol_ref, grow_ref, cs_ref, s0_ref, c0_ref, n0_ref, m0_ref,
                  x_ref, cw_ref, plane_ref, psub_ref, gng_ref, gnm_ref,
                  wo_ref, g1_ref, g2_ref, wu_ref, wd_ref, g3_ref,
                  y_ref, sout_ref, cout_ref, nout_ref, mout_ref,
                  xp_ref, yc_ref, s_ref, wv_ref, lhs_ref, qk_ref, ket_ref, ge_ref, c_ref, n_ref, m_ref,
                  og_ref, om_ref, hn_ref, acc_ref, *, tb, nt, n_blocks):
    g = pl.program_id(0)
    t = jnp.minimum(g, n_blocks - 1) % nt
    ncb = tb // CHUNK
    par = g % 2

    @pl.when(g == 0)
    def _():
        og_ref[1] = jnp.zeros(og_ref.shape[1:], og_ref.dtype)
        om_ref[1] = jnp.zeros(om_ref.shape[1:], om_ref.dtype)

    @pl.when(t == 0)
    def _():
        xp_ref[0:8, :] = cs_ref[...]
        s_ref[...] = s0_ref[...]
        c_ref[...] = c0_ref[...]
        n_ref[...] = n0_ref[...]
        m_ref[...] = m0_ref[...]

    @pl.when(t > 0)
    def _():
        xp_ref[0:8, :] = xp_ref[tb:tb + 8, :]

    xp_ref[8:tb + 8, :] = qkvz_ref[:, COL_QKV:COL_QKV + CONV_CH].astype(F32)

    og_w, om_w = og_ref.at[par], om_ref.at[par]
    og_r, om_r = og_ref.at[1 - par], om_ref.at[1 - par]
    _interleave(
        (_chain(_conv_stage(xp_ref, yc_ref, cw_ref, tb),
                _gdn_phase_a(yc_ref, gcol_ref, grow_ref, plane_ref, psub_ref, wv_ref, lhs_ref, qk_ref, ket_ref,
                             ge_ref, ncb),
                _gdn_phase_b(qkvz_ref, gng_ref, og_w, s_ref, wv_ref, lhs_ref, qk_ref, ket_ref, ge_ref, ncb)),
         CONV_CH // 128 + 20 + 2 * ncb),
        (_out_mlp_block(og_r, om_r, x_ref, wo_ref, g1_ref, g2_ref, wu_ref, wd_ref, g3_ref, y_ref, hn_ref, acc_ref,
                        ff_chunk=256, n_split=2), 2 + 1 + wu_ref.shape[1] // 256 + 1),
        (_mlstm_block(mvo_ref, mqk_ref, gcol_ref, grow_ref, plane_ref, psub_ref, gnm_ref, om_w,
                      c_ref, n_ref, m_ref, ncb), 21 + ncb))

    @pl.when((t == nt - 1) & (g < n_blocks))
    def _():
        sout_ref[...] = s_ref[...]
        cout_ref[...] = c_ref[...]
        nout_ref[...] = n_ref[...]
        mout_ref[...] = m_ref[...]


def _layer_prompt(proj, gates, gates_t, cs8, s0, c0, n0, m0, x2d, conv_w, plane, psub, gng, gnm,
                  wo, g1, g2, wu, wd, g3, tb):
    b, t, _ = proj.shape
    d = x2d.shape[1]
    d_ff = wu.shape[1]
    nt = t // tb
    n_blocks = b * nt
    ncb = tb // CHUNK
    hp = HEADS // 2

    def blk(g):
        gm = jnp.minimum(g, n_blocks - 1)
        return gm // nt, gm % nt

    def tok3(col):
        return lambda g: blk(g) + (col,)

    per_seq3 = lambda g: (blk(g)[0], 0, 0)
    per_seq4 = lambda g: (blk(g)[0], 0, 0, 0)
    prev_rows = lambda g: (jnp.maximum(g - 1, 0), 0)
    const = lambda g: (0, 0)
    resident = dict(pipeline_mode=pl.Buffered(1))
    return pl.pallas_call(
        functools.partial(_layer_kernel, tb=tb, nt=nt, n_blocks=n_blocks),
        grid=(n_blocks + 1,),
        in_specs=[pl.BlockSpec((None, tb, 2048), tok3(0)),
                  pl.BlockSpec((None, tb, 1024), tok3(COL_MV // 1024)),
                  pl.BlockSpec((None, tb, 512), tok3(COL_MQ // 512)),
                  pl.BlockSpec((None, tb, 128), tok3(0)),
                  pl.BlockSpec((None, ncb, 16, CHUNK), lambda g: blk(g) + (0, 0)),
                  pl.BlockSpec((None, 8, CONV_CH), per_seq3),
                  pl.BlockSpec((None, HEADS, GDN_D, GDN_D), per_seq4),
                  pl.BlockSpec((None, hp, 128, 128), per_seq4),
                  pl.BlockSpec((None, hp, 1, 128), per_seq4),
                  pl.BlockSpec((None, 1, 128), per_seq3),
                  pl.BlockSpec((tb, d), prev_rows),
                  pl.BlockSpec((CONV_W, CONV_CH), const),
                  pl.BlockSpec((3, 128), const),
                  pl.BlockSpec((16, 3), const),
                  pl.BlockSpec((1, GDN_D), const),
                  pl.BlockSpec((1, ML_DV), const),
                  pl.BlockSpec((d, d), const, **resident),
                  pl.BlockSpec((1, d), const),
                  pl.BlockSpec((1, d), const),
                  pl.BlockSpec((d, d_ff), const, **resident),
                  pl.BlockSpec((d_ff, d), const, **resident),
                  pl.BlockSpec((1, d), const)],
        out_specs=[pl.BlockSpec((tb, d), prev_rows),
                   pl.BlockSpec((None, HEADS, GDN_D, GDN_D), per_seq4),
                   pl.BlockSpec((None, hp, 128, 128), per_seq4),
                   pl.BlockSpec((None, hp, 1, 128), per_seq4),
                   pl.BlockSpec((None, 1, 128), per_seq3)],
        out_shape=[jax.ShapeDtypeStruct((b * t, d), F32),
                   jax.ShapeDtypeStruct((b, HEADS, GDN_D, GDN_D), F32),
                   jax.ShapeDtypeStruct((b, hp, 128, 128), F32),
                   jax.ShapeDtypeStruct((b, hp, 1, 128), F32),
                   jax.ShapeDtypeStruct((b, 1, 128), F32)],
        scratch_shapes=[pltpu.VMEM((tb + 8, CONV_CH), F32),
                        pltpu.VMEM((tb, CONV_CH), F32),
                        pltpu.VMEM((HEADS, GDN_D, GDN_D), F32),
                        pltpu.VMEM((ncb, HEADS, CHUNK, GDN_D), F32),
                        pltpu.VMEM((ncb, HEADS, 2 * CHUNK, GDN_D), BF16),
                        pltpu.VMEM((ncb, HEADS, CHUNK, CHUNK), BF16),
                        pltpu.VMEM((ncb, HEADS, GDN_D, CHUNK), BF16),
                        pltpu.VMEM((ncb, HEADS, 1, GDN_D), F32),
                        pltpu.VMEM((hp, 128, 128), F32),
                        pltpu.VMEM((hp, 1, 128), F32),
                        pltpu.VMEM((1, 128), F32),
                        pltpu.VMEM((2, tb, HEADS * GDN_D), BF16),
                        pltpu.VMEM((2, tb, HEADS * ML_DV), BF16),
                        pltpu.VMEM((tb, d), BF16),
                        pltpu.VMEM((tb, d), F32)],
        compiler_params=pltpu.CompilerParams(dimension_semantics=("arbitrary",),
                                             vmem_limit_bytes=VMEM_LIMIT),
        name="layer_prompt",
    )(proj, proj, proj, gates, gates_t, cs8, s0, c0, n0, m0, x2d, conv_w, plane, psub, gng, gnm,
      wo, g1, g2, wu, wd, g3)


RING = 3


def _decode_kernel(proj_ref, gates_ref, cs_ref, s0_hbm, c0_hbm, n0_ref, m0_ref, cw_ref, plane_ref, gng_ref, gnm_ref,
                   og_ref, om_ref, sout_ref, cout_ref, nout_ref, mout_ref, rowg_ref, rowm_ref,
                   sbuf_ref, cbuf_ref, sem_ref, *, bb):
    i = pl.program_id(0)
    n_steps = pl.num_programs(0)

    def state_copies(step, slot):
        rows = pl.ds(step * bb, bb)
        return (pltpu.make_async_copy(s0_hbm.at[rows], sbuf_ref.at[slot], sem_ref.at[slot, 0]),
                pltpu.make_async_copy(c0_hbm.at[rows], cbuf_ref.at[slot], sem_ref.at[slot, 1]))

    @pl.when(i == 0)
    def _():
        for step in range(RING - 1):
            for cp in state_copies(step, step):
                cp.start()

    @pl.when(i + RING - 1 < n_steps)
    def _():
        for cp in state_copies(i + RING - 1, (i + RING - 1) % RING):
            cp.start()

    slot = i % RING
    for cp in state_copies(i, slot):
        cp.wait()
    s0_ref = sbuf_ref.at[slot]
    c0_ref = cbuf_ref.at[slot]

    lane = lax.broadcasted_iota(jnp.int32, (1, 128), 1)
    gt = gates_ref[...]
    beta_t = _sigmoid(gt)
    g_t = -jnp.exp(plane_ref[0:1, :]) * _softplus(gt + plane_ref[1:2, :])
    eg_t = jnp.exp(g_t)
    gb_t = gt + plane_ref[2:3, :]
    lf_t = -_softplus(-gb_t)

    heads = []
    for h in range(HEADS):
        cols = []
        for part in range(3):
            c0 = part * 512 + h * GDN_D
            w = cw_ref[:, c0:c0 + GDN_D]
            acc = cs_ref[:, 0, c0:c0 + GDN_D] * w[0:1, :]
            acc = acc + cs_ref[:, 1, c0:c0 + GDN_D] * w[1:2, :]
            acc = acc + cs_ref[:, 2, c0:c0 + GDN_D] * w[2:3, :]
            acc = acc + proj_ref[:, COL_QKV + c0:COL_QKV + c0 + GDN_D] * w[3:4, :]
            cols.append(acc * _sigmoid(acc))
        q, k, v = cols
        q = q * lax.rsqrt(jnp.sum(q * q, axis=-1, keepdims=True) + EPS) * (GDN_D ** -0.5)
        k = k * lax.rsqrt(jnp.sum(k * k, axis=-1, keepdims=True) + EPS)
        heads.append(dict(v=v, qk=jnp.sum(q * k, axis=-1, keepdims=True), k_t=k.T.astype(BF16),
                          kq=jnp.concatenate([k, q], axis=0).astype(BF16),
                          beta=beta_t[:, LANE_GB + h:LANE_GB + h + 1],
                          eg=eg_t[:, LANE_GA + h:LANE_GA + h + 1]))
    items = [dict(h=h, b=b) for h in range(HEADS) for b in range(bb)]
    row_b = lax.broadcasted_iota(jnp.int32, (bb, 1), 0)
    for it in items:
        r = jnp.dot(heads[it["h"]]["kq"], s0_ref[it["b"], it["h"]].astype(BF16), preferred_element_type=F32)
        it["ks"] = r[it["b"]:it["b"] + 1, :]
        it["qs"] = r[bb + it["b"]:bb + it["b"] + 1, :]
    for it in items:
        hd, b = heads[it["h"]], it["b"]
        eg_b = hd["eg"][b:b + 1, :]
        it["eg_b"] = eg_b
        it["u"] = hd["beta"][b:b + 1, :] * (hd["v"][b:b + 1, :] - eg_b * it["ks"])
        rowg_ref[it["h"], b:b + 1, :] = eg_b * it["qs"] + hd["qk"][b:b + 1, :] * it["u"]
    for it in items:
        u_sel = jnp.where(row_b == it["b"], it["u"], 0.0).astype(BF16)
        upd = jnp.dot(heads[it["h"]]["k_t"], u_sel, preferred_element_type=F32)
        sout_ref[it["b"], it["h"]] = it["eg_b"] * s0_ref[it["b"], it["h"]] + upd
    for h in range(HEADS):
        z = proj_ref[:, COL_Z + h * GDN_D:COL_Z + (h + 1) * GDN_D]
        out = _rms(rowg_ref[h], gng_ref[...]) * (z * _sigmoid(z))
        og_ref[:, h * GDN_D:(h + 1) * GDN_D] = out.astype(og_ref.dtype)

    heads = []
    for p in range(HEADS // 2):
        qb = proj_ref[:, COL_MQ + p * 128:COL_MQ + (p + 1) * 128]
        kb = proj_ref[:, COL_MK + p * 128:COL_MK + (p + 1) * 128] * (ML_DK ** -0.5)
        n_p = n0_ref[:, p * 128:(p + 1) * 128]
        k_t = jnp.concatenate([jnp.where(lane < ML_DK, kb, 0.0), jnp.where(lane >= ML_DK, kb, 0.0)],
                              axis=0).T.astype(BF16)
        qq = jnp.concatenate([jnp.where(lane < ML_DK, qb, 0.0), jnp.where(lane >= ML_DK, qb, 0.0)],
                             axis=0).astype(BF16)
        w_prev, p_in = [], []
        for e in range(2):
            h = 2 * p + e
            lm = (lane >= e * ML_DK) & (lane < (e + 1) * ML_DK)
            ig = gb_t[:, LANE_MI + h:LANE_MI + h + 1]
            lf = lf_t[:, LANE_MF + h:LANE_MF + h + 1]
            m_old = m0_ref[:, h:h + 1]
            m_new = jnp.maximum(lf + m_old, ig)
            w_prev.append(jnp.exp(lf + m_old - m_new))
            p_in.append(jnp.exp(ig - m_new))
            qk = jnp.sum(jnp.where(lm, qb * kb, 0.0), axis=-1, keepdims=True)
            qn = jnp.sum(jnp.where(lm, qb * n_p, 0.0), axis=-1, keepdims=True)
            mout_ref[:, h:h + 1] = m_new
            heads.append(dict(p=p, e=e, qq=qq, k_t=k_t, w_prev=w_prev[e], p_in=p_in[e],
                              pqk=p_in[e] * qk, wqn=w_prev[e] * qn, floor=jnp.exp(-m_new),
                              v=proj_ref[:, COL_MV + h * ML_DV:COL_MV + (h + 1) * ML_DV]))
        lo_lane = lane < ML_DK
        nout_ref[:, p * 128:(p + 1) * 128] = (jnp.where(lo_lane, w_prev[0], w_prev[1]) * n_p
                                              + jnp.where(lo_lane, p_in[0], p_in[1]) * kb)
    items = [dict(h=h, b=b) for h in range(HEADS) for b in range(bb)]
    qc_pair = {(p, b): jnp.dot(heads[2 * p]["qq"], c0_ref[b, p].astype(BF16), preferred_element_type=F32)
               for p in range(HEADS // 2) for b in range(bb)}
    for it in items:
        hd, b = heads[it["h"]], it["b"]
        row = hd["e"] * bb + b
        it["qc"] = qc_pair[(hd["p"], b)][row:row + 1, :]
    for it in items:
        hd, b = heads[it["h"]], it["b"]
        pqk = hd["pqk"][b:b + 1, :]
        num = hd["w_prev"][b:b + 1, :] * it["qc"] + pqk * hd["v"][b:b + 1, :]
        den = hd["wqn"][b:b + 1, :] + pqk
        rowm_ref[it["h"], b:b + 1, :] = num / jnp.maximum(jnp.abs(den), hd["floor"][b:b + 1, :])
    row128 = lax.broadcasted_iota(jnp.int32, (128, 1), 0)
    for p in range(HEADS // 2):
        h0, h1 = heads[2 * p], heads[2 * p + 1]
        pv = jnp.concatenate([h0["p_in"] * h0["v"], h1["p_in"] * h1["v"]], axis=0)
        row_2b = lax.broadcasted_iota(jnp.int32, (2 * bb, 1), 0)
        for b in range(bb):
            pv_sel = jnp.where((row_2b == b) | (row_2b == bb + b), pv, 0.0).astype(BF16)
            upd = jnp.dot(h0["k_t"], pv_sel, preferred_element_type=F32)
            w_col = jnp.where(row128 < ML_DK, h0["w_prev"][b:b + 1, :], h1["w_prev"][b:b + 1, :])
            cout_ref[b, p] = w_col * c0_ref[b, p] + upd
    for h in range(HEADS):
        mo = proj_ref[:, COL_MO + h * ML_DV:COL_MO + (h + 1) * ML_DV]
        out = _sigmoid(mo) * _rms(rowm_ref[h], gnm_ref[...])
        om_ref[:, h * ML_DV:(h + 1) * ML_DV] = out.astype(om_ref.dtype)
    mout_ref[:, HEADS:128] = m0_ref[:, HEADS:128]


def _decode(proj, gates, cs, s0, c0, n0, m0, conv_w, plane, gng, gnm, bb):
    b = proj.shape[0]
    hp = HEADS // 2
    assert b % bb == 0 and b // bb >= RING - 1
    return pl.pallas_call(
        functools.partial(_decode_kernel, bb=bb),
        grid=(b // bb,),
        in_specs=[pl.BlockSpec((bb, COL_GATE), lambda i: (i, 0)),
                  pl.BlockSpec((bb, 128), lambda i: (i, 0)),
                  pl.BlockSpec((bb, CONV_W - 1, CONV_CH), lambda i: (i, 0, 0)),
                  pl.BlockSpec(memory_space=pl.ANY),
                  pl.BlockSpec(memory_space=pl.ANY),
                  pl.BlockSpec((bb, HEADS * ML_DK), lambda i: (i, 0)),
                  pl.BlockSpec((bb, 128), lambda i: (i, 0)),
                  pl.BlockSpec((CONV_W, CONV_CH), lambda i: (0, 0)),
                  pl.BlockSpec((3, 128), lambda i: (0, 0)),
                  pl.BlockSpec((1, GDN_D), lambda i: (0, 0)),
                  pl.BlockSpec((1, ML_DV), lambda i: (0, 0))],
        out_specs=[pl.BlockSpec((bb, HEADS * GDN_D), lambda i: (i, 0)),
                   pl.BlockSpec((bb, HEADS * ML_DV), lambda i: (i, 0)),
                   pl.BlockSpec((bb, HEADS, GDN_D, GDN_D), lambda i: (i, 0, 0, 0)),
                   pl.BlockSpec((bb, hp, 128, 128), lambda i: (i, 0, 0, 0)),
                   pl.BlockSpec((bb, HEADS * ML_DK), lambda i: (i, 0)),
                   pl.BlockSpec((bb, 128), lambda i: (i, 0))],
        out_shape=[jax.ShapeDtypeStruct((b, HEADS * GDN_D), BF16),
                   jax.ShapeDtypeStruct((b, HEADS * ML_DV), BF16),
                   jax.ShapeDtypeStruct((b, HEADS, GDN_D, GDN_D), F32),
                   jax.ShapeDtypeStruct((b, hp, 128, 128), F32),
                   jax.ShapeDtypeStruct((b, HEADS * ML_DK), F32),
                   jax.ShapeDtypeStruct((b, 128), F32)],
        scratch_shapes=[pltpu.VMEM((HEADS, bb, 128), F32), pltpu.VMEM((HEADS, bb, 128), F32),
                        pltpu.VMEM((RING, bb, HEADS, GDN_D, GDN_D), F32),
                        pltpu.VMEM((RING, bb, hp, 128, 128), F32),
                        pltpu.SemaphoreType.DMA((RING, 2))],
        compiler_params=pltpu.CompilerParams(dimension_semantics=("arbitrary",),
                                             vmem_limit_bytes=VMEM_LIMIT),
        name="decode_step",
    )(proj, gates, cs, s0, c0, n0, m0, conv_w, plane, gng, gnm)


def _lane_vec(pairs):
    v = jnp.zeros((128,), F32)
    for off, val in pairs:
        v = v.at[off:off + HEADS].set(val.astype(F32))
    return v


def _prep_params(norm_pre_mix, w_in, conv_w, a_log, dt_bias, gdn_norm_g, b_igate, b_fgate, mlstm_norm_g,
                 w_out, norm_post_mix, norm_pre_mlp, w_up, w_down, norm_post_mlp):
    alog_v = _lane_vec([(LANE_GA, a_log[0])])
    dtb_v = _lane_vec([(LANE_GA, dt_bias[0])])
    bias_v = _lane_vec([(LANE_MI, b_igate[0]), (LANE_MF, b_fgate[0])])
    return dict(
        w_in=jnp.swapaxes(w_in[0], 0, 1), wo=w_out[0].astype(BF16), wu=w_up[0].astype(BF16),
        wd=w_down[0].astype(BF16),
        g_pre=norm_pre_mix[0][None, :], g1=norm_post_mix[0][None, :], g2=norm_pre_mlp[0][None, :],
        g3=norm_post_mlp[0][None, :], cw=conv_w[0], gng=gdn_norm_g[0][None, :], gnm=mlstm_norm_g[0][None, :],
        plane=jnp.stack([alog_v, dtb_v, bias_v]),
        psub=jnp.stack([alog_v[:16], dtb_v[:16], bias_v[:16]], axis=1),
    )


def _prompt_path(x, conv0, s0, c0, n0, m0, prm, w_in_r, tb, tm):
    bsz, seq, d = x.shape
    hp = HEADS // 2
    x2d = x.reshape(bsz * seq, d)
    proj, gates, gates_t = _in_proj(x2d, prm["g_pre"], w_in_r, tm=tm, out_dtype=F32)
    proj = proj.reshape(bsz, seq, COL_GATE)
    gates = gates.reshape(bsz, seq, 128)
    gates_t = gates_t.reshape(bsz, seq // CHUNK, 16, CHUNK)
    cs8 = jnp.pad(conv0, ((0, 0), (8 - (CONV_W - 1), 0), (0, 0)))
    y, s_new, c_new, n_new, m_new = _layer_prompt(
        proj, gates, gates_t, cs8, s0, c0.reshape(bsz, hp, 128, 128), n0.reshape(bsz, hp, 1, 128),
        jnp.pad(m0, ((0, 0), (0, 128 - HEADS)))[:, None, :], x2d, prm["cw"], prm["plane"], prm["psub"],
        prm["gng"], prm["gnm"], prm["wo"], prm["g1"], prm["g2"], prm["wu"], prm["wd"], prm["g3"], tb=tb)
    y = y.reshape(bsz, seq, d)
    xp = jnp.concatenate([conv0, proj[:, seq - (CONV_W - 1):, COL_QKV:COL_QKV + CONV_CH].astype(F32)], axis=1)
    conv_new = xp[:, -(CONV_W - 1):]
    return y, (conv_new, s_new, c_new.reshape(bsz, HEADS, ML_DK, ML_DV), n_new.reshape(bsz, HEADS, ML_DK),
               m_new[:, 0, :HEADS])


def _sample_path(x, conv0, s0, c0, n0, m0, prm):
    dec, _, d = x.shape
    hp = HEADS // 2
    xs = x.reshape(dec, d)
    proj, gates, _, w_in_r = _in_proj_regroup(xs, prm["g_pre"], prm["w_in"])
    og, om, s_new, c_new, n_new, m_new = _decode(
        proj, gates, conv0, s0, c0.reshape(dec, hp, 128, 128), n0.reshape(dec, HEADS * ML_DK),
        jnp.pad(m0, ((0, 0), (0, 128 - HEADS))), prm["cw"], prm["plane"], prm["gng"], prm["gnm"], bb=16)
    y = _out_mlp(og, om, xs, prm["wo"], prm["g1"], prm["g2"], prm["wu"], prm["wd"], prm["g3"],
                 tm=dec).reshape(dec, 1, d)
    conv_new = jnp.concatenate([conv0[:, 1:, :], proj[:, None, COL_QKV:COL_QKV + CONV_CH]], axis=1)
    return y, (conv_new, s_new, c_new.reshape(dec, HEADS, ML_DK, ML_DV), n_new.reshape(dec, HEADS, ML_DK),
               m_new[:, :HEADS]), w_in_r


def kernel(x_prompt, x_sample, state_gdn_conv, state_gdn_S, state_mlstm_C, state_mlstm_n, state_mlstm_m,
           norm_pre_mix, w_in, conv_w, a_log, dt_bias, gdn_norm_g, b_igate, b_fgate, mlstm_norm_g, w_out,
           norm_post_mix, norm_pre_mlp, w_up, w_down, norm_post_mlp):
    bsz = x_prompt.shape[0]
    prm = _prep_params(norm_pre_mix, w_in, conv_w, a_log, dt_bias, gdn_norm_g, b_igate, b_fgate, mlstm_norm_g,
                       w_out, norm_post_mix, norm_pre_mlp, w_up, w_down, norm_post_mlp)
    y_s, s_st, w_in_r = _sample_path(x_sample, state_gdn_conv[0], state_gdn_S[0], state_mlstm_C[0],
                                     state_mlstm_n[0], state_mlstm_m[0], prm)
    y_p, p_st = _prompt_path(
        x_prompt, jnp.zeros((bsz, CONV_W - 1, CONV_CH), F32), jnp.zeros((bsz, HEADS, GDN_D, GDN_D), F32),
        jnp.zeros((bsz, HEADS, ML_DK, ML_DV), F32), jnp.zeros((bsz, HEADS, ML_DK), F32),
        jnp.zeros((bsz, HEADS), F32), prm, w_in_r, tb=256, tm=1024)
    return (y_p, y_s) + tuple(a[None] for a in p_st) + tuple(a[None] for a in s_st)
```

```python
import functools

import jax
import jax.numpy as jnp
from jax import lax
from jax.experimental import pallas as pl
from jax.experimental.pallas import tpu as pltpu

F32 = jnp.float32
BF16 = jnp.bfloat16
EPS = 1e-6

HEADS = 4
GDN_D = 128
ML_DK = 64
ML_DV = 128
CONV_W = 4
CONV_CH = 3 * HEADS * GDN_D
CHUNK = 64

COL_QKV = 0
COL_Z = 1536
COL_MV = 2048
COL_MO = 2560
COL_MQ = 3072
COL_MK = 3328
COL_GATE = 3584
N_PROJ = COL_GATE + 128
LANE_GB, LANE_GA, LANE_MI, LANE_MF = 0, 4, 8, 12

V7X_VMEM_BYTES = 64 * 1024 * 1024
VMEM_LIMIT = V7X_VMEM_BYTES - 8 * 1024 * 1024


def _rms(x, g):
    return x * lax.rsqrt(jnp.mean(x * x, axis=-1, keepdims=True) + EPS) * g


def _softplus(x):
    return jnp.maximum(x, 0.0) + jnp.log1p(jnp.exp(-jnp.abs(x)))


def _sigmoid(x):
    return 1.0 / (1.0 + jnp.exp(-x))


def _mm(a, b):
    return jnp.dot(a.astype(BF16), b.astype(BF16), preferred_element_type=F32)


def _mm_nt(a, b):
    return lax.dot_general(a.astype(BF16), b.astype(BF16), (((1,), (1,)), ((), ())),
                           preferred_element_type=F32)


W_IN_COLS = 3600
W_IN_SPANS = ((0, COL_QKV, 2048),
              (2568, COL_MV, 1024),
              (2056, COL_MQ, 512),
              (2048, COL_GATE, 8),
              (3592, COL_GATE + 8, 8))


def _project_rows(x_ref, g_ref, w_ref, o_ref, gc_ref, gt_ref, h_ref, n_chunk):
    h_ref[...] = _rms(x_ref[...], g_ref[...]).astype(BF16)
    n = o_ref.shape[1]
    for c0 in range(0, n, n_chunk):
        c1 = min(c0 + n_chunk, n)
        o_ref[:, c0:c1] = _mm_nt(h_ref[...], w_ref[c0:c1, :]).astype(o_ref.dtype)
    gates = _mm_nt(h_ref[...], w_ref[COL_GATE:COL_GATE + 128, :])
    gc_ref[...] = gates
    gt = gates.T
    for c in range(gt_ref.shape[0]):
        gt_ref[c] = gt[0:16, c * CHUNK:(c + 1) * CHUNK]


def _in_proj_regroup_kernel(x_ref, g_ref, win_ref, o_ref, gc_ref, gt_ref, w_ref, h_ref, *, n_chunk):
    w_ref[COL_GATE:N_PROJ, :] = jnp.zeros((N_PROJ - COL_GATE, w_ref.shape[1]), BF16)
    for src, dst, width in W_IN_SPANS:
        for r0 in range(0, width, 512):
            r1 = min(r0 + 512, width)
            w_ref[dst + r0:dst + r1, :] = win_ref[src + r0:src + r1, :].astype(BF16)
    _project_rows(x_ref, g_ref, w_ref, o_ref, gc_ref, gt_ref, h_ref, n_chunk)


def _in_proj_kernel(x_ref, g_ref, w_ref, o_ref, gc_ref, gt_ref, h_ref, *, n_chunk):
    _project_rows(x_ref, g_ref, w_ref, o_ref, gc_ref, gt_ref, h_ref, n_chunk)


def _in_proj_regroup(x2d, g, w_t):
    m, k = x2d.shape
    full = lambda i: (0, 0)
    return pl.pallas_call(
        functools.partial(_in_proj_regroup_kernel, n_chunk=512),
        grid=(1,),
        in_specs=[pl.BlockSpec((m, k), full), pl.BlockSpec((1, k), full), pl.BlockSpec((W_IN_COLS, k), full)],
        out_specs=[pl.BlockSpec((m, COL_GATE), full), pl.BlockSpec((m, 128), full),
                   pl.BlockSpec((m // CHUNK, 16, CHUNK), lambda i: (0, 0, 0)), pl.BlockSpec((N_PROJ, k), full)],
        out_shape=[jax.ShapeDtypeStruct((m, COL_GATE), F32),
                   jax.ShapeDtypeStruct((m, 128), F32),
                   jax.ShapeDtypeStruct((m // CHUNK, 16, CHUNK), F32),
                   jax.ShapeDtypeStruct((N_PROJ, k), BF16)],
        scratch_shapes=[pltpu.VMEM((m, k), BF16)],
        compiler_params=pltpu.CompilerParams(dimension_semantics=("arbitrary",),
                                             vmem_limit_bytes=VMEM_LIMIT),
        name="in_proj_regroup",
    )(x2d, g, w_t)


def _in_proj(x2d, g, w, tm, out_dtype):
    m, k = x2d.shape
    return pl.pallas_call(
        functools.partial(_in_proj_kernel, n_chunk=512),
        grid=(m // tm,),
        in_specs=[pl.BlockSpec((tm, k), lambda i: (i, 0)),
                  pl.BlockSpec((1, k), lambda i: (0, 0)),
                  pl.BlockSpec((N_PROJ, k), lambda i: (0, 0), pipeline_mode=pl.Buffered(1))],
        out_specs=[pl.BlockSpec((tm, COL_GATE), lambda i: (i, 0)),
                   pl.BlockSpec((tm, 128), lambda i: (i, 0)),
                   pl.BlockSpec((tm // CHUNK, 16, CHUNK), lambda i: (i, 0, 0))],
        out_shape=[jax.ShapeDtypeStruct((m, COL_GATE), out_dtype),
                   jax.ShapeDtypeStruct((m, 128), F32),
                   jax.ShapeDtypeStruct((m // CHUNK, 16, CHUNK), F32)],
        scratch_shapes=[pltpu.VMEM((tm, k), BF16)],
        compiler_params=pltpu.CompilerParams(dimension_semantics=("arbitrary",),
                                             vmem_limit_bytes=VMEM_LIMIT),
        name="in_proj",
    )(x2d, g, w)


def _out_mlp_kernel(og_ref, om_ref, x_ref, wo_ref, g1_ref, g2_ref, wu_hbm, wd_hbm, g3_ref,
                    y_ref, hn_ref, acc_ref, wub_ref, wdb_ref, sem_ref, *, ff_chunk):
    half = og_ref.shape[1]
    n_chunks = wu_hbm.shape[1] // ff_chunk

    def chunk_copies(c, slot):
        return (pltpu.make_async_copy(wu_hbm.at[:, pl.ds(c * ff_chunk, ff_chunk)], wub_ref.at[slot],
                                      sem_ref.at[slot, 0]),
                pltpu.make_async_copy(wd_hbm.at[pl.ds(c * ff_chunk, ff_chunk), :], wdb_ref.at[slot],
                                      sem_ref.at[slot, 1]))

    for c in range(min(2, n_chunks)):
        for cp in chunk_copies(c, c):
            cp.start()
    mix = (jnp.dot(og_ref[...], wo_ref[0:half, :], preferred_element_type=F32)
           + jnp.dot(om_ref[...], wo_ref[half:2 * half, :], preferred_element_type=F32))
    x1 = x_ref[...] + _rms(mix, g1_ref[...])
    y_ref[...] = x1
    hn_ref[...] = _rms(x1, g2_ref[...]).astype(BF16)
    for c in range(n_chunks):
        slot = c % 2
        for cp in chunk_copies(c, slot):
            cp.wait()
        u = jnp.dot(hn_ref[...], wub_ref[slot], preferred_element_type=F32)
        u = jnp.square(jnp.maximum(u, 0.0)).astype(BF16)
        d = jnp.dot(u, wdb_ref[slot], preferred_element_type=F32)
        if c == 0:
            acc_ref[...] = d
        else:
            acc_ref[...] += d
        if c + 2 < n_chunks:
            for cp in chunk_copies(c + 2, slot):
                cp.start()
    y_ref[...] = y_ref[...] + _rms(acc_ref[...], g3_ref[...])


def _out_mlp(og, om, x2d, wo, g1, g2, wu, wd, g3, ff_chunk):
    m, d = x2d.shape
    half = og.shape[1]
    d_ff = wu.shape[1]
    const = lambda i: (0, 0)
    return pl.pallas_call(
        functools.partial(_out_mlp_kernel, ff_chunk=ff_chunk),
        grid=(1,),
        in_specs=[pl.BlockSpec((m, half), const),
                  pl.BlockSpec((m, half), const),
                  pl.BlockSpec((m, d), const),
                  pl.BlockSpec((d, d), const),
                  pl.BlockSpec((1, d), const),
                  pl.BlockSpec((1, d), const),
                  pl.BlockSpec(memory_space=pl.ANY),
                  pl.BlockSpec(memory_space=pl.ANY),
                  pl.BlockSpec((1, d), const)],
        out_specs=pl.BlockSpec((m, d), const),
        out_shape=jax.ShapeDtypeStruct((m, d), F32),
        scratch_shapes=[pltpu.VMEM((m, d), BF16), pltpu.VMEM((m, d), F32),
                        pltpu.VMEM((2, d, ff_chunk), BF16), pltpu.VMEM((2, ff_chunk, d), BF16),
                        pltpu.SemaphoreType.DMA((2, 2))],
        compiler_params=pltpu.CompilerParams(dimension_semantics=("arbitrary",),
                                             vmem_limit_bytes=VMEM_LIMIT),
        name="out_mlp",
    )(og, om, x2d, wo, g1, g2, wu, wd, g3)


def _chunk_masks():
    ii = lax.broadcasted_iota(jnp.int32, (CHUNK, CHUNK), 0)
    jj = lax.broadcasted_iota(jnp.int32, (CHUNK, CHUNK), 1)
    return ii, jj


def _cumsum_col_row(x_col, x_row, ii, jj):
    c_col = jnp.sum(jnp.where(jj <= ii, x_row, 0.0), axis=1, keepdims=True)
    c_row = jnp.sum(jnp.where(ii <= jj, x_col, 0.0), axis=0, keepdims=True)
    return c_col, c_row


def _interleave(*tasks):
    live = [[g, n, 0] for g, n in tasks]
    while live:
        entry = min(live, key=lambda e: (e[2] + 1) / e[1])
        try:
            next(entry[0])
            entry[2] += 1
        except StopIteration:
            live.remove(entry)


def _halves(items):
    mid = len(items) // 2
    return items[:mid], items[mid:]


def _gdn_phase_a(yc_ref, gcol_ref, grow_ref, plane_ref, psub_ref, wv_ref, lhs_ref, qk_ref, ket_ref, ge_ref,
                 ncb):
    ii, jj = _chunk_masks()
    incl = jj <= ii
    strict = jj < ii
    eye = (ii == jj).astype(F32)
    neg_a_lane = -jnp.exp(plane_ref[0:1, :])
    dtb_lane = plane_ref[1:2, :]
    neg_a_sub = -jnp.exp(psub_ref[:, 0:1])
    dtb_sub = psub_ref[:, 1:2]
    items = []
    for c in range(ncb):
        rows = slice(c * CHUNK, (c + 1) * CHUNK)
        gc = gcol_ref[rows, :]
        gr = grow_ref[c]
        beta_t = _sigmoid(gc)
        g_t = neg_a_lane * _softplus(gc + dtb_lane)
        g_r = neg_a_sub * _softplus(gr + dtb_sub)
        for h in range(HEADS):
            lo = h * GDN_D
            items.append(dict(c=c, h=h,
                              q=yc_ref[rows, lo:lo + GDN_D],
                              k=yc_ref[rows, 512 + lo:512 + lo + GDN_D],
                              v=yc_ref[rows, 1024 + lo:1024 + lo + GDN_D],
                              beta=beta_t[:, LANE_GB + h:LANE_GB + h + 1],
                              gg_col=g_t[:, LANE_GA + h:LANE_GA + h + 1],
                              gg_row=g_r[LANE_GA + h:LANE_GA + h + 1, :]))
    yield
    for it in items:
        it["k"] = it["k"] * lax.rsqrt(jnp.sum(it["k"] * it["k"], axis=-1, keepdims=True) + EPS)
    yield
    for it in items:
        it["kk"] = _mm_nt(it["k"], it["k"])
    yield
    for it in items:
        it["q"] = (it["q"] * lax.rsqrt(jnp.sum(it["q"] * it["q"], axis=-1, keepdims=True) + EPS)
                   * (GDN_D ** -0.5))
    yield
    for it in items:
        it["g_col"], g_row = _cumsum_col_row(it["gg_col"], it["gg_row"], ii, jj)
        it["decay"] = jnp.where(incl, jnp.exp(jnp.where(incl, it["g_col"] - g_row, 0.0)), 0.0)
    yield
    for it in items:
        n_mat = jnp.where(strict, it["beta"] * it["kk"] * it["decay"], 0.0)
        it["x"] = eye - n_mat
        it["p"] = -n_mat
    yield
    for _ in range(5):
        yield
        for it in items:
            it["p"] = _mm(it["p"], it["p"])
        yield
        for it in items:
            it["x"] = it["x"] + _mm(it["x"], it["p"])
    yield
    for it in items:
        e_g = jnp.exp(it["g_col"])
        it["e_g"] = e_g
        rhs = jnp.concatenate([it["beta"] * it["v"], (it["beta"] * e_g) * it["k"]], axis=1)
        it["w"] = _mm(it["x"], rhs)
    yield
    for it in items:
        it["qk"] = _mm_nt(it["q"], it["k"]) * it["decay"]
    yield
    for it in items:
        c, h = it["c"], it["h"]
        g_end = it["g_col"][CHUNK - 1:CHUNK, :]
        k_end = it["k"] * jnp.exp(g_end - it["g_col"])
        wv_ref[c, h] = it["w"][:, 0:GDN_D]
        lhs_ref[c, h] = jnp.concatenate([it["w"][:, GDN_D:2 * GDN_D], it["e_g"] * it["q"]],
                                        axis=0).astype(BF16)
        qk_ref[c, h] = it["qk"].astype(BF16)
        ket_ref[c, h] = k_end.T.astype(BF16)
        ge_ref[c, h] = jnp.broadcast_to(jnp.exp(g_end), (1, GDN_D))


def _gdn_phase_b(qkvz_ref, gn_ref, og_ref, s_ref, wv_ref, lhs_ref, qk_ref, ket_ref, ge_ref, ncb):
    gn = gn_ref[...]

    def epilogue(c, o):
        rows = slice(c * CHUNK, (c + 1) * CHUNK)
        for h in range(HEADS):
            lo = h * GDN_D
            z = qkvz_ref[rows, COL_Z + lo:COL_Z + lo + GDN_D].astype(F32)
            out = _rms(o[h], gn) * (z * _sigmoid(z))
            og_ref[rows, lo:lo + GDN_D] = out.astype(og_ref.dtype)

    s = [s_ref[h] for h in range(HEADS)]
    o_prev = None
    for c in range(ncb):
        r = [jnp.dot(lhs_ref[c, h], s[h].astype(BF16), preferred_element_type=F32) for h in range(HEADS)]
        yield
        if o_prev is not None:
            epilogue(c - 1, o_prev)
        ub = [(wv_ref[c, h] - r[h][0:CHUNK]).astype(BF16) for h in range(HEADS)]
        s = [ge_ref[c, h] * s[h] + jnp.dot(ket_ref[c, h], ub[h], preferred_element_type=F32)
             for h in range(HEADS)]
        o_prev = [r[h][CHUNK:2 * CHUNK] + jnp.dot(qk_ref[c, h], ub[h], preferred_element_type=F32)
                  for h in range(HEADS)]
        yield
    epilogue(ncb - 1, o_prev)
    for h in range(HEADS):
        s_ref[h] = s[h]


def _mlstm_block(mvo_ref, mqk_ref, gcol_ref, grow_ref, plane_ref, psub_ref, gn_ref, om_ref,
                 c_ref, n_ref, m_ref, ncb):
    ii, jj = _chunk_masks()
    incl = jj <= ii
    lane = lax.broadcasted_iota(jnp.int32, (1, 128), 1)
    row128 = lax.broadcasted_iota(jnp.int32, (128, 1), 0)
    gn = gn_ref[...]
    blane = plane_ref[2:3, :]
    bsub = psub_ref[:, 2:3]
    items = []
    for c in range(ncb):
        rows = slice(c * CHUNK, (c + 1) * CHUNK)
        gc = gcol_ref[rows, :] + blane
        gr = grow_ref[c] + bsub
        lf_c = -_softplus(-gc)
        lf_r = -_softplus(-gr)
        for p in range(HEADS // 2):
            qb = mqk_ref[rows, p * 128:(p + 1) * 128].astype(F32)
            kb = mqk_ref[rows, 256 + p * 128:256 + (p + 1) * 128].astype(F32) * (ML_DK ** -0.5)
            for e in range(2):
                h = 2 * p + e
                lm = (lane >= e * ML_DK) & (lane < (e + 1) * ML_DK)
                items.append(dict(c=c, rows=rows, p=p, e=e, h=h, qh=jnp.where(lm, qb, 0.0), kb=kb,
                                  kh=jnp.where(lm, kb, 0.0),
                                  lf_col=lf_c[:, LANE_MF + h:LANE_MF + h + 1],
                                  lf_row=lf_r[LANE_MF + h:LANE_MF + h + 1, :],
                                  ig_row=gr[LANE_MI + h:LANE_MI + h + 1, :],
                                  ig_col=gc[:, LANE_MI + h:LANE_MI + h + 1],
                                  v=mvo_ref[rows, h * ML_DV:(h + 1) * ML_DV]))
        if c % 2 == 1:
            yield
    for part in _halves(items):
        for it in part:
            it["qk"] = _mm_nt(it["qh"], it["kb"])
        yield
    for part in _halves(items):
        for it in part:
            it["f_col"], it["f_row"] = _cumsum_col_row(it["lf_col"], it["lf_row"], ii, jj)
        yield
    for part in _halves(items):
        for it in part:
            it["d_mat"] = jnp.where(incl, it["f_col"] - it["f_row"] + it["ig_row"], -jnp.inf)
            it["d_max"] = jnp.max(it["d_mat"], axis=1, keepdims=True)
        yield
    for part in _halves(items):
        for it in part:
            p0 = jnp.where(incl, jnp.exp(jnp.where(incl, it["d_mat"] - it["d_max"], 0.0)), 0.0)
            pend0 = jnp.exp(it["f_col"][CHUNK - 1:CHUNK, :] - it["f_col"] + it["ig_col"]
                            - it["d_max"][CHUNK - 1:CHUNK, :])
            it["kp0"] = it["kh"] * pend0
            it["pqk0"] = p0 * it["qk"]
        yield
    for part in _halves(items):
        for it in part:
            it["pv0"] = _mm(it["pqk0"], it["v"])
            it["rs0"] = jnp.sum(it["pqk0"], axis=-1, keepdims=True)
        yield
    for part in _halves(items):
        for it in part:
            it["cadd0"] = _mm(it["kp0"].T, it["v"])
            it["nadd0"] = jnp.sum(it["kp0"], axis=0, keepdims=True)
        yield

    m_cur = [m_ref[:, h:h + 1] for h in range(HEADS)]
    for it in items:
        h = it["h"]
        it["m_prev"] = m_cur[h]
        m_cur[h] = jnp.maximum(it["f_col"][CHUNK - 1:CHUNK, :] + m_cur[h], it["d_max"][CHUNK - 1:CHUNK, :])
    yield
    for part in _halves(items):
        for it in part:
            bcol = it["f_col"] + it["m_prev"]
            mt = jnp.maximum(bcol, it["d_max"])
            it["mt"] = mt
            it["w_prev"] = jnp.exp(bcol - mt)
            it["sc"] = jnp.exp(it["d_max"] - mt)
        yield
    c_cur = [c_ref[p] for p in range(HEADS // 2)]
    n_cur = [n_ref[p] for p in range(HEADS // 2)]
    for c in range(ncb):
        for p in range(HEADS // 2):
            pair = [it for it in items if it["c"] == c and it["p"] == p]
            w_end = [it["w_prev"][CHUNK - 1:CHUNK, :] for it in pair]
            s_end = [it["sc"][CHUNK - 1:CHUNK, :] for it in pair]
            for it in pair:
                it["c_prev"] = c_cur[p]
                it["n_prev"] = n_cur[p]
            c_cur[p] = (jnp.where(row128 < ML_DK, w_end[0], w_end[1]) * c_cur[p]
                        + s_end[0] * pair[0]["cadd0"] + s_end[1] * pair[1]["cadd0"])
            n_cur[p] = (jnp.where(lane < ML_DK, w_end[0], w_end[1]) * n_cur[p]
                        + s_end[0] * pair[0]["nadd0"] + s_end[1] * pair[1]["nadd0"])
        if c % 2 == 1:
            yield
    for part in _halves(items):
        for it in part:
            it["qc"] = _mm(it["qh"], it["c_prev"])
        yield
    for part in _halves(items):
        for it in part:
            num = it["w_prev"] * it["qc"] + it["sc"] * it["pv0"]
            den = (it["w_prev"] * jnp.sum(it["qh"] * it["n_prev"], axis=-1, keepdims=True)
                   + it["sc"] * it["rs0"])
            it["hh"] = num / jnp.maximum(jnp.abs(den), jnp.exp(-it["mt"]))
        yield
    for part in _halves(items):
        for it in part:
            h = it["h"]
            mo = mvo_ref[it["rows"], 512 + h * ML_DV:512 + (h + 1) * ML_DV].astype(F32)
            out = _sigmoid(mo) * _rms(it["hh"], gn)
            om_ref[it["rows"], h * ML_DV:(h + 1) * ML_DV] = out.astype(om_ref.dtype)
        yield
    for p in range(HEADS // 2):
        c_ref[p] = c_cur[p]
        n_ref[p] = n_cur[p]
    for h in range(HEADS):
        m_ref[:, h:h + 1] = m_cur[h]


def _chain(*gens):
    for g in gens:
        yield from g


def _conv_stage(xp_ref, yc_ref, cw_ref, tb):
    for ct in range(CONV_CH // 128):
        cols = slice(ct * 128, (ct + 1) * 128)
        w = cw_ref[:, cols]
        for r0 in range(0, tb, 128):
            acc = xp_ref[8 + r0:8 + r0 + 128, cols] * w[CONV_W - 1:CONV_W, :]
            for j in range(CONV_W - 1):
                acc = acc + xp_ref[5 + j + r0:5 + j + r0 + 128, cols] * w[j:j + 1, :]
            yc_ref[r0:r0 + 128, cols] = acc * _sigmoid(acc)
        yield


def _out_mlp_block(og_ref, om_ref, x_ref, wo_ref, g1_ref, g2_ref, wu_ref, wd_ref, g3_ref,
                   y_ref, hn_ref, acc_ref, ff_chunk, n_split):
    half = og_ref.shape[1]
    d = x_ref.shape[1]
    d_ff = wu_ref.shape[1]
    col_groups = [slice(c, c + d // n_split) for c in range(0, d, d // n_split)]
    for cg in col_groups:
        acc_ref[:, cg] = (jnp.dot(og_ref[...], wo_ref[0:half, cg], preferred_element_type=F32)
                          + jnp.dot(om_ref[...], wo_ref[half:2 * half, cg], preferred_element_type=F32))
        yield
    x1 = x_ref[...] + _rms(acc_ref[...], g1_ref[...])
    y_ref[...] = x1
    hn_ref[...] = _rms(x1, g2_ref[...]).astype(BF16)
    yield
    for c0 in range(0, d_ff, ff_chunk):
        u = jnp.dot(hn_ref[...], wu_ref[:, c0:c0 + ff_chunk], preferred_element_type=F32)
        u = jnp.square(jnp.maximum(u, 0.0)).astype(BF16)
        for cg in col_groups:
            dd = jnp.dot(u, wd_ref[c0:c0 + ff_chunk, cg], preferred_element_type=F32)
            if c0 == 0:
                acc_ref[:, cg] = dd
            else:
                acc_ref[:, cg] += dd
        yield
    y_ref[...] = y_ref[...] + _rms(acc_ref[...], g3_ref[...])
    yield


def _layer_kernel(qkvz_ref, mvo_ref, mqk_ref, gcol_ref, grow_ref, cs_ref, s0_ref, c0_ref, n0_ref, m0_ref,
                  x_ref, cw_ref, plane_ref, psub_ref, gng_ref, gnm_ref,
                  wo_ref, g1_ref, g2_ref, wu_ref, wd_ref, g3_ref,
                  y_ref, sout_ref, cout_ref, nout_ref, mout_ref,
                  xp_ref, yc_ref, s_ref, wv_ref, lhs_ref, qk_ref, ket_ref, ge_ref, c_ref, n_ref, m_ref,
                  og_ref, om_ref, hn_ref, acc_ref, *, tb, nt, n_blocks):
    g = pl.program_id(0)
    t = jnp.minimum(g, n_blocks - 1) % nt
    ncb = tb // CHUNK
    par = g % 2

    @pl.when(g == 0)
    def _():
        og_ref[1] = jnp.zeros(og_ref.shape[1:], og_ref.dtype)
        om_ref[1] = jnp.zeros(om_ref.shape[1:], om_ref.dtype)

    @pl.when(t == 0)
    def _():
        xp_ref[0:8, :] = cs_ref[...]
        s_ref[...] = s0_ref[...]
        c_ref[...] = c0_ref[...]
        n_ref[...] = n0_ref[...]
        m_ref[...] = m0_ref[...]

    @pl.when(t > 0)
    def _():
        xp_ref[0:8, :] = xp_ref[tb:tb + 8, :]

    xp_ref[8:tb + 8, :] = qkvz_ref[:, COL_QKV:COL_QKV + CONV_CH].astype(F32)

    og_w, om_w = og_ref.at[par], om_ref.at[par]
    og_r, om_r = og_ref.at[1 - par], om_ref.at[1 - par]
    _interleave(
        (_chain(_conv_stage(xp_ref, yc_ref, cw_ref, tb),
                _gdn_phase_a(yc_ref, gcol_ref, grow_ref, plane_ref, psub_ref, wv_ref, lhs_ref, qk_ref, ket_ref,
                             ge_ref, ncb),
                _gdn_phase_b(qkvz_ref, gng_ref, og_w, s_ref, wv_ref, lhs_ref, qk_ref, ket_ref, ge_ref, ncb)),
         CONV_CH // 128 + 20 + 2 * ncb),
        (_out_mlp_block(og_r, om_r, x_ref, wo_ref, g1_ref, g2_ref, wu_ref, wd_ref, g3_ref, y_ref, hn_ref, acc_ref,
                        ff_chunk=256, n_split=2), 2 + 1 + wu_ref.shape[1] // 256 + 1),
        (_mlstm_block(mvo_ref, mqk_ref, gcol_ref, grow_ref, plane_ref, psub_ref, gnm_ref, om_w,
                      c_ref, n_ref, m_ref, ncb), 21 + ncb))

    @pl.when((t == nt - 1) & (g < n_blocks))
    def _():
        sout_ref[...] = s_ref[...]
        cout_ref[...] = c_ref[...]
        nout_ref[...] = n_ref[...]
        mout_ref[...] = m_ref[...]


def _layer_prompt(proj, gates, gates_t, cs8, s0, c0, n0, m0, x2d, conv_w, plane, psub, gng, gnm,
                  wo, g1, g2, wu, wd, g3, tb):
    b, t, _ = proj.shape
    d = x2d.shape[1]
    d_ff = wu.shape[1]
    nt = t // tb
    n_blocks = b * nt
    ncb = tb // CHUNK
    hp = HEADS // 2

    def blk(g):
        gm = jnp.minimum(g, n_blocks - 1)
        return gm // nt, gm % nt

    def tok3(col):
        return lambda g: blk(g) + (col,)

    per_seq3 = lambda g: (blk(g)[0], 0, 0)
    per_seq4 = lambda g: (blk(g)[0], 0, 0, 0)
    prev_rows = lambda g: (jnp.maximum(g - 1, 0), 0)
    const = lambda g: (0, 0)
    resident = dict(pipeline_mode=pl.Buffered(1))
    return pl.pallas_call(
        functools.partial(_layer_kernel, tb=tb, nt=nt, n_blocks=n_blocks),
        grid=(n_blocks + 1,),
        in_specs=[pl.BlockSpec((None, tb, 2048), tok3(0)),
                  pl.BlockSpec((None, tb, 1024), tok3(COL_MV // 1024)),
                  pl.BlockSpec((None, tb, 512), tok3(COL_MQ // 512)),
                  pl.BlockSpec((None, tb, 128), tok3(0)),
                  pl.BlockSpec((None, ncb, 16, CHUNK), lambda g: blk(g) + (0, 0)),
                  pl.BlockSpec((None, 8, CONV_CH), per_seq3),
                  pl.BlockSpec((None, HEADS, GDN_D, GDN_D), per_seq4),
                  pl.BlockSpec((None, hp, 128, 128), per_seq4),
                  pl.BlockSpec((None, hp, 1, 128), per_seq4),
                  pl.BlockSpec((None, 1, 128), per_seq3),
                  pl.BlockSpec((tb, d), prev_rows),
                  pl.BlockSpec((CONV_W, CONV_CH), const),
                  pl.BlockSpec((3, 128), const),
                  pl.BlockSpec((16, 3), const),
                  pl.BlockSpec((1, GDN_D), const),
                  pl.BlockSpec((1, ML_DV), const),
                  pl.BlockSpec((d, d), const, **resident),
                  pl.BlockSpec((1, d), const),
                  pl.BlockSpec((1, d), const),
                  pl.BlockSpec((d, d_ff), const, **resident),
                  pl.BlockSpec((d_ff, d), const, **resident),
                  pl.BlockSpec((1, d), const)],
        out_specs=[pl.BlockSpec((tb, d), prev_rows),
                   pl.BlockSpec((None, HEADS, GDN_D, GDN_D), per_seq4),
                   pl.BlockSpec((None, hp, 128, 128), per_seq4),
                   pl.BlockSpec((None, hp, 1, 128), per_seq4),
                   pl.BlockSpec((None, 1, 128), per_seq3)],
        out_shape=[jax.ShapeDtypeStruct((b * t, d), F32),
                   jax.ShapeDtypeStruct((b, HEADS, GDN_D, GDN_D), F32),
                   jax.ShapeDtypeStruct((b, hp, 128, 128), F32),
                   jax.ShapeDtypeStruct((b, hp, 1, 128), F32),
                   jax.ShapeDtypeStruct((b, 1, 128), F32)],
        scratch_shapes=[pltpu.VMEM((tb + 8, CONV_CH), F32),
                        pltpu.VMEM((tb, CONV_CH), F32),
                        pltpu.VMEM((HEADS, GDN_D, GDN_D), F32),
                        pltpu.VMEM((ncb, HEADS, CHUNK, GDN_D), F32),
                        pltpu.VMEM((ncb, HEADS, 2 * CHUNK, GDN_D), BF16),
                        pltpu.VMEM((ncb, HEADS, CHUNK, CHUNK), BF16),
                        pltpu.VMEM((ncb, HEADS, GDN_D, CHUNK), BF16),
                        pltpu.VMEM((ncb, HEADS, 1, GDN_D), F32),
                        pltpu.VMEM((hp, 128, 128), F32),
                        pltpu.VMEM((hp, 1, 128), F32),
                        pltpu.VMEM((1, 128), F32),
                        pltpu.VMEM((2, tb, HEADS * GDN_D), BF16),
                        pltpu.VMEM((2, tb, HEADS * ML_DV), BF16),
                        pltpu.VMEM((tb, d), BF16),
                        pltpu.VMEM((tb, d), F32)],
        compiler_params=pltpu.CompilerParams(dimension_semantics=("arbitrary",),
                                             vmem_limit_bytes=VMEM_LIMIT),
        name="layer_prompt",
    )(proj, proj, proj, gates, gates_t, cs8, s0, c0, n0, m0, x2d, conv_w, plane, psub, gng, gnm,
      wo, g1, g2, wu, wd, g3)


RING = 3


def _decode_kernel(proj_ref, gates_ref, cs_ref, s0_hbm, c0_hbm, n0_ref, m0_ref, cw_ref, plane_ref, gng_ref, gnm_ref,
                   og_ref, om_ref, sout_ref, cout_ref, nout_ref, mout_ref, rowg_ref, rowm_ref,
                   sbuf_ref, cbuf_ref, sem_ref, *, bb):
    i = pl.program_id(0)
    n_steps = pl.num_programs(0)

    def state_copies(step, slot):
        rows = pl.ds(step * bb, bb)
        return (pltpu.make_async_copy(s0_hbm.at[rows], sbuf_ref.at[slot], sem_ref.at[slot, 0]),
                pltpu.make_async_copy(c0_hbm.at[rows], cbuf_ref.at[slot], sem_ref.at[slot, 1]))

    @pl.when(i == 0)
    def _():
        for step in range(RING - 1):
            for cp in state_copies(step, step):
                cp.start()

    @pl.when(i + RING - 1 < n_steps)
    def _():
        for cp in state_copies(i + RING - 1, (i + RING - 1) % RING):
            cp.start()

    slot = i % RING
    for cp in state_copies(i, slot):
        cp.wait()
    s0_ref = sbuf_ref.at[slot]
    c0_ref = cbuf_ref.at[slot]

    lane = lax.broadcasted_iota(jnp.int32, (1, 128), 1)
    gt = gates_ref[...]
    beta_t = _sigmoid(gt)
    g_t = -jnp.exp(plane_ref[0:1, :]) * _softplus(gt + plane_ref[1:2, :])
    eg_t = jnp.exp(g_t)
    gb_t = gt + plane_ref[2:3, :]
    lf_t = -_softplus(-gb_t)

    heads = []
    for h in range(HEADS):
        cols = []
        for part in range(3):
            c0 = part * 512 + h * GDN_D
            w = cw_ref[:, c0:c0 + GDN_D]
            acc = cs_ref[:, 0, c0:c0 + GDN_D] * w[0:1, :]
            acc = acc + cs_ref[:, 1, c0:c0 + GDN_D] * w[1:2, :]
            acc = acc + cs_ref[:, 2, c0:c0 + GDN_D] * w[2:3, :]
            acc = acc + proj_ref[:, COL_QKV + c0:COL_QKV + c0 + GDN_D] * w[3:4, :]
            cols.append(acc * _sigmoid(acc))
        q, k, v = cols
        q = q * lax.rsqrt(jnp.sum(q * q, axis=-1, keepdims=True) + EPS) * (GDN_D ** -0.5)
        k = k * lax.rsqrt(jnp.sum(k * k, axis=-1, keepdims=True) + EPS)
        heads.append(dict(v=v, qk=jnp.sum(q * k, axis=-1, keepdims=True), k_t=k.T.astype(BF16),
                          kq=jnp.concatenate([k, q], axis=0).astype(BF16),
                          beta=beta_t[:, LANE_GB + h:LANE_GB + h + 1],
                          eg=eg_t[:, LANE_GA + h:LANE_GA + h + 1]))
    items = [dict(h=h, b=b) for h in range(HEADS) for b in range(bb)]
    row_b = lax.broadcasted_iota(jnp.int32, (bb, 1), 0)
    for it in items:
        r = jnp.dot(heads[it["h"]]["kq"], s0_ref[it["b"], it["h"]].astype(BF16), preferred_element_type=F32)
        it["ks"] = r[it["b"]:it["b"] + 1, :]
        it["qs"] = r[bb + it["b"]:bb + it["b"] + 1, :]
    for it in items:
        hd, b = heads[it["h"]], it["b"]
        eg_b = hd["eg"][b:b + 1, :]
        it["eg_b"] = eg_b
        it["u"] = hd["beta"][b:b + 1, :] * (hd["v"][b:b + 1, :] - eg_b * it["ks"])
        rowg_ref[it["h"], b:b + 1, :] = eg_b * it["qs"] + hd["qk"][b:b + 1, :] * it["u"]
    for it in items:
        u_sel = jnp.where(row_b == it["b"], it["u"], 0.0).astype(BF16)
        upd = jnp.dot(heads[it["h"]]["k_t"], u_sel, preferred_element_type=F32)
        sout_ref[it["b"], it["h"]] = it["eg_b"] * s0_ref[it["b"], it["h"]] + upd
    for h in range(HEADS):
        z = proj_ref[:, COL_Z + h * GDN_D:COL_Z + (h + 1) * GDN_D]
        out = _rms(rowg_ref[h], gng_ref[...]) * (z * _sigmoid(z))
        og_ref[:, h * GDN_D:(h + 1) * GDN_D] = out.astype(og_ref.dtype)

    heads = []
    for p in range(HEADS // 2):
        qb = proj_ref[:, COL_MQ + p * 128:COL_MQ + (p + 1) * 128]
        kb = proj_ref[:, COL_MK + p * 128:COL_MK + (p + 1) * 128] * (ML_DK ** -0.5)
        n_p = n0_ref[:, p * 128:(p + 1) * 128]
        k_t = jnp.concatenate([jnp.where(lane < ML_DK, kb, 0.0), jnp.where(lane >= ML_DK, kb, 0.0)],
                              axis=0).T.astype(BF16)
        qq = jnp.concatenate([jnp.where(lane < ML_DK, qb, 0.0), jnp.where(lane >= ML_DK, qb, 0.0)],
                             axis=0).astype(BF16)
        w_prev, p_in = [], []
        for e in range(2):
            h = 2 * p + e
            lm = (lane >= e * ML_DK) & (lane < (e + 1) * ML_DK)
            ig = gb_t[:, LANE_MI + h:LANE_MI + h + 1]
            lf = lf_t[:, LANE_MF + h:LANE_MF + h + 1]
            m_old = m0_ref[:, h:h + 1]
            m_new = jnp.maximum(lf + m_old, ig)
            w_prev.append(jnp.exp(lf + m_old - m_new))
            p_in.append(jnp.exp(ig - m_new))
            qk = jnp.sum(jnp.where(lm, qb * kb, 0.0), axis=-1, keepdims=True)
            qn = jnp.sum(jnp.where(lm, qb * n_p, 0.0), axis=-1, keepdims=True)
            mout_ref[:, h:h + 1] = m_new
            heads.append(dict(p=p, e=e, qq=qq, k_t=k_t, w_prev=w_prev[e], p_in=p_in[e],
                              pqk=p_in[e] * qk, wqn=w_prev[e] * qn, floor=jnp.exp(-m_new),
                              v=proj_ref[:, COL_MV + h * ML_DV:COL_MV + (h + 1) * ML_DV]))
        lo_lane = lane < ML_DK
        nout_ref[:, p * 128:(p + 1) * 128] = (jnp.where(lo_lane, w_prev[0], w_prev[1]) * n_p
                                              + jnp.where(lo_lane, p_in[0], p_in[1]) * kb)
    items = [dict(h=h, b=b) for h in range(HEADS) for b in range(bb)]
    qc_pair = {(p, b): jnp.dot(heads[2 * p]["qq"], c0_ref[b, p].astype(BF16), preferred_element_type=F32)
               for p in range(HEADS // 2) for b in range(bb)}
    for it in items:
        hd, b = heads[it["h"]], it["b"]
        row = hd["e"] * bb + b
        it["qc"] = qc_pair[(hd["p"], b)][row:row + 1, :]
    for it in items:
        hd, b = heads[it["h"]], it["b"]
        pqk = hd["pqk"][b:b + 1, :]
        num = hd["w_prev"][b:b + 1, :] * it["qc"] + pqk * hd["v"][b:b + 1, :]
        den = hd["wqn"][b:b + 1, :] + pqk
        rowm_ref[it["h"], b:b + 1, :] = num / jnp.maximum(jnp.abs(den), hd["floor"][b:b + 1, :])
    row128 = lax.broadcasted_iota(jnp.int32, (128, 1), 0)
    for p in range(HEADS // 2):
        h0, h1 = heads[2 * p], heads[2 * p + 1]
        pv = jnp.concatenate([h0["p_in"] * h0["v"], h1["p_in"] * h1["v"]], axis=0)
        row_2b = lax.broadcasted_iota(jnp.int32, (2 * bb, 1), 0)
        for b in range(bb):
            pv_sel = jnp.where((row_2b == b) | (row_2b == bb + b), pv, 0.0).astype(BF16)
            upd = jnp.dot(h0["k_t"], pv_sel, preferred_element_type=F32)
            w_col = jnp.where(row128 < ML_DK, h0["w_prev"][b:b + 1, :], h1["w_prev"][b:b + 1, :])
            cout_ref[b, p] = w_col * c0_ref[b, p] + upd
    for h in range(HEADS):
        mo = proj_ref[:, COL_MO + h * ML_DV:COL_MO + (h + 1) * ML_DV]
        out = _sigmoid(mo) * _rms(rowm_ref[h], gnm_ref[...])
        om_ref[:, h * ML_DV:(h + 1) * ML_DV] = out.astype(om_ref.dtype)
    mout_ref[:, HEADS:128] = m0_ref[:, HEADS:128]


def _decode(proj, gates, cs, s0, c0, n0, m0, conv_w, plane, gng, gnm, bb):
    b = proj.shape[0]
    hp = HEADS // 2
    assert b % bb == 0 and b // bb >= RING - 1
    return pl.pallas_call(
        functools.partial(_decode_kernel, bb=bb),
        grid=(b // bb,),
        in_specs=[pl.BlockSpec((bb, COL_GATE), lambda i: (i, 0)),
                  pl.BlockSpec((bb, 128), lambda i: (i, 0)),
                  pl.BlockSpec((bb, CONV_W - 1, CONV_CH), lambda i: (i, 0, 0)),
                  pl.BlockSpec(memory_space=pl.ANY),
                  pl.BlockSpec(memory_space=pl.ANY),
                  pl.BlockSpec((bb, HEADS * ML_DK), lambda i: (i, 0)),
                  pl.BlockSpec((bb, 128), lambda i: (i, 0)),
                  pl.BlockSpec((CONV_W, CONV_CH), lambda i: (0, 0)),
                  pl.BlockSpec((3, 128), lambda i: (0, 0)),
                  pl.BlockSpec((1, GDN_D), lambda i: (0, 0)),
                  pl.BlockSpec((1, ML_DV), lambda i: (0, 0))],
        out_specs=[pl.BlockSpec((bb, HEADS * GDN_D), lambda i: (i, 0)),
                   pl.BlockSpec((bb, HEADS * ML_DV), lambda i: (i, 0)),
                   pl.BlockSpec((bb, HEADS, GDN_D, GDN_D), lambda i: (i, 0, 0, 0)),
                   pl.BlockSpec((bb, hp, 128, 128), lambda i: (i, 0, 0, 0)),
                   pl.BlockSpec((bb, HEADS * ML_DK), lambda i: (i, 0)),
                   pl.BlockSpec((bb, 128), lambda i: (i, 0))],
        out_shape=[jax.ShapeDtypeStruct((b, HEADS * GDN_D), BF16),
                   jax.ShapeDtypeStruct((b, HEADS * ML_DV), BF16),
                   jax.ShapeDtypeStruct((b, HEADS, GDN_D, GDN_D), F32),
                   jax.ShapeDtypeStruct((b, hp, 128, 128), F32),
                   jax.ShapeDtypeStruct((b, HEADS * ML_DK), F32),
                   jax.ShapeDtypeStruct((b, 128), F32)],
        scratch_shapes=[pltpu.VMEM((HEADS, bb, 128), F32), pltpu.VMEM((HEADS, bb, 128), F32),
                        pltpu.VMEM((RING, bb, HEADS, GDN_D, GDN_D), F32),
                        pltpu.VMEM((RING, bb, hp, 128, 128), F32),
                        pltpu.SemaphoreType.DMA((RING, 2))],
        compiler_params=pltpu.CompilerParams(dimension_semantics=("arbitrary",),
                                             vmem_limit_bytes=VMEM_LIMIT),
        name="decode_step",
    )(proj, gates, cs, s0, c0, n0, m0, conv_w, plane, gng, gnm)


def _lane_vec(pairs):
    v = jnp.zeros((128,), F32)
    for off, val in pairs:
        v = v.at[off:off + HEADS].set(val.astype(F32))
    return v


def _prep_params(norm_pre_mix, w_in, conv_w, a_log, dt_bias, gdn_norm_g, b_igate, b_fgate, mlstm_norm_g,
                 w_out, norm_post_mix, norm_pre_mlp, w_up, w_down, norm_post_mlp):
    alog_v = _lane_vec([(LANE_GA, a_log[0])])
    dtb_v = _lane_vec([(LANE_GA, dt_bias[0])])
    bias_v = _lane_vec([(LANE_MI, b_igate[0]), (LANE_MF, b_fgate[0])])
    return dict(
        w_in=jnp.swapaxes(w_in[0], 0, 1), wo=w_out[0].astype(BF16), wu=w_up[0].astype(BF16),
        wd=w_down[0].astype(BF16),
        g_pre=norm_pre_mix[0][None, :], g1=norm_post_mix[0][None, :], g2=norm_pre_mlp[0][None, :],
        g3=norm_post_mlp[0][None, :], cw=conv_w[0], gng=gdn_norm_g[0][None, :], gnm=mlstm_norm_g[0][None, :],
        plane=jnp.stack([alog_v, dtb_v, bias_v]),
        psub=jnp.stack([alog_v[:16], dtb_v[:16], bias_v[:16]], axis=1),
    )


def _prompt_path(x, conv0, s0, c0, n0, m0, prm, w_in_r, tb, tm):
    bsz, seq, d = x.shape
    hp = HEADS // 2
    x2d = x.reshape(bsz * seq, d)
    proj, gates, gates_t = _in_proj(x2d, prm["g_pre"], w_in_r, tm=tm, out_dtype=F32)
    proj = proj.reshape(bsz, seq, COL_GATE)
    gates = gates.reshape(bsz, seq, 128)
    gates_t = gates_t.reshape(bsz, seq // CHUNK, 16, CHUNK)
    cs8 = jnp.pad(conv0, ((0, 0), (8 - (CONV_W - 1), 0), (0, 0)))
    y, s_new, c_new, n_new, m_new = _layer_prompt(
        proj, gates, gates_t, cs8, s0, c0.reshape(bsz, hp, 128, 128), n0.reshape(bsz, hp, 1, 128),
        jnp.pad(m0, ((0, 0), (0, 128 - HEADS)))[:, None, :], x2d, prm["cw"], prm["plane"], prm["psub"],
        prm["gng"], prm["gnm"], prm["wo"], prm["g1"], prm["g2"], prm["wu"], prm["wd"], prm["g3"], tb=tb)
    y = y.reshape(bsz, seq, d)
    xp = jnp.concatenate([conv0, proj[:, seq - (CONV_W - 1):, COL_QKV:COL_QKV + CONV_CH].astype(F32)], axis=1)
    conv_new = xp[:, -(CONV_W - 1):]
    return y, (conv_new, s_new, c_new.reshape(bsz, HEADS, ML_DK, ML_DV), n_new.reshape(bsz, HEADS, ML_DK),
               m_new[:, 0, :HEADS])


def _sample_path(x, conv0, s0, c0, n0, m0, prm):
    dec, _, d = x.shape
    hp = HEADS // 2
    xs = x.reshape(dec, d)
    proj, gates, _, w_in_r = _in_proj_regroup(xs, prm["g_pre"], prm["w_in"])
    og, om, s_new, c_new, n_new, m_new = _decode(
        proj, gates, conv0, s0, c0.reshape(dec, hp, 128, 128), n0.reshape(dec, HEADS * ML_DK),
        jnp.pad(m0, ((0, 0), (0, 128 - HEADS))), prm["cw"], prm["plane"], prm["gng"], prm["gnm"], bb=16)
    y = _out_mlp(og, om, xs, prm["wo"], prm["g1"], prm["g2"], prm["wu"], prm["wd"], prm["g3"],
                 ff_chunk=1024).reshape(dec, 1, d)
    conv_new = jnp.concatenate([conv0[:, 1:, :], proj[:, None, COL_QKV:COL_QKV + CONV_CH]], axis=1)
    return y, (conv_new, s_new, c_new.reshape(dec, HEADS, ML_DK, ML_DV), n_new.reshape(dec, HEADS, ML_DK),
               m_new[:, :HEADS]), w_in_r


def kernel(x_prompt, x_sample, state_gdn_conv, state_gdn_S, state_mlstm_C, state_mlstm_n, state_mlstm_m,
           norm_pre_mix, w_in, conv_w, a_log, dt_bias, gdn_norm_g, b_igate, b_fgate, mlstm_norm_g, w_out,
           norm_post_mix, norm_pre_mlp, w_up, w_down, norm_post_mlp):
    bsz = x_prompt.shape[0]
    prm = _prep_params(norm_pre_mix, w_in, conv_w, a_log, dt_bias, gdn_norm_g, b_igate, b_fgate, mlstm_norm_g,
                       w_out, norm_post_mix, norm_pre_mlp, w_up, w_down, norm_post_mlp)
    y_s, s_st, w_in_r = _sample_path(x_sample, state_gdn_conv[0], state_gdn_S[0], state_mlstm_C[0],
                                     state_mlstm_n[0], state_mlstm_m[0], prm)
    y_p, p_st = _prompt_path(
        x_prompt, jnp.zeros((bsz, CONV_W - 1, CONV_CH), F32), jnp.zeros((bsz, HEADS, GDN_D, GDN_D), F32),
        jnp.zeros((bsz, HEADS, ML_DK, ML_DV), F32), jnp.zeros((bsz, HEADS, ML_DK), F32),
        jnp.zeros((bsz, HEADS), F32), prm, w_in_r, tb=256, tm=1024)
    return (y_p, y_s) + tuple(a[None] for a in p_st) + tuple(a[None] for a in s_st)
```

```python
import functools

import jax
import jax.numpy as jnp
from jax import lax
from jax.experimental import pallas as pl
from jax.experimental.pallas import tpu as pltpu

F32 = jnp.float32
BF16 = jnp.bfloat16
EPS = 1e-6

HEADS = 4
GDN_D = 128
ML_DK = 64
ML_DV = 128
CONV_W = 4
CONV_CH = 3 * HEADS * GDN_D
CHUNK = 64

COL_QKV = 0
COL_Z = 1536
COL_MV = 2048
COL_MO = 2560
COL_MQ = 3072
COL_MK = 3328
COL_GATE = 3584
N_PROJ = COL_GATE + 128
LANE_GB, LANE_GA, LANE_MI, LANE_MF = 0, 4, 8, 12

V7X_VMEM_BYTES = 64 * 1024 * 1024
VMEM_LIMIT = V7X_VMEM_BYTES - 8 * 1024 * 1024


def _rms(x, g):
    return x * lax.rsqrt(jnp.mean(x * x, axis=-1, keepdims=True) + EPS) * g


def _softplus(x):
    return jnp.maximum(x, 0.0) + jnp.log1p(jnp.exp(-jnp.abs(x)))


def _sigmoid(x):
    return 1.0 / (1.0 + jnp.exp(-x))


def _mm(a, b):
    return jnp.dot(a.astype(BF16), b.astype(BF16), preferred_element_type=F32)


def _mm_nt(a, b):
    return lax.dot_general(a.astype(BF16), b.astype(BF16), (((1,), (1,)), ((), ())),
                           preferred_element_type=F32)


W_IN_COLS = 3600
W_IN_SPANS = ((0, COL_QKV, 2048),
              (2568, COL_MV, 1024),
              (2056, COL_MQ, 512),
              (2048, COL_GATE, 8),
              (3592, COL_GATE + 8, 8))


def _project_rows(x_ref, g_ref, w_ref, o_ref, gc_ref, gt_ref, h_ref, n_chunk):
    h_ref[...] = _rms(x_ref[...], g_ref[...]).astype(BF16)
    n = o_ref.shape[1]
    for c0 in range(0, n, n_chunk):
        c1 = min(c0 + n_chunk, n)
        o_ref[:, c0:c1] = _mm_nt(h_ref[...], w_ref[c0:c1, :]).astype(o_ref.dtype)
    gates = _mm_nt(h_ref[...], w_ref[COL_GATE:COL_GATE + 128, :])
    gc_ref[...] = gates
    gt = gates.T
    for c in range(gt_ref.shape[0]):
        gt_ref[c] = gt[0:16, c * CHUNK:(c + 1) * CHUNK]


def _in_proj_regroup_kernel(x_ref, g_ref, win_ref, o_ref, gc_ref, gt_ref, w_ref, h_ref, *, n_chunk):
    w_ref[COL_GATE:N_PROJ, :] = jnp.zeros((N_PROJ - COL_GATE, w_ref.shape[1]), BF16)
    for src, dst, width in W_IN_SPANS:
        for r0 in range(0, width, 512):
            r1 = min(r0 + 512, width)
            w_ref[dst + r0:dst + r1, :] = win_ref[src + r0:src + r1, :].astype(BF16)
    _project_rows(x_ref, g_ref, w_ref, o_ref, gc_ref, gt_ref, h_ref, n_chunk)


def _in_proj_kernel(x_ref, g_ref, w_ref, o_ref, gc_ref, gt_ref, h_ref, *, n_chunk):
    _project_rows(x_ref, g_ref, w_ref, o_ref, gc_ref, gt_ref, h_ref, n_chunk)


def _in_proj_regroup(x2d, g, w_t):
    m, k = x2d.shape
    full = lambda i: (0, 0)
    return pl.pallas_call(
        functools.partial(_in_proj_regroup_kernel, n_chunk=512),
        grid=(1,),
        in_specs=[pl.BlockSpec((m, k), full), pl.BlockSpec((1, k), full), pl.BlockSpec((W_IN_COLS, k), full)],
        out_specs=[pl.BlockSpec((m, COL_GATE), full), pl.BlockSpec((m, 128), full),
                   pl.BlockSpec((m // CHUNK, 16, CHUNK), lambda i: (0, 0, 0)), pl.BlockSpec((N_PROJ, k), full)],
        out_shape=[jax.ShapeDtypeStruct((m, COL_GATE), F32),
                   jax.ShapeDtypeStruct((m, 128), F32),
                   jax.ShapeDtypeStruct((m // CHUNK, 16, CHUNK), F32),
                   jax.ShapeDtypeStruct((N_PROJ, k), BF16)],
        scratch_shapes=[pltpu.VMEM((m, k), BF16)],
        compiler_params=pltpu.CompilerParams(dimension_semantics=("arbitrary",),
                                             vmem_limit_bytes=VMEM_LIMIT),
        name="in_proj_regroup",
    )(x2d, g, w_t)


def _in_proj(x2d, g, w, tm, out_dtype):
    m, k = x2d.shape
    return pl.pallas_call(
        functools.partial(_in_proj_kernel, n_chunk=512),
        grid=(m // tm,),
        in_specs=[pl.BlockSpec((tm, k), lambda i: (i, 0)),
                  pl.BlockSpec((1, k), lambda i: (0, 0)),
                  pl.BlockSpec((N_PROJ, k), lambda i: (0, 0), pipeline_mode=pl.Buffered(1))],
        out_specs=[pl.BlockSpec((tm, COL_GATE), lambda i: (i, 0)),
                   pl.BlockSpec((tm, 128), lambda i: (i, 0)),
                   pl.BlockSpec((tm // CHUNK, 16, CHUNK), lambda i: (i, 0, 0))],
        out_shape=[jax.ShapeDtypeStruct((m, COL_GATE), out_dtype),
                   jax.ShapeDtypeStruct((m, 128), F32),
                   jax.ShapeDtypeStruct((m // CHUNK, 16, CHUNK), F32)],
        scratch_shapes=[pltpu.VMEM((tm, k), BF16)],
        compiler_params=pltpu.CompilerParams(dimension_semantics=("arbitrary",),
                                             vmem_limit_bytes=VMEM_LIMIT),
        name="in_proj",
    )(x2d, g, w)


def _out_mlp_kernel(og_ref, om_ref, x_ref, wo_ref, g1_ref, g2_ref, wu_ref, wd_ref, g3_ref,
                    y_ref, hn_ref, acc_ref, *, ff_chunk):
    half = og_ref.shape[1]
    mix = (jnp.dot(og_ref[...], wo_ref[0:half, :], preferred_element_type=F32)
           + jnp.dot(om_ref[...], wo_ref[half:2 * half, :], preferred_element_type=F32))
    x1 = x_ref[...] + _rms(mix, g1_ref[...])
    y_ref[...] = x1
    hn_ref[...] = _rms(x1, g2_ref[...]).astype(BF16)
    d_ff = wu_ref.shape[1]
    for c0 in range(0, d_ff, ff_chunk):
        u = jnp.dot(hn_ref[...], wu_ref[:, c0:c0 + ff_chunk], preferred_element_type=F32)
        u = jnp.square(jnp.maximum(u, 0.0)).astype(BF16)
        d = jnp.dot(u, wd_ref[c0:c0 + ff_chunk, :], preferred_element_type=F32)
        if c0 == 0:
            acc_ref[...] = d
        else:
            acc_ref[...] += d
    y_ref[...] = y_ref[...] + _rms(acc_ref[...], g3_ref[...])


def _out_mlp(og, om, x2d, wo, g1, g2, wu, wd, g3, tm):
    m, d = x2d.shape
    half = og.shape[1]
    d_ff = wu.shape[1]
    const = lambda i: (0, 0)
    return pl.pallas_call(
        functools.partial(_out_mlp_kernel, ff_chunk=1024),
        grid=(m // tm,),
        in_specs=[pl.BlockSpec((tm, half), lambda i: (i, 0)),
                  pl.BlockSpec((tm, half), lambda i: (i, 0)),
                  pl.BlockSpec((tm, d), lambda i: (i, 0)),
                  pl.BlockSpec((d, d), const, pipeline_mode=pl.Buffered(1)),
                  pl.BlockSpec((1, d), const),
                  pl.BlockSpec((1, d), const),
                  pl.BlockSpec((d, d_ff), const, pipeline_mode=pl.Buffered(1)),
                  pl.BlockSpec((d_ff, d), const, pipeline_mode=pl.Buffered(1)),
                  pl.BlockSpec((1, d), const)],
        out_specs=pl.BlockSpec((tm, d), lambda i: (i, 0)),
        out_shape=jax.ShapeDtypeStruct((m, d), F32),
        scratch_shapes=[pltpu.VMEM((tm, d), BF16), pltpu.VMEM((tm, d), F32)],
        compiler_params=pltpu.CompilerParams(dimension_semantics=("arbitrary",),
                                             vmem_limit_bytes=VMEM_LIMIT),
        name="out_mlp",
    )(og, om, x2d, wo, g1, g2, wu, wd, g3)


def _chunk_masks():
    ii = lax.broadcasted_iota(jnp.int32, (CHUNK, CHUNK), 0)
    jj = lax.broadcasted_iota(jnp.int32, (CHUNK, CHUNK), 1)
    return ii, jj


def _cumsum_col_row(x_col, x_row, ii, jj):
    c_col = jnp.sum(jnp.where(jj <= ii, x_row, 0.0), axis=1, keepdims=True)
    c_row = jnp.sum(jnp.where(ii <= jj, x_col, 0.0), axis=0, keepdims=True)
    return c_col, c_row


def _interleave(*tasks):
    live = [[g, n, 0] for g, n in tasks]
    while live:
        entry = min(live, key=lambda e: (e[2] + 1) / e[1])
        try:
            next(entry[0])
            entry[2] += 1
        except StopIteration:
            live.remove(entry)


def _halves(items):
    mid = len(items) // 2
    return items[:mid], items[mid:]


def _gdn_phase_a(yc_ref, gcol_ref, grow_ref, plane_ref, psub_ref, wv_ref, lhs_ref, qk_ref, ket_ref, ge_ref,
                 ncb):
    ii, jj = _chunk_masks()
    incl = jj <= ii
    strict = jj < ii
    eye = (ii == jj).astype(F32)
    neg_a_lane = -jnp.exp(plane_ref[0:1, :])
    dtb_lane = plane_ref[1:2, :]
    neg_a_sub = -jnp.exp(psub_ref[:, 0:1])
    dtb_sub = psub_ref[:, 1:2]
    items = []
    for c in range(ncb):
        rows = slice(c * CHUNK, (c + 1) * CHUNK)
        gc = gcol_ref[rows, :]
        gr = grow_ref[c]
        beta_t = _sigmoid(gc)
        g_t = neg_a_lane * _softplus(gc + dtb_lane)
        g_r = neg_a_sub * _softplus(gr + dtb_sub)
        for h in range(HEADS):
            lo = h * GDN_D
            items.append(dict(c=c, h=h,
                              q=yc_ref[rows, lo:lo + GDN_D],
                              k=yc_ref[rows, 512 + lo:512 + lo + GDN_D],
                              v=yc_ref[rows, 1024 + lo:1024 + lo + GDN_D],
                              beta=beta_t[:, LANE_GB + h:LANE_GB + h + 1],
                              gg_col=g_t[:, LANE_GA + h:LANE_GA + h + 1],
                              gg_row=g_r[LANE_GA + h:LANE_GA + h + 1, :]))
    yield
    for it in items:
        it["k"] = it["k"] * lax.rsqrt(jnp.sum(it["k"] * it["k"], axis=-1, keepdims=True) + EPS)
    yield
    for it in items:
        it["kk"] = _mm_nt(it["k"], it["k"])
    yield
    for it in items:
        it["q"] = (it["q"] * lax.rsqrt(jnp.sum(it["q"] * it["q"], axis=-1, keepdims=True) + EPS)
                   * (GDN_D ** -0.5))
    yield
    for it in items:
        it["g_col"], g_row = _cumsum_col_row(it["gg_col"], it["gg_row"], ii, jj)
        it["decay"] = jnp.where(incl, jnp.exp(jnp.where(incl, it["g_col"] - g_row, 0.0)), 0.0)
    yield
    for it in items:
        n_mat = jnp.where(strict, it["beta"] * it["kk"] * it["decay"], 0.0)
        it["x"] = eye - n_mat
        it["p"] = -n_mat
    yield
    for _ in range(5):
        yield
        for it in items:
            it["p"] = _mm(it["p"], it["p"])
        yield
        for it in items:
            it["x"] = it["x"] + _mm(it["x"], it["p"])
    yield
    for it in items:
        e_g = jnp.exp(it["g_col"])
        it["e_g"] = e_g
        rhs = jnp.concatenate([it["beta"] * it["v"], (it["beta"] * e_g) * it["k"]], axis=1)
        it["w"] = _mm(it["x"], rhs)
    yield
    for it in items:
        it["qk"] = _mm_nt(it["q"], it["k"]) * it["decay"]
    yield
    for it in items:
        c, h = it["c"], it["h"]
        g_end = it["g_col"][CHUNK - 1:CHUNK, :]
        k_end = it["k"] * jnp.exp(g_end - it["g_col"])
        wv_ref[c, h] = it["w"][:, 0:GDN_D]
        lhs_ref[c, h] = jnp.concatenate([it["w"][:, GDN_D:2 * GDN_D], it["e_g"] * it["q"]],
                                        axis=0).astype(BF16)
        qk_ref[c, h] = it["qk"].astype(BF16)
        ket_ref[c, h] = k_end.T.astype(BF16)
        ge_ref[c, h] = jnp.broadcast_to(jnp.exp(g_end), (1, GDN_D))


def _gdn_phase_b(qkvz_ref, gn_ref, og_ref, s_ref, wv_ref, lhs_ref, qk_ref, ket_ref, ge_ref, ncb):
    gn = gn_ref[...]

    def epilogue(c, o):
        rows = slice(c * CHUNK, (c + 1) * CHUNK)
        for h in range(HEADS):
            lo = h * GDN_D
            z = qkvz_ref[rows, COL_Z + lo:COL_Z + lo + GDN_D].astype(F32)
            out = _rms(o[h], gn) * (z * _sigmoid(z))
            og_ref[rows, lo:lo + GDN_D] = out.astype(og_ref.dtype)

    s = [s_ref[h] for h in range(HEADS)]
    o_prev = None
    for c in range(ncb):
        r = [jnp.dot(lhs_ref[c, h], s[h].astype(BF16), preferred_element_type=F32) for h in range(HEADS)]
        yield
        if o_prev is not None:
            epilogue(c - 1, o_prev)
        ub = [(wv_ref[c, h] - r[h][0:CHUNK]).astype(BF16) for h in range(HEADS)]
        s = [ge_ref[c, h] * s[h] + jnp.dot(ket_ref[c, h], ub[h], preferred_element_type=F32)
             for h in range(HEADS)]
        o_prev = [r[h][CHUNK:2 * CHUNK] + jnp.dot(qk_ref[c, h], ub[h], preferred_element_type=F32)
                  for h in range(HEADS)]
        yield
    epilogue(ncb - 1, o_prev)
    for h in range(HEADS):
        s_ref[h] = s[h]


def _mlstm_block(mvo_ref, mqk_ref, gcol_ref, grow_ref, plane_ref, psub_ref, gn_ref, om_ref,
                 c_ref, n_ref, m_ref, ncb):
    ii, jj = _chunk_masks()
    incl = jj <= ii
    lane = lax.broadcasted_iota(jnp.int32, (1, 128), 1)
    row128 = lax.broadcasted_iota(jnp.int32, (128, 1), 0)
    gn = gn_ref[...]
    blane = plane_ref[2:3, :]
    bsub = psub_ref[:, 2:3]
    items = []
    for c in range(ncb):
        rows = slice(c * CHUNK, (c + 1) * CHUNK)
        gc = gcol_ref[rows, :] + blane
        gr = grow_ref[c] + bsub
        lf_c = -_softplus(-gc)
        lf_r = -_softplus(-gr)
        for p in range(HEADS // 2):
            qb = mqk_ref[rows, p * 128:(p + 1) * 128].astype(F32)
            kb = mqk_ref[rows, 256 + p * 128:256 + (p + 1) * 128].astype(F32) * (ML_DK ** -0.5)
            for e in range(2):
                h = 2 * p + e
                lm = (lane >= e * ML_DK) & (lane < (e + 1) * ML_DK)
                items.append(dict(c=c, rows=rows, p=p, e=e, h=h, qh=jnp.where(lm, qb, 0.0), kb=kb,
                                  kh=jnp.where(lm, kb, 0.0),
                                  lf_col=lf_c[:, LANE_MF + h:LANE_MF + h + 1],
                                  lf_row=lf_r[LANE_MF + h:LANE_MF + h + 1, :],
                                  ig_row=gr[LANE_MI + h:LANE_MI + h + 1, :],
                                  ig_col=gc[:, LANE_MI + h:LANE_MI + h + 1],
                                  v=mvo_ref[rows, h * ML_DV:(h + 1) * ML_DV]))
        if c % 2 == 1:
            yield
    for part in _halves(items):
        for it in part:
            it["qk"] = _mm_nt(it["qh"], it["kb"])
        yield
    for part in _halves(items):
        for it in part:
            it["f_col"], it["f_row"] = _cumsum_col_row(it["lf_col"], it["lf_row"], ii, jj)
        yield
    for part in _halves(items):
        for it in part:
            it["d_mat"] = jnp.where(incl, it["f_col"] - it["f_row"] + it["ig_row"], -jnp.inf)
            it["d_max"] = jnp.max(it["d_mat"], axis=1, keepdims=True)
        yield
    for part in _halves(items):
        for it in part:
            p0 = jnp.where(incl, jnp.exp(jnp.where(incl, it["d_mat"] - it["d_max"], 0.0)), 0.0)
            pend0 = jnp.exp(it["f_col"][CHUNK - 1:CHUNK, :] - it["f_col"] + it["ig_col"]
                            - it["d_max"][CHUNK - 1:CHUNK, :])
            it["kp0"] = it["kh"] * pend0
            it["pqk0"] = p0 * it["qk"]
        yield
    for part in _halves(items):
        for it in part:
            it["pv0"] = _mm(it["pqk0"], it["v"])
            it["rs0"] = jnp.sum(it["pqk0"], axis=-1, keepdims=True)
        yield
    for part in _halves(items):
        for it in part:
            it["cadd0"] = _mm(it["kp0"].T, it["v"])
            it["nadd0"] = jnp.sum(it["kp0"], axis=0, keepdims=True)
        yield

    m_cur = [m_ref[:, h:h + 1] for h in range(HEADS)]
    for it in items:
        h = it["h"]
        it["m_prev"] = m_cur[h]
        m_cur[h] = jnp.maximum(it["f_col"][CHUNK - 1:CHUNK, :] + m_cur[h], it["d_max"][CHUNK - 1:CHUNK, :])
    yield
    for part in _halves(items):
        for it in part:
            bcol = it["f_col"] + it["m_prev"]
            mt = jnp.maximum(bcol, it["d_max"])
            it["mt"] = mt
            it["w_prev"] = jnp.exp(bcol - mt)
            it["sc"] = jnp.exp(it["d_max"] - mt)
        yield
    c_cur = [c_ref[p] for p in range(HEADS // 2)]
    n_cur = [n_ref[p] for p in range(HEADS // 2)]
    for c in range(ncb):
        for p in range(HEADS // 2):
            pair = [it for it in items if it["c"] == c and it["p"] == p]
            w_end = [it["w_prev"][CHUNK - 1:CHUNK, :] for it in pair]
            s_end = [it["sc"][CHUNK - 1:CHUNK, :] for it in pair]
            for it in pair:
                it["c_prev"] = c_cur[p]
                it["n_prev"] = n_cur[p]
            c_cur[p] = (jnp.where(row128 < ML_DK, w_end[0], w_end[1]) * c_cur[p]
                        + s_end[0] * pair[0]["cadd0"] + s_end[1] * pair[1]["cadd0"])
            n_cur[p] = (jnp.where(lane < ML_DK, w_end[0], w_end[1]) * n_cur[p]
                        + s_end[0] * pair[0]["nadd0"] + s_end[1] * pair[1]["nadd0"])
        if c % 2 == 1:
            yield
    for part in _halves(items):
        for it in part:
            it["qc"] = _mm(it["qh"], it["c_prev"])
        yield
    for part in _halves(items):
        for it in part:
            num = it["w_prev"] * it["qc"] + it["sc"] * it["pv0"]
            den = (it["w_prev"] * jnp.sum(it["qh"] * it["n_prev"], axis=-1, keepdims=True)
                   + it["sc"] * it["rs0"])
            it["hh"] = num / jnp.maximum(jnp.abs(den), jnp.exp(-it["mt"]))
        yield
    for part in _halves(items):
        for it in part:
            h = it["h"]
            mo = mvo_ref[it["rows"], 512 + h * ML_DV:512 + (h + 1) * ML_DV].astype(F32)
            out = _sigmoid(mo) * _rms(it["hh"], gn)
            om_ref[it["rows"], h * ML_DV:(h + 1) * ML_DV] = out.astype(om_ref.dtype)
        yield
    for p in range(HEADS // 2):
        c_ref[p] = c_cur[p]
        n_ref[p] = n_cur[p]
    for h in range(HEADS):
        m_ref[:, h:h + 1] = m_cur[h]


def _chain(*gens):
    for g in gens:
        yield from g


def _conv_stage(xp_ref, yc_ref, cw_ref, tb):
    for ct in range(CONV_CH // 128):
        cols = slice(ct * 128, (ct + 1) * 128)
        w = cw_ref[:, cols]
        for r0 in range(0, tb, 128):
            acc = xp_ref[8 + r0:8 + r0 + 128, cols] * w[CONV_W - 1:CONV_W, :]
            for j in range(CONV_W - 1):
                acc = acc + xp_ref[5 + j + r0:5 + j + r0 + 128, cols] * w[j:j + 1, :]
            yc_ref[r0:r0 + 128, cols] = acc * _sigmoid(acc)
        yield


def _out_mlp_block(og_ref, om_ref, x_ref, wo_ref, g1_ref, g2_ref, wu_ref, wd_ref, g3_ref,
                   y_ref, hn_ref, acc_ref, ff_chunk, n_split):
    half = og_ref.shape[1]
    d = x_ref.shape[1]
    d_ff = wu_ref.shape[1]
    col_groups = [slice(c, c + d // n_split) for c in range(0, d, d // n_split)]
    for cg in col_groups:
        acc_ref[:, cg] = (jnp.dot(og_ref[...], wo_ref[0:half, cg], preferred_element_type=F32)
                          + jnp.dot(om_ref[...], wo_ref[half:2 * half, cg], preferred_element_type=F32))
        yield
    x1 = x_ref[...] + _rms(acc_ref[...], g1_ref[...])
    y_ref[...] = x1
    hn_ref[...] = _rms(x1, g2_ref[...]).astype(BF16)
    yield
    for c0 in range(0, d_ff, ff_chunk):
        u = jnp.dot(hn_ref[...], wu_ref[:, c0:c0 + ff_chunk], preferred_element_type=F32)
        u = jnp.square(jnp.maximum(u, 0.0)).astype(BF16)
        for cg in col_groups:
            dd = jnp.dot(u, wd_ref[c0:c0 + ff_chunk, cg], preferred_element_type=F32)
            if c0 == 0:
                acc_ref[:, cg] = dd
            else:
                acc_ref[:, cg] += dd
        yield
    y_ref[...] = y_ref[...] + _rms(acc_ref[...], g3_ref[...])
    yield


def _layer_kernel(qkvz_ref, mvo_ref, mqk_ref, gcol_ref, grow_ref, cs_ref, s0_ref, c0_ref, n0_ref, m0_ref,
                  x_ref, cw_ref, plane_ref, psub_ref, gng_ref, gnm_ref,
                  wo_ref, g1_ref, g2_ref, wu_ref, wd_ref, g3_ref,
                  y_ref, sout_ref, cout_ref, nout_ref, mout_ref,
                  xp_ref, yc_ref, s_ref, wv_ref, lhs_ref, qk_ref, ket_ref, ge_ref, c_ref, n_ref, m_ref,
                  og_ref, om_ref, hn_ref, acc_ref, *, tb, nt, n_blocks):
    g = pl.program_id(0)
    t = jnp.minimum(g, n_blocks - 1) % nt
    ncb = tb // CHUNK
    par = g % 2

    @pl.when(g == 0)
    def _():
        og_ref[1] = jnp.zeros(og_ref.shape[1:], og_ref.dtype)
        om_ref[1] = jnp.zeros(om_ref.shape[1:], om_ref.dtype)

    @pl.when(t == 0)
    def _():
        xp_ref[0:8, :] = cs_ref[...]
        s_ref[...] = s0_ref[...]
        c_ref[...] = c0_ref[...]
        n_ref[...] = n0_ref[...]
        m_ref[...] = m0_ref[...]

    @pl.when(t > 0)
    def _():
        xp_ref[0:8, :] = xp_ref[tb:tb + 8, :]

    xp_ref[8:tb + 8, :] = qkvz_ref[:, COL_QKV:COL_QKV + CONV_CH].astype(F32)

    og_w, om_w = og_ref.at[par], om_ref.at[par]
    og_r, om_r = og_ref.at[1 - par], om_ref.at[1 - par]
    _interleave(
        (_chain(_conv_stage(xp_ref, yc_ref, cw_ref, tb),
                _gdn_phase_a(yc_ref, gcol_ref, grow_ref, plane_ref, psub_ref, wv_ref, lhs_ref, qk_ref, ket_ref,
                             ge_ref, ncb),
                _gdn_phase_b(qkvz_ref, gng_ref, og_w, s_ref, wv_ref, lhs_ref, qk_ref, ket_ref, ge_ref, ncb)),
         CONV_CH // 128 + 20 + 2 * ncb),
        (_out_mlp_block(og_r, om_r, x_ref, wo_ref, g1_ref, g2_ref, wu_ref, wd_ref, g3_ref, y_ref, hn_ref, acc_ref,
                        ff_chunk=256, n_split=2), 2 + 1 + wu_ref.shape[1] // 256 + 1),
        (_mlstm_block(mvo_ref, mqk_ref, gcol_ref, grow_ref, plane_ref, psub_ref, gnm_ref, om_w,
                      c_ref, n_ref, m_ref, ncb), 21 + ncb))

    @pl.when((t == nt - 1) & (g < n_blocks))
    def _():
        sout_ref[...] = s_ref[...]
        cout_ref[...] = c_ref[...]
        nout_ref[...] = n_ref[...]
        mout_ref[...] = m_ref[...]


def _layer_prompt(proj, gates, gates_t, cs8, s0, c0, n0, m0, x2d, conv_w, plane, psub, gng, gnm,
                  wo, g1, g2, wu, wd, g3, tb):
    b, t, _ = proj.shape
    d = x2d.shape[1]
    d_ff = wu.shape[1]
    nt = t // tb
    n_blocks = b * nt
    ncb = tb // CHUNK
    hp = HEADS // 2

    def blk(g):
        gm = jnp.minimum(g, n_blocks - 1)
        return gm // nt, gm % nt

    def tok3(col):
        return lambda g: blk(g) + (col,)

    per_seq3 = lambda g: (blk(g)[0], 0, 0)
    per_seq4 = lambda g: (blk(g)[0], 0, 0, 0)
    prev_rows = lambda g: (jnp.maximum(g - 1, 0), 0)
    const = lambda g: (0, 0)
    resident = dict(pipeline_mode=pl.Buffered(1))
    return pl.pallas_call(
        functools.partial(_layer_kernel, tb=tb, nt=nt, n_blocks=n_blocks),
        grid=(n_blocks + 1,),
        in_specs=[pl.BlockSpec((None, tb, 2048), tok3(0)),
                  pl.BlockSpec((None, tb, 1024), tok3(COL_MV // 1024)),
                  pl.BlockSpec((None, tb, 512), tok3(COL_MQ // 512)),
                  pl.BlockSpec((None, tb, 128), tok3(0)),
                  pl.BlockSpec((None, ncb, 16, CHUNK), lambda g: blk(g) + (0, 0)),
                  pl.BlockSpec((None, 8, CONV_CH), per_seq3),
                  pl.BlockSpec((None, HEADS, GDN_D, GDN_D), per_seq4),
                  pl.BlockSpec((None, hp, 128, 128), per_seq4),
                  pl.BlockSpec((None, hp, 1, 128), per_seq4),
                  pl.BlockSpec((None, 1, 128), per_seq3),
                  pl.BlockSpec((tb, d), prev_rows),
                  pl.BlockSpec((CONV_W, CONV_CH), const),
                  pl.BlockSpec((3, 128), const),
                  pl.BlockSpec((16, 3), const),
                  pl.BlockSpec((1, GDN_D), const),
                  pl.BlockSpec((1, ML_DV), const),
                  pl.BlockSpec((d, d), const, **resident),
                  pl.BlockSpec((1, d), const),
                  pl.BlockSpec((1, d), const),
                  pl.BlockSpec((d, d_ff), const, **resident),
                  pl.BlockSpec((d_ff, d), const, **resident),
                  pl.BlockSpec((1, d), const)],
        out_specs=[pl.BlockSpec((tb, d), prev_rows),
                   pl.BlockSpec((None, HEADS, GDN_D, GDN_D), per_seq4),
                   pl.BlockSpec((None, hp, 128, 128), per_seq4),
                   pl.BlockSpec((None, hp, 1, 128), per_seq4),
                   pl.BlockSpec((None, 1, 128), per_seq3)],
        out_shape=[jax.ShapeDtypeStruct((b * t, d), F32),
                   jax.ShapeDtypeStruct((b, HEADS, GDN_D, GDN_D), F32),
                   jax.ShapeDtypeStruct((b, hp, 128, 128), F32),
                   jax.ShapeDtypeStruct((b, hp, 1, 128), F32),
                   jax.ShapeDtypeStruct((b, 1, 128), F32)],
        scratch_shapes=[pltpu.VMEM((tb + 8, CONV_CH), F32),
                        pltpu.VMEM((tb, CONV_CH), F32),
                        pltpu.VMEM((HEADS, GDN_D, GDN_D), F32),
                        pltpu.VMEM((ncb, HEADS, CHUNK, GDN_D), F32),
                        pltpu.VMEM((ncb, HEADS, 2 * CHUNK, GDN_D), BF16),
                        pltpu.VMEM((ncb, HEADS, CHUNK, CHUNK), BF16),
                        pltpu.VMEM((ncb, HEADS, GDN_D, CHUNK), BF16),
                        pltpu.VMEM((ncb, HEADS, 1, GDN_D), F32),
                        pltpu.VMEM((hp, 128, 128), F32),
                        pltpu.VMEM((hp, 1, 128), F32),
                        pltpu.VMEM((1, 128), F32),
                        pltpu.VMEM((2, tb, HEADS * GDN_D), BF16),
                        pltpu.VMEM((2, tb, HEADS * ML_DV), BF16),
                        pltpu.VMEM((tb, d), BF16),
                        pltpu.VMEM((tb, d), F32)],
        compiler_params=pltpu.CompilerParams(dimension_semantics=("arbitrary",),
                                             vmem_limit_bytes=VMEM_LIMIT,
                                             allow_input_fusion=[i in (5, 9) for i in range(22)]),
        name="layer_prompt",
    )(proj, proj, proj, gates, gates_t, cs8, s0, c0, n0, m0, x2d, conv_w, plane, psub, gng, gnm,
      wo, g1, g2, wu, wd, g3)


RING = 3


def _decode_kernel(proj_ref, gates_ref, cs_ref, s0_hbm, c0_hbm, n0_ref, m0_ref, cw_ref, plane_ref, gng_ref, gnm_ref,
                   og_ref, om_ref, sout_ref, cout_ref, nout_ref, mout_ref, rowg_ref, rowm_ref,
                   sbuf_ref, cbuf_ref, sem_ref, *, bb):
    i = pl.program_id(0)
    n_steps = pl.num_programs(0)

    def state_copies(step, slot):
        rows = pl.ds(step * bb, bb)
        return (pltpu.make_async_copy(s0_hbm.at[rows], sbuf_ref.at[slot], sem_ref.at[slot, 0]),
                pltpu.make_async_copy(c0_hbm.at[rows], cbuf_ref.at[slot], sem_ref.at[slot, 1]))

    @pl.when(i == 0)
    def _():
        for step in range(RING - 1):
            for cp in state_copies(step, step):
                cp.start()

    @pl.when(i + RING - 1 < n_steps)
    def _():
        for cp in state_copies(i + RING - 1, (i + RING - 1) % RING):
            cp.start()

    slot = i % RING
    for cp in state_copies(i, slot):
        cp.wait()
    s0_ref = sbuf_ref.at[slot]
    c0_ref = cbuf_ref.at[slot]

    lane = lax.broadcasted_iota(jnp.int32, (1, 128), 1)
    gt = gates_ref[...]
    beta_t = _sigmoid(gt)
    g_t = -jnp.exp(plane_ref[0:1, :]) * _softplus(gt + plane_ref[1:2, :])
    eg_t = jnp.exp(g_t)
    gb_t = gt + plane_ref[2:3, :]
    lf_t = -_softplus(-gb_t)

    heads = []
    for h in range(HEADS):
        cols = []
        for part in range(3):
            c0 = part * 512 + h * GDN_D
            w = cw_ref[:, c0:c0 + GDN_D]
            acc = cs_ref[:, 0, c0:c0 + GDN_D] * w[0:1, :]
            acc = acc + cs_ref[:, 1, c0:c0 + GDN_D] * w[1:2, :]
            acc = acc + cs_ref[:, 2, c0:c0 + GDN_D] * w[2:3, :]
            acc = acc + proj_ref[:, COL_QKV + c0:COL_QKV + c0 + GDN_D] * w[3:4, :]
            cols.append(acc * _sigmoid(acc))
        q, k, v = cols
        q = q * lax.rsqrt(jnp.sum(q * q, axis=-1, keepdims=True) + EPS) * (GDN_D ** -0.5)
        k = k * lax.rsqrt(jnp.sum(k * k, axis=-1, keepdims=True) + EPS)
        heads.append(dict(v=v, qk=jnp.sum(q * k, axis=-1, keepdims=True), k_t=k.T.astype(BF16),
                          kq=jnp.concatenate([k, q], axis=0).astype(BF16),
                          beta=beta_t[:, LANE_GB + h:LANE_GB + h + 1],
                          eg=eg_t[:, LANE_GA + h:LANE_GA + h + 1]))
    items = [dict(h=h, b=b) for h in range(HEADS) for b in range(bb)]
    row_b = lax.broadcasted_iota(jnp.int32, (bb, 1), 0)
    for it in items:
        r = jnp.dot(heads[it["h"]]["kq"], s0_ref[it["b"], it["h"]].astype(BF16), preferred_element_type=F32)
        it["ks"] = r[it["b"]:it["b"] + 1, :]
        it["qs"] = r[bb + it["b"]:bb + it["b"] + 1, :]
    for it in items:
        hd, b = heads[it["h"]], it["b"]
        eg_b = hd["eg"][b:b + 1, :]
        it["eg_b"] = eg_b
        it["u"] = hd["beta"][b:b + 1, :] * (hd["v"][b:b + 1, :] - eg_b * it["ks"])
        rowg_ref[it["h"], b:b + 1, :] = eg_b * it["qs"] + hd["qk"][b:b + 1, :] * it["u"]
    for it in items:
        u_sel = jnp.where(row_b == it["b"], it["u"], 0.0).astype(BF16)
        upd = jnp.dot(heads[it["h"]]["k_t"], u_sel, preferred_element_type=F32)
        sout_ref[it["b"], it["h"]] = it["eg_b"] * s0_ref[it["b"], it["h"]] + upd
    for h in range(HEADS):
        z = proj_ref[:, COL_Z + h * GDN_D:COL_Z + (h + 1) * GDN_D]
        out = _rms(rowg_ref[h], gng_ref[...]) * (z * _sigmoid(z))
        og_ref[:, h * GDN_D:(h + 1) * GDN_D] = out.astype(og_ref.dtype)

    heads = []
    for p in range(HEADS // 2):
        qb = proj_ref[:, COL_MQ + p * 128:COL_MQ + (p + 1) * 128]
        kb = proj_ref[:, COL_MK + p * 128:COL_MK + (p + 1) * 128] * (ML_DK ** -0.5)
        n_p = n0_ref[:, p * 128:(p + 1) * 128]
        k_t = jnp.concatenate([jnp.where(lane < ML_DK, kb, 0.0), jnp.where(lane >= ML_DK, kb, 0.0)],
                              axis=0).T.astype(BF16)
        qq = jnp.concatenate([jnp.where(lane < ML_DK, qb, 0.0), jnp.where(lane >= ML_DK, qb, 0.0)],
                             axis=0).astype(BF16)
        w_prev, p_in = [], []
        for e in range(2):
            h = 2 * p + e
            lm = (lane >= e * ML_DK) & (lane < (e + 1) * ML_DK)
            ig = gb_t[:, LANE_MI + h:LANE_MI + h + 1]
            lf = lf_t[:, LANE_MF + h:LANE_MF + h + 1]
            m_old = m0_ref[:, h:h + 1]
            m_new = jnp.maximum(lf + m_old, ig)
            w_prev.append(jnp.exp(lf + m_old - m_new))
            p_in.append(jnp.exp(ig - m_new))
            qk = jnp.sum(jnp.where(lm, qb * kb, 0.0), axis=-1, keepdims=True)
            qn = jnp.sum(jnp.where(lm, qb * n_p, 0.0), axis=-1, keepdims=True)
            mout_ref[:, h:h + 1] = m_new
            heads.append(dict(p=p, e=e, qq=qq, k_t=k_t, w_prev=w_prev[e], p_in=p_in[e],
                              pqk=p_in[e] * qk, wqn=w_prev[e] * qn, floor=jnp.exp(-m_new),
                              v=proj_ref[:, COL_MV + h * ML_DV:COL_MV + (h + 1) * ML_DV]))
        lo_lane = lane < ML_DK
        nout_ref[:, p * 128:(p + 1) * 128] = (jnp.where(lo_lane, w_prev[0], w_prev[1]) * n_p
                                              + jnp.where(lo_lane, p_in[0], p_in[1]) * kb)
    items = [dict(h=h, b=b) for h in range(HEADS) for b in range(bb)]
    qc_pair = {(p, b): jnp.dot(heads[2 * p]["qq"], c0_ref[b, p].astype(BF16), preferred_element_type=F32)
               for p in range(HEADS // 2) for b in range(bb)}
    for it in items:
        hd, b = heads[it["h"]], it["b"]
        row = hd["e"] * bb + b
        it["qc"] = qc_pair[(hd["p"], b)][row:row + 1, :]
    for it in items:
        hd, b = heads[it["h"]], it["b"]
        pqk = hd["pqk"][b:b + 1, :]
        num = hd["w_prev"][b:b + 1, :] * it["qc"] + pqk * hd["v"][b:b + 1, :]
        den = hd["wqn"][b:b + 1, :] + pqk
        rowm_ref[it["h"], b:b + 1, :] = num / jnp.maximum(jnp.abs(den), hd["floor"][b:b + 1, :])
    row128 = lax.broadcasted_iota(jnp.int32, (128, 1), 0)
    for p in range(HEADS // 2):
        h0, h1 = heads[2 * p], heads[2 * p + 1]
        pv = jnp.concatenate([h0["p_in"] * h0["v"], h1["p_in"] * h1["v"]], axis=0)
        row_2b = lax.broadcasted_iota(jnp.int32, (2 * bb, 1), 0)
        for b in range(bb):
            pv_sel = jnp.where((row_2b == b) | (row_2b == bb + b), pv, 0.0).astype(BF16)
            upd = jnp.dot(h0["k_t"], pv_sel, preferred_element_type=F32)
            w_col = jnp.where(row128 < ML_DK, h0["w_prev"][b:b + 1, :], h1["w_prev"][b:b + 1, :])
            cout_ref[b, p] = w_col * c0_ref[b, p] + upd
    for h in range(HEADS):
        mo = proj_ref[:, COL_MO + h * ML_DV:COL_MO + (h + 1) * ML_DV]
        out = _sigmoid(mo) * _rms(rowm_ref[h], gnm_ref[...])
        om_ref[:, h * ML_DV:(h + 1) * ML_DV] = out.astype(om_ref.dtype)
    mout_ref[:, HEADS:128] = m0_ref[:, HEADS:128]


def _decode(proj, gates, cs, s0, c0, n0, m0, conv_w, plane, gng, gnm, bb):
    b = proj.shape[0]
    hp = HEADS // 2
    assert b % bb == 0 and b // bb >= RING - 1
    return pl.pallas_call(
        functools.partial(_decode_kernel, bb=bb),
        grid=(b // bb,),
        in_specs=[pl.BlockSpec((bb, COL_GATE), lambda i: (i, 0)),
                  pl.BlockSpec((bb, 128), lambda i: (i, 0)),
                  pl.BlockSpec((bb, CONV_W - 1, CONV_CH), lambda i: (i, 0, 0)),
                  pl.BlockSpec(memory_space=pl.ANY),
                  pl.BlockSpec(memory_space=pl.ANY),
                  pl.BlockSpec((bb, HEADS * ML_DK), lambda i: (i, 0)),
                  pl.BlockSpec((bb, 128), lambda i: (i, 0)),
                  pl.BlockSpec((CONV_W, CONV_CH), lambda i: (0, 0)),
                  pl.BlockSpec((3, 128), lambda i: (0, 0)),
                  pl.BlockSpec((1, GDN_D), lambda i: (0, 0)),
                  pl.BlockSpec((1, ML_DV), lambda i: (0, 0))],
        out_specs=[pl.BlockSpec((bb, HEADS * GDN_D), lambda i: (i, 0)),
                   pl.BlockSpec((bb, HEADS * ML_DV), lambda i: (i, 0)),
                   pl.BlockSpec((bb, HEADS, GDN_D, GDN_D), lambda i: (i, 0, 0, 0)),
                   pl.BlockSpec((bb, hp, 128, 128), lambda i: (i, 0, 0, 0)),
                   pl.BlockSpec((bb, HEADS * ML_DK), lambda i: (i, 0)),
                   pl.BlockSpec((bb, 128), lambda i: (i, 0))],
        out_shape=[jax.ShapeDtypeStruct((b, HEADS * GDN_D), BF16),
                   jax.ShapeDtypeStruct((b, HEADS * ML_DV), BF16),
                   jax.ShapeDtypeStruct((b, HEADS, GDN_D, GDN_D), F32),
                   jax.ShapeDtypeStruct((b, hp, 128, 128), F32),
                   jax.ShapeDtypeStruct((b, HEADS * ML_DK), F32),
                   jax.ShapeDtypeStruct((b, 128), F32)],
        scratch_shapes=[pltpu.VMEM((HEADS, bb, 128), F32), pltpu.VMEM((HEADS, bb, 128), F32),
                        pltpu.VMEM((RING, bb, HEADS, GDN_D, GDN_D), F32),
                        pltpu.VMEM((RING, bb, hp, 128, 128), F32),
                        pltpu.SemaphoreType.DMA((RING, 2))],
        compiler_params=pltpu.CompilerParams(dimension_semantics=("arbitrary",),
                                             vmem_limit_bytes=VMEM_LIMIT),
        name="decode_step",
    )(proj, gates, cs, s0, c0, n0, m0, conv_w, plane, gng, gnm)


def _lane_vec(pairs):
    v = jnp.zeros((128,), F32)
    for off, val in pairs:
        v = v.at[off:off + HEADS].set(val.astype(F32))
    return v


def _prep_params(norm_pre_mix, w_in, conv_w, a_log, dt_bias, gdn_norm_g, b_igate, b_fgate, mlstm_norm_g,
                 w_out, norm_post_mix, norm_pre_mlp, w_up, w_down, norm_post_mlp):
    alog_v = _lane_vec([(LANE_GA, a_log[0])])
    dtb_v = _lane_vec([(LANE_GA, dt_bias[0])])
    bias_v = _lane_vec([(LANE_MI, b_igate[0]), (LANE_MF, b_fgate[0])])
    return dict(
        w_in=jnp.swapaxes(w_in[0], 0, 1), wo=w_out[0].astype(BF16), wu=w_up[0].astype(BF16),
        wd=w_down[0].astype(BF16),
        g_pre=norm_pre_mix[0][None, :], g1=norm_post_mix[0][None, :], g2=norm_pre_mlp[0][None, :],
        g3=norm_post_mlp[0][None, :], cw=conv_w[0], gng=gdn_norm_g[0][None, :], gnm=mlstm_norm_g[0][None, :],
        plane=jnp.stack([alog_v, dtb_v, bias_v]),
        psub=jnp.stack([alog_v[:16], dtb_v[:16], bias_v[:16]], axis=1),
    )


def _prompt_path(x, conv0, s0, c0, n0, m0, prm, w_in_r, tb, tm):
    bsz, seq, d = x.shape
    hp = HEADS // 2
    x2d = x.reshape(bsz * seq, d)
    proj, gates, gates_t = _in_proj(x2d, prm["g_pre"], w_in_r, tm=tm, out_dtype=F32)
    proj = proj.reshape(bsz, seq, COL_GATE)
    gates = gates.reshape(bsz, seq, 128)
    gates_t = gates_t.reshape(bsz, seq // CHUNK, 16, CHUNK)
    cs8 = jnp.pad(conv0, ((0, 0), (8 - (CONV_W - 1), 0), (0, 0)))
    y, s_new, c_new, n_new, m_new = _layer_prompt(
        proj, gates, gates_t, cs8, s0, c0.reshape(bsz, hp, 128, 128), n0.reshape(bsz, hp, 1, 128),
        jnp.pad(m0, ((0, 0), (0, 128 - HEADS)))[:, None, :], x2d, prm["cw"], prm["plane"], prm["psub"],
        prm["gng"], prm["gnm"], prm["wo"], prm["g1"], prm["g2"], prm["wu"], prm["wd"], prm["g3"], tb=tb)
    y = y.reshape(bsz, seq, d)
    xp = jnp.concatenate([conv0, proj[:, seq - (CONV_W - 1):, COL_QKV:COL_QKV + CONV_CH].astype(F32)], axis=1)
    conv_new = xp[:, -(CONV_W - 1):]
    return y, (conv_new, s_new, c_new.reshape(bsz, HEADS, ML_DK, ML_DV), n_new.reshape(bsz, HEADS, ML_DK),
               m_new[:, 0, :HEADS])


def _sample_path(x, conv0, s0, c0, n0, m0, prm):
    dec, _, d = x.shape
    hp = HEADS // 2
    xs = x.reshape(dec, d)
    proj, gates, _, w_in_r = _in_proj_regroup(xs, prm["g_pre"], prm["w_in"])
    og, om, s_new, c_new, n_new, m_new = _decode(
        proj, gates, conv0, s0, c0.reshape(dec, hp, 128, 128), n0.reshape(dec, HEADS * ML_DK),
        jnp.pad(m0, ((0, 0), (0, 128 - HEADS))), prm["cw"], prm["plane"], prm["gng"], prm["gnm"], bb=16)
    y = _out_mlp(og, om, xs, prm["wo"], prm["g1"], prm["g2"], prm["wu"], prm["wd"], prm["g3"],
                 tm=dec).reshape(dec, 1, d)
    conv_new = jnp.concatenate([conv0[:, 1:, :], proj[:, None, COL_QKV:COL_QKV + CONV_CH]], axis=1)
    return y, (conv_new, s_new, c_new.reshape(dec, HEADS, ML_DK, ML_DV), n_new.reshape(dec, HEADS, ML_DK),
               m_new[:, :HEADS]), w_in_r


def kernel(x_prompt, x_sample, state_gdn_conv, state_gdn_S, state_mlstm_C, state_mlstm_n, state_mlstm_m,
           norm_pre_mix, w_in, conv_w, a_log, dt_bias, gdn_norm_g, b_igate, b_fgate, mlstm_norm_g, w_out,
           norm_post_mix, norm_pre_mlp, w_up, w_down, norm_post_mlp):
    bsz = x_prompt.shape[0]
    prm = _prep_params(norm_pre_mix, w_in, conv_w, a_log, dt_bias, gdn_norm_g, b_igate, b_fgate, mlstm_norm_g,
                       w_out, norm_post_mix, norm_pre_mlp, w_up, w_down, norm_post_mlp)
    y_s, s_st, w_in_r = _sample_path(x_sample, state_gdn_conv[0], state_gdn_S[0], state_mlstm_C[0],
                                     state_mlstm_n[0], state_mlstm_m[0], prm)
    y_p, p_st = _prompt_path(
        x_prompt, jnp.zeros((bsz, CONV_W - 1, CONV_CH), F32), jnp.zeros((bsz, HEADS, GDN_D, GDN_D), F32),
        jnp.zeros((bsz, HEADS, ML_DK, ML_DV), F32), jnp.zeros((bsz, HEADS, ML_DK), F32),
        jnp.zeros((bsz, HEADS), F32), prm, w_in_r, tb=256, tm=1024)
    return (y_p, y_s) + tuple(a[None] for a in p_st) + tuple(a[None] for a in s_st)
```
